```python
import jax, jax.numpy as jnp
from jax import lax
import numpy as np

D_MODEL = 1024
BATCH = 8
SEQ = 2048
DEPTH = 4

CHUNK = 64
N_MEM = 256
N_A_LAYERS = DEPTH // 2
N_B_LAYERS = DEPTH - N_A_LAYERS
MAIN_WIDTH = 3 * D_MODEL // 4
MEM_WIDTH = D_MODEL // 4
MIX_WIDTH = MAIN_WIDTH + MEM_WIDTH
HG_HEAD_DIM = 128
HG_HEADS = MAIN_WIDTH // HG_HEAD_DIM
FOX_HEAD_DIM = 64
FOX_HEADS = MAIN_WIDTH // FOX_HEAD_DIM
MEM_HEADS = 4
MEM_HEAD_DIM = MEM_WIDTH // MEM_HEADS
D_FF = 2816
Q_BLOCK = 128
EPS = 1e-6
A_IN_WIDTH = 4 * MAIN_WIDTH + MEM_WIDTH
B_IN_WIDTH = 2 * MAIN_WIDTH + MEM_WIDTH
KV_WIDTH = 2 * MAIN_WIDTH + FOX_HEADS

kernel_name = 'hybrid_hgrn2_fox_yoco_macaron'


def rms_norm(x, gain):
    x32 = x.astype(jnp.float32)
    y = x32 * lax.rsqrt(jnp.mean(x32 * x32, axis=-1, keepdims=True) + EPS)
    return (y * gain.astype(jnp.float32)).astype(x.dtype)


def swiglu(h, w_gate, w_up, w_down):
    return (jax.nn.silu(h @ w_gate) * (h @ w_up)) @ w_down


def split_heads(t, n_heads):
    b, s, _ = t.shape
    return t.reshape(b, s, n_heads, -1).transpose(0, 2, 1, 3)


def merge_heads(t):
    b, h, s, d = t.shape
    return t.transpose(0, 2, 1, 3).reshape(b, s, h * d)


def hgrn2_recurrence(q, k, v, log_f):
    b, h, s, dk = q.shape
    dv = v.shape[-1]
    n = s // CHUNK

    def to_chunks(t):
        return t.reshape(b, h, n, CHUNK, t.shape[-1]).transpose(2, 0, 1, 3, 4)

    qc, kc, vc = to_chunks(q), to_chunks(k), to_chunks(v)
    cum = jnp.cumsum(to_chunks(log_f), axis=-2)
    tri = jnp.tril(jnp.ones((CHUNK, CHUNK), dtype=bool))

    def step(state, inp):
        q_, k_, v_, c_ = inp
        diff = c_[:, :, :, None, :] - c_[:, :, None, :, :]
        decay = jnp.exp(jnp.where(tri[:, :, None], diff, -jnp.inf))
        scores = jnp.einsum('bhtd,bhsd,bhtsd->bhts', q_, k_, decay)
        o = (jnp.einsum('bhts,bhsv->bhtv', scores, v_)
             + jnp.einsum('bhtd,bhdv->bhtv', q_ * jnp.exp(c_), state))
        c_end = c_[:, :, -1, :]
        state = (jnp.exp(c_end)[..., None] * state
                 + jnp.einsum('bhsd,bhsv->bhdv', k_ * jnp.exp(c_end[:, :, None, :] - c_), v_))
        return state, o

    state0 = jnp.zeros((b, h, dk, dv), jnp.float32)
    _, o = lax.scan(step, state0, (qc, kc, vc, cum))
    return o.transpose(1, 2, 0, 3, 4).reshape(b, h, s, dv)


def forgetting_attention(q, k, v, cum_log_f):
    s = q.shape[2]
    scale = FOX_HEAD_DIM ** -0.5
    outs = []
    for blk in range(s // Q_BLOCK):
        start = blk * Q_BLOCK
        end = start + Q_BLOCK
        logits = jnp.einsum('bhqd,bhkd->bhqk', q[:, :, start:end], k[:, :, :end]).astype(jnp.float32) * scale
        logits = logits + cum_log_f[:, :, start:end, None] - cum_log_f[:, :, None, :end]
        causal = jnp.arange(end)[None, :] <= jnp.arange(start, end)[:, None]
        p = jax.nn.softmax(jnp.where(causal, logits, -jnp.inf), axis=-1)
        outs.append(jnp.einsum('bhqk,bhkd->bhqd', p.astype(v.dtype), v[:, :, :end]))
    return jnp.concatenate(outs, axis=2)


def memory_attention(qm_raw, mem_n, w_mem_kv, q_gain, k_gain):
    kv = mem_n @ w_mem_kv
    km = rms_norm(split_heads(kv[..., :MEM_WIDTH], MEM_HEADS), k_gain)
    vm = split_heads(kv[..., MEM_WIDTH:], MEM_HEADS)
    qm = rms_norm(split_heads(qm_raw, MEM_HEADS), q_gain)
    logits = jnp.einsum('bhqd,bhkd->bhqk', qm, km).astype(jnp.float32) * (MEM_HEAD_DIM ** -0.5)
    p = jax.nn.softmax(logits, axis=-1)
    return merge_heads(jnp.einsum('bhqk,bhkd->bhqd', p.astype(vm.dtype), vm))


def _fwd_setup_inputs(seed: int = 0) -> dict:
    key = jax.random.key(seed)
    ks = jax.random.split(key, 32)

    def w(k, shape, fan_in):
        return jax.random.normal(k, shape, jnp.float32) * (fan_in ** -0.5)

    def g(k, shape):
        return 1.0 + 0.02 * jax.random.normal(k, shape, jnp.float32)

    return {
        'x': jax.random.normal(ks[0], (BATCH, SEQ, D_MODEL), jnp.float32),
        'mem': jax.random.normal(ks[1], (BATCH, N_MEM, D_MODEL), jnp.float32),
        'ffn1_norm': g(ks[2], (DEPTH, D_MODEL)),
        'ffn1_w_gate': w(ks[3], (DEPTH, D_MODEL, D_FF), D_MODEL),
        'ffn1_w_up': w(ks[4], (DEPTH, D_MODEL, D_FF), D_MODEL),
        'ffn1_w_down': w(ks[5], (DEPTH, D_FF, D_MODEL), D_FF),
        'mix_norm': g(ks[6], (DEPTH, D_MODEL)),
        'mem_norm': g(ks[7], (DEPTH, D_MODEL)),
        'w_mem_kv': w(ks[8], (DEPTH, D_MODEL, 2 * MEM_WIDTH), D_MODEL),
        'mem_q_gain': g(ks[9], (DEPTH, MEM_HEAD_DIM)),
        'mem_k_gain': g(ks[10], (DEPTH, MEM_HEAD_DIM)),
        'w_in_a': w(ks[11], (N_A_LAYERS, D_MODEL, A_IN_WIDTH), D_MODEL),
        'hgrn_lb_logits': jax.random.normal(ks[12], (N_A_LAYERS, MAIN_WIDTH), jnp.float32),
        'hgrn_o_gain': g(ks[13], (N_A_LAYERS, HG_HEAD_DIM)),
        'w_in_b': w(ks[14], (N_B_LAYERS, D_MODEL, B_IN_WIDTH), D_MODEL),
        'fox_q_gain': g(ks[15], (N_B_LAYERS, FOX_HEAD_DIM)),
        'kv_norm': g(ks[16], (D_MODEL,)),
        'w_kv': w(ks[17], (D_MODEL, KV_WIDTH), D_MODEL),
        'fox_f_bias': 0.1 * jax.random.normal(ks[18], (FOX_HEADS,), jnp.float32),
        'fox_k_gain': g(ks[19], (FOX_HEAD_DIM,)),
        'w_out': w(ks[20], (DEPTH, MIX_WIDTH, D_MODEL), MIX_WIDTH),
        'ffn2_norm': g(ks[21], (DEPTH, D_MODEL)),
        'ffn2_w_gate': w(ks[22], (DEPTH, D_MODEL, D_FF), D_MODEL),
        'ffn2_w_up': w(ks[23], (DEPTH, D_MODEL, D_FF), D_MODEL),
        'ffn2_w_down': w(ks[24], (DEPTH, D_FF, D_MODEL), D_FF),
    }


def _fwd_reference(x, mem, ffn1_norm, ffn1_w_gate, ffn1_w_up, ffn1_w_down, mix_norm, mem_norm,
              w_mem_kv, mem_q_gain, mem_k_gain, w_in_a, hgrn_lb_logits, hgrn_o_gain,
              w_in_b, fox_q_gain, kv_norm, w_kv, fox_f_bias, fox_k_gain, w_out,
              ffn2_norm, ffn2_w_gate, ffn2_w_up, ffn2_w_down):
    lb = jnp.cumsum(jax.nn.softmax(hgrn_lb_logits.astype(jnp.float32), axis=0), axis=0)
    lb = lb - lb[0:1]
    k_sh = v_sh = cum_log_f = None
    for l in range(DEPTH):
        x = x + 0.5 * swiglu(rms_norm(x, ffn1_norm[l]), ffn1_w_gate[l], ffn1_w_up[l], ffn1_w_down[l])
        h = rms_norm(x, mix_norm[l])
        mem_n = rms_norm(mem, mem_norm[l])
        if l < N_A_LAYERS:
            proj = h @ w_in_a[l]
            q_raw = proj[..., :MAIN_WIDTH]
            f_raw = proj[..., MAIN_WIDTH:2 * MAIN_WIDTH]
            i_raw = proj[..., 2 * MAIN_WIDTH:3 * MAIN_WIDTH]
            g_raw = proj[..., 3 * MAIN_WIDTH:4 * MAIN_WIDTH]
            qm_raw = proj[..., 4 * MAIN_WIDTH:]
            f = lb[l] + (1.0 - lb[l]) * jax.nn.sigmoid(f_raw.astype(jnp.float32))
            q = jax.nn.silu(q_raw.astype(jnp.float32))
            o = hgrn2_recurrence(split_heads(q, HG_HEADS), split_heads(1.0 - f, HG_HEADS),
                                 split_heads(i_raw.astype(jnp.float32), HG_HEADS),
                                 split_heads(jnp.log(f), HG_HEADS))
            main = merge_heads(rms_norm(o, hgrn_o_gain[l])) * jax.nn.silu(g_raw.astype(jnp.float32))
        else:
            j = l - N_A_LAYERS
            proj = h @ w_in_b[j]
            q = rms_norm(split_heads(proj[..., :MAIN_WIDTH], FOX_HEADS), fox_q_gain[j])
            gate = proj[..., MAIN_WIDTH:2 * MAIN_WIDTH]
            qm_raw = proj[..., 2 * MAIN_WIDTH:]
            o = forgetting_attention(q, k_sh, v_sh, cum_log_f)
            main = merge_heads(o) * jax.nn.sigmoid(gate)
        mem_o = memory_attention(qm_raw, mem_n, w_mem_kv[l], mem_q_gain[l], mem_k_gain[l])
        mixed = jnp.concatenate([main.astype(x.dtype), mem_o.astype(x.dtype)], axis=-1)
        x = x + mixed @ w_out[l]
        x = x + 0.5 * swiglu(rms_norm(x, ffn2_norm[l]), ffn2_w_gate[l], ffn2_w_up[l], ffn2_w_down[l])
        if l == N_A_LAYERS - 1:
            kvf = rms_norm(x, kv_norm) @ w_kv
            k_sh = rms_norm(split_heads(kvf[..., :MAIN_WIDTH], FOX_HEADS), fox_k_gain)
            v_sh = split_heads(kvf[..., MAIN_WIDTH:2 * MAIN_WIDTH], FOX_HEADS)
            log_f = jax.nn.log_sigmoid(kvf[..., 2 * MAIN_WIDTH:].astype(jnp.float32) + fox_f_bias.astype(jnp.float32))
            cum_log_f = jnp.cumsum(log_f.transpose(0, 2, 1), axis=-1)
    return x


import jax as _jax
import jax.numpy as _jnp

TWIN_FORMAT = 'train_step'
FWD_PARAMS = ['x', 'mem', 'ffn1_norm', 'ffn1_w_gate', 'ffn1_w_up', 'ffn1_w_down', 'mix_norm', 'mem_norm', 'w_mem_kv', 'mem_q_gain', 'mem_k_gain', 'w_in_a', 'hgrn_lb_logits', 'hgrn_o_gain', 'w_in_b', 'fox_q_gain', 'kv_norm', 'w_kv', 'fox_f_bias', 'fox_k_gain', 'w_out', 'ffn2_norm', 'ffn2_w_gate', 'ffn2_w_up', 'ffn2_w_down']
TWIN_WEIGHTS = ['ffn1_norm', 'ffn1_w_gate', 'ffn1_w_up', 'ffn1_w_down', 'mix_norm', 'mem_norm', 'w_mem_kv', 'mem_q_gain', 'mem_k_gain', 'w_in_a', 'hgrn_lb_logits', 'hgrn_o_gain', 'w_in_b', 'fox_q_gain', 'kv_norm', 'w_kv', 'fox_f_bias', 'fox_k_gain', 'w_out', 'ffn2_norm', 'ffn2_w_gate', 'ffn2_w_up', 'ffn2_w_down']
TWIN_DIFF_INPUT = 'x'
TWIN_INPUTS = ['x', 'mem', 'ffn1_norm', 'ffn1_w_gate', 'ffn1_w_up', 'ffn1_w_down', 'mix_norm', 'mem_norm', 'w_mem_kv', 'mem_q_gain', 'mem_k_gain', 'w_in_a', 'hgrn_lb_logits', 'hgrn_o_gain', 'w_in_b', 'fox_q_gain', 'kv_norm', 'w_kv', 'fox_f_bias', 'fox_k_gain', 'w_out', 'ffn2_norm', 'ffn2_w_gate', 'ffn2_w_up', 'ffn2_w_down', 'loss_target', 'm_ffn1_norm', 'm_ffn1_w_gate', 'm_ffn1_w_up', 'm_ffn1_w_down', 'm_mix_norm', 'm_mem_norm', 'm_w_mem_kv', 'm_mem_q_gain', 'm_mem_k_gain', 'm_w_in_a', 'm_hgrn_lb_logits', 'm_hgrn_o_gain', 'm_w_in_b', 'm_fox_q_gain', 'm_kv_norm', 'm_w_kv', 'm_fox_f_bias', 'm_fox_k_gain', 'm_w_out', 'm_ffn2_norm', 'm_ffn2_w_gate', 'm_ffn2_w_up', 'm_ffn2_w_down', 'v_ffn1_norm', 'v_ffn1_w_gate', 'v_ffn1_w_up', 'v_ffn1_w_down', 'v_mix_norm', 'v_mem_norm', 'v_w_mem_kv', 'v_mem_q_gain', 'v_mem_k_gain', 'v_w_in_a', 'v_hgrn_lb_logits', 'v_hgrn_o_gain', 'v_w_in_b', 'v_fox_q_gain', 'v_kv_norm', 'v_w_kv', 'v_fox_f_bias', 'v_fox_k_gain', 'v_w_out', 'v_ffn2_norm', 'v_ffn2_w_gate', 'v_ffn2_w_up', 'v_ffn2_w_down']
TWIN_OUTPUTS = ['loss', 'grad_x', 'grad_ffn1_norm', 'grad_ffn1_w_gate', 'grad_ffn1_w_up', 'grad_ffn1_w_down', 'grad_mix_norm', 'grad_mem_norm', 'grad_w_mem_kv', 'grad_mem_q_gain', 'grad_mem_k_gain', 'grad_w_in_a', 'grad_hgrn_lb_logits', 'grad_hgrn_o_gain', 'grad_w_in_b', 'grad_fox_q_gain', 'grad_kv_norm', 'grad_w_kv', 'grad_fox_f_bias', 'grad_fox_k_gain', 'grad_w_out', 'grad_ffn2_norm', 'grad_ffn2_w_gate', 'grad_ffn2_w_up', 'grad_ffn2_w_down', 'delta_ffn1_norm', 'delta_ffn1_w_gate', 'delta_ffn1_w_up', 'delta_ffn1_w_down', 'delta_mix_norm', 'delta_mem_norm', 'delta_w_mem_kv', 'delta_mem_q_gain', 'delta_mem_k_gain', 'delta_w_in_a', 'delta_hgrn_lb_logits', 'delta_hgrn_o_gain', 'delta_w_in_b', 'delta_fox_q_gain', 'delta_kv_norm', 'delta_w_kv', 'delta_fox_f_bias', 'delta_fox_k_gain', 'delta_w_out', 'delta_ffn2_norm', 'delta_ffn2_w_gate', 'delta_ffn2_w_up', 'delta_ffn2_w_down', 'new_m_ffn1_norm', 'new_m_ffn1_w_gate', 'new_m_ffn1_w_up', 'new_m_ffn1_w_down', 'new_m_mix_norm', 'new_m_mem_norm', 'new_m_w_mem_kv', 'new_m_mem_q_gain', 'new_m_mem_k_gain', 'new_m_w_in_a', 'new_m_hgrn_lb_logits', 'new_m_hgrn_o_gain', 'new_m_w_in_b', 'new_m_fox_q_gain', 'new_m_kv_norm', 'new_m_w_kv', 'new_m_fox_f_bias', 'new_m_fox_k_gain', 'new_m_w_out', 'new_m_ffn2_norm', 'new_m_ffn2_w_gate', 'new_m_ffn2_w_up', 'new_m_ffn2_w_down', 'new_v_ffn1_norm', 'new_v_ffn1_w_gate', 'new_v_ffn1_w_up', 'new_v_ffn1_w_down', 'new_v_mix_norm', 'new_v_mem_norm', 'new_v_w_mem_kv', 'new_v_mem_q_gain', 'new_v_mem_k_gain', 'new_v_w_in_a', 'new_v_hgrn_lb_logits', 'new_v_hgrn_o_gain', 'new_v_w_in_b', 'new_v_fox_q_gain', 'new_v_kv_norm', 'new_v_w_kv', 'new_v_fox_f_bias', 'new_v_fox_k_gain', 'new_v_w_out', 'new_v_ffn2_norm', 'new_v_ffn2_w_gate', 'new_v_ffn2_w_up', 'new_v_ffn2_w_down']
TWIN_LEAF_KINDS = {'loss': 'loss', 'grad_x': 'grad_x', 'grad_ffn1_norm': 'grad_w', 'grad_ffn1_w_gate': 'grad_w', 'grad_ffn1_w_up': 'grad_w', 'grad_ffn1_w_down': 'grad_w', 'grad_mix_norm': 'grad_w', 'grad_mem_norm': 'grad_w', 'grad_w_mem_kv': 'grad_w', 'grad_mem_q_gain': 'grad_w', 'grad_mem_k_gain': 'grad_w', 'grad_w_in_a': 'grad_w', 'grad_hgrn_lb_logits': 'grad_w', 'grad_hgrn_o_gain': 'grad_w', 'grad_w_in_b': 'grad_w', 'grad_fox_q_gain': 'grad_w', 'grad_kv_norm': 'grad_w', 'grad_w_kv': 'grad_w', 'grad_fox_f_bias': 'grad_w', 'grad_fox_k_gain': 'grad_w', 'grad_w_out': 'grad_w', 'grad_ffn2_norm': 'grad_w', 'grad_ffn2_w_gate': 'grad_w', 'grad_ffn2_w_up': 'grad_w', 'grad_ffn2_w_down': 'grad_w', 'delta_ffn1_norm': 'delta_w', 'delta_ffn1_w_gate': 'delta_w', 'delta_ffn1_w_up': 'delta_w', 'delta_ffn1_w_down': 'delta_w', 'delta_mix_norm': 'delta_w', 'delta_mem_norm': 'delta_w', 'delta_w_mem_kv': 'delta_w', 'delta_mem_q_gain': 'delta_w', 'delta_mem_k_gain': 'delta_w', 'delta_w_in_a': 'delta_w', 'delta_hgrn_lb_logits': 'delta_w', 'delta_hgrn_o_gain': 'delta_w', 'delta_w_in_b': 'delta_w', 'delta_fox_q_gain': 'delta_w', 'delta_kv_norm': 'delta_w', 'delta_w_kv': 'delta_w', 'delta_fox_f_bias': 'delta_w', 'delta_fox_k_gain': 'delta_w', 'delta_w_out': 'delta_w', 'delta_ffn2_norm': 'delta_w', 'delta_ffn2_w_gate': 'delta_w', 'delta_ffn2_w_up': 'delta_w', 'delta_ffn2_w_down': 'delta_w', 'new_m_ffn1_norm': 'new_m', 'new_m_ffn1_w_gate': 'new_m', 'new_m_ffn1_w_up': 'new_m', 'new_m_ffn1_w_down': 'new_m', 'new_m_mix_norm': 'new_m', 'new_m_mem_norm': 'new_m', 'new_m_w_mem_kv': 'new_m', 'new_m_mem_q_gain': 'new_m', 'new_m_mem_k_gain': 'new_m', 'new_m_w_in_a': 'new_m', 'new_m_hgrn_lb_logits': 'new_m', 'new_m_hgrn_o_gain': 'new_m', 'new_m_w_in_b': 'new_m', 'new_m_fox_q_gain': 'new_m', 'new_m_kv_norm': 'new_m', 'new_m_w_kv': 'new_m', 'new_m_fox_f_bias': 'new_m', 'new_m_fox_k_gain': 'new_m', 'new_m_w_out': 'new_m', 'new_m_ffn2_norm': 'new_m', 'new_m_ffn2_w_gate': 'new_m', 'new_m_ffn2_w_up': 'new_m', 'new_m_ffn2_w_down': 'new_m', 'new_v_ffn1_norm': 'new_v', 'new_v_ffn1_w_gate': 'new_v', 'new_v_ffn1_w_up': 'new_v', 'new_v_ffn1_w_down': 'new_v', 'new_v_mix_norm': 'new_v', 'new_v_mem_norm': 'new_v', 'new_v_w_mem_kv': 'new_v', 'new_v_mem_q_gain': 'new_v', 'new_v_mem_k_gain': 'new_v', 'new_v_w_in_a': 'new_v', 'new_v_hgrn_lb_logits': 'new_v', 'new_v_hgrn_o_gain': 'new_v', 'new_v_w_in_b': 'new_v', 'new_v_fox_q_gain': 'new_v', 'new_v_kv_norm': 'new_v', 'new_v_w_kv': 'new_v', 'new_v_fox_f_bias': 'new_v', 'new_v_fox_k_gain': 'new_v', 'new_v_w_out': 'new_v', 'new_v_ffn2_norm': 'new_v', 'new_v_ffn2_w_gate': 'new_v', 'new_v_ffn2_w_up': 'new_v', 'new_v_ffn2_w_down': 'new_v'}


def _forward(args):
    return _fwd_reference(*[args[k] for k in FWD_PARAMS])


def _output_shape():
    out = _jax.eval_shape(lambda: _forward(_fwd_setup_inputs(0)))
    return out.shape, out.dtype

N_MICROBATCH = 1
ADAM_LR = 0.001
ADAM_B1 = 0.9
ADAM_B2 = 0.999
ADAM_EPS = 1e-08
ADAM_WD = 0.01
ADAM_STEP = 10
PER_EXAMPLE_BATCH_AXIS = {'x': 0, 'mem': 0, 'loss_target': 0}
SHARED_INPUTS = []
_WEIGHT_DTYPES = {'ffn1_norm': _jnp.float32, 'ffn1_w_gate': _jnp.float32, 'ffn1_w_up': _jnp.float32, 'ffn1_w_down': _jnp.float32, 'mix_norm': _jnp.float32, 'mem_norm': _jnp.float32, 'w_mem_kv': _jnp.float32, 'mem_q_gain': _jnp.float32, 'mem_k_gain': _jnp.float32, 'w_in_a': _jnp.float32, 'hgrn_lb_logits': _jnp.float32, 'hgrn_o_gain': _jnp.float32, 'w_in_b': _jnp.float32, 'fox_q_gain': _jnp.float32, 'kv_norm': _jnp.float32, 'w_kv': _jnp.float32, 'fox_f_bias': _jnp.float32, 'fox_k_gain': _jnp.float32, 'w_out': _jnp.float32, 'ffn2_norm': _jnp.float32, 'ffn2_w_gate': _jnp.float32, 'ffn2_w_up': _jnp.float32, 'ffn2_w_down': _jnp.float32}
MOMENT_SCALE = {'ffn1_norm': 3.072565e+00, 'ffn1_w_gate': 7.686406e-02, 'ffn1_w_up': 8.027084e-02, 'ffn1_w_down': 1.320436e-01, 'mix_norm': 3.535932e+00, 'mem_norm': 7.253184e-02, 'w_mem_kv': 7.818314e-02, 'mem_q_gain': 6.068612e-01, 'mem_k_gain': 6.052018e-01, 'w_in_a': 2.050354e-01, 'hgrn_lb_logits': 1.970490e-02, 'hgrn_o_gain': 3.446105e+01, 'w_in_b': 5.419978e-02, 'fox_q_gain': 2.609983e+00, 'kv_norm': 3.345729e+00, 'w_kv': 2.206750e-01, 'fox_f_bias': 4.667672e+01, 'fox_k_gain': 5.269037e+00, 'w_out': 2.119618e-01, 'ffn2_norm': 3.073402e+00, 'ffn2_w_gate': 6.244247e-02, 'ffn2_w_up': 6.856869e-02, 'ffn2_w_down': 1.123751e-01}


def _to_microbatches(a, axis):
    t = _jnp.moveaxis(a, axis, 0)
    t = t.reshape((N_MICROBATCH, t.shape[0] // N_MICROBATCH) + t.shape[1:])
    return _jnp.moveaxis(t, 1, axis + 1)


def setup_inputs(seed: int = 0) -> dict:
    inp = _fwd_setup_inputs(seed)
    key = _jax.random.fold_in(_jax.random.key(seed), 7919)
    shape, _ = _output_shape()
    out = dict(inp)
    out["loss_target"] = _jax.random.normal(_jax.random.fold_in(key, 0), shape, _jnp.float32)
    for i, name in enumerate(TWIN_WEIGHTS):
        w = inp[name].astype(_jnp.float32)
        if MOMENT_SCALE is None:
            s = _jnp.sqrt(_jnp.mean(_jnp.square(w)) + 1e-30)
        else:
            s = MOMENT_SCALE[name]
        km, kv = _jax.random.split(_jax.random.fold_in(key, i + 1))
        out[name] = w
        out["m_" + name] = s * _jax.random.normal(km, w.shape, _jnp.float32)
        out["v_" + name] = (s * s) * _jax.random.uniform(kv, w.shape, _jnp.float32, 0.5, 1.5)
    if N_MICROBATCH > 1:
        for name, axis in PER_EXAMPLE_BATCH_AXIS.items():
            out[name] = _to_microbatches(out[name], axis)
    return {'x': out['x'], 'mem': out['mem'], 'ffn1_norm': out['ffn1_norm'], 'ffn1_w_gate': out['ffn1_w_gate'], 'ffn1_w_up': out['ffn1_w_up'], 'ffn1_w_down': out['ffn1_w_down'], 'mix_norm': out['mix_norm'], 'mem_norm': out['mem_norm'], 'w_mem_kv': out['w_mem_kv'], 'mem_q_gain': out['mem_q_gain'], 'mem_k_gain': out['mem_k_gain'], 'w_in_a': out['w_in_a'], 'hgrn_lb_logits': out['hgrn_lb_logits'], 'hgrn_o_gain': out['hgrn_o_gain'], 'w_in_b': out['w_in_b'], 'fox_q_gain': out['fox_q_gain'], 'kv_norm': out['kv_norm'], 'w_kv': out['w_kv'], 'fox_f_bias': out['fox_f_bias'], 'fox_k_gain': out['fox_k_gain'], 'w_out': out['w_out'], 'ffn2_norm': out['ffn2_norm'], 'ffn2_w_gate': out['ffn2_w_gate'], 'ffn2_w_up': out['ffn2_w_up'], 'ffn2_w_down': out['ffn2_w_down'], 'loss_target': out['loss_target'], 'm_ffn1_norm': out['m_ffn1_norm'], 'm_ffn1_w_gate': out['m_ffn1_w_gate'], 'm_ffn1_w_up': out['m_ffn1_w_up'], 'm_ffn1_w_down': out['m_ffn1_w_down'], 'm_mix_norm': out['m_mix_norm'], 'm_mem_norm': out['m_mem_norm'], 'm_w_mem_kv': out['m_w_mem_kv'], 'm_mem_q_gain': out['m_mem_q_gain'], 'm_mem_k_gain': out['m_mem_k_gain'], 'm_w_in_a': out['m_w_in_a'], 'm_hgrn_lb_logits': out['m_hgrn_lb_logits'], 'm_hgrn_o_gain': out['m_hgrn_o_gain'], 'm_w_in_b': out['m_w_in_b'], 'm_fox_q_gain': out['m_fox_q_gain'], 'm_kv_norm': out['m_kv_norm'], 'm_w_kv': out['m_w_kv'], 'm_fox_f_bias': out['m_fox_f_bias'], 'm_fox_k_gain': out['m_fox_k_gain'], 'm_w_out': out['m_w_out'], 'm_ffn2_norm': out['m_ffn2_norm'], 'm_ffn2_w_gate': out['m_ffn2_w_gate'], 'm_ffn2_w_up': out['m_ffn2_w_up'], 'm_ffn2_w_down': out['m_ffn2_w_down'], 'v_ffn1_norm': out['v_ffn1_norm'], 'v_ffn1_w_gate': out['v_ffn1_w_gate'], 'v_ffn1_w_up': out['v_ffn1_w_up'], 'v_ffn1_w_down': out['v_ffn1_w_down'], 'v_mix_norm': out['v_mix_norm'], 'v_mem_norm': out['v_mem_norm'], 'v_w_mem_kv': out['v_w_mem_kv'], 'v_mem_q_gain': out['v_mem_q_gain'], 'v_mem_k_gain': out['v_mem_k_gain'], 'v_w_in_a': out['v_w_in_a'], 'v_hgrn_lb_logits': out['v_hgrn_lb_logits'], 'v_hgrn_o_gain': out['v_hgrn_o_gain'], 'v_w_in_b': out['v_w_in_b'], 'v_fox_q_gain': out['v_fox_q_gain'], 'v_kv_norm': out['v_kv_norm'], 'v_w_kv': out['v_w_kv'], 'v_fox_f_bias': out['v_fox_f_bias'], 'v_fox_k_gain': out['v_fox_k_gain'], 'v_w_out': out['v_w_out'], 'v_ffn2_norm': out['v_ffn2_norm'], 'v_ffn2_w_gate': out['v_ffn2_w_gate'], 'v_ffn2_w_up': out['v_ffn2_w_up'], 'v_ffn2_w_down': out['v_ffn2_w_down']}


def _loss(weights, diff, rest, loss_target):
    with _jax.named_scope("forward"):
        args = {**rest, TWIN_DIFF_INPUT: diff, **{k: w.astype(_WEIGHT_DTYPES[k]) for k, w in weights.items()}}
        y = _forward(args)
    with _jax.named_scope("loss_head"):
        err = _jnp.square(y.astype(_jnp.float32) - loss_target)
        return 0.5 * _jnp.sum(_jnp.mean(err, axis=-1)) if err.ndim else 0.5 * err


def _adamw(w, g, m, v):
    m = ADAM_B1 * m + (1.0 - ADAM_B1) * g
    v = ADAM_B2 * v + (1.0 - ADAM_B2) * _jnp.square(g)
    m_hat = m / (1.0 - ADAM_B1 ** ADAM_STEP)
    v_hat = v / (1.0 - ADAM_B2 ** ADAM_STEP)
    delta = -ADAM_LR * (m_hat / (_jnp.sqrt(v_hat) + ADAM_EPS) + ADAM_WD * w)
    return delta, m, v


def reference(x, mem, ffn1_norm, ffn1_w_gate, ffn1_w_up, ffn1_w_down, mix_norm, mem_norm, w_mem_kv, mem_q_gain, mem_k_gain, w_in_a, hgrn_lb_logits, hgrn_o_gain, w_in_b, fox_q_gain, kv_norm, w_kv, fox_f_bias, fox_k_gain, w_out, ffn2_norm, ffn2_w_gate, ffn2_w_up, ffn2_w_down, loss_target, m_ffn1_norm, m_ffn1_w_gate, m_ffn1_w_up, m_ffn1_w_down, m_mix_norm, m_mem_norm, m_w_mem_kv, m_mem_q_gain, m_mem_k_gain, m_w_in_a, m_hgrn_lb_logits, m_hgrn_o_gain, m_w_in_b, m_fox_q_gain, m_kv_norm, m_w_kv, m_fox_f_bias, m_fox_k_gain, m_w_out, m_ffn2_norm, m_ffn2_w_gate, m_ffn2_w_up, m_ffn2_w_down, v_ffn1_norm, v_ffn1_w_gate, v_ffn1_w_up, v_ffn1_w_down, v_mix_norm, v_mem_norm, v_w_mem_kv, v_mem_q_gain, v_mem_k_gain, v_w_in_a, v_hgrn_lb_logits, v_hgrn_o_gain, v_w_in_b, v_fox_q_gain, v_kv_norm, v_w_kv, v_fox_f_bias, v_fox_k_gain, v_w_out, v_ffn2_norm, v_ffn2_w_gate, v_ffn2_w_up, v_ffn2_w_down):
    given = dict(x=x, mem=mem, ffn1_norm=ffn1_norm, ffn1_w_gate=ffn1_w_gate, ffn1_w_up=ffn1_w_up, ffn1_w_down=ffn1_w_down, mix_norm=mix_norm, mem_norm=mem_norm, w_mem_kv=w_mem_kv, mem_q_gain=mem_q_gain, mem_k_gain=mem_k_gain, w_in_a=w_in_a, hgrn_lb_logits=hgrn_lb_logits, hgrn_o_gain=hgrn_o_gain, w_in_b=w_in_b, fox_q_gain=fox_q_gain, kv_norm=kv_norm, w_kv=w_kv, fox_f_bias=fox_f_bias, fox_k_gain=fox_k_gain, w_out=w_out, ffn2_norm=ffn2_norm, ffn2_w_gate=ffn2_w_gate, ffn2_w_up=ffn2_w_up, ffn2_w_down=ffn2_w_down, loss_target=loss_target, m_ffn1_norm=m_ffn1_norm, m_ffn1_w_gate=m_ffn1_w_gate, m_ffn1_w_up=m_ffn1_w_up, m_ffn1_w_down=m_ffn1_w_down, m_mix_norm=m_mix_norm, m_mem_norm=m_mem_norm, m_w_mem_kv=m_w_mem_kv, m_mem_q_gain=m_mem_q_gain, m_mem_k_gain=m_mem_k_gain, m_w_in_a=m_w_in_a, m_hgrn_lb_logits=m_hgrn_lb_logits, m_hgrn_o_gain=m_hgrn_o_gain, m_w_in_b=m_w_in_b, m_fox_q_gain=m_fox_q_gain, m_kv_norm=m_kv_norm, m_w_kv=m_w_kv, m_fox_f_bias=m_fox_f_bias, m_fox_k_gain=m_fox_k_gain, m_w_out=m_w_out, m_ffn2_norm=m_ffn2_norm, m_ffn2_w_gate=m_ffn2_w_gate, m_ffn2_w_up=m_ffn2_w_up, m_ffn2_w_down=m_ffn2_w_down, v_ffn1_norm=v_ffn1_norm, v_ffn1_w_gate=v_ffn1_w_gate, v_ffn1_w_up=v_ffn1_w_up, v_ffn1_w_down=v_ffn1_w_down, v_mix_norm=v_mix_norm, v_mem_norm=v_mem_norm, v_w_mem_kv=v_w_mem_kv, v_mem_q_gain=v_mem_q_gain, v_mem_k_gain=v_mem_k_gain, v_w_in_a=v_w_in_a, v_hgrn_lb_logits=v_hgrn_lb_logits, v_hgrn_o_gain=v_hgrn_o_gain, v_w_in_b=v_w_in_b, v_fox_q_gain=v_fox_q_gain, v_kv_norm=v_kv_norm, v_w_kv=v_w_kv, v_fox_f_bias=v_fox_f_bias, v_fox_k_gain=v_fox_k_gain, v_w_out=v_w_out, v_ffn2_norm=v_ffn2_norm, v_ffn2_w_gate=v_ffn2_w_gate, v_ffn2_w_up=v_ffn2_w_up, v_ffn2_w_down=v_ffn2_w_down)
    weights = {n: given[n] for n in TWIN_WEIGHTS}
    shared = {n: given[n] for n in SHARED_INPUTS}
    per_example = {n: given[n] for n in ['x', 'mem']}
    grad_fn = _jax.value_and_grad(_loss, argnums=(0, 1))

    def one_microbatch(ex, loss_target):
        ex = dict(ex)
        diff = ex.pop(TWIN_DIFF_INPUT)
        return grad_fn(weights, diff, {**shared, **ex}, loss_target)

    if N_MICROBATCH == 1:
        loss, (grad_w, grad_x) = one_microbatch(per_example, given["loss_target"])
    else:
        def body(carry, xs):
            loss_sum, grad_sum = carry
            l_k, (gw_k, gx_k) = one_microbatch(xs[0], xs[1])
            with _jax.named_scope("update"):
                return (loss_sum + l_k, _jax.tree.map(_jnp.add, grad_sum, gw_k)), gx_k

        init = (_jnp.zeros((), _jnp.float32), _jax.tree.map(_jnp.zeros_like, weights))
        (loss, grad_w), grad_x = _jax.lax.scan(body, init, (per_example, given["loss_target"]))
    with _jax.named_scope("update"):
        delta_w, new_m, new_v = {}, {}, {}
        for n in TWIN_WEIGHTS:
            delta_w[n], new_m[n], new_v[n] = _adamw(weights[n], grad_w[n], given["m_" + n], given["v_" + n])
    return (loss, grad_x, *[grad_w[n] for n in TWIN_WEIGHTS], *[delta_w[n] for n in TWIN_WEIGHTS],
            *[new_m[n] for n in TWIN_WEIGHTS], *[new_v[n] for n in TWIN_WEIGHTS])
```

```python
import functools

import jax
import jax.numpy as jnp
from jax import lax
from jax.experimental import pallas as pl
from jax.experimental.pallas import tpu as pltpu

F32, BF16 = jnp.float32, jnp.bfloat16
HI = lax.Precision.HIGHEST
EPS = 1e-6
MESH = pl.DeviceIdType.MESH
ANY = pl.BlockSpec(memory_space=pl.ANY)

VMEM_LIMIT_BYTES = 56 << 20
N_CHIPS = 4
N_DEV = 8
LANES = 128
HEAD64 = 64
CHUNK = 64
SUB = 16
TQ = 256
TOK = 256

ADAM_LR, ADAM_B1, ADAM_B2, ADAM_EPS, ADAM_WD, ADAM_STEP = 0.001, 0.9, 0.999, 1e-08, 0.01, 10


def _cparams(sem=None):
    return pltpu.CompilerParams(dimension_semantics=sem, vmem_limit_bytes=VMEM_LIMIT_BYTES)


def _mm(a, b, dims, prec=None):
    return lax.dot_general(a, b, (dims, ((), ())), preferred_element_type=F32, precision=prec)


def dot_nn(a, b, prec=None):
    return _mm(a, b, ((1,), (0,)), prec)


def dot_nt(a, b, prec=None):
    return _mm(a, b, ((1,), (1,)), prec)


def dot_tn(a, b, prec=None):
    return _mm(a, b, ((0,), (0,)), prec)


def bf(v):
    return v.astype(BF16)


def _sigmoid(z):
    return jax.nn.sigmoid(z)


def _dsilu(z, s):
    return s * (1.0 + z * (1.0 - s))


def _rms(x):
    r = lax.rsqrt(jnp.mean(x * x, axis=-1, keepdims=True) + EPS)
    return x * r, r


def _rms_bwd(dxn, u, r, g):
    du = dxn * g
    dx = r * (du - u * jnp.mean(du * u, axis=-1, keepdims=True))
    return dx, jnp.sum(dxn * u, axis=0, keepdims=True)


def _lane_mask0(shape):
    return lax.broadcasted_iota(jnp.int32, shape, len(shape) - 1) < HEAD64


def _rms64(x, m0):
    sq = x * x
    s0 = jnp.sum(jnp.where(m0, sq, 0.0), axis=-1, keepdims=True)
    s1 = jnp.sum(jnp.where(m0, 0.0, sq), axis=-1, keepdims=True)
    r = lax.rsqrt(jnp.where(m0, s0, s1) * (1.0 / HEAD64) + EPS)
    return x * r, r


def _rms64_bwd(dxn, u, r, g, m0):
    du = dxn * g
    t = du * u
    t0 = jnp.sum(jnp.where(m0, t, 0.0), axis=-1, keepdims=True)
    t1 = jnp.sum(jnp.where(m0, 0.0, t), axis=-1, keepdims=True)
    dx = r * (du - u * (jnp.where(m0, t0, t1) * (1.0 / HEAD64)))
    return dx, jnp.sum(dxn * u, axis=0, keepdims=True)


def _tok(s):
    return TOK if s % TOK == 0 else s


def _const(shape):
    return pl.BlockSpec(shape, lambda *_: (0,) * len(shape))


def ffn_fwd(x, gain3, wg, wu, wd, l):
    s, d = x.shape
    nc, _, _, fc = wg.shape
    tm = _tok(s)

    def body(x_ref, g_ref, wg_ref, wu_ref, wd_ref, xo_ref, a_ref, b_ref):
        xv = x_ref[...]
        u, _ = _rms(xv)
        xn = bf(u * g_ref[...])
        y = jnp.zeros((tm, d), F32)
        for c in range(nc):
            a = dot_nn(xn, wg_ref[c])
            b = dot_nn(xn, wu_ref[c])
            a_ref[c] = bf(a)
            b_ref[c] = bf(b)
            y = y + dot_nn(bf(a * _sigmoid(a) * b), wd_ref[c])
        xo_ref[...] = xv + 0.5 * y

    wspec = pl.BlockSpec((nc, None, d, fc), lambda i: (0, l, 0, 0), pipeline_mode=pl.Buffered(1))
    wdspec = pl.BlockSpec((nc, None, fc, d), lambda i: (0, l, 0, 0), pipeline_mode=pl.Buffered(1))
    row = pl.BlockSpec((tm, d), lambda i: (i, 0))
    act = pl.BlockSpec((nc, tm, fc), lambda i: (0, i, 0))
    return pl.pallas_call(
        body,
        name="ffn_fwd",
        grid=(s // tm,),
        in_specs=[row, pl.BlockSpec((None, 1, d), lambda i: (l, 0, 0)), wspec, wspec, wdspec],
        out_specs=[row, act, act],
        out_shape=[
            jax.ShapeDtypeStruct((s, d), F32),
            jax.ShapeDtypeStruct((nc, s, fc), BF16),
            jax.ShapeDtypeStruct((nc, s, fc), BF16),
        ],
        compiler_params=_cparams(("arbitrary",)),
    )(x, gain3, wg, wu, wd)


def ffn_bwd(x, gain3, dout, a, b, wg, wu, wd, l):
    s, d = x.shape
    nc, _, _, fc = wg.shape
    tm = _tok(s)

    def body(x_ref, g_ref, do_ref, a_ref, b_ref, wg_ref, wu_ref, wd_ref, dx_ref, da_ref, db_ref, hm_ref, xn_ref, dy_ref, dg_ref):
        xv = x_ref[...]
        g = g_ref[...]
        u, r = _rms(xv)
        xn_ref[...] = bf(u * g)
        dout = do_ref[...]
        dy = bf(0.5 * dout)
        dy_ref[...] = dy
        dxn = jnp.zeros((tm, d), F32)
        for c in range(nc):
            av = a_ref[c].astype(F32)
            bv = b_ref[c].astype(F32)
            sg = _sigmoid(av)
            sl = av * sg
            dh = dot_nt(dy, wd_ref[c])
            da = bf(dh * bv * _dsilu(av, sg))
            db = bf(dh * sl)
            da_ref[c] = da
            db_ref[c] = db
            hm_ref[c] = bf(sl * bv)
            dxn = dxn + dot_nt(da, wg_ref[c]) + dot_nt(db, wu_ref[c])
        dx, dg = _rms_bwd(dxn, u, r, g)
        dx_ref[...] = dout + dx

        @pl.when(pl.program_id(0) == 0)
        def _():
            dg_ref[...] = jnp.zeros_like(dg_ref)

        dg_ref[...] += dg

    wspec = pl.BlockSpec((nc, None, d, fc), lambda i: (0, l, 0, 0), pipeline_mode=pl.Buffered(1))
    wdspec = pl.BlockSpec((nc, None, fc, d), lambda i: (0, l, 0, 0), pipeline_mode=pl.Buffered(1))
    row = pl.BlockSpec((tm, d), lambda i: (i, 0))
    act = pl.BlockSpec((nc, tm, fc), lambda i: (0, i, 0))
    act_shape = jax.ShapeDtypeStruct((nc, s, fc), BF16)
    return pl.pallas_call(
        body,
        name="ffn_bwd",
        grid=(s // tm,),
        in_specs=[row, pl.BlockSpec((None, 1, d), lambda i: (l, 0, 0)), row, act, act, wspec, wspec, wdspec],
        out_specs=[row, act, act, act, row, row, _const((1, d))],
        out_shape=[
            jax.ShapeDtypeStruct((s, d), F32),
            act_shape,
            act_shape,
            act_shape,
            jax.ShapeDtypeStruct((s, d), BF16),
            jax.ShapeDtypeStruct((s, d), BF16),
            jax.ShapeDtypeStruct((1, d), F32),
        ],
        compiler_params=_cparams(("arbitrary",)),
    )(x, gain3, dout, a, b, wg, wu, wd)


def wgrad(a, b, tn=None, buf=None, l=None):
    ca = a.shape[0] if a.ndim == 3 else 1
    cb = b.shape[0] if b.ndim == 3 else 1
    nc = max(ca, cb)
    s, m = a.shape[-2:]
    n = b.shape[-1]
    tn = n if tn is None else tn
    assert n % tn == 0

    def body(*refs):
        a_ref, b_ref, o_ref = refs[-3:] if buf is None else (refs[0], refs[1], refs[3])
        o_ref[...] = dot_tn(a_ref[...], b_ref[...])

    a_spec = pl.BlockSpec((None, s, m), lambda c, j: (c, 0, 0)) if a.ndim == 3 else pl.BlockSpec((s, m), lambda c, j: (0, 0))
    b_spec = pl.BlockSpec((None, s, tn), lambda c, j: (c, 0, j)) if b.ndim == 3 else pl.BlockSpec((s, tn), lambda c, j: (0, j))
    if buf is None:
        assert nc == 1
        return pl.pallas_call(
            body,
            name="wgrad",
            grid=(1, n // tn),
            in_specs=[a_spec, b_spec],
            out_specs=pl.BlockSpec((m, tn), lambda c, j: (0, j)),
            out_shape=jax.ShapeDtypeStruct((m, n), F32),
            compiler_params=_cparams(("arbitrary", "arbitrary")),
        )(a, b)
    lh = buf.shape[2]
    hi, lo = l // lh, l % lh
    o_spec = pl.BlockSpec((None, None, None, m, tn), lambda c, j: (hi, c, lo, 0, j))
    return pl.pallas_call(
        body,
        name="wgrad_buf",
        grid=(nc, n // tn),
        in_specs=[a_spec, b_spec, ANY],
        out_specs=o_spec,
        out_shape=jax.ShapeDtypeStruct(buf.shape, F32),
        input_output_aliases={2: 0},
        compiler_params=_cparams(("arbitrary", "arbitrary")),
    )(a, b, buf)


def proj_fwd(x, gain3, l, w, wl):
    s, d = x.shape
    n = w.shape[-1]
    tm = _tok(s)

    def body(x_ref, g_ref, w_ref, o_ref):
        u, _ = _rms(x_ref[...])
        o_ref[...] = dot_nn(bf(u * g_ref[...]), w_ref[...])

    return pl.pallas_call(
        body,
        name="proj_fwd",
        grid=(s // tm,),
        in_specs=[
            pl.BlockSpec((tm, d), lambda i: (i, 0)),
            pl.BlockSpec((None, 1, d), lambda i: (l, 0, 0)),
            pl.BlockSpec((None, d, n), lambda i: (wl, 0, 0)),
        ],
        out_specs=pl.BlockSpec((tm, n), lambda i: (i, 0)),
        out_shape=jax.ShapeDtypeStruct((s, n), F32),
        compiler_params=_cparams(("arbitrary",)),
    )(x, gain3, w)


def proj_bwd(x, gain3, l, parts, w, wl, dx_in):
    s, d = x.shape
    n = w.shape[-1]
    widths = [p.shape[1] for p in parts]
    assert sum(widths) == n
    tm = _tok(s)
    npart = len(parts)

    def body(*refs):
        x_ref, g_ref, w_ref, dxin_ref = refs[:4]
        p_refs = refs[4 : 4 + npart]
        dx_ref, dg_ref, xn_ref, dpb_ref = refs[4 + npart :]
        g = g_ref[...]
        u, r = _rms(x_ref[...])
        xn_ref[...] = bf(u * g)
        dxn = jnp.zeros((tm, d), F32)
        off = 0
        for p_ref, wd_ in zip(p_refs, widths):
            dp = bf(p_ref[...])
            dpb_ref[:, off : off + wd_] = dp
            dxn = dxn + dot_nt(dp, w_ref[:, off : off + wd_])
            off += wd_
        dx, dg = _rms_bwd(dxn, u, r, g)
        dx_ref[...] = dxin_ref[...] + dx

        @pl.when(pl.program_id(0) == 0)
        def _():
            dg_ref[...] = jnp.zeros_like(dg_ref)

        dg_ref[...] += dg

    row = pl.BlockSpec((tm, d), lambda i: (i, 0))
    return pl.pallas_call(
        body,
        name="proj_bwd",
        grid=(s // tm,),
        in_specs=[row, pl.BlockSpec((None, 1, d), lambda i: (l, 0, 0)), pl.BlockSpec((None, d, n), lambda i: (wl, 0, 0)), row]
        + [pl.BlockSpec((tm, wd_), lambda i: (i, 0)) for wd_ in widths],
        out_specs=[row, _const((1, d)), row, pl.BlockSpec((tm, n), lambda i: (i, 0))],
        out_shape=[
            jax.ShapeDtypeStruct((s, d), F32),
            jax.ShapeDtypeStruct((1, d), F32),
            jax.ShapeDtypeStruct((s, d), BF16),
            jax.ShapeDtypeStruct((s, n), BF16),
        ],
        compiler_params=_cparams(("arbitrary",)),
    )(x, gain3, w, dx_in, *parts)


def mm_res(x, a, w, l):
    s, d = x.shape
    k = a.shape[1]
    tm = _tok(s)

    def body(x_ref, a_ref, w_ref, o_ref):
        o_ref[...] = x_ref[...] + dot_nn(a_ref[...], w_ref[...])

    return pl.pallas_call(
        body,
        name="mm_res",
        grid=(s // tm,),
        in_specs=[
            pl.BlockSpec((tm, d), lambda i: (i, 0)),
            pl.BlockSpec((tm, k), lambda i: (i, 0)),
            pl.BlockSpec((None, k, d), lambda i: (l, 0, 0)),
        ],
        out_specs=pl.BlockSpec((tm, d), lambda i: (i, 0)),
        out_shape=jax.ShapeDtypeStruct((s, d), F32),
        compiler_params=_cparams(("arbitrary",)),
    )(x, a, w)


def mm_nt(dx, w, l):
    s, d = dx.shape
    k = w.shape[1]
    tm = _tok(s)

    def body(dx_ref, w_ref, o_ref, dxb_ref):
        dxb = bf(dx_ref[...])
        dxb_ref[...] = dxb
        o_ref[...] = dot_nt(dxb, w_ref[...])

    return pl.pallas_call(
        body,
        name="mm_nt",
        grid=(s // tm,),
        in_specs=[pl.BlockSpec((tm, d), lambda i: (i, 0)), pl.BlockSpec((None, k, d), lambda i: (l, 0, 0))],
        out_specs=[pl.BlockSpec((tm, k), lambda i: (i, 0)), pl.BlockSpec((tm, d), lambda i: (i, 0))],
        out_shape=[jax.ShapeDtypeStruct((s, k), F32), jax.ShapeDtypeStruct((s, d), BF16)],
        compiler_params=_cparams(("arbitrary",)),
    )(dx, w)


def lb_fwd(logits3):
    def body(l_ref, o_ref):
        l0, l1 = l_ref[0], l_ref[1]
        m = jnp.maximum(l0, l1)
        e0, e1 = jnp.exp(l0 - m), jnp.exp(l1 - m)
        p0, p1 = e0 / (e0 + e1), e1 / (e0 + e1)
        o_ref[0] = p0 - p0
        o_ref[1] = (p0 + p1) - p0

    return pl.pallas_call(body, name="lb_fwd", out_shape=jax.ShapeDtypeStruct(logits3.shape, F32))(logits3)


def lb_bwd(logits3, dlb1):
    def body(l_ref, d_ref, o_ref):
        l0, l1 = l_ref[0], l_ref[1]
        m = jnp.maximum(l0, l1)
        e0, e1 = jnp.exp(l0 - m), jnp.exp(l1 - m)
        p0, p1 = e0 / (e0 + e1), e1 / (e0 + e1)
        t = d_ref[...] * p0 * p1
        o_ref[0] = -t
        o_ref[1] = t

    return pl.pallas_call(body, name="lb_bwd", out_shape=jax.ShapeDtypeStruct(logits3.shape, F32))(logits3, dlb1)


def _hgrn_gates(zq, zf, lb):
    sf = _sigmoid(zf)
    f = lb + (1.0 - lb) * sf
    sq = _sigmoid(zq)
    return sf, f, jnp.log(f), 1.0 - f, sq, zq * sq


def _tri(n, upper=False):
    r = lax.broadcasted_iota(jnp.int32, (n, n), 0)
    c = lax.broadcasted_iota(jnp.int32, (n, n), 1)
    return jnp.where((c >= r) if upper else (r >= c), 1.0, 0.0).astype(F32)


def hgrn_fwd(proj, lb3, og3, l):
    s = proj.shape[0]
    nh = 6
    n_chunk = s // CHUNK
    nsub = CHUNK // SUB

    def body(zq_ref, zf_ref, vi_ref, zg_ref, lb_ref, og_ref, main_ref, o_ref, q_s, k_s, v_s, c_s):
        lb = lb_ref[...]
        og = og_ref[...]
        tril = _tri(CHUNK)
        rowi = lax.broadcasted_iota(jnp.int32, (SUB, LANES), 0)

        def chunk(ci, st):
            r0 = pl.multiple_of(ci * CHUNK, CHUNK)
            rows = pl.ds(r0, CHUNK)
            zg = zg_ref[rows, :]
            _, _, lf, k, _, q = _hgrn_gates(zq_ref[rows, :], zf_ref[rows, :], lb)
            v = vi_ref[rows, :]
            c = dot_nn(tril, lf, HI)
            q_s[...] = q
            k_s[...] = k
            v_s[...] = v
            c_s[...] = c
            o_inter = dot_nt(q * jnp.exp(c), st, HI)
            parts = []
            for i in range(nsub):
                lo = i * SUB
                blk = pl.ds(lo, SUB)
                qb, kb, cb = q_s[blk, :], k_s[blk, :], c_s[blk, :]
                ob = o_inter[lo : lo + SUB]
                if i > 0:
                    rr = c_s[pl.ds(lo - 1, 1), :]
                    qt = qb * jnp.exp(cb - rr)
                    kt = k_s[pl.ds(0, lo), :] * jnp.exp(rr - c_s[pl.ds(0, lo), :])
                    ob = ob + dot_nn(dot_nt(qt, kt, HI), v_s[pl.ds(0, lo), :], HI)
                for t in range(SUB):
                    e = jnp.where(rowi >= t, jnp.exp(cb - c_s[pl.ds(lo + t, 1), :]), 0.0)
                    a = jnp.sum(qb * k_s[pl.ds(lo + t, 1), :] * e, axis=-1, keepdims=True)
                    ob = ob + a * v_s[pl.ds(lo + t, 1), :]
                parts.append(ob)
            o = jnp.concatenate(parts, axis=0)
            ce = c_s[pl.ds(CHUNK - 1, 1), :]
            st = st * jnp.exp(ce) + dot_tn(v, k * jnp.exp(ce - c), HI)
            on, _ = _rms(o)
            o_ref[rows, :] = o
            main_ref[rows, :] = bf(on * og * (zg * _sigmoid(zg)))
            return st

        lax.fori_loop(0, n_chunk, chunk, jnp.zeros((LANES, LANES), F32))

    def col(k):
        return pl.BlockSpec((s, LANES), lambda h: (0, k * nh + h))

    vec = pl.BlockSpec((None, 1, LANES), lambda h: (l, 0, h))
    return pl.pallas_call(
        body,
        name="hgrn_fwd",
        grid=(nh,),
        in_specs=[col(0), col(1), col(2), col(3), vec, pl.BlockSpec((None, 1, LANES), lambda h: (l, 0, 0))],
        out_specs=[pl.BlockSpec((s, LANES), lambda h: (0, h))] * 2,
        out_shape=[jax.ShapeDtypeStruct((s, nh * LANES), BF16), jax.ShapeDtypeStruct((s, nh * LANES), F32)],
        scratch_shapes=[pltpu.VMEM((CHUNK, LANES), F32)] * 4,
        compiler_params=_cparams(("arbitrary",)),
    )(proj, proj, proj, proj, lb3, og3)


def hgrn_bwd(proj, lb3, og3, l, o, dmixed):
    s = proj.shape[0]
    nh = 6
    n_chunk = s // CHUNK
    nsub = CHUNK // SUB

    def body(zq_ref, zf_ref, vi_ref, zg_ref, lb_ref, og_ref, o_ref, dm_ref,
             dzq_ref, dzf_ref, dvi_ref, dzg_ref, dlb_ref, dog_ref,
             st_s, q_s, k_s, v_s, c_s, do_s, dq_s, dk_s, dv_s, acc_s):
        lb = lb_ref[...]
        og = og_ref[...]
        tril = _tri(CHUNK)
        triu = _tri(CHUNK, upper=True)
        rowi = lax.broadcasted_iota(jnp.int32, (SUB, LANES), 0)

        def fwd_chunk(ci, st):
            rows = pl.ds(pl.multiple_of(ci * CHUNK, CHUNK), CHUNK)
            _, _, lf, k, _, _ = _hgrn_gates(zq_ref[rows, :], zf_ref[rows, :], lb)
            c = dot_nn(tril, lf, HI)
            ce = jnp.sum(lf, axis=0, keepdims=True)
            st_s[ci] = st
            return st * jnp.exp(ce) + dot_tn(vi_ref[rows, :], k * jnp.exp(ce - c), HI)

        lax.fori_loop(0, n_chunk, fwd_chunk, jnp.zeros((LANES, LANES), F32))
        acc_s[...] = jnp.zeros_like(acc_s)

        def bwd_chunk(jj, carry):
            dst, cg = carry
            ci = n_chunk - 1 - jj
            rows = pl.ds(pl.multiple_of(ci * CHUNK, CHUNK), CHUNK)
            zq, zf, zg = zq_ref[rows, :], zf_ref[rows, :], zg_ref[rows, :]
            sf, f, lf, k, sq, q = _hgrn_gates(zq, zf, lb)
            v = vi_ref[rows, :]
            c = dot_nn(tril, lf, HI)
            st = st_s[ci]
            on, r = _rms(o_ref[rows, :])
            sg = _sigmoid(zg)
            dmain = dm_ref[rows, :]
            dy = dmain * (zg * sg)
            dzg_ref[rows, :] = dmain * (on * og) * _dsilu(zg, sg)
            do, dog = _rms_bwd(dy, on, r, og)
            acc_s[pl.ds(0, 1), :] += dog
            q_s[...] = q
            k_s[...] = k
            v_s[...] = v
            c_s[...] = c
            do_s[...] = do
            ce = c_s[pl.ds(CHUNK - 1, 1), :]
            eq = jnp.exp(c)
            ek = jnp.exp(ce - c)
            qt_all = q * eq
            dq_s[...] = dot_nn(do, st, HI) * eq
            dv_s[...] = dot_nt(k * ek, dst, HI)
            dk_s[...] = dot_nn(v, dst, HI) * ek
            dst = dst * jnp.exp(ce) + dot_tn(do, qt_all, HI)
            for i in range(nsub):
                lo = i * SUB
                blk = pl.ds(lo, SUB)
                qb, cb, dob = q_s[blk, :], c_s[blk, :], do_s[blk, :]
                if i > 0:
                    prev = pl.ds(0, lo)
                    rr = c_s[pl.ds(lo - 1, 1), :]
                    eqi = jnp.exp(cb - rr)
                    eki = jnp.exp(rr - c_s[prev, :])
                    qt = qb * eqi
                    kt = k_s[prev, :] * eki
                    amat = dot_nt(qt, kt, HI)
                    damat = dot_nt(dob, v_s[prev, :], HI)
                    dv_s[prev, :] += dot_tn(amat, dob, HI)
                    dq_s[blk, :] += dot_nn(damat, kt, HI) * eqi
                    dk_s[prev, :] += dot_tn(damat, qt, HI) * eki
                dqb = jnp.zeros((SUB, LANES), F32)
                for t in range(SUB):
                    row = pl.ds(lo + t, 1)
                    e = jnp.where(rowi >= t, jnp.exp(cb - c_s[row, :]), 0.0)
                    kr = k_s[row, :]
                    a = jnp.sum(qb * kr * e, axis=-1, keepdims=True)
                    da = jnp.sum(dob * v_s[row, :], axis=-1, keepdims=True)
                    dv_s[row, :] += jnp.sum(a * dob, axis=0, keepdims=True)
                    dqb = dqb + da * kr * e
                    dk_s[row, :] += jnp.sum(da * qb * e, axis=0, keepdims=True)
                dq_s[blk, :] += dqb
            dq, dk = dq_s[...], dk_s[...]
            dg = q * dq - k * dk
            dlf = dot_nn(triu, dg, HI) + cg
            cg = cg + jnp.sum(dg, axis=0, keepdims=True)
            df = dlf / f - dk
            dzf_ref[rows, :] = df * (1.0 - lb) * sf * (1.0 - sf)
            acc_s[pl.ds(1, 1), :] += jnp.sum(df * (1.0 - sf), axis=0, keepdims=True)
            dzq_ref[rows, :] = dq * _dsilu(zq, sq)
            dvi_ref[rows, :] = dv_s[...]
            return dst, cg

        lax.fori_loop(0, n_chunk, bwd_chunk, (jnp.zeros((LANES, LANES), F32), jnp.zeros((1, LANES), F32)))
        dlb_ref[...] = acc_s[pl.ds(1, 1), :]

        @pl.when(pl.program_id(0) == 0)
        def _():
            dog_ref[...] = jnp.zeros_like(dog_ref)

        dog_ref[...] += acc_s[pl.ds(0, 1), :]

    def col(k):
        return pl.BlockSpec((s, LANES), lambda h: (0, k * nh + h))

    head = pl.BlockSpec((s, LANES), lambda h: (0, h))
    vec = pl.BlockSpec((None, 1, LANES), lambda h: (l, 0, h))
    ck = pltpu.VMEM((CHUNK, LANES), F32)
    return pl.pallas_call(
        body,
        name="hgrn_bwd",
        grid=(nh,),
        in_specs=[col(0), col(1), col(2), col(3), vec, pl.BlockSpec((None, 1, LANES), lambda h: (l, 0, 0)), head, head],
        out_specs=[head] * 4 + [pl.BlockSpec((1, LANES), lambda h: (0, h)), _const((1, LANES))],
        out_shape=[jax.ShapeDtypeStruct((s, nh * LANES), F32)] * 4
        + [jax.ShapeDtypeStruct((1, nh * LANES), F32), jax.ShapeDtypeStruct((1, LANES), F32)],
        scratch_shapes=[pltpu.VMEM((n_chunk, LANES, LANES), F32)] + [ck] * 8 + [pltpu.VMEM((8, LANES), F32)],
        compiler_params=_cparams(("arbitrary",)),
    )(proj, proj, proj, proj, lb3, og3, o, dmixed)


MEM_SCALE = HEAD64**-0.5


def _mem_heads(qraw, kvm, qg, kg, pr, m0):
    lo = pr * LANES
    uq, rq = _rms64(qraw[:, lo : lo + LANES], m0)
    uk, rk = _rms64(kvm[:, lo : lo + LANES], m0)
    v = bf(kvm[:, 2 * LANES + lo : 3 * LANES + lo])
    return uq, rq, uk, rk, v, uq * qg, bf(uk * kg)


def memattn_fwd(proj, qblk, kvm, qg3, kg3, l):
    s = proj.shape[0]
    nm = kvm.shape[0]
    tm = _tok(s)

    def body(q_ref, kv_ref, qg_ref, kg_ref, o_ref):
        m0 = _lane_mask0((1, LANES))
        qraw, kvv = q_ref[...], kv_ref[...]
        for pr in range(2):
            _, _, _, _, v, qn, kn = _mem_heads(qraw, kvv, qg_ref[...], kg_ref[...], pr, m0)
            out = jnp.zeros((tm, LANES), F32)
            for hh in range(2):
                mh = m0 if hh == 0 else jnp.logical_not(m0)
                sc = dot_nt(bf(jnp.where(mh, qn, 0.0)), kn) * MEM_SCALE
                p = jnp.exp(sc - jnp.max(sc, axis=-1, keepdims=True))
                p = p / jnp.sum(p, axis=-1, keepdims=True)
                out = jnp.where(mh, dot_nn(bf(p), v), out)
            o_ref[:, pr * LANES : (pr + 1) * LANES] = bf(out)

    gspec = pl.BlockSpec((None, 1, LANES), lambda i: (l, 0, 0))
    return pl.pallas_call(
        body,
        name="memattn_fwd",
        grid=(s // tm,),
        in_specs=[pl.BlockSpec((tm, 2 * LANES), lambda i: (i, qblk)), _const((nm, 4 * LANES)), gspec, gspec],
        out_specs=pl.BlockSpec((tm, 2 * LANES), lambda i: (i, 0)),
        out_shape=jax.ShapeDtypeStruct((s, 2 * LANES), BF16),
        compiler_params=_cparams(("arbitrary",)),
    )(proj, kvm, qg3, kg3)


def memattn_bwd(proj, qblk, kvm, qg3, kg3, l, dmixed):
    s = proj.shape[0]
    nm = kvm.shape[0]
    tm = _tok(s)

    def body(q_ref, kv_ref, qg_ref, kg_ref, dm_ref, dq_ref, dkv_ref, dqg_ref, dkg_ref):
        m0 = _lane_mask0((1, LANES))
        qraw, kvv = q_ref[...], kv_ref[...]
        qg, kg = qg_ref[...], kg_ref[...]

        @pl.when(pl.program_id(0) == 0)
        def _():
            dkv_ref[...] = jnp.zeros_like(dkv_ref)
            dqg_ref[...] = jnp.zeros_like(dqg_ref)
            dkg_ref[...] = jnp.zeros_like(dkg_ref)

        for pr in range(2):
            lo = pr * LANES
            uq, rq, uk, rk, v, qn, kn = _mem_heads(qraw, kvv, qg, kg, pr, m0)
            do = dm_ref[:, lo : lo + LANES]
            dqn = jnp.zeros((tm, LANES), F32)
            dkn = jnp.zeros((nm, LANES), F32)
            dv = jnp.zeros((nm, LANES), F32)
            for hh in range(2):
                mh = m0 if hh == 0 else jnp.logical_not(m0)
                qh = bf(jnp.where(mh, qn, 0.0))
                doh = bf(jnp.where(mh, do, 0.0))
                sc = dot_nt(qh, kn) * MEM_SCALE
                p = jnp.exp(sc - jnp.max(sc, axis=-1, keepdims=True))
                p = p / jnp.sum(p, axis=-1, keepdims=True)
                dp = dot_nt(doh, v)
                ds = bf(p * (dp - jnp.sum(p * dp, axis=-1, keepdims=True)))
                dqn = dqn + jnp.where(mh, dot_nn(ds, kn), 0.0) * MEM_SCALE
                dkn = dkn + dot_tn(ds, qh) * MEM_SCALE
                dv = dv + dot_tn(bf(p), doh)
            dqr, dqg = _rms64_bwd(dqn, uq, rq, qg, m0)
            dkr, dkg = _rms64_bwd(dkn, uk, rk, kg, m0)
            dq_ref[:, lo : lo + LANES] = dqr
            dkv_ref[:, lo : lo + LANES] += dkr
            dkv_ref[:, 2 * LANES + lo : 3 * LANES + lo] += dv
            dqg_ref[...] += dqg
            dkg_ref[...] += dkg

    gspec = pl.BlockSpec((None, 1, LANES), lambda i: (l, 0, 0))
    return pl.pallas_call(
        body,
        name="memattn_bwd",
        grid=(s // tm,),
        in_specs=[
            pl.BlockSpec((tm, 2 * LANES), lambda i: (i, qblk)),
            _const((nm, 4 * LANES)),
            gspec,
            gspec,
            pl.BlockSpec((tm, 2 * LANES), lambda i: (i, 3)),
        ],
        out_specs=[pl.BlockSpec((tm, 2 * LANES), lambda i: (i, 0)), _const((nm, 4 * LANES)), _const((1, LANES)), _const((1, LANES))],
        out_shape=[
            jax.ShapeDtypeStruct((s, 2 * LANES), F32),
            jax.ShapeDtypeStruct((nm, 4 * LANES), F32),
            jax.ShapeDtypeStruct((1, LANES), F32),
            jax.ShapeDtypeStruct((1, LANES), F32),
        ],
        compiler_params=_cparams(("arbitrary",)),
    )(proj, kvm, qg3, kg3, dmixed)


KV_MAIN = 768


def _log_sigmoid(z):
    return jnp.minimum(z, 0.0) - jnp.log(1.0 + jnp.exp(-jnp.abs(z)))


def kvprep_fwd(kvf, kg, fb):
    s = kvf.shape[0]
    tm = _tok(s)

    def body(kvf_ref, kg_ref, fb_ref, k_ref, v_ref, clf_ref, carry):
        m0 = _lane_mask0((1, LANES))

        @pl.when(pl.program_id(0) == 0)
        def _():
            carry[...] = jnp.zeros_like(carry)

        for j in range(KV_MAIN // LANES):
            u, _ = _rms64(kvf_ref[:, j * LANES : (j + 1) * LANES], m0)
            k_ref[:, j * LANES : (j + 1) * LANES] = bf(u * kg_ref[...])
        v_ref[...] = bf(kvf_ref[:, KV_MAIN : 2 * KV_MAIN])
        lf = _log_sigmoid(kvf_ref[:, 2 * KV_MAIN :] + fb_ref[...])
        clf_ref[...] = dot_nn(_tri(tm), lf, HI) + carry[...]
        carry[...] += jnp.sum(lf, axis=0, keepdims=True)

    n = kvf.shape[1]
    return pl.pallas_call(
        body,
        name="kvprep_fwd",
        grid=(s // tm,),
        in_specs=[pl.BlockSpec((tm, n), lambda i: (i, 0)), _const((1, LANES)), _const((1, LANES))],
        out_specs=[pl.BlockSpec((tm, KV_MAIN), lambda i: (i, 0))] * 2 + [pl.BlockSpec((tm, LANES), lambda i: (i, 0))],
        out_shape=[jax.ShapeDtypeStruct((s, KV_MAIN), BF16)] * 2 + [jax.ShapeDtypeStruct((s, LANES), F32)],
        scratch_shapes=[pltpu.VMEM((1, LANES), F32)],
        compiler_params=_cparams(("arbitrary",)),
    )(kvf, kg, fb)


def kvprep_bwd(kvf, kg, fb, dk, dv, dclf):
    s, n = kvf.shape
    tm = _tok(s)
    nb = s // tm

    def body(kvf_ref, kg_ref, fb_ref, dk_ref, dv_ref, dclf_ref, o_ref, dkg_ref, dfb_ref, carry):
        m0 = _lane_mask0((1, LANES))

        @pl.when(pl.program_id(0) == 0)
        def _():
            carry[...] = jnp.zeros_like(carry)
            dkg_ref[...] = jnp.zeros_like(dkg_ref)
            dfb_ref[...] = jnp.zeros_like(dfb_ref)

        kg_ = kg_ref[...]
        for j in range(KV_MAIN // LANES):
            cols = slice(j * LANES, (j + 1) * LANES)
            u, r = _rms64(kvf_ref[:, cols], m0)
            dkr, dkg = _rms64_bwd(dk_ref[:, cols], u, r, kg_, m0)
            o_ref[:, cols] = dkr
            dkg_ref[...] += dkg
        o_ref[:, KV_MAIN : 2 * KV_MAIN] = dv_ref[...]
        z = kvf_ref[:, 2 * KV_MAIN :] + fb_ref[...]
        dc = dclf_ref[...]
        dlf = dot_nn(_tri(tm, upper=True), dc, HI) + carry[...]
        carry[...] += jnp.sum(dc, axis=0, keepdims=True)
        dz = dlf * _sigmoid(-z)
        o_ref[:, 2 * KV_MAIN :] = dz
        dfb_ref[...] += jnp.sum(dz, axis=0, keepdims=True)

    rev = lambda i: (nb - 1 - i, 0)
    return pl.pallas_call(
        body,
        name="kvprep_bwd",
        grid=(nb,),
        in_specs=[pl.BlockSpec((tm, n), rev), _const((1, LANES)), _const((1, LANES)), pl.BlockSpec((tm, KV_MAIN), rev),
                  pl.BlockSpec((tm, KV_MAIN), rev), pl.BlockSpec((tm, LANES), rev)],
        out_specs=[pl.BlockSpec((tm, n), rev), _const((1, LANES)), _const((1, LANES))],
        out_shape=[jax.ShapeDtypeStruct((s, n), F32), jax.ShapeDtypeStruct((1, LANES), F32), jax.ShapeDtypeStruct((1, LANES), F32)],
        scratch_shapes=[pltpu.VMEM((1, LANES), F32)],
        compiler_params=_cparams(("arbitrary",)),
    )(kvf, kg, fb, dk, dv, dclf)


FOX_SCALE = HEAD64**-0.5


def _lane_col(block, lane_idx, h):
    return jnp.sum(jnp.where(lane_idx == h, block, 0.0), axis=-1, keepdims=True)


def _causal(tq, ext, i, transposed=False):
    if transposed:
        key = lax.broadcasted_iota(jnp.int32, (ext, tq), 0)
        qry = lax.broadcasted_iota(jnp.int32, (ext, tq), 1) + i * tq
    else:
        qry = lax.broadcasted_iota(jnp.int32, (tq, ext), 0) + i * tq
        key = lax.broadcasted_iota(jnp.int32, (tq, ext), 1)
    return key <= qry


def fox_fwd(proj, k_sh, v_sh, clf, clf_t, qg3, j_layer):
    s = proj.shape[0]
    npair = 6
    tq = TQ if s % TQ == 0 else s
    nq = s // tq

    def body(q_ref, gate_ref, k_ref, v_ref, clf_ref, clft_ref, qg_ref, main_ref, o_ref, lse_ref):
        j = pl.program_id(0)
        lane = lax.broadcasted_iota(jnp.int32, (1, LANES), 1)
        m0 = lane < HEAD64
        u, _ = _rms64(q_ref[...], m0)
        qn = u * qg_ref[...] * FOX_SCALE
        clfv = clf_ref[...]
        for hh in range(2):
            h = 2 * j + hh
            mh = m0 if hh == 0 else jnp.logical_not(m0)
            qh = bf(jnp.where(mh, qn, 0.0))
            dcol = _lane_col(clfv, lane, h)
            drow = clft_ref[pl.ds(h, 1), :]
            for i in range(nq):
                rows = slice(i * tq, (i + 1) * tq)
                ext = (i + 1) * tq
                sc = dot_nt(qh[rows], k_ref[0:ext, :]) + dcol[rows] - drow[:, :ext]
                sc = jnp.where(_causal(tq, ext, i), sc, -jnp.inf)
                m = jnp.max(sc, axis=-1, keepdims=True)
                p = jnp.exp(sc - m)
                lsum = jnp.sum(p, axis=-1, keepdims=True)
                pv = dot_nn(bf(p), v_ref[0:ext, :]) / lsum
                lse = m + jnp.log(lsum)
                if hh == 0:
                    o_ref[rows, :] = pv
                    lse_ref[rows, :] = jnp.where(lane == 0, lse, 0.0)
                else:
                    o_ref[rows, :] = jnp.where(mh, pv, o_ref[rows, :])
                    lse_ref[rows, :] = jnp.where(lane == 1, lse, lse_ref[rows, :])
        main_ref[...] = bf(o_ref[...] * _sigmoid(gate_ref[...]))

    blk = lambda off: pl.BlockSpec((s, LANES), lambda j: (0, off + j))
    return pl.pallas_call(
        body,
        name="fox_fwd",
        grid=(npair,),
        in_specs=[blk(0), blk(npair), blk(0), blk(0), _const((s, LANES)), _const((16, s)),
                  pl.BlockSpec((None, 1, LANES), lambda j: (j_layer, 0, 0))],
        out_specs=[blk(0)] * 3,
        out_shape=[jax.ShapeDtypeStruct((s, npair * LANES), BF16)] + [jax.ShapeDtypeStruct((s, npair * LANES), F32)] * 2,
        compiler_params=_cparams(("arbitrary",)),
    )(proj, proj, k_sh, v_sh, clf, clf_t, qg3)


def fox_bwd(proj, k_sh, v_sh, clf, clf_t, qg3, j_layer, o, lse, lse_t, dmixed, dk_in, dv_in, dclf_in):
    s = proj.shape[0]
    npair = 6
    tq = TQ if s % TQ == 0 else s
    nq = s // tq

    def body(q_ref, gate_ref, k_ref, v_ref, clf_ref, clft_ref, qg_ref, o_ref, lse_ref, lset_ref, dm_ref, dkin_ref, dvin_ref, dclfin_ref,
             dq_ref, dgate_ref, dk_ref, dv_ref, dclf_ref, dqg_ref, dqn_s, dcl_s):
        j = pl.program_id(0)
        lane = lax.broadcasted_iota(jnp.int32, (1, LANES), 1)
        m0 = lane < HEAD64
        qg = qg_ref[...]
        u, r = _rms64(q_ref[...], m0)
        qn = u * qg * FOX_SCALE
        ov = o_ref[...]
        gate = gate_ref[...]
        sg = _sigmoid(gate)
        dmain = dm_ref[...]
        do = dmain * sg
        dgate_ref[...] = dmain * ov * sg * (1.0 - sg)
        dk_ref[...] = dkin_ref[...]
        dv_ref[...] = dvin_ref[...]
        clfv = clf_ref[...]
        lsev = lse_ref[...]
        ones8 = jnp.ones((8, LANES), F32)

        @pl.when(j == 0)
        def _():
            dclf_ref[...] = dclfin_ref[...]
            dqg_ref[...] = jnp.zeros_like(dqg_ref)

        for hh in range(2):
            h = 2 * j + hh
            mh = m0 if hh == 0 else jnp.logical_not(m0)
            qh = bf(jnp.where(mh, qn, 0.0))
            doh = jnp.where(mh, do, 0.0)
            dohb = bf(doh)
            doo = doh * ov
            dcol = _lane_col(clfv, lane, h)
            drow = clft_ref[pl.ds(h, 1), :]
            lcol = _lane_col(lsev, lane, hh)
            lrow = lset_ref[pl.ds(h, 1), :]
            delta = jnp.sum(doo, axis=-1, keepdims=True)
            dcl_s[...] = jnp.zeros_like(dcl_s)
            for i in range(nq):
                rows = slice(i * tq, (i + 1) * tq)
                ext = (i + 1) * tq
                kk, vv = k_ref[0:ext, :], v_ref[0:ext, :]
                sc = dot_nt(qh[rows], kk) + dcol[rows] - drow[:, :ext]
                p = jnp.where(_causal(tq, ext, i), jnp.exp(sc - lcol[rows]), 0.0)
                ds = p * (dot_nt(dohb[rows], vv) - delta[rows])
                dqh = dot_nn(bf(ds), kk) * FOX_SCALE
                if hh == 0:
                    dqn_s[rows, :] = dqh
                else:
                    dqn_s[rows, :] = jnp.where(mh, dqh, dqn_s[rows, :])
                dcl_s[rows, :] += jnp.sum(ds, axis=-1, keepdims=True)
                sct = dot_nt(kk, qh[rows]) + drow[:, rows] - dcol[:ext]
                pt = jnp.where(_causal(tq, ext, i, transposed=True), jnp.exp(sct - lrow[:, rows]), 0.0)
                delta_row = dot_nt(ones8, doo[rows], HI)[0:1]
                dst = pt * (dot_nt(vv, dohb[rows]) - delta_row)
                dv_ref[0:ext, :] += dot_nn(bf(pt), dohb[rows])
                dk_ref[0:ext, :] += dot_nn(bf(dst), qh[rows])
                dcl_s[0:ext, :] -= jnp.sum(dst, axis=-1, keepdims=True)
            dclf_ref[...] += jnp.where(lane == h, dcl_s[...], 0.0)
        dqr, dqg = _rms64_bwd(dqn_s[...], u, r, qg, m0)
        dq_ref[...] = dqr
        dqg_ref[...] += dqg

    blk = lambda off: pl.BlockSpec((s, LANES), lambda j: (0, off + j))
    full = _const((s, LANES))
    return pl.pallas_call(
        body,
        name="fox_bwd",
        grid=(npair,),
        in_specs=[blk(0), blk(npair), blk(0), blk(0), full, _const((16, s)), pl.BlockSpec((None, 1, LANES), lambda j: (j_layer, 0, 0)),
                  blk(0), blk(0), _const((16, s)), blk(0), blk(0), blk(0), full],
        out_specs=[blk(0)] * 4 + [full, _const((1, LANES))],
        out_shape=[jax.ShapeDtypeStruct((s, npair * LANES), F32)] * 4
        + [jax.ShapeDtypeStruct((s, LANES), F32), jax.ShapeDtypeStruct((1, LANES), F32)],
        scratch_shapes=[pltpu.VMEM((s, LANES), F32), pltpu.VMEM((s, LANES), F32)],
        compiler_params=_cparams(("arbitrary",)),
    )(proj, proj, k_sh, v_sh, clf, clf_t, qg3, o, lse, lse_t, dmixed, dk_in, dv_in, dclf_in)


def loss_head(y, target):
    s, d = y.shape
    tm = _tok(s)

    def body(y_ref, t_ref, loss_ref, dy_ref):
        err = y_ref[...] - t_ref[...]
        dy_ref[...] = err * (1.0 / d)

        @pl.when(pl.program_id(0) == 0)
        def _():
            loss_ref[...] = jnp.zeros_like(loss_ref)

        part = jnp.sum(jnp.mean(err * err, axis=-1, keepdims=True), axis=0, keepdims=True)
        loss_ref[...] += 0.5 * part

    row = pl.BlockSpec((tm, d), lambda i: (i, 0))
    return pl.pallas_call(
        body,
        name="loss_head",
        grid=(s // tm,),
        in_specs=[row, row],
        out_specs=[_const((1, 1)), row],
        out_shape=[jax.ShapeDtypeStruct((1, 1), F32), jax.ShapeDtypeStruct((s, d), F32)],
        compiler_params=_cparams(("arbitrary",)),
    )(y, target)


def _row_tile(r, c, n_arrays):
    budget = VMEM_LIMIT_BYTES // 2
    padded_c = -(-c // LANES) * LANES
    best = None
    for t in range(8, r + 1, 8):
        if r % t == 0 and 2 * n_arrays * t * padded_c * 4 <= budget:
            best = t
    return r if best is None else best


def _as2d(a):
    return a.reshape(-1, a.shape[-1]) if a.ndim >= 2 else a.reshape(1, -1)


def adamw(w, g, m, v):
    shape = w.shape
    w2, g2, m2, v2 = (_as2d(t) for t in (w, g, m, v))
    r, c = w2.shape
    tr = _row_tile(r, c, 7)
    c1 = 1.0 - ADAM_B1**ADAM_STEP
    c2 = 1.0 - ADAM_B2**ADAM_STEP

    def body(w_ref, g_ref, m_ref, v_ref, d_ref, nm_ref, nv_ref):
        gv = g_ref[...]
        nm = ADAM_B1 * m_ref[...] + (1.0 - ADAM_B1) * gv
        nv = ADAM_B2 * v_ref[...] + (1.0 - ADAM_B2) * (gv * gv)
        nm_ref[...] = nm
        nv_ref[...] = nv
        d_ref[...] = -ADAM_LR * ((nm / c1) / (jnp.sqrt(nv / c2) + ADAM_EPS) + ADAM_WD * w_ref[...])

    spec = pl.BlockSpec((tr, c), lambda i: (i, 0))
    outs = pl.pallas_call(
        body,
        name="adamw",
        grid=(r // tr,),
        in_specs=[spec] * 4,
        out_specs=[spec] * 3,
        out_shape=[jax.ShapeDtypeStruct((r, c), F32)] * 3,
        compiler_params=_cparams(("arbitrary",)),
    )(w2, g2, m2, v2)
    return tuple(t.reshape(shape) for t in outs)


def pair_sum(g, recv, c_arr):
    _, k, r, c = g.shape
    tr = _row_tile(r, c, 3)

    def body(c_ref, g_ref, r_ref, o_ref):
        o_ref[...] = bf(g_ref[...] + r_ref[...])

    return pl.pallas_call(
        body,
        name="pair_sum",
        grid_spec=pltpu.PrefetchScalarGridSpec(
            num_scalar_prefetch=1,
            grid=(k, r // tr),
            in_specs=[pl.BlockSpec((None, None, tr, c), lambda kk, i, cr: (cr[0], kk, i, 0)), pl.BlockSpec((None, tr, c), lambda kk, i, cr: (kk, i, 0))],
            out_specs=pl.BlockSpec((None, tr, c), lambda kk, i, cr: (kk, i, 0)),
        ),
        out_shape=jax.ShapeDtypeStruct((k, r, c), BF16),
        compiler_params=_cparams(("arbitrary", "arbitrary")),
    )(c_arr, g, recv)


def chip_sum(q):
    k, r, c = q.shape
    tr = _row_tile(r, c, 3)

    def body(q_ref, o_ref):
        acc = q_ref[0].astype(F32)
        for i in range(1, k):
            acc = acc + q_ref[i].astype(F32)
        o_ref[...] = acc

    return pl.pallas_call(
        body,
        name="chip_sum",
        grid=(r // tr,),
        in_specs=[pl.BlockSpec((k, tr, c), lambda i: (0, i, 0))],
        out_specs=pl.BlockSpec((tr, c), lambda i: (i, 0)),
        out_shape=jax.ShapeDtypeStruct((r, c), F32),
        compiler_params=_cparams(("arbitrary",)),
    )(q)


def _place():
    x, y, c = lax.axis_index("x"), lax.axis_index("y"), lax.axis_index("c")
    chips = [(1 - x, y), (x, 1 - y), (1 - x, 1 - y)]
    return x, y, c, 2 * x + y, chips, [2 * cx + cy for cx, cy in chips]


def _rcopy(src, dst, send, recv, dev):
    return pltpu.make_async_remote_copy(src_ref=src, dst_ref=dst, send_sem=send, recv_sem=recv, device_id=dev, device_id_type=MESH)


def all_gather_chips(shards):
    n = len(shards)

    def body(*refs):
        ins, outs = refs[:n], refs[n : 2 * n]
        send, recv, loc = refs[2 * n :]
        x, y, c, me, chips, cidx = _place()
        sib = (x, y, 1 - c)
        local = [pltpu.make_async_copy(ins[a], outs[a].at[me], loc.at[a]) for a in range(n)]
        for cp in local:
            cp.start()
        sends = [
            _rcopy(ins[a].at[c], outs[a].at[me, c], send.at[a, j], recv.at[a, j], (*chips[j], c))
            for a in range(n)
            for j in range(3)
        ]
        for cp in sends:
            cp.start()
        passed = []
        for a in range(n):
            for j in range(3):
                landed = outs[a].at[cidx[j], c]
                _rcopy(ins[a].at[c], landed, send.at[a, j], recv.at[a, j], (*chips[j], c)).wait_recv()
                fwd = _rcopy(landed, landed, send.at[a, 3 + j], recv.at[a, 3 + j], sib)
                fwd.start()
                passed.append(fwd)
        for a in range(n):
            for j in range(3):
                theirs = outs[a].at[cidx[j], 1 - c]
                _rcopy(theirs, theirs, send.at[a, 3 + j], recv.at[a, 3 + j], sib).wait_recv()
        for cp in sends + passed:
            cp.wait_send()
        for cp in local:
            cp.wait()

    return pl.pallas_call(
        body,
        name="all_gather_chips",
        in_specs=[ANY] * n,
        out_specs=[ANY] * n,
        out_shape=[jax.ShapeDtypeStruct((N_CHIPS,) + t.shape, t.dtype) for t in shards],
        scratch_shapes=[pltpu.SemaphoreType.DMA((n, 6)), pltpu.SemaphoreType.DMA((n, 6)), pltpu.SemaphoreType.DMA((n,))],
    )(*shards)


def pair_exchange(gs):
    n = len(gs)

    def body(*refs):
        ins, outs = refs[:n], refs[n : 2 * n]
        send, recv = refs[2 * n :]
        x, y, c = lax.axis_index("x"), lax.axis_index("y"), lax.axis_index("c")
        cps = [_rcopy(ins[a].at[1 - c], outs[a], send.at[a], recv.at[a], (x, y, 1 - c)) for a in range(n)]
        for cp in cps:
            cp.start()
        for cp in cps:
            cp.wait()

    return pl.pallas_call(
        body,
        name="pair_exchange",
        in_specs=[ANY] * n,
        out_specs=[ANY] * n,
        out_shape=[jax.ShapeDtypeStruct(t.shape[1:], t.dtype) for t in gs],
        scratch_shapes=[pltpu.SemaphoreType.DMA((n,)), pltpu.SemaphoreType.DMA((n,))],
    )(*gs)


def chip_exchange(ps):
    n = len(ps)

    def body(*refs):
        ins, outs = refs[:n], refs[n : 2 * n]
        send, recv, loc = refs[2 * n :]
        x, y, c, me, chips, cidx = _place()
        local = [pltpu.make_async_copy(ins[a].at[me], outs[a].at[me], loc.at[a]) for a in range(n)]
        for cp in local:
            cp.start()
        cps = [
            _rcopy(ins[a].at[cidx[j]], outs[a].at[me], send.at[a, j], recv.at[a, j], (*chips[j], c))
            for a in range(n)
            for j in range(3)
        ]
        for cp in cps:
            cp.start()
        for a in range(n):
            for j in range(3):
                landed = outs[a].at[cidx[j]]
                _rcopy(landed, landed, send.at[a, j], recv.at[a, j], (*chips[j], c)).wait_recv()
        for cp in cps:
            cp.wait_send()
        for cp in local:
            cp.wait()

    return pl.pallas_call(
        body,
        name="chip_exchange",
        in_specs=[ANY] * n,
        out_specs=[ANY] * n,
        out_shape=[jax.ShapeDtypeStruct(t.shape, t.dtype) for t in ps],
        scratch_shapes=[pltpu.SemaphoreType.DMA((n, 3)), pltpu.SemaphoreType.DMA((n, 3)), pltpu.SemaphoreType.DMA((n,))],
    )(*ps)


def pair_share(fs):
    n = len(fs)

    def body(*refs):
        ins, outs = refs[:n], refs[n : 2 * n]
        send, recv, loc = refs[2 * n :]
        x, y, c = lax.axis_index("x"), lax.axis_index("y"), lax.axis_index("c")
        local = [pltpu.make_async_copy(ins[a], outs[a].at[c], loc.at[a]) for a in range(n)]
        cps = [_rcopy(ins[a], outs[a].at[c], send.at[a], recv.at[a], (x, y, 1 - c)) for a in range(n)]
        for cp in local + cps:
            cp.start()
        for a in range(n):
            theirs = outs[a].at[1 - c]
            _rcopy(theirs, theirs, send.at[a], recv.at[a], (x, y, 1 - c)).wait_recv()
        for cp in cps:
            cp.wait_send()
        for cp in local:
            cp.wait()

    return pl.pallas_call(
        body,
        name="pair_share",
        in_specs=[ANY] * n,
        out_specs=[ANY] * n,
        out_shape=[jax.ShapeDtypeStruct((2,) + t.shape, t.dtype) for t in fs],
        scratch_shapes=[pltpu.SemaphoreType.DMA((n,)), pltpu.SemaphoreType.DMA((n,)), pltpu.SemaphoreType.DMA((n,))],
    )(*fs)


def small_allreduce(buf):
    r = buf.shape[0]

    def body(b_ref, o_ref, slots, send, recv):
        x, y, c = lax.axis_index("x"), lax.axis_index("y"), lax.axis_index("c")
        me = 4 * x + 2 * y + c
        slots[me] = b_ref[...]
        cps = []
        peers = []
        for mask in range(1, N_DEV):
            fx, fy, fc = (mask >> 2) & 1, (mask >> 1) & 1, mask & 1
            px, py, pc = (1 - x if fx else x), (1 - y if fy else y), (1 - c if fc else c)
            peers.append(4 * px + 2 * py + pc)
            cps.append(_rcopy(b_ref, slots.at[me], send.at[mask - 1], recv.at[mask - 1], (px, py, pc)))
        for cp in cps:
            cp.start()
        for k, pid in enumerate(peers):
            landed = slots.at[pid]
            _rcopy(landed, landed, send.at[k], recv.at[k], (x, y, c)).wait_recv()
        for cp in cps:
            cp.wait_send()
        acc = slots[0]
        for i in range(1, N_DEV):
            acc = acc + slots[i]
        o_ref[...] = acc

    vm = pl.BlockSpec(memory_space=pltpu.VMEM)
    return pl.pallas_call(
        body,
        name="small_allreduce",
        in_specs=[vm],
        out_specs=vm,
        out_shape=jax.ShapeDtypeStruct(buf.shape, F32),
        scratch_shapes=[pltpu.VMEM((N_DEV, r, LANES), F32), pltpu.SemaphoreType.DMA((N_DEV - 1,)), pltpu.SemaphoreType.DMA((N_DEV - 1,))],
    )(buf)


WEIGHT_NAMES = ["ffn1_norm", "ffn1_w_gate", "ffn1_w_up", "ffn1_w_down", "mix_norm", "mem_norm", "w_mem_kv", "mem_q_gain",
                "mem_k_gain", "w_in_a", "hgrn_lb_logits", "hgrn_o_gain", "w_in_b", "fox_q_gain", "kv_norm", "w_kv", "fox_f_bias",
                "fox_k_gain", "w_out", "ffn2_norm", "ffn2_w_gate", "ffn2_w_up", "ffn2_w_down"]
SHARDED = ["ffn1_w_gate", "ffn1_w_up", "ffn1_w_down", "w_mem_kv", "w_in_a", "w_in_b", "w_kv", "w_out", "ffn2_w_gate", "ffn2_w_up", "ffn2_w_down"]
SMALL = [n for n in WEIGHT_NAMES if n not in SHARDED]
N_LAYERS, N_A = 4, 2
KV_PAD = 13 * LANES


def _halves(t):
    return t.reshape((2, t.shape[0] // 2) + t.shape[1:])


def _cols_from_chips(g):
    return jnp.moveaxis(g, 0, 2).reshape(g.shape[1], g.shape[2], N_CHIPS * g.shape[3])


def _rows_from_chips(g):
    return jnp.moveaxis(g, 0, 1).reshape(g.shape[1], N_CHIPS * g.shape[2], g.shape[3])


def _pair_tile(g):
    return jnp.tile(g, (1, 2)).reshape(g.shape[0], 1, LANES)


def _pair_fold(g):
    return g[:, :HEAD64] + g[:, HEAD64:]


def kernel(x, mem, ffn1_norm, ffn1_w_gate, ffn1_w_up, ffn1_w_down, mix_norm, mem_norm, w_mem_kv, mem_q_gain, mem_k_gain, w_in_a, hgrn_lb_logits, hgrn_o_gain, w_in_b, fox_q_gain, kv_norm, w_kv, fox_f_bias, fox_k_gain, w_out, ffn2_norm, ffn2_w_gate, ffn2_w_up, ffn2_w_down, loss_target, m_ffn1_norm, m_ffn1_w_gate, m_ffn1_w_up, m_ffn1_w_down, m_mix_norm, m_mem_norm, m_w_mem_kv, m_mem_q_gain, m_mem_k_gain, m_w_in_a, m_hgrn_lb_logits, m_hgrn_o_gain, m_w_in_b, m_fox_q_gain, m_kv_norm, m_w_kv, m_fox_f_bias, m_fox_k_gain, m_w_out, m_ffn2_norm, m_ffn2_w_gate, m_ffn2_w_up, m_ffn2_w_down, v_ffn1_norm, v_ffn1_w_gate, v_ffn1_w_up, v_ffn1_w_down, v_mix_norm, v_mem_norm, v_w_mem_kv, v_mem_q_gain, v_mem_k_gain, v_w_in_a, v_hgrn_lb_logits, v_hgrn_o_gain, v_w_in_b, v_fox_q_gain, v_kv_norm, v_w_kv, v_fox_f_bias, v_fox_k_gain, v_w_out, v_ffn2_norm, v_ffn2_w_gate, v_ffn2_w_up, v_ffn2_w_down):
    given = dict(locals())
    w = {n: given[n] for n in WEIGHT_NAMES}
    xs, mems, tgt = x[0], mem[0], loss_target[0]
    s, d = xs.shape
    my_chip = 2 * lax.axis_index("x") + lax.axis_index("y")
    c_arr = lax.axis_index("c").astype(jnp.int32).reshape(1)

    to_gather = [_halves(bf(w[n])) for n in SHARDED] + [hgrn_lb_logits.reshape(2, 1, -1)]
    gathered = all_gather_chips(to_gather)
    gw = {n: g.reshape((N_CHIPS,) + w[n].shape) for n, g in zip(SHARDED, gathered[:-1])}
    w_in = {"a": _cols_from_chips(gw["w_in_a"]), "b": _cols_from_chips(gw["w_in_b"])}
    w_kv_full = _cols_from_chips(gw["w_kv"][:, None])
    w_kv_full = jnp.pad(w_kv_full, ((0, 0), (0, 0), (0, KV_PAD - w_kv_full.shape[-1])))
    w_mkv = _rows_from_chips(gw["w_mem_kv"])
    w_o = _rows_from_chips(gw["w_out"])
    logits3 = jnp.moveaxis(gathered[-1].reshape(N_CHIPS, 2, -1), 0, 1).reshape(2, 1, -1)
    lb3 = lb_fwd(logits3)

    norm3 = {n: w[n].reshape(N_LAYERS, 1, d) for n in ("ffn1_norm", "mix_norm", "mem_norm", "ffn2_norm")}
    kvn3 = kv_norm.reshape(1, 1, d)
    mqg3, mkg3 = _pair_tile(mem_q_gain), _pair_tile(mem_k_gain)
    og3 = hgrn_o_gain.reshape(N_A, 1, LANES)
    fqg3 = _pair_tile(fox_q_gain)
    fkg = jnp.tile(fox_k_gain, 2).reshape(1, LANES)
    fb = jnp.pad(fox_f_bias, (0, LANES - fox_f_bias.shape[0])).reshape(1, LANES)

    sv = [dict() for _ in range(N_LAYERS)]
    h = xs
    kv = None
    for l in range(N_LAYERS):
        t = sv[l]
        t["x0"] = h
        h, t["a1"], t["b1"] = ffn_fwd(h, norm3["ffn1_norm"], gw["ffn1_w_gate"], gw["ffn1_w_up"], gw["ffn1_w_down"], l)
        t["x1"] = h
        if l < N_A:
            t["proj"] = proj_fwd(h, norm3["mix_norm"], l, w_in["a"], l)
            main, t["o"] = hgrn_fwd(t["proj"], lb3, og3, l)
            t["qblk"] = 12
        else:
            t["proj"] = proj_fwd(h, norm3["mix_norm"], l, w_in["b"], l - N_A)
            main, t["o"], t["lse"] = fox_fwd(t["proj"], kv["k"], kv["v"], kv["clf"], kv["clf_t"], fqg3, l - N_A)
            t["qblk"] = 6
        t["kvm"] = proj_fwd(mems, norm3["mem_norm"], l, w_mkv, l)
        memo = memattn_fwd(t["proj"], t["qblk"], t["kvm"], mqg3, mkg3, l)
        t["mixed"] = jnp.concatenate([main, memo], axis=-1)
        h = mm_res(h, t["mixed"], w_o, l)
        t["x2"] = h
        h, t["a2"], t["b2"] = ffn_fwd(h, norm3["ffn2_norm"], gw["ffn2_w_gate"], gw["ffn2_w_up"], gw["ffn2_w_down"], l)
        if l == N_A - 1:
            kv = {"x": h, "kvf": proj_fwd(h, kvn3, 0, w_kv_full, 0)}
            kv["k"], kv["v"], kv["clf"] = kvprep_fwd(kv["kvf"], fkg, fb)
            kv["clf_t"] = kv["clf"][:, :16].T

    loss_local, dx = loss_head(h, tgt)

    nc = N_CHIPS
    fc = ffn1_w_gate.shape[-1]
    gbuf = {}
    for n in ("ffn1_w_gate", "ffn1_w_up", "ffn2_w_gate", "ffn2_w_up"):
        gbuf[n] = lax.empty((2, nc, N_LAYERS // 2, d, fc), F32)
    for n in ("ffn1_w_down", "ffn2_w_down"):
        gbuf[n] = lax.empty((2, nc, N_LAYERS // 2, fc, d), F32)
    dw_in = {"a": [None] * N_A, "b": [None] * (N_LAYERS - N_A)}
    dw_o, dw_mkv = [None] * N_LAYERS, [None] * N_LAYERS
    sg = {n: [None] * N_LAYERS for n in ("ffn1_norm", "mix_norm", "mem_norm", "ffn2_norm", "mem_q_gain", "mem_k_gain")}
    sg["hgrn_o_gain"], sg["fox_q_gain"], dlb = [None] * N_A, [None] * (N_LAYERS - N_A), [None] * N_A
    dk_sh = jnp.zeros((s, KV_MAIN), F32)
    dv_sh = jnp.zeros((s, KV_MAIN), F32)
    dclf = jnp.zeros((s, LANES), F32)
    zero_mem = jnp.zeros(mems.shape, F32)
    dw_kv = None
    for l in reversed(range(N_LAYERS)):
        t = sv[l]
        if l == N_A - 1:
            dkvf, dfkg, dfb = kvprep_bwd(kv["kvf"], fkg, fb, dk_sh, dv_sh, dclf)
            dx, sg["kv_norm"], xn_kv, dpb = proj_bwd(kv["x"], kvn3, 0, [dkvf], w_kv_full, 0, dx)
            dw_kv = wgrad(xn_kv, dpb)
        dx, da, db, hm, xn, dyb, sg["ffn2_norm"][l] = ffn_bwd(t["x2"], norm3["ffn2_norm"], dx, t["a2"], t["b2"], gw["ffn2_w_gate"], gw["ffn2_w_up"], gw["ffn2_w_down"], l)
        gbuf["ffn2_w_gate"] = wgrad(xn, da, buf=gbuf["ffn2_w_gate"], l=l)
        gbuf["ffn2_w_up"] = wgrad(xn, db, buf=gbuf["ffn2_w_up"], l=l)
        gbuf["ffn2_w_down"] = wgrad(hm, dyb, buf=gbuf["ffn2_w_down"], l=l)
        dmixed, dxb = mm_nt(dx, w_o, l)
        dw_o[l] = wgrad(t["mixed"], dxb)
        dqm, dkvm, dmq, dmk = memattn_bwd(t["proj"], t["qblk"], t["kvm"], mqg3, mkg3, l, dmixed)
        sg["mem_q_gain"][l], sg["mem_k_gain"][l] = _pair_fold(dmq), _pair_fold(dmk)
        _, sg["mem_norm"][l], memn, dkvmb = proj_bwd(mems, norm3["mem_norm"], l, [dkvm], w_mkv, l, zero_mem)
        dw_mkv[l] = wgrad(memn, dkvmb)
        if l < N_A:
            dzq, dzf, dvi, dzg, dlb[l], sg["hgrn_o_gain"][l] = hgrn_bwd(t["proj"], lb3, og3, l, t["o"], dmixed)
            parts, key, wl, tn = [dzq, dzf, dvi, dzg, dqm], "a", l, 13 * LANES
        else:
            lse_t = t["lse"].reshape(s, 6, LANES)[:, :, :2].reshape(s, 12).T
            lse_t = jnp.pad(lse_t, ((0, 4), (0, 0)))
            dq, dgate, dk_sh, dv_sh, dclf, dfq = fox_bwd(t["proj"], kv["k"], kv["v"], kv["clf"], kv["clf_t"], fqg3, l - N_A, t["o"], t["lse"], lse_t, dmixed, dk_sh, dv_sh, dclf)
            sg["fox_q_gain"][l - N_A] = _pair_fold(dfq)
            parts, key, wl, tn = [dq, dgate, dqm], "b", l - N_A, 7 * LANES
        dx, sg["mix_norm"][l], hn, dpb = proj_bwd(t["x1"], norm3["mix_norm"], l, parts, w_in[key], wl, dx)
        dw_in[key][wl] = wgrad(hn, dpb, tn=tn)
        dx, da, db, hm, xn, dyb, sg["ffn1_norm"][l] = ffn_bwd(t["x0"], norm3["ffn1_norm"], dx, t["a1"], t["b1"], gw["ffn1_w_gate"], gw["ffn1_w_up"], gw["ffn1_w_down"], l)
        gbuf["ffn1_w_gate"] = wgrad(xn, da, buf=gbuf["ffn1_w_gate"], l=l)
        gbuf["ffn1_w_up"] = wgrad(xn, db, buf=gbuf["ffn1_w_up"], l=l)
        gbuf["ffn1_w_down"] = wgrad(hm, dyb, buf=gbuf["ffn1_w_down"], l=l)

    def col_layout(stack):
        lw, dd, nn = stack.shape
        g = stack.reshape(2, lw // 2, dd, nc, nn // nc)
        return jnp.transpose(g, (0, 3, 1, 2, 4)).reshape(2, nc, (lw // 2) * dd, nn // nc)

    def row_layout(stack):
        lw, rr, cc = stack.shape
        g = stack.reshape(2, lw // 2, nc, rr // nc, cc)
        return jnp.transpose(g, (0, 2, 1, 3, 4)).reshape(2, nc, (lw // 2) * (rr // nc), cc)

    glay = {n: gbuf[n].reshape(2, nc, -1, gbuf[n].shape[-1]) for n in gbuf}
    glay["w_in_a"] = col_layout(jnp.stack(dw_in["a"]))
    glay["w_in_b"] = col_layout(jnp.stack(dw_in["b"]))
    kv_cols = w_kv.shape[-1] * nc
    dkv_rows = dw_kv[:, :kv_cols].reshape(2, d // 2, nc, kv_cols // nc)
    glay["w_kv"] = jnp.transpose(dkv_rows, (0, 2, 1, 3))
    glay["w_mem_kv"] = row_layout(jnp.stack(dw_mkv))
    glay["w_out"] = row_layout(jnp.stack(dw_o))

    gl = [glay[n] for n in SHARDED]
    recv = pair_exchange(gl)
    partial = [pair_sum(g, r, c_arr) for g, r in zip(gl, recv)]
    landed = chip_exchange(partial)
    mine = [chip_sum(q) for q in landed]
    both = pair_share(mine)
    grads = {n: g.reshape(w[n].shape) for n, g in zip(SHARDED, both)}

    dlogits = lb_bwd(logits3, dlb[1]).reshape(2, -1)
    small = {
        "ffn1_norm": jnp.concatenate(sg["ffn1_norm"]), "mix_norm": jnp.concatenate(sg["mix_norm"]),
        "mem_norm": jnp.concatenate(sg["mem_norm"]), "ffn2_norm": jnp.concatenate(sg["ffn2_norm"]),
        "mem_q_gain": jnp.concatenate(sg["mem_q_gain"]), "mem_k_gain": jnp.concatenate(sg["mem_k_gain"]),
        "hgrn_o_gain": jnp.concatenate(sg["hgrn_o_gain"]), "fox_q_gain": jnp.concatenate(sg["fox_q_gain"]),
        "kv_norm": sg["kv_norm"], "fox_f_bias": dfb[:, : fox_f_bias.shape[0]], "fox_k_gain": _pair_fold(dfkg),
        "hgrn_lb_logits": dlogits,
    }
    flat = [small[n].reshape(-1) for n in SMALL] + [loss_local.reshape(-1)]
    sizes = [f.shape[0] for f in flat]
    total = sum(sizes)
    padded = -(-total // (8 * LANES)) * (8 * LANES)
    packed = jnp.pad(jnp.concatenate(flat), (0, padded - total)).reshape(-1, LANES)
    summed = small_allreduce(packed).reshape(-1)
    off = 0
    for n, sz in zip(SMALL, sizes[:-1]):
        grads[n] = summed[off : off + sz].reshape(dlogits.shape if n == "hgrn_lb_logits" else w[n].shape)
        off += sz
    loss = summed[off]
    lbw = hgrn_lb_logits.shape[1]
    grads["hgrn_lb_logits"] = lax.dynamic_slice_in_dim(grads["hgrn_lb_logits"], my_chip * lbw, lbw, axis=1)

    delta, new_m, new_v = {}, {}, {}
    for n in WEIGHT_NAMES:
        delta[n], new_m[n], new_v[n] = adamw(w[n], grads[n], given["m_" + n], given["v_" + n])
    return (loss, dx[None], *[grads[n] for n in WEIGHT_NAMES], *[delta[n] for n in WEIGHT_NAMES],
            *[new_m[n] for n in WEIGHT_NAMES], *[new_v[n] for n in WEIGHT_NAMES])
```

```python
import functools

import jax
import jax.numpy as jnp
from jax import lax
from jax.experimental import pallas as pl
from jax.experimental.pallas import tpu as pltpu

F32, BF16 = jnp.float32, jnp.bfloat16
HI = lax.Precision.HIGHEST
EPS = 1e-6
MESH = pl.DeviceIdType.MESH
ANY = pl.BlockSpec(memory_space=pl.ANY)

VMEM_LIMIT_BYTES = 56 << 20
N_CHIPS = 4
N_DEV = 8
LANES = 128
HEAD64 = 64
CHUNK = 64
SUB = 16
TQ = 256
TOK = 256

ADAM_LR, ADAM_B1, ADAM_B2, ADAM_EPS, ADAM_WD, ADAM_STEP = 0.001, 0.9, 0.999, 1e-08, 0.01, 10


def _cparams(sem=None):
    return pltpu.CompilerParams(dimension_semantics=sem, vmem_limit_bytes=VMEM_LIMIT_BYTES)


def _mm(a, b, dims, prec=None):
    return lax.dot_general(a, b, (dims, ((), ())), preferred_element_type=F32, precision=prec)


def dot_nn(a, b, prec=None):
    return _mm(a, b, ((1,), (0,)), prec)


def dot_nt(a, b, prec=None):
    return _mm(a, b, ((1,), (1,)), prec)


def dot_tn(a, b, prec=None):
    return _mm(a, b, ((0,), (0,)), prec)


def bf(v):
    return v.astype(BF16)


def _sigmoid(z):
    return jax.nn.sigmoid(z)


def _dsilu(z, s):
    return s * (1.0 + z * (1.0 - s))


def _rms(x):
    r = lax.rsqrt(jnp.mean(x * x, axis=-1, keepdims=True) + EPS)
    return x * r, r


def _rms_bwd(dxn, u, r, g):
    du = dxn * g
    dx = r * (du - u * jnp.mean(du * u, axis=-1, keepdims=True))
    return dx, jnp.sum(dxn * u, axis=0, keepdims=True)


def _lane_mask0(shape):
    return lax.broadcasted_iota(jnp.int32, shape, len(shape) - 1) < HEAD64


def _rms64(x, m0):
    sq = x * x
    s0 = jnp.sum(jnp.where(m0, sq, 0.0), axis=-1, keepdims=True)
    s1 = jnp.sum(jnp.where(m0, 0.0, sq), axis=-1, keepdims=True)
    r = lax.rsqrt(jnp.where(m0, s0, s1) * (1.0 / HEAD64) + EPS)
    return x * r, r


def _rms64_bwd(dxn, u, r, g, m0):
    du = dxn * g
    t = du * u
    t0 = jnp.sum(jnp.where(m0, t, 0.0), axis=-1, keepdims=True)
    t1 = jnp.sum(jnp.where(m0, 0.0, t), axis=-1, keepdims=True)
    dx = r * (du - u * (jnp.where(m0, t0, t1) * (1.0 / HEAD64)))
    return dx, jnp.sum(dxn * u, axis=0, keepdims=True)


def _tok(s):
    return TOK if s % TOK == 0 else s


def _const(shape):
    return pl.BlockSpec(shape, lambda *_: (0,) * len(shape))


def ffn_fwd(x, gain3, wg, wu, wd, l):
    s, d = x.shape
    nc, _, _, fc = wg.shape
    tm = _tok(s)

    def body(x_ref, g_ref, wg_ref, wu_ref, wd_ref, xo_ref, a_ref, b_ref):
        xv = x_ref[...]
        u, _ = _rms(xv)
        xn = bf(u * g_ref[...])
        y = jnp.zeros((tm, d), F32)
        for c in range(nc):
            a = dot_nn(xn, wg_ref[c])
            b = dot_nn(xn, wu_ref[c])
            a_ref[c] = bf(a)
            b_ref[c] = bf(b)
            y = y + dot_nn(bf(a * _sigmoid(a) * b), wd_ref[c])
        xo_ref[...] = xv + 0.5 * y

    wspec = pl.BlockSpec((nc, None, d, fc), lambda i: (0, l, 0, 0), pipeline_mode=pl.Buffered(1))
    wdspec = pl.BlockSpec((nc, None, fc, d), lambda i: (0, l, 0, 0), pipeline_mode=pl.Buffered(1))
    row = pl.BlockSpec((tm, d), lambda i: (i, 0))
    act = pl.BlockSpec((nc, tm, fc), lambda i: (0, i, 0))
    return pl.pallas_call(
        body,
        name="ffn_fwd",
        grid=(s // tm,),
        in_specs=[row, pl.BlockSpec((None, 1, d), lambda i: (l, 0, 0)), wspec, wspec, wdspec],
        out_specs=[row, act, act],
        out_shape=[
            jax.ShapeDtypeStruct((s, d), F32),
            jax.ShapeDtypeStruct((nc, s, fc), BF16),
            jax.ShapeDtypeStruct((nc, s, fc), BF16),
        ],
        compiler_params=_cparams(("arbitrary",)),
    )(x, gain3, wg, wu, wd)


def ffn_bwd(x, gain3, dout, a, b, wg, wu, wd, l):
    s, d = x.shape
    nc, _, _, fc = wg.shape
    tm = _tok(s)

    def body(x_ref, g_ref, do_ref, a_ref, b_ref, wg_ref, wu_ref, wd_ref, dx_ref, da_ref, db_ref, hm_ref, xn_ref, dy_ref, dg_ref):
        xv = x_ref[...]
        g = g_ref[...]
        u, r = _rms(xv)
        xn_ref[...] = bf(u * g)
        dout = do_ref[...]
        dy = bf(0.5 * dout)
        dy_ref[...] = dy
        dxn = jnp.zeros((tm, d), F32)
        for c in range(nc):
            av = a_ref[c].astype(F32)
            bv = b_ref[c].astype(F32)
            sg = _sigmoid(av)
            sl = av * sg
            dh = dot_nt(dy, wd_ref[c])
            da = bf(dh * bv * _dsilu(av, sg))
            db = bf(dh * sl)
            da_ref[c] = da
            db_ref[c] = db
            hm_ref[c] = bf(sl * bv)
            dxn = dxn + dot_nt(da, wg_ref[c]) + dot_nt(db, wu_ref[c])
        dx, dg = _rms_bwd(dxn, u, r, g)
        dx_ref[...] = dout + dx

        @pl.when(pl.program_id(0) == 0)
        def _():
            dg_ref[...] = jnp.zeros_like(dg_ref)

        dg_ref[...] += dg

    wspec = pl.BlockSpec((nc, None, d, fc), lambda i: (0, l, 0, 0), pipeline_mode=pl.Buffered(1))
    wdspec = pl.BlockSpec((nc, None, fc, d), lambda i: (0, l, 0, 0), pipeline_mode=pl.Buffered(1))
    row = pl.BlockSpec((tm, d), lambda i: (i, 0))
    act = pl.BlockSpec((nc, tm, fc), lambda i: (0, i, 0))
    act_shape = jax.ShapeDtypeStruct((nc, s, fc), BF16)
    return pl.pallas_call(
        body,
        name="ffn_bwd",
        grid=(s // tm,),
        in_specs=[row, pl.BlockSpec((None, 1, d), lambda i: (l, 0, 0)), row, act, act, wspec, wspec, wdspec],
        out_specs=[row, act, act, act, row, row, _const((1, d))],
        out_shape=[
            jax.ShapeDtypeStruct((s, d), F32),
            act_shape,
            act_shape,
            act_shape,
            jax.ShapeDtypeStruct((s, d), BF16),
            jax.ShapeDtypeStruct((s, d), BF16),
            jax.ShapeDtypeStruct((1, d), F32),
        ],
        compiler_params=_cparams(("arbitrary",)),
    )(x, gain3, dout, a, b, wg, wu, wd)


def wgrad(a, b, tn=None, buf=None, l=None):
    ca = a.shape[0] if a.ndim == 3 else 1
    cb = b.shape[0] if b.ndim == 3 else 1
    nc = max(ca, cb)
    s, m = a.shape[-2:]
    n = b.shape[-1]
    tn = n if tn is None else tn
    assert n % tn == 0

    def body(*refs):
        a_ref, b_ref, o_ref = refs[-3:] if buf is None else (refs[0], refs[1], refs[3])
        o_ref[...] = dot_tn(a_ref[...], b_ref[...])

    a_spec = pl.BlockSpec((None, s, m), lambda c, j: (c, 0, 0)) if a.ndim == 3 else pl.BlockSpec((s, m), lambda c, j: (0, 0))
    b_spec = pl.BlockSpec((None, s, tn), lambda c, j: (c, 0, j)) if b.ndim == 3 else pl.BlockSpec((s, tn), lambda c, j: (0, j))
    if buf is None:
        assert nc == 1
        return pl.pallas_call(
            body,
            name="wgrad",
            grid=(1, n // tn),
            in_specs=[a_spec, b_spec],
            out_specs=pl.BlockSpec((m, tn), lambda c, j: (0, j)),
            out_shape=jax.ShapeDtypeStruct((m, n), F32),
            compiler_params=_cparams(("arbitrary", "arbitrary")),
        )(a, b)
    lh = buf.shape[2]
    hi, lo = l // lh, l % lh
    o_spec = pl.BlockSpec((None, None, None, m, tn), lambda c, j: (hi, c, lo, 0, j))
    return pl.pallas_call(
        body,
        name="wgrad_buf",
        grid=(nc, n // tn),
        in_specs=[a_spec, b_spec, ANY],
        out_specs=o_spec,
        out_shape=jax.ShapeDtypeStruct(buf.shape, F32),
        input_output_aliases={2: 0},
        compiler_params=_cparams(("arbitrary", "arbitrary")),
    )(a, b, buf)


def proj_fwd(x, gain3, l, w, wl):
    s, d = x.shape
    n = w.shape[-1]
    tm = _tok(s)

    def body(x_ref, g_ref, w_ref, o_ref):
        u, _ = _rms(x_ref[...])
        o_ref[...] = dot_nn(bf(u * g_ref[...]), w_ref[...])

    return pl.pallas_call(
        body,
        name="proj_fwd",
        grid=(s // tm,),
        in_specs=[
            pl.BlockSpec((tm, d), lambda i: (i, 0)),
            pl.BlockSpec((None, 1, d), lambda i: (l, 0, 0)),
            pl.BlockSpec((None, d, n), lambda i: (wl, 0, 0)),
        ],
        out_specs=pl.BlockSpec((tm, n), lambda i: (i, 0)),
        out_shape=jax.ShapeDtypeStruct((s, n), F32),
        compiler_params=_cparams(("arbitrary",)),
    )(x, gain3, w)


def proj_bwd(x, gain3, l, parts, w, wl, dx_in):
    s, d = x.shape
    n = w.shape[-1]
    widths = [p.shape[1] for p in parts]
    assert sum(widths) == n
    tm = _tok(s)
    npart = len(parts)

    def body(*refs):
        x_ref, g_ref, w_ref, dxin_ref = refs[:4]
        p_refs = refs[4 : 4 + npart]
        dx_ref, dg_ref, xn_ref, dpb_ref = refs[4 + npart :]
        g = g_ref[...]
        u, r = _rms(x_ref[...])
        xn_ref[...] = bf(u * g)
        dxn = jnp.zeros((tm, d), F32)
        off = 0
        for p_ref, wd_ in zip(p_refs, widths):
            dp = bf(p_ref[...])
            dpb_ref[:, off : off + wd_] = dp
            dxn = dxn + dot_nt(dp, w_ref[:, off : off + wd_])
            off += wd_
        dx, dg = _rms_bwd(dxn, u, r, g)
        dx_ref[...] = dxin_ref[...] + dx

        @pl.when(pl.program_id(0) == 0)
        def _():
            dg_ref[...] = jnp.zeros_like(dg_ref)

        dg_ref[...] += dg

    row = pl.BlockSpec((tm, d), lambda i: (i, 0))
    return pl.pallas_call(
        body,
        name="proj_bwd",
        grid=(s // tm,),
        in_specs=[row, pl.BlockSpec((None, 1, d), lambda i: (l, 0, 0)), pl.BlockSpec((None, d, n), lambda i: (wl, 0, 0)), row]
        + [pl.BlockSpec((tm, wd_), lambda i: (i, 0)) for wd_ in widths],
        out_specs=[row, _const((1, d)), row, pl.BlockSpec((tm, n), lambda i: (i, 0))],
        out_shape=[
            jax.ShapeDtypeStruct((s, d), F32),
            jax.ShapeDtypeStruct((1, d), F32),
            jax.ShapeDtypeStruct((s, d), BF16),
            jax.ShapeDtypeStruct((s, n), BF16),
        ],
        compiler_params=_cparams(("arbitrary",)),
    )(x, gain3, w, dx_in, *parts)


def mm_res(x, a, w, l):
    s, d = x.shape
    k = a.shape[1]
    tm = _tok(s)

    def body(x_ref, a_ref, w_ref, o_ref):
        o_ref[...] = x_ref[...] + dot_nn(a_ref[...], w_ref[...])

    return pl.pallas_call(
        body,
        name="mm_res",
        grid=(s // tm,),
        in_specs=[
            pl.BlockSpec((tm, d), lambda i: (i, 0)),
            pl.BlockSpec((tm, k), lambda i: (i, 0)),
            pl.BlockSpec((None, k, d), lambda i: (l, 0, 0)),
        ],
        out_specs=pl.BlockSpec((tm, d), lambda i: (i, 0)),
        out_shape=jax.ShapeDtypeStruct((s, d), F32),
        compiler_params=_cparams(("arbitrary",)),
    )(x, a, w)


def mm_nt(dx, w, l):
    s, d = dx.shape
    k = w.shape[1]
    tm = _tok(s)

    def body(dx_ref, w_ref, o_ref, dxb_ref):
        dxb = bf(dx_ref[...])
        dxb_ref[...] = dxb
        o_ref[...] = dot_nt(dxb, w_ref[...])

    return pl.pallas_call(
        body,
        name="mm_nt",
        grid=(s // tm,),
        in_specs=[pl.BlockSpec((tm, d), lambda i: (i, 0)), pl.BlockSpec((None, k, d), lambda i: (l, 0, 0))],
        out_specs=[pl.BlockSpec((tm, k), lambda i: (i, 0)), pl.BlockSpec((tm, d), lambda i: (i, 0))],
        out_shape=[jax.ShapeDtypeStruct((s, k), F32), jax.ShapeDtypeStruct((s, d), BF16)],
        compiler_params=_cparams(("arbitrary",)),
    )(dx, w)


def lb_fwd(logits3):
    def body(l_ref, o_ref):
        l0, l1 = l_ref[0], l_ref[1]
        m = jnp.maximum(l0, l1)
        e0, e1 = jnp.exp(l0 - m), jnp.exp(l1 - m)
        p0, p1 = e0 / (e0 + e1), e1 / (e0 + e1)
        o_ref[0] = p0 - p0
        o_ref[1] = (p0 + p1) - p0

    return pl.pallas_call(body, name="lb_fwd", out_shape=jax.ShapeDtypeStruct(logits3.shape, F32))(logits3)


def lb_bwd(logits3, dlb1):
    def body(l_ref, d_ref, o_ref):
        l0, l1 = l_ref[0], l_ref[1]
        m = jnp.maximum(l0, l1)
        e0, e1 = jnp.exp(l0 - m), jnp.exp(l1 - m)
        p0, p1 = e0 / (e0 + e1), e1 / (e0 + e1)
        t = d_ref[...] * p0 * p1
        o_ref[0] = -t
        o_ref[1] = t

    return pl.pallas_call(body, name="lb_bwd", out_shape=jax.ShapeDtypeStruct(logits3.shape, F32))(logits3, dlb1)


def _hgrn_gates(zq, zf, lb):
    sf = _sigmoid(zf)
    f = lb + (1.0 - lb) * sf
    sq = _sigmoid(zq)
    return sf, f, jnp.log(f), 1.0 - f, sq, zq * sq


def _tri(n, upper=False):
    r = lax.broadcasted_iota(jnp.int32, (n, n), 0)
    c = lax.broadcasted_iota(jnp.int32, (n, n), 1)
    return jnp.where((c >= r) if upper else (r >= c), 1.0, 0.0).astype(F32)


def hgrn_fwd(proj, lb3, og3, l):
    s = proj.shape[0]
    nh = 6
    n_chunk = s // CHUNK
    nsub = CHUNK // SUB

    def body(zq_ref, zf_ref, vi_ref, zg_ref, lb_ref, og_ref, main_ref, o_ref, q_s, k_s, v_s, c_s):
        lb = lb_ref[...]
        og = og_ref[...]
        tril = _tri(CHUNK)
        rowi = lax.broadcasted_iota(jnp.int32, (SUB, LANES), 0)

        def chunk(ci, st):
            r0 = pl.multiple_of(ci * CHUNK, CHUNK)
            rows = pl.ds(r0, CHUNK)
            zg = zg_ref[rows, :]
            _, _, lf, k, _, q = _hgrn_gates(zq_ref[rows, :], zf_ref[rows, :], lb)
            v = vi_ref[rows, :]
            c = dot_nn(tril, lf, HI)
            q_s[...] = q
            k_s[...] = k
            v_s[...] = v
            c_s[...] = c
            o_inter = dot_nt(q * jnp.exp(c), st, HI)
            parts = []
            for i in range(nsub):
                lo = i * SUB
                blk = pl.ds(lo, SUB)
                qb, kb, cb = q_s[blk, :], k_s[blk, :], c_s[blk, :]
                ob = o_inter[lo : lo + SUB]
                if i > 0:
                    rr = c_s[pl.ds(lo - 1, 1), :]
                    qt = qb * jnp.exp(cb - rr)
                    kt = k_s[pl.ds(0, lo), :] * jnp.exp(rr - c_s[pl.ds(0, lo), :])
                    ob = ob + dot_nn(dot_nt(qt, kt, HI), v_s[pl.ds(0, lo), :], HI)
                for t in range(SUB):
                    e = jnp.where(rowi >= t, jnp.exp(cb - c_s[pl.ds(lo + t, 1), :]), 0.0)
                    a = jnp.sum(qb * k_s[pl.ds(lo + t, 1), :] * e, axis=-1, keepdims=True)
                    ob = ob + a * v_s[pl.ds(lo + t, 1), :]
                parts.append(ob)
            o = jnp.concatenate(parts, axis=0)
            ce = c_s[pl.ds(CHUNK - 1, 1), :]
            st = st * jnp.exp(ce) + dot_tn(v, k * jnp.exp(ce - c), HI)
            on, _ = _rms(o)
            o_ref[rows, :] = o
            main_ref[rows, :] = bf(on * og * (zg * _sigmoid(zg)))
            return st

        lax.fori_loop(0, n_chunk, chunk, jnp.zeros((LANES, LANES), F32))

    def col(k):
        return pl.BlockSpec((s, LANES), lambda h: (0, k * nh + h))

    vec = pl.BlockSpec((None, 1, LANES), lambda h: (l, 0, h))
    return pl.pallas_call(
        body,
        name="hgrn_fwd",
        grid=(nh,),
        in_specs=[col(0), col(1), col(2), col(3), vec, pl.BlockSpec((None, 1, LANES), lambda h: (l, 0, 0))],
        out_specs=[pl.BlockSpec((s, LANES), lambda h: (0, h))] * 2,
        out_shape=[jax.ShapeDtypeStruct((s, nh * LANES), BF16), jax.ShapeDtypeStruct((s, nh * LANES), F32)],
        scratch_shapes=[pltpu.VMEM((CHUNK, LANES), F32)] * 4,
        compiler_params=_cparams(("arbitrary",)),
    )(proj, proj, proj, proj, lb3, og3)


def hgrn_bwd(proj, lb3, og3, l, o, dmixed):
    s = proj.shape[0]
    nh = 6
    n_chunk = s // CHUNK
    nsub = CHUNK // SUB

    def body(zq_ref, zf_ref, vi_ref, zg_ref, lb_ref, og_ref, o_ref, dm_ref,
             dzq_ref, dzf_ref, dvi_ref, dzg_ref, dlb_ref, dog_ref,
             st_s, q_s, k_s, v_s, c_s, do_s, dq_s, dk_s, dv_s, acc_s):
        lb = lb_ref[...]
        og = og_ref[...]
        tril = _tri(CHUNK)
        triu = _tri(CHUNK, upper=True)
        rowi = lax.broadcasted_iota(jnp.int32, (SUB, LANES), 0)

        def fwd_chunk(ci, st):
            rows = pl.ds(pl.multiple_of(ci * CHUNK, CHUNK), CHUNK)
            _, _, lf, k, _, _ = _hgrn_gates(zq_ref[rows, :], zf_ref[rows, :], lb)
            c = dot_nn(tril, lf, HI)
            ce = jnp.sum(lf, axis=0, keepdims=True)
            st_s[ci] = st
            return st * jnp.exp(ce) + dot_tn(vi_ref[rows, :], k * jnp.exp(ce - c), HI)

        lax.fori_loop(0, n_chunk, fwd_chunk, jnp.zeros((LANES, LANES), F32))
        acc_s[...] = jnp.zeros_like(acc_s)

        def bwd_chunk(jj, carry):
            dst, cg = carry
            ci = n_chunk - 1 - jj
            rows = pl.ds(pl.multiple_of(ci * CHUNK, CHUNK), CHUNK)
            zq, zf, zg = zq_ref[rows, :], zf_ref[rows, :], zg_ref[rows, :]
            sf, f, lf, k, sq, q = _hgrn_gates(zq, zf, lb)
            v = vi_ref[rows, :]
            c = dot_nn(tril, lf, HI)
            st = st_s[ci]
            on, r = _rms(o_ref[rows, :])
            sg = _sigmoid(zg)
            dmain = dm_ref[rows, :]
            dy = dmain * (zg * sg)
            dzg_ref[rows, :] = dmain * (on * og) * _dsilu(zg, sg)
            do, dog = _rms_bwd(dy, on, r, og)
            acc_s[pl.ds(0, 1), :] += dog
            q_s[...] = q
            k_s[...] = k
            v_s[...] = v
            c_s[...] = c
            do_s[...] = do
            ce = c_s[pl.ds(CHUNK - 1, 1), :]
            eq = jnp.exp(c)
            ek = jnp.exp(ce - c)
            qt_all = q * eq
            dq_s[...] = dot_nn(do, st, HI) * eq
            dv_s[...] = dot_nt(k * ek, dst, HI)
            dk_s[...] = dot_nn(v, dst, HI) * ek
            dst = dst * jnp.exp(ce) + dot_tn(do, qt_all, HI)
            for i in range(nsub):
                lo = i * SUB
                blk = pl.ds(lo, SUB)
                qb, cb, dob = q_s[blk, :], c_s[blk, :], do_s[blk, :]
                if i > 0:
                    prev = pl.ds(0, lo)
                    rr = c_s[pl.ds(lo - 1, 1), :]
                    eqi = jnp.exp(cb - rr)
                    eki = jnp.exp(rr - c_s[prev, :])
                    qt = qb * eqi
                    kt = k_s[prev, :] * eki
                    amat = dot_nt(qt, kt, HI)
                    damat = dot_nt(dob, v_s[prev, :], HI)
                    dv_s[prev, :] += dot_tn(amat, dob, HI)
                    dq_s[blk, :] += dot_nn(damat, kt, HI) * eqi
                    dk_s[prev, :] += dot_tn(damat, qt, HI) * eki
                dqb = jnp.zeros((SUB, LANES), F32)
                for t in range(SUB):
                    row = pl.ds(lo + t, 1)
                    e = jnp.where(rowi >= t, jnp.exp(cb - c_s[row, :]), 0.0)
                    kr = k_s[row, :]
                    a = jnp.sum(qb * kr * e, axis=-1, keepdims=True)
                    da = jnp.sum(dob * v_s[row, :], axis=-1, keepdims=True)
                    dv_s[row, :] += jnp.sum(a * dob, axis=0, keepdims=True)
                    dqb = dqb + da * kr * e
                    dk_s[row, :] += jnp.sum(da * qb * e, axis=0, keepdims=True)
                dq_s[blk, :] += dqb
            dq, dk = dq_s[...], dk_s[...]
            dg = q * dq - k * dk
            dlf = dot_nn(triu, dg, HI) + cg
            cg = cg + jnp.sum(dg, axis=0, keepdims=True)
            df = dlf / f - dk
            dzf_ref[rows, :] = df * (1.0 - lb) * sf * (1.0 - sf)
            acc_s[pl.ds(1, 1), :] += jnp.sum(df * (1.0 - sf), axis=0, keepdims=True)
            dzq_ref[rows, :] = dq * _dsilu(zq, sq)
            dvi_ref[rows, :] = dv_s[...]
            return dst, cg

        lax.fori_loop(0, n_chunk, bwd_chunk, (jnp.zeros((LANES, LANES), F32), jnp.zeros((1, LANES), F32)))
        dlb_ref[...] = acc_s[pl.ds(1, 1), :]

        @pl.when(pl.program_id(0) == 0)
        def _():
            dog_ref[...] = jnp.zeros_like(dog_ref)

        dog_ref[...] += acc_s[pl.ds(0, 1), :]

    def col(k):
        return pl.BlockSpec((s, LANES), lambda h: (0, k * nh + h))

    head = pl.BlockSpec((s, LANES), lambda h: (0, h))
    vec = pl.BlockSpec((None, 1, LANES), lambda h: (l, 0, h))
    ck = pltpu.VMEM((CHUNK, LANES), F32)
    return pl.pallas_call(
        body,
        name="hgrn_bwd",
        grid=(nh,),
        in_specs=[col(0), col(1), col(2), col(3), vec, pl.BlockSpec((None, 1, LANES), lambda h: (l, 0, 0)), head, head],
        out_specs=[head] * 4 + [pl.BlockSpec((1, LANES), lambda h: (0, h)), _const((1, LANES))],
        out_shape=[jax.ShapeDtypeStruct((s, nh * LANES), F32)] * 4
        + [jax.ShapeDtypeStruct((1, nh * LANES), F32), jax.ShapeDtypeStruct((1, LANES), F32)],
        scratch_shapes=[pltpu.VMEM((n_chunk, LANES, LANES), F32)] + [ck] * 8 + [pltpu.VMEM((8, LANES), F32)],
        compiler_params=_cparams(("arbitrary",)),
    )(proj, proj, proj, proj, lb3, og3, o, dmixed)


MEM_SCALE = HEAD64**-0.5


def _mem_heads(qraw, kvm, qg, kg, pr, m0):
    lo = pr * LANES
    uq, rq = _rms64(qraw[:, lo : lo + LANES], m0)
    uk, rk = _rms64(kvm[:, lo : lo + LANES], m0)
    v = bf(kvm[:, 2 * LANES + lo : 3 * LANES + lo])
    return uq, rq, uk, rk, v, uq * qg, bf(uk * kg)


def memattn_fwd(proj, qblk, kvm, qg3, kg3, l):
    s = proj.shape[0]
    nm = kvm.shape[0]
    tm = _tok(s)

    def body(q_ref, kv_ref, qg_ref, kg_ref, o_ref):
        m0 = _lane_mask0((1, LANES))
        qraw, kvv = q_ref[...], kv_ref[...]
        for pr in range(2):
            _, _, _, _, v, qn, kn = _mem_heads(qraw, kvv, qg_ref[...], kg_ref[...], pr, m0)
            out = jnp.zeros((tm, LANES), F32)
            for hh in range(2):
                mh = m0 if hh == 0 else jnp.logical_not(m0)
                sc = dot_nt(bf(jnp.where(mh, qn, 0.0)), kn) * MEM_SCALE
                p = jnp.exp(sc - jnp.max(sc, axis=-1, keepdims=True))
                p = p / jnp.sum(p, axis=-1, keepdims=True)
                out = jnp.where(mh, dot_nn(bf(p), v), out)
            o_ref[:, pr * LANES : (pr + 1) * LANES] = bf(out)

    gspec = pl.BlockSpec((None, 1, LANES), lambda i: (l, 0, 0))
    return pl.pallas_call(
        body,
        name="memattn_fwd",
        grid=(s // tm,),
        in_specs=[pl.BlockSpec((tm, 2 * LANES), lambda i: (i, qblk)), _const((nm, 4 * LANES)), gspec, gspec],
        out_specs=pl.BlockSpec((tm, 2 * LANES), lambda i: (i, 0)),
        out_shape=jax.ShapeDtypeStruct((s, 2 * LANES), BF16),
        compiler_params=_cparams(("arbitrary",)),
    )(proj, kvm, qg3, kg3)


def memattn_bwd(proj, qblk, kvm, qg3, kg3, l, dmixed):
    s = proj.shape[0]
    nm = kvm.shape[0]
    tm = _tok(s)

    def body(q_ref, kv_ref, qg_ref, kg_ref, dm_ref, dq_ref, dkv_ref, dqg_ref, dkg_ref):
        m0 = _lane_mask0((1, LANES))
        qraw, kvv = q_ref[...], kv_ref[...]
        qg, kg = qg_ref[...], kg_ref[...]

        @pl.when(pl.program_id(0) == 0)
        def _():
            dkv_ref[...] = jnp.zeros_like(dkv_ref)
            dqg_ref[...] = jnp.zeros_like(dqg_ref)
            dkg_ref[...] = jnp.zeros_like(dkg_ref)

        for pr in range(2):
            lo = pr * LANES
            uq, rq, uk, rk, v, qn, kn = _mem_heads(qraw, kvv, qg, kg, pr, m0)
            do = dm_ref[:, lo : lo + LANES]
            dqn = jnp.zeros((tm, LANES), F32)
            dkn = jnp.zeros((nm, LANES), F32)
            dv = jnp.zeros((nm, LANES), F32)
            for hh in range(2):
                mh = m0 if hh == 0 else jnp.logical_not(m0)
                qh = bf(jnp.where(mh, qn, 0.0))
                doh = bf(jnp.where(mh, do, 0.0))
                sc = dot_nt(qh, kn) * MEM_SCALE
                p = jnp.exp(sc - jnp.max(sc, axis=-1, keepdims=True))
                p = p / jnp.sum(p, axis=-1, keepdims=True)
                dp = dot_nt(doh, v)
                ds = bf(p * (dp - jnp.sum(p * dp, axis=-1, keepdims=True)))
                dqn = dqn + jnp.where(mh, dot_nn(ds, kn), 0.0) * MEM_SCALE
                dkn = dkn + dot_tn(ds, qh) * MEM_SCALE
                dv = dv + dot_tn(bf(p), doh)
            dqr, dqg = _rms64_bwd(dqn, uq, rq, qg, m0)
            dkr, dkg = _rms64_bwd(dkn, uk, rk, kg, m0)
            dq_ref[:, lo : lo + LANES] = dqr
            dkv_ref[:, lo : lo + LANES] += dkr
            dkv_ref[:, 2 * LANES + lo : 3 * LANES + lo] += dv
            dqg_ref[...] += dqg
            dkg_ref[...] += dkg

    gspec = pl.BlockSpec((None, 1, LANES), lambda i: (l, 0, 0))
    return pl.pallas_call(
        body,
        name="memattn_bwd",
        grid=(s // tm,),
        in_specs=[
            pl.BlockSpec((tm, 2 * LANES), lambda i: (i, qblk)),
            _const((nm, 4 * LANES)),
            gspec,
            gspec,
            pl.BlockSpec((tm, 2 * LANES), lambda i: (i, 3)),
        ],
        out_specs=[pl.BlockSpec((tm, 2 * LANES), lambda i: (i, 0)), _const((nm, 4 * LANES)), _const((1, LANES)), _const((1, LANES))],
        out_shape=[
            jax.ShapeDtypeStruct((s, 2 * LANES), F32),
            jax.ShapeDtypeStruct((nm, 4 * LANES), F32),
            jax.ShapeDtypeStruct((1, LANES), F32),
            jax.ShapeDtypeStruct((1, LANES), F32),
        ],
        compiler_params=_cparams(("arbitrary",)),
    )(proj, kvm, qg3, kg3, dmixed)


KV_MAIN = 768


def _log_sigmoid(z):
    return jnp.minimum(z, 0.0) - jnp.log(1.0 + jnp.exp(-jnp.abs(z)))


def kvprep_fwd(kvf, kg, fb):
    s = kvf.shape[0]
    tm = _tok(s)

    def body(kvf_ref, kg_ref, fb_ref, k_ref, v_ref, clf_ref, carry):
        m0 = _lane_mask0((1, LANES))

        @pl.when(pl.program_id(0) == 0)
        def _():
            carry[...] = jnp.zeros_like(carry)

        for j in range(KV_MAIN // LANES):
            u, _ = _rms64(kvf_ref[:, j * LANES : (j + 1) * LANES], m0)
            k_ref[:, j * LANES : (j + 1) * LANES] = bf(u * kg_ref[...])
        v_ref[...] = bf(kvf_ref[:, KV_MAIN : 2 * KV_MAIN])
        lf = _log_sigmoid(kvf_ref[:, 2 * KV_MAIN :] + fb_ref[...])
        clf_ref[...] = dot_nn(_tri(tm), lf, HI) + carry[...]
        carry[...] += jnp.sum(lf, axis=0, keepdims=True)

    n = kvf.shape[1]
    return pl.pallas_call(
        body,
        name="kvprep_fwd",
        grid=(s // tm,),
        in_specs=[pl.BlockSpec((tm, n), lambda i: (i, 0)), _const((1, LANES)), _const((1, LANES))],
        out_specs=[pl.BlockSpec((tm, KV_MAIN), lambda i: (i, 0))] * 2 + [pl.BlockSpec((tm, LANES), lambda i: (i, 0))],
        out_shape=[jax.ShapeDtypeStruct((s, KV_MAIN), BF16)] * 2 + [jax.ShapeDtypeStruct((s, LANES), F32)],
        scratch_shapes=[pltpu.VMEM((1, LANES), F32)],
        compiler_params=_cparams(("arbitrary",)),
    )(kvf, kg, fb)


def kvprep_bwd(kvf, kg, fb, dk, dv, dclf):
    s, n = kvf.shape
    tm = _tok(s)
    nb = s // tm

    def body(kvf_ref, kg_ref, fb_ref, dk_ref, dv_ref, dclf_ref, o_ref, dkg_ref, dfb_ref, carry):
        m0 = _lane_mask0((1, LANES))

        @pl.when(pl.program_id(0) == 0)
        def _():
            carry[...] = jnp.zeros_like(carry)
            dkg_ref[...] = jnp.zeros_like(dkg_ref)
            dfb_ref[...] = jnp.zeros_like(dfb_ref)

        kg_ = kg_ref[...]
        for j in range(KV_MAIN // LANES):
            cols = slice(j * LANES, (j + 1) * LANES)
            u, r = _rms64(kvf_ref[:, cols], m0)
            dkr, dkg = _rms64_bwd(dk_ref[:, cols], u, r, kg_, m0)
            o_ref[:, cols] = dkr
            dkg_ref[...] += dkg
        o_ref[:, KV_MAIN : 2 * KV_MAIN] = dv_ref[...]
        z = kvf_ref[:, 2 * KV_MAIN :] + fb_ref[...]
        dc = dclf_ref[...]
        dlf = dot_nn(_tri(tm, upper=True), dc, HI) + carry[...]
        carry[...] += jnp.sum(dc, axis=0, keepdims=True)
        dz = dlf * _sigmoid(-z)
        o_ref[:, 2 * KV_MAIN :] = dz
        dfb_ref[...] += jnp.sum(dz, axis=0, keepdims=True)

    rev = lambda i: (nb - 1 - i, 0)
    return pl.pallas_call(
        body,
        name="kvprep_bwd",
        grid=(nb,),
        in_specs=[pl.BlockSpec((tm, n), rev), _const((1, LANES)), _const((1, LANES)), pl.BlockSpec((tm, KV_MAIN), rev),
                  pl.BlockSpec((tm, KV_MAIN), rev), pl.BlockSpec((tm, LANES), rev)],
        out_specs=[pl.BlockSpec((tm, n), rev), _const((1, LANES)), _const((1, LANES))],
        out_shape=[jax.ShapeDtypeStruct((s, n), F32), jax.ShapeDtypeStruct((1, LANES), F32), jax.ShapeDtypeStruct((1, LANES), F32)],
        scratch_shapes=[pltpu.VMEM((1, LANES), F32)],
        compiler_params=_cparams(("arbitrary",)),
    )(kvf, kg, fb, dk, dv, dclf)


FOX_SCALE = HEAD64**-0.5


def _lane_col(block, lane_idx, h):
    return jnp.sum(jnp.where(lane_idx == h, block, 0.0), axis=-1, keepdims=True)


def _causal(tq, ext, i, transposed=False):
    if transposed:
        key = lax.broadcasted_iota(jnp.int32, (ext, tq), 0)
        qry = lax.broadcasted_iota(jnp.int32, (ext, tq), 1) + i * tq
    else:
        qry = lax.broadcasted_iota(jnp.int32, (tq, ext), 0) + i * tq
        key = lax.broadcasted_iota(jnp.int32, (tq, ext), 1)
    return key <= qry


def fox_fwd(proj, k_sh, v_sh, clf, clf_t, qg3, j_layer):
    s = proj.shape[0]
    npair = 6
    tq = TQ if s % TQ == 0 else s
    nq = s // tq

    def body(q_ref, gate_ref, k_ref, v_ref, clf_ref, clft_ref, qg_ref, main_ref, o_ref, lse_ref):
        j = pl.program_id(0)
        lane = lax.broadcasted_iota(jnp.int32, (1, LANES), 1)
        m0 = lane < HEAD64
        u, _ = _rms64(q_ref[...], m0)
        qn = u * qg_ref[...] * FOX_SCALE
        clfv = clf_ref[...]
        for hh in range(2):
            h = 2 * j + hh
            mh = m0 if hh == 0 else jnp.logical_not(m0)
            qh = bf(jnp.where(mh, qn, 0.0))
            dcol = _lane_col(clfv, lane, h)
            drow = clft_ref[pl.ds(h, 1), :]
            for i in range(nq):
                rows = slice(i * tq, (i + 1) * tq)
                ext = (i + 1) * tq
                sc = dot_nt(qh[rows], k_ref[0:ext, :]) + dcol[rows] - drow[:, :ext]
                sc = jnp.where(_causal(tq, ext, i), sc, -jnp.inf)
                m = jnp.max(sc, axis=-1, keepdims=True)
                p = jnp.exp(sc - m)
                lsum = jnp.sum(p, axis=-1, keepdims=True)
                pv = dot_nn(bf(p), v_ref[0:ext, :]) / lsum
                lse = m + jnp.log(lsum)
                if hh == 0:
                    o_ref[rows, :] = pv
                    lse_ref[rows, :] = jnp.where(lane == 0, lse, 0.0)
                else:
                    o_ref[rows, :] = jnp.where(mh, pv, o_ref[rows, :])
                    lse_ref[rows, :] = jnp.where(lane == 1, lse, lse_ref[rows, :])
        main_ref[...] = bf(o_ref[...] * _sigmoid(gate_ref[...]))

    blk = lambda off: pl.BlockSpec((s, LANES), lambda j: (0, off + j))
    return pl.pallas_call(
        body,
        name="fox_fwd",
        grid=(npair,),
        in_specs=[blk(0), blk(npair), blk(0), blk(0), _const((s, LANES)), _const((16, s)),
                  pl.BlockSpec((None, 1, LANES), lambda j: (j_layer, 0, 0))],
        out_specs=[blk(0)] * 3,
        out_shape=[jax.ShapeDtypeStruct((s, npair * LANES), BF16)] + [jax.ShapeDtypeStruct((s, npair * LANES), F32)] * 2,
        compiler_params=_cparams(("arbitrary",)),
    )(proj, proj, k_sh, v_sh, clf, clf_t, qg3)


def fox_bwd(proj, k_sh, v_sh, clf, clf_t, qg3, j_layer, o, lse, lse_t, dmixed, dk_in, dv_in, dclf_in):
    s = proj.shape[0]
    npair = 6
    tq = TQ if s % TQ == 0 else s
    nq = s // tq

    def body(q_ref, gate_ref, k_ref, v_ref, clf_ref, clft_ref, qg_ref, o_ref, lse_ref, lset_ref, dm_ref, dkin_ref, dvin_ref, dclfin_ref,
             dq_ref, dgate_ref, dk_ref, dv_ref, dclf_ref, dqg_ref, dqn_s, dcl_s):
        j = pl.program_id(0)
        lane = lax.broadcasted_iota(jnp.int32, (1, LANES), 1)
        m0 = lane < HEAD64
        qg = qg_ref[...]
        u, r = _rms64(q_ref[...], m0)
        qn = u * qg * FOX_SCALE
        ov = o_ref[...]
        gate = gate_ref[...]
        sg = _sigmoid(gate)
        dmain = dm_ref[...]
        do = dmain * sg
        dgate_ref[...] = dmain * ov * sg * (1.0 - sg)
        dk_ref[...] = dkin_ref[...]
        dv_ref[...] = dvin_ref[...]
        clfv = clf_ref[...]
        lsev = lse_ref[...]
        ones8 = jnp.ones((8, LANES), F32)

        @pl.when(j == 0)
        def _():
            dclf_ref[...] = dclfin_ref[...]
            dqg_ref[...] = jnp.zeros_like(dqg_ref)

        for hh in range(2):
            h = 2 * j + hh
            mh = m0 if hh == 0 else jnp.logical_not(m0)
            qh = bf(jnp.where(mh, qn, 0.0))
            doh = jnp.where(mh, do, 0.0)
            dohb = bf(doh)
            doo = doh * ov
            dcol = _lane_col(clfv, lane, h)
            drow = clft_ref[pl.ds(h, 1), :]
            lcol = _lane_col(lsev, lane, hh)
            lrow = lset_ref[pl.ds(h, 1), :]
            delta = jnp.sum(doo, axis=-1, keepdims=True)
            dcl_s[...] = jnp.zeros_like(dcl_s)
            for i in range(nq):
                rows = slice(i * tq, (i + 1) * tq)
                ext = (i + 1) * tq
                kk, vv = k_ref[0:ext, :], v_ref[0:ext, :]
                sc = dot_nt(qh[rows], kk) + dcol[rows] - drow[:, :ext]
                p = jnp.where(_causal(tq, ext, i), jnp.exp(sc - lcol[rows]), 0.0)
                ds = p * (dot_nt(dohb[rows], vv) - delta[rows])
                dqh = dot_nn(bf(ds), kk) * FOX_SCALE
                if hh == 0:
                    dqn_s[rows, :] = dqh
                else:
                    dqn_s[rows, :] = jnp.where(mh, dqh, dqn_s[rows, :])
                dcl_s[rows, :] += jnp.sum(ds, axis=-1, keepdims=True)
                sct = dot_nt(kk, qh[rows]) + drow[:, rows] - dcol[:ext]
                pt = jnp.where(_causal(tq, ext, i, transposed=True), jnp.exp(sct - lrow[:, rows]), 0.0)
                delta_row = dot_nt(ones8, doo[rows], HI)[0:1]
                dst = pt * (dot_nt(vv, dohb[rows]) - delta_row)
                dv_ref[0:ext, :] += dot_nn(bf(pt), dohb[rows])
                dk_ref[0:ext, :] += dot_nn(bf(dst), qh[rows])
                dcl_s[0:ext, :] -= jnp.sum(dst, axis=-1, keepdims=True)
            dclf_ref[...] += jnp.where(lane == h, dcl_s[...], 0.0)
        dqr, dqg = _rms64_bwd(dqn_s[...], u, r, qg, m0)
        dq_ref[...] = dqr
        dqg_ref[...] += dqg

    blk = lambda off: pl.BlockSpec((s, LANES), lambda j: (0, off + j))
    full = _const((s, LANES))
    return pl.pallas_call(
        body,
        name="fox_bwd",
        grid=(npair,),
        in_specs=[blk(0), blk(npair), blk(0), blk(0), full, _const((16, s)), pl.BlockSpec((None, 1, LANES), lambda j: (j_layer, 0, 0)),
                  blk(0), blk(0), _const((16, s)), blk(0), blk(0), blk(0), full],
        out_specs=[blk(0)] * 4 + [full, _const((1, LANES))],
        out_shape=[jax.ShapeDtypeStruct((s, npair * LANES), F32)] * 4
        + [jax.ShapeDtypeStruct((s, LANES), F32), jax.ShapeDtypeStruct((1, LANES), F32)],
        scratch_shapes=[pltpu.VMEM((s, LANES), F32), pltpu.VMEM((s, LANES), F32)],
        compiler_params=_cparams(("arbitrary",)),
    )(proj, proj, k_sh, v_sh, clf, clf_t, qg3, o, lse, lse_t, dmixed, dk_in, dv_in, dclf_in)


def loss_head(y, target):
    s, d = y.shape
    tm = _tok(s)

    def body(y_ref, t_ref, loss_ref, dy_ref):
        err = y_ref[...] - t_ref[...]
        dy_ref[...] = err * (1.0 / d)

        @pl.when(pl.program_id(0) == 0)
        def _():
            loss_ref[...] = jnp.zeros_like(loss_ref)

        part = jnp.sum(jnp.mean(err * err, axis=-1, keepdims=True), axis=0, keepdims=True)
        loss_ref[...] += 0.5 * part

    row = pl.BlockSpec((tm, d), lambda i: (i, 0))
    return pl.pallas_call(
        body,
        name="loss_head",
        grid=(s // tm,),
        in_specs=[row, row],
        out_specs=[_const((1, 1)), row],
        out_shape=[jax.ShapeDtypeStruct((1, 1), F32), jax.ShapeDtypeStruct((s, d), F32)],
        compiler_params=_cparams(("arbitrary",)),
    )(y, target)


def _row_tile(r, c, n_arrays):
    budget = VMEM_LIMIT_BYTES // 2
    padded_c = -(-c // LANES) * LANES
    best = None
    for t in range(8, r + 1, 8):
        if r % t == 0 and 2 * n_arrays * t * padded_c * 4 <= budget:
            best = t
    return r if best is None else best


def _as2d(a):
    return a.reshape(-1, a.shape[-1]) if a.ndim >= 2 else a.reshape(1, -1)


def adamw(w, g, m, v):
    shape = w.shape
    w2, g2, m2, v2 = (_as2d(t) for t in (w, g, m, v))
    r, c = w2.shape
    tr = _row_tile(r, c, 7)
    c1 = 1.0 - ADAM_B1**ADAM_STEP
    c2 = 1.0 - ADAM_B2**ADAM_STEP

    def body(w_ref, g_ref, m_ref, v_ref, d_ref, nm_ref, nv_ref):
        gv = g_ref[...]
        nm = ADAM_B1 * m_ref[...] + (1.0 - ADAM_B1) * gv
        nv = ADAM_B2 * v_ref[...] + (1.0 - ADAM_B2) * (gv * gv)
        nm_ref[...] = nm
        nv_ref[...] = nv
        d_ref[...] = -ADAM_LR * ((nm / c1) / (jnp.sqrt(nv / c2) + ADAM_EPS) + ADAM_WD * w_ref[...])

    spec = pl.BlockSpec((tr, c), lambda i: (i, 0))
    outs = pl.pallas_call(
        body,
        name="adamw",
        grid=(r // tr,),
        in_specs=[spec] * 4,
        out_specs=[spec] * 3,
        out_shape=[jax.ShapeDtypeStruct((r, c), F32)] * 3,
        compiler_params=_cparams(("arbitrary",)),
    )(w2, g2, m2, v2)
    return tuple(t.reshape(shape) for t in outs)


def pair_sum(g, recv, c_arr):
    _, k, r, c = g.shape
    tr = _row_tile(r, c, 3)

    def body(c_ref, g_ref, r_ref, o_ref):
        o_ref[...] = bf(g_ref[...] + r_ref[...])

    return pl.pallas_call(
        body,
        name="pair_sum",
        grid_spec=pltpu.PrefetchScalarGridSpec(
            num_scalar_prefetch=1,
            grid=(k, r // tr),
            in_specs=[pl.BlockSpec((None, None, tr, c), lambda kk, i, cr: (cr[0], kk, i, 0)), pl.BlockSpec((None, tr, c), lambda kk, i, cr: (kk, i, 0))],
            out_specs=pl.BlockSpec((None, tr, c), lambda kk, i, cr: (kk, i, 0)),
        ),
        out_shape=jax.ShapeDtypeStruct((k, r, c), BF16),
        compiler_params=_cparams(("arbitrary", "arbitrary")),
    )(c_arr, g, recv)


def chip_sum(p, q, sel):
    _, r, c = p.shape
    tr = _row_tile(r, c, 4)

    def body(sel_ref, p_ref, q_ref, o_ref):
        acc = p_ref[...].astype(F32)
        for i in range(q.shape[0]):
            acc = acc + q_ref[i].astype(F32)
        o_ref[...] = acc

    return pl.pallas_call(
        body,
        name="chip_sum",
        grid_spec=pltpu.PrefetchScalarGridSpec(
            num_scalar_prefetch=1,
            grid=(r // tr,),
            in_specs=[pl.BlockSpec((None, tr, c), lambda i, sr: (sr[0], i, 0)), pl.BlockSpec((q.shape[0], tr, c), lambda i, sr: (0, i, 0))],
            out_specs=pl.BlockSpec((None, tr, c), lambda i, sr: (sr[1], i, 0)),
        ),
        out_shape=jax.ShapeDtypeStruct((2, r, c), F32),
        compiler_params=_cparams(("arbitrary",)),
    )(sel, p, q)


def cast_into_slot(w3, sel, dtype):
    _, r, c = w3.shape
    tr = _row_tile(r, c, 2)

    def body(sel_ref, w_ref, o_ref):
        o_ref[...] = w_ref[...].astype(dtype)

    return pl.pallas_call(
        body,
        name="cast_into_slot",
        grid_spec=pltpu.PrefetchScalarGridSpec(
            num_scalar_prefetch=1,
            grid=(2, r // tr),
            in_specs=[pl.BlockSpec((None, tr, c), lambda hf, i, sr: (hf, i, 0))],
            out_specs=pl.BlockSpec((None, None, tr, c), lambda hf, i, sr: (sr[0], hf, i, 0)),
        ),
        out_shape=jax.ShapeDtypeStruct((N_CHIPS, 2, r, c), dtype),
        compiler_params=_cparams(("arbitrary", "arbitrary")),
    )(sel, w3)


def _place():
    x, y, c = lax.axis_index("x"), lax.axis_index("y"), lax.axis_index("c")
    chips = [(1 - x, y), (x, 1 - y), (1 - x, 1 - y)]
    return x, y, c, 2 * x + y, chips, [2 * cx + cy for cx, cy in chips]


def _rcopy(src, dst, send, recv, dev):
    return pltpu.make_async_remote_copy(src_ref=src, dst_ref=dst, send_sem=send, recv_sem=recv, device_id=dev, device_id_type=MESH)


def all_gather_chips(bufs):
    n = len(bufs)

    def body(*refs):
        outs = refs[n : 2 * n]
        send, recv = refs[2 * n :]
        x, y, c, me, chips, cidx = _place()
        sib = (x, y, 1 - c)
        sends = []
        for a in range(n):
            mine = outs[a].at[me, c]
            sends += [_rcopy(mine, mine, send.at[a, j], recv.at[a, j], (*chips[j], c)) for j in range(3)]
        for cp in sends:
            cp.start()
        passed = []
        for a in range(n):
            for j in range(3):
                landed = outs[a].at[cidx[j], c]
                _rcopy(landed, landed, send.at[a, j], recv.at[a, j], (*chips[j], c)).wait_recv()
                fwd = _rcopy(landed, landed, send.at[a, 3 + j], recv.at[a, 3 + j], sib)
                fwd.start()
                passed.append(fwd)
        for a in range(n):
            for j in range(3):
                theirs = outs[a].at[cidx[j], 1 - c]
                _rcopy(theirs, theirs, send.at[a, 3 + j], recv.at[a, 3 + j], sib).wait_recv()
        for cp in sends + passed:
            cp.wait_send()

    return pl.pallas_call(
        body,
        name="all_gather_chips",
        in_specs=[ANY] * n,
        out_specs=[ANY] * n,
        out_shape=[jax.ShapeDtypeStruct(t.shape, t.dtype) for t in bufs],
        input_output_aliases={a: a for a in range(n)},
        scratch_shapes=[pltpu.SemaphoreType.DMA((n, 6)), pltpu.SemaphoreType.DMA((n, 6))],
    )(*bufs)


def pair_exchange(gs):
    n = len(gs)

    def body(*refs):
        ins, outs = refs[:n], refs[n : 2 * n]
        send, recv = refs[2 * n :]
        x, y, c = lax.axis_index("x"), lax.axis_index("y"), lax.axis_index("c")
        cps = [_rcopy(ins[a].at[1 - c], outs[a], send.at[a], recv.at[a], (x, y, 1 - c)) for a in range(n)]
        for cp in cps:
            cp.start()
        for cp in cps:
            cp.wait()

    return pl.pallas_call(
        body,
        name="pair_exchange",
        in_specs=[ANY] * n,
        out_specs=[ANY] * n,
        out_shape=[jax.ShapeDtypeStruct(t.shape[1:], t.dtype) for t in gs],
        scratch_shapes=[pltpu.SemaphoreType.DMA((n,)), pltpu.SemaphoreType.DMA((n,))],
    )(*gs)


def chip_exchange(ps):
    n = len(ps)

    def body(*refs):
        ins, outs = refs[:n], refs[n : 2 * n]
        send, recv = refs[2 * n :]
        x, y, c, me, chips, cidx = _place()
        cps = [
            _rcopy(ins[a].at[cidx[j]], outs[a].at[j], send.at[a, j], recv.at[a, j], (*chips[j], c))
            for a in range(n)
            for j in range(3)
        ]
        for cp in cps:
            cp.start()
        for a in range(n):
            for j in range(3):
                landed = outs[a].at[j]
                _rcopy(landed, landed, send.at[a, j], recv.at[a, j], (*chips[j], c)).wait_recv()
        for cp in cps:
            cp.wait_send()

    return pl.pallas_call(
        body,
        name="chip_exchange",
        in_specs=[ANY] * n,
        out_specs=[ANY] * n,
        out_shape=[jax.ShapeDtypeStruct((3,) + t.shape[1:], t.dtype) for t in ps],
        scratch_shapes=[pltpu.SemaphoreType.DMA((n, 3)), pltpu.SemaphoreType.DMA((n, 3))],
    )(*ps)


def pair_share(bufs):
    n = len(bufs)

    def body(*refs):
        outs = refs[n : 2 * n]
        send, recv = refs[2 * n :]
        x, y, c = lax.axis_index("x"), lax.axis_index("y"), lax.axis_index("c")
        cps = [_rcopy(outs[a].at[c], outs[a].at[c], send.at[a], recv.at[a], (x, y, 1 - c)) for a in range(n)]
        for cp in cps:
            cp.start()
        for a in range(n):
            theirs = outs[a].at[1 - c]
            _rcopy(theirs, theirs, send.at[a], recv.at[a], (x, y, 1 - c)).wait_recv()
        for cp in cps:
            cp.wait_send()

    return pl.pallas_call(
        body,
        name="pair_share",
        in_specs=[ANY] * n,
        out_specs=[ANY] * n,
        out_shape=[jax.ShapeDtypeStruct(t.shape, t.dtype) for t in bufs],
        input_output_aliases={a: a for a in range(n)},
        scratch_shapes=[pltpu.SemaphoreType.DMA((n,)), pltpu.SemaphoreType.DMA((n,))],
    )(*bufs)


def small_allreduce(buf):
    r = buf.shape[0]

    def body(b_ref, o_ref, slots, send, recv):
        x, y, c = lax.axis_index("x"), lax.axis_index("y"), lax.axis_index("c")
        me = 4 * x + 2 * y + c
        slots[me] = b_ref[...]
        cps = []
        peers = []
        for mask in range(1, N_DEV):
            fx, fy, fc = (mask >> 2) & 1, (mask >> 1) & 1, mask & 1
            px, py, pc = (1 - x if fx else x), (1 - y if fy else y), (1 - c if fc else c)
            peers.append(4 * px + 2 * py + pc)
            cps.append(_rcopy(b_ref, slots.at[me], send.at[mask - 1], recv.at[mask - 1], (px, py, pc)))
        for cp in cps:
            cp.start()
        for k, pid in enumerate(peers):
            landed = slots.at[pid]
            _rcopy(landed, landed, send.at[k], recv.at[k], (x, y, c)).wait_recv()
        for cp in cps:
            cp.wait_send()
        acc = slots[0]
        for i in range(1, N_DEV):
            acc = acc + slots[i]
        o_ref[...] = acc

    vm = pl.BlockSpec(memory_space=pltpu.VMEM)
    return pl.pallas_call(
        body,
        name="small_allreduce",
        in_specs=[vm],
        out_specs=vm,
        out_shape=jax.ShapeDtypeStruct(buf.shape, F32),
        scratch_shapes=[pltpu.VMEM((N_DEV, r, LANES), F32), pltpu.SemaphoreType.DMA((N_DEV - 1,)), pltpu.SemaphoreType.DMA((N_DEV - 1,))],
    )(buf)


WEIGHT_NAMES = ["ffn1_norm", "ffn1_w_gate", "ffn1_w_up", "ffn1_w_down", "mix_norm", "mem_norm", "w_mem_kv", "mem_q_gain",
                "mem_k_gain", "w_in_a", "hgrn_lb_logits", "hgrn_o_gain", "w_in_b", "fox_q_gain", "kv_norm", "w_kv", "fox_f_bias",
                "fox_k_gain", "w_out", "ffn2_norm", "ffn2_w_gate", "ffn2_w_up", "ffn2_w_down"]
SHARDED = ["ffn1_w_gate", "ffn1_w_up", "ffn1_w_down", "w_mem_kv", "w_in_a", "w_in_b", "w_kv", "w_out", "ffn2_w_gate", "ffn2_w_up", "ffn2_w_down"]
SMALL = [n for n in WEIGHT_NAMES if n not in SHARDED]
N_LAYERS, N_A = 4, 2
KV_PAD = 13 * LANES


def _halves(t):
    return t.reshape((2, t.shape[0] // 2) + t.shape[1:])


def _cols_from_chips(g):
    return jnp.moveaxis(g, 0, 2).reshape(g.shape[1], g.shape[2], N_CHIPS * g.shape[3])


def _rows_from_chips(g):
    return jnp.moveaxis(g, 0, 1).reshape(g.shape[1], N_CHIPS * g.shape[2], g.shape[3])


def _pair_tile(g):
    return jnp.tile(g, (1, 2)).reshape(g.shape[0], 1, LANES)


def _pair_fold(g):
    return g[:, :HEAD64] + g[:, HEAD64:]


def kernel(x, mem, ffn1_norm, ffn1_w_gate, ffn1_w_up, ffn1_w_down, mix_norm, mem_norm, w_mem_kv, mem_q_gain, mem_k_gain, w_in_a, hgrn_lb_logits, hgrn_o_gain, w_in_b, fox_q_gain, kv_norm, w_kv, fox_f_bias, fox_k_gain, w_out, ffn2_norm, ffn2_w_gate, ffn2_w_up, ffn2_w_down, loss_target, m_ffn1_norm, m_ffn1_w_gate, m_ffn1_w_up, m_ffn1_w_down, m_mix_norm, m_mem_norm, m_w_mem_kv, m_mem_q_gain, m_mem_k_gain, m_w_in_a, m_hgrn_lb_logits, m_hgrn_o_gain, m_w_in_b, m_fox_q_gain, m_kv_norm, m_w_kv, m_fox_f_bias, m_fox_k_gain, m_w_out, m_ffn2_norm, m_ffn2_w_gate, m_ffn2_w_up, m_ffn2_w_down, v_ffn1_norm, v_ffn1_w_gate, v_ffn1_w_up, v_ffn1_w_down, v_mix_norm, v_mem_norm, v_w_mem_kv, v_mem_q_gain, v_mem_k_gain, v_w_in_a, v_hgrn_lb_logits, v_hgrn_o_gain, v_w_in_b, v_fox_q_gain, v_kv_norm, v_w_kv, v_fox_f_bias, v_fox_k_gain, v_w_out, v_ffn2_norm, v_ffn2_w_gate, v_ffn2_w_up, v_ffn2_w_down):
    given = dict(locals())
    w = {n: given[n] for n in WEIGHT_NAMES}
    xs, mems, tgt = x[0], mem[0], loss_target[0]
    s, d = xs.shape
    my_chip = 2 * lax.axis_index("x") + lax.axis_index("y")
    sel = jnp.stack([my_chip, lax.axis_index("c")]).astype(jnp.int32)
    c_arr = sel[1:]

    def shard3(t):
        hv = _halves(t)
        return hv.reshape(2, -1, hv.shape[-1])

    to_gather = [cast_into_slot(shard3(w[n]), sel, BF16) for n in SHARDED]
    to_gather.append(cast_into_slot(hgrn_lb_logits.reshape(2, 1, -1), sel, F32))
    gathered = all_gather_chips(to_gather)
    gw = {n: g.reshape((N_CHIPS,) + w[n].shape) for n, g in zip(SHARDED, gathered[:-1])}
    w_in = {"a": _cols_from_chips(gw["w_in_a"]), "b": _cols_from_chips(gw["w_in_b"])}
    w_kv_full = _cols_from_chips(gw["w_kv"][:, None])
    w_kv_full = jnp.pad(w_kv_full, ((0, 0), (0, 0), (0, KV_PAD - w_kv_full.shape[-1])))
    w_mkv = _rows_from_chips(gw["w_mem_kv"])
    w_o = _rows_from_chips(gw["w_out"])
    logits3 = jnp.moveaxis(gathered[-1].reshape(N_CHIPS, 2, -1), 0, 1).reshape(2, 1, -1)
    lb3 = lb_fwd(logits3)

    norm3 = {n: w[n].reshape(N_LAYERS, 1, d) for n in ("ffn1_norm", "mix_norm", "mem_norm", "ffn2_norm")}
    kvn3 = kv_norm.reshape(1, 1, d)
    mqg3, mkg3 = _pair_tile(mem_q_gain), _pair_tile(mem_k_gain)
    og3 = hgrn_o_gain.reshape(N_A, 1, LANES)
    fqg3 = _pair_tile(fox_q_gain)
    fkg = jnp.tile(fox_k_gain, 2).reshape(1, LANES)
    fb = jnp.pad(fox_f_bias, (0, LANES - fox_f_bias.shape[0])).reshape(1, LANES)

    sv = [dict() for _ in range(N_LAYERS)]
    h = xs
    kv = None
    for l in range(N_LAYERS):
        t = sv[l]
        t["x0"] = h
        h, t["a1"], t["b1"] = ffn_fwd(h, norm3["ffn1_norm"], gw["ffn1_w_gate"], gw["ffn1_w_up"], gw["ffn1_w_down"], l)
        t["x1"] = h
        if l < N_A:
            t["proj"] = proj_fwd(h, norm3["mix_norm"], l, w_in["a"], l)
            main, t["o"] = hgrn_fwd(t["proj"], lb3, og3, l)
            t["qblk"] = 12
        else:
            t["proj"] = proj_fwd(h, norm3["mix_norm"], l, w_in["b"], l - N_A)
            main, t["o"], t["lse"] = fox_fwd(t["proj"], kv["k"], kv["v"], kv["clf"], kv["clf_t"], fqg3, l - N_A)
            t["qblk"] = 6
        t["kvm"] = proj_fwd(mems, norm3["mem_norm"], l, w_mkv, l)
        memo = memattn_fwd(t["proj"], t["qblk"], t["kvm"], mqg3, mkg3, l)
        t["mixed"] = jnp.concatenate([main, memo], axis=-1)
        h = mm_res(h, t["mixed"], w_o, l)
        t["x2"] = h
        h, t["a2"], t["b2"] = ffn_fwd(h, norm3["ffn2_norm"], gw["ffn2_w_gate"], gw["ffn2_w_up"], gw["ffn2_w_down"], l)
        if l == N_A - 1:
            kv = {"x": h, "kvf": proj_fwd(h, kvn3, 0, w_kv_full, 0)}
            kv["k"], kv["v"], kv["clf"] = kvprep_fwd(kv["kvf"], fkg, fb)
            kv["clf_t"] = kv["clf"][:, :16].T

    loss_local, dx = loss_head(h, tgt)

    nc = N_CHIPS
    fc = ffn1_w_gate.shape[-1]
    gbuf = {}
    for n in ("ffn1_w_gate", "ffn1_w_up", "ffn2_w_gate", "ffn2_w_up"):
        gbuf[n] = lax.empty((2, nc, N_LAYERS // 2, d, fc), F32)
    for n in ("ffn1_w_down", "ffn2_w_down"):
        gbuf[n] = lax.empty((2, nc, N_LAYERS // 2, fc, d), F32)
    dw_in = {"a": [None] * N_A, "b": [None] * (N_LAYERS - N_A)}
    dw_o, dw_mkv = [None] * N_LAYERS, [None] * N_LAYERS
    sg = {n: [None] * N_LAYERS for n in ("ffn1_norm", "mix_norm", "mem_norm", "ffn2_norm", "mem_q_gain", "mem_k_gain")}
    sg["hgrn_o_gain"], sg["fox_q_gain"], dlb = [None] * N_A, [None] * (N_LAYERS - N_A), [None] * N_A
    dk_sh = jnp.zeros((s, KV_MAIN), F32)
    dv_sh = jnp.zeros((s, KV_MAIN), F32)
    dclf = jnp.zeros((s, LANES), F32)
    zero_mem = jnp.zeros(mems.shape, F32)
    dw_kv = None
    for l in reversed(range(N_LAYERS)):
        t = sv[l]
        if l == N_A - 1:
            dkvf, dfkg, dfb = kvprep_bwd(kv["kvf"], fkg, fb, dk_sh, dv_sh, dclf)
            dx, sg["kv_norm"], xn_kv, dpb = proj_bwd(kv["x"], kvn3, 0, [dkvf], w_kv_full, 0, dx)
            dw_kv = wgrad(xn_kv, dpb)
        dx, da, db, hm, xn, dyb, sg["ffn2_norm"][l] = ffn_bwd(t["x2"], norm3["ffn2_norm"], dx, t["a2"], t["b2"], gw["ffn2_w_gate"], gw["ffn2_w_up"], gw["ffn2_w_down"], l)
        gbuf["ffn2_w_gate"] = wgrad(xn, da, buf=gbuf["ffn2_w_gate"], l=l)
        gbuf["ffn2_w_up"] = wgrad(xn, db, buf=gbuf["ffn2_w_up"], l=l)
        gbuf["ffn2_w_down"] = wgrad(hm, dyb, buf=gbuf["ffn2_w_down"], l=l)
        dmixed, dxb = mm_nt(dx, w_o, l)
        dw_o[l] = wgrad(t["mixed"], dxb)
        dqm, dkvm, dmq, dmk = memattn_bwd(t["proj"], t["qblk"], t["kvm"], mqg3, mkg3, l, dmixed)
        sg["mem_q_gain"][l], sg["mem_k_gain"][l] = _pair_fold(dmq), _pair_fold(dmk)
        _, sg["mem_norm"][l], memn, dkvmb = proj_bwd(mems, norm3["mem_norm"], l, [dkvm], w_mkv, l, zero_mem)
        dw_mkv[l] = wgrad(memn, dkvmb)
        if l < N_A:
            dzq, dzf, dvi, dzg, dlb[l], sg["hgrn_o_gain"][l] = hgrn_bwd(t["proj"], lb3, og3, l, t["o"], dmixed)
            parts, key, wl, tn = [dzq, dzf, dvi, dzg, dqm], "a", l, 13 * LANES
        else:
            lse_t = t["lse"].reshape(s, 6, LANES)[:, :, :2].reshape(s, 12).T
            lse_t = jnp.pad(lse_t, ((0, 4), (0, 0)))
            dq, dgate, dk_sh, dv_sh, dclf, dfq = fox_bwd(t["proj"], kv["k"], kv["v"], kv["clf"], kv["clf_t"], fqg3, l - N_A, t["o"], t["lse"], lse_t, dmixed, dk_sh, dv_sh, dclf)
            sg["fox_q_gain"][l - N_A] = _pair_fold(dfq)
            parts, key, wl, tn = [dq, dgate, dqm], "b", l - N_A, 7 * LANES
        dx, sg["mix_norm"][l], hn, dpb = proj_bwd(t["x1"], norm3["mix_norm"], l, parts, w_in[key], wl, dx)
        dw_in[key][wl] = wgrad(hn, dpb, tn=tn)
        dx, da, db, hm, xn, dyb, sg["ffn1_norm"][l] = ffn_bwd(t["x0"], norm3["ffn1_norm"], dx, t["a1"], t["b1"], gw["ffn1_w_gate"], gw["ffn1_w_up"], gw["ffn1_w_down"], l)
        gbuf["ffn1_w_gate"] = wgrad(xn, da, buf=gbuf["ffn1_w_gate"], l=l)
        gbuf["ffn1_w_up"] = wgrad(xn, db, buf=gbuf["ffn1_w_up"], l=l)
        gbuf["ffn1_w_down"] = wgrad(hm, dyb, buf=gbuf["ffn1_w_down"], l=l)

    def col_layout(stack):
        lw, dd, nn = stack.shape
        g = stack.reshape(2, lw // 2, dd, nc, nn // nc)
        return jnp.transpose(g, (0, 3, 1, 2, 4)).reshape(2, nc, (lw // 2) * dd, nn // nc)

    def row_layout(stack):
        lw, rr, cc = stack.shape
        g = stack.reshape(2, lw // 2, nc, rr // nc, cc)
        return jnp.transpose(g, (0, 2, 1, 3, 4)).reshape(2, nc, (lw // 2) * (rr // nc), cc)

    glay = {n: gbuf[n].reshape(2, nc, -1, gbuf[n].shape[-1]) for n in gbuf}
    glay["w_in_a"] = col_layout(jnp.stack(dw_in["a"]))
    glay["w_in_b"] = col_layout(jnp.stack(dw_in["b"]))
    kv_cols = w_kv.shape[-1] * nc
    dkv_rows = dw_kv[:, :kv_cols].reshape(2, d // 2, nc, kv_cols // nc)
    glay["w_kv"] = jnp.transpose(dkv_rows, (0, 2, 1, 3))
    glay["w_mem_kv"] = row_layout(jnp.stack(dw_mkv))
    glay["w_out"] = row_layout(jnp.stack(dw_o))

    gl = [glay[n] for n in SHARDED]
    recv = pair_exchange(gl)
    partial = [pair_sum(g, r, c_arr) for g, r in zip(gl, recv)]
    landed = chip_exchange(partial)
    mine = [chip_sum(p, q, sel) for p, q in zip(partial, landed)]
    both = pair_share(mine)
    grads = {n: g.reshape(w[n].shape) for n, g in zip(SHARDED, both)}

    dlogits = lb_bwd(logits3, dlb[1]).reshape(2, -1)
    small = {
        "ffn1_norm": jnp.concatenate(sg["ffn1_norm"]), "mix_norm": jnp.concatenate(sg["mix_norm"]),
        "mem_norm": jnp.concatenate(sg["mem_norm"]), "ffn2_norm": jnp.concatenate(sg["ffn2_norm"]),
        "mem_q_gain": jnp.concatenate(sg["mem_q_gain"]), "mem_k_gain": jnp.concatenate(sg["mem_k_gain"]),
        "hgrn_o_gain": jnp.concatenate(sg["hgrn_o_gain"]), "fox_q_gain": jnp.concatenate(sg["fox_q_gain"]),
        "kv_norm": sg["kv_norm"], "fox_f_bias": dfb[:, : fox_f_bias.shape[0]], "fox_k_gain": _pair_fold(dfkg),
        "hgrn_lb_logits": dlogits,
    }
    flat = [small[n].reshape(-1) for n in SMALL] + [loss_local.reshape(-1)]
    sizes = [f.shape[0] for f in flat]
    total = sum(sizes)
    padded = -(-total // (8 * LANES)) * (8 * LANES)
    packed = jnp.pad(jnp.concatenate(flat), (0, padded - total)).reshape(-1, LANES)
    summed = small_allreduce(packed).reshape(-1)
    off = 0
    for n, sz in zip(SMALL, sizes[:-1]):
        grads[n] = summed[off : off + sz].reshape(dlogits.shape if n == "hgrn_lb_logits" else w[n].shape)
        off += sz
    loss = summed[off]
    lbw = hgrn_lb_logits.shape[1]
    grads["hgrn_lb_logits"] = lax.dynamic_slice_in_dim(grads["hgrn_lb_logits"], my_chip * lbw, lbw, axis=1)

    delta, new_m, new_v = {}, {}, {}
    for n in WEIGHT_NAMES:
        delta[n], new_m[n], new_v[n] = adamw(w[n], grads[n], given["m_" + n], given["v_" + n])
    return (loss, dx[None], *[grads[n] for n in WEIGHT_NAMES], *[delta[n] for n in WEIGHT_NAMES],
            *[new_m[n] for n in WEIGHT_NAMES], *[new_v[n] for n in WEIGHT_NAMES])
```

```python
import functools

import jax
import jax.numpy as jnp
from jax import lax
from jax.experimental import pallas as pl
from jax.experimental.pallas import tpu as pltpu

F32, BF16 = jnp.float32, jnp.bfloat16
HI = lax.Precision.HIGHEST
EPS = 1e-6
MESH = pl.DeviceIdType.MESH
ANY = pl.BlockSpec(memory_space=pl.ANY)

VMEM_LIMIT_BYTES = 56 << 20
N_CHIPS = 4
N_DEV = 8
LANES = 128
HEAD64 = 64
CHUNK = 64
SUB = 16
TQ = 256
TOK = 256

ADAM_LR, ADAM_B1, ADAM_B2, ADAM_EPS, ADAM_WD, ADAM_STEP = 0.001, 0.9, 0.999, 1e-08, 0.01, 10


def _cparams(sem=None):
    return pltpu.CompilerParams(dimension_semantics=sem, vmem_limit_bytes=VMEM_LIMIT_BYTES)


def _mm(a, b, dims, prec=None):
    return lax.dot_general(a, b, (dims, ((), ())), preferred_element_type=F32, precision=prec)


def dot_nn(a, b, prec=None):
    return _mm(a, b, ((1,), (0,)), prec)


def dot_nt(a, b, prec=None):
    return _mm(a, b, ((1,), (1,)), prec)


def dot_tn(a, b, prec=None):
    return _mm(a, b, ((0,), (0,)), prec)


def bf(v):
    return v.astype(BF16)


def _sigmoid(z):
    return jax.nn.sigmoid(z)


def _dsilu(z, s):
    return s * (1.0 + z * (1.0 - s))


def _rms(x):
    r = lax.rsqrt(jnp.mean(x * x, axis=-1, keepdims=True) + EPS)
    return x * r, r


def _rms_bwd(dxn, u, r, g):
    du = dxn * g
    dx = r * (du - u * jnp.mean(du * u, axis=-1, keepdims=True))
    return dx, jnp.sum(dxn * u, axis=0, keepdims=True)


def _lane_mask0(shape):
    return lax.broadcasted_iota(jnp.int32, shape, len(shape) - 1) < HEAD64


def _rms64(x, m0):
    sq = x * x
    s0 = jnp.sum(jnp.where(m0, sq, 0.0), axis=-1, keepdims=True)
    s1 = jnp.sum(jnp.where(m0, 0.0, sq), axis=-1, keepdims=True)
    r = lax.rsqrt(jnp.where(m0, s0, s1) * (1.0 / HEAD64) + EPS)
    return x * r, r


def _rms64_bwd(dxn, u, r, g, m0):
    du = dxn * g
    t = du * u
    t0 = jnp.sum(jnp.where(m0, t, 0.0), axis=-1, keepdims=True)
    t1 = jnp.sum(jnp.where(m0, 0.0, t), axis=-1, keepdims=True)
    dx = r * (du - u * (jnp.where(m0, t0, t1) * (1.0 / HEAD64)))
    return dx, jnp.sum(dxn * u, axis=0, keepdims=True)


def _tok(s):
    return TOK if s % TOK == 0 else s


def _const(shape):
    return pl.BlockSpec(shape, lambda *_: (0,) * len(shape))


def ffn_fwd(x, gain3, l, wg, wu, wd, wl, comm=None):
    s, d = x.shape
    nc, _, _, fc = wg.shape
    tm = _tok(s)

    def body(x_ref, g_ref, wg_ref, wu_ref, wd_ref, xo_ref, a_ref, b_ref):
        xv = x_ref[...]
        u, _ = _rms(xv)
        xn = bf(u * g_ref[...])
        y = jnp.zeros((tm, d), F32)
        for c in range(nc):
            a = dot_nn(xn, wg_ref[c])
            b = dot_nn(xn, wu_ref[c])
            a_ref[c] = bf(a)
            b_ref[c] = bf(b)
            y = y + dot_nn(bf(a * _sigmoid(a) * b), wd_ref[c])
        xo_ref[...] = xv + 0.5 * y

    wspec = pl.BlockSpec((nc, None, d, fc), lambda i: (0, wl, 0, 0), pipeline_mode=pl.Buffered(1))
    wdspec = pl.BlockSpec((nc, None, fc, d), lambda i: (0, wl, 0, 0), pipeline_mode=pl.Buffered(1))
    row = pl.BlockSpec((tm, d), lambda i: (i, 0))
    act = pl.BlockSpec((nc, tm, fc), lambda i: (0, i, 0))
    return _carry(
        body,
        comm,
        name="ffn_fwd",
        grid=(s // tm,),
        in_specs=[row, pl.BlockSpec((None, 1, d), lambda i: (l, 0, 0)), wspec, wspec, wdspec],
        out_specs=[row, act, act],
        out_shape=[
            jax.ShapeDtypeStruct((s, d), F32),
            jax.ShapeDtypeStruct((nc, s, fc), BF16),
            jax.ShapeDtypeStruct((nc, s, fc), BF16),
        ],
        scratch_shapes=[],
        args=(x, gain3, wg, wu, wd),
    )


def ffn_bwd(x, gain3, l, dout, a, b, wg, wu, wd, wl, comm=None):
    s, d = x.shape
    nc, _, _, fc = wg.shape
    tm = _tok(s)

    def body(x_ref, g_ref, do_ref, a_ref, b_ref, wg_ref, wu_ref, wd_ref, dx_ref, da_ref, db_ref, hm_ref, xn_ref, dy_ref, dg_ref):
        xv = x_ref[...]
        g = g_ref[...]
        u, r = _rms(xv)
        xn_ref[...] = bf(u * g)
        dout = do_ref[...]
        dy = bf(0.5 * dout)
        dy_ref[...] = dy
        dxn = jnp.zeros((tm, d), F32)
        for c in range(nc):
            av = a_ref[c].astype(F32)
            bv = b_ref[c].astype(F32)
            sg = _sigmoid(av)
            sl = av * sg
            dh = dot_nt(dy, wd_ref[c])
            da = bf(dh * bv * _dsilu(av, sg))
            db = bf(dh * sl)
            da_ref[c] = da
            db_ref[c] = db
            hm_ref[c] = bf(sl * bv)
            dxn = dxn + dot_nt(da, wg_ref[c]) + dot_nt(db, wu_ref[c])
        dx, dg = _rms_bwd(dxn, u, r, g)
        dx_ref[...] = dout + dx

        @pl.when(pl.program_id(0) == 0)
        def _():
            dg_ref[...] = jnp.zeros_like(dg_ref)

        dg_ref[...] += dg

    wspec = pl.BlockSpec((nc, None, d, fc), lambda i: (0, wl, 0, 0), pipeline_mode=pl.Buffered(1))
    wdspec = pl.BlockSpec((nc, None, fc, d), lambda i: (0, wl, 0, 0), pipeline_mode=pl.Buffered(1))
    row = pl.BlockSpec((tm, d), lambda i: (i, 0))
    act = pl.BlockSpec((nc, tm, fc), lambda i: (0, i, 0))
    act_shape = jax.ShapeDtypeStruct((nc, s, fc), BF16)
    return _carry(
        body,
        comm,
        name="ffn_bwd",
        grid=(s // tm,),
        in_specs=[row, pl.BlockSpec((None, 1, d), lambda i: (l, 0, 0)), row, act, act, wspec, wspec, wdspec],
        out_specs=[row, act, act, act, row, row, _const((1, d))],
        out_shape=[
            jax.ShapeDtypeStruct((s, d), F32),
            act_shape,
            act_shape,
            act_shape,
            jax.ShapeDtypeStruct((s, d), BF16),
            jax.ShapeDtypeStruct((s, d), BF16),
            jax.ShapeDtypeStruct((1, d), F32),
        ],
        scratch_shapes=[],
        args=(x, gain3, dout, a, b, wg, wu, wd),
    )


def wgrad(a, b, tn=None, buf=None, l=None):
    ca = a.shape[0] if a.ndim == 3 else 1
    cb = b.shape[0] if b.ndim == 3 else 1
    nc = max(ca, cb)
    s, m = a.shape[-2:]
    n = b.shape[-1]
    tn = n if tn is None else tn
    assert n % tn == 0

    def body(*refs):
        a_ref, b_ref, o_ref = refs[-3:] if buf is None else (refs[0], refs[1], refs[3])
        o_ref[...] = dot_tn(a_ref[...], b_ref[...])

    a_spec = pl.BlockSpec((None, s, m), lambda c, j: (c, 0, 0)) if a.ndim == 3 else pl.BlockSpec((s, m), lambda c, j: (0, 0))
    b_spec = pl.BlockSpec((None, s, tn), lambda c, j: (c, 0, j)) if b.ndim == 3 else pl.BlockSpec((s, tn), lambda c, j: (0, j))
    if buf is None:
        assert nc == 1
        return pl.pallas_call(
            body,
            name="wgrad",
            grid=(1, n // tn),
            in_specs=[a_spec, b_spec],
            out_specs=pl.BlockSpec((m, tn), lambda c, j: (0, j)),
            out_shape=jax.ShapeDtypeStruct((m, n), F32),
            compiler_params=_cparams(("arbitrary", "arbitrary")),
        )(a, b)
    lh = buf.shape[2]
    hi, lo = l // lh, l % lh
    o_spec = pl.BlockSpec((None, None, None, m, tn), lambda c, j: (hi, c, lo, 0, j))
    return pl.pallas_call(
        body,
        name="wgrad_buf",
        grid=(nc, n // tn),
        in_specs=[a_spec, b_spec, ANY],
        out_specs=o_spec,
        out_shape=jax.ShapeDtypeStruct(buf.shape, F32),
        input_output_aliases={2: 0},
        compiler_params=_cparams(("arbitrary", "arbitrary")),
    )(a, b, buf)


def proj_fwd(x, gain3, l, w, wl):
    s, d = x.shape
    n = w.shape[-1]
    tm = _tok(s)

    def body(x_ref, g_ref, w_ref, o_ref):
        u, _ = _rms(x_ref[...])
        o_ref[...] = dot_nn(bf(u * g_ref[...]), w_ref[...])

    return pl.pallas_call(
        body,
        name="proj_fwd",
        grid=(s // tm,),
        in_specs=[
            pl.BlockSpec((tm, d), lambda i: (i, 0)),
            pl.BlockSpec((None, 1, d), lambda i: (l, 0, 0)),
            pl.BlockSpec((None, d, n), lambda i: (wl, 0, 0)),
        ],
        out_specs=pl.BlockSpec((tm, n), lambda i: (i, 0)),
        out_shape=jax.ShapeDtypeStruct((s, n), F32),
        compiler_params=_cparams(("arbitrary",)),
    )(x, gain3, w)


def proj_bwd(x, gain3, l, parts, w, wl, dx_in):
    s, d = x.shape
    n = w.shape[-1]
    widths = [p.shape[1] for p in parts]
    assert sum(widths) == n
    tm = _tok(s)
    npart = len(parts)

    def body(*refs):
        x_ref, g_ref, w_ref, dxin_ref = refs[:4]
        p_refs = refs[4 : 4 + npart]
        dx_ref, dg_ref, xn_ref, dpb_ref = refs[4 + npart :]
        g = g_ref[...]
        u, r = _rms(x_ref[...])
        xn_ref[...] = bf(u * g)
        dxn = jnp.zeros((tm, d), F32)
        off = 0
        for p_ref, wd_ in zip(p_refs, widths):
            dp = bf(p_ref[...])
            dpb_ref[:, off : off + wd_] = dp
            dxn = dxn + dot_nt(dp, w_ref[:, off : off + wd_])
            off += wd_
        dx, dg = _rms_bwd(dxn, u, r, g)
        dx_ref[...] = dxin_ref[...] + dx

        @pl.when(pl.program_id(0) == 0)
        def _():
            dg_ref[...] = jnp.zeros_like(dg_ref)

        dg_ref[...] += dg

    row = pl.BlockSpec((tm, d), lambda i: (i, 0))
    return pl.pallas_call(
        body,
        name="proj_bwd",
        grid=(s // tm,),
        in_specs=[row, pl.BlockSpec((None, 1, d), lambda i: (l, 0, 0)), pl.BlockSpec((None, d, n), lambda i: (wl, 0, 0)), row]
        + [pl.BlockSpec((tm, wd_), lambda i: (i, 0)) for wd_ in widths],
        out_specs=[row, _const((1, d)), row, pl.BlockSpec((tm, n), lambda i: (i, 0))],
        out_shape=[
            jax.ShapeDtypeStruct((s, d), F32),
            jax.ShapeDtypeStruct((1, d), F32),
            jax.ShapeDtypeStruct((s, d), BF16),
            jax.ShapeDtypeStruct((s, n), BF16),
        ],
        compiler_params=_cparams(("arbitrary",)),
    )(x, gain3, w, dx_in, *parts)


def mm_res(x, a, w, l):
    s, d = x.shape
    k = a.shape[1]
    tm = _tok(s)

    def body(x_ref, a_ref, w_ref, o_ref):
        o_ref[...] = x_ref[...] + dot_nn(a_ref[...], w_ref[...])

    return pl.pallas_call(
        body,
        name="mm_res",
        grid=(s // tm,),
        in_specs=[
            pl.BlockSpec((tm, d), lambda i: (i, 0)),
            pl.BlockSpec((tm, k), lambda i: (i, 0)),
            pl.BlockSpec((None, k, d), lambda i: (l, 0, 0)),
        ],
        out_specs=pl.BlockSpec((tm, d), lambda i: (i, 0)),
        out_shape=jax.ShapeDtypeStruct((s, d), F32),
        compiler_params=_cparams(("arbitrary",)),
    )(x, a, w)


def mm_nt(dx, w, l):
    s, d = dx.shape
    k = w.shape[1]
    tm = _tok(s)

    def body(dx_ref, w_ref, o_ref, dxb_ref):
        dxb = bf(dx_ref[...])
        dxb_ref[...] = dxb
        o_ref[...] = dot_nt(dxb, w_ref[...])

    return pl.pallas_call(
        body,
        name="mm_nt",
        grid=(s // tm,),
        in_specs=[pl.BlockSpec((tm, d), lambda i: (i, 0)), pl.BlockSpec((None, k, d), lambda i: (l, 0, 0))],
        out_specs=[pl.BlockSpec((tm, k), lambda i: (i, 0)), pl.BlockSpec((tm, d), lambda i: (i, 0))],
        out_shape=[jax.ShapeDtypeStruct((s, k), F32), jax.ShapeDtypeStruct((s, d), BF16)],
        compiler_params=_cparams(("arbitrary",)),
    )(dx, w)


def lb_fwd(logits3):
    def body(l_ref, o_ref):
        l0, l1 = l_ref[0], l_ref[1]
        m = jnp.maximum(l0, l1)
        e0, e1 = jnp.exp(l0 - m), jnp.exp(l1 - m)
        p0, p1 = e0 / (e0 + e1), e1 / (e0 + e1)
        o_ref[0] = p0 - p0
        o_ref[1] = (p0 + p1) - p0

    return pl.pallas_call(body, name="lb_fwd", out_shape=jax.ShapeDtypeStruct(logits3.shape, F32))(logits3)


def lb_bwd(logits3, dlb1):
    def body(l_ref, d_ref, o_ref):
        l0, l1 = l_ref[0], l_ref[1]
        m = jnp.maximum(l0, l1)
        e0, e1 = jnp.exp(l0 - m), jnp.exp(l1 - m)
        p0, p1 = e0 / (e0 + e1), e1 / (e0 + e1)
        t = d_ref[...] * p0 * p1
        o_ref[0] = -t
        o_ref[1] = t

    return pl.pallas_call(body, name="lb_bwd", out_shape=jax.ShapeDtypeStruct(logits3.shape, F32))(logits3, dlb1)


def _hgrn_gates(zq, zf, lb):
    sf = _sigmoid(zf)
    f = lb + (1.0 - lb) * sf
    sq = _sigmoid(zq)
    return sf, f, jnp.log(f), 1.0 - f, sq, zq * sq


def _tri(n, upper=False):
    r = lax.broadcasted_iota(jnp.int32, (n, n), 0)
    c = lax.broadcasted_iota(jnp.int32, (n, n), 1)
    return jnp.where((c >= r) if upper else (r >= c), 1.0, 0.0).astype(F32)


def hgrn_fwd(proj, lb3, og3, l, comm=None):
    s = proj.shape[0]
    nh = 6
    n_chunk = s // CHUNK
    nsub = CHUNK // SUB

    def body(zq_ref, zf_ref, vi_ref, zg_ref, lb_ref, og_ref, main_ref, o_ref, q_s, k_s, v_s, c_s):
        lb = lb_ref[...]
        og = og_ref[...]
        tril = _tri(CHUNK)
        rowi = lax.broadcasted_iota(jnp.int32, (SUB, LANES), 0)

        def chunk(ci, st):
            r0 = pl.multiple_of(ci * CHUNK, CHUNK)
            rows = pl.ds(r0, CHUNK)
            zg = zg_ref[rows, :]
            _, _, lf, k, _, q = _hgrn_gates(zq_ref[rows, :], zf_ref[rows, :], lb)
            v = vi_ref[rows, :]
            c = dot_nn(tril, lf, HI)
            q_s[...] = q
            k_s[...] = k
            v_s[...] = v
            c_s[...] = c
            o_inter = dot_nt(q * jnp.exp(c), st, HI)
            parts = []
            for i in range(nsub):
                lo = i * SUB
                blk = pl.ds(lo, SUB)
                qb, kb, cb = q_s[blk, :], k_s[blk, :], c_s[blk, :]
                ob = o_inter[lo : lo + SUB]
                if i > 0:
                    rr = c_s[pl.ds(lo - 1, 1), :]
                    qt = qb * jnp.exp(cb - rr)
                    kt = k_s[pl.ds(0, lo), :] * jnp.exp(rr - c_s[pl.ds(0, lo), :])
                    ob = ob + dot_nn(dot_nt(qt, kt, HI), v_s[pl.ds(0, lo), :], HI)
                for t in range(SUB):
                    e = jnp.where(rowi >= t, jnp.exp(cb - c_s[pl.ds(lo + t, 1), :]), 0.0)
                    a = jnp.sum(qb * k_s[pl.ds(lo + t, 1), :] * e, axis=-1, keepdims=True)
                    ob = ob + a * v_s[pl.ds(lo + t, 1), :]
                parts.append(ob)
            o = jnp.concatenate(parts, axis=0)
            ce = c_s[pl.ds(CHUNK - 1, 1), :]
            st = st * jnp.exp(ce) + dot_tn(v, k * jnp.exp(ce - c), HI)
            on, _ = _rms(o)
            o_ref[rows, :] = o
            main_ref[rows, :] = bf(on * og * (zg * _sigmoid(zg)))
            return st

        lax.fori_loop(0, n_chunk, chunk, jnp.zeros((LANES, LANES), F32))

    def col(k):
        return pl.BlockSpec((s, LANES), lambda h: (0, k * nh + h))

    vec = pl.BlockSpec((None, 1, LANES), lambda h: (l, 0, h))
    return _carry(
        body,
        comm,
        name="hgrn_fwd",
        grid=(nh,),
        in_specs=[col(0), col(1), col(2), col(3), vec, pl.BlockSpec((None, 1, LANES), lambda h: (l, 0, 0))],
        out_specs=[pl.BlockSpec((s, LANES), lambda h: (0, h))] * 2,
        out_shape=[jax.ShapeDtypeStruct((s, nh * LANES), BF16), jax.ShapeDtypeStruct((s, nh * LANES), F32)],
        scratch_shapes=[pltpu.VMEM((CHUNK, LANES), F32)] * 4,
        args=(proj, proj, proj, proj, lb3, og3),
    )


def hgrn_bwd(proj, lb3, og3, l, o, dmixed, comm=None):
    s = proj.shape[0]
    nh = 6
    n_chunk = s // CHUNK
    nsub = CHUNK // SUB

    def body(zq_ref, zf_ref, vi_ref, zg_ref, lb_ref, og_ref, o_ref, dm_ref,
             dzq_ref, dzf_ref, dvi_ref, dzg_ref, dlb_ref, dog_ref,
             st_s, q_s, k_s, v_s, c_s, do_s, dq_s, dk_s, dv_s, acc_s):
        lb = lb_ref[...]
        og = og_ref[...]
        tril = _tri(CHUNK)
        triu = _tri(CHUNK, upper=True)
        rowi = lax.broadcasted_iota(jnp.int32, (SUB, LANES), 0)

        def fwd_chunk(ci, st):
            rows = pl.ds(pl.multiple_of(ci * CHUNK, CHUNK), CHUNK)
            _, _, lf, k, _, _ = _hgrn_gates(zq_ref[rows, :], zf_ref[rows, :], lb)
            c = dot_nn(tril, lf, HI)
            ce = jnp.sum(lf, axis=0, keepdims=True)
            st_s[ci] = st
            return st * jnp.exp(ce) + dot_tn(vi_ref[rows, :], k * jnp.exp(ce - c), HI)

        lax.fori_loop(0, n_chunk, fwd_chunk, jnp.zeros((LANES, LANES), F32))
        acc_s[...] = jnp.zeros_like(acc_s)

        def bwd_chunk(jj, carry):
            dst, cg = carry
            ci = n_chunk - 1 - jj
            rows = pl.ds(pl.multiple_of(ci * CHUNK, CHUNK), CHUNK)
            zq, zf, zg = zq_ref[rows, :], zf_ref[rows, :], zg_ref[rows, :]
            sf, f, lf, k, sq, q = _hgrn_gates(zq, zf, lb)
            v = vi_ref[rows, :]
            c = dot_nn(tril, lf, HI)
            st = st_s[ci]
            on, r = _rms(o_ref[rows, :])
            sg = _sigmoid(zg)
            dmain = dm_ref[rows, :]
            dy = dmain * (zg * sg)
            dzg_ref[rows, :] = dmain * (on * og) * _dsilu(zg, sg)
            do, dog = _rms_bwd(dy, on, r, og)
            acc_s[pl.ds(0, 1), :] += dog
            q_s[...] = q
            k_s[...] = k
            v_s[...] = v
            c_s[...] = c
            do_s[...] = do
            ce = c_s[pl.ds(CHUNK - 1, 1), :]
            eq = jnp.exp(c)
            ek = jnp.exp(ce - c)
            qt_all = q * eq
            dq_s[...] = dot_nn(do, st, HI) * eq
            dv_s[...] = dot_nt(k * ek, dst, HI)
            dk_s[...] = dot_nn(v, dst, HI) * ek
            dst = dst * jnp.exp(ce) + dot_tn(do, qt_all, HI)
            for i in range(nsub):
                lo = i * SUB
                blk = pl.ds(lo, SUB)
                qb, cb, dob = q_s[blk, :], c_s[blk, :], do_s[blk, :]
                if i > 0:
                    prev = pl.ds(0, lo)
                    rr = c_s[pl.ds(lo - 1, 1), :]
                    eqi = jnp.exp(cb - rr)
                    eki = jnp.exp(rr - c_s[prev, :])
                    qt = qb * eqi
                    kt = k_s[prev, :] * eki
                    amat = dot_nt(qt, kt, HI)
                    damat = dot_nt(dob, v_s[prev, :], HI)
                    dv_s[prev, :] += dot_tn(amat, dob, HI)
                    dq_s[blk, :] += dot_nn(damat, kt, HI) * eqi
                    dk_s[prev, :] += dot_tn(damat, qt, HI) * eki
                dqb = jnp.zeros((SUB, LANES), F32)
                for t in range(SUB):
                    row = pl.ds(lo + t, 1)
                    e = jnp.where(rowi >= t, jnp.exp(cb - c_s[row, :]), 0.0)
                    kr = k_s[row, :]
                    a = jnp.sum(qb * kr * e, axis=-1, keepdims=True)
                    da = jnp.sum(dob * v_s[row, :], axis=-1, keepdims=True)
                    dv_s[row, :] += jnp.sum(a * dob, axis=0, keepdims=True)
                    dqb = dqb + da * kr * e
                    dk_s[row, :] += jnp.sum(da * qb * e, axis=0, keepdims=True)
                dq_s[blk, :] += dqb
            dq, dk = dq_s[...], dk_s[...]
            dg = q * dq - k * dk
            dlf = dot_nn(triu, dg, HI) + cg
            cg = cg + jnp.sum(dg, axis=0, keepdims=True)
            df = dlf / f - dk
            dzf_ref[rows, :] = df * (1.0 - lb) * sf * (1.0 - sf)
            acc_s[pl.ds(1, 1), :] += jnp.sum(df * (1.0 - sf), axis=0, keepdims=True)
            dzq_ref[rows, :] = dq * _dsilu(zq, sq)
            dvi_ref[rows, :] = dv_s[...]
            return dst, cg

        lax.fori_loop(0, n_chunk, bwd_chunk, (jnp.zeros((LANES, LANES), F32), jnp.zeros((1, LANES), F32)))
        dlb_ref[...] = acc_s[pl.ds(1, 1), :]

        @pl.when(pl.program_id(0) == 0)
        def _():
            dog_ref[...] = jnp.zeros_like(dog_ref)

        dog_ref[...] += acc_s[pl.ds(0, 1), :]

    def col(k):
        return pl.BlockSpec((s, LANES), lambda h: (0, k * nh + h))

    head = pl.BlockSpec((s, LANES), lambda h: (0, h))
    vec = pl.BlockSpec((None, 1, LANES), lambda h: (l, 0, h))
    ck = pltpu.VMEM((CHUNK, LANES), F32)
    return _carry(
        body,
        comm,
        name="hgrn_bwd",
        grid=(nh,),
        in_specs=[col(0), col(1), col(2), col(3), vec, pl.BlockSpec((None, 1, LANES), lambda h: (l, 0, 0)), head, head],
        out_specs=[head] * 4 + [pl.BlockSpec((1, LANES), lambda h: (0, h)), _const((1, LANES))],
        out_shape=[jax.ShapeDtypeStruct((s, nh * LANES), F32)] * 4
        + [jax.ShapeDtypeStruct((1, nh * LANES), F32), jax.ShapeDtypeStruct((1, LANES), F32)],
        scratch_shapes=[pltpu.VMEM((n_chunk, LANES, LANES), F32)] + [ck] * 8 + [pltpu.VMEM((8, LANES), F32)],
        args=(proj, proj, proj, proj, lb3, og3, o, dmixed),
    )


MEM_SCALE = HEAD64**-0.5


def _mem_heads(qraw, kvm, qg, kg, pr, m0):
    lo = pr * LANES
    uq, rq = _rms64(qraw[:, lo : lo + LANES], m0)
    uk, rk = _rms64(kvm[:, lo : lo + LANES], m0)
    v = bf(kvm[:, 2 * LANES + lo : 3 * LANES + lo])
    return uq, rq, uk, rk, v, uq * qg, bf(uk * kg)


def memattn_fwd(proj, qblk, kvm, qg3, kg3, l):
    s = proj.shape[0]
    nm = kvm.shape[0]
    tm = _tok(s)

    def body(q_ref, kv_ref, qg_ref, kg_ref, o_ref):
        m0 = _lane_mask0((1, LANES))
        qraw, kvv = q_ref[...], kv_ref[...]
        for pr in range(2):
            _, _, _, _, v, qn, kn = _mem_heads(qraw, kvv, qg_ref[...], kg_ref[...], pr, m0)
            out = jnp.zeros((tm, LANES), F32)
            for hh in range(2):
                mh = m0 if hh == 0 else jnp.logical_not(m0)
                sc = dot_nt(bf(jnp.where(mh, qn, 0.0)), kn) * MEM_SCALE
                p = jnp.exp(sc - jnp.max(sc, axis=-1, keepdims=True))
                p = p / jnp.sum(p, axis=-1, keepdims=True)
                out = jnp.where(mh, dot_nn(bf(p), v), out)
            o_ref[:, pr * LANES : (pr + 1) * LANES] = bf(out)

    gspec = pl.BlockSpec((None, 1, LANES), lambda i: (l, 0, 0))
    return pl.pallas_call(
        body,
        name="memattn_fwd",
        grid=(s // tm,),
        in_specs=[pl.BlockSpec((tm, 2 * LANES), lambda i: (i, qblk)), _const((nm, 4 * LANES)), gspec, gspec],
        out_specs=pl.BlockSpec((tm, 2 * LANES), lambda i: (i, 0)),
        out_shape=jax.ShapeDtypeStruct((s, 2 * LANES), BF16),
        compiler_params=_cparams(("arbitrary",)),
    )(proj, kvm, qg3, kg3)


def memattn_bwd(proj, qblk, kvm, qg3, kg3, l, dmixed):
    s = proj.shape[0]
    nm = kvm.shape[0]
    tm = _tok(s)

    def body(q_ref, kv_ref, qg_ref, kg_ref, dm_ref, dq_ref, dkv_ref, dqg_ref, dkg_ref):
        m0 = _lane_mask0((1, LANES))
        qraw, kvv = q_ref[...], kv_ref[...]
        qg, kg = qg_ref[...], kg_ref[...]

        @pl.when(pl.program_id(0) == 0)
        def _():
            dkv_ref[...] = jnp.zeros_like(dkv_ref)
            dqg_ref[...] = jnp.zeros_like(dqg_ref)
            dkg_ref[...] = jnp.zeros_like(dkg_ref)

        for pr in range(2):
            lo = pr * LANES
            uq, rq, uk, rk, v, qn, kn = _mem_heads(qraw, kvv, qg, kg, pr, m0)
            do = dm_ref[:, lo : lo + LANES]
            dqn = jnp.zeros((tm, LANES), F32)
            dkn = jnp.zeros((nm, LANES), F32)
            dv = jnp.zeros((nm, LANES), F32)
            for hh in range(2):
                mh = m0 if hh == 0 else jnp.logical_not(m0)
                qh = bf(jnp.where(mh, qn, 0.0))
                doh = bf(jnp.where(mh, do, 0.0))
                sc = dot_nt(qh, kn) * MEM_SCALE
                p = jnp.exp(sc - jnp.max(sc, axis=-1, keepdims=True))
                p = p / jnp.sum(p, axis=-1, keepdims=True)
                dp = dot_nt(doh, v)
                ds = bf(p * (dp - jnp.sum(p * dp, axis=-1, keepdims=True)))
                dqn = dqn + jnp.where(mh, dot_nn(ds, kn), 0.0) * MEM_SCALE
                dkn = dkn + dot_tn(ds, qh) * MEM_SCALE
                dv = dv + dot_tn(bf(p), doh)
            dqr, dqg = _rms64_bwd(dqn, uq, rq, qg, m0)
            dkr, dkg = _rms64_bwd(dkn, uk, rk, kg, m0)
            dq_ref[:, lo : lo + LANES] = dqr
            dkv_ref[:, lo : lo + LANES] += dkr
            dkv_ref[:, 2 * LANES + lo : 3 * LANES + lo] += dv
            dqg_ref[...] += dqg
            dkg_ref[...] += dkg

    gspec = pl.BlockSpec((None, 1, LANES), lambda i: (l, 0, 0))
    return pl.pallas_call(
        body,
        name="memattn_bwd",
        grid=(s // tm,),
        in_specs=[
            pl.BlockSpec((tm, 2 * LANES), lambda i: (i, qblk)),
            _const((nm, 4 * LANES)),
            gspec,
            gspec,
            pl.BlockSpec((tm, 2 * LANES), lambda i: (i, 3)),
        ],
        out_specs=[pl.BlockSpec((tm, 2 * LANES), lambda i: (i, 0)), _const((nm, 4 * LANES)), _const((1, LANES)), _const((1, LANES))],
        out_shape=[
            jax.ShapeDtypeStruct((s, 2 * LANES), F32),
            jax.ShapeDtypeStruct((nm, 4 * LANES), F32),
            jax.ShapeDtypeStruct((1, LANES), F32),
            jax.ShapeDtypeStruct((1, LANES), F32),
        ],
        compiler_params=_cparams(("arbitrary",)),
    )(proj, kvm, qg3, kg3, dmixed)


KV_MAIN = 768


def _log_sigmoid(z):
    return jnp.minimum(z, 0.0) - jnp.log(1.0 + jnp.exp(-jnp.abs(z)))


def kvprep_fwd(kvf, kg, fb):
    s = kvf.shape[0]
    tm = _tok(s)

    def body(kvf_ref, kg_ref, fb_ref, k_ref, v_ref, clf_ref, carry):
        m0 = _lane_mask0((1, LANES))

        @pl.when(pl.program_id(0) == 0)
        def _():
            carry[...] = jnp.zeros_like(carry)

        for j in range(KV_MAIN // LANES):
            u, _ = _rms64(kvf_ref[:, j * LANES : (j + 1) * LANES], m0)
            k_ref[:, j * LANES : (j + 1) * LANES] = bf(u * kg_ref[...])
        v_ref[...] = bf(kvf_ref[:, KV_MAIN : 2 * KV_MAIN])
        lf = _log_sigmoid(kvf_ref[:, 2 * KV_MAIN :] + fb_ref[...])
        clf_ref[...] = dot_nn(_tri(tm), lf, HI) + carry[...]
        carry[...] += jnp.sum(lf, axis=0, keepdims=True)

    n = kvf.shape[1]
    return pl.pallas_call(
        body,
        name="kvprep_fwd",
        grid=(s // tm,),
        in_specs=[pl.BlockSpec((tm, n), lambda i: (i, 0)), _const((1, LANES)), _const((1, LANES))],
        out_specs=[pl.BlockSpec((tm, KV_MAIN), lambda i: (i, 0))] * 2 + [pl.BlockSpec((tm, LANES), lambda i: (i, 0))],
        out_shape=[jax.ShapeDtypeStruct((s, KV_MAIN), BF16)] * 2 + [jax.ShapeDtypeStruct((s, LANES), F32)],
        scratch_shapes=[pltpu.VMEM((1, LANES), F32)],
        compiler_params=_cparams(("arbitrary",)),
    )(kvf, kg, fb)


def kvprep_bwd(kvf, kg, fb, dk, dv, dclf):
    s, n = kvf.shape
    tm = _tok(s)
    nb = s // tm

    def body(kvf_ref, kg_ref, fb_ref, dk_ref, dv_ref, dclf_ref, o_ref, dkg_ref, dfb_ref, carry):
        m0 = _lane_mask0((1, LANES))

        @pl.when(pl.program_id(0) == 0)
        def _():
            carry[...] = jnp.zeros_like(carry)
            dkg_ref[...] = jnp.zeros_like(dkg_ref)
            dfb_ref[...] = jnp.zeros_like(dfb_ref)

        kg_ = kg_ref[...]
        for j in range(KV_MAIN // LANES):
            cols = slice(j * LANES, (j + 1) * LANES)
            u, r = _rms64(kvf_ref[:, cols], m0)
            dkr, dkg = _rms64_bwd(dk_ref[:, cols], u, r, kg_, m0)
            o_ref[:, cols] = dkr
            dkg_ref[...] += dkg
        o_ref[:, KV_MAIN : 2 * KV_MAIN] = dv_ref[...]
        z = kvf_ref[:, 2 * KV_MAIN :] + fb_ref[...]
        dc = dclf_ref[...]
        dlf = dot_nn(_tri(tm, upper=True), dc, HI) + carry[...]
        carry[...] += jnp.sum(dc, axis=0, keepdims=True)
        dz = dlf * _sigmoid(-z)
        o_ref[:, 2 * KV_MAIN :] = dz
        dfb_ref[...] += jnp.sum(dz, axis=0, keepdims=True)

    rev = lambda i: (nb - 1 - i, 0)
    return pl.pallas_call(
        body,
        name="kvprep_bwd",
        grid=(nb,),
        in_specs=[pl.BlockSpec((tm, n), rev), _const((1, LANES)), _const((1, LANES)), pl.BlockSpec((tm, KV_MAIN), rev),
                  pl.BlockSpec((tm, KV_MAIN), rev), pl.BlockSpec((tm, LANES), rev)],
        out_specs=[pl.BlockSpec((tm, n), rev), _const((1, LANES)), _const((1, LANES))],
        out_shape=[jax.ShapeDtypeStruct((s, n), F32), jax.ShapeDtypeStruct((1, LANES), F32), jax.ShapeDtypeStruct((1, LANES), F32)],
        scratch_shapes=[pltpu.VMEM((1, LANES), F32)],
        compiler_params=_cparams(("arbitrary",)),
    )(kvf, kg, fb, dk, dv, dclf)


FOX_SCALE = HEAD64**-0.5


def _lane_col(block, lane_idx, h):
    return jnp.sum(jnp.where(lane_idx == h, block, 0.0), axis=-1, keepdims=True)


def _causal(tq, ext, i, transposed=False):
    if transposed:
        key = lax.broadcasted_iota(jnp.int32, (ext, tq), 0)
        qry = lax.broadcasted_iota(jnp.int32, (ext, tq), 1) + i * tq
    else:
        qry = lax.broadcasted_iota(jnp.int32, (tq, ext), 0) + i * tq
        key = lax.broadcasted_iota(jnp.int32, (tq, ext), 1)
    return key <= qry


def fox_fwd(proj, k_sh, v_sh, clf, clf_t, qg3, j_layer):
    s = proj.shape[0]
    npair = 6
    tq = TQ if s % TQ == 0 else s
    nq = s // tq

    def body(q_ref, gate_ref, k_ref, v_ref, clf_ref, clft_ref, qg_ref, main_ref, o_ref, lse_ref):
        j = pl.program_id(0)
        lane = lax.broadcasted_iota(jnp.int32, (1, LANES), 1)
        m0 = lane < HEAD64
        u, _ = _rms64(q_ref[...], m0)
        qn = u * qg_ref[...] * FOX_SCALE
        clfv = clf_ref[...]
        for hh in range(2):
            h = 2 * j + hh
            mh = m0 if hh == 0 else jnp.logical_not(m0)
            qh = bf(jnp.where(mh, qn, 0.0))
            dcol = _lane_col(clfv, lane, h)
            drow = clft_ref[pl.ds(h, 1), :]
            for i in range(nq):
                rows = slice(i * tq, (i + 1) * tq)
                ext = (i + 1) * tq
                sc = dot_nt(qh[rows], k_ref[0:ext, :]) + dcol[rows] - drow[:, :ext]
                sc = jnp.where(_causal(tq, ext, i), sc, -jnp.inf)
                m = jnp.max(sc, axis=-1, keepdims=True)
                p = jnp.exp(sc - m)
                lsum = jnp.sum(p, axis=-1, keepdims=True)
                pv = dot_nn(bf(p), v_ref[0:ext, :]) / lsum
                lse = m + jnp.log(lsum)
                if hh == 0:
                    o_ref[rows, :] = pv
                    lse_ref[rows, :] = jnp.where(lane == 0, lse, 0.0)
                else:
                    o_ref[rows, :] = jnp.where(mh, pv, o_ref[rows, :])
                    lse_ref[rows, :] = jnp.where(lane == 1, lse, lse_ref[rows, :])
        main_ref[...] = bf(o_ref[...] * _sigmoid(gate_ref[...]))

    blk = lambda off: pl.BlockSpec((s, LANES), lambda j: (0, off + j))
    return pl.pallas_call(
        body,
        name="fox_fwd",
        grid=(npair,),
        in_specs=[blk(0), blk(npair), blk(0), blk(0), _const((s, LANES)), _const((16, s)),
                  pl.BlockSpec((None, 1, LANES), lambda j: (j_layer, 0, 0))],
        out_specs=[blk(0)] * 3,
        out_shape=[jax.ShapeDtypeStruct((s, npair * LANES), BF16)] + [jax.ShapeDtypeStruct((s, npair * LANES), F32)] * 2,
        compiler_params=_cparams(("arbitrary",)),
    )(proj, proj, k_sh, v_sh, clf, clf_t, qg3)


def fox_bwd(proj, k_sh, v_sh, clf, clf_t, qg3, j_layer, o, lse, lse_t, dmixed, dk_in, dv_in, dclf_in):
    s = proj.shape[0]
    npair = 6
    tq = TQ if s % TQ == 0 else s
    nq = s // tq

    def body(q_ref, gate_ref, k_ref, v_ref, clf_ref, clft_ref, qg_ref, o_ref, lse_ref, lset_ref, dm_ref, dkin_ref, dvin_ref, dclfin_ref,
             dq_ref, dgate_ref, dk_ref, dv_ref, dclf_ref, dqg_ref, dqn_s, dcl_s):
        j = pl.program_id(0)
        lane = lax.broadcasted_iota(jnp.int32, (1, LANES), 1)
        m0 = lane < HEAD64
        qg = qg_ref[...]
        u, r = _rms64(q_ref[...], m0)
        qn = u * qg * FOX_SCALE
        ov = o_ref[...]
        gate = gate_ref[...]
        sg = _sigmoid(gate)
        dmain = dm_ref[...]
        do = dmain * sg
        dgate_ref[...] = dmain * ov * sg * (1.0 - sg)
        dk_ref[...] = dkin_ref[...]
        dv_ref[...] = dvin_ref[...]
        clfv = clf_ref[...]
        lsev = lse_ref[...]
        ones8 = jnp.ones((8, LANES), F32)

        @pl.when(j == 0)
        def _():
            dclf_ref[...] = dclfin_ref[...]
            dqg_ref[...] = jnp.zeros_like(dqg_ref)

        for hh in range(2):
            h = 2 * j + hh
            mh = m0 if hh == 0 else jnp.logical_not(m0)
            qh = bf(jnp.where(mh, qn, 0.0))
            doh = jnp.where(mh, do, 0.0)
            dohb = bf(doh)
            doo = doh * ov
            dcol = _lane_col(clfv, lane, h)
            drow = clft_ref[pl.ds(h, 1), :]
            lcol = _lane_col(lsev, lane, hh)
            lrow = lset_ref[pl.ds(h, 1), :]
            delta = jnp.sum(doo, axis=-1, keepdims=True)
            dcl_s[...] = jnp.zeros_like(dcl_s)
            for i in range(nq):
                rows = slice(i * tq, (i + 1) * tq)
                ext = (i + 1) * tq
                kk, vv = k_ref[0:ext, :], v_ref[0:ext, :]
                sc = dot_nt(qh[rows], kk) + dcol[rows] - drow[:, :ext]
                p = jnp.where(_causal(tq, ext, i), jnp.exp(sc - lcol[rows]), 0.0)
                ds = p * (dot_nt(dohb[rows], vv) - delta[rows])
                dqh = dot_nn(bf(ds), kk) * FOX_SCALE
                if hh == 0:
                    dqn_s[rows, :] = dqh
                else:
                    dqn_s[rows, :] = jnp.where(mh, dqh, dqn_s[rows, :])
                dcl_s[rows, :] += jnp.sum(ds, axis=-1, keepdims=True)
                sct = dot_nt(kk, qh[rows]) + drow[:, rows] - dcol[:ext]
                pt = jnp.where(_causal(tq, ext, i, transposed=True), jnp.exp(sct - lrow[:, rows]), 0.0)
                delta_row = dot_nt(ones8, doo[rows], HI)[0:1]
                dst = pt * (dot_nt(vv, dohb[rows]) - delta_row)
                dv_ref[0:ext, :] += dot_nn(bf(pt), dohb[rows])
                dk_ref[0:ext, :] += dot_nn(bf(dst), qh[rows])
                dcl_s[0:ext, :] -= jnp.sum(dst, axis=-1, keepdims=True)
            dclf_ref[...] += jnp.where(lane == h, dcl_s[...], 0.0)
        dqr, dqg = _rms64_bwd(dqn_s[...], u, r, qg, m0)
        dq_ref[...] = dqr
        dqg_ref[...] += dqg

    blk = lambda off: pl.BlockSpec((s, LANES), lambda j: (0, off + j))
    full = _const((s, LANES))
    return pl.pallas_call(
        body,
        name="fox_bwd",
        grid=(npair,),
        in_specs=[blk(0), blk(npair), blk(0), blk(0), full, _const((16, s)), pl.BlockSpec((None, 1, LANES), lambda j: (j_layer, 0, 0)),
                  blk(0), blk(0), _const((16, s)), blk(0), blk(0), blk(0), full],
        out_specs=[blk(0)] * 4 + [full, _const((1, LANES))],
        out_shape=[jax.ShapeDtypeStruct((s, npair * LANES), F32)] * 4
        + [jax.ShapeDtypeStruct((s, LANES), F32), jax.ShapeDtypeStruct((1, LANES), F32)],
        scratch_shapes=[pltpu.VMEM((s, LANES), F32), pltpu.VMEM((s, LANES), F32)],
        compiler_params=_cparams(("arbitrary",)),
    )(proj, proj, k_sh, v_sh, clf, clf_t, qg3, o, lse, lse_t, dmixed, dk_in, dv_in, dclf_in)


def loss_head(y, target):
    s, d = y.shape
    tm = _tok(s)

    def body(y_ref, t_ref, loss_ref, dy_ref):
        err = y_ref[...] - t_ref[...]
        dy_ref[...] = err * (1.0 / d)

        @pl.when(pl.program_id(0) == 0)
        def _():
            loss_ref[...] = jnp.zeros_like(loss_ref)

        part = jnp.sum(jnp.mean(err * err, axis=-1, keepdims=True), axis=0, keepdims=True)
        loss_ref[...] += 0.5 * part

    row = pl.BlockSpec((tm, d), lambda i: (i, 0))
    return pl.pallas_call(
        body,
        name="loss_head",
        grid=(s // tm,),
        in_specs=[row, row],
        out_specs=[_const((1, 1)), row],
        out_shape=[jax.ShapeDtypeStruct((1, 1), F32), jax.ShapeDtypeStruct((s, d), F32)],
        compiler_params=_cparams(("arbitrary",)),
    )(y, target)


def _row_tile(r, c, n_arrays):
    budget = VMEM_LIMIT_BYTES // 2
    padded_c = -(-c // LANES) * LANES
    best = None
    for t in range(8, r + 1, 8):
        if r % t == 0 and 2 * n_arrays * t * padded_c * 4 <= budget:
            best = t
    return r if best is None else best


def _as2d(a):
    return a.reshape(-1, a.shape[-1]) if a.ndim >= 2 else a.reshape(1, -1)


def adamw(w, gs, m, v):
    shape = w.shape
    ng = len(gs)
    w2 = _as2d(w)
    c = w2.shape[1]
    r = w2.shape[0] // ng
    w3, m3, v3 = (_as2d(t).reshape(ng, r, c) for t in (w, m, v))
    tr = _row_tile(r, c, 8)
    c1 = 1.0 - ADAM_B1**ADAM_STEP
    c2 = 1.0 - ADAM_B2**ADAM_STEP
    outs = None
    for k, g in enumerate(gs):
        n_prev = 0 if outs is None else 4

        def body(w_ref, g_ref, m_ref, v_ref, *rest, n_prev=n_prev):
            go_ref, d_ref, nm_ref, nv_ref = rest[n_prev:]
            gv = g_ref[...]
            nm = ADAM_B1 * m_ref[...] + (1.0 - ADAM_B1) * gv
            nv = ADAM_B2 * v_ref[...] + (1.0 - ADAM_B2) * (gv * gv)
            go_ref[...] = gv
            nm_ref[...] = nm
            nv_ref[...] = nv
            d_ref[...] = -ADAM_LR * ((nm / c1) / (jnp.sqrt(nv / c2) + ADAM_EPS) + ADAM_WD * w_ref[...])

        spec = pl.BlockSpec((None, tr, c), lambda i, k=k: (k, i, 0))
        outs = pl.pallas_call(
            body,
            name="adamw",
            grid=(r // tr,),
            in_specs=[spec, pl.BlockSpec((tr, c), lambda i: (i, 0)), spec, spec] + [ANY] * n_prev,
            out_specs=[spec] * 4,
            out_shape=[jax.ShapeDtypeStruct((ng, r, c), F32)] * 4,
            input_output_aliases={4 + i: i for i in range(n_prev)},
            compiler_params=_cparams(("arbitrary",)),
        )(w3, g.reshape(r, c), m3, v3, *([] if outs is None else outs))
    return tuple(t.reshape(shape) for t in outs)


def pair_sum(g, recv, c_arr):
    _, k, r, c = g.shape
    tr = _row_tile(r, c, 3)

    def body(c_ref, g_ref, r_ref, o_ref):
        o_ref[...] = bf(g_ref[...] + r_ref[...])

    return pl.pallas_call(
        body,
        name="pair_sum",
        grid_spec=pltpu.PrefetchScalarGridSpec(
            num_scalar_prefetch=1,
            grid=(k, r // tr),
            in_specs=[pl.BlockSpec((None, None, tr, c), lambda kk, i, cr: (cr[0], kk, i, 0)), pl.BlockSpec((None, tr, c), lambda kk, i, cr: (kk, i, 0))],
            out_specs=pl.BlockSpec((None, tr, c), lambda kk, i, cr: (kk, i, 0)),
        ),
        out_shape=jax.ShapeDtypeStruct((k, r, c), BF16),
        compiler_params=_cparams(("arbitrary", "arbitrary")),
    )(c_arr, g, recv)


def chip_sum(p, q, sel):
    _, r, c = p.shape
    tr = _row_tile(r, c, 4)

    def body(sel_ref, p_ref, q_ref, o_ref):
        acc = p_ref[...].astype(F32)
        for i in range(q.shape[0]):
            acc = acc + q_ref[i].astype(F32)
        o_ref[...] = acc

    return pl.pallas_call(
        body,
        name="chip_sum",
        grid_spec=pltpu.PrefetchScalarGridSpec(
            num_scalar_prefetch=1,
            grid=(r // tr,),
            in_specs=[pl.BlockSpec((None, tr, c), lambda i, sr: (sr[0], i, 0)), pl.BlockSpec((q.shape[0], tr, c), lambda i, sr: (0, i, 0))],
            out_specs=pl.BlockSpec((None, tr, c), lambda i, sr: (sr[1], i, 0)),
        ),
        out_shape=jax.ShapeDtypeStruct((2, r, c), F32),
        compiler_params=_cparams(("arbitrary",)),
    )(sel, p, q)


def cast_into_slot(w4, g, sel, dtype):
    _, _, r, c = w4.shape
    tr = _row_tile(r, c, 2)

    def body(sel_ref, w_ref, o_ref):
        o_ref[...] = w_ref[...].astype(dtype)

    return pl.pallas_call(
        body,
        name="cast_into_slot",
        grid_spec=pltpu.PrefetchScalarGridSpec(
            num_scalar_prefetch=1,
            grid=(2, r // tr),
            in_specs=[pl.BlockSpec((None, None, tr, c), lambda hf, i, sr: (g, hf, i, 0))],
            out_specs=pl.BlockSpec((None, None, tr, c), lambda hf, i, sr: (sr[0], hf, i, 0)),
        ),
        out_shape=jax.ShapeDtypeStruct((N_CHIPS, 2, r, c), dtype),
        compiler_params=_cparams(("arbitrary", "arbitrary")),
    )(sel, w4)


def _place():
    x, y, c = lax.axis_index("x"), lax.axis_index("y"), lax.axis_index("c")
    chips = [(1 - x, y), (x, 1 - y), (1 - x, 1 - y)]
    return x, y, c, 2 * x + y, chips, [2 * cx + cy for cx, cy in chips]


def _rcopy(src, dst, send, recv, dev):
    return pltpu.make_async_remote_copy(src_ref=src, dst_ref=dst, send_sem=send, recv_sem=recv, device_id=dev, device_id_type=MESH)


class Gather:
    def __init__(self, bufs):
        n = len(bufs)
        self.n = n
        self.args = list(bufs)
        self.out_shape = [jax.ShapeDtypeStruct(t.shape, t.dtype) for t in bufs]
        self.aliases = {a: a for a in range(n)}
        self.scratch = [pltpu.SemaphoreType.DMA((n, 6)), pltpu.SemaphoreType.DMA((n, 6))]

    def _sends(self, outs, send, recv):
        x, y, c, me, chips, _ = _place()
        cps = []
        for a in range(self.n):
            mine = outs[a].at[me, c]
            cps += [_rcopy(mine, mine, send.at[a, j], recv.at[a, j], (*chips[j], c)) for j in range(3)]
        return cps

    def start(self, ins, outs, scr):
        for cp in self._sends(outs, *scr):
            cp.start()

    def finish(self, ins, outs, scr):
        send, recv = scr
        x, y, c, me, chips, cidx = _place()
        sib = (x, y, 1 - c)
        passed = []
        for a in range(self.n):
            for j in range(3):
                landed = outs[a].at[cidx[j], c]
                _rcopy(landed, landed, send.at[a, j], recv.at[a, j], (*chips[j], c)).wait_recv()
                fwd = _rcopy(landed, landed, send.at[a, 3 + j], recv.at[a, 3 + j], sib)
                fwd.start()
                passed.append(fwd)
        for a in range(self.n):
            for j in range(3):
                theirs = outs[a].at[cidx[j], 1 - c]
                _rcopy(theirs, theirs, send.at[a, 3 + j], recv.at[a, 3 + j], sib).wait_recv()
        for cp in self._sends(outs, send, recv) + passed:
            cp.wait_send()


class PairExchange:
    def __init__(self, gs):
        n = len(gs)
        self.n = n
        self.args = list(gs)
        self.out_shape = [jax.ShapeDtypeStruct(t.shape[1:], t.dtype) for t in gs]
        self.aliases = {}
        self.scratch = [pltpu.SemaphoreType.DMA((n,)), pltpu.SemaphoreType.DMA((n,))]

    def _copies(self, ins, outs, send, recv):
        x, y, c = lax.axis_index("x"), lax.axis_index("y"), lax.axis_index("c")
        return [_rcopy(ins[a].at[1 - c], outs[a], send.at[a], recv.at[a], (x, y, 1 - c)) for a in range(self.n)]

    def start(self, ins, outs, scr):
        for cp in self._copies(ins, outs, *scr):
            cp.start()

    def finish(self, ins, outs, scr):
        for cp in self._copies(ins, outs, *scr):
            cp.wait()


class ChipExchange:
    def __init__(self, ps):
        n = len(ps)
        self.n = n
        self.args = list(ps)
        self.out_shape = [jax.ShapeDtypeStruct((3,) + t.shape[1:], t.dtype) for t in ps]
        self.aliases = {}
        self.scratch = [pltpu.SemaphoreType.DMA((n, 3)), pltpu.SemaphoreType.DMA((n, 3))]

    def _sends(self, ins, outs, send, recv):
        x, y, c, me, chips, cidx = _place()
        return [
            _rcopy(ins[a].at[cidx[j]], outs[a].at[j], send.at[a, j], recv.at[a, j], (*chips[j], c))
            for a in range(self.n)
            for j in range(3)
        ]

    def start(self, ins, outs, scr):
        for cp in self._sends(ins, outs, *scr):
            cp.start()

    def finish(self, ins, outs, scr):
        send, recv = scr
        x, y, c, me, chips, _ = _place()
        for a in range(self.n):
            for j in range(3):
                landed = outs[a].at[j]
                _rcopy(landed, landed, send.at[a, j], recv.at[a, j], (*chips[j], c)).wait_recv()
        for cp in self._sends(ins, outs, send, recv):
            cp.wait_send()


class PairShare:
    def __init__(self, bufs):
        n = len(bufs)
        self.n = n
        self.args = list(bufs)
        self.out_shape = [jax.ShapeDtypeStruct(t.shape, t.dtype) for t in bufs]
        self.aliases = {a: a for a in range(n)}
        self.scratch = [pltpu.SemaphoreType.DMA((n,)), pltpu.SemaphoreType.DMA((n,))]

    def _sends(self, outs, send, recv):
        x, y, c = lax.axis_index("x"), lax.axis_index("y"), lax.axis_index("c")
        return [_rcopy(outs[a].at[c], outs[a].at[c], send.at[a], recv.at[a], (x, y, 1 - c)) for a in range(self.n)]

    def start(self, ins, outs, scr):
        for cp in self._sends(outs, *scr):
            cp.start()

    def finish(self, ins, outs, scr):
        send, recv = scr
        x, y, c = lax.axis_index("x"), lax.axis_index("y"), lax.axis_index("c")
        for a in range(self.n):
            theirs = outs[a].at[1 - c]
            _rcopy(theirs, theirs, send.at[a], recv.at[a], (x, y, 1 - c)).wait_recv()
        for cp in self._sends(outs, send, recv):
            cp.wait_send()


def run_comm(comm, name):
    na, no = len(comm.args), len(comm.out_shape)

    def body(*refs):
        ins, outs, scr = refs[:na], refs[na : na + no], refs[na + no :]
        comm.start(ins, outs, scr)
        comm.finish(ins, outs, scr)

    return pl.pallas_call(
        body,
        name=name,
        in_specs=[ANY] * na,
        out_specs=[ANY] * no,
        out_shape=comm.out_shape,
        input_output_aliases=comm.aliases,
        scratch_shapes=comm.scratch,
    )(*comm.args)


def _carry(body, comm, *, name, grid, in_specs, out_specs, out_shape, scratch_shapes, args):
    params = _cparams(("arbitrary",))
    if comm is None:
        res = pl.pallas_call(body, name=name, grid=grid, in_specs=in_specs, out_specs=out_specs, out_shape=out_shape,
                             scratch_shapes=scratch_shapes, compiler_params=params)(*args)
        return res, None
    ni, no, ns = len(in_specs), len(out_specs), len(scratch_shapes)
    ci, co = len(comm.args), len(comm.out_shape)

    def wrapped(*refs):
        ins, c_ins = refs[:ni], refs[ni : ni + ci]
        p = ni + ci
        outs, c_outs = refs[p : p + no], refs[p + no : p + no + co]
        p += no + co
        scr, c_scr = refs[p : p + ns], refs[p + ns :]

        @pl.when(pl.program_id(0) == 0)
        def _():
            comm.start(c_ins, c_outs, c_scr)

        body(*ins, *outs, *scr)

        @pl.when(pl.program_id(0) == grid[0] - 1)
        def _():
            comm.finish(c_ins, c_outs, c_scr)

    res = pl.pallas_call(
        wrapped,
        name=name + "_carry",
        grid=grid,
        in_specs=list(in_specs) + [ANY] * ci,
        out_specs=list(out_specs) + [ANY] * co,
        out_shape=list(out_shape) + list(comm.out_shape),
        input_output_aliases={ni + i: no + o for i, o in comm.aliases.items()},
        scratch_shapes=list(scratch_shapes) + list(comm.scratch),
        compiler_params=params,
    )(*args, *comm.args)
    return res[:no], res[no:]


def small_allreduce(buf):
    r = buf.shape[0]

    def body(b_ref, o_ref, slots, send, recv):
        x, y, c = lax.axis_index("x"), lax.axis_index("y"), lax.axis_index("c")
        me = 4 * x + 2 * y + c
        slots[me] = b_ref[...]
        cps = []
        peers = []
        for mask in range(1, N_DEV):
            fx, fy, fc = (mask >> 2) & 1, (mask >> 1) & 1, mask & 1
            px, py, pc = (1 - x if fx else x), (1 - y if fy else y), (1 - c if fc else c)
            peers.append(4 * px + 2 * py + pc)
            cps.append(_rcopy(b_ref, slots.at[me], send.at[mask - 1], recv.at[mask - 1], (px, py, pc)))
        for cp in cps:
            cp.start()
        for k, pid in enumerate(peers):
            landed = slots.at[pid]
            _rcopy(landed, landed, send.at[k], recv.at[k], (x, y, c)).wait_recv()
        for cp in cps:
            cp.wait_send()
        acc = slots[0]
        for i in range(1, N_DEV):
            acc = acc + slots[i]
        o_ref[...] = acc

    vm = pl.BlockSpec(memory_space=pltpu.VMEM)
    return pl.pallas_call(
        body,
        name="small_allreduce",
        in_specs=[vm],
        out_specs=vm,
        out_shape=jax.ShapeDtypeStruct(buf.shape, F32),
        scratch_shapes=[pltpu.VMEM((N_DEV, r, LANES), F32), pltpu.SemaphoreType.DMA((N_DEV - 1,)), pltpu.SemaphoreType.DMA((N_DEV - 1,))],
    )(buf)


WEIGHT_NAMES = ["ffn1_norm", "ffn1_w_gate", "ffn1_w_up", "ffn1_w_down", "mix_norm", "mem_norm", "w_mem_kv", "mem_q_gain",
                "mem_k_gain", "w_in_a", "hgrn_lb_logits", "hgrn_o_gain", "w_in_b", "fox_q_gain", "kv_norm", "w_kv", "fox_f_bias",
                "fox_k_gain", "w_out", "ffn2_norm", "ffn2_w_gate", "ffn2_w_up", "ffn2_w_down"]
SHARDED = ["ffn1_w_gate", "ffn1_w_up", "ffn1_w_down", "w_mem_kv", "w_in_a", "w_in_b", "w_kv", "w_out", "ffn2_w_gate", "ffn2_w_up", "ffn2_w_down"]
SMALL = [n for n in WEIGHT_NAMES if n not in SHARDED]
FFN1 = ["ffn1_w_gate", "ffn1_w_up", "ffn1_w_down"]
FFN2 = ["ffn2_w_gate", "ffn2_w_up", "ffn2_w_down"]
PER_LAYER = FFN1 + FFN2 + ["w_mem_kv", "w_out"]
N_LAYERS, N_A = 4, 2
KV_PAD = 13 * LANES


def _halves(t):
    return t.reshape((2, t.shape[0] // 2) + t.shape[1:])


def _cols_from_chips(g):
    return jnp.moveaxis(g, 0, 2).reshape(g.shape[1], g.shape[2], N_CHIPS * g.shape[3])


def _rows_from_chips(g):
    return jnp.moveaxis(g, 0, 1).reshape(g.shape[1], N_CHIPS * g.shape[2], g.shape[3])


def _pair_tile(g):
    return jnp.tile(g, (1, 2)).reshape(g.shape[0], 1, LANES)


def _pair_fold(g):
    return g[:, :HEAD64] + g[:, HEAD64:]


def kernel(x, mem, ffn1_norm, ffn1_w_gate, ffn1_w_up, ffn1_w_down, mix_norm, mem_norm, w_mem_kv, mem_q_gain, mem_k_gain, w_in_a, hgrn_lb_logits, hgrn_o_gain, w_in_b, fox_q_gain, kv_norm, w_kv, fox_f_bias, fox_k_gain, w_out, ffn2_norm, ffn2_w_gate, ffn2_w_up, ffn2_w_down, loss_target, m_ffn1_norm, m_ffn1_w_gate, m_ffn1_w_up, m_ffn1_w_down, m_mix_norm, m_mem_norm, m_w_mem_kv, m_mem_q_gain, m_mem_k_gain, m_w_in_a, m_hgrn_lb_logits, m_hgrn_o_gain, m_w_in_b, m_fox_q_gain, m_kv_norm, m_w_kv, m_fox_f_bias, m_fox_k_gain, m_w_out, m_ffn2_norm, m_ffn2_w_gate, m_ffn2_w_up, m_ffn2_w_down, v_ffn1_norm, v_ffn1_w_gate, v_ffn1_w_up, v_ffn1_w_down, v_mix_norm, v_mem_norm, v_w_mem_kv, v_mem_q_gain, v_mem_k_gain, v_w_in_a, v_hgrn_lb_logits, v_hgrn_o_gain, v_w_in_b, v_fox_q_gain, v_kv_norm, v_w_kv, v_fox_f_bias, v_fox_k_gain, v_w_out, v_ffn2_norm, v_ffn2_w_gate, v_ffn2_w_up, v_ffn2_w_down):
    given = dict(locals())
    w = {n: given[n] for n in WEIGHT_NAMES}
    xs, mems, tgt = x[0], mem[0], loss_target[0]
    s, d = xs.shape
    my_chip = 2 * lax.axis_index("x") + lax.axis_index("y")
    sel = jnp.stack([my_chip, lax.axis_index("c")]).astype(jnp.int32)
    c_arr = sel[1:]

    def cast(n, g=0, ng=1):
        t = w[n]
        if t.ndim == 2:
            t4 = t.reshape(1, 2, t.shape[0] // 2, t.shape[1])
        else:
            t4 = t.reshape(ng, 2, (t.shape[0] // (2 * ng)) * t.shape[1], t.shape[2])
        return cast_into_slot(t4, g, sel, BF16)

    def view(buf, n):
        return buf.reshape((N_CHIPS, 2) + w[n].shape[1:]) if w[n].ndim == 3 else buf.reshape((N_CHIPS,) + w[n].shape)

    names0 = PER_LAYER + ["w_in_a", "w_kv"]
    bufs0 = [cast(n, 0, 2) for n in PER_LAYER] + [cast("w_in_a"), cast("w_kv")]
    bufs0.append(cast_into_slot(hgrn_lb_logits.reshape(1, 2, 1, -1), 0, sel, F32))
    got0 = run_comm(Gather(bufs0), "gather_group0")
    gw = [{n: view(b, n) for n, b in zip(names0, got0[:-1])}, None]
    buf1 = {n: cast(n, 1, 2) for n in PER_LAYER}
    buf1["w_in_b"] = cast("w_in_b")
    got1 = {}
    w_in = {"a": _cols_from_chips(gw[0]["w_in_a"])}
    w_kv_full = _cols_from_chips(gw[0]["w_kv"][:, None])
    w_kv_full = jnp.pad(w_kv_full, ((0, 0), (0, 0), (0, KV_PAD - w_kv_full.shape[-1])))
    w_mkv = [_rows_from_chips(gw[0]["w_mem_kv"]), None]
    w_o = [_rows_from_chips(gw[0]["w_out"]), None]
    logits3 = jnp.moveaxis(got0[-1].reshape(N_CHIPS, 2, -1), 0, 1).reshape(2, 1, -1)
    lb3 = lb_fwd(logits3)
    carried = {(0, "mix"): FFN1, (0, "ffn2"): ["w_in_b"], (1, "ffn1"): ["w_out", "w_mem_kv"], (1, "mix"): FFN2}

    def gather1(key):
        names = carried.get(key)
        return None if names is None else Gather([buf1[n] for n in names])

    def landed1(key, res):
        if res is not None:
            got1.update({n: view(b, n) for n, b in zip(carried[key], res)})

    norm3 = {n: w[n].reshape(N_LAYERS, 1, d) for n in ("ffn1_norm", "mix_norm", "mem_norm", "ffn2_norm")}
    kvn3 = kv_norm.reshape(1, 1, d)
    mqg3, mkg3 = _pair_tile(mem_q_gain), _pair_tile(mem_k_gain)
    og3 = hgrn_o_gain.reshape(N_A, 1, LANES)
    fqg3 = _pair_tile(fox_q_gain)
    fkg = jnp.tile(fox_k_gain, 2).reshape(1, LANES)
    fb = jnp.pad(fox_f_bias, (0, LANES - fox_f_bias.shape[0])).reshape(1, LANES)

    sv = [dict() for _ in range(N_LAYERS)]
    h = xs
    kv = None
    for l in range(N_LAYERS):
        t = sv[l]
        gi, li = l // 2, l % 2
        if l == N_A:
            gw[1] = got1
            w_in["b"] = _cols_from_chips(got1["w_in_b"])
            w_mkv[1], w_o[1] = _rows_from_chips(got1["w_mem_kv"]), _rows_from_chips(got1["w_out"])
        t["x0"] = h
        (h, t["a1"], t["b1"]), res = ffn_fwd(h, norm3["ffn1_norm"], l, *[gw[gi][n] for n in FFN1], li, comm=gather1((l, "ffn1")))
        landed1((l, "ffn1"), res)
        t["x1"] = h
        if l < N_A:
            t["proj"] = proj_fwd(h, norm3["mix_norm"], l, w_in["a"], l)
            (main, t["o"]), res = hgrn_fwd(t["proj"], lb3, og3, l, comm=gather1((l, "mix")))
            landed1((l, "mix"), res)
            t["qblk"] = 12
        else:
            t["proj"] = proj_fwd(h, norm3["mix_norm"], l, w_in["b"], l - N_A)
            main, t["o"], t["lse"] = fox_fwd(t["proj"], kv["k"], kv["v"], kv["clf"], kv["clf_t"], fqg3, l - N_A)
            t["qblk"] = 6
        t["kvm"] = proj_fwd(mems, norm3["mem_norm"], l, w_mkv[gi], li)
        memo = memattn_fwd(t["proj"], t["qblk"], t["kvm"], mqg3, mkg3, l)
        t["mixed"] = jnp.concatenate([main, memo], axis=-1)
        h = mm_res(h, t["mixed"], w_o[gi], li)
        t["x2"] = h
        (h, t["a2"], t["b2"]), res = ffn_fwd(h, norm3["ffn2_norm"], l, *[gw[gi][n] for n in FFN2], li, comm=gather1((l, "ffn2")))
        landed1((l, "ffn2"), res)
        if l == N_A - 1:
            kv = {"x": h, "kvf": proj_fwd(h, kvn3, 0, w_kv_full, 0)}
            kv["k"], kv["v"], kv["clf"] = kvprep_fwd(kv["kvf"], fkg, fb)
            kv["clf_t"] = kv["clf"][:, :16].T

    loss_local, dx = loss_head(h, tgt)

    nc = N_CHIPS
    fc = ffn1_w_gate.shape[-1]
    gbuf = [dict(), dict()]
    for gi in range(2):
        for n in FFN1[:2] + FFN2[:2]:
            gbuf[gi][n] = lax.empty((2, nc, 1, d, fc), F32)
        for n in (FFN1[2], FFN2[2]):
            gbuf[gi][n] = lax.empty((2, nc, 1, fc, d), F32)

    def col_layout(stack):
        lw, dd, nn = stack.shape
        return jnp.transpose(stack.reshape(lw, dd, nc, nn // nc), (0, 2, 1, 3))

    def row_layout(stack):
        lw, rr, cc = stack.shape
        return stack.reshape(lw, nc, rr // nc, cc)

    def group_layout(gi):
        lay = {n: b.reshape(2, nc, -1, b.shape[-1]) for n, b in gbuf[gi].items()}
        lay["w_mem_kv"] = row_layout(jnp.stack(dw_mkv[2 * gi : 2 * gi + 2]))
        lay["w_out"] = row_layout(jnp.stack(dw_o[2 * gi : 2 * gi + 2]))
        if gi == 0:
            lay["w_in_a"] = col_layout(jnp.stack(dw_in["a"]))
            kv_cols = w_kv.shape[-1] * nc
            lay["w_kv"] = jnp.transpose(dw_kv[:, :kv_cols].reshape(2, d // 2, nc, kv_cols // nc), (0, 2, 1, 3))
            names = PER_LAYER + ["w_in_a", "w_kv"]
        else:
            lay["w_in_b"] = col_layout(jnp.stack(dw_in["b"]))
            names = PER_LAYER + ["w_in_b"]
        return names, [lay[n] for n in names]

    n_ffn = len(FFN1) + len(FFN2)
    names1 = gl1 = partial1 = landed1a = mine1 = both1 = None
    dw_in = {"a": [None] * N_A, "b": [None] * (N_LAYERS - N_A)}
    dw_o, dw_mkv = [None] * N_LAYERS, [None] * N_LAYERS
    sg = {n: [None] * N_LAYERS for n in ("ffn1_norm", "mix_norm", "mem_norm", "ffn2_norm", "mem_q_gain", "mem_k_gain")}
    sg["hgrn_o_gain"], sg["fox_q_gain"], dlb = [None] * N_A, [None] * (N_LAYERS - N_A), [None] * N_A
    dk_sh = jnp.zeros((s, KV_MAIN), F32)
    dv_sh = jnp.zeros((s, KV_MAIN), F32)
    dclf = jnp.zeros((s, LANES), F32)
    zero_mem = jnp.zeros(mems.shape, F32)
    dw_kv = None
    for l in reversed(range(N_LAYERS)):
        t = sv[l]
        gi, li = l // 2, l % 2
        if l == N_A - 1:
            dkvf, dfkg, dfb = kvprep_bwd(kv["kvf"], fkg, fb, dk_sh, dv_sh, dclf)
            dx, sg["kv_norm"], xn_kv, dpb = proj_bwd(kv["x"], kvn3, 0, [dkvf], w_kv_full, 0, dx)
            dw_kv = wgrad(xn_kv, dpb)
            names1, gl1 = group_layout(1)
        comm = PairExchange(gl1) if l == 1 else (PairShare(mine1) if l == 0 else None)
        (dx, da, db, hm, xn, dyb, sg["ffn2_norm"][l]), res = ffn_bwd(t["x2"], norm3["ffn2_norm"], l, dx, t["a2"], t["b2"], *[gw[gi][n] for n in FFN2], li, comm=comm)
        if l == 1:
            partial1 = [pair_sum(g, r, c_arr) for g, r in zip(gl1, res)]
        if l == 0:
            both1 = res
        gbuf[gi]["ffn2_w_gate"] = wgrad(xn, da, buf=gbuf[gi]["ffn2_w_gate"], l=li)
        gbuf[gi]["ffn2_w_up"] = wgrad(xn, db, buf=gbuf[gi]["ffn2_w_up"], l=li)
        gbuf[gi]["ffn2_w_down"] = wgrad(hm, dyb, buf=gbuf[gi]["ffn2_w_down"], l=li)
        dmixed, dxb = mm_nt(dx, w_o[gi], li)
        dw_o[l] = wgrad(t["mixed"], dxb)
        dqm, dkvm, dmq, dmk = memattn_bwd(t["proj"], t["qblk"], t["kvm"], mqg3, mkg3, l, dmixed)
        sg["mem_q_gain"][l], sg["mem_k_gain"][l] = _pair_fold(dmq), _pair_fold(dmk)
        _, sg["mem_norm"][l], memn, dkvmb = proj_bwd(mems, norm3["mem_norm"], l, [dkvm], w_mkv[gi], li, zero_mem)
        dw_mkv[l] = wgrad(memn, dkvmb)
        if l < N_A:
            comm = ChipExchange(partial1[:n_ffn]) if l == 1 else None
            (dzq, dzf, dvi, dzg, dlb[l], sg["hgrn_o_gain"][l]), res = hgrn_bwd(t["proj"], lb3, og3, l, t["o"], dmixed, comm=comm)
            if l == 1:
                landed1a = list(res)
            parts, key, wl, tn = [dzq, dzf, dvi, dzg, dqm], "a", l, 13 * LANES
        else:
            lse_t = t["lse"].reshape(s, 6, LANES)[:, :, :2].reshape(s, 12).T
            lse_t = jnp.pad(lse_t, ((0, 4), (0, 0)))
            dq, dgate, dk_sh, dv_sh, dclf, dfq = fox_bwd(t["proj"], kv["k"], kv["v"], kv["clf"], kv["clf_t"], fqg3, l - N_A, t["o"], t["lse"], lse_t, dmixed, dk_sh, dv_sh, dclf)
            sg["fox_q_gain"][l - N_A] = _pair_fold(dfq)
            parts, key, wl, tn = [dq, dgate, dqm], "b", l - N_A, 7 * LANES
        dx, sg["mix_norm"][l], hn, dpb = proj_bwd(t["x1"], norm3["mix_norm"], l, parts, w_in[key], wl, dx)
        dw_in[key][wl] = wgrad(hn, dpb, tn=tn)
        comm = ChipExchange(partial1[n_ffn:]) if l == 1 else None
        (dx, da, db, hm, xn, dyb, sg["ffn1_norm"][l]), res = ffn_bwd(t["x0"], norm3["ffn1_norm"], l, dx, t["a1"], t["b1"], *[gw[gi][n] for n in FFN1], li, comm=comm)
        if l == 1:
            mine1 = [chip_sum(p, q, sel) for p, q in zip(partial1, landed1a + list(res))]
        gbuf[gi]["ffn1_w_gate"] = wgrad(xn, da, buf=gbuf[gi]["ffn1_w_gate"], l=li)
        gbuf[gi]["ffn1_w_up"] = wgrad(xn, db, buf=gbuf[gi]["ffn1_w_up"], l=li)
        gbuf[gi]["ffn1_w_down"] = wgrad(hm, dyb, buf=gbuf[gi]["ffn1_w_down"], l=li)

    names0, gl0 = group_layout(0)
    recv0 = run_comm(PairExchange(gl0), "pair_exchange")
    partial0 = [pair_sum(g, r, c_arr) for g, r in zip(gl0, recv0)]
    landed0 = run_comm(ChipExchange(partial0), "chip_exchange")
    mine0 = [chip_sum(p, q, sel) for p, q in zip(partial0, landed0)]
    both0 = run_comm(PairShare(mine0), "pair_share")
    reduced = [dict(zip(names0, both0)), dict(zip(names1, both1))]
    gparts = {n: [reduced[0][n], reduced[1][n]] for n in PER_LAYER}
    gparts.update({"w_in_a": [reduced[0]["w_in_a"]], "w_kv": [reduced[0]["w_kv"]], "w_in_b": [reduced[1]["w_in_b"]]})

    dlogits = lb_bwd(logits3, dlb[1]).reshape(2, -1)
    small = {
        "ffn1_norm": jnp.concatenate(sg["ffn1_norm"]), "mix_norm": jnp.concatenate(sg["mix_norm"]),
        "mem_norm": jnp.concatenate(sg["mem_norm"]), "ffn2_norm": jnp.concatenate(sg["ffn2_norm"]),
        "mem_q_gain": jnp.concatenate(sg["mem_q_gain"]), "mem_k_gain": jnp.concatenate(sg["mem_k_gain"]),
        "hgrn_o_gain": jnp.concatenate(sg["hgrn_o_gain"]), "fox_q_gain": jnp.concatenate(sg["fox_q_gain"]),
        "kv_norm": sg["kv_norm"], "fox_f_bias": dfb[:, : fox_f_bias.shape[0]], "fox_k_gain": _pair_fold(dfkg),
        "hgrn_lb_logits": dlogits,
    }
    flat = [small[n].reshape(-1) for n in SMALL] + [loss_local.reshape(-1)]
    sizes = [f.shape[0] for f in flat]
    total = sum(sizes)
    padded = -(-total // (8 * LANES)) * (8 * LANES)
    packed = jnp.pad(jnp.concatenate(flat), (0, padded - total)).reshape(-1, LANES)
    summed = small_allreduce(packed).reshape(-1)
    off = 0
    for n, sz in zip(SMALL, sizes[:-1]):
        gparts[n] = [summed[off : off + sz].reshape(dlogits.shape if n == "hgrn_lb_logits" else w[n].shape)]
        off += sz
    loss = summed[off]
    lbw = hgrn_lb_logits.shape[1]
    gparts["hgrn_lb_logits"] = [lax.dynamic_slice_in_dim(gparts["hgrn_lb_logits"][0], my_chip * lbw, lbw, axis=1)]

    grads, delta, new_m, new_v = {}, {}, {}, {}
    for n in WEIGHT_NAMES:
        grads[n], delta[n], new_m[n], new_v[n] = adamw(w[n], gparts[n], given["m_" + n], given["v_" + n])
    return (loss, dx[None], *[grads[n] for n in WEIGHT_NAMES], *[delta[n] for n in WEIGHT_NAMES],
            *[new_m[n] for n in WEIGHT_NAMES], *[new_v[n] for n in WEIGHT_NAMES])
```

```python
import functools

import jax
import jax.numpy as jnp
from jax import lax
from jax.experimental import pallas as pl
from jax.experimental.pallas import tpu as pltpu

F32, BF16 = jnp.float32, jnp.bfloat16
HI = lax.Precision.HIGHEST
EPS = 1e-6
MESH = pl.DeviceIdType.MESH
ANY = pl.BlockSpec(memory_space=pl.ANY)

VMEM_LIMIT_BYTES = 56 << 20
N_CHIPS = 4
N_DEV = 8
LANES = 128
HEAD64 = 64
CHUNK = 64
SUB = 16
HGRN_HEADS_PER_STEP = 2
TQ = 256
TOK = 256

ADAM_LR, ADAM_B1, ADAM_B2, ADAM_EPS, ADAM_WD, ADAM_STEP = 0.001, 0.9, 0.999, 1e-08, 0.01, 10


def _cparams(sem=None):
    return pltpu.CompilerParams(dimension_semantics=sem, vmem_limit_bytes=VMEM_LIMIT_BYTES)


def _mm(a, b, dims, prec=None):
    return lax.dot_general(a, b, (dims, ((), ())), preferred_element_type=F32, precision=prec)


def dot_nn(a, b, prec=None):
    return _mm(a, b, ((1,), (0,)), prec)


def dot_nt(a, b, prec=None):
    return _mm(a, b, ((1,), (1,)), prec)


def dot_tn(a, b, prec=None):
    return _mm(a, b, ((0,), (0,)), prec)


def bf(v):
    return v.astype(BF16)


def _sigmoid(z):
    return jax.nn.sigmoid(z)


def _dsilu(z, s):
    return s * (1.0 + z * (1.0 - s))


def _rms(x):
    r = lax.rsqrt(jnp.mean(x * x, axis=-1, keepdims=True) + EPS)
    return x * r, r


def _rms_bwd(dxn, u, r, g):
    du = dxn * g
    dx = r * (du - u * jnp.mean(du * u, axis=-1, keepdims=True))
    return dx, jnp.sum(dxn * u, axis=0, keepdims=True)


def _lane_mask0(shape):
    return lax.broadcasted_iota(jnp.int32, shape, len(shape) - 1) < HEAD64


def _rms64(x, m0):
    sq = x * x
    s0 = jnp.sum(jnp.where(m0, sq, 0.0), axis=-1, keepdims=True)
    s1 = jnp.sum(jnp.where(m0, 0.0, sq), axis=-1, keepdims=True)
    r = lax.rsqrt(jnp.where(m0, s0, s1) * (1.0 / HEAD64) + EPS)
    return x * r, r


def _rms64_bwd(dxn, u, r, g, m0):
    du = dxn * g
    t = du * u
    t0 = jnp.sum(jnp.where(m0, t, 0.0), axis=-1, keepdims=True)
    t1 = jnp.sum(jnp.where(m0, 0.0, t), axis=-1, keepdims=True)
    dx = r * (du - u * (jnp.where(m0, t0, t1) * (1.0 / HEAD64)))
    return dx, jnp.sum(dxn * u, axis=0, keepdims=True)


def _tok(s):
    return TOK if s % TOK == 0 else s


def _const(shape):
    return pl.BlockSpec(shape, lambda *_: (0,) * len(shape))


def ffn_fwd(x, gain3, l, wg, wu, wd, wl, comm=None):
    s, d = x.shape
    nc, _, _, fc = wg.shape
    tm = _tok(s)

    def body(x_ref, g_ref, wg_ref, wu_ref, wd_ref, xo_ref, a_ref, b_ref):
        xv = x_ref[...]
        u, _ = _rms(xv)
        xn = bf(u * g_ref[...])
        y = jnp.zeros((tm, d), F32)
        for c in range(nc):
            a = dot_nn(xn, wg_ref[c])
            b = dot_nn(xn, wu_ref[c])
            a_ref[c] = bf(a)
            b_ref[c] = bf(b)
            y = y + dot_nn(bf(a * _sigmoid(a) * b), wd_ref[c])
        xo_ref[...] = xv + 0.5 * y

    wspec = pl.BlockSpec((nc, None, d, fc), lambda i: (0, wl, 0, 0), pipeline_mode=pl.Buffered(1))
    wdspec = pl.BlockSpec((nc, None, fc, d), lambda i: (0, wl, 0, 0), pipeline_mode=pl.Buffered(1))
    row = pl.BlockSpec((tm, d), lambda i: (i, 0))
    act = pl.BlockSpec((nc, tm, fc), lambda i: (0, i, 0))
    return _carry(
        body,
        comm,
        name="ffn_fwd",
        grid=(s // tm,),
        in_specs=[row, pl.BlockSpec((None, 1, d), lambda i: (l, 0, 0)), wspec, wspec, wdspec],
        out_specs=[row, act, act],
        out_shape=[
            jax.ShapeDtypeStruct((s, d), F32),
            jax.ShapeDtypeStruct((nc, s, fc), BF16),
            jax.ShapeDtypeStruct((nc, s, fc), BF16),
        ],
        scratch_shapes=[],
        args=(x, gain3, wg, wu, wd),
    )


def ffn_bwd(x, gain3, l, dout, a, b, wg, wu, wd, wl, comm=None):
    s, d = x.shape
    nc, _, _, fc = wg.shape
    tm = _tok(s)

    def body(x_ref, g_ref, do_ref, a_ref, b_ref, wg_ref, wu_ref, wd_ref, dx_ref, da_ref, db_ref, hm_ref, xn_ref, dy_ref, dg_ref):
        xv = x_ref[...]
        g = g_ref[...]
        u, r = _rms(xv)
        xn_ref[...] = bf(u * g)
        dout = do_ref[...]
        dy = bf(0.5 * dout)
        dy_ref[...] = dy
        dxn = jnp.zeros((tm, d), F32)
        for c in range(nc):
            av = a_ref[c].astype(F32)
            bv = b_ref[c].astype(F32)
            sg = _sigmoid(av)
            sl = av * sg
            dh = dot_nt(dy, wd_ref[c])
            da = bf(dh * bv * _dsilu(av, sg))
            db = bf(dh * sl)
            da_ref[c] = da
            db_ref[c] = db
            hm_ref[c] = bf(sl * bv)
            dxn = dxn + dot_nt(da, wg_ref[c]) + dot_nt(db, wu_ref[c])
        dx, dg = _rms_bwd(dxn, u, r, g)
        dx_ref[...] = dout + dx

        @pl.when(pl.program_id(0) == 0)
        def _():
            dg_ref[...] = jnp.zeros_like(dg_ref)

        dg_ref[...] += dg

    wspec = pl.BlockSpec((nc, None, d, fc), lambda i: (0, wl, 0, 0), pipeline_mode=pl.Buffered(1))
    wdspec = pl.BlockSpec((nc, None, fc, d), lambda i: (0, wl, 0, 0), pipeline_mode=pl.Buffered(1))
    row = pl.BlockSpec((tm, d), lambda i: (i, 0))
    act = pl.BlockSpec((nc, tm, fc), lambda i: (0, i, 0))
    act_shape = jax.ShapeDtypeStruct((nc, s, fc), BF16)
    return _carry(
        body,
        comm,
        name="ffn_bwd",
        grid=(s // tm,),
        in_specs=[row, pl.BlockSpec((None, 1, d), lambda i: (l, 0, 0)), row, act, act, wspec, wspec, wdspec],
        out_specs=[row, act, act, act, row, row, _const((1, d))],
        out_shape=[
            jax.ShapeDtypeStruct((s, d), F32),
            act_shape,
            act_shape,
            act_shape,
            jax.ShapeDtypeStruct((s, d), BF16),
            jax.ShapeDtypeStruct((s, d), BF16),
            jax.ShapeDtypeStruct((1, d), F32),
        ],
        scratch_shapes=[],
        args=(x, gain3, dout, a, b, wg, wu, wd),
    )


def wgrad(a, b, tn=None, buf=None, l=None):
    ca = a.shape[0] if a.ndim == 3 else 1
    cb = b.shape[0] if b.ndim == 3 else 1
    nc = max(ca, cb)
    s, m = a.shape[-2:]
    n = b.shape[-1]
    tn = n if tn is None else tn
    assert n % tn == 0

    def body(*refs):
        a_ref, b_ref, o_ref = refs[-3:] if buf is None else (refs[0], refs[1], refs[3])
        o_ref[...] = dot_tn(a_ref[...], b_ref[...])

    a_spec = pl.BlockSpec((None, s, m), lambda c, j: (c, 0, 0)) if a.ndim == 3 else pl.BlockSpec((s, m), lambda c, j: (0, 0))
    b_spec = pl.BlockSpec((None, s, tn), lambda c, j: (c, 0, j)) if b.ndim == 3 else pl.BlockSpec((s, tn), lambda c, j: (0, j))
    if buf is None:
        assert nc == 1
        return pl.pallas_call(
            body,
            name="wgrad",
            grid=(1, n // tn),
            in_specs=[a_spec, b_spec],
            out_specs=pl.BlockSpec((m, tn), lambda c, j: (0, j)),
            out_shape=jax.ShapeDtypeStruct((m, n), F32),
            compiler_params=_cparams(("arbitrary", "arbitrary")),
        )(a, b)
    lh = buf.shape[2]
    hi, lo = l // lh, l % lh
    o_spec = pl.BlockSpec((None, None, None, m, tn), lambda c, j: (hi, c, lo, 0, j))
    return pl.pallas_call(
        body,
        name="wgrad_buf",
        grid=(nc, n // tn),
        in_specs=[a_spec, b_spec, ANY],
        out_specs=o_spec,
        out_shape=jax.ShapeDtypeStruct(buf.shape, F32),
        input_output_aliases={2: 0},
        compiler_params=_cparams(("arbitrary", "arbitrary")),
    )(a, b, buf)


def proj_fwd(x, gain3, l, w, wl):
    s, d = x.shape
    n = w.shape[-1]
    tm = _tok(s)

    def body(x_ref, g_ref, w_ref, o_ref):
        u, _ = _rms(x_ref[...])
        o_ref[...] = dot_nn(bf(u * g_ref[...]), w_ref[...])

    return pl.pallas_call(
        body,
        name="proj_fwd",
        grid=(s // tm,),
        in_specs=[
            pl.BlockSpec((tm, d), lambda i: (i, 0)),
            pl.BlockSpec((None, 1, d), lambda i: (l, 0, 0)),
            pl.BlockSpec((None, d, n), lambda i: (wl, 0, 0)),
        ],
        out_specs=pl.BlockSpec((tm, n), lambda i: (i, 0)),
        out_shape=jax.ShapeDtypeStruct((s, n), F32),
        compiler_params=_cparams(("arbitrary",)),
    )(x, gain3, w)


def proj_bwd(x, gain3, l, parts, w, wl, dx_in):
    s, d = x.shape
    n = w.shape[-1]
    widths = [p.shape[1] for p in parts]
    assert sum(widths) == n
    tm = _tok(s)
    npart = len(parts)

    def body(*refs):
        x_ref, g_ref, w_ref, dxin_ref = refs[:4]
        p_refs = refs[4 : 4 + npart]
        dx_ref, dg_ref, xn_ref, dpb_ref = refs[4 + npart :]
        g = g_ref[...]
        u, r = _rms(x_ref[...])
        xn_ref[...] = bf(u * g)
        dxn = jnp.zeros((tm, d), F32)
        off = 0
        for p_ref, wd_ in zip(p_refs, widths):
            dp = bf(p_ref[...])
            dpb_ref[:, off : off + wd_] = dp
            dxn = dxn + dot_nt(dp, w_ref[:, off : off + wd_])
            off += wd_
        dx, dg = _rms_bwd(dxn, u, r, g)
        dx_ref[...] = dxin_ref[...] + dx

        @pl.when(pl.program_id(0) == 0)
        def _():
            dg_ref[...] = jnp.zeros_like(dg_ref)

        dg_ref[...] += dg

    row = pl.BlockSpec((tm, d), lambda i: (i, 0))
    return pl.pallas_call(
        body,
        name="proj_bwd",
        grid=(s // tm,),
        in_specs=[row, pl.BlockSpec((None, 1, d), lambda i: (l, 0, 0)), pl.BlockSpec((None, d, n), lambda i: (wl, 0, 0)), row]
        + [pl.BlockSpec((tm, wd_), lambda i: (i, 0)) for wd_ in widths],
        out_specs=[row, _const((1, d)), row, pl.BlockSpec((tm, n), lambda i: (i, 0))],
        out_shape=[
            jax.ShapeDtypeStruct((s, d), F32),
            jax.ShapeDtypeStruct((1, d), F32),
            jax.ShapeDtypeStruct((s, d), BF16),
            jax.ShapeDtypeStruct((s, n), BF16),
        ],
        compiler_params=_cparams(("arbitrary",)),
    )(x, gain3, w, dx_in, *parts)


def mm_res(x, a, w, l):
    s, d = x.shape
    k = a.shape[1]
    tm = _tok(s)

    def body(x_ref, a_ref, w_ref, o_ref):
        o_ref[...] = x_ref[...] + dot_nn(a_ref[...], w_ref[...])

    return pl.pallas_call(
        body,
        name="mm_res",
        grid=(s // tm,),
        in_specs=[
            pl.BlockSpec((tm, d), lambda i: (i, 0)),
            pl.BlockSpec((tm, k), lambda i: (i, 0)),
            pl.BlockSpec((None, k, d), lambda i: (l, 0, 0)),
        ],
        out_specs=pl.BlockSpec((tm, d), lambda i: (i, 0)),
        out_shape=jax.ShapeDtypeStruct((s, d), F32),
        compiler_params=_cparams(("arbitrary",)),
    )(x, a, w)


def mm_nt(dx, w, l):
    s, d = dx.shape
    k = w.shape[1]
    tm = _tok(s)

    def body(dx_ref, w_ref, o_ref, dxb_ref):
        dxb = bf(dx_ref[...])
        dxb_ref[...] = dxb
        o_ref[...] = dot_nt(dxb, w_ref[...])

    return pl.pallas_call(
        body,
        name="mm_nt",
        grid=(s // tm,),
        in_specs=[pl.BlockSpec((tm, d), lambda i: (i, 0)), pl.BlockSpec((None, k, d), lambda i: (l, 0, 0))],
        out_specs=[pl.BlockSpec((tm, k), lambda i: (i, 0)), pl.BlockSpec((tm, d), lambda i: (i, 0))],
        out_shape=[jax.ShapeDtypeStruct((s, k), F32), jax.ShapeDtypeStruct((s, d), BF16)],
        compiler_params=_cparams(("arbitrary",)),
    )(dx, w)


def lb_fwd(logits3):
    def body(l_ref, o_ref):
        l0, l1 = l_ref[0], l_ref[1]
        m = jnp.maximum(l0, l1)
        e0, e1 = jnp.exp(l0 - m), jnp.exp(l1 - m)
        p0, p1 = e0 / (e0 + e1), e1 / (e0 + e1)
        o_ref[0] = p0 - p0
        o_ref[1] = (p0 + p1) - p0

    return pl.pallas_call(body, name="lb_fwd", out_shape=jax.ShapeDtypeStruct(logits3.shape, F32))(logits3)


def lb_bwd(logits3, dlb1):
    def body(l_ref, d_ref, o_ref):
        l0, l1 = l_ref[0], l_ref[1]
        m = jnp.maximum(l0, l1)
        e0, e1 = jnp.exp(l0 - m), jnp.exp(l1 - m)
        p0, p1 = e0 / (e0 + e1), e1 / (e0 + e1)
        t = d_ref[...] * p0 * p1
        o_ref[0] = -t
        o_ref[1] = t

    return pl.pallas_call(body, name="lb_bwd", out_shape=jax.ShapeDtypeStruct(logits3.shape, F32))(logits3, dlb1)


def _hgrn_gates(zq, zf, lb):
    sf = _sigmoid(zf)
    f = lb + (1.0 - lb) * sf
    sq = _sigmoid(zq)
    return sf, f, jnp.log(f), 1.0 - f, sq, zq * sq


def _tri(n, upper=False):
    r = lax.broadcasted_iota(jnp.int32, (n, n), 0)
    c = lax.broadcasted_iota(jnp.int32, (n, n), 1)
    return jnp.where((c >= r) if upper else (r >= c), 1.0, 0.0).astype(F32)


def hgrn_fwd(proj, lb3, og3, l, comm=None):
    s = proj.shape[0]
    nh = 6
    n_chunk = s // CHUNK
    nsub = CHUNK // SUB

    hb = HGRN_HEADS_PER_STEP
    wide = hb * LANES

    def body(zq_ref, zf_ref, vi_ref, zg_ref, lb_ref, og_ref, main_ref, o_ref, q_a, k_a, v_a, c_a):
        og = og_ref[...]
        tril = _tri(CHUNK)
        rowi = lax.broadcasted_iota(jnp.int32, (SUB, LANES), 0)

        def one_head(hd, rows, st):
            cols = slice(hd * LANES, (hd + 1) * LANES)
            q_s, k_s, v_s, c_s = q_a.at[hd], k_a.at[hd], v_a.at[hd], c_a.at[hd]
            zg = zg_ref[rows, cols]
            _, _, lf, k, _, q = _hgrn_gates(zq_ref[rows, cols], zf_ref[rows, cols], lb_ref[:, cols])
            v = vi_ref[rows, cols]
            c = dot_nn(tril, lf, HI)
            q_s[...] = q
            k_s[...] = k
            v_s[...] = v
            c_s[...] = c
            o_inter = dot_nt(q * jnp.exp(c), st, HI)
            parts = []
            for i in range(nsub):
                lo = i * SUB
                blk = pl.ds(lo, SUB)
                qb, cb = q_s[blk, :], c_s[blk, :]
                ob = o_inter[lo : lo + SUB]
                if i > 0:
                    rr = c_s[pl.ds(lo - 1, 1), :]
                    qt = qb * jnp.exp(cb - rr)
                    kt = k_s[pl.ds(0, lo), :] * jnp.exp(rr - c_s[pl.ds(0, lo), :])
                    ob = ob + dot_nn(dot_nt(qt, kt, HI), v_s[pl.ds(0, lo), :], HI)
                for t in range(SUB):
                    e = jnp.where(rowi >= t, jnp.exp(cb - c_s[pl.ds(lo + t, 1), :]), 0.0)
                    a = jnp.sum(qb * k_s[pl.ds(lo + t, 1), :] * e, axis=-1, keepdims=True)
                    ob = ob + a * v_s[pl.ds(lo + t, 1), :]
                parts.append(ob)
            o = jnp.concatenate(parts, axis=0)
            ce = c_s[pl.ds(CHUNK - 1, 1), :]
            st = st * jnp.exp(ce) + dot_tn(v, k * jnp.exp(ce - c), HI)
            on, _ = _rms(o)
            o_ref[rows, cols] = o
            main_ref[rows, cols] = bf(on * og * (zg * _sigmoid(zg)))
            return st

        def chunk(ci, sts):
            rows = pl.ds(pl.multiple_of(ci * CHUNK, CHUNK), CHUNK)
            return tuple(one_head(hd, rows, sts[hd]) for hd in range(hb))

        lax.fori_loop(0, n_chunk, chunk, tuple(jnp.zeros((LANES, LANES), F32) for _ in range(hb)))

    def col(k):
        return pl.BlockSpec((s, wide), lambda h: (0, k * (nh // hb) + h))

    vec = pl.BlockSpec((None, 1, wide), lambda h: (l, 0, h))
    return _carry(
        body,
        comm,
        name="hgrn_fwd",
        grid=(nh // hb,),
        in_specs=[col(0), col(1), col(2), col(3), vec, pl.BlockSpec((None, 1, LANES), lambda h: (l, 0, 0))],
        out_specs=[pl.BlockSpec((s, wide), lambda h: (0, h))] * 2,
        out_shape=[jax.ShapeDtypeStruct((s, nh * LANES), BF16), jax.ShapeDtypeStruct((s, nh * LANES), F32)],
        scratch_shapes=[pltpu.VMEM((hb, CHUNK, LANES), F32)] * 4,
        args=(proj, proj, proj, proj, lb3, og3),
    )


def hgrn_bwd(proj, lb3, og3, l, o, dmixed, comm=None):
    s = proj.shape[0]
    nh = 6
    n_chunk = s // CHUNK
    nsub = CHUNK // SUB

    hb = HGRN_HEADS_PER_STEP
    wide = hb * LANES

    def body(zq_ref, zf_ref, vi_ref, zg_ref, lb_ref, og_ref, o_ref, dm_ref,
             dzq_ref, dzf_ref, dvi_ref, dzg_ref, dlb_ref, dog_ref,
             st_a, q_a, k_a, v_a, c_a, do_a, dq_a, dk_a, dv_a, acc_a):
        og = og_ref[...]
        tril = _tri(CHUNK)
        triu = _tri(CHUNK, upper=True)
        rowi = lax.broadcasted_iota(jnp.int32, (SUB, LANES), 0)

        def fwd_head(hd, ci, rows, st):
            cols = slice(hd * LANES, (hd + 1) * LANES)
            _, _, lf, k, _, _ = _hgrn_gates(zq_ref[rows, cols], zf_ref[rows, cols], lb_ref[:, cols])
            c = dot_nn(tril, lf, HI)
            ce = jnp.sum(lf, axis=0, keepdims=True)
            st_a[hd, ci] = st
            return st * jnp.exp(ce) + dot_tn(vi_ref[rows, cols], k * jnp.exp(ce - c), HI)

        def fwd_chunk(ci, sts):
            rows = pl.ds(pl.multiple_of(ci * CHUNK, CHUNK), CHUNK)
            return tuple(fwd_head(hd, ci, rows, sts[hd]) for hd in range(hb))

        lax.fori_loop(0, n_chunk, fwd_chunk, tuple(jnp.zeros((LANES, LANES), F32) for _ in range(hb)))
        acc_a[...] = jnp.zeros_like(acc_a)

        def bwd_head(hd, ci, rows, carry):
            dst, cg = carry
            cols = slice(hd * LANES, (hd + 1) * LANES)
            q_s, k_s, v_s, c_s, do_s = q_a.at[hd], k_a.at[hd], v_a.at[hd], c_a.at[hd], do_a.at[hd]
            dq_s, dk_s, dv_s, acc_s = dq_a.at[hd], dk_a.at[hd], dv_a.at[hd], acc_a.at[hd]
            lb = lb_ref[:, cols]
            zq, zf, zg = zq_ref[rows, cols], zf_ref[rows, cols], zg_ref[rows, cols]
            sf, f, lf, k, sq, q = _hgrn_gates(zq, zf, lb)
            v = vi_ref[rows, cols]
            c = dot_nn(tril, lf, HI)
            st = st_a[hd, ci]
            on, r = _rms(o_ref[rows, cols])
            sg = _sigmoid(zg)
            dmain = dm_ref[rows, cols]
            dy = dmain * (zg * sg)
            dzg_ref[rows, cols] = dmain * (on * og) * _dsilu(zg, sg)
            do, dog = _rms_bwd(dy, on, r, og)
            acc_s[pl.ds(0, 1), :] += dog
            q_s[...] = q
            k_s[...] = k
            v_s[...] = v
            c_s[...] = c
            do_s[...] = do
            ce = c_s[pl.ds(CHUNK - 1, 1), :]
            eq = jnp.exp(c)
            ek = jnp.exp(ce - c)
            qt_all = q * eq
            dq_s[...] = dot_nn(do, st, HI) * eq
            dv_s[...] = dot_nt(k * ek, dst, HI)
            dk_s[...] = dot_nn(v, dst, HI) * ek
            dst = dst * jnp.exp(ce) + dot_tn(do, qt_all, HI)
            for i in range(nsub):
                lo = i * SUB
                blk = pl.ds(lo, SUB)
                qb, cb, dob = q_s[blk, :], c_s[blk, :], do_s[blk, :]
                if i > 0:
                    prev = pl.ds(0, lo)
                    rr = c_s[pl.ds(lo - 1, 1), :]
                    eqi = jnp.exp(cb - rr)
                    eki = jnp.exp(rr - c_s[prev, :])
                    qt = qb * eqi
                    kt = k_s[prev, :] * eki
                    amat = dot_nt(qt, kt, HI)
                    damat = dot_nt(dob, v_s[prev, :], HI)
                    dv_s[prev, :] += dot_tn(amat, dob, HI)
                    dq_s[blk, :] += dot_nn(damat, kt, HI) * eqi
                    dk_s[prev, :] += dot_tn(damat, qt, HI) * eki
                dqb = jnp.zeros((SUB, LANES), F32)
                for t in range(SUB):
                    row = pl.ds(lo + t, 1)
                    e = jnp.where(rowi >= t, jnp.exp(cb - c_s[row, :]), 0.0)
                    kr = k_s[row, :]
                    a = jnp.sum(qb * kr * e, axis=-1, keepdims=True)
                    da = jnp.sum(dob * v_s[row, :], axis=-1, keepdims=True)
                    dv_s[row, :] += jnp.sum(a * dob, axis=0, keepdims=True)
                    dqb = dqb + da * kr * e
                    dk_s[row, :] += jnp.sum(da * qb * e, axis=0, keepdims=True)
                dq_s[blk, :] += dqb
            dq, dk = dq_s[...], dk_s[...]
            dg = q * dq - k * dk
            dlf = dot_nn(triu, dg, HI) + cg
            cg = cg + jnp.sum(dg, axis=0, keepdims=True)
            df = dlf / f - dk
            dzf_ref[rows, cols] = df * (1.0 - lb) * sf * (1.0 - sf)
            acc_s[pl.ds(1, 1), :] += jnp.sum(df * (1.0 - sf), axis=0, keepdims=True)
            dzq_ref[rows, cols] = dq * _dsilu(zq, sq)
            dvi_ref[rows, cols] = dv_s[...]
            return dst, cg

        def bwd_chunk(jj, carries):
            ci = n_chunk - 1 - jj
            rows = pl.ds(pl.multiple_of(ci * CHUNK, CHUNK), CHUNK)
            return tuple(bwd_head(hd, ci, rows, carries[hd]) for hd in range(hb))

        zero = (jnp.zeros((LANES, LANES), F32), jnp.zeros((1, LANES), F32))
        lax.fori_loop(0, n_chunk, bwd_chunk, tuple(zero for _ in range(hb)))

        @pl.when(pl.program_id(0) == 0)
        def _():
            dog_ref[...] = jnp.zeros_like(dog_ref)

        for hd in range(hb):
            dlb_ref[:, hd * LANES : (hd + 1) * LANES] = acc_a[hd, pl.ds(1, 1), :]
            dog_ref[...] += acc_a[hd, pl.ds(0, 1), :]

    def col(k):
        return pl.BlockSpec((s, wide), lambda h: (0, k * (nh // hb) + h), pipeline_mode=pl.Buffered(1))

    head_in = pl.BlockSpec((s, wide), lambda h: (0, h), pipeline_mode=pl.Buffered(1))
    head = pl.BlockSpec((s, wide), lambda h: (0, h))
    vec = pl.BlockSpec((None, 1, wide), lambda h: (l, 0, h))
    ck = pltpu.VMEM((hb, CHUNK, LANES), F32)
    return _carry(
        body,
        comm,
        name="hgrn_bwd",
        grid=(nh // hb,),
        in_specs=[col(0), col(1), col(2), col(3), vec, pl.BlockSpec((None, 1, LANES), lambda h: (l, 0, 0)), head_in, head_in],
        out_specs=[head] * 4 + [pl.BlockSpec((1, wide), lambda h: (0, h)), _const((1, LANES))],
        out_shape=[jax.ShapeDtypeStruct((s, nh * LANES), F32)] * 4
        + [jax.ShapeDtypeStruct((1, nh * LANES), F32), jax.ShapeDtypeStruct((1, LANES), F32)],
        scratch_shapes=[pltpu.VMEM((hb, n_chunk, LANES, LANES), F32)] + [ck] * 8 + [pltpu.VMEM((hb, 8, LANES), F32)],
        args=(proj, proj, proj, proj, lb3, og3, o, dmixed),
    )


MEM_SCALE = HEAD64**-0.5


def _mem_heads(qraw, kvm, qg, kg, pr, m0):
    lo = pr * LANES
    uq, rq = _rms64(qraw[:, lo : lo + LANES], m0)
    uk, rk = _rms64(kvm[:, lo : lo + LANES], m0)
    v = bf(kvm[:, 2 * LANES + lo : 3 * LANES + lo])
    return uq, rq, uk, rk, v, uq * qg, bf(uk * kg)


def memattn_fwd(proj, qblk, kvm, qg3, kg3, l):
    s = proj.shape[0]
    nm = kvm.shape[0]
    tm = _tok(s)

    def body(q_ref, kv_ref, qg_ref, kg_ref, o_ref):
        m0 = _lane_mask0((1, LANES))
        qraw, kvv = q_ref[...], kv_ref[...]
        for pr in range(2):
            _, _, _, _, v, qn, kn = _mem_heads(qraw, kvv, qg_ref[...], kg_ref[...], pr, m0)
            out = jnp.zeros((tm, LANES), F32)
            for hh in range(2):
                mh = m0 if hh == 0 else jnp.logical_not(m0)
                sc = dot_nt(bf(jnp.where(mh, qn, 0.0)), kn) * MEM_SCALE
                p = jnp.exp(sc - jnp.max(sc, axis=-1, keepdims=True))
                p = p / jnp.sum(p, axis=-1, keepdims=True)
                out = jnp.where(mh, dot_nn(bf(p), v), out)
            o_ref[:, pr * LANES : (pr + 1) * LANES] = bf(out)

    gspec = pl.BlockSpec((None, 1, LANES), lambda i: (l, 0, 0))
    return pl.pallas_call(
        body,
        name="memattn_fwd",
        grid=(s // tm,),
        in_specs=[pl.BlockSpec((tm, 2 * LANES), lambda i: (i, qblk)), _const((nm, 4 * LANES)), gspec, gspec],
        out_specs=pl.BlockSpec((tm, 2 * LANES), lambda i: (i, 0)),
        out_shape=jax.ShapeDtypeStruct((s, 2 * LANES), BF16),
        compiler_params=_cparams(("arbitrary",)),
    )(proj, kvm, qg3, kg3)


def memattn_bwd(proj, qblk, kvm, qg3, kg3, l, dmixed):
    s = proj.shape[0]
    nm = kvm.shape[0]
    tm = _tok(s)

    def body(q_ref, kv_ref, qg_ref, kg_ref, dm_ref, dq_ref, dkv_ref, dqg_ref, dkg_ref):
        m0 = _lane_mask0((1, LANES))
        qraw, kvv = q_ref[...], kv_ref[...]
        qg, kg = qg_ref[...], kg_ref[...]

        @pl.when(pl.program_id(0) == 0)
        def _():
            dkv_ref[...] = jnp.zeros_like(dkv_ref)
            dqg_ref[...] = jnp.zeros_like(dqg_ref)
            dkg_ref[...] = jnp.zeros_like(dkg_ref)

        for pr in range(2):
            lo = pr * LANES
            uq, rq, uk, rk, v, qn, kn = _mem_heads(qraw, kvv, qg, kg, pr, m0)
            do = dm_ref[:, lo : lo + LANES]
            dqn = jnp.zeros((tm, LANES), F32)
            dkn = jnp.zeros((nm, LANES), F32)
            dv = jnp.zeros((nm, LANES), F32)
            for hh in range(2):
                mh = m0 if hh == 0 else jnp.logical_not(m0)
                qh = bf(jnp.where(mh, qn, 0.0))
                doh = bf(jnp.where(mh, do, 0.0))
                sc = dot_nt(qh, kn) * MEM_SCALE
                p = jnp.exp(sc - jnp.max(sc, axis=-1, keepdims=True))
                p = p / jnp.sum(p, axis=-1, keepdims=True)
                dp = dot_nt(doh, v)
                ds = bf(p * (dp - jnp.sum(p * dp, axis=-1, keepdims=True)))
                dqn = dqn + jnp.where(mh, dot_nn(ds, kn), 0.0) * MEM_SCALE
                dkn = dkn + dot_tn(ds, qh) * MEM_SCALE
                dv = dv + dot_tn(bf(p), doh)
            dqr, dqg = _rms64_bwd(dqn, uq, rq, qg, m0)
            dkr, dkg = _rms64_bwd(dkn, uk, rk, kg, m0)
            dq_ref[:, lo : lo + LANES] = dqr
            dkv_ref[:, lo : lo + LANES] += dkr
            dkv_ref[:, 2 * LANES + lo : 3 * LANES + lo] += dv
            dqg_ref[...] += dqg
            dkg_ref[...] += dkg

    gspec = pl.BlockSpec((None, 1, LANES), lambda i: (l, 0, 0))
    return pl.pallas_call(
        body,
        name="memattn_bwd",
        grid=(s // tm,),
        in_specs=[
            pl.BlockSpec((tm, 2 * LANES), lambda i: (i, qblk)),
            _const((nm, 4 * LANES)),
            gspec,
            gspec,
            pl.BlockSpec((tm, 2 * LANES), lambda i: (i, 3)),
        ],
        out_specs=[pl.BlockSpec((tm, 2 * LANES), lambda i: (i, 0)), _const((nm, 4 * LANES)), _const((1, LANES)), _const((1, LANES))],
        out_shape=[
            jax.ShapeDtypeStruct((s, 2 * LANES), F32),
            jax.ShapeDtypeStruct((nm, 4 * LANES), F32),
            jax.ShapeDtypeStruct((1, LANES), F32),
            jax.ShapeDtypeStruct((1, LANES), F32),
        ],
        compiler_params=_cparams(("arbitrary",)),
    )(proj, kvm, qg3, kg3, dmixed)


KV_MAIN = 768


def _log_sigmoid(z):
    return jnp.minimum(z, 0.0) - jnp.log(1.0 + jnp.exp(-jnp.abs(z)))


def kvprep_fwd(kvf, kg, fb):
    s = kvf.shape[0]
    tm = _tok(s)

    def body(kvf_ref, kg_ref, fb_ref, k_ref, v_ref, clf_ref, carry):
        m0 = _lane_mask0((1, LANES))

        @pl.when(pl.program_id(0) == 0)
        def _():
            carry[...] = jnp.zeros_like(carry)

        for j in range(KV_MAIN // LANES):
            u, _ = _rms64(kvf_ref[:, j * LANES : (j + 1) * LANES], m0)
            k_ref[:, j * LANES : (j + 1) * LANES] = bf(u * kg_ref[...])
        v_ref[...] = bf(kvf_ref[:, KV_MAIN : 2 * KV_MAIN])
        lf = _log_sigmoid(kvf_ref[:, 2 * KV_MAIN :] + fb_ref[...])
        clf_ref[...] = dot_nn(_tri(tm), lf, HI) + carry[...]
        carry[...] += jnp.sum(lf, axis=0, keepdims=True)

    n = kvf.shape[1]
    return pl.pallas_call(
        body,
        name="kvprep_fwd",
        grid=(s // tm,),
        in_specs=[pl.BlockSpec((tm, n), lambda i: (i, 0)), _const((1, LANES)), _const((1, LANES))],
        out_specs=[pl.BlockSpec((tm, KV_MAIN), lambda i: (i, 0))] * 2 + [pl.BlockSpec((tm, LANES), lambda i: (i, 0))],
        out_shape=[jax.ShapeDtypeStruct((s, KV_MAIN), BF16)] * 2 + [jax.ShapeDtypeStruct((s, LANES), F32)],
        scratch_shapes=[pltpu.VMEM((1, LANES), F32)],
        compiler_params=_cparams(("arbitrary",)),
    )(kvf, kg, fb)


def kvprep_bwd(kvf, kg, fb, dk, dv, dclf):
    s, n = kvf.shape
    tm = _tok(s)
    nb = s // tm

    def body(kvf_ref, kg_ref, fb_ref, dk_ref, dv_ref, dclf_ref, o_ref, dkg_ref, dfb_ref, carry):
        m0 = _lane_mask0((1, LANES))

        @pl.when(pl.program_id(0) == 0)
        def _():
            carry[...] = jnp.zeros_like(carry)
            dkg_ref[...] = jnp.zeros_like(dkg_ref)
            dfb_ref[...] = jnp.zeros_like(dfb_ref)

        kg_ = kg_ref[...]
        for j in range(KV_MAIN // LANES):
            cols = slice(j * LANES, (j + 1) * LANES)
            u, r = _rms64(kvf_ref[:, cols], m0)
            dkr, dkg = _rms64_bwd(dk_ref[:, cols], u, r, kg_, m0)
            o_ref[:, cols] = dkr
            dkg_ref[...] += dkg
        o_ref[:, KV_MAIN : 2 * KV_MAIN] = dv_ref[...]
        z = kvf_ref[:, 2 * KV_MAIN :] + fb_ref[...]
        dc = dclf_ref[...]
        dlf = dot_nn(_tri(tm, upper=True), dc, HI) + carry[...]
        carry[...] += jnp.sum(dc, axis=0, keepdims=True)
        dz = dlf * _sigmoid(-z)
        o_ref[:, 2 * KV_MAIN :] = dz
        dfb_ref[...] += jnp.sum(dz, axis=0, keepdims=True)

    rev = lambda i: (nb - 1 - i, 0)
    return pl.pallas_call(
        body,
        name="kvprep_bwd",
        grid=(nb,),
        in_specs=[pl.BlockSpec((tm, n), rev), _const((1, LANES)), _const((1, LANES)), pl.BlockSpec((tm, KV_MAIN), rev),
                  pl.BlockSpec((tm, KV_MAIN), rev), pl.BlockSpec((tm, LANES), rev)],
        out_specs=[pl.BlockSpec((tm, n), rev), _const((1, LANES)), _const((1, LANES))],
        out_shape=[jax.ShapeDtypeStruct((s, n), F32), jax.ShapeDtypeStruct((1, LANES), F32), jax.ShapeDtypeStruct((1, LANES), F32)],
        scratch_shapes=[pltpu.VMEM((1, LANES), F32)],
        compiler_params=_cparams(("arbitrary",)),
    )(kvf, kg, fb, dk, dv, dclf)


FOX_SCALE = HEAD64**-0.5


def _lane_col(block, lane_idx, h):
    return jnp.sum(jnp.where(lane_idx == h, block, 0.0), axis=-1, keepdims=True)


def _causal(tq, ext, i, transposed=False):
    if transposed:
        key = lax.broadcasted_iota(jnp.int32, (ext, tq), 0)
        qry = lax.broadcasted_iota(jnp.int32, (ext, tq), 1) + i * tq
    else:
        qry = lax.broadcasted_iota(jnp.int32, (tq, ext), 0) + i * tq
        key = lax.broadcasted_iota(jnp.int32, (tq, ext), 1)
    return key <= qry


def fox_fwd(proj, k_sh, v_sh, clf, clf_t, qg3, j_layer):
    s = proj.shape[0]
    npair = 6
    tq = TQ if s % TQ == 0 else s
    nq = s // tq

    def body(q_ref, gate_ref, k_ref, v_ref, clf_ref, clft_ref, qg_ref, main_ref, o_ref, lse_ref):
        j = pl.program_id(0)
        lane = lax.broadcasted_iota(jnp.int32, (1, LANES), 1)
        m0 = lane < HEAD64
        u, _ = _rms64(q_ref[...], m0)
        qn = u * qg_ref[...] * FOX_SCALE
        clfv = clf_ref[...]
        for hh in range(2):
            h = 2 * j + hh
            mh = m0 if hh == 0 else jnp.logical_not(m0)
            qh = bf(jnp.where(mh, qn, 0.0))
            dcol = _lane_col(clfv, lane, h)
            drow = clft_ref[pl.ds(h, 1), :]
            for i in range(nq):
                rows = slice(i * tq, (i + 1) * tq)
                ext = (i + 1) * tq
                sc = dot_nt(qh[rows], k_ref[0:ext, :]) + dcol[rows] - drow[:, :ext]
                sc = jnp.where(_causal(tq, ext, i), sc, -jnp.inf)
                m = jnp.max(sc, axis=-1, keepdims=True)
                p = jnp.exp(sc - m)
                lsum = jnp.sum(p, axis=-1, keepdims=True)
                pv = dot_nn(bf(p), v_ref[0:ext, :]) / lsum
                lse = m + jnp.log(lsum)
                if hh == 0:
                    o_ref[rows, :] = pv
                    lse_ref[rows, :] = jnp.where(lane == 0, lse, 0.0)
                else:
                    o_ref[rows, :] = jnp.where(mh, pv, o_ref[rows, :])
                    lse_ref[rows, :] = jnp.where(lane == 1, lse, lse_ref[rows, :])
        main_ref[...] = bf(o_ref[...] * _sigmoid(gate_ref[...]))

    blk = lambda off: pl.BlockSpec((s, LANES), lambda j: (0, off + j))
    return pl.pallas_call(
        body,
        name="fox_fwd",
        grid=(npair,),
        in_specs=[blk(0), blk(npair), blk(0), blk(0), _const((s, LANES)), _const((16, s)),
                  pl.BlockSpec((None, 1, LANES), lambda j: (j_layer, 0, 0))],
        out_specs=[blk(0)] * 3,
        out_shape=[jax.ShapeDtypeStruct((s, npair * LANES), BF16)] + [jax.ShapeDtypeStruct((s, npair * LANES), F32)] * 2,
        compiler_params=_cparams(("arbitrary",)),
    )(proj, proj, k_sh, v_sh, clf, clf_t, qg3)


def fox_bwd(proj, k_sh, v_sh, clf, clf_t, qg3, j_layer, o, lse, lse_t, dmixed, dk_in, dv_in, dclf_in):
    s = proj.shape[0]
    npair = 6
    tq = TQ if s % TQ == 0 else s
    nq = s // tq

    def body(q_ref, gate_ref, k_ref, v_ref, clf_ref, clft_ref, qg_ref, o_ref, lse_ref, lset_ref, dm_ref, dkin_ref, dvin_ref, dclfin_ref,
             dq_ref, dgate_ref, dk_ref, dv_ref, dclf_ref, dqg_ref, dqn_s, dcl_s):
        j = pl.program_id(0)
        lane = lax.broadcasted_iota(jnp.int32, (1, LANES), 1)
        m0 = lane < HEAD64
        qg = qg_ref[...]
        u, r = _rms64(q_ref[...], m0)
        qn = u * qg * FOX_SCALE
        ov = o_ref[...]
        gate = gate_ref[...]
        sg = _sigmoid(gate)
        dmain = dm_ref[...]
        do = dmain * sg
        dgate_ref[...] = dmain * ov * sg * (1.0 - sg)
        dk_ref[...] = dkin_ref[...]
        dv_ref[...] = dvin_ref[...]
        clfv = clf_ref[...]
        lsev = lse_ref[...]
        ones8 = jnp.ones((8, LANES), F32)

        @pl.when(j == 0)
        def _():
            dclf_ref[...] = dclfin_ref[...]
            dqg_ref[...] = jnp.zeros_like(dqg_ref)

        for hh in range(2):
            h = 2 * j + hh
            mh = m0 if hh == 0 else jnp.logical_not(m0)
            qh = bf(jnp.where(mh, qn, 0.0))
            doh = jnp.where(mh, do, 0.0)
            dohb = bf(doh)
            doo = doh * ov
            dcol = _lane_col(clfv, lane, h)
            drow = clft_ref[pl.ds(h, 1), :]
            lcol = _lane_col(lsev, lane, hh)
            lrow = lset_ref[pl.ds(h, 1), :]
            delta = jnp.sum(doo, axis=-1, keepdims=True)
            dcl_s[...] = jnp.zeros_like(dcl_s)
            for i in range(nq):
                rows = slice(i * tq, (i + 1) * tq)
                ext = (i + 1) * tq
                kk, vv = k_ref[0:ext, :], v_ref[0:ext, :]
                sc = dot_nt(qh[rows], kk) + dcol[rows] - drow[:, :ext]
                p = jnp.where(_causal(tq, ext, i), jnp.exp(sc - lcol[rows]), 0.0)
                ds = p * (dot_nt(dohb[rows], vv) - delta[rows])
                dqh = dot_nn(bf(ds), kk) * FOX_SCALE
                if hh == 0:
                    dqn_s[rows, :] = dqh
                else:
                    dqn_s[rows, :] = jnp.where(mh, dqh, dqn_s[rows, :])
                dcl_s[rows, :] += jnp.sum(ds, axis=-1, keepdims=True)
                sct = dot_nt(kk, qh[rows]) + drow[:, rows] - dcol[:ext]
                pt = jnp.where(_causal(tq, ext, i, transposed=True), jnp.exp(sct - lrow[:, rows]), 0.0)
                delta_row = dot_nt(ones8, doo[rows], HI)[0:1]
                dst = pt * (dot_nt(vv, dohb[rows]) - delta_row)
                dv_ref[0:ext, :] += dot_nn(bf(pt), dohb[rows])
                dk_ref[0:ext, :] += dot_nn(bf(dst), qh[rows])
                dcl_s[0:ext, :] -= jnp.sum(dst, axis=-1, keepdims=True)
            dclf_ref[...] += jnp.where(lane == h, dcl_s[...], 0.0)
        dqr, dqg = _rms64_bwd(dqn_s[...], u, r, qg, m0)
        dq_ref[...] = dqr
        dqg_ref[...] += dqg

    blk = lambda off: pl.BlockSpec((s, LANES), lambda j: (0, off + j))
    full = _const((s, LANES))
    return pl.pallas_call(
        body,
        name="fox_bwd",
        grid=(npair,),
        in_specs=[blk(0), blk(npair), blk(0), blk(0), full, _const((16, s)), pl.BlockSpec((None, 1, LANES), lambda j: (j_layer, 0, 0)),
                  blk(0), blk(0), _const((16, s)), blk(0), blk(0), blk(0), full],
        out_specs=[blk(0)] * 4 + [full, _const((1, LANES))],
        out_shape=[jax.ShapeDtypeStruct((s, npair * LANES), F32)] * 4
        + [jax.ShapeDtypeStruct((s, LANES), F32), jax.ShapeDtypeStruct((1, LANES), F32)],
        scratch_shapes=[pltpu.VMEM((s, LANES), F32), pltpu.VMEM((s, LANES), F32)],
        compiler_params=_cparams(("arbitrary",)),
    )(proj, proj, k_sh, v_sh, clf, clf_t, qg3, o, lse, lse_t, dmixed, dk_in, dv_in, dclf_in)


def loss_head(y, target):
    s, d = y.shape
    tm = _tok(s)

    def body(y_ref, t_ref, loss_ref, dy_ref):
        err = y_ref[...] - t_ref[...]
        dy_ref[...] = err * (1.0 / d)

        @pl.when(pl.program_id(0) == 0)
        def _():
            loss_ref[...] = jnp.zeros_like(loss_ref)

        part = jnp.sum(jnp.mean(err * err, axis=-1, keepdims=True), axis=0, keepdims=True)
        loss_ref[...] += 0.5 * part

    row = pl.BlockSpec((tm, d), lambda i: (i, 0))
    return pl.pallas_call(
        body,
        name="loss_head",
        grid=(s // tm,),
        in_specs=[row, row],
        out_specs=[_const((1, 1)), row],
        out_shape=[jax.ShapeDtypeStruct((1, 1), F32), jax.ShapeDtypeStruct((s, d), F32)],
        compiler_params=_cparams(("arbitrary",)),
    )(y, target)


def _row_tile(r, c, n_arrays):
    budget = VMEM_LIMIT_BYTES // 2
    padded_c = -(-c // LANES) * LANES
    best = None
    for t in range(8, r + 1, 8):
        if r % t == 0 and 2 * n_arrays * t * padded_c * 4 <= budget:
            best = t
    return r if best is None else best


def _as2d(a):
    return a.reshape(-1, a.shape[-1]) if a.ndim >= 2 else a.reshape(1, -1)


def adamw(w, gs, m, v):
    shape = w.shape
    ng = len(gs)
    w2 = _as2d(w)
    c = w2.shape[1]
    r = w2.shape[0] // ng
    w3, m3, v3 = (_as2d(t).reshape(ng, r, c) for t in (w, m, v))
    tr = _row_tile(r, c, 8)
    c1 = 1.0 - ADAM_B1**ADAM_STEP
    c2 = 1.0 - ADAM_B2**ADAM_STEP
    outs = None
    for k, g in enumerate(gs):
        n_prev = 0 if outs is None else 4

        def body(w_ref, g_ref, m_ref, v_ref, *rest, n_prev=n_prev):
            go_ref, d_ref, nm_ref, nv_ref = rest[n_prev:]
            gv = g_ref[...]
            nm = ADAM_B1 * m_ref[...] + (1.0 - ADAM_B1) * gv
            nv = ADAM_B2 * v_ref[...] + (1.0 - ADAM_B2) * (gv * gv)
            go_ref[...] = gv
            nm_ref[...] = nm
            nv_ref[...] = nv
            d_ref[...] = -ADAM_LR * ((nm / c1) / (jnp.sqrt(nv / c2) + ADAM_EPS) + ADAM_WD * w_ref[...])

        spec = pl.BlockSpec((None, tr, c), lambda i, k=k: (k, i, 0))
        outs = pl.pallas_call(
            body,
            name="adamw",
            grid=(r // tr,),
            in_specs=[spec, pl.BlockSpec((tr, c), lambda i: (i, 0)), spec, spec] + [ANY] * n_prev,
            out_specs=[spec] * 4,
            out_shape=[jax.ShapeDtypeStruct((ng, r, c), F32)] * 4,
            input_output_aliases={4 + i: i for i in range(n_prev)},
            compiler_params=_cparams(("arbitrary",)),
        )(w3, g.reshape(r, c), m3, v3, *([] if outs is None else outs))
    return tuple(t.reshape(shape) for t in outs)


def pair_sum(g, recv, c_arr):
    _, k, r, c = g.shape
    tr = _row_tile(r, c, 3)

    def body(c_ref, g_ref, r_ref, o_ref):
        o_ref[...] = bf(g_ref[...] + r_ref[...])

    return pl.pallas_call(
        body,
        name="pair_sum",
        grid_spec=pltpu.PrefetchScalarGridSpec(
            num_scalar_prefetch=1,
            grid=(k, r // tr),
            in_specs=[pl.BlockSpec((None, None, tr, c), lambda kk, i, cr: (cr[0], kk, i, 0)), pl.BlockSpec((None, tr, c), lambda kk, i, cr: (kk, i, 0))],
            out_specs=pl.BlockSpec((None, tr, c), lambda kk, i, cr: (kk, i, 0)),
        ),
        out_shape=jax.ShapeDtypeStruct((k, r, c), BF16),
        compiler_params=_cparams(("arbitrary", "arbitrary")),
    )(c_arr, g, recv)


def chip_sum(p, q, sel):
    _, r, c = p.shape
    tr = _row_tile(r, c, 4)

    def body(sel_ref, p_ref, q_ref, o_ref):
        acc = p_ref[...].astype(F32)
        for i in range(q.shape[0]):
            acc = acc + q_ref[i].astype(F32)
        o_ref[...] = acc

    return pl.pallas_call(
        body,
        name="chip_sum",
        grid_spec=pltpu.PrefetchScalarGridSpec(
            num_scalar_prefetch=1,
            grid=(r // tr,),
            in_specs=[pl.BlockSpec((None, tr, c), lambda i, sr: (sr[0], i, 0)), pl.BlockSpec((q.shape[0], tr, c), lambda i, sr: (0, i, 0))],
            out_specs=pl.BlockSpec((None, tr, c), lambda i, sr: (sr[1], i, 0)),
        ),
        out_shape=jax.ShapeDtypeStruct((2, r, c), F32),
        compiler_params=_cparams(("arbitrary",)),
    )(sel, p, q)


def cast_into_slot(w4, g, sel, dtype):
    _, _, r, c = w4.shape
    tr = _row_tile(r, c, 2)

    def body(sel_ref, w_ref, o_ref):
        o_ref[...] = w_ref[...].astype(dtype)

    return pl.pallas_call(
        body,
        name="cast_into_slot",
        grid_spec=pltpu.PrefetchScalarGridSpec(
            num_scalar_prefetch=1,
            grid=(2, r // tr),
            in_specs=[pl.BlockSpec((None, None, tr, c), lambda hf, i, sr: (g, hf, i, 0))],
            out_specs=pl.BlockSpec((None, None, tr, c), lambda hf, i, sr: (sr[0], hf, i, 0)),
        ),
        out_shape=jax.ShapeDtypeStruct((N_CHIPS, 2, r, c), dtype),
        compiler_params=_cparams(("arbitrary", "arbitrary")),
    )(sel, w4)


def _place():
    x, y, c = lax.axis_index("x"), lax.axis_index("y"), lax.axis_index("c")
    chips = [(1 - x, y), (x, 1 - y), (1 - x, 1 - y)]
    return x, y, c, 2 * x + y, chips, [2 * cx + cy for cx, cy in chips]


def _rcopy(src, dst, send, recv, dev):
    return pltpu.make_async_remote_copy(src_ref=src, dst_ref=dst, send_sem=send, recv_sem=recv, device_id=dev, device_id_type=MESH)


class Gather:
    def __init__(self, bufs):
        n = len(bufs)
        self.n = n
        self.args = list(bufs)
        self.out_shape = [jax.ShapeDtypeStruct(t.shape, t.dtype) for t in bufs]
        self.aliases = {a: a for a in range(n)}
        self.scratch = [pltpu.SemaphoreType.DMA((n, 6)), pltpu.SemaphoreType.DMA((n, 6))]

    def _sends(self, outs, send, recv):
        x, y, c, me, chips, _ = _place()
        cps = []
        for a in range(self.n):
            mine = outs[a].at[me, c]
            cps += [_rcopy(mine, mine, send.at[a, j], recv.at[a, j], (*chips[j], c)) for j in range(3)]
        return cps

    def start(self, ins, outs, scr):
        for cp in self._sends(outs, *scr):
            cp.start()

    def finish(self, ins, outs, scr):
        send, recv = scr
        x, y, c, me, chips, cidx = _place()
        sib = (x, y, 1 - c)
        passed = []
        for a in range(self.n):
            for j in range(3):
                landed = outs[a].at[cidx[j], c]
                _rcopy(landed, landed, send.at[a, j], recv.at[a, j], (*chips[j], c)).wait_recv()
                fwd = _rcopy(landed, landed, send.at[a, 3 + j], recv.at[a, 3 + j], sib)
                fwd.start()
                passed.append(fwd)
        for a in range(self.n):
            for j in range(3):
                theirs = outs[a].at[cidx[j], 1 - c]
                _rcopy(theirs, theirs, send.at[a, 3 + j], recv.at[a, 3 + j], sib).wait_recv()
        for cp in self._sends(outs, send, recv) + passed:
            cp.wait_send()


class PairExchange:
    def __init__(self, gs):
        n = len(gs)
        self.n = n
        self.args = list(gs)
        self.out_shape = [jax.ShapeDtypeStruct(t.shape[1:], t.dtype) for t in gs]
        self.aliases = {}
        self.scratch = [pltpu.SemaphoreType.DMA((n,)), pltpu.SemaphoreType.DMA((n,))]

    def _copies(self, ins, outs, send, recv):
        x, y, c = lax.axis_index("x"), lax.axis_index("y"), lax.axis_index("c")
        return [_rcopy(ins[a].at[1 - c], outs[a], send.at[a], recv.at[a], (x, y, 1 - c)) for a in range(self.n)]

    def start(self, ins, outs, scr):
        for cp in self._copies(ins, outs, *scr):
            cp.start()

    def finish(self, ins, outs, scr):
        for cp in self._copies(ins, outs, *scr):
            cp.wait()


class ChipExchange:
    def __init__(self, ps):
        n = len(ps)
        self.n = n
        self.args = list(ps)
        self.out_shape = [jax.ShapeDtypeStruct((3,) + t.shape[1:], t.dtype) for t in ps]
        self.aliases = {}
        self.scratch = [pltpu.SemaphoreType.DMA((n, 3)), pltpu.SemaphoreType.DMA((n, 3))]

    def _sends(self, ins, outs, send, recv):
        x, y, c, me, chips, cidx = _place()
        return [
            _rcopy(ins[a].at[cidx[j]], outs[a].at[j], send.at[a, j], recv.at[a, j], (*chips[j], c))
            for a in range(self.n)
            for j in range(3)
        ]

    def start(self, ins, outs, scr):
        for cp in self._sends(ins, outs, *scr):
            cp.start()

    def finish(self, ins, outs, scr):
        send, recv = scr
        x, y, c, me, chips, _ = _place()
        for a in range(self.n):
            for j in range(3):
                landed = outs[a].at[j]
                _rcopy(landed, landed, send.at[a, j], recv.at[a, j], (*chips[j], c)).wait_recv()
        for cp in self._sends(ins, outs, send, recv):
            cp.wait_send()


class PairShare:
    def __init__(self, bufs):
        n = len(bufs)
        self.n = n
        self.args = list(bufs)
        self.out_shape = [jax.ShapeDtypeStruct(t.shape, t.dtype) for t in bufs]
        self.aliases = {a: a for a in range(n)}
        self.scratch = [pltpu.SemaphoreType.DMA((n,)), pltpu.SemaphoreType.DMA((n,))]

    def _sends(self, outs, send, recv):
        x, y, c = lax.axis_index("x"), lax.axis_index("y"), lax.axis_index("c")
        return [_rcopy(outs[a].at[c], outs[a].at[c], send.at[a], recv.at[a], (x, y, 1 - c)) for a in range(self.n)]

    def start(self, ins, outs, scr):
        for cp in self._sends(outs, *scr):
            cp.start()

    def finish(self, ins, outs, scr):
        send, recv = scr
        x, y, c = lax.axis_index("x"), lax.axis_index("y"), lax.axis_index("c")
        for a in range(self.n):
            theirs = outs[a].at[1 - c]
            _rcopy(theirs, theirs, send.at[a], recv.at[a], (x, y, 1 - c)).wait_recv()
        for cp in self._sends(outs, send, recv):
            cp.wait_send()


def run_comm(comm, name):
    na, no = len(comm.args), len(comm.out_shape)

    def body(*refs):
        ins, outs, scr = refs[:na], refs[na : na + no], refs[na + no :]
        comm.start(ins, outs, scr)
        comm.finish(ins, outs, scr)

    return pl.pallas_call(
        body,
        name=name,
        in_specs=[ANY] * na,
        out_specs=[ANY] * no,
        out_shape=comm.out_shape,
        input_output_aliases=comm.aliases,
        scratch_shapes=comm.scratch,
    )(*comm.args)


def _carry(body, comm, *, name, grid, in_specs, out_specs, out_shape, scratch_shapes, args):
    params = _cparams(("arbitrary",))
    if comm is None:
        res = pl.pallas_call(body, name=name, grid=grid, in_specs=in_specs, out_specs=out_specs, out_shape=out_shape,
                             scratch_shapes=scratch_shapes, compiler_params=params)(*args)
        return res, None
    ni, no, ns = len(in_specs), len(out_specs), len(scratch_shapes)
    ci, co = len(comm.args), len(comm.out_shape)

    def wrapped(*refs):
        ins, c_ins = refs[:ni], refs[ni : ni + ci]
        p = ni + ci
        outs, c_outs = refs[p : p + no], refs[p + no : p + no + co]
        p += no + co
        scr, c_scr = refs[p : p + ns], refs[p + ns :]

        @pl.when(pl.program_id(0) == 0)
        def _():
            comm.start(c_ins, c_outs, c_scr)

        body(*ins, *outs, *scr)

        @pl.when(pl.program_id(0) == grid[0] - 1)
        def _():
            comm.finish(c_ins, c_outs, c_scr)

    res = pl.pallas_call(
        wrapped,
        name=name + "_carry",
        grid=grid,
        in_specs=list(in_specs) + [ANY] * ci,
        out_specs=list(out_specs) + [ANY] * co,
        out_shape=list(out_shape) + list(comm.out_shape),
        input_output_aliases={ni + i: no + o for i, o in comm.aliases.items()},
        scratch_shapes=list(scratch_shapes) + list(comm.scratch),
        compiler_params=params,
    )(*args, *comm.args)
    return res[:no], res[no:]


def small_allreduce(buf):
    r = buf.shape[0]

    def body(b_ref, o_ref, slots, send, recv):
        x, y, c = lax.axis_index("x"), lax.axis_index("y"), lax.axis_index("c")
        me = 4 * x + 2 * y + c
        slots[me] = b_ref[...]
        cps = []
        peers = []
        for mask in range(1, N_DEV):
            fx, fy, fc = (mask >> 2) & 1, (mask >> 1) & 1, mask & 1
            px, py, pc = (1 - x if fx else x), (1 - y if fy else y), (1 - c if fc else c)
            peers.append(4 * px + 2 * py + pc)
            cps.append(_rcopy(b_ref, slots.at[me], send.at[mask - 1], recv.at[mask - 1], (px, py, pc)))
        for cp in cps:
            cp.start()
        for k, pid in enumerate(peers):
            landed = slots.at[pid]
            _rcopy(landed, landed, send.at[k], recv.at[k], (x, y, c)).wait_recv()
        for cp in cps:
            cp.wait_send()
        acc = slots[0]
        for i in range(1, N_DEV):
            acc = acc + slots[i]
        o_ref[...] = acc

    vm = pl.BlockSpec(memory_space=pltpu.VMEM)
    return pl.pallas_call(
        body,
        name="small_allreduce",
        in_specs=[vm],
        out_specs=vm,
        out_shape=jax.ShapeDtypeStruct(buf.shape, F32),
        scratch_shapes=[pltpu.VMEM((N_DEV, r, LANES), F32), pltpu.SemaphoreType.DMA((N_DEV - 1,)), pltpu.SemaphoreType.DMA((N_DEV - 1,))],
    )(buf)


WEIGHT_NAMES = ["ffn1_norm", "ffn1_w_gate", "ffn1_w_up", "ffn1_w_down", "mix_norm", "mem_norm", "w_mem_kv", "mem_q_gain",
                "mem_k_gain", "w_in_a", "hgrn_lb_logits", "hgrn_o_gain", "w_in_b", "fox_q_gain", "kv_norm", "w_kv", "fox_f_bias",
                "fox_k_gain", "w_out", "ffn2_norm", "ffn2_w_gate", "ffn2_w_up", "ffn2_w_down"]
SHARDED = ["ffn1_w_gate", "ffn1_w_up", "ffn1_w_down", "w_mem_kv", "w_in_a", "w_in_b", "w_kv", "w_out", "ffn2_w_gate", "ffn2_w_up", "ffn2_w_down"]
SMALL = [n for n in WEIGHT_NAMES if n not in SHARDED]
FFN1 = ["ffn1_w_gate", "ffn1_w_up", "ffn1_w_down"]
FFN2 = ["ffn2_w_gate", "ffn2_w_up", "ffn2_w_down"]
PER_LAYER = FFN1 + FFN2 + ["w_mem_kv", "w_out"]
N_LAYERS, N_A = 4, 2
KV_PAD = 13 * LANES


def _halves(t):
    return t.reshape((2, t.shape[0] // 2) + t.shape[1:])


def _cols_from_chips(g):
    return jnp.moveaxis(g, 0, 2).reshape(g.shape[1], g.shape[2], N_CHIPS * g.shape[3])


def _rows_from_chips(g):
    return jnp.moveaxis(g, 0, 1).reshape(g.shape[1], N_CHIPS * g.shape[2], g.shape[3])


def _pair_tile(g):
    return jnp.tile(g, (1, 2)).reshape(g.shape[0], 1, LANES)


def _pair_fold(g):
    return g[:, :HEAD64] + g[:, HEAD64:]


def kernel(x, mem, ffn1_norm, ffn1_w_gate, ffn1_w_up, ffn1_w_down, mix_norm, mem_norm, w_mem_kv, mem_q_gain, mem_k_gain, w_in_a, hgrn_lb_logits, hgrn_o_gain, w_in_b, fox_q_gain, kv_norm, w_kv, fox_f_bias, fox_k_gain, w_out, ffn2_norm, ffn2_w_gate, ffn2_w_up, ffn2_w_down, loss_target, m_ffn1_norm, m_ffn1_w_gate, m_ffn1_w_up, m_ffn1_w_down, m_mix_norm, m_mem_norm, m_w_mem_kv, m_mem_q_gain, m_mem_k_gain, m_w_in_a, m_hgrn_lb_logits, m_hgrn_o_gain, m_w_in_b, m_fox_q_gain, m_kv_norm, m_w_kv, m_fox_f_bias, m_fox_k_gain, m_w_out, m_ffn2_norm, m_ffn2_w_gate, m_ffn2_w_up, m_ffn2_w_down, v_ffn1_norm, v_ffn1_w_gate, v_ffn1_w_up, v_ffn1_w_down, v_mix_norm, v_mem_norm, v_w_mem_kv, v_mem_q_gain, v_mem_k_gain, v_w_in_a, v_hgrn_lb_logits, v_hgrn_o_gain, v_w_in_b, v_fox_q_gain, v_kv_norm, v_w_kv, v_fox_f_bias, v_fox_k_gain, v_w_out, v_ffn2_norm, v_ffn2_w_gate, v_ffn2_w_up, v_ffn2_w_down):
    given = dict(locals())
    w = {n: given[n] for n in WEIGHT_NAMES}
    xs, mems, tgt = x[0], mem[0], loss_target[0]
    s, d = xs.shape
    my_chip = 2 * lax.axis_index("x") + lax.axis_index("y")
    sel = jnp.stack([my_chip, lax.axis_index("c")]).astype(jnp.int32)
    c_arr = sel[1:]

    def cast(n, g=0, ng=1):
        t = w[n]
        if t.ndim == 2:
            t4 = t.reshape(1, 2, t.shape[0] // 2, t.shape[1])
        else:
            t4 = t.reshape(ng, 2, (t.shape[0] // (2 * ng)) * t.shape[1], t.shape[2])
        return cast_into_slot(t4, g, sel, BF16)

    def view(buf, n):
        return buf.reshape((N_CHIPS, 2) + w[n].shape[1:]) if w[n].ndim == 3 else buf.reshape((N_CHIPS,) + w[n].shape)

    names0 = PER_LAYER + ["w_in_a", "w_kv"]
    bufs0 = [cast(n, 0, 2) for n in PER_LAYER] + [cast("w_in_a"), cast("w_kv")]
    bufs0.append(cast_into_slot(hgrn_lb_logits.reshape(1, 2, 1, -1), 0, sel, F32))
    got0 = run_comm(Gather(bufs0), "gather_group0")
    gw = [{n: view(b, n) for n, b in zip(names0, got0[:-1])}, None]
    buf1 = {n: cast(n, 1, 2) for n in PER_LAYER}
    buf1["w_in_b"] = cast("w_in_b")
    got1 = {}
    w_in = {"a": _cols_from_chips(gw[0]["w_in_a"])}
    w_kv_full = _cols_from_chips(gw[0]["w_kv"][:, None])
    w_kv_full = jnp.pad(w_kv_full, ((0, 0), (0, 0), (0, KV_PAD - w_kv_full.shape[-1])))
    w_mkv = [_rows_from_chips(gw[0]["w_mem_kv"]), None]
    w_o = [_rows_from_chips(gw[0]["w_out"]), None]
    logits3 = jnp.moveaxis(got0[-1].reshape(N_CHIPS, 2, -1), 0, 1).reshape(2, 1, -1)
    lb3 = lb_fwd(logits3)
    carried = {(0, "mix"): FFN1, (0, "ffn2"): ["w_in_b"], (1, "ffn1"): ["w_out", "w_mem_kv"], (1, "mix"): FFN2}

    def gather1(key):
        names = carried.get(key)
        return None if names is None else Gather([buf1[n] for n in names])

    def landed1(key, res):
        if res is not None:
            got1.update({n: view(b, n) for n, b in zip(carried[key], res)})

    norm3 = {n: w[n].reshape(N_LAYERS, 1, d) for n in ("ffn1_norm", "mix_norm", "mem_norm", "ffn2_norm")}
    kvn3 = kv_norm.reshape(1, 1, d)
    mqg3, mkg3 = _pair_tile(mem_q_gain), _pair_tile(mem_k_gain)
    og3 = hgrn_o_gain.reshape(N_A, 1, LANES)
    fqg3 = _pair_tile(fox_q_gain)
    fkg = jnp.tile(fox_k_gain, 2).reshape(1, LANES)
    fb = jnp.pad(fox_f_bias, (0, LANES - fox_f_bias.shape[0])).reshape(1, LANES)

    sv = [dict() for _ in range(N_LAYERS)]
    h = xs
    kv = None
    for l in range(N_LAYERS):
        t = sv[l]
        gi, li = l // 2, l % 2
        if l == N_A:
            gw[1] = got1
            w_in["b"] = _cols_from_chips(got1["w_in_b"])
            w_mkv[1], w_o[1] = _rows_from_chips(got1["w_mem_kv"]), _rows_from_chips(got1["w_out"])
        t["x0"] = h
        (h, t["a1"], t["b1"]), res = ffn_fwd(h, norm3["ffn1_norm"], l, *[gw[gi][n] for n in FFN1], li, comm=gather1((l, "ffn1")))
        landed1((l, "ffn1"), res)
        t["x1"] = h
        if l < N_A:
            t["proj"] = proj_fwd(h, norm3["mix_norm"], l, w_in["a"], l)
            (main, t["o"]), res = hgrn_fwd(t["proj"], lb3, og3, l, comm=gather1((l, "mix")))
            landed1((l, "mix"), res)
            t["qblk"] = 12
        else:
            t["proj"] = proj_fwd(h, norm3["mix_norm"], l, w_in["b"], l - N_A)
            main, t["o"], t["lse"] = fox_fwd(t["proj"], kv["k"], kv["v"], kv["clf"], kv["clf_t"], fqg3, l - N_A)
            t["qblk"] = 6
        t["kvm"] = proj_fwd(mems, norm3["mem_norm"], l, w_mkv[gi], li)
        memo = memattn_fwd(t["proj"], t["qblk"], t["kvm"], mqg3, mkg3, l)
        t["mixed"] = jnp.concatenate([main, memo], axis=-1)
        h = mm_res(h, t["mixed"], w_o[gi], li)
        t["x2"] = h
        (h, t["a2"], t["b2"]), res = ffn_fwd(h, norm3["ffn2_norm"], l, *[gw[gi][n] for n in FFN2], li, comm=gather1((l, "ffn2")))
        landed1((l, "ffn2"), res)
        if l == N_A - 1:
            kv = {"x": h, "kvf": proj_fwd(h, kvn3, 0, w_kv_full, 0)}
            kv["k"], kv["v"], kv["clf"] = kvprep_fwd(kv["kvf"], fkg, fb)
            kv["clf_t"] = kv["clf"][:, :16].T

    loss_local, dx = loss_head(h, tgt)

    nc = N_CHIPS
    fc = ffn1_w_gate.shape[-1]
    gbuf = [dict(), dict()]
    for gi in range(2):
        for n in FFN1[:2] + FFN2[:2]:
            gbuf[gi][n] = lax.empty((2, nc, 1, d, fc), F32)
        for n in (FFN1[2], FFN2[2]):
            gbuf[gi][n] = lax.empty((2, nc, 1, fc, d), F32)

    def col_layout(stack):
        lw, dd, nn = stack.shape
        return jnp.transpose(stack.reshape(lw, dd, nc, nn // nc), (0, 2, 1, 3))

    def row_layout(stack):
        lw, rr, cc = stack.shape
        return stack.reshape(lw, nc, rr // nc, cc)

    def group_layout(gi):
        lay = {n: b.reshape(2, nc, -1, b.shape[-1]) for n, b in gbuf[gi].items()}
        lay["w_mem_kv"] = row_layout(jnp.stack(dw_mkv[2 * gi : 2 * gi + 2]))
        lay["w_out"] = row_layout(jnp.stack(dw_o[2 * gi : 2 * gi + 2]))
        if gi == 0:
            lay["w_in_a"] = col_layout(jnp.stack(dw_in["a"]))
            kv_cols = w_kv.shape[-1] * nc
            lay["w_kv"] = jnp.transpose(dw_kv[:, :kv_cols].reshape(2, d // 2, nc, kv_cols // nc), (0, 2, 1, 3))
            names = PER_LAYER + ["w_in_a", "w_kv"]
        else:
            lay["w_in_b"] = col_layout(jnp.stack(dw_in["b"]))
            names = PER_LAYER + ["w_in_b"]
        return names, [lay[n] for n in names]

    n_ffn = len(FFN1) + len(FFN2)
    names1 = gl1 = partial1 = landed1a = mine1 = both1 = None
    dw_in = {"a": [None] * N_A, "b": [None] * (N_LAYERS - N_A)}
    dw_o, dw_mkv = [None] * N_LAYERS, [None] * N_LAYERS
    sg = {n: [None] * N_LAYERS for n in ("ffn1_norm", "mix_norm", "mem_norm", "ffn2_norm", "mem_q_gain", "mem_k_gain")}
    sg["hgrn_o_gain"], sg["fox_q_gain"], dlb = [None] * N_A, [None] * (N_LAYERS - N_A), [None] * N_A
    dk_sh = jnp.zeros((s, KV_MAIN), F32)
    dv_sh = jnp.zeros((s, KV_MAIN), F32)
    dclf = jnp.zeros((s, LANES), F32)
    zero_mem = jnp.zeros(mems.shape, F32)
    dw_kv = None
    for l in reversed(range(N_LAYERS)):
        t = sv[l]
        gi, li = l // 2, l % 2
        if l == N_A - 1:
            dkvf, dfkg, dfb = kvprep_bwd(kv["kvf"], fkg, fb, dk_sh, dv_sh, dclf)
            dx, sg["kv_norm"], xn_kv, dpb = proj_bwd(kv["x"], kvn3, 0, [dkvf], w_kv_full, 0, dx)
            dw_kv = wgrad(xn_kv, dpb)
            names1, gl1 = group_layout(1)
        comm = PairExchange(gl1) if l == 1 else (PairShare(mine1) if l == 0 else None)
        (dx, da, db, hm, xn, dyb, sg["ffn2_norm"][l]), res = ffn_bwd(t["x2"], norm3["ffn2_norm"], l, dx, t["a2"], t["b2"], *[gw[gi][n] for n in FFN2], li, comm=comm)
        if l == 1:
            partial1 = [pair_sum(g, r, c_arr) for g, r in zip(gl1, res)]
        if l == 0:
            both1 = res
        gbuf[gi]["ffn2_w_gate"] = wgrad(xn, da, buf=gbuf[gi]["ffn2_w_gate"], l=li)
        gbuf[gi]["ffn2_w_up"] = wgrad(xn, db, buf=gbuf[gi]["ffn2_w_up"], l=li)
        gbuf[gi]["ffn2_w_down"] = wgrad(hm, dyb, buf=gbuf[gi]["ffn2_w_down"], l=li)
        dmixed, dxb = mm_nt(dx, w_o[gi], li)
        dw_o[l] = wgrad(t["mixed"], dxb)
        dqm, dkvm, dmq, dmk = memattn_bwd(t["proj"], t["qblk"], t["kvm"], mqg3, mkg3, l, dmixed)
        sg["mem_q_gain"][l], sg["mem_k_gain"][l] = _pair_fold(dmq), _pair_fold(dmk)
        _, sg["mem_norm"][l], memn, dkvmb = proj_bwd(mems, norm3["mem_norm"], l, [dkvm], w_mkv[gi], li, zero_mem)
        dw_mkv[l] = wgrad(memn, dkvmb)
        if l < N_A:
            comm = ChipExchange(partial1[:n_ffn]) if l == 1 else None
            (dzq, dzf, dvi, dzg, dlb[l], sg["hgrn_o_gain"][l]), res = hgrn_bwd(t["proj"], lb3, og3, l, t["o"], dmixed, comm=comm)
            if l == 1:
                landed1a = list(res)
            parts, key, wl, tn = [dzq, dzf, dvi, dzg, dqm], "a", l, 13 * LANES
        else:
            lse_t = t["lse"].reshape(s, 6, LANES)[:, :, :2].reshape(s, 12).T
            lse_t = jnp.pad(lse_t, ((0, 4), (0, 0)))
            dq, dgate, dk_sh, dv_sh, dclf, dfq = fox_bwd(t["proj"], kv["k"], kv["v"], kv["clf"], kv["clf_t"], fqg3, l - N_A, t["o"], t["lse"], lse_t, dmixed, dk_sh, dv_sh, dclf)
            sg["fox_q_gain"][l - N_A] = _pair_fold(dfq)
            parts, key, wl, tn = [dq, dgate, dqm], "b", l - N_A, 7 * LANES
        dx, sg["mix_norm"][l], hn, dpb = proj_bwd(t["x1"], norm3["mix_norm"], l, parts, w_in[key], wl, dx)
        dw_in[key][wl] = wgrad(hn, dpb, tn=tn)
        comm = ChipExchange(partial1[n_ffn:]) if l == 1 else None
        (dx, da, db, hm, xn, dyb, sg["ffn1_norm"][l]), res = ffn_bwd(t["x0"], norm3["ffn1_norm"], l, dx, t["a1"], t["b1"], *[gw[gi][n] for n in FFN1], li, comm=comm)
        if l == 1:
            mine1 = [chip_sum(p, q, sel) for p, q in zip(partial1, landed1a + list(res))]
        gbuf[gi]["ffn1_w_gate"] = wgrad(xn, da, buf=gbuf[gi]["ffn1_w_gate"], l=li)
        gbuf[gi]["ffn1_w_up"] = wgrad(xn, db, buf=gbuf[gi]["ffn1_w_up"], l=li)
        gbuf[gi]["ffn1_w_down"] = wgrad(hm, dyb, buf=gbuf[gi]["ffn1_w_down"], l=li)

    names0, gl0 = group_layout(0)
    recv0 = run_comm(PairExchange(gl0), "pair_exchange")
    partial0 = [pair_sum(g, r, c_arr) for g, r in zip(gl0, recv0)]
    landed0 = run_comm(ChipExchange(partial0), "chip_exchange")
    mine0 = [chip_sum(p, q, sel) for p, q in zip(partial0, landed0)]
    both0 = run_comm(PairShare(mine0), "pair_share")
    reduced = [dict(zip(names0, both0)), dict(zip(names1, both1))]
    gparts = {n: [reduced[0][n], reduced[1][n]] for n in PER_LAYER}
    gparts.update({"w_in_a": [reduced[0]["w_in_a"]], "w_kv": [reduced[0]["w_kv"]], "w_in_b": [reduced[1]["w_in_b"]]})

    dlogits = lb_bwd(logits3, dlb[1]).reshape(2, -1)
    small = {
        "ffn1_norm": jnp.concatenate(sg["ffn1_norm"]), "mix_norm": jnp.concatenate(sg["mix_norm"]),
        "mem_norm": jnp.concatenate(sg["mem_norm"]), "ffn2_norm": jnp.concatenate(sg["ffn2_norm"]),
        "mem_q_gain": jnp.concatenate(sg["mem_q_gain"]), "mem_k_gain": jnp.concatenate(sg["mem_k_gain"]),
        "hgrn_o_gain": jnp.concatenate(sg["hgrn_o_gain"]), "fox_q_gain": jnp.concatenate(sg["fox_q_gain"]),
        "kv_norm": sg["kv_norm"], "fox_f_bias": dfb[:, : fox_f_bias.shape[0]], "fox_k_gain": _pair_fold(dfkg),
        "hgrn_lb_logits": dlogits,
    }
    flat = [small[n].reshape(-1) for n in SMALL] + [loss_local.reshape(-1)]
    sizes = [f.shape[0] for f in flat]
    total = sum(sizes)
    padded = -(-total // (8 * LANES)) * (8 * LANES)
    packed = jnp.pad(jnp.concatenate(flat), (0, padded - total)).reshape(-1, LANES)
    summed = small_allreduce(packed).reshape(-1)
    off = 0
    for n, sz in zip(SMALL, sizes[:-1]):
        gparts[n] = [summed[off : off + sz].reshape(dlogits.shape if n == "hgrn_lb_logits" else w[n].shape)]
        off += sz
    loss = summed[off]
    lbw = hgrn_lb_logits.shape[1]
    gparts["hgrn_lb_logits"] = [lax.dynamic_slice_in_dim(gparts["hgrn_lb_logits"][0], my_chip * lbw, lbw, axis=1)]

    grads, delta, new_m, new_v = {}, {}, {}, {}
    for n in WEIGHT_NAMES:
        grads[n], delta[n], new_m[n], new_v[n] = adamw(w[n], gparts[n], given["m_" + n], given["v_" + n])
    return (loss, dx[None], *[grads[n] for n in WEIGHT_NAMES], *[delta[n] for n in WEIGHT_NAMES],
            *[new_m[n] for n in WEIGHT_NAMES], *[new_v[n] for n in WEIGHT_NAMES])
```

```python
import functools

import jax
import jax.numpy as jnp
from jax import lax
from jax.experimental import pallas as pl
from jax.experimental.pallas import tpu as pltpu

F32, BF16 = jnp.float32, jnp.bfloat16
HI = lax.Precision.HIGHEST
EPS = 1e-6
MESH = pl.DeviceIdType.MESH
ANY = pl.BlockSpec(memory_space=pl.ANY)

VMEM_LIMIT_BYTES = 56 << 20
N_CHIPS = 4
N_DEV = 8
LANES = 128
HEAD64 = 64
CHUNK = 64
SUB = 16
HGRN_HEADS_PER_STEP = 2
TQ = 256
TOK = 256

ADAM_LR, ADAM_B1, ADAM_B2, ADAM_EPS, ADAM_WD, ADAM_STEP = 0.001, 0.9, 0.999, 1e-08, 0.01, 10


def _cparams(sem=None):
    return pltpu.CompilerParams(dimension_semantics=sem, vmem_limit_bytes=VMEM_LIMIT_BYTES)


def _mm(a, b, dims, prec=None):
    return lax.dot_general(a, b, (dims, ((), ())), preferred_element_type=F32, precision=prec)


def dot_nn(a, b, prec=None):
    return _mm(a, b, ((1,), (0,)), prec)


def dot_nt(a, b, prec=None):
    return _mm(a, b, ((1,), (1,)), prec)


def dot_tn(a, b, prec=None):
    return _mm(a, b, ((0,), (0,)), prec)


def bf(v):
    return v.astype(BF16)


def _sigmoid(z):
    return jax.nn.sigmoid(z)


def _dsilu(z, s):
    return s * (1.0 + z * (1.0 - s))


def _rms(x):
    r = lax.rsqrt(jnp.mean(x * x, axis=-1, keepdims=True) + EPS)
    return x * r, r


def _rms_bwd(dxn, u, r, g):
    du = dxn * g
    dx = r * (du - u * jnp.mean(du * u, axis=-1, keepdims=True))
    return dx, jnp.sum(dxn * u, axis=0, keepdims=True)


def _lane_mask0(shape):
    return lax.broadcasted_iota(jnp.int32, shape, len(shape) - 1) < HEAD64


def _rms64(x, m0):
    sq = x * x
    s0 = jnp.sum(jnp.where(m0, sq, 0.0), axis=-1, keepdims=True)
    s1 = jnp.sum(jnp.where(m0, 0.0, sq), axis=-1, keepdims=True)
    r = lax.rsqrt(jnp.where(m0, s0, s1) * (1.0 / HEAD64) + EPS)
    return x * r, r


def _rms64_bwd(dxn, u, r, g, m0):
    du = dxn * g
    t = du * u
    t0 = jnp.sum(jnp.where(m0, t, 0.0), axis=-1, keepdims=True)
    t1 = jnp.sum(jnp.where(m0, 0.0, t), axis=-1, keepdims=True)
    dx = r * (du - u * (jnp.where(m0, t0, t1) * (1.0 / HEAD64)))
    return dx, jnp.sum(dxn * u, axis=0, keepdims=True)


def _tok(s):
    return TOK if s % TOK == 0 else s


def _const(shape):
    return pl.BlockSpec(shape, lambda *_: (0,) * len(shape))


def ffn_fwd(x, gain3, l, wg, wu, wd, wl, comm=None):
    s, d = x.shape
    nc, _, fc, _ = wg.shape
    tm = _tok(s)

    def body(x_ref, g_ref, wg_ref, wu_ref, wd_ref, xo_ref, a_ref, b_ref):
        xv = x_ref[...]
        u, _ = _rms(xv)
        xn = bf(u * g_ref[...])
        y = jnp.zeros((tm, d), F32)
        for c in range(nc):
            a = dot_nt(xn, wg_ref[c])
            b = dot_nt(xn, wu_ref[c])
            a_ref[c] = bf(a)
            b_ref[c] = bf(b)
            y = y + dot_nn(bf(a * _sigmoid(a) * b), wd_ref[c])
        xo_ref[...] = xv + 0.5 * y

    wspec = pl.BlockSpec((nc, None, fc, d), lambda i: (0, wl, 0, 0), pipeline_mode=pl.Buffered(1))
    wdspec = pl.BlockSpec((nc, None, fc, d), lambda i: (0, wl, 0, 0), pipeline_mode=pl.Buffered(1))
    row = pl.BlockSpec((tm, d), lambda i: (i, 0))
    act = pl.BlockSpec((nc, tm, fc), lambda i: (0, i, 0))
    return _carry(
        body,
        comm,
        name="ffn_fwd",
        grid=(s // tm,),
        in_specs=[row, pl.BlockSpec((None, 1, d), lambda i: (l, 0, 0)), wspec, wspec, wdspec],
        out_specs=[row, act, act],
        out_shape=[
            jax.ShapeDtypeStruct((s, d), F32),
            jax.ShapeDtypeStruct((nc, s, fc), BF16),
            jax.ShapeDtypeStruct((nc, s, fc), BF16),
        ],
        scratch_shapes=[],
        args=(x, gain3, wg, wu, wd),
    )


def ffn_bwd(x, gain3, l, dout, a, b, wg, wu, wd, wl, comm=None):
    s, d = x.shape
    nc, _, fc, _ = wg.shape
    tm = _tok(s)

    def body(x_ref, g_ref, do_ref, a_ref, b_ref, wg_ref, wu_ref, wd_ref, dx_ref, da_ref, db_ref, hm_ref, xn_ref, dy_ref, dg_ref):
        xv = x_ref[...]
        g = g_ref[...]
        u, r = _rms(xv)
        xn_ref[...] = bf(u * g)
        dout = do_ref[...]
        dy = bf(0.5 * dout)
        dy_ref[...] = dy
        dxn = jnp.zeros((tm, d), F32)
        for c in range(nc):
            av = a_ref[c].astype(F32)
            bv = b_ref[c].astype(F32)
            sg = _sigmoid(av)
            sl = av * sg
            dh = dot_nt(dy, wd_ref[c])
            da = bf(dh * bv * _dsilu(av, sg))
            db = bf(dh * sl)
            da_ref[c] = da
            db_ref[c] = db
            hm_ref[c] = bf(sl * bv)
            dxn = dxn + dot_nn(da, wg_ref[c]) + dot_nn(db, wu_ref[c])
        dx, dg = _rms_bwd(dxn, u, r, g)
        dx_ref[...] = dout + dx

        @pl.when(pl.program_id(0) == 0)
        def _():
            dg_ref[...] = jnp.zeros_like(dg_ref)

        dg_ref[...] += dg

    wspec = pl.BlockSpec((nc, None, fc, d), lambda i: (0, wl, 0, 0), pipeline_mode=pl.Buffered(1))
    wdspec = pl.BlockSpec((nc, None, fc, d), lambda i: (0, wl, 0, 0), pipeline_mode=pl.Buffered(1))
    row = pl.BlockSpec((tm, d), lambda i: (i, 0))
    act = pl.BlockSpec((nc, tm, fc), lambda i: (0, i, 0))
    act_shape = jax.ShapeDtypeStruct((nc, s, fc), BF16)
    return _carry(
        body,
        comm,
        name="ffn_bwd",
        grid=(s // tm,),
        in_specs=[row, pl.BlockSpec((None, 1, d), lambda i: (l, 0, 0)), row, act, act, wspec, wspec, wdspec],
        out_specs=[row, act, act, act, row, row, _const((1, d))],
        out_shape=[
            jax.ShapeDtypeStruct((s, d), F32),
            act_shape,
            act_shape,
            act_shape,
            jax.ShapeDtypeStruct((s, d), BF16),
            jax.ShapeDtypeStruct((s, d), BF16),
            jax.ShapeDtypeStruct((1, d), F32),
        ],
        scratch_shapes=[],
        args=(x, gain3, dout, a, b, wg, wu, wd),
    )


def wgrad(a, b, tn=None, buf=None, l=None, tm=None):
    ca = a.shape[0] if a.ndim == 3 else 1
    cb = b.shape[0] if b.ndim == 3 else 1
    nc = max(ca, cb)
    s, m = a.shape[-2:]
    n = b.shape[-1]
    tn = n if tn is None else tn
    assert n % tn == 0

    def body(*refs):
        a_ref, b_ref, o_ref = refs[-3:] if buf is None else (refs[0], refs[1], refs[3])
        o_ref[...] = dot_tn(a_ref[...], b_ref[...])

    if buf is None:
        assert nc == 1 and a.ndim == 2 and b.ndim == 2
        tm = m if tm is None else tm
        assert m % tm == 0
        return pl.pallas_call(
            body,
            name="wgrad",
            grid=(m // tm, n // tn),
            in_specs=[pl.BlockSpec((s, tm), lambda i, j: (0, i)), pl.BlockSpec((s, tn), lambda i, j: (0, j))],
            out_specs=pl.BlockSpec((tm, tn), lambda i, j: (i, j)),
            out_shape=jax.ShapeDtypeStruct((m, n), F32),
            compiler_params=_cparams(("arbitrary", "arbitrary")),
        )(a, b)
    a_spec = pl.BlockSpec((None, s, m), lambda c, j: (c, 0, 0)) if a.ndim == 3 else pl.BlockSpec((s, m), lambda c, j: (0, 0))
    b_spec = pl.BlockSpec((None, s, tn), lambda c, j: (c, 0, j)) if b.ndim == 3 else pl.BlockSpec((s, tn), lambda c, j: (0, j))
    lh = buf.shape[2]
    hi, lo = l // lh, l % lh
    o_spec = pl.BlockSpec((None, None, None, m, tn), lambda c, j: (hi, c, lo, 0, j))
    return pl.pallas_call(
        body,
        name="wgrad_buf",
        grid=(nc, n // tn),
        in_specs=[a_spec, b_spec, ANY],
        out_specs=o_spec,
        out_shape=jax.ShapeDtypeStruct(buf.shape, F32),
        input_output_aliases={2: 0},
        compiler_params=_cparams(("arbitrary", "arbitrary")),
    )(a, b, buf)


def proj_fwd(x, gain3, l, w, wl, wt=False):
    s, d = x.shape
    n = w.shape[1] if wt else w.shape[2]
    tm = _tok(s)

    def body(x_ref, g_ref, w_ref, o_ref):
        u, _ = _rms(x_ref[...])
        xn = bf(u * g_ref[...])
        o_ref[...] = dot_nt(xn, w_ref[...]) if wt else dot_nn(xn, w_ref[...])

    return pl.pallas_call(
        body,
        name="proj_fwd",
        grid=(s // tm,),
        in_specs=[
            pl.BlockSpec((tm, d), lambda i: (i, 0)),
            pl.BlockSpec((None, 1, d), lambda i: (l, 0, 0)),
            pl.BlockSpec((None,) + w.shape[1:], lambda i: (wl, 0, 0)),
        ],
        out_specs=pl.BlockSpec((tm, n), lambda i: (i, 0)),
        out_shape=jax.ShapeDtypeStruct((s, n), F32),
        compiler_params=_cparams(("arbitrary",)),
    )(x, gain3, w)


def proj_bwd(x, gain3, l, parts, w, wl, dx_in, wt=False):
    s, d = x.shape
    n = w.shape[1] if wt else w.shape[2]
    widths = [p.shape[1] for p in parts]
    assert sum(widths) == n
    tm = _tok(s)
    npart = len(parts)

    def body(*refs):
        x_ref, g_ref, w_ref, dxin_ref = refs[:4]
        p_refs = refs[4 : 4 + npart]
        dx_ref, dg_ref, xn_ref, dpb_ref = refs[4 + npart :]
        g = g_ref[...]
        u, r = _rms(x_ref[...])
        xn_ref[...] = bf(u * g)
        dxn = jnp.zeros((tm, d), F32)
        off = 0
        for p_ref, wd_ in zip(p_refs, widths):
            dp = bf(p_ref[...])
            dpb_ref[:, off : off + wd_] = dp
            dxn = dxn + (dot_nn(dp, w_ref[off : off + wd_, :]) if wt else dot_nt(dp, w_ref[:, off : off + wd_]))
            off += wd_
        dx, dg = _rms_bwd(dxn, u, r, g)
        dx_ref[...] = dxin_ref[...] + dx

        @pl.when(pl.program_id(0) == 0)
        def _():
            dg_ref[...] = jnp.zeros_like(dg_ref)

        dg_ref[...] += dg

    row = pl.BlockSpec((tm, d), lambda i: (i, 0))
    return pl.pallas_call(
        body,
        name="proj_bwd",
        grid=(s // tm,),
        in_specs=[row, pl.BlockSpec((None, 1, d), lambda i: (l, 0, 0)), pl.BlockSpec((None,) + w.shape[1:], lambda i: (wl, 0, 0)), row]
        + [pl.BlockSpec((tm, wd_), lambda i: (i, 0)) for wd_ in widths],
        out_specs=[row, _const((1, d)), row, pl.BlockSpec((tm, n), lambda i: (i, 0))],
        out_shape=[
            jax.ShapeDtypeStruct((s, d), F32),
            jax.ShapeDtypeStruct((1, d), F32),
            jax.ShapeDtypeStruct((s, d), BF16),
            jax.ShapeDtypeStruct((s, n), BF16),
        ],
        compiler_params=_cparams(("arbitrary",)),
    )(x, gain3, w, dx_in, *parts)


def mm_res(x, a, w, l):
    s, d = x.shape
    k = a.shape[1]
    tm = _tok(s)

    def body(x_ref, a_ref, w_ref, o_ref):
        o_ref[...] = x_ref[...] + dot_nn(a_ref[...], w_ref[...])

    return pl.pallas_call(
        body,
        name="mm_res",
        grid=(s // tm,),
        in_specs=[
            pl.BlockSpec((tm, d), lambda i: (i, 0)),
            pl.BlockSpec((tm, k), lambda i: (i, 0)),
            pl.BlockSpec((None, k, d), lambda i: (l, 0, 0)),
        ],
        out_specs=pl.BlockSpec((tm, d), lambda i: (i, 0)),
        out_shape=jax.ShapeDtypeStruct((s, d), F32),
        compiler_params=_cparams(("arbitrary",)),
    )(x, a, w)


def mm_nt(dx, w, l):
    s, d = dx.shape
    k = w.shape[1]
    tm = _tok(s)

    def body(dx_ref, w_ref, o_ref, dxb_ref):
        dxb = bf(dx_ref[...])
        dxb_ref[...] = dxb
        o_ref[...] = dot_nt(dxb, w_ref[...])

    return pl.pallas_call(
        body,
        name="mm_nt",
        grid=(s // tm,),
        in_specs=[pl.BlockSpec((tm, d), lambda i: (i, 0)), pl.BlockSpec((None, k, d), lambda i: (l, 0, 0))],
        out_specs=[pl.BlockSpec((tm, k), lambda i: (i, 0)), pl.BlockSpec((tm, d), lambda i: (i, 0))],
        out_shape=[jax.ShapeDtypeStruct((s, k), F32), jax.ShapeDtypeStruct((s, d), BF16)],
        compiler_params=_cparams(("arbitrary",)),
    )(dx, w)


def lb_fwd(logits3):
    def body(l_ref, o_ref):
        l0, l1 = l_ref[0], l_ref[1]
        m = jnp.maximum(l0, l1)
        e0, e1 = jnp.exp(l0 - m), jnp.exp(l1 - m)
        p0, p1 = e0 / (e0 + e1), e1 / (e0 + e1)
        o_ref[0] = p0 - p0
        o_ref[1] = (p0 + p1) - p0

    return pl.pallas_call(body, name="lb_fwd", out_shape=jax.ShapeDtypeStruct(logits3.shape, F32))(logits3)


def lb_bwd(logits3, dlb1):
    def body(l_ref, d_ref, o_ref):
        l0, l1 = l_ref[0], l_ref[1]
        m = jnp.maximum(l0, l1)
        e0, e1 = jnp.exp(l0 - m), jnp.exp(l1 - m)
        p0, p1 = e0 / (e0 + e1), e1 / (e0 + e1)
        t = d_ref[...] * p0 * p1
        o_ref[0] = -t
        o_ref[1] = t

    return pl.pallas_call(body, name="lb_bwd", out_shape=jax.ShapeDtypeStruct(logits3.shape, F32))(logits3, dlb1)


def _hgrn_gates(zq, zf, lb):
    sf = _sigmoid(zf)
    f = lb + (1.0 - lb) * sf
    sq = _sigmoid(zq)
    return sf, f, jnp.log(f), 1.0 - f, sq, zq * sq


def _tri(n, upper=False):
    r = lax.broadcasted_iota(jnp.int32, (n, n), 0)
    c = lax.broadcasted_iota(jnp.int32, (n, n), 1)
    return jnp.where((c >= r) if upper else (r >= c), 1.0, 0.0).astype(F32)


def hgrn_fwd(proj, lb3, og3, l, comm=None):
    s = proj.shape[0]
    nh = 6
    n_chunk = s // CHUNK
    nsub = CHUNK // SUB

    hb = HGRN_HEADS_PER_STEP
    wide = hb * LANES

    def body(zq_ref, zf_ref, vi_ref, zg_ref, lb_ref, og_ref, main_ref, o_ref, q_a, k_a, v_a, c_a):
        og = og_ref[...]
        tril = _tri(CHUNK)
        rowi = lax.broadcasted_iota(jnp.int32, (SUB, LANES), 0)

        def one_head(hd, rows, st):
            cols = slice(hd * LANES, (hd + 1) * LANES)
            q_s, k_s, v_s, c_s = q_a.at[hd], k_a.at[hd], v_a.at[hd], c_a.at[hd]
            zg = zg_ref[rows, cols]
            _, _, lf, k, _, q = _hgrn_gates(zq_ref[rows, cols], zf_ref[rows, cols], lb_ref[:, cols])
            v = vi_ref[rows, cols]
            c = dot_nn(tril, lf, HI)
            q_s[...] = q
            k_s[...] = k
            v_s[...] = v
            c_s[...] = c
            o_inter = dot_nt(q * jnp.exp(c), st, HI)
            parts = []
            for i in range(nsub):
                lo = i * SUB
                blk = pl.ds(lo, SUB)
                qb, cb = q_s[blk, :], c_s[blk, :]
                ob = o_inter[lo : lo + SUB]
                if i > 0:
                    rr = c_s[pl.ds(lo - 1, 1), :]
                    qt = qb * jnp.exp(cb - rr)
                    kt = k_s[pl.ds(0, lo), :] * jnp.exp(rr - c_s[pl.ds(0, lo), :])
                    ob = ob + dot_nn(dot_nt(qt, kt, HI), v_s[pl.ds(0, lo), :], HI)
                for t in range(SUB):
                    e = jnp.where(rowi >= t, jnp.exp(cb - c_s[pl.ds(lo + t, 1), :]), 0.0)
                    a = jnp.sum(qb * k_s[pl.ds(lo + t, 1), :] * e, axis=-1, keepdims=True)
                    ob = ob + a * v_s[pl.ds(lo + t, 1), :]
                parts.append(ob)
            o = jnp.concatenate(parts, axis=0)
            ce = c_s[pl.ds(CHUNK - 1, 1), :]
            st = st * jnp.exp(ce) + dot_tn(v, k * jnp.exp(ce - c), HI)
            on, _ = _rms(o)
            o_ref[rows, cols] = o
            main_ref[rows, cols] = bf(on * og * (zg * _sigmoid(zg)))
            return st

        def chunk(ci, sts):
            rows = pl.ds(pl.multiple_of(ci * CHUNK, CHUNK), CHUNK)
            return tuple(one_head(hd, rows, sts[hd]) for hd in range(hb))

        lax.fori_loop(0, n_chunk, chunk, tuple(jnp.zeros((LANES, LANES), F32) for _ in range(hb)))

    def col(k):
        return pl.BlockSpec((s, wide), lambda h: (0, k * (nh // hb) + h))

    vec = pl.BlockSpec((None, 1, wide), lambda h: (l, 0, h))
    return _carry(
        body,
        comm,
        name="hgrn_fwd",
        grid=(nh // hb,),
        in_specs=[col(0), col(1), col(2), col(3), vec, pl.BlockSpec((None, 1, LANES), lambda h: (l, 0, 0))],
        out_specs=[pl.BlockSpec((s, wide), lambda h: (0, h))] * 2,
        out_shape=[jax.ShapeDtypeStruct((s, nh * LANES), BF16), jax.ShapeDtypeStruct((s, nh * LANES), F32)],
        scratch_shapes=[pltpu.VMEM((hb, CHUNK, LANES), F32)] * 4,
        args=(proj, proj, proj, proj, lb3, og3),
    )


def hgrn_bwd(proj, lb3, og3, l, o, dmixed, comm=None):
    s = proj.shape[0]
    nh = 6
    n_chunk = s // CHUNK
    nsub = CHUNK // SUB

    hb = HGRN_HEADS_PER_STEP
    wide = hb * LANES

    def body(zq_ref, zf_ref, vi_ref, zg_ref, lb_ref, og_ref, o_ref, dm_ref,
             dzq_ref, dzf_ref, dvi_ref, dzg_ref, dlb_ref, dog_ref,
             st_a, q_a, k_a, v_a, c_a, do_a, dq_a, dk_a, dv_a, acc_a):
        og = og_ref[...]
        tril = _tri(CHUNK)
        triu = _tri(CHUNK, upper=True)
        rowi = lax.broadcasted_iota(jnp.int32, (SUB, LANES), 0)

        def fwd_head(hd, ci, rows, st):
            cols = slice(hd * LANES, (hd + 1) * LANES)
            _, _, lf, k, _, _ = _hgrn_gates(zq_ref[rows, cols], zf_ref[rows, cols], lb_ref[:, cols])
            c = dot_nn(tril, lf, HI)
            ce = jnp.sum(lf, axis=0, keepdims=True)
            st_a[hd, ci] = st
            return st * jnp.exp(ce) + dot_tn(vi_ref[rows, cols], k * jnp.exp(ce - c), HI)

        def fwd_chunk(ci, sts):
            rows = pl.ds(pl.multiple_of(ci * CHUNK, CHUNK), CHUNK)
            return tuple(fwd_head(hd, ci, rows, sts[hd]) for hd in range(hb))

        lax.fori_loop(0, n_chunk, fwd_chunk, tuple(jnp.zeros((LANES, LANES), F32) for _ in range(hb)))
        acc_a[...] = jnp.zeros_like(acc_a)

        def bwd_head(hd, ci, rows, carry):
            dst, cg = carry
            cols = slice(hd * LANES, (hd + 1) * LANES)
            q_s, k_s, v_s, c_s, do_s = q_a.at[hd], k_a.at[hd], v_a.at[hd], c_a.at[hd], do_a.at[hd]
            dq_s, dk_s, dv_s, acc_s = dq_a.at[hd], dk_a.at[hd], dv_a.at[hd], acc_a.at[hd]
            lb = lb_ref[:, cols]
            zq, zf, zg = zq_ref[rows, cols], zf_ref[rows, cols], zg_ref[rows, cols]
            sf, f, lf, k, sq, q = _hgrn_gates(zq, zf, lb)
            v = vi_ref[rows, cols]
            c = dot_nn(tril, lf, HI)
            st = st_a[hd, ci]
            on, r = _rms(o_ref[rows, cols])
            sg = _sigmoid(zg)
            dmain = dm_ref[rows, cols]
            dy = dmain * (zg * sg)
            dzg_ref[rows, cols] = dmain * (on * og) * _dsilu(zg, sg)
            do, dog = _rms_bwd(dy, on, r, og)
            acc_s[pl.ds(0, 1), :] += dog
            q_s[...] = q
            k_s[...] = k
            v_s[...] = v
            c_s[...] = c
            do_s[...] = do
            ce = c_s[pl.ds(CHUNK - 1, 1), :]
            eq = jnp.exp(c)
            ek = jnp.exp(ce - c)
            qt_all = q * eq
            dq_s[...] = dot_nn(do, st, HI) * eq
            dv_s[...] = dot_nt(k * ek, dst, HI)
            dk_s[...] = dot_nn(v, dst, HI) * ek
            dst = dst * jnp.exp(ce) + dot_tn(do, qt_all, HI)
            for i in range(nsub):
                lo = i * SUB
                blk = pl.ds(lo, SUB)
                qb, cb, dob = q_s[blk, :], c_s[blk, :], do_s[blk, :]
                if i > 0:
                    prev = pl.ds(0, lo)
                    rr = c_s[pl.ds(lo - 1, 1), :]
                    eqi = jnp.exp(cb - rr)
                    eki = jnp.exp(rr - c_s[prev, :])
                    qt = qb * eqi
                    kt = k_s[prev, :] * eki
                    amat = dot_nt(qt, kt, HI)
                    damat = dot_nt(dob, v_s[prev, :], HI)
                    dv_s[prev, :] += dot_tn(amat, dob, HI)
                    dq_s[blk, :] += dot_nn(damat, kt, HI) * eqi
                    dk_s[prev, :] += dot_tn(damat, qt, HI) * eki
                dqb = jnp.zeros((SUB, LANES), F32)
                for t in range(SUB):
                    row = pl.ds(lo + t, 1)
                    e = jnp.where(rowi >= t, jnp.exp(cb - c_s[row, :]), 0.0)
                    kr = k_s[row, :]
                    a = jnp.sum(qb * kr * e, axis=-1, keepdims=True)
                    da = jnp.sum(dob * v_s[row, :], axis=-1, keepdims=True)
                    dv_s[row, :] += jnp.sum(a * dob, axis=0, keepdims=True)
                    dqb = dqb + da * kr * e
                    dk_s[row, :] += jnp.sum(da * qb * e, axis=0, keepdims=True)
                dq_s[blk, :] += dqb
            dq, dk = dq_s[...], dk_s[...]
            dg = q * dq - k * dk
            dlf = dot_nn(triu, dg, HI) + cg
            cg = cg + jnp.sum(dg, axis=0, keepdims=True)
            df = dlf / f - dk
            dzf_ref[rows, cols] = df * (1.0 - lb) * sf * (1.0 - sf)
            acc_s[pl.ds(1, 1), :] += jnp.sum(df * (1.0 - sf), axis=0, keepdims=True)
            dzq_ref[rows, cols] = dq * _dsilu(zq, sq)
            dvi_ref[rows, cols] = dv_s[...]
            return dst, cg

        def bwd_chunk(jj, carries):
            ci = n_chunk - 1 - jj
            rows = pl.ds(pl.multiple_of(ci * CHUNK, CHUNK), CHUNK)
            return tuple(bwd_head(hd, ci, rows, carries[hd]) for hd in range(hb))

        zero = (jnp.zeros((LANES, LANES), F32), jnp.zeros((1, LANES), F32))
        lax.fori_loop(0, n_chunk, bwd_chunk, tuple(zero for _ in range(hb)))

        @pl.when(pl.program_id(0) == 0)
        def _():
            dog_ref[...] = jnp.zeros_like(dog_ref)

        for hd in range(hb):
            dlb_ref[:, hd * LANES : (hd + 1) * LANES] = acc_a[hd, pl.ds(1, 1), :]
            dog_ref[...] += acc_a[hd, pl.ds(0, 1), :]

    def col(k):
        return pl.BlockSpec((s, wide), lambda h: (0, k * (nh // hb) + h), pipeline_mode=pl.Buffered(1))

    head_in = pl.BlockSpec((s, wide), lambda h: (0, h), pipeline_mode=pl.Buffered(1))
    head = pl.BlockSpec((s, wide), lambda h: (0, h))
    vec = pl.BlockSpec((None, 1, wide), lambda h: (l, 0, h))
    ck = pltpu.VMEM((hb, CHUNK, LANES), F32)
    return _carry(
        body,
        comm,
        name="hgrn_bwd",
        grid=(nh // hb,),
        in_specs=[col(0), col(1), col(2), col(3), vec, pl.BlockSpec((None, 1, LANES), lambda h: (l, 0, 0)), head_in, head_in],
        out_specs=[head] * 4 + [pl.BlockSpec((1, wide), lambda h: (0, h)), _const((1, LANES))],
        out_shape=[jax.ShapeDtypeStruct((s, nh * LANES), F32)] * 4
        + [jax.ShapeDtypeStruct((1, nh * LANES), F32), jax.ShapeDtypeStruct((1, LANES), F32)],
        scratch_shapes=[pltpu.VMEM((hb, n_chunk, LANES, LANES), F32)] + [ck] * 8 + [pltpu.VMEM((hb, 8, LANES), F32)],
        args=(proj, proj, proj, proj, lb3, og3, o, dmixed),
    )


MEM_SCALE = HEAD64**-0.5


def _mem_heads(qraw, kvm, qg, kg, pr, m0):
    lo = pr * LANES
    uq, rq = _rms64(qraw[:, lo : lo + LANES], m0)
    uk, rk = _rms64(kvm[:, lo : lo + LANES], m0)
    v = bf(kvm[:, 2 * LANES + lo : 3 * LANES + lo])
    return uq, rq, uk, rk, v, uq * qg, bf(uk * kg)


def memattn_fwd(proj, qblk, kvm, qg3, kg3, l):
    s = proj.shape[0]
    nm = kvm.shape[0]
    tm = _tok(s)

    def body(q_ref, kv_ref, qg_ref, kg_ref, o_ref):
        m0 = _lane_mask0((1, LANES))
        qraw, kvv = q_ref[...], kv_ref[...]
        for pr in range(2):
            _, _, _, _, v, qn, kn = _mem_heads(qraw, kvv, qg_ref[...], kg_ref[...], pr, m0)
            out = jnp.zeros((tm, LANES), F32)
            for hh in range(2):
                mh = m0 if hh == 0 else jnp.logical_not(m0)
                sc = dot_nt(bf(jnp.where(mh, qn, 0.0)), kn) * MEM_SCALE
                p = jnp.exp(sc - jnp.max(sc, axis=-1, keepdims=True))
                p = p / jnp.sum(p, axis=-1, keepdims=True)
                out = jnp.where(mh, dot_nn(bf(p), v), out)
            o_ref[:, pr * LANES : (pr + 1) * LANES] = bf(out)

    gspec = pl.BlockSpec((None, 1, LANES), lambda i: (l, 0, 0))
    return pl.pallas_call(
        body,
        name="memattn_fwd",
        grid=(s // tm,),
        in_specs=[pl.BlockSpec((tm, 2 * LANES), lambda i: (i, qblk)), _const((nm, 4 * LANES)), gspec, gspec],
        out_specs=pl.BlockSpec((tm, 2 * LANES), lambda i: (i, 0)),
        out_shape=jax.ShapeDtypeStruct((s, 2 * LANES), BF16),
        compiler_params=_cparams(("arbitrary",)),
    )(proj, kvm, qg3, kg3)


def memattn_bwd(proj, qblk, kvm, qg3, kg3, l, dmixed):
    s = proj.shape[0]
    nm = kvm.shape[0]
    tm = _tok(s)

    def body(q_ref, kv_ref, qg_ref, kg_ref, dm_ref, dq_ref, dkv_ref, dqg_ref, dkg_ref):
        m0 = _lane_mask0((1, LANES))
        qraw, kvv = q_ref[...], kv_ref[...]
        qg, kg = qg_ref[...], kg_ref[...]

        @pl.when(pl.program_id(0) == 0)
        def _():
            dkv_ref[...] = jnp.zeros_like(dkv_ref)
            dqg_ref[...] = jnp.zeros_like(dqg_ref)
            dkg_ref[...] = jnp.zeros_like(dkg_ref)

        for pr in range(2):
            lo = pr * LANES
            uq, rq, uk, rk, v, qn, kn = _mem_heads(qraw, kvv, qg, kg, pr, m0)
            do = dm_ref[:, lo : lo + LANES]
            dqn = jnp.zeros((tm, LANES), F32)
            dkn = jnp.zeros((nm, LANES), F32)
            dv = jnp.zeros((nm, LANES), F32)
            for hh in range(2):
                mh = m0 if hh == 0 else jnp.logical_not(m0)
                qh = bf(jnp.where(mh, qn, 0.0))
                doh = bf(jnp.where(mh, do, 0.0))
                sc = dot_nt(qh, kn) * MEM_SCALE
                p = jnp.exp(sc - jnp.max(sc, axis=-1, keepdims=True))
                p = p / jnp.sum(p, axis=-1, keepdims=True)
                dp = dot_nt(doh, v)
                ds = bf(p * (dp - jnp.sum(p * dp, axis=-1, keepdims=True)))
                dqn = dqn + jnp.where(mh, dot_nn(ds, kn), 0.0) * MEM_SCALE
                dkn = dkn + dot_tn(ds, qh) * MEM_SCALE
                dv = dv + dot_tn(bf(p), doh)
            dqr, dqg = _rms64_bwd(dqn, uq, rq, qg, m0)
            dkr, dkg = _rms64_bwd(dkn, uk, rk, kg, m0)
            dq_ref[:, lo : lo + LANES] = dqr
            dkv_ref[:, lo : lo + LANES] += dkr
            dkv_ref[:, 2 * LANES + lo : 3 * LANES + lo] += dv
            dqg_ref[...] += dqg
            dkg_ref[...] += dkg

    gspec = pl.BlockSpec((None, 1, LANES), lambda i: (l, 0, 0))
    return pl.pallas_call(
        body,
        name="memattn_bwd",
        grid=(s // tm,),
        in_specs=[
            pl.BlockSpec((tm, 2 * LANES), lambda i: (i, qblk)),
            _const((nm, 4 * LANES)),
            gspec,
            gspec,
            pl.BlockSpec((tm, 2 * LANES), lambda i: (i, 3)),
        ],
        out_specs=[pl.BlockSpec((tm, 2 * LANES), lambda i: (i, 0)), _const((nm, 4 * LANES)), _const((1, LANES)), _const((1, LANES))],
        out_shape=[
            jax.ShapeDtypeStruct((s, 2 * LANES), F32),
            jax.ShapeDtypeStruct((nm, 4 * LANES), F32),
            jax.ShapeDtypeStruct((1, LANES), F32),
            jax.ShapeDtypeStruct((1, LANES), F32),
        ],
        compiler_params=_cparams(("arbitrary",)),
    )(proj, kvm, qg3, kg3, dmixed)


KV_MAIN = 768


def _log_sigmoid(z):
    return jnp.minimum(z, 0.0) - jnp.log(1.0 + jnp.exp(-jnp.abs(z)))


def kvprep_fwd(kvf, kg, fb):
    s = kvf.shape[0]
    tm = _tok(s)

    def body(kvf_ref, kg_ref, fb_ref, k_ref, v_ref, clf_ref, carry):
        m0 = _lane_mask0((1, LANES))

        @pl.when(pl.program_id(0) == 0)
        def _():
            carry[...] = jnp.zeros_like(carry)

        for j in range(KV_MAIN // LANES):
            u, _ = _rms64(kvf_ref[:, j * LANES : (j + 1) * LANES], m0)
            k_ref[:, j * LANES : (j + 1) * LANES] = bf(u * kg_ref[...])
        v_ref[...] = bf(kvf_ref[:, KV_MAIN : 2 * KV_MAIN])
        lf = _log_sigmoid(kvf_ref[:, 2 * KV_MAIN :] + fb_ref[...])
        clf_ref[...] = dot_nn(_tri(tm), lf, HI) + carry[...]
        carry[...] += jnp.sum(lf, axis=0, keepdims=True)

    n = kvf.shape[1]
    return pl.pallas_call(
        body,
        name="kvprep_fwd",
        grid=(s // tm,),
        in_specs=[pl.BlockSpec((tm, n), lambda i: (i, 0)), _const((1, LANES)), _const((1, LANES))],
        out_specs=[pl.BlockSpec((tm, KV_MAIN), lambda i: (i, 0))] * 2 + [pl.BlockSpec((tm, LANES), lambda i: (i, 0))],
        out_shape=[jax.ShapeDtypeStruct((s, KV_MAIN), BF16)] * 2 + [jax.ShapeDtypeStruct((s, LANES), F32)],
        scratch_shapes=[pltpu.VMEM((1, LANES), F32)],
        compiler_params=_cparams(("arbitrary",)),
    )(kvf, kg, fb)


def kvprep_bwd(kvf, kg, fb, dk, dv, dclf):
    s, n = kvf.shape
    tm = _tok(s)
    nb = s // tm

    def body(kvf_ref, kg_ref, fb_ref, dk_ref, dv_ref, dclf_ref, o_ref, dkg_ref, dfb_ref, carry):
        m0 = _lane_mask0((1, LANES))

        @pl.when(pl.program_id(0) == 0)
        def _():
            carry[...] = jnp.zeros_like(carry)
            dkg_ref[...] = jnp.zeros_like(dkg_ref)
            dfb_ref[...] = jnp.zeros_like(dfb_ref)

        kg_ = kg_ref[...]
        for j in range(KV_MAIN // LANES):
            cols = slice(j * LANES, (j + 1) * LANES)
            u, r = _rms64(kvf_ref[:, cols], m0)
            dkr, dkg = _rms64_bwd(dk_ref[:, cols], u, r, kg_, m0)
            o_ref[:, cols] = dkr
            dkg_ref[...] += dkg
        o_ref[:, KV_MAIN : 2 * KV_MAIN] = dv_ref[...]
        z = kvf_ref[:, 2 * KV_MAIN :] + fb_ref[...]
        dc = dclf_ref[...]
        dlf = dot_nn(_tri(tm, upper=True), dc, HI) + carry[...]
        carry[...] += jnp.sum(dc, axis=0, keepdims=True)
        dz = dlf * _sigmoid(-z)
        o_ref[:, 2 * KV_MAIN :] = dz
        dfb_ref[...] += jnp.sum(dz, axis=0, keepdims=True)

    rev = lambda i: (nb - 1 - i, 0)
    return pl.pallas_call(
        body,
        name="kvprep_bwd",
        grid=(nb,),
        in_specs=[pl.BlockSpec((tm, n), rev), _const((1, LANES)), _const((1, LANES)), pl.BlockSpec((tm, KV_MAIN), rev),
                  pl.BlockSpec((tm, KV_MAIN), rev), pl.BlockSpec((tm, LANES), rev)],
        out_specs=[pl.BlockSpec((tm, n), rev), _const((1, LANES)), _const((1, LANES))],
        out_shape=[jax.ShapeDtypeStruct((s, n), F32), jax.ShapeDtypeStruct((1, LANES), F32), jax.ShapeDtypeStruct((1, LANES), F32)],
        scratch_shapes=[pltpu.VMEM((1, LANES), F32)],
        compiler_params=_cparams(("arbitrary",)),
    )(kvf, kg, fb, dk, dv, dclf)


FOX_SCALE = HEAD64**-0.5


def _lane_col(block, lane_idx, h):
    return jnp.sum(jnp.where(lane_idx == h, block, 0.0), axis=-1, keepdims=True)


def _causal(tq, ext, i, transposed=False):
    if transposed:
        key = lax.broadcasted_iota(jnp.int32, (ext, tq), 0)
        qry = lax.broadcasted_iota(jnp.int32, (ext, tq), 1) + i * tq
    else:
        qry = lax.broadcasted_iota(jnp.int32, (tq, ext), 0) + i * tq
        key = lax.broadcasted_iota(jnp.int32, (tq, ext), 1)
    return key <= qry


def fox_fwd(proj, k_sh, v_sh, clf, clf_t, qg3, j_layer):
    s = proj.shape[0]
    npair = 6
    tq = TQ if s % TQ == 0 else s
    nq = s // tq

    def body(q_ref, gate_ref, k_ref, v_ref, clf_ref, clft_ref, qg_ref, main_ref, o_ref, lse_ref):
        j = pl.program_id(0)
        lane = lax.broadcasted_iota(jnp.int32, (1, LANES), 1)
        m0 = lane < HEAD64
        u, _ = _rms64(q_ref[...], m0)
        qn = u * qg_ref[...] * FOX_SCALE
        clfv = clf_ref[...]
        for hh in range(2):
            h = 2 * j + hh
            mh = m0 if hh == 0 else jnp.logical_not(m0)
            qh = bf(jnp.where(mh, qn, 0.0))
            dcol = _lane_col(clfv, lane, h)
            drow = clft_ref[pl.ds(h, 1), :]
            for i in range(nq):
                rows = slice(i * tq, (i + 1) * tq)
                ext = (i + 1) * tq
                sc = dot_nt(qh[rows], k_ref[0:ext, :]) + dcol[rows] - drow[:, :ext]
                sc = jnp.where(_causal(tq, ext, i), sc, -jnp.inf)
                m = jnp.max(sc, axis=-1, keepdims=True)
                p = jnp.exp(sc - m)
                lsum = jnp.sum(p, axis=-1, keepdims=True)
                pv = dot_nn(bf(p), v_ref[0:ext, :]) / lsum
                lse = m + jnp.log(lsum)
                if hh == 0:
                    o_ref[rows, :] = pv
                    lse_ref[rows, :] = jnp.where(lane == 0, lse, 0.0)
                else:
                    o_ref[rows, :] = jnp.where(mh, pv, o_ref[rows, :])
                    lse_ref[rows, :] = jnp.where(lane == 1, lse, lse_ref[rows, :])
        main_ref[...] = bf(o_ref[...] * _sigmoid(gate_ref[...]))

    blk = lambda off: pl.BlockSpec((s, LANES), lambda j: (0, off + j))
    return pl.pallas_call(
        body,
        name="fox_fwd",
        grid=(npair,),
        in_specs=[blk(0), blk(npair), blk(0), blk(0), _const((s, LANES)), _const((16, s)),
                  pl.BlockSpec((None, 1, LANES), lambda j: (j_layer, 0, 0))],
        out_specs=[blk(0)] * 3,
        out_shape=[jax.ShapeDtypeStruct((s, npair * LANES), BF16)] + [jax.ShapeDtypeStruct((s, npair * LANES), F32)] * 2,
        compiler_params=_cparams(("arbitrary",)),
    )(proj, proj, k_sh, v_sh, clf, clf_t, qg3)


def fox_bwd(proj, k_sh, v_sh, clf, clf_t, qg3, j_layer, o, lse, lse_t, dmixed, dk_in, dv_in, dclf_in):
    s = proj.shape[0]
    npair = 6
    tq = TQ if s % TQ == 0 else s
    nq = s // tq

    def body(q_ref, gate_ref, k_ref, v_ref, clf_ref, clft_ref, qg_ref, o_ref, lse_ref, lset_ref, dm_ref, dkin_ref, dvin_ref, dclfin_ref,
             dq_ref, dgate_ref, dk_ref, dv_ref, dclf_ref, dqg_ref, dqn_s, dcl_s):
        j = pl.program_id(0)
        lane = lax.broadcasted_iota(jnp.int32, (1, LANES), 1)
        m0 = lane < HEAD64
        qg = qg_ref[...]
        u, r = _rms64(q_ref[...], m0)
        qn = u * qg * FOX_SCALE
        ov = o_ref[...]
        gate = gate_ref[...]
        sg = _sigmoid(gate)
        dmain = dm_ref[...]
        do = dmain * sg
        dgate_ref[...] = dmain * ov * sg * (1.0 - sg)
        dk_ref[...] = dkin_ref[...]
        dv_ref[...] = dvin_ref[...]
        clfv = clf_ref[...]
        lsev = lse_ref[...]
        ones8 = jnp.ones((8, LANES), F32)

        @pl.when(j == 0)
        def _():
            dclf_ref[...] = dclfin_ref[...]
            dqg_ref[...] = jnp.zeros_like(dqg_ref)

        for hh in range(2):
            h = 2 * j + hh
            mh = m0 if hh == 0 else jnp.logical_not(m0)
            qh = bf(jnp.where(mh, qn, 0.0))
            doh = jnp.where(mh, do, 0.0)
            dohb = bf(doh)
            doo = doh * ov
            dcol = _lane_col(clfv, lane, h)
            drow = clft_ref[pl.ds(h, 1), :]
            lcol = _lane_col(lsev, lane, hh)
            lrow = lset_ref[pl.ds(h, 1), :]
            delta = jnp.sum(doo, axis=-1, keepdims=True)
            dcl_s[...] = jnp.zeros_like(dcl_s)
            for i in range(nq):
                rows = slice(i * tq, (i + 1) * tq)
                ext = (i + 1) * tq
                kk, vv = k_ref[0:ext, :], v_ref[0:ext, :]
                sc = dot_nt(qh[rows], kk) + dcol[rows] - drow[:, :ext]
                p = jnp.where(_causal(tq, ext, i), jnp.exp(sc - lcol[rows]), 0.0)
                ds = p * (dot_nt(dohb[rows], vv) - delta[rows])
                dqh = dot_nn(bf(ds), kk) * FOX_SCALE
                if hh == 0:
                    dqn_s[rows, :] = dqh
                else:
                    dqn_s[rows, :] = jnp.where(mh, dqh, dqn_s[rows, :])
                dcl_s[rows, :] += jnp.sum(ds, axis=-1, keepdims=True)
                sct = dot_nt(kk, qh[rows]) + drow[:, rows] - dcol[:ext]
                pt = jnp.where(_causal(tq, ext, i, transposed=True), jnp.exp(sct - lrow[:, rows]), 0.0)
                delta_row = dot_nt(ones8, doo[rows], HI)[0:1]
                dst = pt * (dot_nt(vv, dohb[rows]) - delta_row)
                dv_ref[0:ext, :] += dot_nn(bf(pt), dohb[rows])
                dk_ref[0:ext, :] += dot_nn(bf(dst), qh[rows])
                dcl_s[0:ext, :] -= jnp.sum(dst, axis=-1, keepdims=True)
            dclf_ref[...] += jnp.where(lane == h, dcl_s[...], 0.0)
        dqr, dqg = _rms64_bwd(dqn_s[...], u, r, qg, m0)
        dq_ref[...] = dqr
        dqg_ref[...] += dqg

    blk = lambda off: pl.BlockSpec((s, LANES), lambda j: (0, off + j))
    full = _const((s, LANES))
    return pl.pallas_call(
        body,
        name="fox_bwd",
        grid=(npair,),
        in_specs=[blk(0), blk(npair), blk(0), blk(0), full, _const((16, s)), pl.BlockSpec((None, 1, LANES), lambda j: (j_layer, 0, 0)),
                  blk(0), blk(0), _const((16, s)), blk(0), blk(0), blk(0), full],
        out_specs=[blk(0)] * 4 + [full, _const((1, LANES))],
        out_shape=[jax.ShapeDtypeStruct((s, npair * LANES), F32)] * 4
        + [jax.ShapeDtypeStruct((s, LANES), F32), jax.ShapeDtypeStruct((1, LANES), F32)],
        scratch_shapes=[pltpu.VMEM((s, LANES), F32), pltpu.VMEM((s, LANES), F32)],
        compiler_params=_cparams(("arbitrary",)),
    )(proj, proj, k_sh, v_sh, clf, clf_t, qg3, o, lse, lse_t, dmixed, dk_in, dv_in, dclf_in)


def loss_head(y, target):
    s, d = y.shape
    tm = _tok(s)

    def body(y_ref, t_ref, loss_ref, dy_ref):
        err = y_ref[...] - t_ref[...]
        dy_ref[...] = err * (1.0 / d)

        @pl.when(pl.program_id(0) == 0)
        def _():
            loss_ref[...] = jnp.zeros_like(loss_ref)

        part = jnp.sum(jnp.mean(err * err, axis=-1, keepdims=True), axis=0, keepdims=True)
        loss_ref[...] += 0.5 * part

    row = pl.BlockSpec((tm, d), lambda i: (i, 0))
    return pl.pallas_call(
        body,
        name="loss_head",
        grid=(s // tm,),
        in_specs=[row, row],
        out_specs=[_const((1, 1)), row],
        out_shape=[jax.ShapeDtypeStruct((1, 1), F32), jax.ShapeDtypeStruct((s, d), F32)],
        compiler_params=_cparams(("arbitrary",)),
    )(y, target)


def _row_tile(r, c, n_arrays):
    budget = VMEM_LIMIT_BYTES // 2
    padded_c = -(-c // LANES) * LANES
    best = None
    for t in range(8, r + 1, 8):
        if r % t == 0 and 2 * n_arrays * t * padded_c * 4 <= budget:
            best = t
    return r if best is None else best


def _as2d(a):
    return a.reshape(-1, a.shape[-1]) if a.ndim >= 2 else a.reshape(1, -1)


def adamw(w, gs, m, v):
    shape = w.shape
    ng = len(gs)
    w2 = _as2d(w)
    c = w2.shape[1]
    r = w2.shape[0] // ng
    w3, m3, v3 = (_as2d(t).reshape(ng, r, c) for t in (w, m, v))
    tr = _row_tile(r, c, 8)
    c1 = 1.0 - ADAM_B1**ADAM_STEP
    c2 = 1.0 - ADAM_B2**ADAM_STEP
    outs = None
    for k, g in enumerate(gs):
        n_prev = 0 if outs is None else 4

        def body(w_ref, g_ref, m_ref, v_ref, *rest, n_prev=n_prev):
            go_ref, d_ref, nm_ref, nv_ref = rest[n_prev:]
            gv = g_ref[...]
            nm = ADAM_B1 * m_ref[...] + (1.0 - ADAM_B1) * gv
            nv = ADAM_B2 * v_ref[...] + (1.0 - ADAM_B2) * (gv * gv)
            go_ref[...] = gv
            nm_ref[...] = nm
            nv_ref[...] = nv
            d_ref[...] = -ADAM_LR * ((nm / c1) / (jnp.sqrt(nv / c2) + ADAM_EPS) + ADAM_WD * w_ref[...])

        spec = pl.BlockSpec((None, tr, c), lambda i, k=k: (k, i, 0))
        outs = pl.pallas_call(
            body,
            name="adamw",
            grid=(r // tr,),
            in_specs=[spec, pl.BlockSpec((tr, c), lambda i: (i, 0)), spec, spec] + [ANY] * n_prev,
            out_specs=[spec] * 4,
            out_shape=[jax.ShapeDtypeStruct((ng, r, c), F32)] * 4,
            input_output_aliases={4 + i: i for i in range(n_prev)},
            compiler_params=_cparams(("arbitrary",)),
        )(w3, g.reshape(r, c), m3, v3, *([] if outs is None else outs))
    return tuple(t.reshape(shape) for t in outs)


def pair_sum(g, recv, c_arr):
    _, k, r, c = g.shape
    tr = _row_tile(r, c, 3)

    def body(c_ref, g_ref, r_ref, o_ref):
        o_ref[...] = bf(g_ref[...] + r_ref[...])

    return pl.pallas_call(
        body,
        name="pair_sum",
        grid_spec=pltpu.PrefetchScalarGridSpec(
            num_scalar_prefetch=1,
            grid=(k, r // tr),
            in_specs=[pl.BlockSpec((None, None, tr, c), lambda kk, i, cr: (cr[0], kk, i, 0)), pl.BlockSpec((None, tr, c), lambda kk, i, cr: (kk, i, 0))],
            out_specs=pl.BlockSpec((None, tr, c), lambda kk, i, cr: (kk, i, 0)),
        ),
        out_shape=jax.ShapeDtypeStruct((k, r, c), BF16),
        compiler_params=_cparams(("arbitrary", "arbitrary")),
    )(c_arr, g, recv)


def chip_sum(p, q, sel):
    _, r, c = p.shape
    tr = _row_tile(r, c, 4)

    def body(sel_ref, p_ref, q_ref, o_ref):
        acc = p_ref[...].astype(F32)
        for i in range(q.shape[0]):
            acc = acc + q_ref[i].astype(F32)
        o_ref[...] = acc

    return pl.pallas_call(
        body,
        name="chip_sum",
        grid_spec=pltpu.PrefetchScalarGridSpec(
            num_scalar_prefetch=1,
            grid=(r // tr,),
            in_specs=[pl.BlockSpec((None, tr, c), lambda i, sr: (sr[0], i, 0)), pl.BlockSpec((q.shape[0], tr, c), lambda i, sr: (0, i, 0))],
            out_specs=pl.BlockSpec((None, tr, c), lambda i, sr: (sr[1], i, 0)),
        ),
        out_shape=jax.ShapeDtypeStruct((2, r, c), F32),
        compiler_params=_cparams(("arbitrary",)),
    )(sel, p, q)


def cast_into_slot(w4, g, sel, dtype):
    _, _, r, c = w4.shape
    tr = _row_tile(r, c, 2)

    def body(sel_ref, w_ref, o_ref):
        o_ref[...] = w_ref[...].astype(dtype)

    return pl.pallas_call(
        body,
        name="cast_into_slot",
        grid_spec=pltpu.PrefetchScalarGridSpec(
            num_scalar_prefetch=1,
            grid=(2, r // tr),
            in_specs=[pl.BlockSpec((None, None, tr, c), lambda hf, i, sr: (g, hf, i, 0))],
            out_specs=pl.BlockSpec((None, None, tr, c), lambda hf, i, sr: (sr[0], hf, i, 0)),
        ),
        out_shape=jax.ShapeDtypeStruct((N_CHIPS, 2, r, c), dtype),
        compiler_params=_cparams(("arbitrary", "arbitrary")),
    )(sel, w4)


def _place():
    x, y, c = lax.axis_index("x"), lax.axis_index("y"), lax.axis_index("c")
    chips = [(1 - x, y), (x, 1 - y), (1 - x, 1 - y)]
    return x, y, c, 2 * x + y, chips, [2 * cx + cy for cx, cy in chips]


def _rcopy(src, dst, send, recv, dev):
    return pltpu.make_async_remote_copy(src_ref=src, dst_ref=dst, send_sem=send, recv_sem=recv, device_id=dev, device_id_type=MESH)


class Gather:
    def __init__(self, bufs):
        n = len(bufs)
        self.n = n
        self.args = list(bufs)
        self.out_shape = [jax.ShapeDtypeStruct(t.shape, t.dtype) for t in bufs]
        self.aliases = {a: a for a in range(n)}
        self.scratch = [pltpu.SemaphoreType.DMA((n, 6)), pltpu.SemaphoreType.DMA((n, 6))]

    def _sends(self, outs, send, recv):
        x, y, c, me, chips, _ = _place()
        cps = []
        for a in range(self.n):
            mine = outs[a].at[me, c]
            cps += [_rcopy(mine, mine, send.at[a, j], recv.at[a, j], (*chips[j], c)) for j in range(3)]
        return cps

    def start(self, ins, outs, scr):
        for cp in self._sends(outs, *scr):
            cp.start()

    def finish(self, ins, outs, scr):
        send, recv = scr
        x, y, c, me, chips, cidx = _place()
        sib = (x, y, 1 - c)
        passed = []
        for a in range(self.n):
            for j in range(3):
                landed = outs[a].at[cidx[j], c]
                _rcopy(landed, landed, send.at[a, j], recv.at[a, j], (*chips[j], c)).wait_recv()
                fwd = _rcopy(landed, landed, send.at[a, 3 + j], recv.at[a, 3 + j], sib)
                fwd.start()
                passed.append(fwd)
        for a in range(self.n):
            for j in range(3):
                theirs = outs[a].at[cidx[j], 1 - c]
                _rcopy(theirs, theirs, send.at[a, 3 + j], recv.at[a, 3 + j], sib).wait_recv()
        for cp in self._sends(outs, send, recv) + passed:
            cp.wait_send()


class PairExchange:
    def __init__(self, gs):
        n = len(gs)
        self.n = n
        self.args = list(gs)
        self.out_shape = [jax.ShapeDtypeStruct(t.shape[1:], t.dtype) for t in gs]
        self.aliases = {}
        self.scratch = [pltpu.SemaphoreType.DMA((n,)), pltpu.SemaphoreType.DMA((n,))]

    def _copies(self, ins, outs, send, recv):
        x, y, c = lax.axis_index("x"), lax.axis_index("y"), lax.axis_index("c")
        return [_rcopy(ins[a].at[1 - c], outs[a], send.at[a], recv.at[a], (x, y, 1 - c)) for a in range(self.n)]

    def start(self, ins, outs, scr):
        for cp in self._copies(ins, outs, *scr):
            cp.start()

    def finish(self, ins, outs, scr):
        for cp in self._copies(ins, outs, *scr):
            cp.wait()


class ChipExchange:
    def __init__(self, ps):
        n = len(ps)
        self.n = n
        self.args = list(ps)
        self.out_shape = [jax.ShapeDtypeStruct((3,) + t.shape[1:], t.dtype) for t in ps]
        self.aliases = {}
        self.scratch = [pltpu.SemaphoreType.DMA((n, 3)), pltpu.SemaphoreType.DMA((n, 3))]

    def _sends(self, ins, outs, send, recv):
        x, y, c, me, chips, cidx = _place()
        return [
            _rcopy(ins[a].at[cidx[j]], outs[a].at[j], send.at[a, j], recv.at[a, j], (*chips[j], c))
            for a in range(self.n)
            for j in range(3)
        ]

    def start(self, ins, outs, scr):
        for cp in self._sends(ins, outs, *scr):
            cp.start()

    def finish(self, ins, outs, scr):
        send, recv = scr
        x, y, c, me, chips, _ = _place()
        for a in range(self.n):
            for j in range(3):
                landed = outs[a].at[j]
                _rcopy(landed, landed, send.at[a, j], recv.at[a, j], (*chips[j], c)).wait_recv()
        for cp in self._sends(ins, outs, send, recv):
            cp.wait_send()


class PairShare:
    def __init__(self, bufs):
        n = len(bufs)
        self.n = n
        self.args = list(bufs)
        self.out_shape = [jax.ShapeDtypeStruct(t.shape, t.dtype) for t in bufs]
        self.aliases = {a: a for a in range(n)}
        self.scratch = [pltpu.SemaphoreType.DMA((n,)), pltpu.SemaphoreType.DMA((n,))]

    def _sends(self, outs, send, recv):
        x, y, c = lax.axis_index("x"), lax.axis_index("y"), lax.axis_index("c")
        return [_rcopy(outs[a].at[c], outs[a].at[c], send.at[a], recv.at[a], (x, y, 1 - c)) for a in range(self.n)]

    def start(self, ins, outs, scr):
        for cp in self._sends(outs, *scr):
            cp.start()

    def finish(self, ins, outs, scr):
        send, recv = scr
        x, y, c = lax.axis_index("x"), lax.axis_index("y"), lax.axis_index("c")
        for a in range(self.n):
            theirs = outs[a].at[1 - c]
            _rcopy(theirs, theirs, send.at[a], recv.at[a], (x, y, 1 - c)).wait_recv()
        for cp in self._sends(outs, send, recv):
            cp.wait_send()


def run_comm(comm, name):
    na, no = len(comm.args), len(comm.out_shape)

    def body(*refs):
        ins, outs, scr = refs[:na], refs[na : na + no], refs[na + no :]
        comm.start(ins, outs, scr)
        comm.finish(ins, outs, scr)

    return pl.pallas_call(
        body,
        name=name,
        in_specs=[ANY] * na,
        out_specs=[ANY] * no,
        out_shape=comm.out_shape,
        input_output_aliases=comm.aliases,
        scratch_shapes=comm.scratch,
    )(*comm.args)


def _carry(body, comm, *, name, grid, in_specs, out_specs, out_shape, scratch_shapes, args):
    params = _cparams(("arbitrary",))
    if comm is None:
        res = pl.pallas_call(body, name=name, grid=grid, in_specs=in_specs, out_specs=out_specs, out_shape=out_shape,
                             scratch_shapes=scratch_shapes, compiler_params=params)(*args)
        return res, None
    ni, no, ns = len(in_specs), len(out_specs), len(scratch_shapes)
    ci, co = len(comm.args), len(comm.out_shape)

    def wrapped(*refs):
        ins, c_ins = refs[:ni], refs[ni : ni + ci]
        p = ni + ci
        outs, c_outs = refs[p : p + no], refs[p + no : p + no + co]
        p += no + co
        scr, c_scr = refs[p : p + ns], refs[p + ns :]

        @pl.when(pl.program_id(0) == 0)
        def _():
            comm.start(c_ins, c_outs, c_scr)

        body(*ins, *outs, *scr)

        @pl.when(pl.program_id(0) == grid[0] - 1)
        def _():
            comm.finish(c_ins, c_outs, c_scr)

    res = pl.pallas_call(
        wrapped,
        name=name + "_carry",
        grid=grid,
        in_specs=list(in_specs) + [ANY] * ci,
        out_specs=list(out_specs) + [ANY] * co,
        out_shape=list(out_shape) + list(comm.out_shape),
        input_output_aliases={ni + i: no + o for i, o in comm.aliases.items()},
        scratch_shapes=list(scratch_shapes) + list(comm.scratch),
        compiler_params=params,
    )(*args, *comm.args)
    return res[:no], res[no:]


def small_allreduce(buf):
    r = buf.shape[0]

    def body(b_ref, o_ref, slots, send, recv):
        x, y, c = lax.axis_index("x"), lax.axis_index("y"), lax.axis_index("c")
        me = 4 * x + 2 * y + c
        slots[me] = b_ref[...]
        cps = []
        peers = []
        for mask in range(1, N_DEV):
            fx, fy, fc = (mask >> 2) & 1, (mask >> 1) & 1, mask & 1
            px, py, pc = (1 - x if fx else x), (1 - y if fy else y), (1 - c if fc else c)
            peers.append(4 * px + 2 * py + pc)
            cps.append(_rcopy(b_ref, slots.at[me], send.at[mask - 1], recv.at[mask - 1], (px, py, pc)))
        for cp in cps:
            cp.start()
        for k, pid in enumerate(peers):
            landed = slots.at[pid]
            _rcopy(landed, landed, send.at[k], recv.at[k], (x, y, c)).wait_recv()
        for cp in cps:
            cp.wait_send()
        acc = slots[0]
        for i in range(1, N_DEV):
            acc = acc + slots[i]
        o_ref[...] = acc

    vm = pl.BlockSpec(memory_space=pltpu.VMEM)
    return pl.pallas_call(
        body,
        name="small_allreduce",
        in_specs=[vm],
        out_specs=vm,
        out_shape=jax.ShapeDtypeStruct(buf.shape, F32),
        scratch_shapes=[pltpu.VMEM((N_DEV, r, LANES), F32), pltpu.SemaphoreType.DMA((N_DEV - 1,)), pltpu.SemaphoreType.DMA((N_DEV - 1,))],
    )(buf)


WEIGHT_NAMES = ["ffn1_norm", "ffn1_w_gate", "ffn1_w_up", "ffn1_w_down", "mix_norm", "mem_norm", "w_mem_kv", "mem_q_gain",
                "mem_k_gain", "w_in_a", "hgrn_lb_logits", "hgrn_o_gain", "w_in_b", "fox_q_gain", "kv_norm", "w_kv", "fox_f_bias",
                "fox_k_gain", "w_out", "ffn2_norm", "ffn2_w_gate", "ffn2_w_up", "ffn2_w_down"]
SHARDED = ["ffn1_w_gate", "ffn1_w_up", "ffn1_w_down", "w_mem_kv", "w_in_a", "w_in_b", "w_kv", "w_out", "ffn2_w_gate", "ffn2_w_up", "ffn2_w_down"]
SMALL = [n for n in WEIGHT_NAMES if n not in SHARDED]
FFN1 = ["ffn1_w_gate", "ffn1_w_up", "ffn1_w_down"]
FFN2 = ["ffn2_w_gate", "ffn2_w_up", "ffn2_w_down"]
PER_LAYER = FFN1 + FFN2 + ["w_mem_kv", "w_out"]
TRANSPOSED = ["ffn1_w_gate", "ffn1_w_up", "ffn2_w_gate", "ffn2_w_up", "w_in_a", "w_in_b"]
N_LAYERS, N_A = 4, 2
KV_PAD = 13 * LANES


def _halves(t):
    return t.reshape((2, t.shape[0] // 2) + t.shape[1:])


def _cols_from_chips(g):
    return jnp.moveaxis(g, 0, 2).reshape(g.shape[1], g.shape[2], N_CHIPS * g.shape[3])


def _rows_from_chips(g):
    return jnp.moveaxis(g, 0, 1).reshape(g.shape[1], N_CHIPS * g.shape[2], g.shape[3])


def _pair_tile(g):
    return jnp.tile(g, (1, 2)).reshape(g.shape[0], 1, LANES)


def _pair_fold(g):
    return g[:, :HEAD64] + g[:, HEAD64:]


def kernel(x, mem, ffn1_norm, ffn1_w_gate, ffn1_w_up, ffn1_w_down, mix_norm, mem_norm, w_mem_kv, mem_q_gain, mem_k_gain, w_in_a, hgrn_lb_logits, hgrn_o_gain, w_in_b, fox_q_gain, kv_norm, w_kv, fox_f_bias, fox_k_gain, w_out, ffn2_norm, ffn2_w_gate, ffn2_w_up, ffn2_w_down, loss_target, m_ffn1_norm, m_ffn1_w_gate, m_ffn1_w_up, m_ffn1_w_down, m_mix_norm, m_mem_norm, m_w_mem_kv, m_mem_q_gain, m_mem_k_gain, m_w_in_a, m_hgrn_lb_logits, m_hgrn_o_gain, m_w_in_b, m_fox_q_gain, m_kv_norm, m_w_kv, m_fox_f_bias, m_fox_k_gain, m_w_out, m_ffn2_norm, m_ffn2_w_gate, m_ffn2_w_up, m_ffn2_w_down, v_ffn1_norm, v_ffn1_w_gate, v_ffn1_w_up, v_ffn1_w_down, v_mix_norm, v_mem_norm, v_w_mem_kv, v_mem_q_gain, v_mem_k_gain, v_w_in_a, v_hgrn_lb_logits, v_hgrn_o_gain, v_w_in_b, v_fox_q_gain, v_kv_norm, v_w_kv, v_fox_f_bias, v_fox_k_gain, v_w_out, v_ffn2_norm, v_ffn2_w_gate, v_ffn2_w_up, v_ffn2_w_down):
    given = dict(locals())
    def oriented(n, t):
        return jnp.swapaxes(t, 1, 2) if n in TRANSPOSED else t

    w = {n: oriented(n, given[n]) for n in WEIGHT_NAMES}
    xs, mems, tgt = x[0], mem[0], loss_target[0]
    s, d = xs.shape
    my_chip = 2 * lax.axis_index("x") + lax.axis_index("y")
    sel = jnp.stack([my_chip, lax.axis_index("c")]).astype(jnp.int32)
    c_arr = sel[1:]

    def cast(n, g=0, ng=1):
        t = w[n]
        if t.ndim == 2:
            t4 = t.reshape(1, 2, t.shape[0] // 2, t.shape[1])
        else:
            t4 = t.reshape(ng, 2, (t.shape[0] // (2 * ng)) * t.shape[1], t.shape[2])
        return cast_into_slot(t4, g, sel, BF16)

    def view(buf, n):
        return buf.reshape((N_CHIPS, 2) + w[n].shape[1:]) if w[n].ndim == 3 else buf.reshape((N_CHIPS,) + w[n].shape)

    names0 = PER_LAYER + ["w_in_a", "w_kv"]
    bufs0 = [cast(n, 0, 2) for n in PER_LAYER] + [cast("w_in_a"), cast("w_kv")]
    bufs0.append(cast_into_slot(hgrn_lb_logits.reshape(1, 2, 1, -1), 0, sel, F32))
    got0 = run_comm(Gather(bufs0), "gather_group0")
    gw = [{n: view(b, n) for n, b in zip(names0, got0[:-1])}, None]
    buf1 = {n: cast(n, 1, 2) for n in PER_LAYER}
    buf1["w_in_b"] = cast("w_in_b")
    got1 = {}
    w_in = {"a": _rows_from_chips(gw[0]["w_in_a"])}
    w_kv_full = _cols_from_chips(gw[0]["w_kv"][:, None])
    w_kv_full = jnp.pad(w_kv_full, ((0, 0), (0, 0), (0, KV_PAD - w_kv_full.shape[-1])))
    w_mkv = [_rows_from_chips(gw[0]["w_mem_kv"]), None]
    w_o = [_rows_from_chips(gw[0]["w_out"]), None]
    logits3 = jnp.moveaxis(got0[-1].reshape(N_CHIPS, 2, -1), 0, 1).reshape(2, 1, -1)
    lb3 = lb_fwd(logits3)
    carried = {(0, "mix"): FFN1, (0, "ffn2"): ["w_in_b"], (1, "ffn1"): ["w_out", "w_mem_kv"], (1, "mix"): FFN2}

    def gather1(key):
        names = carried.get(key)
        return None if names is None else Gather([buf1[n] for n in names])

    def landed1(key, res):
        if res is not None:
            got1.update({n: view(b, n) for n, b in zip(carried[key], res)})

    norm3 = {n: w[n].reshape(N_LAYERS, 1, d) for n in ("ffn1_norm", "mix_norm", "mem_norm", "ffn2_norm")}
    kvn3 = kv_norm.reshape(1, 1, d)
    mqg3, mkg3 = _pair_tile(mem_q_gain), _pair_tile(mem_k_gain)
    og3 = hgrn_o_gain.reshape(N_A, 1, LANES)
    fqg3 = _pair_tile(fox_q_gain)
    fkg = jnp.tile(fox_k_gain, 2).reshape(1, LANES)
    fb = jnp.pad(fox_f_bias, (0, LANES - fox_f_bias.shape[0])).reshape(1, LANES)

    sv = [dict() for _ in range(N_LAYERS)]
    h = xs
    kv = None
    for l in range(N_LAYERS):
        t = sv[l]
        gi, li = l // 2, l % 2
        if l == N_A:
            gw[1] = got1
            w_in["b"] = _rows_from_chips(got1["w_in_b"])
            w_mkv[1], w_o[1] = _rows_from_chips(got1["w_mem_kv"]), _rows_from_chips(got1["w_out"])
        t["x0"] = h
        (h, t["a1"], t["b1"]), res = ffn_fwd(h, norm3["ffn1_norm"], l, *[gw[gi][n] for n in FFN1], li, comm=gather1((l, "ffn1")))
        landed1((l, "ffn1"), res)
        t["x1"] = h
        if l < N_A:
            t["proj"] = proj_fwd(h, norm3["mix_norm"], l, w_in["a"], l, wt=True)
            (main, t["o"]), res = hgrn_fwd(t["proj"], lb3, og3, l, comm=gather1((l, "mix")))
            landed1((l, "mix"), res)
            t["qblk"] = 12
        else:
            t["proj"] = proj_fwd(h, norm3["mix_norm"], l, w_in["b"], l - N_A, wt=True)
            main, t["o"], t["lse"] = fox_fwd(t["proj"], kv["k"], kv["v"], kv["clf"], kv["clf_t"], fqg3, l - N_A)
            t["qblk"] = 6
        t["kvm"] = proj_fwd(mems, norm3["mem_norm"], l, w_mkv[gi], li)
        memo = memattn_fwd(t["proj"], t["qblk"], t["kvm"], mqg3, mkg3, l)
        t["mixed"] = jnp.concatenate([main, memo], axis=-1)
        h = mm_res(h, t["mixed"], w_o[gi], li)
        t["x2"] = h
        (h, t["a2"], t["b2"]), res = ffn_fwd(h, norm3["ffn2_norm"], l, *[gw[gi][n] for n in FFN2], li, comm=gather1((l, "ffn2")))
        landed1((l, "ffn2"), res)
        if l == N_A - 1:
            kv = {"x": h, "kvf": proj_fwd(h, kvn3, 0, w_kv_full, 0)}
            kv["k"], kv["v"], kv["clf"] = kvprep_fwd(kv["kvf"], fkg, fb)
            kv["clf_t"] = kv["clf"][:, :16].T

    loss_local, dx = loss_head(h, tgt)

    nc = N_CHIPS
    fc = ffn1_w_down.shape[1]
    gbuf = [{n: lax.empty((2, nc, 1, fc, d), F32) for n in FFN1 + FFN2} for _ in range(2)]

    def row_layout(stack):
        lw, rr, cc = stack.shape
        return stack.reshape(lw, nc, rr // nc, cc)

    def group_layout(gi):
        lay = {n: b.reshape(2, nc, -1, b.shape[-1]) for n, b in gbuf[gi].items()}
        lay["w_mem_kv"] = row_layout(jnp.stack(dw_mkv[2 * gi : 2 * gi + 2]))
        lay["w_out"] = row_layout(jnp.stack(dw_o[2 * gi : 2 * gi + 2]))
        if gi == 0:
            lay["w_in_a"] = row_layout(jnp.stack(dw_in["a"]))
            kv_cols = w_kv.shape[-1] * nc
            lay["w_kv"] = jnp.transpose(dw_kv[:, :kv_cols].reshape(2, d // 2, nc, kv_cols // nc), (0, 2, 1, 3))
            names = PER_LAYER + ["w_in_a", "w_kv"]
        else:
            lay["w_in_b"] = row_layout(jnp.stack(dw_in["b"]))
            names = PER_LAYER + ["w_in_b"]
        return names, [lay[n] for n in names]

    n_ffn = len(FFN1) + len(FFN2)
    names1 = gl1 = partial1 = landed1a = mine1 = both1 = None
    dw_in = {"a": [None] * N_A, "b": [None] * (N_LAYERS - N_A)}
    dw_o, dw_mkv = [None] * N_LAYERS, [None] * N_LAYERS
    sg = {n: [None] * N_LAYERS for n in ("ffn1_norm", "mix_norm", "mem_norm", "ffn2_norm", "mem_q_gain", "mem_k_gain")}
    sg["hgrn_o_gain"], sg["fox_q_gain"], dlb = [None] * N_A, [None] * (N_LAYERS - N_A), [None] * N_A
    dk_sh = jnp.zeros((s, KV_MAIN), F32)
    dv_sh = jnp.zeros((s, KV_MAIN), F32)
    dclf = jnp.zeros((s, LANES), F32)
    zero_mem = jnp.zeros(mems.shape, F32)
    dw_kv = None
    for l in reversed(range(N_LAYERS)):
        t = sv[l]
        gi, li = l // 2, l % 2
        if l == N_A - 1:
            dkvf, dfkg, dfb = kvprep_bwd(kv["kvf"], fkg, fb, dk_sh, dv_sh, dclf)
            dx, sg["kv_norm"], xn_kv, dpb = proj_bwd(kv["x"], kvn3, 0, [dkvf], w_kv_full, 0, dx)
            dw_kv = wgrad(xn_kv, dpb)
            names1, gl1 = group_layout(1)
        comm = PairExchange(gl1) if l == 1 else (PairShare(mine1) if l == 0 else None)
        (dx, da, db, hm, xn, dyb, sg["ffn2_norm"][l]), res = ffn_bwd(t["x2"], norm3["ffn2_norm"], l, dx, t["a2"], t["b2"], *[gw[gi][n] for n in FFN2], li, comm=comm)
        if l == 1:
            partial1 = [pair_sum(g, r, c_arr) for g, r in zip(gl1, res)]
        if l == 0:
            both1 = res
        gbuf[gi]["ffn2_w_gate"] = wgrad(da, xn, buf=gbuf[gi]["ffn2_w_gate"], l=li)
        gbuf[gi]["ffn2_w_up"] = wgrad(db, xn, buf=gbuf[gi]["ffn2_w_up"], l=li)
        gbuf[gi]["ffn2_w_down"] = wgrad(hm, dyb, buf=gbuf[gi]["ffn2_w_down"], l=li)
        dmixed, dxb = mm_nt(dx, w_o[gi], li)
        dw_o[l] = wgrad(t["mixed"], dxb)
        dqm, dkvm, dmq, dmk = memattn_bwd(t["proj"], t["qblk"], t["kvm"], mqg3, mkg3, l, dmixed)
        sg["mem_q_gain"][l], sg["mem_k_gain"][l] = _pair_fold(dmq), _pair_fold(dmk)
        _, sg["mem_norm"][l], memn, dkvmb = proj_bwd(mems, norm3["mem_norm"], l, [dkvm], w_mkv[gi], li, zero_mem)
        dw_mkv[l] = wgrad(memn, dkvmb)
        if l < N_A:
            comm = ChipExchange(partial1[:n_ffn]) if l == 1 else None
            (dzq, dzf, dvi, dzg, dlb[l], sg["hgrn_o_gain"][l]), res = hgrn_bwd(t["proj"], lb3, og3, l, t["o"], dmixed, comm=comm)
            if l == 1:
                landed1a = list(res)
            parts, key, wl, tmw = [dzq, dzf, dvi, dzg, dqm], "a", l, 13 * LANES
        else:
            lse_t = t["lse"].reshape(s, 6, LANES)[:, :, :2].reshape(s, 12).T
            lse_t = jnp.pad(lse_t, ((0, 4), (0, 0)))
            dq, dgate, dk_sh, dv_sh, dclf, dfq = fox_bwd(t["proj"], kv["k"], kv["v"], kv["clf"], kv["clf_t"], fqg3, l - N_A, t["o"], t["lse"], lse_t, dmixed, dk_sh, dv_sh, dclf)
            sg["fox_q_gain"][l - N_A] = _pair_fold(dfq)
            parts, key, wl, tmw = [dq, dgate, dqm], "b", l - N_A, 7 * LANES
        dx, sg["mix_norm"][l], hn, dpb = proj_bwd(t["x1"], norm3["mix_norm"], l, parts, w_in[key], wl, dx, wt=True)
        dw_in[key][wl] = wgrad(dpb, hn, tm=tmw)
        comm = ChipExchange(partial1[n_ffn:]) if l == 1 else None
        (dx, da, db, hm, xn, dyb, sg["ffn1_norm"][l]), res = ffn_bwd(t["x0"], norm3["ffn1_norm"], l, dx, t["a1"], t["b1"], *[gw[gi][n] for n in FFN1], li, comm=comm)
        if l == 1:
            mine1 = [chip_sum(p, q, sel) for p, q in zip(partial1, landed1a + list(res))]
        gbuf[gi]["ffn1_w_gate"] = wgrad(da, xn, buf=gbuf[gi]["ffn1_w_gate"], l=li)
        gbuf[gi]["ffn1_w_up"] = wgrad(db, xn, buf=gbuf[gi]["ffn1_w_up"], l=li)
        gbuf[gi]["ffn1_w_down"] = wgrad(hm, dyb, buf=gbuf[gi]["ffn1_w_down"], l=li)

    names0, gl0 = group_layout(0)
    recv0 = run_comm(PairExchange(gl0), "pair_exchange")
    partial0 = [pair_sum(g, r, c_arr) for g, r in zip(gl0, recv0)]
    landed0 = run_comm(ChipExchange(partial0), "chip_exchange")
    mine0 = [chip_sum(p, q, sel) for p, q in zip(partial0, landed0)]
    both0 = run_comm(PairShare(mine0), "pair_share")
    reduced = [dict(zip(names0, both0)), dict(zip(names1, both1))]
    gparts = {n: [reduced[0][n], reduced[1][n]] for n in PER_LAYER}
    gparts.update({"w_in_a": [reduced[0]["w_in_a"]], "w_kv": [reduced[0]["w_kv"]], "w_in_b": [reduced[1]["w_in_b"]]})

    dlogits = lb_bwd(logits3, dlb[1]).reshape(2, -1)
    small = {
        "ffn1_norm": jnp.concatenate(sg["ffn1_norm"]), "mix_norm": jnp.concatenate(sg["mix_norm"]),
        "mem_norm": jnp.concatenate(sg["mem_norm"]), "ffn2_norm": jnp.concatenate(sg["ffn2_norm"]),
        "mem_q_gain": jnp.concatenate(sg["mem_q_gain"]), "mem_k_gain": jnp.concatenate(sg["mem_k_gain"]),
        "hgrn_o_gain": jnp.concatenate(sg["hgrn_o_gain"]), "fox_q_gain": jnp.concatenate(sg["fox_q_gain"]),
        "kv_norm": sg["kv_norm"], "fox_f_bias": dfb[:, : fox_f_bias.shape[0]], "fox_k_gain": _pair_fold(dfkg),
        "hgrn_lb_logits": dlogits,
    }
    flat = [small[n].reshape(-1) for n in SMALL] + [loss_local.reshape(-1)]
    sizes = [f.shape[0] for f in flat]
    total = sum(sizes)
    padded = -(-total // (8 * LANES)) * (8 * LANES)
    packed = jnp.pad(jnp.concatenate(flat), (0, padded - total)).reshape(-1, LANES)
    summed = small_allreduce(packed).reshape(-1)
    off = 0
    for n, sz in zip(SMALL, sizes[:-1]):
        gparts[n] = [summed[off : off + sz].reshape(dlogits.shape if n == "hgrn_lb_logits" else w[n].shape)]
        off += sz
    loss = summed[off]
    lbw = hgrn_lb_logits.shape[1]
    gparts["hgrn_lb_logits"] = [lax.dynamic_slice_in_dim(gparts["hgrn_lb_logits"][0], my_chip * lbw, lbw, axis=1)]

    grads, delta, new_m, new_v = {}, {}, {}, {}
    for n in WEIGHT_NAMES:
        res = adamw(w[n], gparts[n], oriented(n, given["m_" + n]), oriented(n, given["v_" + n]))
        grads[n], delta[n], new_m[n], new_v[n] = (oriented(n, t) for t in res)
    return (loss, dx[None], *[grads[n] for n in WEIGHT_NAMES], *[delta[n] for n in WEIGHT_NAMES],
            *[new_m[n] for n in WEIGHT_NAMES], *[new_v[n] for n in WEIGHT_NAMES])
```

```python
import functools

import jax
import jax.numpy as jnp
from jax import lax
from jax.experimental import pallas as pl
from jax.experimental.pallas import tpu as pltpu

F32, BF16 = jnp.float32, jnp.bfloat16
HI = lax.Precision.HIGHEST
EPS = 1e-6
MESH = pl.DeviceIdType.MESH
ANY = pl.BlockSpec(memory_space=pl.ANY)

VMEM_LIMIT_BYTES = 56 << 20
N_CHIPS = 4
N_DEV = 8
LANES = 128
HEAD64 = 64
CHUNK = 64
SUB = 16
HGRN_HEADS_PER_STEP = 2
TQ = 256
TOK = 256

ADAM_LR, ADAM_B1, ADAM_B2, ADAM_EPS, ADAM_WD, ADAM_STEP = 0.001, 0.9, 0.999, 1e-08, 0.01, 10


def _cparams(sem=None):
    return pltpu.CompilerParams(dimension_semantics=sem, vmem_limit_bytes=VMEM_LIMIT_BYTES)


def _mm(a, b, dims, prec=None):
    return lax.dot_general(a, b, (dims, ((), ())), preferred_element_type=F32, precision=prec)


def dot_nn(a, b, prec=None):
    return _mm(a, b, ((1,), (0,)), prec)


def dot_nt(a, b, prec=None):
    return _mm(a, b, ((1,), (1,)), prec)


def dot_tn(a, b, prec=None):
    return _mm(a, b, ((0,), (0,)), prec)


def bf(v):
    return v.astype(BF16)


def _sigmoid(z):
    return jax.nn.sigmoid(z)


def _dsilu(z, s):
    return s * (1.0 + z * (1.0 - s))


def _rms(x):
    r = lax.rsqrt(jnp.mean(x * x, axis=-1, keepdims=True) + EPS)
    return x * r, r


def _rms_bwd(dxn, u, r, g):
    du = dxn * g
    dx = r * (du - u * jnp.mean(du * u, axis=-1, keepdims=True))
    return dx, jnp.sum(dxn * u, axis=0, keepdims=True)


def _lane_mask0(shape):
    return lax.broadcasted_iota(jnp.int32, shape, len(shape) - 1) < HEAD64


def _rms64(x, m0):
    sq = x * x
    s0 = jnp.sum(jnp.where(m0, sq, 0.0), axis=-1, keepdims=True)
    s1 = jnp.sum(jnp.where(m0, 0.0, sq), axis=-1, keepdims=True)
    r = lax.rsqrt(jnp.where(m0, s0, s1) * (1.0 / HEAD64) + EPS)
    return x * r, r


def _rms64_bwd(dxn, u, r, g, m0):
    du = dxn * g
    t = du * u
    t0 = jnp.sum(jnp.where(m0, t, 0.0), axis=-1, keepdims=True)
    t1 = jnp.sum(jnp.where(m0, 0.0, t), axis=-1, keepdims=True)
    dx = r * (du - u * (jnp.where(m0, t0, t1) * (1.0 / HEAD64)))
    return dx, jnp.sum(dxn * u, axis=0, keepdims=True)


def _tok(s):
    return TOK if s % TOK == 0 else s


def _const(shape):
    return pl.BlockSpec(shape, lambda *_: (0,) * len(shape))


def ffn_fwd(x, gain3, l, wg, wu, wd, wl, comm=None):
    s, d = x.shape
    nc, _, fc, _ = wg.shape
    tm = _tok(s)

    def body(x_ref, g_ref, wg_ref, wu_ref, wd_ref, xo_ref, a_ref, b_ref):
        xv = x_ref[...]
        u, _ = _rms(xv)
        xn = bf(u * g_ref[...])
        y = jnp.zeros((tm, d), F32)
        for c in range(nc):
            a = dot_nt(xn, wg_ref[c])
            b = dot_nt(xn, wu_ref[c])
            a_ref[c] = bf(a)
            b_ref[c] = bf(b)
            y = y + dot_nn(bf(a * _sigmoid(a) * b), wd_ref[c])
        xo_ref[...] = xv + 0.5 * y

    wspec = pl.BlockSpec((nc, None, fc, d), lambda i: (0, wl, 0, 0), pipeline_mode=pl.Buffered(1))
    wdspec = pl.BlockSpec((nc, None, fc, d), lambda i: (0, wl, 0, 0), pipeline_mode=pl.Buffered(1))
    row = pl.BlockSpec((tm, d), lambda i: (i, 0))
    act = pl.BlockSpec((nc, tm, fc), lambda i: (0, i, 0))
    return _carry(
        body,
        comm,
        name="ffn_fwd",
        grid=(s // tm,),
        in_specs=[row, pl.BlockSpec((None, 1, d), lambda i: (l, 0, 0)), wspec, wspec, wdspec],
        out_specs=[row, act, act],
        out_shape=[
            jax.ShapeDtypeStruct((s, d), F32),
            jax.ShapeDtypeStruct((nc, s, fc), BF16),
            jax.ShapeDtypeStruct((nc, s, fc), BF16),
        ],
        scratch_shapes=[],
        args=(x, gain3, wg, wu, wd),
    )


def ffn_bwd(x, gain3, l, dout, a, b, wg, wu, wd, wl, comm=None):
    s, d = x.shape
    nc, _, fc, _ = wg.shape
    tm = _tok(s)

    def body(x_ref, g_ref, do_ref, a_ref, b_ref, wg_ref, wu_ref, wd_ref, dx_ref, da_ref, db_ref, hm_ref, xn_ref, dy_ref, dg_ref):
        xv = x_ref[...]
        g = g_ref[...]
        u, r = _rms(xv)
        xn_ref[...] = bf(u * g)
        dout = do_ref[...]
        dy = bf(0.5 * dout)
        dy_ref[...] = dy
        dxn = jnp.zeros((tm, d), F32)
        for c in range(nc):
            av = a_ref[c].astype(F32)
            bv = b_ref[c].astype(F32)
            sg = _sigmoid(av)
            sl = av * sg
            dh = dot_nt(dy, wd_ref[c])
            da = bf(dh * bv * _dsilu(av, sg))
            db = bf(dh * sl)
            da_ref[c] = da
            db_ref[c] = db
            hm_ref[c] = bf(sl * bv)
            dxn = dxn + dot_nn(da, wg_ref[c]) + dot_nn(db, wu_ref[c])
        dx, dg = _rms_bwd(dxn, u, r, g)
        dx_ref[...] = dout + dx

        @pl.when(pl.program_id(0) == 0)
        def _():
            dg_ref[...] = jnp.zeros_like(dg_ref)

        dg_ref[...] += dg

    wspec = pl.BlockSpec((nc, None, fc, d), lambda i: (0, wl, 0, 0), pipeline_mode=pl.Buffered(1))
    wdspec = pl.BlockSpec((nc, None, fc, d), lambda i: (0, wl, 0, 0), pipeline_mode=pl.Buffered(1))
    row = pl.BlockSpec((tm, d), lambda i: (i, 0))
    act = pl.BlockSpec((nc, tm, fc), lambda i: (0, i, 0))
    act_shape = jax.ShapeDtypeStruct((nc, s, fc), BF16)
    return _carry(
        body,
        comm,
        name="ffn_bwd",
        grid=(s // tm,),
        in_specs=[row, pl.BlockSpec((None, 1, d), lambda i: (l, 0, 0)), row, act, act, wspec, wspec, wdspec],
        out_specs=[row, act, act, act, row, row, _const((1, d))],
        out_shape=[
            jax.ShapeDtypeStruct((s, d), F32),
            act_shape,
            act_shape,
            act_shape,
            jax.ShapeDtypeStruct((s, d), BF16),
            jax.ShapeDtypeStruct((s, d), BF16),
            jax.ShapeDtypeStruct((1, d), F32),
        ],
        scratch_shapes=[],
        args=(x, gain3, dout, a, b, wg, wu, wd),
    )


def wgrad(a, b, tn=None, buf=None, l=None, tm=None, split=False):
    ca = a.shape[0] if a.ndim == 3 else 1
    cb = b.shape[0] if b.ndim == 3 else 1
    nc = max(ca, cb)
    s, m = a.shape[-2:]
    n = b.shape[-1]
    tn = n if tn is None else tn
    assert n % tn == 0

    def body(*refs):
        a_ref, b_ref, o_ref = refs[0], refs[1], refs[-1]
        res = dot_tn(a_ref[...], b_ref[...])
        if split:
            o_ref[0] = res[: m // 2]
            o_ref[1] = res[m // 2 :]
        else:
            o_ref[...] = res

    if buf is None and not split:
        assert nc == 1 and a.ndim == 2 and b.ndim == 2
        tm = m if tm is None else tm
        assert m % tm == 0
        return pl.pallas_call(
            body,
            name="wgrad",
            grid=(m // tm, n // tn),
            in_specs=[pl.BlockSpec((s, tm), lambda i, j: (0, i)), pl.BlockSpec((s, tn), lambda i, j: (0, j))],
            out_specs=pl.BlockSpec((tm, tn), lambda i, j: (i, j)),
            out_shape=jax.ShapeDtypeStruct((m, n), F32),
            compiler_params=_cparams(("arbitrary", "arbitrary")),
        )(a, b)
    a_spec = pl.BlockSpec((None, s, m), lambda c, j: (c, 0, 0)) if a.ndim == 3 else pl.BlockSpec((s, m), lambda c, j: (0, 0))
    b_spec = pl.BlockSpec((None, s, tn), lambda c, j: (c, 0, j)) if b.ndim == 3 else pl.BlockSpec((s, tn), lambda c, j: (0, j))
    if split:
        return pl.pallas_call(
            body,
            name="wgrad_split",
            grid=(nc, n // tn),
            in_specs=[a_spec, b_spec],
            out_specs=pl.BlockSpec((2, None, None, m // 2, tn), lambda c, j: (0, c, 0, 0, j)),
            out_shape=jax.ShapeDtypeStruct((2, nc, 1, m // 2, n), F32),
            compiler_params=_cparams(("arbitrary", "arbitrary")),
        )(a, b)
    lh = buf.shape[2]
    hi, lo = l // lh, l % lh
    o_spec = pl.BlockSpec((None, None, None, m, tn), lambda c, j: (hi, c, lo, 0, j))
    return pl.pallas_call(
        body,
        name="wgrad_buf",
        grid=(nc, n // tn),
        in_specs=[a_spec, b_spec, ANY],
        out_specs=o_spec,
        out_shape=jax.ShapeDtypeStruct(buf.shape, F32),
        input_output_aliases={2: 0},
        compiler_params=_cparams(("arbitrary", "arbitrary")),
    )(a, b, buf)


def proj_fwd(x, gain3, l, w, wl, wt=False):
    s, d = x.shape
    n = w.shape[1] if wt else w.shape[2]
    tm = _tok(s)

    def body(x_ref, g_ref, w_ref, o_ref):
        u, _ = _rms(x_ref[...])
        xn = bf(u * g_ref[...])
        o_ref[...] = dot_nt(xn, w_ref[...]) if wt else dot_nn(xn, w_ref[...])

    return pl.pallas_call(
        body,
        name="proj_fwd",
        grid=(s // tm,),
        in_specs=[
            pl.BlockSpec((tm, d), lambda i: (i, 0)),
            pl.BlockSpec((None, 1, d), lambda i: (l, 0, 0)),
            pl.BlockSpec((None,) + w.shape[1:], lambda i: (wl, 0, 0)),
        ],
        out_specs=pl.BlockSpec((tm, n), lambda i: (i, 0)),
        out_shape=jax.ShapeDtypeStruct((s, n), F32),
        compiler_params=_cparams(("arbitrary",)),
    )(x, gain3, w)


def proj_bwd(x, gain3, l, parts, w, wl, dx_in, wt=False):
    s, d = x.shape
    n = w.shape[1] if wt else w.shape[2]
    widths = [p.shape[1] for p in parts]
    assert sum(widths) == n
    tm = _tok(s)
    npart = len(parts)

    def body(*refs):
        x_ref, g_ref, w_ref, dxin_ref = refs[:4]
        p_refs = refs[4 : 4 + npart]
        dx_ref, dg_ref, xn_ref, dpb_ref = refs[4 + npart :]
        g = g_ref[...]
        u, r = _rms(x_ref[...])
        xn_ref[...] = bf(u * g)
        dxn = jnp.zeros((tm, d), F32)
        off = 0
        for p_ref, wd_ in zip(p_refs, widths):
            dp = bf(p_ref[...])
            dpb_ref[:, off : off + wd_] = dp
            dxn = dxn + (dot_nn(dp, w_ref[off : off + wd_, :]) if wt else dot_nt(dp, w_ref[:, off : off + wd_]))
            off += wd_
        dx, dg = _rms_bwd(dxn, u, r, g)
        dx_ref[...] = dxin_ref[...] + dx

        @pl.when(pl.program_id(0) == 0)
        def _():
            dg_ref[...] = jnp.zeros_like(dg_ref)

        dg_ref[...] += dg

    row = pl.BlockSpec((tm, d), lambda i: (i, 0))
    return pl.pallas_call(
        body,
        name="proj_bwd",
        grid=(s // tm,),
        in_specs=[row, pl.BlockSpec((None, 1, d), lambda i: (l, 0, 0)), pl.BlockSpec((None,) + w.shape[1:], lambda i: (wl, 0, 0)), row]
        + [pl.BlockSpec((tm, wd_), lambda i: (i, 0)) for wd_ in widths],
        out_specs=[row, _const((1, d)), row, pl.BlockSpec((tm, n), lambda i: (i, 0))],
        out_shape=[
            jax.ShapeDtypeStruct((s, d), F32),
            jax.ShapeDtypeStruct((1, d), F32),
            jax.ShapeDtypeStruct((s, d), BF16),
            jax.ShapeDtypeStruct((s, n), BF16),
        ],
        compiler_params=_cparams(("arbitrary",)),
    )(x, gain3, w, dx_in, *parts)


def mm_res(x, a, w, l):
    s, d = x.shape
    k = a.shape[1]
    tm = _tok(s)

    def body(x_ref, a_ref, w_ref, o_ref):
        o_ref[...] = x_ref[...] + dot_nn(a_ref[...], w_ref[...])

    return pl.pallas_call(
        body,
        name="mm_res",
        grid=(s // tm,),
        in_specs=[
            pl.BlockSpec((tm, d), lambda i: (i, 0)),
            pl.BlockSpec((tm, k), lambda i: (i, 0)),
            pl.BlockSpec((None, k, d), lambda i: (l, 0, 0)),
        ],
        out_specs=pl.BlockSpec((tm, d), lambda i: (i, 0)),
        out_shape=jax.ShapeDtypeStruct((s, d), F32),
        compiler_params=_cparams(("arbitrary",)),
    )(x, a, w)


def mm_nt(dx, w, l):
    s, d = dx.shape
    k = w.shape[1]
    tm = _tok(s)

    def body(dx_ref, w_ref, o_ref, dxb_ref):
        dxb = bf(dx_ref[...])
        dxb_ref[...] = dxb
        o_ref[...] = dot_nt(dxb, w_ref[...])

    return pl.pallas_call(
        body,
        name="mm_nt",
        grid=(s // tm,),
        in_specs=[pl.BlockSpec((tm, d), lambda i: (i, 0)), pl.BlockSpec((None, k, d), lambda i: (l, 0, 0))],
        out_specs=[pl.BlockSpec((tm, k), lambda i: (i, 0)), pl.BlockSpec((tm, d), lambda i: (i, 0))],
        out_shape=[jax.ShapeDtypeStruct((s, k), F32), jax.ShapeDtypeStruct((s, d), BF16)],
        compiler_params=_cparams(("arbitrary",)),
    )(dx, w)


def lb_fwd(logits3):
    def body(l_ref, o_ref):
        l0, l1 = l_ref[0], l_ref[1]
        m = jnp.maximum(l0, l1)
        e0, e1 = jnp.exp(l0 - m), jnp.exp(l1 - m)
        p0, p1 = e0 / (e0 + e1), e1 / (e0 + e1)
        o_ref[0] = p0 - p0
        o_ref[1] = (p0 + p1) - p0

    return pl.pallas_call(body, name="lb_fwd", out_shape=jax.ShapeDtypeStruct(logits3.shape, F32))(logits3)


def lb_bwd(logits3, dlb1):
    def body(l_ref, d_ref, o_ref):
        l0, l1 = l_ref[0], l_ref[1]
        m = jnp.maximum(l0, l1)
        e0, e1 = jnp.exp(l0 - m), jnp.exp(l1 - m)
        p0, p1 = e0 / (e0 + e1), e1 / (e0 + e1)
        t = d_ref[...] * p0 * p1
        o_ref[0] = -t
        o_ref[1] = t

    return pl.pallas_call(body, name="lb_bwd", out_shape=jax.ShapeDtypeStruct(logits3.shape, F32))(logits3, dlb1)


def _hgrn_gates(zq, zf, lb):
    sf = _sigmoid(zf)
    f = lb + (1.0 - lb) * sf
    sq = _sigmoid(zq)
    return sf, f, jnp.log(f), 1.0 - f, sq, zq * sq


def _tri(n, upper=False):
    r = lax.broadcasted_iota(jnp.int32, (n, n), 0)
    c = lax.broadcasted_iota(jnp.int32, (n, n), 1)
    return jnp.where((c >= r) if upper else (r >= c), 1.0, 0.0).astype(F32)


def hgrn_fwd(proj, lb3, og3, l, comm=None):
    s = proj.shape[0]
    nh = 6
    n_chunk = s // CHUNK
    nsub = CHUNK // SUB

    hb = HGRN_HEADS_PER_STEP
    wide = hb * LANES

    def body(zq_ref, zf_ref, vi_ref, zg_ref, lb_ref, og_ref, main_ref, o_ref, q_a, k_a, v_a, c_a):
        og = og_ref[...]
        tril = _tri(CHUNK)
        rowi = lax.broadcasted_iota(jnp.int32, (SUB, LANES), 0)

        def one_head(hd, rows, st):
            cols = slice(hd * LANES, (hd + 1) * LANES)
            q_s, k_s, v_s, c_s = q_a.at[hd], k_a.at[hd], v_a.at[hd], c_a.at[hd]
            zg = zg_ref[rows, cols]
            _, _, lf, k, _, q = _hgrn_gates(zq_ref[rows, cols], zf_ref[rows, cols], lb_ref[:, cols])
            v = vi_ref[rows, cols]
            c = dot_nn(tril, lf, HI)
            q_s[...] = q
            k_s[...] = k
            v_s[...] = v
            c_s[...] = c
            o_inter = dot_nt(q * jnp.exp(c), st, HI)
            parts = []
            for i in range(nsub):
                lo = i * SUB
                blk = pl.ds(lo, SUB)
                qb, cb = q_s[blk, :], c_s[blk, :]
                ob = o_inter[lo : lo + SUB]
                if i > 0:
                    rr = c_s[pl.ds(lo - 1, 1), :]
                    qt = qb * jnp.exp(cb - rr)
                    kt = k_s[pl.ds(0, lo), :] * jnp.exp(rr - c_s[pl.ds(0, lo), :])
                    ob = ob + dot_nn(dot_nt(qt, kt, HI), v_s[pl.ds(0, lo), :], HI)
                for t in range(SUB):
                    e = jnp.where(rowi >= t, jnp.exp(cb - c_s[pl.ds(lo + t, 1), :]), 0.0)
                    a = jnp.sum(qb * k_s[pl.ds(lo + t, 1), :] * e, axis=-1, keepdims=True)
                    ob = ob + a * v_s[pl.ds(lo + t, 1), :]
                parts.append(ob)
            o = jnp.concatenate(parts, axis=0)
            ce = c_s[pl.ds(CHUNK - 1, 1), :]
            st = st * jnp.exp(ce) + dot_tn(v, k * jnp.exp(ce - c), HI)
            on, _ = _rms(o)
            o_ref[rows, cols] = o
            main_ref[rows, cols] = bf(on * og * (zg * _sigmoid(zg)))
            return st

        def chunk(ci, sts):
            rows = pl.ds(pl.multiple_of(ci * CHUNK, CHUNK), CHUNK)
            return tuple(one_head(hd, rows, sts[hd]) for hd in range(hb))

        lax.fori_loop(0, n_chunk, chunk, tuple(jnp.zeros((LANES, LANES), F32) for _ in range(hb)))

    def col(k):
        return pl.BlockSpec((s, wide), lambda h: (0, k * (nh // hb) + h))

    vec = pl.BlockSpec((None, 1, wide), lambda h: (l, 0, h))
    return _carry(
        body,
        comm,
        name="hgrn_fwd",
        grid=(nh // hb,),
        in_specs=[col(0), col(1), col(2), col(3), vec, pl.BlockSpec((None, 1, LANES), lambda h: (l, 0, 0))],
        out_specs=[pl.BlockSpec((s, wide), lambda h: (0, h))] * 2,
        out_shape=[jax.ShapeDtypeStruct((s, nh * LANES), BF16), jax.ShapeDtypeStruct((s, nh * LANES), F32)],
        scratch_shapes=[pltpu.VMEM((hb, CHUNK, LANES), F32)] * 4,
        args=(proj, proj, proj, proj, lb3, og3),
    )


def hgrn_bwd(proj, lb3, og3, l, o, dmixed, comm=None):
    s = proj.shape[0]
    nh = 6
    n_chunk = s // CHUNK
    nsub = CHUNK // SUB

    hb = HGRN_HEADS_PER_STEP
    wide = hb * LANES

    def body(zq_ref, zf_ref, vi_ref, zg_ref, lb_ref, og_ref, o_ref, dm_ref,
             dzq_ref, dzf_ref, dvi_ref, dzg_ref, dlb_ref, dog_ref,
             st_a, q_a, k_a, v_a, c_a, do_a, dq_a, dk_a, dv_a, acc_a):
        og = og_ref[...]
        tril = _tri(CHUNK)
        triu = _tri(CHUNK, upper=True)
        rowi = lax.broadcasted_iota(jnp.int32, (SUB, LANES), 0)

        def fwd_head(hd, ci, rows, st):
            cols = slice(hd * LANES, (hd + 1) * LANES)
            _, _, lf, k, _, _ = _hgrn_gates(zq_ref[rows, cols], zf_ref[rows, cols], lb_ref[:, cols])
            c = dot_nn(tril, lf, HI)
            ce = jnp.sum(lf, axis=0, keepdims=True)
            st_a[hd, ci] = st
            return st * jnp.exp(ce) + dot_tn(vi_ref[rows, cols], k * jnp.exp(ce - c), HI)

        def fwd_chunk(ci, sts):
            rows = pl.ds(pl.multiple_of(ci * CHUNK, CHUNK), CHUNK)
            return tuple(fwd_head(hd, ci, rows, sts[hd]) for hd in range(hb))

        lax.fori_loop(0, n_chunk, fwd_chunk, tuple(jnp.zeros((LANES, LANES), F32) for _ in range(hb)))
        acc_a[...] = jnp.zeros_like(acc_a)

        def bwd_head(hd, ci, rows, carry):
            dst, cg = carry
            cols = slice(hd * LANES, (hd + 1) * LANES)
            q_s, k_s, v_s, c_s, do_s = q_a.at[hd], k_a.at[hd], v_a.at[hd], c_a.at[hd], do_a.at[hd]
            dq_s, dk_s, dv_s, acc_s = dq_a.at[hd], dk_a.at[hd], dv_a.at[hd], acc_a.at[hd]
            lb = lb_ref[:, cols]
            zq, zf, zg = zq_ref[rows, cols], zf_ref[rows, cols], zg_ref[rows, cols]
            sf, f, lf, k, sq, q = _hgrn_gates(zq, zf, lb)
            v = vi_ref[rows, cols]
            c = dot_nn(tril, lf, HI)
            st = st_a[hd, ci]
            on, r = _rms(o_ref[rows, cols])
            sg = _sigmoid(zg)
            dmain = dm_ref[rows, cols]
            dy = dmain * (zg * sg)
            dzg_ref[rows, cols] = dmain * (on * og) * _dsilu(zg, sg)
            do, dog = _rms_bwd(dy, on, r, og)
            acc_s[pl.ds(0, 1), :] += dog
            q_s[...] = q
            k_s[...] = k
            v_s[...] = v
            c_s[...] = c
            do_s[...] = do
            ce = c_s[pl.ds(CHUNK - 1, 1), :]
            eq = jnp.exp(c)
            ek = jnp.exp(ce - c)
            qt_all = q * eq
            dq_s[...] = dot_nn(do, st, HI) * eq
            dv_s[...] = dot_nt(k * ek, dst, HI)
            dk_s[...] = dot_nn(v, dst, HI) * ek
            dst = dst * jnp.exp(ce) + dot_tn(do, qt_all, HI)
            for i in range(nsub):
                lo = i * SUB
                blk = pl.ds(lo, SUB)
                qb, cb, dob = q_s[blk, :], c_s[blk, :], do_s[blk, :]
                if i > 0:
                    prev = pl.ds(0, lo)
                    rr = c_s[pl.ds(lo - 1, 1), :]
                    eqi = jnp.exp(cb - rr)
                    eki = jnp.exp(rr - c_s[prev, :])
                    qt = qb * eqi
                    kt = k_s[prev, :] * eki
                    amat = dot_nt(qt, kt, HI)
                    damat = dot_nt(dob, v_s[prev, :], HI)
                    dv_s[prev, :] += dot_tn(amat, dob, HI)
                    dq_s[blk, :] += dot_nn(damat, kt, HI) * eqi
                    dk_s[prev, :] += dot_tn(damat, qt, HI) * eki
                dqb = jnp.zeros((SUB, LANES), F32)
                for t in range(SUB):
                    row = pl.ds(lo + t, 1)
                    e = jnp.where(rowi >= t, jnp.exp(cb - c_s[row, :]), 0.0)
                    kr = k_s[row, :]
                    a = jnp.sum(qb * kr * e, axis=-1, keepdims=True)
                    da = jnp.sum(dob * v_s[row, :], axis=-1, keepdims=True)
                    dv_s[row, :] += jnp.sum(a * dob, axis=0, keepdims=True)
                    dqb = dqb + da * kr * e
                    dk_s[row, :] += jnp.sum(da * qb * e, axis=0, keepdims=True)
                dq_s[blk, :] += dqb
            dq, dk = dq_s[...], dk_s[...]
            dg = q * dq - k * dk
            dlf = dot_nn(triu, dg, HI) + cg
            cg = cg + jnp.sum(dg, axis=0, keepdims=True)
            df = dlf / f - dk
            dzf_ref[rows, cols] = df * (1.0 - lb) * sf * (1.0 - sf)
            acc_s[pl.ds(1, 1), :] += jnp.sum(df * (1.0 - sf), axis=0, keepdims=True)
            dzq_ref[rows, cols] = dq * _dsilu(zq, sq)
            dvi_ref[rows, cols] = dv_s[...]
            return dst, cg

        def bwd_chunk(jj, carries):
            ci = n_chunk - 1 - jj
            rows = pl.ds(pl.multiple_of(ci * CHUNK, CHUNK), CHUNK)
            return tuple(bwd_head(hd, ci, rows, carries[hd]) for hd in range(hb))

        zero = (jnp.zeros((LANES, LANES), F32), jnp.zeros((1, LANES), F32))
        lax.fori_loop(0, n_chunk, bwd_chunk, tuple(zero for _ in range(hb)))

        @pl.when(pl.program_id(0) == 0)
        def _():
            dog_ref[...] = jnp.zeros_like(dog_ref)

        for hd in range(hb):
            dlb_ref[:, hd * LANES : (hd + 1) * LANES] = acc_a[hd, pl.ds(1, 1), :]
            dog_ref[...] += acc_a[hd, pl.ds(0, 1), :]

    def col(k):
        return pl.BlockSpec((s, wide), lambda h: (0, k * (nh // hb) + h), pipeline_mode=pl.Buffered(1))

    head_in = pl.BlockSpec((s, wide), lambda h: (0, h), pipeline_mode=pl.Buffered(1))
    head = pl.BlockSpec((s, wide), lambda h: (0, h))
    vec = pl.BlockSpec((None, 1, wide), lambda h: (l, 0, h))
    ck = pltpu.VMEM((hb, CHUNK, LANES), F32)
    return _carry(
        body,
        comm,
        name="hgrn_bwd",
        grid=(nh // hb,),
        in_specs=[col(0), col(1), col(2), col(3), vec, pl.BlockSpec((None, 1, LANES), lambda h: (l, 0, 0)), head_in, head_in],
        out_specs=[head] * 4 + [pl.BlockSpec((1, wide), lambda h: (0, h)), _const((1, LANES))],
        out_shape=[jax.ShapeDtypeStruct((s, nh * LANES), F32)] * 4
        + [jax.ShapeDtypeStruct((1, nh * LANES), F32), jax.ShapeDtypeStruct((1, LANES), F32)],
        scratch_shapes=[pltpu.VMEM((hb, n_chunk, LANES, LANES), F32)] + [ck] * 8 + [pltpu.VMEM((hb, 8, LANES), F32)],
        args=(proj, proj, proj, proj, lb3, og3, o, dmixed),
    )


MEM_SCALE = HEAD64**-0.5


def _mem_heads(qraw, kvm, qg, kg, pr, m0):
    lo = pr * LANES
    uq, rq = _rms64(qraw[:, lo : lo + LANES], m0)
    uk, rk = _rms64(kvm[:, lo : lo + LANES], m0)
    v = bf(kvm[:, 2 * LANES + lo : 3 * LANES + lo])
    return uq, rq, uk, rk, v, uq * qg, bf(uk * kg)


def memattn_fwd(proj, qblk, kvm, qg3, kg3, l):
    s = proj.shape[0]
    nm = kvm.shape[0]
    tm = _tok(s)

    def body(q_ref, kv_ref, qg_ref, kg_ref, o_ref):
        m0 = _lane_mask0((1, LANES))
        qraw, kvv = q_ref[...], kv_ref[...]
        for pr in range(2):
            _, _, _, _, v, qn, kn = _mem_heads(qraw, kvv, qg_ref[...], kg_ref[...], pr, m0)
            out = jnp.zeros((tm, LANES), F32)
            for hh in range(2):
                mh = m0 if hh == 0 else jnp.logical_not(m0)
                sc = dot_nt(bf(jnp.where(mh, qn, 0.0)), kn) * MEM_SCALE
                p = jnp.exp(sc - jnp.max(sc, axis=-1, keepdims=True))
                p = p / jnp.sum(p, axis=-1, keepdims=True)
                out = jnp.where(mh, dot_nn(bf(p), v), out)
            o_ref[:, pr * LANES : (pr + 1) * LANES] = bf(out)

    gspec = pl.BlockSpec((None, 1, LANES), lambda i: (l, 0, 0))
    return pl.pallas_call(
        body,
        name="memattn_fwd",
        grid=(s // tm,),
        in_specs=[pl.BlockSpec((tm, 2 * LANES), lambda i: (i, qblk)), _const((nm, 4 * LANES)), gspec, gspec],
        out_specs=pl.BlockSpec((tm, 2 * LANES), lambda i: (i, 0)),
        out_shape=jax.ShapeDtypeStruct((s, 2 * LANES), BF16),
        compiler_params=_cparams(("arbitrary",)),
    )(proj, kvm, qg3, kg3)


def memattn_bwd(proj, qblk, kvm, qg3, kg3, l, dmixed):
    s = proj.shape[0]
    nm = kvm.shape[0]
    tm = _tok(s)

    def body(q_ref, kv_ref, qg_ref, kg_ref, dm_ref, dq_ref, dkv_ref, dqg_ref, dkg_ref):
        m0 = _lane_mask0((1, LANES))
        qraw, kvv = q_ref[...], kv_ref[...]
        qg, kg = qg_ref[...], kg_ref[...]

        @pl.when(pl.program_id(0) == 0)
        def _():
            dkv_ref[...] = jnp.zeros_like(dkv_ref)
            dqg_ref[...] = jnp.zeros_like(dqg_ref)
            dkg_ref[...] = jnp.zeros_like(dkg_ref)

        for pr in range(2):
            lo = pr * LANES
            uq, rq, uk, rk, v, qn, kn = _mem_heads(qraw, kvv, qg, kg, pr, m0)
            do = dm_ref[:, lo : lo + LANES]
            dqn = jnp.zeros((tm, LANES), F32)
            dkn = jnp.zeros((nm, LANES), F32)
            dv = jnp.zeros((nm, LANES), F32)
            for hh in range(2):
                mh = m0 if hh == 0 else jnp.logical_not(m0)
                qh = bf(jnp.where(mh, qn, 0.0))
                doh = bf(jnp.where(mh, do, 0.0))
                sc = dot_nt(qh, kn) * MEM_SCALE
                p = jnp.exp(sc - jnp.max(sc, axis=-1, keepdims=True))
                p = p / jnp.sum(p, axis=-1, keepdims=True)
                dp = dot_nt(doh, v)
                ds = bf(p * (dp - jnp.sum(p * dp, axis=-1, keepdims=True)))
                dqn = dqn + jnp.where(mh, dot_nn(ds, kn), 0.0) * MEM_SCALE
                dkn = dkn + dot_tn(ds, qh) * MEM_SCALE
                dv = dv + dot_tn(bf(p), doh)
            dqr, dqg = _rms64_bwd(dqn, uq, rq, qg, m0)
            dkr, dkg = _rms64_bwd(dkn, uk, rk, kg, m0)
            dq_ref[:, lo : lo + LANES] = dqr
            dkv_ref[:, lo : lo + LANES] += dkr
            dkv_ref[:, 2 * LANES + lo : 3 * LANES + lo] += dv
            dqg_ref[...] += dqg
            dkg_ref[...] += dkg

    gspec = pl.BlockSpec((None, 1, LANES), lambda i: (l, 0, 0))
    return pl.pallas_call(
        body,
        name="memattn_bwd",
        grid=(s // tm,),
        in_specs=[
            pl.BlockSpec((tm, 2 * LANES), lambda i: (i, qblk)),
            _const((nm, 4 * LANES)),
            gspec,
            gspec,
            pl.BlockSpec((tm, 2 * LANES), lambda i: (i, 3)),
        ],
        out_specs=[pl.BlockSpec((tm, 2 * LANES), lambda i: (i, 0)), _const((nm, 4 * LANES)), _const((1, LANES)), _const((1, LANES))],
        out_shape=[
            jax.ShapeDtypeStruct((s, 2 * LANES), F32),
            jax.ShapeDtypeStruct((nm, 4 * LANES), F32),
            jax.ShapeDtypeStruct((1, LANES), F32),
            jax.ShapeDtypeStruct((1, LANES), F32),
        ],
        compiler_params=_cparams(("arbitrary",)),
    )(proj, kvm, qg3, kg3, dmixed)


KV_MAIN = 768


def _log_sigmoid(z):
    return jnp.minimum(z, 0.0) - jnp.log(1.0 + jnp.exp(-jnp.abs(z)))


def kvprep_fwd(kvf, kg, fb):
    s = kvf.shape[0]
    tm = _tok(s)

    def body(kvf_ref, kg_ref, fb_ref, k_ref, v_ref, clf_ref, carry):
        m0 = _lane_mask0((1, LANES))

        @pl.when(pl.program_id(0) == 0)
        def _():
            carry[...] = jnp.zeros_like(carry)

        for j in range(KV_MAIN // LANES):
            u, _ = _rms64(kvf_ref[:, j * LANES : (j + 1) * LANES], m0)
            k_ref[:, j * LANES : (j + 1) * LANES] = bf(u * kg_ref[...])
        v_ref[...] = bf(kvf_ref[:, KV_MAIN : 2 * KV_MAIN])
        lf = _log_sigmoid(kvf_ref[:, 2 * KV_MAIN :] + fb_ref[...])
        clf_ref[...] = dot_nn(_tri(tm), lf, HI) + carry[...]
        carry[...] += jnp.sum(lf, axis=0, keepdims=True)

    n = kvf.shape[1]
    return pl.pallas_call(
        body,
        name="kvprep_fwd",
        grid=(s // tm,),
        in_specs=[pl.BlockSpec((tm, n), lambda i: (i, 0)), _const((1, LANES)), _const((1, LANES))],
        out_specs=[pl.BlockSpec((tm, KV_MAIN), lambda i: (i, 0))] * 2 + [pl.BlockSpec((tm, LANES), lambda i: (i, 0))],
        out_shape=[jax.ShapeDtypeStruct((s, KV_MAIN), BF16)] * 2 + [jax.ShapeDtypeStruct((s, LANES), F32)],
        scratch_shapes=[pltpu.VMEM((1, LANES), F32)],
        compiler_params=_cparams(("arbitrary",)),
    )(kvf, kg, fb)


def kvprep_bwd(kvf, kg, fb, dk, dv, dclf):
    s, n = kvf.shape
    tm = _tok(s)
    nb = s // tm

    def body(kvf_ref, kg_ref, fb_ref, dk_ref, dv_ref, dclf_ref, o_ref, dkg_ref, dfb_ref, carry):
        m0 = _lane_mask0((1, LANES))

        @pl.when(pl.program_id(0) == 0)
        def _():
            carry[...] = jnp.zeros_like(carry)
            dkg_ref[...] = jnp.zeros_like(dkg_ref)
            dfb_ref[...] = jnp.zeros_like(dfb_ref)

        kg_ = kg_ref[...]
        for j in range(KV_MAIN // LANES):
            cols = slice(j * LANES, (j + 1) * LANES)
            u, r = _rms64(kvf_ref[:, cols], m0)
            dkr, dkg = _rms64_bwd(dk_ref[:, cols], u, r, kg_, m0)
            o_ref[:, cols] = dkr
            dkg_ref[...] += dkg
        o_ref[:, KV_MAIN : 2 * KV_MAIN] = dv_ref[...]
        z = kvf_ref[:, 2 * KV_MAIN :] + fb_ref[...]
        dc = dclf_ref[...]
        dlf = dot_nn(_tri(tm, upper=True), dc, HI) + carry[...]
        carry[...] += jnp.sum(dc, axis=0, keepdims=True)
        dz = dlf * _sigmoid(-z)
        o_ref[:, 2 * KV_MAIN :] = dz
        dfb_ref[...] += jnp.sum(dz, axis=0, keepdims=True)

    rev = lambda i: (nb - 1 - i, 0)
    return pl.pallas_call(
        body,
        name="kvprep_bwd",
        grid=(nb,),
        in_specs=[pl.BlockSpec((tm, n), rev), _const((1, LANES)), _const((1, LANES)), pl.BlockSpec((tm, KV_MAIN), rev),
                  pl.BlockSpec((tm, KV_MAIN), rev), pl.BlockSpec((tm, LANES), rev)],
        out_specs=[pl.BlockSpec((tm, n), rev), _const((1, LANES)), _const((1, LANES))],
        out_shape=[jax.ShapeDtypeStruct((s, n), F32), jax.ShapeDtypeStruct((1, LANES), F32), jax.ShapeDtypeStruct((1, LANES), F32)],
        scratch_shapes=[pltpu.VMEM((1, LANES), F32)],
        compiler_params=_cparams(("arbitrary",)),
    )(kvf, kg, fb, dk, dv, dclf)


FOX_SCALE = HEAD64**-0.5


def _lane_col(block, lane_idx, h):
    return jnp.sum(jnp.where(lane_idx == h, block, 0.0), axis=-1, keepdims=True)


def _causal(tq, ext, i, transposed=False):
    if transposed:
        key = lax.broadcasted_iota(jnp.int32, (ext, tq), 0)
        qry = lax.broadcasted_iota(jnp.int32, (ext, tq), 1) + i * tq
    else:
        qry = lax.broadcasted_iota(jnp.int32, (tq, ext), 0) + i * tq
        key = lax.broadcasted_iota(jnp.int32, (tq, ext), 1)
    return key <= qry


def fox_fwd(proj, k_sh, v_sh, clf, clf_t, qg3, j_layer):
    s = proj.shape[0]
    npair = 6
    tq = TQ if s % TQ == 0 else s
    nq = s // tq

    def body(q_ref, gate_ref, k_ref, v_ref, clf_ref, clft_ref, qg_ref, main_ref, o_ref, lse_ref):
        j = pl.program_id(0)
        lane = lax.broadcasted_iota(jnp.int32, (1, LANES), 1)
        m0 = lane < HEAD64
        u, _ = _rms64(q_ref[...], m0)
        qn = u * qg_ref[...] * FOX_SCALE
        clfv = clf_ref[...]
        for hh in range(2):
            h = 2 * j + hh
            mh = m0 if hh == 0 else jnp.logical_not(m0)
            qh = bf(jnp.where(mh, qn, 0.0))
            dcol = _lane_col(clfv, lane, h)
            drow = clft_ref[pl.ds(h, 1), :]
            for i in range(nq):
                rows = slice(i * tq, (i + 1) * tq)
                ext = (i + 1) * tq
                sc = dot_nt(qh[rows], k_ref[0:ext, :]) + dcol[rows] - drow[:, :ext]
                sc = jnp.where(_causal(tq, ext, i), sc, -jnp.inf)
                m = jnp.max(sc, axis=-1, keepdims=True)
                p = jnp.exp(sc - m)
                lsum = jnp.sum(p, axis=-1, keepdims=True)
                pv = dot_nn(bf(p), v_ref[0:ext, :]) / lsum
                lse = m + jnp.log(lsum)
                if hh == 0:
                    o_ref[rows, :] = pv
                    lse_ref[rows, :] = jnp.where(lane == 0, lse, 0.0)
                else:
                    o_ref[rows, :] = jnp.where(mh, pv, o_ref[rows, :])
                    lse_ref[rows, :] = jnp.where(lane == 1, lse, lse_ref[rows, :])
        main_ref[...] = bf(o_ref[...] * _sigmoid(gate_ref[...]))

    blk = lambda off: pl.BlockSpec((s, LANES), lambda j: (0, off + j))
    return pl.pallas_call(
        body,
        name="fox_fwd",
        grid=(npair,),
        in_specs=[blk(0), blk(npair), blk(0), blk(0), _const((s, LANES)), _const((16, s)),
                  pl.BlockSpec((None, 1, LANES), lambda j: (j_layer, 0, 0))],
        out_specs=[blk(0)] * 3,
        out_shape=[jax.ShapeDtypeStruct((s, npair * LANES), BF16)] + [jax.ShapeDtypeStruct((s, npair * LANES), F32)] * 2,
        compiler_params=_cparams(("arbitrary",)),
    )(proj, proj, k_sh, v_sh, clf, clf_t, qg3)


def fox_bwd(proj, k_sh, v_sh, clf, clf_t, qg3, j_layer, o, lse, lse_t, dmixed, dk_in, dv_in, dclf_in):
    s = proj.shape[0]
    npair = 6
    tq = TQ if s % TQ == 0 else s
    nq = s // tq

    def body(q_ref, gate_ref, k_ref, v_ref, clf_ref, clft_ref, qg_ref, o_ref, lse_ref, lset_ref, dm_ref, dkin_ref, dvin_ref, dclfin_ref,
             dq_ref, dgate_ref, dk_ref, dv_ref, dclf_ref, dqg_ref, dqn_s, dcl_s):
        j = pl.program_id(0)
        lane = lax.broadcasted_iota(jnp.int32, (1, LANES), 1)
        m0 = lane < HEAD64
        qg = qg_ref[...]
        u, r = _rms64(q_ref[...], m0)
        qn = u * qg * FOX_SCALE
        ov = o_ref[...]
        gate = gate_ref[...]
        sg = _sigmoid(gate)
        dmain = dm_ref[...]
        do = dmain * sg
        dgate_ref[...] = dmain * ov * sg * (1.0 - sg)
        dk_ref[...] = dkin_ref[...]
        dv_ref[...] = dvin_ref[...]
        clfv = clf_ref[...]
        lsev = lse_ref[...]
        ones8 = jnp.ones((8, LANES), F32)

        @pl.when(j == 0)
        def _():
            dclf_ref[...] = dclfin_ref[...]
            dqg_ref[...] = jnp.zeros_like(dqg_ref)

        for hh in range(2):
            h = 2 * j + hh
            mh = m0 if hh == 0 else jnp.logical_not(m0)
            qh = bf(jnp.where(mh, qn, 0.0))
            doh = jnp.where(mh, do, 0.0)
            dohb = bf(doh)
            doo = doh * ov
            dcol = _lane_col(clfv, lane, h)
            drow = clft_ref[pl.ds(h, 1), :]
            lcol = _lane_col(lsev, lane, hh)
            lrow = lset_ref[pl.ds(h, 1), :]
            delta = jnp.sum(doo, axis=-1, keepdims=True)
            dcl_s[...] = jnp.zeros_like(dcl_s)
            for i in range(nq):
                rows = slice(i * tq, (i + 1) * tq)
                ext = (i + 1) * tq
                kk, vv = k_ref[0:ext, :], v_ref[0:ext, :]
                sc = dot_nt(qh[rows], kk) + dcol[rows] - drow[:, :ext]
                p = jnp.where(_causal(tq, ext, i), jnp.exp(sc - lcol[rows]), 0.0)
                ds = p * (dot_nt(dohb[rows], vv) - delta[rows])
                dqh = dot_nn(bf(ds), kk) * FOX_SCALE
                if hh == 0:
                    dqn_s[rows, :] = dqh
                else:
                    dqn_s[rows, :] = jnp.where(mh, dqh, dqn_s[rows, :])
                dcl_s[rows, :] += jnp.sum(ds, axis=-1, keepdims=True)
                sct = dot_nt(kk, qh[rows]) + drow[:, rows] - dcol[:ext]
                pt = jnp.where(_causal(tq, ext, i, transposed=True), jnp.exp(sct - lrow[:, rows]), 0.0)
                delta_row = dot_nt(ones8, doo[rows], HI)[0:1]
                dst = pt * (dot_nt(vv, dohb[rows]) - delta_row)
                dv_ref[0:ext, :] += dot_nn(bf(pt), dohb[rows])
                dk_ref[0:ext, :] += dot_nn(bf(dst), qh[rows])
                dcl_s[0:ext, :] -= jnp.sum(dst, axis=-1, keepdims=True)
            dclf_ref[...] += jnp.where(lane == h, dcl_s[...], 0.0)
        dqr, dqg = _rms64_bwd(dqn_s[...], u, r, qg, m0)
        dq_ref[...] = dqr
        dqg_ref[...] += dqg

    blk = lambda off: pl.BlockSpec((s, LANES), lambda j: (0, off + j))
    full = _const((s, LANES))
    return pl.pallas_call(
        body,
        name="fox_bwd",
        grid=(npair,),
        in_specs=[blk(0), blk(npair), blk(0), blk(0), full, _const((16, s)), pl.BlockSpec((None, 1, LANES), lambda j: (j_layer, 0, 0)),
                  blk(0), blk(0), _const((16, s)), blk(0), blk(0), blk(0), full],
        out_specs=[blk(0)] * 4 + [full, _const((1, LANES))],
        out_shape=[jax.ShapeDtypeStruct((s, npair * LANES), F32)] * 4
        + [jax.ShapeDtypeStruct((s, LANES), F32), jax.ShapeDtypeStruct((1, LANES), F32)],
        scratch_shapes=[pltpu.VMEM((s, LANES), F32), pltpu.VMEM((s, LANES), F32)],
        compiler_params=_cparams(("arbitrary",)),
    )(proj, proj, k_sh, v_sh, clf, clf_t, qg3, o, lse, lse_t, dmixed, dk_in, dv_in, dclf_in)


def loss_head(y, target):
    s, d = y.shape
    tm = _tok(s)

    def body(y_ref, t_ref, loss_ref, dy_ref):
        err = y_ref[...] - t_ref[...]
        dy_ref[...] = err * (1.0 / d)

        @pl.when(pl.program_id(0) == 0)
        def _():
            loss_ref[...] = jnp.zeros_like(loss_ref)

        part = jnp.sum(jnp.mean(err * err, axis=-1, keepdims=True), axis=0, keepdims=True)
        loss_ref[...] += 0.5 * part

    row = pl.BlockSpec((tm, d), lambda i: (i, 0))
    return pl.pallas_call(
        body,
        name="loss_head",
        grid=(s // tm,),
        in_specs=[row, row],
        out_specs=[_const((1, 1)), row],
        out_shape=[jax.ShapeDtypeStruct((1, 1), F32), jax.ShapeDtypeStruct((s, d), F32)],
        compiler_params=_cparams(("arbitrary",)),
    )(y, target)


def _row_tile(r, c, n_arrays):
    budget = VMEM_LIMIT_BYTES // 2
    padded_c = -(-c // LANES) * LANES
    best = None
    for t in range(8, r + 1, 8):
        if r % t == 0 and 2 * n_arrays * t * padded_c * 4 <= budget:
            best = t
    return r if best is None else best


def _as2d(a):
    return a.reshape(-1, a.shape[-1]) if a.ndim >= 2 else a.reshape(1, -1)


def adamw(w, gs, m, v):
    shape = w.shape
    w2, m2, v2 = (_as2d(t) for t in (w, m, v))
    rows, c = w2.shape
    gs = [g.reshape(-1, c) for g in gs]
    assert sum(g.shape[0] for g in gs) == rows
    tr = _row_tile(min(g.shape[0] for g in gs), c, 8)
    assert all(g.shape[0] % tr == 0 for g in gs)
    c1 = 1.0 - ADAM_B1**ADAM_STEP
    c2 = 1.0 - ADAM_B2**ADAM_STEP
    outs = None
    first = 0
    for g in gs:
        n_prev = 0 if outs is None else 4
        r = g.shape[0]

        def body(w_ref, g_ref, m_ref, v_ref, *rest, n_prev=n_prev):
            go_ref, d_ref, nm_ref, nv_ref = rest[n_prev:]
            gv = g_ref[...]
            nm = ADAM_B1 * m_ref[...] + (1.0 - ADAM_B1) * gv
            nv = ADAM_B2 * v_ref[...] + (1.0 - ADAM_B2) * (gv * gv)
            go_ref[...] = gv
            nm_ref[...] = nm
            nv_ref[...] = nv
            d_ref[...] = -ADAM_LR * ((nm / c1) / (jnp.sqrt(nv / c2) + ADAM_EPS) + ADAM_WD * w_ref[...])

        spec = pl.BlockSpec((tr, c), lambda i, b0=first // tr: (b0 + i, 0))
        outs = pl.pallas_call(
            body,
            name="adamw",
            grid=(r // tr,),
            in_specs=[spec, pl.BlockSpec((tr, c), lambda i: (i, 0)), spec, spec] + [ANY] * n_prev,
            out_specs=[spec] * 4,
            out_shape=[jax.ShapeDtypeStruct((rows, c), F32)] * 4,
            input_output_aliases={4 + i: i for i in range(n_prev)},
            compiler_params=_cparams(("arbitrary",)),
        )(w2, g, m2, v2, *([] if outs is None else outs))
        first += r
    return tuple(t.reshape(shape) for t in outs)


def pair_sum(g, recv, c_arr):
    _, k, r, c = g.shape
    tr = _row_tile(r, c, 3)

    def body(c_ref, g_ref, r_ref, o_ref):
        o_ref[...] = bf(g_ref[...] + r_ref[...])

    return pl.pallas_call(
        body,
        name="pair_sum",
        grid_spec=pltpu.PrefetchScalarGridSpec(
            num_scalar_prefetch=1,
            grid=(k, r // tr),
            in_specs=[pl.BlockSpec((None, None, tr, c), lambda kk, i, cr: (cr[0], kk, i, 0)), pl.BlockSpec((None, tr, c), lambda kk, i, cr: (kk, i, 0))],
            out_specs=pl.BlockSpec((None, tr, c), lambda kk, i, cr: (kk, i, 0)),
        ),
        out_shape=jax.ShapeDtypeStruct((k, r, c), BF16),
        compiler_params=_cparams(("arbitrary", "arbitrary")),
    )(c_arr, g, recv)


def chip_sum(p, q, sel):
    _, r, c = p.shape
    tr = _row_tile(r, c, 4)

    def body(sel_ref, p_ref, q_ref, o_ref):
        acc = p_ref[...].astype(F32)
        for i in range(q.shape[0]):
            acc = acc + q_ref[i].astype(F32)
        o_ref[...] = acc

    return pl.pallas_call(
        body,
        name="chip_sum",
        grid_spec=pltpu.PrefetchScalarGridSpec(
            num_scalar_prefetch=1,
            grid=(r // tr,),
            in_specs=[pl.BlockSpec((None, tr, c), lambda i, sr: (sr[0], i, 0)), pl.BlockSpec((q.shape[0], tr, c), lambda i, sr: (0, i, 0))],
            out_specs=pl.BlockSpec((None, tr, c), lambda i, sr: (sr[1], i, 0)),
        ),
        out_shape=jax.ShapeDtypeStruct((2, r, c), F32),
        compiler_params=_cparams(("arbitrary",)),
    )(sel, p, q)


def cast_into_slot(w4, g, sel, dtype):
    _, _, r, c = w4.shape
    tr = _row_tile(r, c, 2)

    def body(sel_ref, w_ref, o_ref):
        o_ref[...] = w_ref[...].astype(dtype)

    return pl.pallas_call(
        body,
        name="cast_into_slot",
        grid_spec=pltpu.PrefetchScalarGridSpec(
            num_scalar_prefetch=1,
            grid=(2, r // tr),
            in_specs=[pl.BlockSpec((None, None, tr, c), lambda hf, i, sr: (g, hf, i, 0))],
            out_specs=pl.BlockSpec((None, None, tr, c), lambda hf, i, sr: (sr[0], hf, i, 0)),
        ),
        out_shape=jax.ShapeDtypeStruct((N_CHIPS, 2, r, c), dtype),
        compiler_params=_cparams(("arbitrary", "arbitrary")),
    )(sel, w4)


def _place():
    x, y, c = lax.axis_index("x"), lax.axis_index("y"), lax.axis_index("c")
    chips = [(1 - x, y), (x, 1 - y), (1 - x, 1 - y)]
    return x, y, c, 2 * x + y, chips, [2 * cx + cy for cx, cy in chips]


def _rcopy(src, dst, send, recv, dev):
    return pltpu.make_async_remote_copy(src_ref=src, dst_ref=dst, send_sem=send, recv_sem=recv, device_id=dev, device_id_type=MESH)


class Gather:
    def __init__(self, bufs):
        n = len(bufs)
        self.n = n
        self.args = list(bufs)
        self.out_shape = [jax.ShapeDtypeStruct(t.shape, t.dtype) for t in bufs]
        self.aliases = {a: a for a in range(n)}
        self.scratch = [pltpu.SemaphoreType.DMA((n, 6)), pltpu.SemaphoreType.DMA((n, 6))]

    def _sends(self, outs, send, recv):
        x, y, c, me, chips, _ = _place()
        cps = []
        for a in range(self.n):
            mine = outs[a].at[me, c]
            cps += [_rcopy(mine, mine, send.at[a, j], recv.at[a, j], (*chips[j], c)) for j in range(3)]
        return cps

    def start(self, ins, outs, scr):
        for cp in self._sends(outs, *scr):
            cp.start()

    def finish(self, ins, outs, scr):
        send, recv = scr
        x, y, c, me, chips, cidx = _place()
        sib = (x, y, 1 - c)
        passed = []
        for a in range(self.n):
            for j in range(3):
                landed = outs[a].at[cidx[j], c]
                _rcopy(landed, landed, send.at[a, j], recv.at[a, j], (*chips[j], c)).wait_recv()
                fwd = _rcopy(landed, landed, send.at[a, 3 + j], recv.at[a, 3 + j], sib)
                fwd.start()
                passed.append(fwd)
        for a in range(self.n):
            for j in range(3):
                theirs = outs[a].at[cidx[j], 1 - c]
                _rcopy(theirs, theirs, send.at[a, 3 + j], recv.at[a, 3 + j], sib).wait_recv()
        for cp in self._sends(outs, send, recv) + passed:
            cp.wait_send()


class PairExchange:
    def __init__(self, gs):
        n = len(gs)
        self.n = n
        self.args = list(gs)
        self.out_shape = [jax.ShapeDtypeStruct(t.shape[1:], t.dtype) for t in gs]
        self.aliases = {}
        self.scratch = [pltpu.SemaphoreType.DMA((n,)), pltpu.SemaphoreType.DMA((n,))]

    def _copies(self, ins, outs, send, recv):
        x, y, c = lax.axis_index("x"), lax.axis_index("y"), lax.axis_index("c")
        return [_rcopy(ins[a].at[1 - c], outs[a], send.at[a], recv.at[a], (x, y, 1 - c)) for a in range(self.n)]

    def start(self, ins, outs, scr):
        for cp in self._copies(ins, outs, *scr):
            cp.start()

    def finish(self, ins, outs, scr):
        for cp in self._copies(ins, outs, *scr):
            cp.wait()


class ChipExchange:
    def __init__(self, ps):
        n = len(ps)
        self.n = n
        self.args = list(ps)
        self.out_shape = [jax.ShapeDtypeStruct((3,) + t.shape[1:], t.dtype) for t in ps]
        self.aliases = {}
        self.scratch = [pltpu.SemaphoreType.DMA((n, 3)), pltpu.SemaphoreType.DMA((n, 3))]

    def _sends(self, ins, outs, send, recv):
        x, y, c, me, chips, cidx = _place()
        return [
            _rcopy(ins[a].at[cidx[j]], outs[a].at[j], send.at[a, j], recv.at[a, j], (*chips[j], c))
            for a in range(self.n)
            for j in range(3)
        ]

    def start(self, ins, outs, scr):
        for cp in self._sends(ins, outs, *scr):
            cp.start()

    def finish(self, ins, outs, scr):
        send, recv = scr
        x, y, c, me, chips, _ = _place()
        for a in range(self.n):
            for j in range(3):
                landed = outs[a].at[j]
                _rcopy(landed, landed, send.at[a, j], recv.at[a, j], (*chips[j], c)).wait_recv()
        for cp in self._sends(ins, outs, send, recv):
            cp.wait_send()


class PairShare:
    def __init__(self, bufs):
        n = len(bufs)
        self.n = n
        self.args = list(bufs)
        self.out_shape = [jax.ShapeDtypeStruct(t.shape, t.dtype) for t in bufs]
        self.aliases = {a: a for a in range(n)}
        self.scratch = [pltpu.SemaphoreType.DMA((n,)), pltpu.SemaphoreType.DMA((n,))]

    def _sends(self, outs, send, recv):
        x, y, c = lax.axis_index("x"), lax.axis_index("y"), lax.axis_index("c")
        return [_rcopy(outs[a].at[c], outs[a].at[c], send.at[a], recv.at[a], (x, y, 1 - c)) for a in range(self.n)]

    def start(self, ins, outs, scr):
        for cp in self._sends(outs, *scr):
            cp.start()

    def finish(self, ins, outs, scr):
        send, recv = scr
        x, y, c = lax.axis_index("x"), lax.axis_index("y"), lax.axis_index("c")
        for a in range(self.n):
            theirs = outs[a].at[1 - c]
            _rcopy(theirs, theirs, send.at[a], recv.at[a], (x, y, 1 - c)).wait_recv()
        for cp in self._sends(outs, send, recv):
            cp.wait_send()


class Multi:
    def __init__(self, comms):
        self.comms = comms
        self.args, self.out_shape, self.scratch, self.aliases = [], [], [], {}
        self.spans = []
        for cm in comms:
            a0, o0, s0 = len(self.args), len(self.out_shape), len(self.scratch)
            self.aliases.update({a0 + i: o0 + o for i, o in cm.aliases.items()})
            self.args += cm.args
            self.out_shape += cm.out_shape
            self.scratch += cm.scratch
            self.spans.append((slice(a0, len(self.args)), slice(o0, len(self.out_shape)), slice(s0, len(self.scratch))))

    def start(self, ins, outs, scr):
        for cm, (sa, so, ss) in zip(self.comms, self.spans):
            cm.start(ins[sa], outs[so], scr[ss])

    def finish(self, ins, outs, scr):
        for cm, (sa, so, ss) in zip(self.comms, self.spans):
            cm.finish(ins[sa], outs[so], scr[ss])

    def split(self, res):
        return [list(res[so]) for _, so, _ in self.spans]


def run_comm(comm, name):
    na, no = len(comm.args), len(comm.out_shape)

    def body(*refs):
        ins, outs, scr = refs[:na], refs[na : na + no], refs[na + no :]
        comm.start(ins, outs, scr)
        comm.finish(ins, outs, scr)

    return pl.pallas_call(
        body,
        name=name,
        in_specs=[ANY] * na,
        out_specs=[ANY] * no,
        out_shape=comm.out_shape,
        input_output_aliases=comm.aliases,
        scratch_shapes=comm.scratch,
    )(*comm.args)


def _carry(body, comm, *, name, grid, in_specs, out_specs, out_shape, scratch_shapes, args):
    params = _cparams(("arbitrary",))
    if comm is None:
        res = pl.pallas_call(body, name=name, grid=grid, in_specs=in_specs, out_specs=out_specs, out_shape=out_shape,
                             scratch_shapes=scratch_shapes, compiler_params=params)(*args)
        return res, None
    ni, no, ns = len(in_specs), len(out_specs), len(scratch_shapes)
    ci, co = len(comm.args), len(comm.out_shape)

    def wrapped(*refs):
        ins, c_ins = refs[:ni], refs[ni : ni + ci]
        p = ni + ci
        outs, c_outs = refs[p : p + no], refs[p + no : p + no + co]
        p += no + co
        scr, c_scr = refs[p : p + ns], refs[p + ns :]

        @pl.when(pl.program_id(0) == 0)
        def _():
            comm.start(c_ins, c_outs, c_scr)

        body(*ins, *outs, *scr)

        @pl.when(pl.program_id(0) == grid[0] - 1)
        def _():
            comm.finish(c_ins, c_outs, c_scr)

    res = pl.pallas_call(
        wrapped,
        name=name + "_carry",
        grid=grid,
        in_specs=list(in_specs) + [ANY] * ci,
        out_specs=list(out_specs) + [ANY] * co,
        out_shape=list(out_shape) + list(comm.out_shape),
        input_output_aliases={ni + i: no + o for i, o in comm.aliases.items()},
        scratch_shapes=list(scratch_shapes) + list(comm.scratch),
        compiler_params=params,
    )(*args, *comm.args)
    return res[:no], res[no:]


def small_allreduce(buf):
    r = buf.shape[0]

    def body(b_ref, o_ref, slots, send, recv):
        x, y, c = lax.axis_index("x"), lax.axis_index("y"), lax.axis_index("c")
        me = 4 * x + 2 * y + c
        slots[me] = b_ref[...]
        cps = []
        peers = []
        for mask in range(1, N_DEV):
            fx, fy, fc = (mask >> 2) & 1, (mask >> 1) & 1, mask & 1
            px, py, pc = (1 - x if fx else x), (1 - y if fy else y), (1 - c if fc else c)
            peers.append(4 * px + 2 * py + pc)
            cps.append(_rcopy(b_ref, slots.at[me], send.at[mask - 1], recv.at[mask - 1], (px, py, pc)))
        for cp in cps:
            cp.start()
        for k, pid in enumerate(peers):
            landed = slots.at[pid]
            _rcopy(landed, landed, send.at[k], recv.at[k], (x, y, c)).wait_recv()
        for cp in cps:
            cp.wait_send()
        acc = slots[0]
        for i in range(1, N_DEV):
            acc = acc + slots[i]
        o_ref[...] = acc

    vm = pl.BlockSpec(memory_space=pltpu.VMEM)
    return pl.pallas_call(
        body,
        name="small_allreduce",
        in_specs=[vm],
        out_specs=vm,
        out_shape=jax.ShapeDtypeStruct(buf.shape, F32),
        scratch_shapes=[pltpu.VMEM((N_DEV, r, LANES), F32), pltpu.SemaphoreType.DMA((N_DEV - 1,)), pltpu.SemaphoreType.DMA((N_DEV - 1,))],
    )(buf)


WEIGHT_NAMES = ["ffn1_norm", "ffn1_w_gate", "ffn1_w_up", "ffn1_w_down", "mix_norm", "mem_norm", "w_mem_kv", "mem_q_gain",
                "mem_k_gain", "w_in_a", "hgrn_lb_logits", "hgrn_o_gain", "w_in_b", "fox_q_gain", "kv_norm", "w_kv", "fox_f_bias",
                "fox_k_gain", "w_out", "ffn2_norm", "ffn2_w_gate", "ffn2_w_up", "ffn2_w_down"]
SHARDED = ["ffn1_w_gate", "ffn1_w_up", "ffn1_w_down", "w_mem_kv", "w_in_a", "w_in_b", "w_kv", "w_out", "ffn2_w_gate", "ffn2_w_up", "ffn2_w_down"]
SMALL = [n for n in WEIGHT_NAMES if n not in SHARDED]
FFN1 = ["ffn1_w_gate", "ffn1_w_up", "ffn1_w_down"]
FFN2 = ["ffn2_w_gate", "ffn2_w_up", "ffn2_w_down"]
PER_LAYER = FFN1 + FFN2 + ["w_mem_kv", "w_out"]
TRANSPOSED = ["ffn1_w_gate", "ffn1_w_up", "ffn2_w_gate", "ffn2_w_up", "w_in_a", "w_in_b"]
N_LAYERS, N_A = 4, 2
KV_PAD = 13 * LANES


def _halves(t):
    return t.reshape((2, t.shape[0] // 2) + t.shape[1:])


def _cols_from_chips(g):
    return jnp.moveaxis(g, 0, 2).reshape(g.shape[1], g.shape[2], N_CHIPS * g.shape[3])


def _rows_from_chips(g):
    return jnp.moveaxis(g, 0, 1).reshape(g.shape[1], N_CHIPS * g.shape[2], g.shape[3])


def _pair_tile(g):
    return jnp.tile(g, (1, 2)).reshape(g.shape[0], 1, LANES)


def _pair_fold(g):
    return g[:, :HEAD64] + g[:, HEAD64:]


def kernel(x, mem, ffn1_norm, ffn1_w_gate, ffn1_w_up, ffn1_w_down, mix_norm, mem_norm, w_mem_kv, mem_q_gain, mem_k_gain, w_in_a, hgrn_lb_logits, hgrn_o_gain, w_in_b, fox_q_gain, kv_norm, w_kv, fox_f_bias, fox_k_gain, w_out, ffn2_norm, ffn2_w_gate, ffn2_w_up, ffn2_w_down, loss_target, m_ffn1_norm, m_ffn1_w_gate, m_ffn1_w_up, m_ffn1_w_down, m_mix_norm, m_mem_norm, m_w_mem_kv, m_mem_q_gain, m_mem_k_gain, m_w_in_a, m_hgrn_lb_logits, m_hgrn_o_gain, m_w_in_b, m_fox_q_gain, m_kv_norm, m_w_kv, m_fox_f_bias, m_fox_k_gain, m_w_out, m_ffn2_norm, m_ffn2_w_gate, m_ffn2_w_up, m_ffn2_w_down, v_ffn1_norm, v_ffn1_w_gate, v_ffn1_w_up, v_ffn1_w_down, v_mix_norm, v_mem_norm, v_w_mem_kv, v_mem_q_gain, v_mem_k_gain, v_w_in_a, v_hgrn_lb_logits, v_hgrn_o_gain, v_w_in_b, v_fox_q_gain, v_kv_norm, v_w_kv, v_fox_f_bias, v_fox_k_gain, v_w_out, v_ffn2_norm, v_ffn2_w_gate, v_ffn2_w_up, v_ffn2_w_down):
    given = dict(locals())
    def oriented(n, t):
        return jnp.swapaxes(t, 1, 2) if n in TRANSPOSED else t

    w = {n: oriented(n, given[n]) for n in WEIGHT_NAMES}
    xs, mems, tgt = x[0], mem[0], loss_target[0]
    s, d = xs.shape
    my_chip = 2 * lax.axis_index("x") + lax.axis_index("y")
    sel = jnp.stack([my_chip, lax.axis_index("c")]).astype(jnp.int32)
    c_arr = sel[1:]

    def cast(n, g=0, ng=1):
        t = w[n]
        if t.ndim == 2:
            t4 = t.reshape(1, 2, t.shape[0] // 2, t.shape[1])
        else:
            t4 = t.reshape(ng, 2, (t.shape[0] // (2 * ng)) * t.shape[1], t.shape[2])
        return cast_into_slot(t4, g, sel, BF16)

    def view(buf, n):
        return buf.reshape((N_CHIPS, 2) + w[n].shape[1:]) if w[n].ndim == 3 else buf.reshape((N_CHIPS,) + w[n].shape)

    names0 = PER_LAYER + ["w_in_a", "w_kv"]
    bufs0 = [cast(n, 0, 2) for n in PER_LAYER] + [cast("w_in_a"), cast("w_kv")]
    bufs0.append(cast_into_slot(hgrn_lb_logits.reshape(1, 2, 1, -1), 0, sel, F32))
    got0 = run_comm(Gather(bufs0), "gather_group0")
    gw = [{n: view(b, n) for n, b in zip(names0, got0[:-1])}, None]
    buf1 = {n: cast(n, 1, 2) for n in PER_LAYER}
    buf1["w_in_b"] = cast("w_in_b")
    got1 = {}
    w_in = {"a": _rows_from_chips(gw[0]["w_in_a"])}
    w_kv_full = _cols_from_chips(gw[0]["w_kv"][:, None])
    w_kv_full = jnp.pad(w_kv_full, ((0, 0), (0, 0), (0, KV_PAD - w_kv_full.shape[-1])))
    w_mkv = [_rows_from_chips(gw[0]["w_mem_kv"]), None]
    w_o = [_rows_from_chips(gw[0]["w_out"]), None]
    logits3 = jnp.moveaxis(got0[-1].reshape(N_CHIPS, 2, -1), 0, 1).reshape(2, 1, -1)
    lb3 = lb_fwd(logits3)
    carried = {(0, "mix"): FFN1, (0, "ffn2"): ["w_in_b"], (1, "ffn1"): ["w_out", "w_mem_kv"], (1, "mix"): FFN2}

    def gather1(key):
        names = carried.get(key)
        return None if names is None else Gather([buf1[n] for n in names])

    def landed1(key, res):
        if res is not None:
            got1.update({n: view(b, n) for n, b in zip(carried[key], res)})

    norm3 = {n: w[n].reshape(N_LAYERS, 1, d) for n in ("ffn1_norm", "mix_norm", "mem_norm", "ffn2_norm")}
    kvn3 = kv_norm.reshape(1, 1, d)
    mqg3, mkg3 = _pair_tile(mem_q_gain), _pair_tile(mem_k_gain)
    og3 = hgrn_o_gain.reshape(N_A, 1, LANES)
    fqg3 = _pair_tile(fox_q_gain)
    fkg = jnp.tile(fox_k_gain, 2).reshape(1, LANES)
    fb = jnp.pad(fox_f_bias, (0, LANES - fox_f_bias.shape[0])).reshape(1, LANES)

    sv = [dict() for _ in range(N_LAYERS)]
    h = xs
    kv = None
    for l in range(N_LAYERS):
        t = sv[l]
        gi, li = l // 2, l % 2
        if l == N_A:
            gw[1] = got1
            w_in["b"] = _rows_from_chips(got1["w_in_b"])
            w_mkv[1], w_o[1] = _rows_from_chips(got1["w_mem_kv"]), _rows_from_chips(got1["w_out"])
        t["x0"] = h
        (h, t["a1"], t["b1"]), res = ffn_fwd(h, norm3["ffn1_norm"], l, *[gw[gi][n] for n in FFN1], li, comm=gather1((l, "ffn1")))
        landed1((l, "ffn1"), res)
        t["x1"] = h
        if l < N_A:
            t["proj"] = proj_fwd(h, norm3["mix_norm"], l, w_in["a"], l, wt=True)
            (main, t["o"]), res = hgrn_fwd(t["proj"], lb3, og3, l, comm=gather1((l, "mix")))
            landed1((l, "mix"), res)
            t["qblk"] = 12
        else:
            t["proj"] = proj_fwd(h, norm3["mix_norm"], l, w_in["b"], l - N_A, wt=True)
            main, t["o"], t["lse"] = fox_fwd(t["proj"], kv["k"], kv["v"], kv["clf"], kv["clf_t"], fqg3, l - N_A)
            t["qblk"] = 6
        t["kvm"] = proj_fwd(mems, norm3["mem_norm"], l, w_mkv[gi], li)
        memo = memattn_fwd(t["proj"], t["qblk"], t["kvm"], mqg3, mkg3, l)
        t["mixed"] = jnp.concatenate([main, memo], axis=-1)
        h = mm_res(h, t["mixed"], w_o[gi], li)
        t["x2"] = h
        (h, t["a2"], t["b2"]), res = ffn_fwd(h, norm3["ffn2_norm"], l, *[gw[gi][n] for n in FFN2], li, comm=gather1((l, "ffn2")))
        landed1((l, "ffn2"), res)
        if l == N_A - 1:
            kv = {"x": h, "kvf": proj_fwd(h, kvn3, 0, w_kv_full, 0)}
            kv["k"], kv["v"], kv["clf"] = kvprep_fwd(kv["kvf"], fkg, fb)
            kv["clf_t"] = kv["clf"][:, :16].T

    loss_local, dx = loss_head(h, tgt)

    nc = N_CHIPS
    fc = ffn1_w_down.shape[1]
    FOX_GROUP = "fox"
    gbuf1 = {n: lax.empty((2, nc, 1, fc, d), F32) for n in FFN1 + FFN2}
    gsplit = [dict(), dict()]

    def row_layout(stack):
        lw, rr, cc = stack.shape
        return stack.reshape(lw, nc, rr // nc, cc)

    def halves_layout(g):
        rr, cc = g.shape
        return jnp.transpose(g.reshape(nc, 2, rr // (2 * nc), cc), (1, 0, 2, 3))

    def group_layout(key):
        if key == FOX_GROUP:
            lay = {n: b.reshape(2, nc, fc, d) for n, b in gbuf1.items()}
            lay["w_mem_kv"] = row_layout(jnp.stack(dw_mkv[N_A:]))
            lay["w_out"] = row_layout(jnp.stack(dw_o[N_A:]))
            lay["w_in_b"] = row_layout(jnp.stack(dw_in["b"]))
            names = PER_LAYER + ["w_in_b"]
        else:
            lay = {n: b.reshape(2, nc, fc // 2, d) for n, b in gsplit[key].items()}
            lay["w_mem_kv"], lay["w_out"] = halves_layout(dw_mkv[key]), halves_layout(dw_o[key])
            lay["w_in_a"] = halves_layout(dw_in["a"][key])
            names = PER_LAYER + ["w_in_a"]
            if key == N_A - 1:
                kv_cols = w_kv.shape[-1] * nc
                lay["w_kv"] = jnp.transpose(dw_kv[:, :kv_cols].reshape(2, d // 2, nc, kv_cols // nc), (0, 2, 1, 3))
                names = names + ["w_kv"]
        return names, [lay[n] for n in names]

    n_ffn = len(FFN1) + len(FFN2)
    riding = {1: FOX_GROUP, 0: 1}
    reduced = {}
    unshared = None
    dw_in = {"a": [None] * N_A, "b": [None] * (N_LAYERS - N_A)}
    dw_o, dw_mkv = [None] * N_LAYERS, [None] * N_LAYERS
    sg = {n: [None] * N_LAYERS for n in ("ffn1_norm", "mix_norm", "mem_norm", "ffn2_norm", "mem_q_gain", "mem_k_gain")}
    sg["hgrn_o_gain"], sg["fox_q_gain"], dlb = [None] * N_A, [None] * (N_LAYERS - N_A), [None] * N_A
    dk_sh = jnp.zeros((s, KV_MAIN), F32)
    dv_sh = jnp.zeros((s, KV_MAIN), F32)
    dclf = jnp.zeros((s, LANES), F32)
    zero_mem = jnp.zeros(mems.shape, F32)
    dw_kv = None
    for l in reversed(range(N_LAYERS)):
        t = sv[l]
        gi, li = l // 2, l % 2
        if l == N_A - 1:
            dkvf, dfkg, dfb = kvprep_bwd(kv["kvf"], fkg, fb, dk_sh, dv_sh, dclf)
            dx, sg["kv_norm"], xn_kv, dpb = proj_bwd(kv["x"], kvn3, 0, [dkvf], w_kv_full, 0, dx)
            dw_kv = wgrad(xn_kv, dpb)
        ride = riding.get(l)
        comms = []
        if unshared is not None:
            comms.append(PairShare(unshared[2]))
        if ride is not None:
            names_r, gl_r = group_layout(ride)
            comms.append(PairExchange(gl_r))
        comm = Multi(comms) if comms else None
        (dx, da, db, hm, xn, dyb, sg["ffn2_norm"][l]), res = ffn_bwd(t["x2"], norm3["ffn2_norm"], l, dx, t["a2"], t["b2"], *[gw[gi][n] for n in FFN2], li, comm=comm)
        if comm is not None:
            res = comm.split(res)
            if unshared is not None:
                reduced[unshared[0]] = dict(zip(unshared[1], res.pop(0)))
                unshared = None
            if ride is not None:
                partial_r = [pair_sum(g, r, c_arr) for g, r in zip(gl_r, res.pop(0))]

        def ffn_wgrads(which, da, db, hm, xn, dyb):
            for n, (a_, b_) in zip(which, ((da, xn), (db, xn), (hm, dyb))):
                if gi == 1:
                    gbuf1[n] = wgrad(a_, b_, buf=gbuf1[n], l=li)
                else:
                    gsplit[l][n] = wgrad(a_, b_, split=True)

        ffn_wgrads(FFN2, da, db, hm, xn, dyb)
        dmixed, dxb = mm_nt(dx, w_o[gi], li)
        dw_o[l] = wgrad(t["mixed"], dxb)
        dqm, dkvm, dmq, dmk = memattn_bwd(t["proj"], t["qblk"], t["kvm"], mqg3, mkg3, l, dmixed)
        sg["mem_q_gain"][l], sg["mem_k_gain"][l] = _pair_fold(dmq), _pair_fold(dmk)
        _, sg["mem_norm"][l], memn, dkvmb = proj_bwd(mems, norm3["mem_norm"], l, [dkvm], w_mkv[gi], li, zero_mem)
        dw_mkv[l] = wgrad(memn, dkvmb)
        if l < N_A:
            comm = ChipExchange(partial_r[:n_ffn]) if ride is not None else None
            (dzq, dzf, dvi, dzg, dlb[l], sg["hgrn_o_gain"][l]), res = hgrn_bwd(t["proj"], lb3, og3, l, t["o"], dmixed, comm=comm)
            if ride is not None:
                landed_r = list(res)
            parts, key, wl, tmw = [dzq, dzf, dvi, dzg, dqm], "a", l, 13 * LANES
        else:
            lse_t = t["lse"].reshape(s, 6, LANES)[:, :, :2].reshape(s, 12).T
            lse_t = jnp.pad(lse_t, ((0, 4), (0, 0)))
            dq, dgate, dk_sh, dv_sh, dclf, dfq = fox_bwd(t["proj"], kv["k"], kv["v"], kv["clf"], kv["clf_t"], fqg3, l - N_A, t["o"], t["lse"], lse_t, dmixed, dk_sh, dv_sh, dclf)
            sg["fox_q_gain"][l - N_A] = _pair_fold(dfq)
            parts, key, wl, tmw = [dq, dgate, dqm], "b", l - N_A, 7 * LANES
        dx, sg["mix_norm"][l], hn, dpb = proj_bwd(t["x1"], norm3["mix_norm"], l, parts, w_in[key], wl, dx, wt=True)
        dw_in[key][wl] = wgrad(dpb, hn, tm=tmw)
        comm = ChipExchange(partial_r[n_ffn:]) if ride is not None else None
        (dx, da, db, hm, xn, dyb, sg["ffn1_norm"][l]), res = ffn_bwd(t["x0"], norm3["ffn1_norm"], l, dx, t["a1"], t["b1"], *[gw[gi][n] for n in FFN1], li, comm=comm)
        if ride is not None:
            unshared = (ride, names_r, [chip_sum(p, q, sel) for p, q in zip(partial_r, landed_r + list(res))])
        ffn_wgrads(FFN1, da, db, hm, xn, dyb)

    names0, gl0 = group_layout(0)
    recv0 = run_comm(PairExchange(gl0), "pair_exchange")
    partial0 = [pair_sum(g, r, c_arr) for g, r in zip(gl0, recv0)]
    landed0 = run_comm(ChipExchange(partial0), "chip_exchange")
    mine0 = [chip_sum(p, q, sel) for p, q in zip(partial0, landed0)]
    both = run_comm(PairShare(unshared[2] + mine0), "pair_share")
    reduced[unshared[0]] = dict(zip(unshared[1], both[: len(unshared[1])]))
    reduced[0] = dict(zip(names0, both[len(unshared[1]) :]))
    gparts = {n: [reduced[0][n], reduced[1][n], reduced[FOX_GROUP][n]] for n in PER_LAYER}
    gparts.update({"w_in_a": [reduced[0]["w_in_a"], reduced[1]["w_in_a"]], "w_kv": [reduced[1]["w_kv"]],
                   "w_in_b": [reduced[FOX_GROUP]["w_in_b"]]})

    dlogits = lb_bwd(logits3, dlb[1]).reshape(2, -1)
    small = {
        "ffn1_norm": jnp.concatenate(sg["ffn1_norm"]), "mix_norm": jnp.concatenate(sg["mix_norm"]),
        "mem_norm": jnp.concatenate(sg["mem_norm"]), "ffn2_norm": jnp.concatenate(sg["ffn2_norm"]),
        "mem_q_gain": jnp.concatenate(sg["mem_q_gain"]), "mem_k_gain": jnp.concatenate(sg["mem_k_gain"]),
        "hgrn_o_gain": jnp.concatenate(sg["hgrn_o_gain"]), "fox_q_gain": jnp.concatenate(sg["fox_q_gain"]),
        "kv_norm": sg["kv_norm"], "fox_f_bias": dfb[:, : fox_f_bias.shape[0]], "fox_k_gain": _pair_fold(dfkg),
        "hgrn_lb_logits": dlogits,
    }
    flat = [small[n].reshape(-1) for n in SMALL] + [loss_local.reshape(-1)]
    sizes = [f.shape[0] for f in flat]
    total = sum(sizes)
    padded = -(-total // (8 * LANES)) * (8 * LANES)
    packed = jnp.pad(jnp.concatenate(flat), (0, padded - total)).reshape(-1, LANES)
    summed = small_allreduce(packed).reshape(-1)
    off = 0
    for n, sz in zip(SMALL, sizes[:-1]):
        gparts[n] = [summed[off : off + sz].reshape(dlogits.shape if n == "hgrn_lb_logits" else w[n].shape)]
        off += sz
    loss = summed[off]
    lbw = hgrn_lb_logits.shape[1]
    gparts["hgrn_lb_logits"] = [lax.dynamic_slice_in_dim(gparts["hgrn_lb_logits"][0], my_chip * lbw, lbw, axis=1)]

    grads, delta, new_m, new_v = {}, {}, {}, {}
    for n in WEIGHT_NAMES:
        res = adamw(w[n], gparts[n], oriented(n, given["m_" + n]), oriented(n, given["v_" + n]))
        grads[n], delta[n], new_m[n], new_v[n] = (oriented(n, t) for t in res)
    return (loss, dx[None], *[grads[n] for n in WEIGHT_NAMES], *[delta[n] for n in WEIGHT_NAMES],
            *[new_m[n] for n in WEIGHT_NAMES], *[new_v[n] for n in WEIGHT_NAMES])
```

```python
import functools

import jax
import jax.numpy as jnp
from jax import lax
from jax.experimental import pallas as pl
from jax.experimental.pallas import tpu as pltpu

F32, BF16 = jnp.float32, jnp.bfloat16
HI = lax.Precision.HIGHEST
EPS = 1e-6
MESH = pl.DeviceIdType.MESH
ANY = pl.BlockSpec(memory_space=pl.ANY)

VMEM_LIMIT_BYTES = 56 << 20
N_CHIPS = 4
N_DEV = 8
LANES = 128
HEAD64 = 64
CHUNK = 64
SUB = 16
HGRN_HEADS_PER_STEP = 2
TQ = 256
TOK = 256

ADAM_LR, ADAM_B1, ADAM_B2, ADAM_EPS, ADAM_WD, ADAM_STEP = 0.001, 0.9, 0.999, 1e-08, 0.01, 10


def _cparams(sem=None):
    return pltpu.CompilerParams(dimension_semantics=sem, vmem_limit_bytes=VMEM_LIMIT_BYTES)


def _mm(a, b, dims, prec=None):
    return lax.dot_general(a, b, (dims, ((), ())), preferred_element_type=F32, precision=prec)


def dot_nn(a, b, prec=None):
    return _mm(a, b, ((1,), (0,)), prec)


def dot_nt(a, b, prec=None):
    return _mm(a, b, ((1,), (1,)), prec)


def dot_tn(a, b, prec=None):
    return _mm(a, b, ((0,), (0,)), prec)


def bf(v):
    return v.astype(BF16)


def _sigmoid(z):
    return jax.nn.sigmoid(z)


def _dsilu(z, s):
    return s * (1.0 + z * (1.0 - s))


def _rms(x):
    r = lax.rsqrt(jnp.mean(x * x, axis=-1, keepdims=True) + EPS)
    return x * r, r


def _rms_bwd(dxn, u, r, g):
    du = dxn * g
    dx = r * (du - u * jnp.mean(du * u, axis=-1, keepdims=True))
    return dx, jnp.sum(dxn * u, axis=0, keepdims=True)


def _lane_mask0(shape):
    return lax.broadcasted_iota(jnp.int32, shape, len(shape) - 1) < HEAD64


def _rms64(x, m0):
    sq = x * x
    s0 = jnp.sum(jnp.where(m0, sq, 0.0), axis=-1, keepdims=True)
    s1 = jnp.sum(jnp.where(m0, 0.0, sq), axis=-1, keepdims=True)
    r = lax.rsqrt(jnp.where(m0, s0, s1) * (1.0 / HEAD64) + EPS)
    return x * r, r


def _rms64_bwd(dxn, u, r, g, m0):
    du = dxn * g
    t = du * u
    t0 = jnp.sum(jnp.where(m0, t, 0.0), axis=-1, keepdims=True)
    t1 = jnp.sum(jnp.where(m0, 0.0, t), axis=-1, keepdims=True)
    dx = r * (du - u * (jnp.where(m0, t0, t1) * (1.0 / HEAD64)))
    return dx, jnp.sum(dxn * u, axis=0, keepdims=True)


def _tok(s):
    return TOK if s % TOK == 0 else s


def _const(shape):
    return pl.BlockSpec(shape, lambda *_: (0,) * len(shape))


def ffn_fwd(x, gain3, l, wg, wu, wd, wl, comm=None):
    s, d = x.shape
    nc, _, fc, _ = wg.shape
    tm = _tok(s)

    def body(x_ref, g_ref, wg_ref, wu_ref, wd_ref, xo_ref, a_ref, b_ref):
        xv = x_ref[...]
        u, _ = _rms(xv)
        xn = bf(u * g_ref[...])
        y = jnp.zeros((tm, d), F32)
        for c in range(nc):
            a = dot_nt(xn, wg_ref[c])
            b = dot_nt(xn, wu_ref[c])
            a_ref[c] = bf(a)
            b_ref[c] = bf(b)
            y = y + dot_nn(bf(a * _sigmoid(a) * b), wd_ref[c])
        xo_ref[...] = xv + 0.5 * y

    wspec = pl.BlockSpec((nc, None, fc, d), lambda i: (0, wl, 0, 0), pipeline_mode=pl.Buffered(1))
    wdspec = pl.BlockSpec((nc, None, fc, d), lambda i: (0, wl, 0, 0), pipeline_mode=pl.Buffered(1))
    row = pl.BlockSpec((tm, d), lambda i: (i, 0))
    act = pl.BlockSpec((nc, tm, fc), lambda i: (0, i, 0))
    return _carry(
        body,
        comm,
        name="ffn_fwd",
        grid=(s // tm,),
        in_specs=[row, pl.BlockSpec((None, 1, d), lambda i: (l, 0, 0)), wspec, wspec, wdspec],
        out_specs=[row, act, act],
        out_shape=[
            jax.ShapeDtypeStruct((s, d), F32),
            jax.ShapeDtypeStruct((nc, s, fc), BF16),
            jax.ShapeDtypeStruct((nc, s, fc), BF16),
        ],
        scratch_shapes=[],
        args=(x, gain3, wg, wu, wd),
    )


def ffn_bwd(x, gain3, l, dout, a, b, wg, wu, wd, wl, comm=None):
    s, d = x.shape
    nc, _, fc, _ = wg.shape
    tm = _tok(s)

    def body(x_ref, g_ref, do_ref, a_ref, b_ref, wg_ref, wu_ref, wd_ref, dx_ref, da_ref, db_ref, hm_ref, xn_ref, dy_ref, dg_ref):
        xv = x_ref[...]
        g = g_ref[...]
        u, r = _rms(xv)
        xn_ref[...] = bf(u * g)
        dout = do_ref[...]
        dy = bf(0.5 * dout)
        dy_ref[...] = dy
        dxn = jnp.zeros((tm, d), F32)
        for c in range(nc):
            av = a_ref[c].astype(F32)
            bv = b_ref[c].astype(F32)
            sg = _sigmoid(av)
            sl = av * sg
            dh = dot_nt(dy, wd_ref[c])
            da = bf(dh * bv * _dsilu(av, sg))
            db = bf(dh * sl)
            da_ref[c] = da
            db_ref[c] = db
            hm_ref[c] = bf(sl * bv)
            dxn = dxn + dot_nn(da, wg_ref[c]) + dot_nn(db, wu_ref[c])
        dx, dg = _rms_bwd(dxn, u, r, g)
        dx_ref[...] = dout + dx

        @pl.when(pl.program_id(0) == 0)
        def _():
            dg_ref[...] = jnp.zeros_like(dg_ref)

        dg_ref[...] += dg

    wspec = pl.BlockSpec((nc, None, fc, d), lambda i: (0, wl, 0, 0), pipeline_mode=pl.Buffered(1))
    wdspec = pl.BlockSpec((nc, None, fc, d), lambda i: (0, wl, 0, 0), pipeline_mode=pl.Buffered(1))
    row = pl.BlockSpec((tm, d), lambda i: (i, 0))
    act = pl.BlockSpec((nc, tm, fc), lambda i: (0, i, 0))
    act_shape = jax.ShapeDtypeStruct((nc, s, fc), BF16)
    return _carry(
        body,
        comm,
        name="ffn_bwd",
        grid=(s // tm,),
        in_specs=[row, pl.BlockSpec((None, 1, d), lambda i: (l, 0, 0)), row, act, act, wspec, wspec, wdspec],
        out_specs=[row, act, act, act, row, row, _const((1, d))],
        out_shape=[
            jax.ShapeDtypeStruct((s, d), F32),
            act_shape,
            act_shape,
            act_shape,
            jax.ShapeDtypeStruct((s, d), BF16),
            jax.ShapeDtypeStruct((s, d), BF16),
            jax.ShapeDtypeStruct((1, d), F32),
        ],
        scratch_shapes=[],
        args=(x, gain3, dout, a, b, wg, wu, wd),
    )


def wgrad(a, b, tn=None, buf=None, l=None, tm=None, split=False):
    ca = a.shape[0] if a.ndim == 3 else 1
    cb = b.shape[0] if b.ndim == 3 else 1
    nc = max(ca, cb)
    s, m = a.shape[-2:]
    n = b.shape[-1]
    tn = n if tn is None else tn
    assert n % tn == 0

    def body(*refs):
        a_ref, b_ref, o_ref = refs[0], refs[1], refs[-1]
        res = dot_tn(a_ref[...], b_ref[...])
        if split:
            o_ref[0] = res[: m // 2]
            o_ref[1] = res[m // 2 :]
        else:
            o_ref[...] = res

    if buf is None and not split:
        assert nc == 1 and a.ndim == 2 and b.ndim == 2
        tm = m if tm is None else tm
        assert m % tm == 0
        return pl.pallas_call(
            body,
            name="wgrad",
            grid=(m // tm, n // tn),
            in_specs=[pl.BlockSpec((s, tm), lambda i, j: (0, i)), pl.BlockSpec((s, tn), lambda i, j: (0, j))],
            out_specs=pl.BlockSpec((tm, tn), lambda i, j: (i, j)),
            out_shape=jax.ShapeDtypeStruct((m, n), F32),
            compiler_params=_cparams(("arbitrary", "arbitrary")),
        )(a, b)
    a_spec = pl.BlockSpec((None, s, m), lambda c, j: (c, 0, 0)) if a.ndim == 3 else pl.BlockSpec((s, m), lambda c, j: (0, 0))
    b_spec = pl.BlockSpec((None, s, tn), lambda c, j: (c, 0, j)) if b.ndim == 3 else pl.BlockSpec((s, tn), lambda c, j: (0, j))
    if split:
        return pl.pallas_call(
            body,
            name="wgrad_split",
            grid=(nc, n // tn),
            in_specs=[a_spec, b_spec],
            out_specs=pl.BlockSpec((2, None, None, m // 2, tn), lambda c, j: (0, c, 0, 0, j)),
            out_shape=jax.ShapeDtypeStruct((2, nc, 1, m // 2, n), F32),
            compiler_params=_cparams(("arbitrary", "arbitrary")),
        )(a, b)
    lh = buf.shape[2]
    hi, lo = l // lh, l % lh
    o_spec = pl.BlockSpec((None, None, None, m, tn), lambda c, j: (hi, c, lo, 0, j))
    return pl.pallas_call(
        body,
        name="wgrad_buf",
        grid=(nc, n // tn),
        in_specs=[a_spec, b_spec, ANY],
        out_specs=o_spec,
        out_shape=jax.ShapeDtypeStruct(buf.shape, F32),
        input_output_aliases={2: 0},
        compiler_params=_cparams(("arbitrary", "arbitrary")),
    )(a, b, buf)


def proj_fwd(x, gain3, l, w, wl, wt=False):
    s, d = x.shape
    n = w.shape[1] if wt else w.shape[2]
    tm = _tok(s)

    def body(x_ref, g_ref, w_ref, o_ref):
        u, _ = _rms(x_ref[...])
        xn = bf(u * g_ref[...])
        o_ref[...] = dot_nt(xn, w_ref[...]) if wt else dot_nn(xn, w_ref[...])

    return pl.pallas_call(
        body,
        name="proj_fwd",
        grid=(s // tm,),
        in_specs=[
            pl.BlockSpec((tm, d), lambda i: (i, 0)),
            pl.BlockSpec((None, 1, d), lambda i: (l, 0, 0)),
            pl.BlockSpec((None,) + w.shape[1:], lambda i: (wl, 0, 0)),
        ],
        out_specs=pl.BlockSpec((tm, n), lambda i: (i, 0)),
        out_shape=jax.ShapeDtypeStruct((s, n), F32),
        compiler_params=_cparams(("arbitrary",)),
    )(x, gain3, w)


def proj_bwd(x, gain3, l, parts, w, wl, dx_in, wt=False):
    s, d = x.shape
    n = w.shape[1] if wt else w.shape[2]
    widths = [p.shape[1] for p in parts]
    assert sum(widths) == n
    tm = _tok(s)
    npart = len(parts)

    def body(*refs):
        x_ref, g_ref, w_ref, dxin_ref = refs[:4]
        p_refs = refs[4 : 4 + npart]
        dx_ref, dg_ref, xn_ref, dpb_ref = refs[4 + npart :]
        g = g_ref[...]
        u, r = _rms(x_ref[...])
        xn_ref[...] = bf(u * g)
        dxn = jnp.zeros((tm, d), F32)
        off = 0
        for p_ref, wd_ in zip(p_refs, widths):
            dp = bf(p_ref[...])
            dpb_ref[:, off : off + wd_] = dp
            dxn = dxn + (dot_nn(dp, w_ref[off : off + wd_, :]) if wt else dot_nt(dp, w_ref[:, off : off + wd_]))
            off += wd_
        dx, dg = _rms_bwd(dxn, u, r, g)
        dx_ref[...] = dxin_ref[...] + dx

        @pl.when(pl.program_id(0) == 0)
        def _():
            dg_ref[...] = jnp.zeros_like(dg_ref)

        dg_ref[...] += dg

    row = pl.BlockSpec((tm, d), lambda i: (i, 0))
    return pl.pallas_call(
        body,
        name="proj_bwd",
        grid=(s // tm,),
        in_specs=[row, pl.BlockSpec((None, 1, d), lambda i: (l, 0, 0)), pl.BlockSpec((None,) + w.shape[1:], lambda i: (wl, 0, 0)), row]
        + [pl.BlockSpec((tm, wd_), lambda i: (i, 0)) for wd_ in widths],
        out_specs=[row, _const((1, d)), row, pl.BlockSpec((tm, n), lambda i: (i, 0))],
        out_shape=[
            jax.ShapeDtypeStruct((s, d), F32),
            jax.ShapeDtypeStruct((1, d), F32),
            jax.ShapeDtypeStruct((s, d), BF16),
            jax.ShapeDtypeStruct((s, n), BF16),
        ],
        compiler_params=_cparams(("arbitrary",)),
    )(x, gain3, w, dx_in, *parts)


def mm_res(x, a, w, l):
    s, d = x.shape
    k = a.shape[1]
    tm = _tok(s)

    def body(x_ref, a_ref, w_ref, o_ref):
        o_ref[...] = x_ref[...] + dot_nn(a_ref[...], w_ref[...])

    return pl.pallas_call(
        body,
        name="mm_res",
        grid=(s // tm,),
        in_specs=[
            pl.BlockSpec((tm, d), lambda i: (i, 0)),
            pl.BlockSpec((tm, k), lambda i: (i, 0)),
            pl.BlockSpec((None, k, d), lambda i: (l, 0, 0)),
        ],
        out_specs=pl.BlockSpec((tm, d), lambda i: (i, 0)),
        out_shape=jax.ShapeDtypeStruct((s, d), F32),
        compiler_params=_cparams(("arbitrary",)),
    )(x, a, w)


def mm_nt(dx, w, l):
    s, d = dx.shape
    k = w.shape[1]
    tm = _tok(s)

    def body(dx_ref, w_ref, o_ref, dxb_ref):
        dxb = bf(dx_ref[...])
        dxb_ref[...] = dxb
        o_ref[...] = dot_nt(dxb, w_ref[...])

    return pl.pallas_call(
        body,
        name="mm_nt",
        grid=(s // tm,),
        in_specs=[pl.BlockSpec((tm, d), lambda i: (i, 0)), pl.BlockSpec((None, k, d), lambda i: (l, 0, 0))],
        out_specs=[pl.BlockSpec((tm, k), lambda i: (i, 0)), pl.BlockSpec((tm, d), lambda i: (i, 0))],
        out_shape=[jax.ShapeDtypeStruct((s, k), F32), jax.ShapeDtypeStruct((s, d), BF16)],
        compiler_params=_cparams(("arbitrary",)),
    )(dx, w)


def lb_fwd(logits3):
    def body(l_ref, o_ref):
        l0, l1 = l_ref[0], l_ref[1]
        m = jnp.maximum(l0, l1)
        e0, e1 = jnp.exp(l0 - m), jnp.exp(l1 - m)
        p0, p1 = e0 / (e0 + e1), e1 / (e0 + e1)
        o_ref[0] = p0 - p0
        o_ref[1] = (p0 + p1) - p0

    return pl.pallas_call(body, name="lb_fwd", out_shape=jax.ShapeDtypeStruct(logits3.shape, F32))(logits3)


def lb_bwd(logits3, dlb1):
    def body(l_ref, d_ref, o_ref):
        l0, l1 = l_ref[0], l_ref[1]
        m = jnp.maximum(l0, l1)
        e0, e1 = jnp.exp(l0 - m), jnp.exp(l1 - m)
        p0, p1 = e0 / (e0 + e1), e1 / (e0 + e1)
        t = d_ref[...] * p0 * p1
        o_ref[0] = -t
        o_ref[1] = t

    return pl.pallas_call(body, name="lb_bwd", out_shape=jax.ShapeDtypeStruct(logits3.shape, F32))(logits3, dlb1)


def _hgrn_gates(zq, zf, lb):
    sf = _sigmoid(zf)
    f = lb + (1.0 - lb) * sf
    sq = _sigmoid(zq)
    return sf, f, jnp.log(f), 1.0 - f, sq, zq * sq


def _tri(n, upper=False):
    r = lax.broadcasted_iota(jnp.int32, (n, n), 0)
    c = lax.broadcasted_iota(jnp.int32, (n, n), 1)
    return jnp.where((c >= r) if upper else (r >= c), 1.0, 0.0).astype(F32)


def hgrn_fwd(proj, lb3, og3, l, comm=None):
    s = proj.shape[0]
    nh = 6
    n_chunk = s // CHUNK
    nsub = CHUNK // SUB

    hb = HGRN_HEADS_PER_STEP
    wide = hb * LANES

    def body(zq_ref, zf_ref, vi_ref, zg_ref, lb_ref, og_ref, main_ref, o_ref, q_a, k_a, v_a, c_a):
        og = og_ref[...]
        tril = _tri(CHUNK)
        rowi = lax.broadcasted_iota(jnp.int32, (SUB, LANES), 0)

        def one_head(hd, rows, st):
            cols = slice(hd * LANES, (hd + 1) * LANES)
            q_s, k_s, v_s, c_s = q_a.at[hd], k_a.at[hd], v_a.at[hd], c_a.at[hd]
            zg = zg_ref[rows, cols]
            _, _, lf, k, _, q = _hgrn_gates(zq_ref[rows, cols], zf_ref[rows, cols], lb_ref[:, cols])
            v = vi_ref[rows, cols]
            c = dot_nn(tril, lf, HI)
            q_s[...] = q
            k_s[...] = k
            v_s[...] = v
            c_s[...] = c
            o_inter = dot_nt(q * jnp.exp(c), st, HI)
            parts = []
            for i in range(nsub):
                lo = i * SUB
                blk = pl.ds(lo, SUB)
                qb, cb = q_s[blk, :], c_s[blk, :]
                ob = o_inter[lo : lo + SUB]
                if i > 0:
                    rr = c_s[pl.ds(lo - 1, 1), :]
                    qt = qb * jnp.exp(cb - rr)
                    kt = k_s[pl.ds(0, lo), :] * jnp.exp(rr - c_s[pl.ds(0, lo), :])
                    ob = ob + dot_nn(dot_nt(qt, kt, HI), v_s[pl.ds(0, lo), :], HI)
                for t in range(SUB):
                    e = jnp.where(rowi >= t, jnp.exp(cb - c_s[pl.ds(lo + t, 1), :]), 0.0)
                    a = jnp.sum(qb * k_s[pl.ds(lo + t, 1), :] * e, axis=-1, keepdims=True)
                    ob = ob + a * v_s[pl.ds(lo + t, 1), :]
                parts.append(ob)
            o = jnp.concatenate(parts, axis=0)
            ce = c_s[pl.ds(CHUNK - 1, 1), :]
            st = st * jnp.exp(ce) + dot_tn(v, k * jnp.exp(ce - c), HI)
            on, _ = _rms(o)
            o_ref[rows, cols] = o
            main_ref[rows, cols] = bf(on * og * (zg * _sigmoid(zg)))
            return st

        def chunk(ci, sts):
            rows = pl.ds(pl.multiple_of(ci * CHUNK, CHUNK), CHUNK)
            return tuple(one_head(hd, rows, sts[hd]) for hd in range(hb))

        lax.fori_loop(0, n_chunk, chunk, tuple(jnp.zeros((LANES, LANES), F32) for _ in range(hb)))

    def col(k):
        return pl.BlockSpec((s, wide), lambda h: (0, k * (nh // hb) + h))

    vec = pl.BlockSpec((None, 1, wide), lambda h: (l, 0, h))
    return _carry(
        body,
        comm,
        name="hgrn_fwd",
        grid=(nh // hb,),
        in_specs=[col(0), col(1), col(2), col(3), vec, pl.BlockSpec((None, 1, LANES), lambda h: (l, 0, 0))],
        out_specs=[pl.BlockSpec((s, wide), lambda h: (0, h))] * 2,
        out_shape=[jax.ShapeDtypeStruct((s, nh * LANES), BF16), jax.ShapeDtypeStruct((s, nh * LANES), F32)],
        scratch_shapes=[pltpu.VMEM((hb, CHUNK, LANES), F32)] * 4,
        args=(proj, proj, proj, proj, lb3, og3),
    )


def hgrn_bwd(proj, lb3, og3, l, o, dmixed, comm=None):
    s = proj.shape[0]
    nh = 6
    n_chunk = s // CHUNK
    nsub = CHUNK // SUB

    hb = HGRN_HEADS_PER_STEP
    wide = hb * LANES

    def body(zq_ref, zf_ref, vi_ref, zg_ref, lb_ref, og_ref, o_ref, dm_ref,
             dzq_ref, dzf_ref, dvi_ref, dzg_ref, dlb_ref, dog_ref,
             st_a, q_a, k_a, v_a, c_a, do_a, dq_a, dk_a, dv_a, acc_a):
        og = og_ref[...]
        tril = _tri(CHUNK)
        triu = _tri(CHUNK, upper=True)
        rowi = lax.broadcasted_iota(jnp.int32, (SUB, LANES), 0)

        def fwd_head(hd, ci, rows, st):
            cols = slice(hd * LANES, (hd + 1) * LANES)
            _, _, lf, k, _, _ = _hgrn_gates(zq_ref[rows, cols], zf_ref[rows, cols], lb_ref[:, cols])
            c = dot_nn(tril, lf, HI)
            ce = jnp.sum(lf, axis=0, keepdims=True)
            st_a[hd, ci] = st
            return st * jnp.exp(ce) + dot_tn(vi_ref[rows, cols], k * jnp.exp(ce - c), HI)

        def fwd_chunk(ci, sts):
            rows = pl.ds(pl.multiple_of(ci * CHUNK, CHUNK), CHUNK)
            return tuple(fwd_head(hd, ci, rows, sts[hd]) for hd in range(hb))

        lax.fori_loop(0, n_chunk, fwd_chunk, tuple(jnp.zeros((LANES, LANES), F32) for _ in range(hb)))
        acc_a[...] = jnp.zeros_like(acc_a)

        def bwd_head(hd, ci, rows, carry):
            dst, cg = carry
            cols = slice(hd * LANES, (hd + 1) * LANES)
            q_s, k_s, v_s, c_s, do_s = q_a.at[hd], k_a.at[hd], v_a.at[hd], c_a.at[hd], do_a.at[hd]
            dq_s, dk_s, dv_s, acc_s = dq_a.at[hd], dk_a.at[hd], dv_a.at[hd], acc_a.at[hd]
            lb = lb_ref[:, cols]
            zq, zf, zg = zq_ref[rows, cols], zf_ref[rows, cols], zg_ref[rows, cols]
            sf, f, lf, k, sq, q = _hgrn_gates(zq, zf, lb)
            v = vi_ref[rows, cols]
            c = dot_nn(tril, lf, HI)
            st = st_a[hd, ci]
            on, r = _rms(o_ref[rows, cols])
            sg = _sigmoid(zg)
            dmain = dm_ref[rows, cols]
            dy = dmain * (zg * sg)
            dzg_ref[rows, cols] = dmain * (on * og) * _dsilu(zg, sg)
            do, dog = _rms_bwd(dy, on, r, og)
            acc_s[pl.ds(0, 1), :] += dog
            q_s[...] = q
            k_s[...] = k
            v_s[...] = v
            c_s[...] = c
            do_s[...] = do
            ce = c_s[pl.ds(CHUNK - 1, 1), :]
            eq = jnp.exp(c)
            ek = jnp.exp(ce - c)
            qt_all = q * eq
            dq_s[...] = dot_nn(do, st, HI) * eq
            dv_s[...] = dot_nt(k * ek, dst, HI)
            dk_s[...] = dot_nn(v, dst, HI) * ek
            dst = dst * jnp.exp(ce) + dot_tn(do, qt_all, HI)
            for i in range(nsub):
                lo = i * SUB
                blk = pl.ds(lo, SUB)
                qb, cb, dob = q_s[blk, :], c_s[blk, :], do_s[blk, :]
                if i > 0:
                    prev = pl.ds(0, lo)
                    rr = c_s[pl.ds(lo - 1, 1), :]
                    eqi = jnp.exp(cb - rr)
                    eki = jnp.exp(rr - c_s[prev, :])
                    qt = qb * eqi
                    kt = k_s[prev, :] * eki
                    amat = dot_nt(qt, kt, HI)
                    damat = dot_nt(dob, v_s[prev, :], HI)
                    dv_s[prev, :] += dot_tn(amat, dob, HI)
                    dq_s[blk, :] += dot_nn(damat, kt, HI) * eqi
                    dk_s[prev, :] += dot_tn(damat, qt, HI) * eki
                dqb = jnp.zeros((SUB, LANES), F32)
                for t in range(SUB):
                    row = pl.ds(lo + t, 1)
                    e = jnp.where(rowi >= t, jnp.exp(cb - c_s[row, :]), 0.0)
                    kr = k_s[row, :]
                    a = jnp.sum(qb * kr * e, axis=-1, keepdims=True)
                    da = jnp.sum(dob * v_s[row, :], axis=-1, keepdims=True)
                    dv_s[row, :] += jnp.sum(a * dob, axis=0, keepdims=True)
                    dqb = dqb + da * kr * e
                    dk_s[row, :] += jnp.sum(da * qb * e, axis=0, keepdims=True)
                dq_s[blk, :] += dqb
            dq, dk = dq_s[...], dk_s[...]
            dg = q * dq - k * dk
            dlf = dot_nn(triu, dg, HI) + cg
            cg = cg + jnp.sum(dg, axis=0, keepdims=True)
            df = dlf / f - dk
            dzf_ref[rows, cols] = df * (1.0 - lb) * sf * (1.0 - sf)
            acc_s[pl.ds(1, 1), :] += jnp.sum(df * (1.0 - sf), axis=0, keepdims=True)
            dzq_ref[rows, cols] = dq * _dsilu(zq, sq)
            dvi_ref[rows, cols] = dv_s[...]
            return dst, cg

        def bwd_chunk(jj, carries):
            ci = n_chunk - 1 - jj
            rows = pl.ds(pl.multiple_of(ci * CHUNK, CHUNK), CHUNK)
            return tuple(bwd_head(hd, ci, rows, carries[hd]) for hd in range(hb))

        zero = (jnp.zeros((LANES, LANES), F32), jnp.zeros((1, LANES), F32))
        lax.fori_loop(0, n_chunk, bwd_chunk, tuple(zero for _ in range(hb)))

        @pl.when(pl.program_id(0) == 0)
        def _():
            dog_ref[...] = jnp.zeros_like(dog_ref)

        for hd in range(hb):
            dlb_ref[:, hd * LANES : (hd + 1) * LANES] = acc_a[hd, pl.ds(1, 1), :]
            dog_ref[...] += acc_a[hd, pl.ds(0, 1), :]

    def col(k):
        return pl.BlockSpec((s, wide), lambda h: (0, k * (nh // hb) + h), pipeline_mode=pl.Buffered(1))

    head_in = pl.BlockSpec((s, wide), lambda h: (0, h), pipeline_mode=pl.Buffered(1))
    head = pl.BlockSpec((s, wide), lambda h: (0, h))
    vec = pl.BlockSpec((None, 1, wide), lambda h: (l, 0, h))
    ck = pltpu.VMEM((hb, CHUNK, LANES), F32)
    return _carry(
        body,
        comm,
        name="hgrn_bwd",
        grid=(nh // hb,),
        in_specs=[col(0), col(1), col(2), col(3), vec, pl.BlockSpec((None, 1, LANES), lambda h: (l, 0, 0)), head_in, head_in],
        out_specs=[head] * 4 + [pl.BlockSpec((1, wide), lambda h: (0, h)), _const((1, LANES))],
        out_shape=[jax.ShapeDtypeStruct((s, nh * LANES), F32)] * 4
        + [jax.ShapeDtypeStruct((1, nh * LANES), F32), jax.ShapeDtypeStruct((1, LANES), F32)],
        scratch_shapes=[pltpu.VMEM((hb, n_chunk, LANES, LANES), F32)] + [ck] * 8 + [pltpu.VMEM((hb, 8, LANES), F32)],
        args=(proj, proj, proj, proj, lb3, og3, o, dmixed),
    )


MEM_SCALE = HEAD64**-0.5


def _mem_heads(qraw, kvm, qg, kg, pr, m0):
    lo = pr * LANES
    uq, rq = _rms64(qraw[:, lo : lo + LANES], m0)
    uk, rk = _rms64(kvm[:, lo : lo + LANES], m0)
    v = bf(kvm[:, 2 * LANES + lo : 3 * LANES + lo])
    return uq, rq, uk, rk, v, uq * qg, bf(uk * kg)


def memattn_fwd(proj, qblk, kvm, qg3, kg3, l):
    s = proj.shape[0]
    nm = kvm.shape[0]
    tm = _tok(s)

    def body(q_ref, kv_ref, qg_ref, kg_ref, o_ref):
        m0 = _lane_mask0((1, LANES))
        qraw, kvv = q_ref[...], kv_ref[...]
        for pr in range(2):
            _, _, _, _, v, qn, kn = _mem_heads(qraw, kvv, qg_ref[...], kg_ref[...], pr, m0)
            out = jnp.zeros((tm, LANES), F32)
            for hh in range(2):
                mh = m0 if hh == 0 else jnp.logical_not(m0)
                sc = dot_nt(bf(jnp.where(mh, qn, 0.0)), kn) * MEM_SCALE
                p = jnp.exp(sc - jnp.max(sc, axis=-1, keepdims=True))
                p = p / jnp.sum(p, axis=-1, keepdims=True)
                out = jnp.where(mh, dot_nn(bf(p), v), out)
            o_ref[:, pr * LANES : (pr + 1) * LANES] = bf(out)

    gspec = pl.BlockSpec((None, 1, LANES), lambda i: (l, 0, 0))
    return pl.pallas_call(
        body,
        name="memattn_fwd",
        grid=(s // tm,),
        in_specs=[pl.BlockSpec((tm, 2 * LANES), lambda i: (i, qblk)), _const((nm, 4 * LANES)), gspec, gspec],
        out_specs=pl.BlockSpec((tm, 2 * LANES), lambda i: (i, 0)),
        out_shape=jax.ShapeDtypeStruct((s, 2 * LANES), BF16),
        compiler_params=_cparams(("arbitrary",)),
    )(proj, kvm, qg3, kg3)


def memattn_bwd(proj, qblk, kvm, qg3, kg3, l, dmixed):
    s = proj.shape[0]
    nm = kvm.shape[0]
    tm = _tok(s)

    def body(q_ref, kv_ref, qg_ref, kg_ref, dm_ref, dq_ref, dkv_ref, dqg_ref, dkg_ref):
        m0 = _lane_mask0((1, LANES))
        qraw, kvv = q_ref[...], kv_ref[...]
        qg, kg = qg_ref[...], kg_ref[...]

        @pl.when(pl.program_id(0) == 0)
        def _():
            dkv_ref[...] = jnp.zeros_like(dkv_ref)
            dqg_ref[...] = jnp.zeros_like(dqg_ref)
            dkg_ref[...] = jnp.zeros_like(dkg_ref)

        for pr in range(2):
            lo = pr * LANES
            uq, rq, uk, rk, v, qn, kn = _mem_heads(qraw, kvv, qg, kg, pr, m0)
            do = dm_ref[:, lo : lo + LANES]
            dqn = jnp.zeros((tm, LANES), F32)
            dkn = jnp.zeros((nm, LANES), F32)
            dv = jnp.zeros((nm, LANES), F32)
            for hh in range(2):
                mh = m0 if hh == 0 else jnp.logical_not(m0)
                qh = bf(jnp.where(mh, qn, 0.0))
                doh = bf(jnp.where(mh, do, 0.0))
                sc = dot_nt(qh, kn) * MEM_SCALE
                p = jnp.exp(sc - jnp.max(sc, axis=-1, keepdims=True))
                p = p / jnp.sum(p, axis=-1, keepdims=True)
                dp = dot_nt(doh, v)
                ds = bf(p * (dp - jnp.sum(p * dp, axis=-1, keepdims=True)))
                dqn = dqn + jnp.where(mh, dot_nn(ds, kn), 0.0) * MEM_SCALE
                dkn = dkn + dot_tn(ds, qh) * MEM_SCALE
                dv = dv + dot_tn(bf(p), doh)
            dqr, dqg = _rms64_bwd(dqn, uq, rq, qg, m0)
            dkr, dkg = _rms64_bwd(dkn, uk, rk, kg, m0)
            dq_ref[:, lo : lo + LANES] = dqr
            dkv_ref[:, lo : lo + LANES] += dkr
            dkv_ref[:, 2 * LANES + lo : 3 * LANES + lo] += dv
            dqg_ref[...] += dqg
            dkg_ref[...] += dkg

    gspec = pl.BlockSpec((None, 1, LANES), lambda i: (l, 0, 0))
    return pl.pallas_call(
        body,
        name="memattn_bwd",
        grid=(s // tm,),
        in_specs=[
            pl.BlockSpec((tm, 2 * LANES), lambda i: (i, qblk)),
            _const((nm, 4 * LANES)),
            gspec,
            gspec,
            pl.BlockSpec((tm, 2 * LANES), lambda i: (i, 3)),
        ],
        out_specs=[pl.BlockSpec((tm, 2 * LANES), lambda i: (i, 0)), _const((nm, 4 * LANES)), _const((1, LANES)), _const((1, LANES))],
        out_shape=[
            jax.ShapeDtypeStruct((s, 2 * LANES), F32),
            jax.ShapeDtypeStruct((nm, 4 * LANES), F32),
            jax.ShapeDtypeStruct((1, LANES), F32),
            jax.ShapeDtypeStruct((1, LANES), F32),
        ],
        compiler_params=_cparams(("arbitrary",)),
    )(proj, kvm, qg3, kg3, dmixed)


KV_MAIN = 768


def _log_sigmoid(z):
    return jnp.minimum(z, 0.0) - jnp.log(1.0 + jnp.exp(-jnp.abs(z)))


def kvprep_fwd(kvf, kg, fb):
    s = kvf.shape[0]
    tm = _tok(s)

    def body(kvf_ref, kg_ref, fb_ref, k_ref, v_ref, clf_ref, carry):
        m0 = _lane_mask0((1, LANES))

        @pl.when(pl.program_id(0) == 0)
        def _():
            carry[...] = jnp.zeros_like(carry)

        for j in range(KV_MAIN // LANES):
            u, _ = _rms64(kvf_ref[:, j * LANES : (j + 1) * LANES], m0)
            k_ref[:, j * LANES : (j + 1) * LANES] = bf(u * kg_ref[...])
        v_ref[...] = bf(kvf_ref[:, KV_MAIN : 2 * KV_MAIN])
        lf = _log_sigmoid(kvf_ref[:, 2 * KV_MAIN :] + fb_ref[...])
        clf_ref[...] = dot_nn(_tri(tm), lf, HI) + carry[...]
        carry[...] += jnp.sum(lf, axis=0, keepdims=True)

    n = kvf.shape[1]
    return pl.pallas_call(
        body,
        name="kvprep_fwd",
        grid=(s // tm,),
        in_specs=[pl.BlockSpec((tm, n), lambda i: (i, 0)), _const((1, LANES)), _const((1, LANES))],
        out_specs=[pl.BlockSpec((tm, KV_MAIN), lambda i: (i, 0))] * 2 + [pl.BlockSpec((tm, LANES), lambda i: (i, 0))],
        out_shape=[jax.ShapeDtypeStruct((s, KV_MAIN), BF16)] * 2 + [jax.ShapeDtypeStruct((s, LANES), F32)],
        scratch_shapes=[pltpu.VMEM((1, LANES), F32)],
        compiler_params=_cparams(("arbitrary",)),
    )(kvf, kg, fb)


def kvprep_bwd(kvf, kg, fb, dk, dv, dclf):
    s, n = kvf.shape
    tm = _tok(s)
    nb = s // tm

    def body(kvf_ref, kg_ref, fb_ref, dk_ref, dv_ref, dclf_ref, o_ref, dkg_ref, dfb_ref, carry):
        m0 = _lane_mask0((1, LANES))

        @pl.when(pl.program_id(0) == 0)
        def _():
            carry[...] = jnp.zeros_like(carry)
            dkg_ref[...] = jnp.zeros_like(dkg_ref)
            dfb_ref[...] = jnp.zeros_like(dfb_ref)

        kg_ = kg_ref[...]
        for j in range(KV_MAIN // LANES):
            cols = slice(j * LANES, (j + 1) * LANES)
            u, r = _rms64(kvf_ref[:, cols], m0)
            dkr, dkg = _rms64_bwd(dk_ref[:, cols], u, r, kg_, m0)
            o_ref[:, cols] = dkr
            dkg_ref[...] += dkg
        o_ref[:, KV_MAIN : 2 * KV_MAIN] = dv_ref[...]
        z = kvf_ref[:, 2 * KV_MAIN :] + fb_ref[...]
        dc = dclf_ref[...]
        dlf = dot_nn(_tri(tm, upper=True), dc, HI) + carry[...]
        carry[...] += jnp.sum(dc, axis=0, keepdims=True)
        dz = dlf * _sigmoid(-z)
        o_ref[:, 2 * KV_MAIN :] = dz
        dfb_ref[...] += jnp.sum(dz, axis=0, keepdims=True)

    rev = lambda i: (nb - 1 - i, 0)
    return pl.pallas_call(
        body,
        name="kvprep_bwd",
        grid=(nb,),
        in_specs=[pl.BlockSpec((tm, n), rev), _const((1, LANES)), _const((1, LANES)), pl.BlockSpec((tm, KV_MAIN), rev),
                  pl.BlockSpec((tm, KV_MAIN), rev), pl.BlockSpec((tm, LANES), rev)],
        out_specs=[pl.BlockSpec((tm, n), rev), _const((1, LANES)), _const((1, LANES))],
        out_shape=[jax.ShapeDtypeStruct((s, n), F32), jax.ShapeDtypeStruct((1, LANES), F32), jax.ShapeDtypeStruct((1, LANES), F32)],
        scratch_shapes=[pltpu.VMEM((1, LANES), F32)],
        compiler_params=_cparams(("arbitrary",)),
    )(kvf, kg, fb, dk, dv, dclf)


FOX_SCALE = HEAD64**-0.5


def _lane_col(block, lane_idx, h):
    return jnp.sum(jnp.where(lane_idx == h, block, 0.0), axis=-1, keepdims=True)


def _causal(tq, ext, i, transposed=False):
    if transposed:
        key = lax.broadcasted_iota(jnp.int32, (ext, tq), 0)
        qry = lax.broadcasted_iota(jnp.int32, (ext, tq), 1) + i * tq
    else:
        qry = lax.broadcasted_iota(jnp.int32, (tq, ext), 0) + i * tq
        key = lax.broadcasted_iota(jnp.int32, (tq, ext), 1)
    return key <= qry


def fox_fwd(proj, k_sh, v_sh, clf, clf_t, qg3, j_layer, comm=None):
    s = proj.shape[0]
    npair = 6
    tq = TQ if s % TQ == 0 else s
    nq = s // tq

    def body(q_ref, gate_ref, k_ref, v_ref, clf_ref, clft_ref, qg_ref, main_ref, o_ref, lse_ref):
        j = pl.program_id(0)
        lane = lax.broadcasted_iota(jnp.int32, (1, LANES), 1)
        m0 = lane < HEAD64
        u, _ = _rms64(q_ref[...], m0)
        qn = u * qg_ref[...] * FOX_SCALE
        clfv = clf_ref[...]
        for hh in range(2):
            h = 2 * j + hh
            mh = m0 if hh == 0 else jnp.logical_not(m0)
            qh = bf(jnp.where(mh, qn, 0.0))
            dcol = _lane_col(clfv, lane, h)
            drow = clft_ref[pl.ds(h, 1), :]
            for i in range(nq):
                rows = slice(i * tq, (i + 1) * tq)
                ext = (i + 1) * tq
                sc = dot_nt(qh[rows], k_ref[0:ext, :]) + dcol[rows] - drow[:, :ext]
                sc = jnp.where(_causal(tq, ext, i), sc, -jnp.inf)
                m = jnp.max(sc, axis=-1, keepdims=True)
                p = jnp.exp(sc - m)
                lsum = jnp.sum(p, axis=-1, keepdims=True)
                pv = dot_nn(bf(p), v_ref[0:ext, :]) / lsum
                lse = m + jnp.log(lsum)
                if hh == 0:
                    o_ref[rows, :] = pv
                    lse_ref[rows, :] = jnp.where(lane == 0, lse, 0.0)
                else:
                    o_ref[rows, :] = jnp.where(mh, pv, o_ref[rows, :])
                    lse_ref[rows, :] = jnp.where(lane == 1, lse, lse_ref[rows, :])
        main_ref[...] = bf(o_ref[...] * _sigmoid(gate_ref[...]))

    blk = lambda off: pl.BlockSpec((s, LANES), lambda j: (0, off + j))
    return _carry(
        body,
        comm,
        name="fox_fwd",
        grid=(npair,),
        in_specs=[blk(0), blk(npair), blk(0), blk(0), _const((s, LANES)), _const((16, s)),
                  pl.BlockSpec((None, 1, LANES), lambda j: (j_layer, 0, 0))],
        out_specs=[blk(0)] * 3,
        out_shape=[jax.ShapeDtypeStruct((s, npair * LANES), BF16)] + [jax.ShapeDtypeStruct((s, npair * LANES), F32)] * 2,
        scratch_shapes=[],
        args=(proj, proj, k_sh, v_sh, clf, clf_t, qg3),
    )


def fox_bwd(proj, k_sh, v_sh, clf, clf_t, qg3, j_layer, o, lse, lse_t, dmixed, dk_in, dv_in, dclf_in):
    s = proj.shape[0]
    npair = 6
    tq = TQ if s % TQ == 0 else s
    nq = s // tq

    def body(q_ref, gate_ref, k_ref, v_ref, clf_ref, clft_ref, qg_ref, o_ref, lse_ref, lset_ref, dm_ref, dkin_ref, dvin_ref, dclfin_ref,
             dq_ref, dgate_ref, dk_ref, dv_ref, dclf_ref, dqg_ref, dqn_s, dcl_s):
        j = pl.program_id(0)
        lane = lax.broadcasted_iota(jnp.int32, (1, LANES), 1)
        m0 = lane < HEAD64
        qg = qg_ref[...]
        u, r = _rms64(q_ref[...], m0)
        qn = u * qg * FOX_SCALE
        ov = o_ref[...]
        gate = gate_ref[...]
        sg = _sigmoid(gate)
        dmain = dm_ref[...]
        do = dmain * sg
        dgate_ref[...] = dmain * ov * sg * (1.0 - sg)
        dk_ref[...] = dkin_ref[...]
        dv_ref[...] = dvin_ref[...]
        clfv = clf_ref[...]
        lsev = lse_ref[...]
        ones8 = jnp.ones((8, LANES), F32)

        @pl.when(j == 0)
        def _():
            dclf_ref[...] = dclfin_ref[...]
            dqg_ref[...] = jnp.zeros_like(dqg_ref)

        for hh in range(2):
            h = 2 * j + hh
            mh = m0 if hh == 0 else jnp.logical_not(m0)
            qh = bf(jnp.where(mh, qn, 0.0))
            doh = jnp.where(mh, do, 0.0)
            dohb = bf(doh)
            doo = doh * ov
            dcol = _lane_col(clfv, lane, h)
            drow = clft_ref[pl.ds(h, 1), :]
            lcol = _lane_col(lsev, lane, hh)
            lrow = lset_ref[pl.ds(h, 1), :]
            delta = jnp.sum(doo, axis=-1, keepdims=True)
            dcl_s[...] = jnp.zeros_like(dcl_s)
            for i in range(nq):
                rows = slice(i * tq, (i + 1) * tq)
                ext = (i + 1) * tq
                kk, vv = k_ref[0:ext, :], v_ref[0:ext, :]
                sc = dot_nt(qh[rows], kk) + dcol[rows] - drow[:, :ext]
                p = jnp.where(_causal(tq, ext, i), jnp.exp(sc - lcol[rows]), 0.0)
                ds = p * (dot_nt(dohb[rows], vv) - delta[rows])
                dqh = dot_nn(bf(ds), kk) * FOX_SCALE
                if hh == 0:
                    dqn_s[rows, :] = dqh
                else:
                    dqn_s[rows, :] = jnp.where(mh, dqh, dqn_s[rows, :])
                dcl_s[rows, :] += jnp.sum(ds, axis=-1, keepdims=True)
                sct = dot_nt(kk, qh[rows]) + drow[:, rows] - dcol[:ext]
                pt = jnp.where(_causal(tq, ext, i, transposed=True), jnp.exp(sct - lrow[:, rows]), 0.0)
                delta_row = dot_nt(ones8, doo[rows], HI)[0:1]
                dst = pt * (dot_nt(vv, dohb[rows]) - delta_row)
                dv_ref[0:ext, :] += dot_nn(bf(pt), dohb[rows])
                dk_ref[0:ext, :] += dot_nn(bf(dst), qh[rows])
                dcl_s[0:ext, :] -= jnp.sum(dst, axis=-1, keepdims=True)
            dclf_ref[...] += jnp.where(lane == h, dcl_s[...], 0.0)
        dqr, dqg = _rms64_bwd(dqn_s[...], u, r, qg, m0)
        dq_ref[...] = dqr
        dqg_ref[...] += dqg

    blk = lambda off: pl.BlockSpec((s, LANES), lambda j: (0, off + j))
    full = _const((s, LANES))
    return pl.pallas_call(
        body,
        name="fox_bwd",
        grid=(npair,),
        in_specs=[blk(0), blk(npair), blk(0), blk(0), full, _const((16, s)), pl.BlockSpec((None, 1, LANES), lambda j: (j_layer, 0, 0)),
                  blk(0), blk(0), _const((16, s)), blk(0), blk(0), blk(0), full],
        out_specs=[blk(0)] * 4 + [full, _const((1, LANES))],
        out_shape=[jax.ShapeDtypeStruct((s, npair * LANES), F32)] * 4
        + [jax.ShapeDtypeStruct((s, LANES), F32), jax.ShapeDtypeStruct((1, LANES), F32)],
        scratch_shapes=[pltpu.VMEM((s, LANES), F32), pltpu.VMEM((s, LANES), F32)],
        compiler_params=_cparams(("arbitrary",)),
    )(proj, proj, k_sh, v_sh, clf, clf_t, qg3, o, lse, lse_t, dmixed, dk_in, dv_in, dclf_in)


def loss_head(y, target):
    s, d = y.shape
    tm = _tok(s)

    def body(y_ref, t_ref, loss_ref, dy_ref):
        err = y_ref[...] - t_ref[...]
        dy_ref[...] = err * (1.0 / d)

        @pl.when(pl.program_id(0) == 0)
        def _():
            loss_ref[...] = jnp.zeros_like(loss_ref)

        part = jnp.sum(jnp.mean(err * err, axis=-1, keepdims=True), axis=0, keepdims=True)
        loss_ref[...] += 0.5 * part

    row = pl.BlockSpec((tm, d), lambda i: (i, 0))
    return pl.pallas_call(
        body,
        name="loss_head",
        grid=(s // tm,),
        in_specs=[row, row],
        out_specs=[_const((1, 1)), row],
        out_shape=[jax.ShapeDtypeStruct((1, 1), F32), jax.ShapeDtypeStruct((s, d), F32)],
        compiler_params=_cparams(("arbitrary",)),
    )(y, target)


def _row_tile(r, c, n_arrays):
    budget = VMEM_LIMIT_BYTES // 2
    padded_c = -(-c // LANES) * LANES
    best = None
    for t in range(8, r + 1, 8):
        if r % t == 0 and 2 * n_arrays * t * padded_c * 4 <= budget:
            best = t
    return r if best is None else best


def _as2d(a):
    return a.reshape(-1, a.shape[-1]) if a.ndim >= 2 else a.reshape(1, -1)


def adamw(w, gs, m, v):
    shape = w.shape
    w2, m2, v2 = (_as2d(t) for t in (w, m, v))
    rows, c = w2.shape
    gs = [g.reshape(-1, c) for g in gs]
    assert sum(g.shape[0] for g in gs) == rows
    tr = _row_tile(min(g.shape[0] for g in gs), c, 8)
    assert all(g.shape[0] % tr == 0 for g in gs)
    c1 = 1.0 - ADAM_B1**ADAM_STEP
    c2 = 1.0 - ADAM_B2**ADAM_STEP
    outs = None
    first = 0
    for g in gs:
        n_prev = 0 if outs is None else 4
        r = g.shape[0]

        def body(w_ref, g_ref, m_ref, v_ref, *rest, n_prev=n_prev):
            go_ref, d_ref, nm_ref, nv_ref = rest[n_prev:]
            gv = g_ref[...]
            nm = ADAM_B1 * m_ref[...] + (1.0 - ADAM_B1) * gv
            nv = ADAM_B2 * v_ref[...] + (1.0 - ADAM_B2) * (gv * gv)
            go_ref[...] = gv
            nm_ref[...] = nm
            nv_ref[...] = nv
            d_ref[...] = -ADAM_LR * ((nm / c1) / (jnp.sqrt(nv / c2) + ADAM_EPS) + ADAM_WD * w_ref[...])

        spec = pl.BlockSpec((tr, c), lambda i, b0=first // tr: (b0 + i, 0))
        outs = pl.pallas_call(
            body,
            name="adamw",
            grid=(r // tr,),
            in_specs=[spec, pl.BlockSpec((tr, c), lambda i: (i, 0)), spec, spec] + [ANY] * n_prev,
            out_specs=[spec] * 4,
            out_shape=[jax.ShapeDtypeStruct((rows, c), F32)] * 4,
            input_output_aliases={4 + i: i for i in range(n_prev)},
            compiler_params=_cparams(("arbitrary",)),
        )(w2, g, m2, v2, *([] if outs is None else outs))
        first += r
    return tuple(t.reshape(shape) for t in outs)


def pair_sum(g, recv, c_arr):
    _, k, r, c = g.shape
    tr = _row_tile(r, c, 3)

    def body(c_ref, g_ref, r_ref, o_ref):
        o_ref[...] = bf(g_ref[...] + r_ref[...])

    return pl.pallas_call(
        body,
        name="pair_sum",
        grid_spec=pltpu.PrefetchScalarGridSpec(
            num_scalar_prefetch=1,
            grid=(k, r // tr),
            in_specs=[pl.BlockSpec((None, None, tr, c), lambda kk, i, cr: (cr[0], kk, i, 0)), pl.BlockSpec((None, tr, c), lambda kk, i, cr: (kk, i, 0))],
            out_specs=pl.BlockSpec((None, tr, c), lambda kk, i, cr: (kk, i, 0)),
        ),
        out_shape=jax.ShapeDtypeStruct((k, r, c), BF16),
        compiler_params=_cparams(("arbitrary", "arbitrary")),
    )(c_arr, g, recv)


def chip_sum(p, q, sel):
    _, r, c = p.shape
    tr = _row_tile(r, c, 4)

    def body(sel_ref, p_ref, q_ref, o_ref):
        acc = p_ref[...].astype(F32)
        for i in range(q.shape[0]):
            acc = acc + q_ref[i].astype(F32)
        o_ref[...] = acc

    return pl.pallas_call(
        body,
        name="chip_sum",
        grid_spec=pltpu.PrefetchScalarGridSpec(
            num_scalar_prefetch=1,
            grid=(r // tr,),
            in_specs=[pl.BlockSpec((None, tr, c), lambda i, sr: (sr[0], i, 0)), pl.BlockSpec((q.shape[0], tr, c), lambda i, sr: (0, i, 0))],
            out_specs=pl.BlockSpec((None, tr, c), lambda i, sr: (sr[1], i, 0)),
        ),
        out_shape=jax.ShapeDtypeStruct((2, r, c), F32),
        compiler_params=_cparams(("arbitrary",)),
    )(sel, p, q)


def cast_into_slot(w4, g, sel, dtype):
    _, _, r, c = w4.shape
    tr = _row_tile(r, c, 2)

    def body(sel_ref, w_ref, o_ref):
        o_ref[...] = w_ref[...].astype(dtype)

    return pl.pallas_call(
        body,
        name="cast_into_slot",
        grid_spec=pltpu.PrefetchScalarGridSpec(
            num_scalar_prefetch=1,
            grid=(2, r // tr),
            in_specs=[pl.BlockSpec((None, None, tr, c), lambda hf, i, sr: (g, hf, i, 0))],
            out_specs=pl.BlockSpec((None, None, tr, c), lambda hf, i, sr: (sr[0], hf, i, 0)),
        ),
        out_shape=jax.ShapeDtypeStruct((N_CHIPS, 2, r, c), dtype),
        compiler_params=_cparams(("arbitrary", "arbitrary")),
    )(sel, w4)


def _place():
    x, y, c = lax.axis_index("x"), lax.axis_index("y"), lax.axis_index("c")
    chips = [(1 - x, y), (x, 1 - y), (1 - x, 1 - y)]
    return x, y, c, 2 * x + y, chips, [2 * cx + cy for cx, cy in chips]


def _rcopy(src, dst, send, recv, dev):
    return pltpu.make_async_remote_copy(src_ref=src, dst_ref=dst, send_sem=send, recv_sem=recv, device_id=dev, device_id_type=MESH)


class Gather:
    def __init__(self, bufs):
        n = len(bufs)
        self.n = n
        self.args = list(bufs)
        self.out_shape = [jax.ShapeDtypeStruct(t.shape, t.dtype) for t in bufs]
        self.aliases = {a: a for a in range(n)}
        self.scratch = [pltpu.SemaphoreType.DMA((n, 6)), pltpu.SemaphoreType.DMA((n, 6))]

    def _sends(self, outs, send, recv):
        x, y, c, me, chips, _ = _place()
        cps = []
        for a in range(self.n):
            mine = outs[a].at[me, c]
            cps += [_rcopy(mine, mine, send.at[a, j], recv.at[a, j], (*chips[j], c)) for j in range(3)]
        return cps

    def start(self, ins, outs, scr):
        for cp in self._sends(outs, *scr):
            cp.start()

    def finish(self, ins, outs, scr):
        send, recv = scr
        x, y, c, me, chips, cidx = _place()
        sib = (x, y, 1 - c)
        passed = []
        for a in range(self.n):
            for j in range(3):
                landed = outs[a].at[cidx[j], c]
                _rcopy(landed, landed, send.at[a, j], recv.at[a, j], (*chips[j], c)).wait_recv()
                fwd = _rcopy(landed, landed, send.at[a, 3 + j], recv.at[a, 3 + j], sib)
                fwd.start()
                passed.append(fwd)
        for a in range(self.n):
            for j in range(3):
                theirs = outs[a].at[cidx[j], 1 - c]
                _rcopy(theirs, theirs, send.at[a, 3 + j], recv.at[a, 3 + j], sib).wait_recv()
        for cp in self._sends(outs, send, recv) + passed:
            cp.wait_send()


class PairExchange:
    def __init__(self, gs):
        n = len(gs)
        self.n = n
        self.args = list(gs)
        self.out_shape = [jax.ShapeDtypeStruct(t.shape[1:], t.dtype) for t in gs]
        self.aliases = {}
        self.scratch = [pltpu.SemaphoreType.DMA((n,)), pltpu.SemaphoreType.DMA((n,))]

    def _copies(self, ins, outs, send, recv):
        x, y, c = lax.axis_index("x"), lax.axis_index("y"), lax.axis_index("c")
        return [_rcopy(ins[a].at[1 - c], outs[a], send.at[a], recv.at[a], (x, y, 1 - c)) for a in range(self.n)]

    def start(self, ins, outs, scr):
        for cp in self._copies(ins, outs, *scr):
            cp.start()

    def finish(self, ins, outs, scr):
        for cp in self._copies(ins, outs, *scr):
            cp.wait()


class ChipExchange:
    def __init__(self, ps):
        n = len(ps)
        self.n = n
        self.args = list(ps)
        self.out_shape = [jax.ShapeDtypeStruct((3,) + t.shape[1:], t.dtype) for t in ps]
        self.aliases = {}
        self.scratch = [pltpu.SemaphoreType.DMA((n, 3)), pltpu.SemaphoreType.DMA((n, 3))]

    def _sends(self, ins, outs, send, recv):
        x, y, c, me, chips, cidx = _place()
        return [
            _rcopy(ins[a].at[cidx[j]], outs[a].at[j], send.at[a, j], recv.at[a, j], (*chips[j], c))
            for a in range(self.n)
            for j in range(3)
        ]

    def start(self, ins, outs, scr):
        for cp in self._sends(ins, outs, *scr):
            cp.start()

    def finish(self, ins, outs, scr):
        send, recv = scr
        x, y, c, me, chips, _ = _place()
        for a in range(self.n):
            for j in range(3):
                landed = outs[a].at[j]
                _rcopy(landed, landed, send.at[a, j], recv.at[a, j], (*chips[j], c)).wait_recv()
        for cp in self._sends(ins, outs, send, recv):
            cp.wait_send()


class PairShare:
    def __init__(self, bufs):
        n = len(bufs)
        self.n = n
        self.args = list(bufs)
        self.out_shape = [jax.ShapeDtypeStruct(t.shape, t.dtype) for t in bufs]
        self.aliases = {a: a for a in range(n)}
        self.scratch = [pltpu.SemaphoreType.DMA((n,)), pltpu.SemaphoreType.DMA((n,))]

    def _sends(self, outs, send, recv):
        x, y, c = lax.axis_index("x"), lax.axis_index("y"), lax.axis_index("c")
        return [_rcopy(outs[a].at[c], outs[a].at[c], send.at[a], recv.at[a], (x, y, 1 - c)) for a in range(self.n)]

    def start(self, ins, outs, scr):
        for cp in self._sends(outs, *scr):
            cp.start()

    def finish(self, ins, outs, scr):
        send, recv = scr
        x, y, c = lax.axis_index("x"), lax.axis_index("y"), lax.axis_index("c")
        for a in range(self.n):
            theirs = outs[a].at[1 - c]
            _rcopy(theirs, theirs, send.at[a], recv.at[a], (x, y, 1 - c)).wait_recv()
        for cp in self._sends(outs, send, recv):
            cp.wait_send()


class Multi:
    def __init__(self, comms):
        self.comms = comms
        self.args, self.out_shape, self.scratch, self.aliases = [], [], [], {}
        self.spans = []
        for cm in comms:
            a0, o0, s0 = len(self.args), len(self.out_shape), len(self.scratch)
            self.aliases.update({a0 + i: o0 + o for i, o in cm.aliases.items()})
            self.args += cm.args
            self.out_shape += cm.out_shape
            self.scratch += cm.scratch
            self.spans.append((slice(a0, len(self.args)), slice(o0, len(self.out_shape)), slice(s0, len(self.scratch))))

    def start(self, ins, outs, scr):
        for cm, (sa, so, ss) in zip(self.comms, self.spans):
            cm.start(ins[sa], outs[so], scr[ss])

    def finish(self, ins, outs, scr):
        for cm, (sa, so, ss) in zip(self.comms, self.spans):
            cm.finish(ins[sa], outs[so], scr[ss])

    def split(self, res):
        return [list(res[so]) for _, so, _ in self.spans]


def run_comm(comm, name):
    na, no = len(comm.args), len(comm.out_shape)

    def body(*refs):
        ins, outs, scr = refs[:na], refs[na : na + no], refs[na + no :]
        comm.start(ins, outs, scr)
        comm.finish(ins, outs, scr)

    return pl.pallas_call(
        body,
        name=name,
        in_specs=[ANY] * na,
        out_specs=[ANY] * no,
        out_shape=comm.out_shape,
        input_output_aliases=comm.aliases,
        scratch_shapes=comm.scratch,
    )(*comm.args)


def _carry(body, comm, *, name, grid, in_specs, out_specs, out_shape, scratch_shapes, args):
    params = _cparams(("arbitrary",))
    if comm is None:
        res = pl.pallas_call(body, name=name, grid=grid, in_specs=in_specs, out_specs=out_specs, out_shape=out_shape,
                             scratch_shapes=scratch_shapes, compiler_params=params)(*args)
        return res, None
    ni, no, ns = len(in_specs), len(out_specs), len(scratch_shapes)
    ci, co = len(comm.args), len(comm.out_shape)

    def wrapped(*refs):
        ins, c_ins = refs[:ni], refs[ni : ni + ci]
        p = ni + ci
        outs, c_outs = refs[p : p + no], refs[p + no : p + no + co]
        p += no + co
        scr, c_scr = refs[p : p + ns], refs[p + ns :]

        @pl.when(pl.program_id(0) == 0)
        def _():
            comm.start(c_ins, c_outs, c_scr)

        body(*ins, *outs, *scr)

        @pl.when(pl.program_id(0) == grid[0] - 1)
        def _():
            comm.finish(c_ins, c_outs, c_scr)

    res = pl.pallas_call(
        wrapped,
        name=name + "_carry",
        grid=grid,
        in_specs=list(in_specs) + [ANY] * ci,
        out_specs=list(out_specs) + [ANY] * co,
        out_shape=list(out_shape) + list(comm.out_shape),
        input_output_aliases={ni + i: no + o for i, o in comm.aliases.items()},
        scratch_shapes=list(scratch_shapes) + list(comm.scratch),
        compiler_params=params,
    )(*args, *comm.args)
    return res[:no], res[no:]


def small_allreduce(buf):
    r = buf.shape[0]

    def body(b_ref, o_ref, slots, send, recv):
        x, y, c = lax.axis_index("x"), lax.axis_index("y"), lax.axis_index("c")
        me = 4 * x + 2 * y + c
        slots[me] = b_ref[...]
        cps = []
        peers = []
        for mask in range(1, N_DEV):
            fx, fy, fc = (mask >> 2) & 1, (mask >> 1) & 1, mask & 1
            px, py, pc = (1 - x if fx else x), (1 - y if fy else y), (1 - c if fc else c)
            peers.append(4 * px + 2 * py + pc)
            cps.append(_rcopy(b_ref, slots.at[me], send.at[mask - 1], recv.at[mask - 1], (px, py, pc)))
        for cp in cps:
            cp.start()
        for k, pid in enumerate(peers):
            landed = slots.at[pid]
            _rcopy(landed, landed, send.at[k], recv.at[k], (x, y, c)).wait_recv()
        for cp in cps:
            cp.wait_send()
        acc = slots[0]
        for i in range(1, N_DEV):
            acc = acc + slots[i]
        o_ref[...] = acc

    vm = pl.BlockSpec(memory_space=pltpu.VMEM)
    return pl.pallas_call(
        body,
        name="small_allreduce",
        in_specs=[vm],
        out_specs=vm,
        out_shape=jax.ShapeDtypeStruct(buf.shape, F32),
        scratch_shapes=[pltpu.VMEM((N_DEV, r, LANES), F32), pltpu.SemaphoreType.DMA((N_DEV - 1,)), pltpu.SemaphoreType.DMA((N_DEV - 1,))],
    )(buf)


WEIGHT_NAMES = ["ffn1_norm", "ffn1_w_gate", "ffn1_w_up", "ffn1_w_down", "mix_norm", "mem_norm", "w_mem_kv", "mem_q_gain",
                "mem_k_gain", "w_in_a", "hgrn_lb_logits", "hgrn_o_gain", "w_in_b", "fox_q_gain", "kv_norm", "w_kv", "fox_f_bias",
                "fox_k_gain", "w_out", "ffn2_norm", "ffn2_w_gate", "ffn2_w_up", "ffn2_w_down"]
SHARDED = ["ffn1_w_gate", "ffn1_w_up", "ffn1_w_down", "w_mem_kv", "w_in_a", "w_in_b", "w_kv", "w_out", "ffn2_w_gate", "ffn2_w_up", "ffn2_w_down"]
SMALL = [n for n in WEIGHT_NAMES if n not in SHARDED]
FFN1 = ["ffn1_w_gate", "ffn1_w_up", "ffn1_w_down"]
FFN2 = ["ffn2_w_gate", "ffn2_w_up", "ffn2_w_down"]
PER_LAYER = FFN1 + FFN2 + ["w_mem_kv", "w_out"]
TRANSPOSED = ["ffn1_w_gate", "ffn1_w_up", "ffn2_w_gate", "ffn2_w_up", "w_in_a", "w_in_b"]
N_LAYERS, N_A = 4, 2
KV_PAD = 13 * LANES


def _halves(t):
    return t.reshape((2, t.shape[0] // 2) + t.shape[1:])


def _cols_from_chips(g):
    return jnp.moveaxis(g, 0, 2).reshape(g.shape[1], g.shape[2], N_CHIPS * g.shape[3])


def _rows_from_chips(g):
    return jnp.moveaxis(g, 0, 1).reshape(g.shape[1], N_CHIPS * g.shape[2], g.shape[3])


def _pair_tile(g):
    return jnp.tile(g, (1, 2)).reshape(g.shape[0], 1, LANES)


def _pair_fold(g):
    return g[:, :HEAD64] + g[:, HEAD64:]


def kernel(x, mem, ffn1_norm, ffn1_w_gate, ffn1_w_up, ffn1_w_down, mix_norm, mem_norm, w_mem_kv, mem_q_gain, mem_k_gain, w_in_a, hgrn_lb_logits, hgrn_o_gain, w_in_b, fox_q_gain, kv_norm, w_kv, fox_f_bias, fox_k_gain, w_out, ffn2_norm, ffn2_w_gate, ffn2_w_up, ffn2_w_down, loss_target, m_ffn1_norm, m_ffn1_w_gate, m_ffn1_w_up, m_ffn1_w_down, m_mix_norm, m_mem_norm, m_w_mem_kv, m_mem_q_gain, m_mem_k_gain, m_w_in_a, m_hgrn_lb_logits, m_hgrn_o_gain, m_w_in_b, m_fox_q_gain, m_kv_norm, m_w_kv, m_fox_f_bias, m_fox_k_gain, m_w_out, m_ffn2_norm, m_ffn2_w_gate, m_ffn2_w_up, m_ffn2_w_down, v_ffn1_norm, v_ffn1_w_gate, v_ffn1_w_up, v_ffn1_w_down, v_mix_norm, v_mem_norm, v_w_mem_kv, v_mem_q_gain, v_mem_k_gain, v_w_in_a, v_hgrn_lb_logits, v_hgrn_o_gain, v_w_in_b, v_fox_q_gain, v_kv_norm, v_w_kv, v_fox_f_bias, v_fox_k_gain, v_w_out, v_ffn2_norm, v_ffn2_w_gate, v_ffn2_w_up, v_ffn2_w_down):
    given = dict(locals())
    def oriented(n, t):
        return jnp.swapaxes(t, 1, 2) if n in TRANSPOSED else t

    w = {n: oriented(n, given[n]) for n in WEIGHT_NAMES}
    xs, mems, tgt = x[0], mem[0], loss_target[0]
    s, d = xs.shape
    my_chip = 2 * lax.axis_index("x") + lax.axis_index("y")
    sel = jnp.stack([my_chip, lax.axis_index("c")]).astype(jnp.int32)
    c_arr = sel[1:]

    def w_in_name(l):
        return "w_in_a" if l < N_A else "w_in_b"

    def cast(n, l):
        t = w[n]
        rows, cols = t.shape[-2:]
        own = 0 if t.ndim == 2 else (l - N_A if n == "w_in_b" else l)
        return cast_into_slot(t.reshape(-1, 2, rows // 2, cols), own, sel, BF16)

    def view(buf, n):
        rows, cols = w[n].shape[-2:]
        return buf.reshape(N_CHIPS, rows, cols) if w[n].ndim == 2 else buf.reshape(N_CHIPS, 1, rows, cols)

    def mixer(l):
        return [(w_in_name(l), l), ("w_mem_kv", l), ("w_out", l)]

    first = [(n, 0) for n in PER_LAYER] + [("w_in_a", 0), ("w_kv", 0)]
    carried = {
        (0, "ffn1"): mixer(1), (0, "mix"): [(n, 1) for n in FFN1 + FFN2[:2]], (0, "ffn2"): [(FFN2[2], 1)],
        (1, "ffn1"): mixer(2), (1, "mix"): [(n, 2) for n in FFN1 + FFN2[:2]], (1, "ffn2"): [(FFN2[2], 2)],
        (2, "ffn1"): [(FFN1[0], 3)], (2, "mix"): [(FFN1[1], 3), (FFN1[2], 3)], (2, "ffn2"): [(FFN2[0], 3)] + mixer(3),
        (3, "ffn1"): [(FFN2[1], 3)], (3, "mix"): [(FFN2[2], 3)],
    }
    bufs = {it: cast(*it) for it in first + [it for items in carried.values() for it in items]}
    lb_buf = cast_into_slot(hgrn_lb_logits.reshape(1, 2, 1, -1), 0, sel, F32)
    got0 = run_comm(Gather([bufs[it] for it in first] + [lb_buf]), "gather_layer0")
    got = {it: view(b, it[0]) for it, b in zip(first, got0[:-1])}
    w_kv_full = _cols_from_chips(got[("w_kv", 0)][:, None])
    w_kv_full = jnp.pad(w_kv_full, ((0, 0), (0, 0), (0, KV_PAD - w_kv_full.shape[-1])))
    logits3 = jnp.moveaxis(got0[-1].reshape(N_CHIPS, 2, -1), 0, 1).reshape(2, 1, -1)
    lb3 = lb_fwd(logits3)
    w_in, w_mkv, w_o = {}, {}, {}

    def gather_behind(key):
        items = carried.get(key)
        return None if items is None else Gather([bufs[it] for it in items])

    def landed(key, res):
        if res is not None:
            got.update({it: view(b, it[0]) for it, b in zip(carried[key], res)})

    norm3 = {n: w[n].reshape(N_LAYERS, 1, d) for n in ("ffn1_norm", "mix_norm", "mem_norm", "ffn2_norm")}
    kvn3 = kv_norm.reshape(1, 1, d)
    mqg3, mkg3 = _pair_tile(mem_q_gain), _pair_tile(mem_k_gain)
    og3 = hgrn_o_gain.reshape(N_A, 1, LANES)
    fqg3 = _pair_tile(fox_q_gain)
    fkg = jnp.tile(fox_k_gain, 2).reshape(1, LANES)
    fb = jnp.pad(fox_f_bias, (0, LANES - fox_f_bias.shape[0])).reshape(1, LANES)

    sv = [dict() for _ in range(N_LAYERS)]
    h = xs
    kv = None
    for l in range(N_LAYERS):
        t = sv[l]
        t["x0"] = h
        (h, t["a1"], t["b1"]), res = ffn_fwd(h, norm3["ffn1_norm"], l, *[got[(n, l)] for n in FFN1], 0, comm=gather_behind((l, "ffn1")))
        landed((l, "ffn1"), res)
        t["x1"] = h
        w_in[l] = _rows_from_chips(got[(w_in_name(l), l)])
        t["proj"] = proj_fwd(h, norm3["mix_norm"], l, w_in[l], 0, wt=True)
        if l < N_A:
            (main, t["o"]), res = hgrn_fwd(t["proj"], lb3, og3, l, comm=gather_behind((l, "mix")))
            t["qblk"] = 12
        else:
            (main, t["o"], t["lse"]), res = fox_fwd(t["proj"], kv["k"], kv["v"], kv["clf"], kv["clf_t"], fqg3, l - N_A, comm=gather_behind((l, "mix")))
            t["qblk"] = 6
        landed((l, "mix"), res)
        w_mkv[l], w_o[l] = _rows_from_chips(got[("w_mem_kv", l)]), _rows_from_chips(got[("w_out", l)])
        t["kvm"] = proj_fwd(mems, norm3["mem_norm"], l, w_mkv[l], 0)
        memo = memattn_fwd(t["proj"], t["qblk"], t["kvm"], mqg3, mkg3, l)
        t["mixed"] = jnp.concatenate([main, memo], axis=-1)
        h = mm_res(h, t["mixed"], w_o[l], 0)
        t["x2"] = h
        (h, t["a2"], t["b2"]), res = ffn_fwd(h, norm3["ffn2_norm"], l, *[got[(n, l)] for n in FFN2], 0, comm=gather_behind((l, "ffn2")))
        landed((l, "ffn2"), res)
        if l == N_A - 1:
            kv = {"x": h, "kvf": proj_fwd(h, kvn3, 0, w_kv_full, 0)}
            kv["k"], kv["v"], kv["clf"] = kvprep_fwd(kv["kvf"], fkg, fb)
            kv["clf_t"] = kv["clf"][:, :16].T

    loss_local, dx = loss_head(h, tgt)

    nc = N_CHIPS
    fc = ffn1_w_down.shape[1]
    FOX_GROUP = "fox"
    gbuf1 = {n: lax.empty((2, nc, 1, fc, d), F32) for n in FFN1 + FFN2}
    gsplit = [dict(), dict()]

    def row_layout(stack):
        lw, rr, cc = stack.shape
        return stack.reshape(lw, nc, rr // nc, cc)

    def halves_layout(g):
        rr, cc = g.shape
        return jnp.transpose(g.reshape(nc, 2, rr // (2 * nc), cc), (1, 0, 2, 3))

    def group_layout(key):
        if key == FOX_GROUP:
            lay = {n: b.reshape(2, nc, fc, d) for n, b in gbuf1.items()}
            lay["w_mem_kv"] = row_layout(jnp.stack(dw_mkv[N_A:]))
            lay["w_out"] = row_layout(jnp.stack(dw_o[N_A:]))
            lay["w_in_b"] = row_layout(jnp.stack(dw_in["b"]))
            names = PER_LAYER + ["w_in_b"]
        else:
            lay = {n: b.reshape(2, nc, fc // 2, d) for n, b in gsplit[key].items()}
            lay["w_mem_kv"], lay["w_out"] = halves_layout(dw_mkv[key]), halves_layout(dw_o[key])
            lay["w_in_a"] = halves_layout(dw_in["a"][key])
            names = PER_LAYER + ["w_in_a"]
            if key == N_A - 1:
                kv_cols = w_kv.shape[-1] * nc
                lay["w_kv"] = jnp.transpose(dw_kv[:, :kv_cols].reshape(2, d // 2, nc, kv_cols // nc), (0, 2, 1, 3))
                names = names + ["w_kv"]
        return names, [lay[n] for n in names]

    n_ffn = len(FFN1) + len(FFN2)
    riding = {1: FOX_GROUP, 0: 1}
    reduced = {}
    unshared = None
    dw_in = {"a": [None] * N_A, "b": [None] * (N_LAYERS - N_A)}
    dw_o, dw_mkv = [None] * N_LAYERS, [None] * N_LAYERS
    sg = {n: [None] * N_LAYERS for n in ("ffn1_norm", "mix_norm", "mem_norm", "ffn2_norm", "mem_q_gain", "mem_k_gain")}
    sg["hgrn_o_gain"], sg["fox_q_gain"], dlb = [None] * N_A, [None] * (N_LAYERS - N_A), [None] * N_A
    dk_sh = jnp.zeros((s, KV_MAIN), F32)
    dv_sh = jnp.zeros((s, KV_MAIN), F32)
    dclf = jnp.zeros((s, LANES), F32)
    zero_mem = jnp.zeros(mems.shape, F32)
    dw_kv = None
    for l in reversed(range(N_LAYERS)):
        t = sv[l]
        gi, li = l // 2, l % 2
        if l == N_A - 1:
            dkvf, dfkg, dfb = kvprep_bwd(kv["kvf"], fkg, fb, dk_sh, dv_sh, dclf)
            dx, sg["kv_norm"], xn_kv, dpb = proj_bwd(kv["x"], kvn3, 0, [dkvf], w_kv_full, 0, dx)
            dw_kv = wgrad(xn_kv, dpb)
        ride = riding.get(l)
        comms = []
        if unshared is not None:
            comms.append(PairShare(unshared[2]))
        if ride is not None:
            names_r, gl_r = group_layout(ride)
            comms.append(PairExchange(gl_r))
        comm = Multi(comms) if comms else None
        (dx, da, db, hm, xn, dyb, sg["ffn2_norm"][l]), res = ffn_bwd(t["x2"], norm3["ffn2_norm"], l, dx, t["a2"], t["b2"], *[got[(n, l)] for n in FFN2], 0, comm=comm)
        if comm is not None:
            res = comm.split(res)
            if unshared is not None:
                reduced[unshared[0]] = dict(zip(unshared[1], res.pop(0)))
                unshared = None
            if ride is not None:
                partial_r = [pair_sum(g, r, c_arr) for g, r in zip(gl_r, res.pop(0))]

        def ffn_wgrads(which, da, db, hm, xn, dyb):
            for n, (a_, b_) in zip(which, ((da, xn), (db, xn), (hm, dyb))):
                if gi == 1:
                    gbuf1[n] = wgrad(a_, b_, buf=gbuf1[n], l=li)
                else:
                    gsplit[l][n] = wgrad(a_, b_, split=True)

        ffn_wgrads(FFN2, da, db, hm, xn, dyb)
        dmixed, dxb = mm_nt(dx, w_o[l], 0)
        dw_o[l] = wgrad(t["mixed"], dxb)
        dqm, dkvm, dmq, dmk = memattn_bwd(t["proj"], t["qblk"], t["kvm"], mqg3, mkg3, l, dmixed)
        sg["mem_q_gain"][l], sg["mem_k_gain"][l] = _pair_fold(dmq), _pair_fold(dmk)
        _, sg["mem_norm"][l], memn, dkvmb = proj_bwd(mems, norm3["mem_norm"], l, [dkvm], w_mkv[l], 0, zero_mem)
        dw_mkv[l] = wgrad(memn, dkvmb)
        if l < N_A:
            comm = ChipExchange(partial_r[:n_ffn]) if ride is not None else None
            (dzq, dzf, dvi, dzg, dlb[l], sg["hgrn_o_gain"][l]), res = hgrn_bwd(t["proj"], lb3, og3, l, t["o"], dmixed, comm=comm)
            if ride is not None:
                landed_r = list(res)
            parts, key, wl, tmw = [dzq, dzf, dvi, dzg, dqm], "a", l, 13 * LANES
        else:
            lse_t = t["lse"].reshape(s, 6, LANES)[:, :, :2].reshape(s, 12).T
            lse_t = jnp.pad(lse_t, ((0, 4), (0, 0)))
            dq, dgate, dk_sh, dv_sh, dclf, dfq = fox_bwd(t["proj"], kv["k"], kv["v"], kv["clf"], kv["clf_t"], fqg3, l - N_A, t["o"], t["lse"], lse_t, dmixed, dk_sh, dv_sh, dclf)
            sg["fox_q_gain"][l - N_A] = _pair_fold(dfq)
            parts, key, wl, tmw = [dq, dgate, dqm], "b", l - N_A, 7 * LANES
        dx, sg["mix_norm"][l], hn, dpb = proj_bwd(t["x1"], norm3["mix_norm"], l, parts, w_in[l], 0, dx, wt=True)
        dw_in[key][wl] = wgrad(dpb, hn, tm=tmw)
        comm = ChipExchange(partial_r[n_ffn:]) if ride is not None else None
        (dx, da, db, hm, xn, dyb, sg["ffn1_norm"][l]), res = ffn_bwd(t["x0"], norm3["ffn1_norm"], l, dx, t["a1"], t["b1"], *[got[(n, l)] for n in FFN1], 0, comm=comm)
        if ride is not None:
            unshared = (ride, names_r, [chip_sum(p, q, sel) for p, q in zip(partial_r, landed_r + list(res))])
        ffn_wgrads(FFN1, da, db, hm, xn, dyb)

    names0, gl0 = group_layout(0)
    recv0 = run_comm(PairExchange(gl0), "pair_exchange")
    partial0 = [pair_sum(g, r, c_arr) for g, r in zip(gl0, recv0)]
    landed0 = run_comm(ChipExchange(partial0), "chip_exchange")
    mine0 = [chip_sum(p, q, sel) for p, q in zip(partial0, landed0)]
    both = run_comm(PairShare(unshared[2] + mine0), "pair_share")
    reduced[unshared[0]] = dict(zip(unshared[1], both[: len(unshared[1])]))
    reduced[0] = dict(zip(names0, both[len(unshared[1]) :]))
    gparts = {n: [reduced[0][n], reduced[1][n], reduced[FOX_GROUP][n]] for n in PER_LAYER}
    gparts.update({"w_in_a": [reduced[0]["w_in_a"], reduced[1]["w_in_a"]], "w_kv": [reduced[1]["w_kv"]],
                   "w_in_b": [reduced[FOX_GROUP]["w_in_b"]]})

    dlogits = lb_bwd(logits3, dlb[1]).reshape(2, -1)
    small = {
        "ffn1_norm": jnp.concatenate(sg["ffn1_norm"]), "mix_norm": jnp.concatenate(sg["mix_norm"]),
        "mem_norm": jnp.concatenate(sg["mem_norm"]), "ffn2_norm": jnp.concatenate(sg["ffn2_norm"]),
        "mem_q_gain": jnp.concatenate(sg["mem_q_gain"]), "mem_k_gain": jnp.concatenate(sg["mem_k_gain"]),
        "hgrn_o_gain": jnp.concatenate(sg["hgrn_o_gain"]), "fox_q_gain": jnp.concatenate(sg["fox_q_gain"]),
        "kv_norm": sg["kv_norm"], "fox_f_bias": dfb[:, : fox_f_bias.shape[0]], "fox_k_gain": _pair_fold(dfkg),
        "hgrn_lb_logits": dlogits,
    }
    flat = [small[n].reshape(-1) for n in SMALL] + [loss_local.reshape(-1)]
    sizes = [f.shape[0] for f in flat]
    total = sum(sizes)
    padded = -(-total // (8 * LANES)) * (8 * LANES)
    packed = jnp.pad(jnp.concatenate(flat), (0, padded - total)).reshape(-1, LANES)
    summed = small_allreduce(packed).reshape(-1)
    off = 0
    for n, sz in zip(SMALL, sizes[:-1]):
        gparts[n] = [summed[off : off + sz].reshape(dlogits.shape if n == "hgrn_lb_logits" else w[n].shape)]
        off += sz
    loss = summed[off]
    lbw = hgrn_lb_logits.shape[1]
    gparts["hgrn_lb_logits"] = [lax.dynamic_slice_in_dim(gparts["hgrn_lb_logits"][0], my_chip * lbw, lbw, axis=1)]

    grads, delta, new_m, new_v = {}, {}, {}, {}
    for n in WEIGHT_NAMES:
        res = adamw(w[n], gparts[n], oriented(n, given["m_" + n]), oriented(n, given["v_" + n]))
        grads[n], delta[n], new_m[n], new_v[n] = (oriented(n, t) for t in res)
    return (loss, dx[None], *[grads[n] for n in WEIGHT_NAMES], *[delta[n] for n in WEIGHT_NAMES],
            *[new_m[n] for n in WEIGHT_NAMES], *[new_v[n] for n in WEIGHT_NAMES])
```

```python
import functools

import jax
import jax.numpy as jnp
from jax import lax
from jax.experimental import pallas as pl
from jax.experimental.pallas import tpu as pltpu

F32, BF16 = jnp.float32, jnp.bfloat16
HI = lax.Precision.HIGHEST
EPS = 1e-6
MESH = pl.DeviceIdType.MESH
ANY = pl.BlockSpec(memory_space=pl.ANY)

VMEM_LIMIT_BYTES = 56 << 20
N_CHIPS = 4
N_DEV = 8
LANES = 128
HEAD64 = 64
CHUNK = 64
SUB = 32
HGRN_HEADS_PER_STEP = 2
TQ = 256
TOK = 256

ADAM_LR, ADAM_B1, ADAM_B2, ADAM_EPS, ADAM_WD, ADAM_STEP = 0.001, 0.9, 0.999, 1e-08, 0.01, 10


def _cparams(sem=None, **kw):
    return pltpu.CompilerParams(dimension_semantics=sem, vmem_limit_bytes=VMEM_LIMIT_BYTES, **kw)


def _mm(a, b, dims, prec=None):
    return lax.dot_general(a, b, (dims, ((), ())), preferred_element_type=F32, precision=prec)


def dot_nn(a, b, prec=None):
    return _mm(a, b, ((1,), (0,)), prec)


def dot_nt(a, b, prec=None):
    return _mm(a, b, ((1,), (1,)), prec)


def dot_tn(a, b, prec=None):
    return _mm(a, b, ((0,), (0,)), prec)


def bf(v):
    return v.astype(BF16)


def _sigmoid(z):
    return jax.nn.sigmoid(z)


def _dsilu(z, s):
    return s * (1.0 + z * (1.0 - s))


def _rms(x):
    r = lax.rsqrt(jnp.mean(x * x, axis=-1, keepdims=True) + EPS)
    return x * r, r


def _rms_bwd(dxn, u, r, g):
    du = dxn * g
    dx = r * (du - u * jnp.mean(du * u, axis=-1, keepdims=True))
    return dx, jnp.sum(dxn * u, axis=0, keepdims=True)


def _lane_mask0(shape):
    return lax.broadcasted_iota(jnp.int32, shape, len(shape) - 1) < HEAD64


def _rms64(x, m0):
    sq = x * x
    s0 = jnp.sum(jnp.where(m0, sq, 0.0), axis=-1, keepdims=True)
    s1 = jnp.sum(jnp.where(m0, 0.0, sq), axis=-1, keepdims=True)
    r = lax.rsqrt(jnp.where(m0, s0, s1) * (1.0 / HEAD64) + EPS)
    return x * r, r


def _rms64_bwd(dxn, u, r, g, m0):
    du = dxn * g
    t = du * u
    t0 = jnp.sum(jnp.where(m0, t, 0.0), axis=-1, keepdims=True)
    t1 = jnp.sum(jnp.where(m0, 0.0, t), axis=-1, keepdims=True)
    dx = r * (du - u * (jnp.where(m0, t0, t1) * (1.0 / HEAD64)))
    return dx, jnp.sum(dxn * u, axis=0, keepdims=True)


def _tok(s):
    return TOK if s % TOK == 0 else s


def _const(shape):
    return pl.BlockSpec(shape, lambda *_: (0,) * len(shape))


def ffn_fwd(x, gain3, l, wg, wu, wd, wl, comm=None):
    s, d = x.shape
    nc, _, fc, _ = wg.shape
    tm = _tok(s)

    def body(x_ref, g_ref, wg_ref, wu_ref, wd_ref, xo_ref, a_ref, b_ref):
        xv = x_ref[...]
        u, _ = _rms(xv)
        xn = bf(u * g_ref[...])
        y = jnp.zeros((tm, d), F32)
        for c in range(nc):
            a = dot_nt(xn, wg_ref[c])
            b = dot_nt(xn, wu_ref[c])
            a_ref[c] = bf(a)
            b_ref[c] = bf(b)
            y = y + dot_nn(bf(a * _sigmoid(a) * b), wd_ref[c])
        xo_ref[...] = xv + 0.5 * y

    wspec = pl.BlockSpec((nc, None, fc, d), lambda i: (0, wl, 0, 0), pipeline_mode=pl.Buffered(1))
    wdspec = pl.BlockSpec((nc, None, fc, d), lambda i: (0, wl, 0, 0), pipeline_mode=pl.Buffered(1))
    row = pl.BlockSpec((tm, d), lambda i: (i, 0))
    act = pl.BlockSpec((nc, tm, fc), lambda i: (0, i, 0))
    return _carry(
        body,
        comm,
        name="ffn_fwd",
        grid=(s // tm,),
        in_specs=[row, pl.BlockSpec((None, 1, d), lambda i: (l, 0, 0)), wspec, wspec, wdspec],
        out_specs=[row, act, act],
        out_shape=[
            jax.ShapeDtypeStruct((s, d), F32),
            jax.ShapeDtypeStruct((nc, s, fc), BF16),
            jax.ShapeDtypeStruct((nc, s, fc), BF16),
        ],
        scratch_shapes=[],
        args=(x, gain3, wg, wu, wd),
    )


def ffn_bwd(x, gain3, l, dout, a, b, wg, wu, wd, wl, comm=None):
    s, d = x.shape
    nc, _, fc, _ = wg.shape
    tm = _tok(s)

    def body(x_ref, g_ref, do_ref, a_ref, b_ref, wg_ref, wu_ref, wd_ref, dx_ref, da_ref, db_ref, hm_ref, xn_ref, dy_ref, dg_ref):
        xv = x_ref[...]
        g = g_ref[...]
        u, r = _rms(xv)
        xn_ref[...] = bf(u * g)
        dout = do_ref[...]
        dy = bf(0.5 * dout)
        dy_ref[...] = dy
        dxn = jnp.zeros((tm, d), F32)
        for c in range(nc):
            av = a_ref[c].astype(F32)
            bv = b_ref[c].astype(F32)
            sg = _sigmoid(av)
            sl = av * sg
            dh = dot_nt(dy, wd_ref[c])
            da = bf(dh * bv * _dsilu(av, sg))
            db = bf(dh * sl)
            da_ref[c] = da
            db_ref[c] = db
            hm_ref[c] = bf(sl * bv)
            dxn = dxn + dot_nn(da, wg_ref[c]) + dot_nn(db, wu_ref[c])
        dx, dg = _rms_bwd(dxn, u, r, g)
        dx_ref[...] = dout + dx

        @pl.when(pl.program_id(0) == 0)
        def _():
            dg_ref[...] = jnp.zeros_like(dg_ref)

        dg_ref[...] += dg

    wspec = pl.BlockSpec((nc, None, fc, d), lambda i: (0, wl, 0, 0), pipeline_mode=pl.Buffered(1))
    wdspec = pl.BlockSpec((nc, None, fc, d), lambda i: (0, wl, 0, 0), pipeline_mode=pl.Buffered(1))
    row = pl.BlockSpec((tm, d), lambda i: (i, 0))
    act = pl.BlockSpec((nc, tm, fc), lambda i: (0, i, 0))
    act_shape = jax.ShapeDtypeStruct((nc, s, fc), BF16)
    return _carry(
        body,
        comm,
        name="ffn_bwd",
        grid=(s // tm,),
        in_specs=[row, pl.BlockSpec((None, 1, d), lambda i: (l, 0, 0)), row, act, act, wspec, wspec, wdspec],
        out_specs=[row, act, act, act, row, row, _const((1, d))],
        out_shape=[
            jax.ShapeDtypeStruct((s, d), F32),
            act_shape,
            act_shape,
            act_shape,
            jax.ShapeDtypeStruct((s, d), BF16),
            jax.ShapeDtypeStruct((s, d), BF16),
            jax.ShapeDtypeStruct((1, d), F32),
        ],
        scratch_shapes=[],
        args=(x, gain3, dout, a, b, wg, wu, wd),
    )


def wgrad(a, b, tn=None, buf=None, l=None, tm=None, split=False):
    ca = a.shape[0] if a.ndim == 3 else 1
    cb = b.shape[0] if b.ndim == 3 else 1
    nc = max(ca, cb)
    s, m = a.shape[-2:]
    n = b.shape[-1]
    tn = n if tn is None else tn
    assert n % tn == 0

    def body(*refs):
        a_ref, b_ref, o_ref = refs[0], refs[1], refs[-1]
        res = dot_tn(a_ref[...], b_ref[...])
        if split:
            o_ref[0] = res[: m // 2]
            o_ref[1] = res[m // 2 :]
        else:
            o_ref[...] = res

    params = _cparams(("arbitrary", "arbitrary"))
    if buf is None and not split:
        assert nc == 1 and a.ndim == 2 and b.ndim == 2
        tm = m if tm is None else tm
        assert m % tm == 0
        return pl.pallas_call(
            body,
            name="wgrad",
            grid=(m // tm, n // tn),
            in_specs=[pl.BlockSpec((s, tm), lambda i, j: (0, i)), pl.BlockSpec((s, tn), lambda i, j: (0, j))],
            out_specs=pl.BlockSpec((tm, tn), lambda i, j: (i, j)),
            out_shape=jax.ShapeDtypeStruct((m, n), F32),
            compiler_params=params,
        )(a, b)
    a_spec = pl.BlockSpec((None, s, m), lambda c, j: (c, 0, 0)) if a.ndim == 3 else pl.BlockSpec((s, m), lambda c, j: (0, 0))
    b_spec = pl.BlockSpec((None, s, tn), lambda c, j: (c, 0, j)) if b.ndim == 3 else pl.BlockSpec((s, tn), lambda c, j: (0, j))
    if split:
        return pl.pallas_call(
            body,
            name="wgrad_split",
            grid=(nc, n // tn),
            in_specs=[a_spec, b_spec],
            out_specs=pl.BlockSpec((2, None, None, m // 2, tn), lambda c, j: (0, c, 0, 0, j)),
            out_shape=jax.ShapeDtypeStruct((2, nc, 1, m // 2, n), F32),
            compiler_params=params,
        )(a, b)
    lh = buf.shape[2]
    hi, lo = l // lh, l % lh
    o_spec = pl.BlockSpec((None, None, None, m, tn), lambda c, j: (hi, c, lo, 0, j))
    return pl.pallas_call(
        body,
        name="wgrad_buf",
        grid=(nc, n // tn),
        in_specs=[a_spec, b_spec, ANY],
        out_specs=o_spec,
        out_shape=jax.ShapeDtypeStruct(buf.shape, F32),
        input_output_aliases={2: 0},
        compiler_params=params,
    )(a, b, buf)


def proj_fwd(x, gain3, l, w, wl, wt=False):
    s, d = x.shape
    n = w.shape[1] if wt else w.shape[2]
    tm = _tok(s)

    def body(x_ref, g_ref, w_ref, o_ref):
        u, _ = _rms(x_ref[...])
        xn = bf(u * g_ref[...])
        o_ref[...] = dot_nt(xn, w_ref[...]) if wt else dot_nn(xn, w_ref[...])

    return pl.pallas_call(
        body,
        name="proj_fwd",
        grid=(s // tm,),
        in_specs=[
            pl.BlockSpec((tm, d), lambda i: (i, 0)),
            pl.BlockSpec((None, 1, d), lambda i: (l, 0, 0)),
            pl.BlockSpec((None,) + w.shape[1:], lambda i: (wl, 0, 0)),
        ],
        out_specs=pl.BlockSpec((tm, n), lambda i: (i, 0)),
        out_shape=jax.ShapeDtypeStruct((s, n), F32),
        compiler_params=_cparams(("arbitrary",)),
    )(x, gain3, w)


def proj_bwd(x, gain3, l, parts, w, wl, dx_in, wt=False):
    s, d = x.shape
    n = w.shape[1] if wt else w.shape[2]
    widths = [p.shape[1] for p in parts]
    assert sum(widths) == n
    tm = _tok(s)
    npart = len(parts)

    def body(*refs):
        x_ref, g_ref, w_ref, dxin_ref = refs[:4]
        p_refs = refs[4 : 4 + npart]
        dx_ref, dg_ref, xn_ref, dpb_ref = refs[4 + npart :]
        g = g_ref[...]
        u, r = _rms(x_ref[...])
        xn_ref[...] = bf(u * g)
        dxn = jnp.zeros((tm, d), F32)
        off = 0
        for p_ref, wd_ in zip(p_refs, widths):
            dp = bf(p_ref[...])
            dpb_ref[:, off : off + wd_] = dp
            dxn = dxn + (dot_nn(dp, w_ref[off : off + wd_, :]) if wt else dot_nt(dp, w_ref[:, off : off + wd_]))
            off += wd_
        dx, dg = _rms_bwd(dxn, u, r, g)
        dx_ref[...] = dxin_ref[...] + dx

        @pl.when(pl.program_id(0) == 0)
        def _():
            dg_ref[...] = jnp.zeros_like(dg_ref)

        dg_ref[...] += dg

    row = pl.BlockSpec((tm, d), lambda i: (i, 0))
    return pl.pallas_call(
        body,
        name="proj_bwd",
        grid=(s // tm,),
        in_specs=[row, pl.BlockSpec((None, 1, d), lambda i: (l, 0, 0)), pl.BlockSpec((None,) + w.shape[1:], lambda i: (wl, 0, 0)), row]
        + [pl.BlockSpec((tm, wd_), lambda i: (i, 0)) for wd_ in widths],
        out_specs=[row, _const((1, d)), row, pl.BlockSpec((tm, n), lambda i: (i, 0))],
        out_shape=[
            jax.ShapeDtypeStruct((s, d), F32),
            jax.ShapeDtypeStruct((1, d), F32),
            jax.ShapeDtypeStruct((s, d), BF16),
            jax.ShapeDtypeStruct((s, n), BF16),
        ],
        compiler_params=_cparams(("arbitrary",)),
    )(x, gain3, w, dx_in, *parts)


def mm_res(x, a, w, l):
    s, d = x.shape
    k = a.shape[1]
    tm = _tok(s)

    def body(x_ref, a_ref, w_ref, o_ref):
        o_ref[...] = x_ref[...] + dot_nn(a_ref[...], w_ref[...])

    return pl.pallas_call(
        body,
        name="mm_res",
        grid=(s // tm,),
        in_specs=[
            pl.BlockSpec((tm, d), lambda i: (i, 0)),
            pl.BlockSpec((tm, k), lambda i: (i, 0)),
            pl.BlockSpec((None, k, d), lambda i: (l, 0, 0)),
        ],
        out_specs=pl.BlockSpec((tm, d), lambda i: (i, 0)),
        out_shape=jax.ShapeDtypeStruct((s, d), F32),
        compiler_params=_cparams(("arbitrary",)),
    )(x, a, w)


def mm_nt(dx, w, l):
    s, d = dx.shape
    k = w.shape[1]
    tm = _tok(s)

    def body(dx_ref, w_ref, o_ref, dxb_ref):
        dxb = bf(dx_ref[...])
        dxb_ref[...] = dxb
        o_ref[...] = dot_nt(dxb, w_ref[...])

    return pl.pallas_call(
        body,
        name="mm_nt",
        grid=(s // tm,),
        in_specs=[pl.BlockSpec((tm, d), lambda i: (i, 0)), pl.BlockSpec((None, k, d), lambda i: (l, 0, 0))],
        out_specs=[pl.BlockSpec((tm, k), lambda i: (i, 0)), pl.BlockSpec((tm, d), lambda i: (i, 0))],
        out_shape=[jax.ShapeDtypeStruct((s, k), F32), jax.ShapeDtypeStruct((s, d), BF16)],
        compiler_params=_cparams(("arbitrary",)),
    )(dx, w)


def lb_fwd(logits3):
    def body(l_ref, o_ref):
        l0, l1 = l_ref[0], l_ref[1]
        m = jnp.maximum(l0, l1)
        e0, e1 = jnp.exp(l0 - m), jnp.exp(l1 - m)
        p0, p1 = e0 / (e0 + e1), e1 / (e0 + e1)
        o_ref[0] = p0 - p0
        o_ref[1] = (p0 + p1) - p0

    return pl.pallas_call(body, name="lb_fwd", out_shape=jax.ShapeDtypeStruct(logits3.shape, F32))(logits3)


def lb_bwd(logits3, dlb1):
    def body(l_ref, d_ref, o_ref):
        l0, l1 = l_ref[0], l_ref[1]
        m = jnp.maximum(l0, l1)
        e0, e1 = jnp.exp(l0 - m), jnp.exp(l1 - m)
        p0, p1 = e0 / (e0 + e1), e1 / (e0 + e1)
        t = d_ref[...] * p0 * p1
        o_ref[0] = -t
        o_ref[1] = t

    return pl.pallas_call(body, name="lb_bwd", out_shape=jax.ShapeDtypeStruct(logits3.shape, F32))(logits3, dlb1)


def _hgrn_gates(zq, zf, lb):
    sf = _sigmoid(zf)
    f = lb + (1.0 - lb) * sf
    sq = _sigmoid(zq)
    return sf, f, jnp.log(f), 1.0 - f, sq, zq * sq


def _tri(n, upper=False):
    r = lax.broadcasted_iota(jnp.int32, (n, n), 0)
    c = lax.broadcasted_iota(jnp.int32, (n, n), 1)
    return jnp.where((c >= r) if upper else (r >= c), 1.0, 0.0).astype(F32)


def hgrn_fwd(proj, lb3, og3, l, comm=None):
    s = proj.shape[0]
    nh = 6
    n_chunk = s // CHUNK
    nsub = CHUNK // SUB

    hb = HGRN_HEADS_PER_STEP
    wide = hb * LANES

    def body(zq_ref, zf_ref, vi_ref, zg_ref, lb_ref, og_ref, main_ref, o_ref, q_a, k_a, v_a, c_a):
        og = og_ref[...]
        tril = _tri(CHUNK)
        rowi = lax.broadcasted_iota(jnp.int32, (SUB, LANES), 0)

        def one_head(hd, rows, st):
            cols = slice(hd * LANES, (hd + 1) * LANES)
            q_s, k_s, v_s, c_s = q_a.at[hd], k_a.at[hd], v_a.at[hd], c_a.at[hd]
            zg = zg_ref[rows, cols]
            _, _, lf, k, _, q = _hgrn_gates(zq_ref[rows, cols], zf_ref[rows, cols], lb_ref[:, cols])
            v = vi_ref[rows, cols]
            c = dot_nn(tril, lf, HI)
            q_s[...] = q
            k_s[...] = k
            v_s[...] = v
            c_s[...] = c
            o_inter = dot_nt(q * jnp.exp(c), st, HI)
            parts = []
            for i in range(nsub):
                lo = i * SUB
                blk = pl.ds(lo, SUB)
                qb, cb = q_s[blk, :], c_s[blk, :]
                ob = o_inter[lo : lo + SUB]
                if i > 0:
                    rr = c_s[pl.ds(lo - 1, 1), :]
                    qt = qb * jnp.exp(cb - rr)
                    kt = k_s[pl.ds(0, lo), :] * jnp.exp(rr - c_s[pl.ds(0, lo), :])
                    ob = ob + dot_nn(dot_nt(qt, kt, HI), v_s[pl.ds(0, lo), :], HI)
                for t in range(SUB):
                    e = jnp.where(rowi >= t, jnp.exp(cb - c_s[pl.ds(lo + t, 1), :]), 0.0)
                    a = jnp.sum(qb * k_s[pl.ds(lo + t, 1), :] * e, axis=-1, keepdims=True)
                    ob = ob + a * v_s[pl.ds(lo + t, 1), :]
                parts.append(ob)
            o = jnp.concatenate(parts, axis=0)
            ce = c_s[pl.ds(CHUNK - 1, 1), :]
            st = st * jnp.exp(ce) + dot_tn(v, k * jnp.exp(ce - c), HI)
            on, _ = _rms(o)
            o_ref[rows, cols] = o
            main_ref[rows, cols] = bf(on * og * (zg * _sigmoid(zg)))
            return st

        def chunk(ci, sts):
            rows = pl.ds(pl.multiple_of(ci * CHUNK, CHUNK), CHUNK)
            return tuple(one_head(hd, rows, sts[hd]) for hd in range(hb))

        lax.fori_loop(0, n_chunk, chunk, tuple(jnp.zeros((LANES, LANES), F32) for _ in range(hb)))

    def col(k):
        return pl.BlockSpec((s, wide), lambda h: (0, k * (nh // hb) + h))

    vec = pl.BlockSpec((None, 1, wide), lambda h: (l, 0, h))
    return _carry(
        body,
        comm,
        name="hgrn_fwd",
        grid=(nh // hb,),
        in_specs=[col(0), col(1), col(2), col(3), vec, pl.BlockSpec((None, 1, LANES), lambda h: (l, 0, 0))],
        out_specs=[pl.BlockSpec((s, wide), lambda h: (0, h))] * 2,
        out_shape=[jax.ShapeDtypeStruct((s, nh * LANES), BF16), jax.ShapeDtypeStruct((s, nh * LANES), F32)],
        scratch_shapes=[pltpu.VMEM((hb, CHUNK, LANES), F32)] * 4,
        args=(proj, proj, proj, proj, lb3, og3),
    )


def hgrn_bwd(proj, lb3, og3, l, o, dmixed, comm=None):
    s = proj.shape[0]
    nh = 6
    n_chunk = s // CHUNK
    nsub = CHUNK // SUB

    hb = HGRN_HEADS_PER_STEP
    wide = hb * LANES

    def body(zq_ref, zf_ref, vi_ref, zg_ref, lb_ref, og_ref, o_ref, dm_ref,
             dzq_ref, dzf_ref, dvi_ref, dzg_ref, dlb_ref, dog_ref,
             st_a, q_a, k_a, v_a, c_a, do_a, dq_a, dk_a, dv_a, acc_a):
        og = og_ref[...]
        tril = _tri(CHUNK)
        triu = _tri(CHUNK, upper=True)
        rowi = lax.broadcasted_iota(jnp.int32, (SUB, LANES), 0)

        def fwd_head(hd, ci, rows, st):
            cols = slice(hd * LANES, (hd + 1) * LANES)
            _, _, lf, k, _, _ = _hgrn_gates(zq_ref[rows, cols], zf_ref[rows, cols], lb_ref[:, cols])
            c = dot_nn(tril, lf, HI)
            ce = jnp.sum(lf, axis=0, keepdims=True)
            st_a[hd, ci] = st
            return st * jnp.exp(ce) + dot_tn(vi_ref[rows, cols], k * jnp.exp(ce - c), HI)

        def fwd_chunk(ci, sts):
            rows = pl.ds(pl.multiple_of(ci * CHUNK, CHUNK), CHUNK)
            return tuple(fwd_head(hd, ci, rows, sts[hd]) for hd in range(hb))

        lax.fori_loop(0, n_chunk, fwd_chunk, tuple(jnp.zeros((LANES, LANES), F32) for _ in range(hb)))
        acc_a[...] = jnp.zeros_like(acc_a)

        def bwd_head(hd, ci, rows, carry):
            dst, cg = carry
            cols = slice(hd * LANES, (hd + 1) * LANES)
            q_s, k_s, v_s, c_s, do_s = q_a.at[hd], k_a.at[hd], v_a.at[hd], c_a.at[hd], do_a.at[hd]
            dq_s, dk_s, dv_s, acc_s = dq_a.at[hd], dk_a.at[hd], dv_a.at[hd], acc_a.at[hd]
            lb = lb_ref[:, cols]
            zq, zf, zg = zq_ref[rows, cols], zf_ref[rows, cols], zg_ref[rows, cols]
            sf, f, lf, k, sq, q = _hgrn_gates(zq, zf, lb)
            v = vi_ref[rows, cols]
            c = dot_nn(tril, lf, HI)
            st = st_a[hd, ci]
            on, r = _rms(o_ref[rows, cols])
            sg = _sigmoid(zg)
            dmain = dm_ref[rows, cols]
            dy = dmain * (zg * sg)
            dzg_ref[rows, cols] = dmain * (on * og) * _dsilu(zg, sg)
            do, dog = _rms_bwd(dy, on, r, og)
            acc_s[pl.ds(0, 1), :] += dog
            q_s[...] = q
            k_s[...] = k
            v_s[...] = v
            c_s[...] = c
            do_s[...] = do
            ce = c_s[pl.ds(CHUNK - 1, 1), :]
            eq = jnp.exp(c)
            ek = jnp.exp(ce - c)
            qt_all = q * eq
            dq_s[...] = dot_nn(do, st, HI) * eq
            dv_s[...] = dot_nt(k * ek, dst, HI)
            dk_s[...] = dot_nn(v, dst, HI) * ek
            dst = dst * jnp.exp(ce) + dot_tn(do, qt_all, HI)
            for i in range(nsub):
                lo = i * SUB
                blk = pl.ds(lo, SUB)
                qb, cb, dob = q_s[blk, :], c_s[blk, :], do_s[blk, :]
                if i > 0:
                    prev = pl.ds(0, lo)
                    rr = c_s[pl.ds(lo - 1, 1), :]
                    eqi = jnp.exp(cb - rr)
                    eki = jnp.exp(rr - c_s[prev, :])
                    qt = qb * eqi
                    kt = k_s[prev, :] * eki
                    amat = dot_nt(qt, kt, HI)
                    damat = dot_nt(dob, v_s[prev, :], HI)
                    dv_s[prev, :] += dot_tn(amat, dob, HI)
                    dq_s[blk, :] += dot_nn(damat, kt, HI) * eqi
                    dk_s[prev, :] += dot_tn(damat, qt, HI) * eki
                dqb = jnp.zeros((SUB, LANES), F32)
                for t in range(SUB):
                    row = pl.ds(lo + t, 1)
                    e = jnp.where(rowi >= t, jnp.exp(cb - c_s[row, :]), 0.0)
                    kr = k_s[row, :]
                    a = jnp.sum(qb * kr * e, axis=-1, keepdims=True)
                    da = jnp.sum(dob * v_s[row, :], axis=-1, keepdims=True)
                    dv_s[row, :] += jnp.sum(a * dob, axis=0, keepdims=True)
                    dqb = dqb + da * kr * e
                    dk_s[row, :] += jnp.sum(da * qb * e, axis=0, keepdims=True)
                dq_s[blk, :] += dqb
            dq, dk = dq_s[...], dk_s[...]
            dg = q * dq - k * dk
            dlf = dot_nn(triu, dg, HI) + cg
            cg = cg + jnp.sum(dg, axis=0, keepdims=True)
            df = dlf / f - dk
            dzf_ref[rows, cols] = df * (1.0 - lb) * sf * (1.0 - sf)
            acc_s[pl.ds(1, 1), :] += jnp.sum(df * (1.0 - sf), axis=0, keepdims=True)
            dzq_ref[rows, cols] = dq * _dsilu(zq, sq)
            dvi_ref[rows, cols] = dv_s[...]
            return dst, cg

        def bwd_chunk(jj, carries):
            ci = n_chunk - 1 - jj
            rows = pl.ds(pl.multiple_of(ci * CHUNK, CHUNK), CHUNK)
            return tuple(bwd_head(hd, ci, rows, carries[hd]) for hd in range(hb))

        zero = (jnp.zeros((LANES, LANES), F32), jnp.zeros((1, LANES), F32))
        lax.fori_loop(0, n_chunk, bwd_chunk, tuple(zero for _ in range(hb)))

        @pl.when(pl.program_id(0) == 0)
        def _():
            dog_ref[...] = jnp.zeros_like(dog_ref)

        for hd in range(hb):
            dlb_ref[:, hd * LANES : (hd + 1) * LANES] = acc_a[hd, pl.ds(1, 1), :]
            dog_ref[...] += acc_a[hd, pl.ds(0, 1), :]

    def col(k):
        return pl.BlockSpec((s, wide), lambda h: (0, k * (nh // hb) + h), pipeline_mode=pl.Buffered(1))

    head_in = pl.BlockSpec((s, wide), lambda h: (0, h), pipeline_mode=pl.Buffered(1))
    head = pl.BlockSpec((s, wide), lambda h: (0, h))
    vec = pl.BlockSpec((None, 1, wide), lambda h: (l, 0, h))
    ck = pltpu.VMEM((hb, CHUNK, LANES), F32)
    return _carry(
        body,
        comm,
        name="hgrn_bwd",
        grid=(nh // hb,),
        in_specs=[col(0), col(1), col(2), col(3), vec, pl.BlockSpec((None, 1, LANES), lambda h: (l, 0, 0)), head_in, head_in],
        out_specs=[head] * 4 + [pl.BlockSpec((1, wide), lambda h: (0, h)), _const((1, LANES))],
        out_shape=[jax.ShapeDtypeStruct((s, nh * LANES), F32)] * 4
        + [jax.ShapeDtypeStruct((1, nh * LANES), F32), jax.ShapeDtypeStruct((1, LANES), F32)],
        scratch_shapes=[pltpu.VMEM((hb, n_chunk, LANES, LANES), F32)] + [ck] * 8 + [pltpu.VMEM((hb, 8, LANES), F32)],
        args=(proj, proj, proj, proj, lb3, og3, o, dmixed),
    )


MEM_SCALE = HEAD64**-0.5


def _mem_heads(qraw, kvm, qg, kg, pr, m0):
    lo = pr * LANES
    uq, rq = _rms64(qraw[:, lo : lo + LANES], m0)
    uk, rk = _rms64(kvm[:, lo : lo + LANES], m0)
    v = bf(kvm[:, 2 * LANES + lo : 3 * LANES + lo])
    return uq, rq, uk, rk, v, uq * qg, bf(uk * kg)


def memattn_fwd(proj, qblk, kvm, qg3, kg3, l):
    s = proj.shape[0]
    nm = kvm.shape[0]
    tm = _tok(s)

    def body(q_ref, kv_ref, qg_ref, kg_ref, o_ref):
        m0 = _lane_mask0((1, LANES))
        qraw, kvv = q_ref[...], kv_ref[...]
        for pr in range(2):
            _, _, _, _, v, qn, kn = _mem_heads(qraw, kvv, qg_ref[...], kg_ref[...], pr, m0)
            out = jnp.zeros((tm, LANES), F32)
            for hh in range(2):
                mh = m0 if hh == 0 else jnp.logical_not(m0)
                sc = dot_nt(bf(jnp.where(mh, qn, 0.0)), kn) * MEM_SCALE
                p = jnp.exp(sc - jnp.max(sc, axis=-1, keepdims=True))
                p = p / jnp.sum(p, axis=-1, keepdims=True)
                out = jnp.where(mh, dot_nn(bf(p), v), out)
            o_ref[:, pr * LANES : (pr + 1) * LANES] = bf(out)

    gspec = pl.BlockSpec((None, 1, LANES), lambda i: (l, 0, 0))
    return pl.pallas_call(
        body,
        name="memattn_fwd",
        grid=(s // tm,),
        in_specs=[pl.BlockSpec((tm, 2 * LANES), lambda i: (i, qblk)), _const((nm, 4 * LANES)), gspec, gspec],
        out_specs=pl.BlockSpec((tm, 2 * LANES), lambda i: (i, 0)),
        out_shape=jax.ShapeDtypeStruct((s, 2 * LANES), BF16),
        compiler_params=_cparams(("arbitrary",)),
    )(proj, kvm, qg3, kg3)


def memattn_bwd(proj, qblk, kvm, qg3, kg3, l, dmixed):
    s = proj.shape[0]
    nm = kvm.shape[0]
    tm = _tok(s)

    def body(q_ref, kv_ref, qg_ref, kg_ref, dm_ref, dq_ref, dkv_ref, dqg_ref, dkg_ref):
        m0 = _lane_mask0((1, LANES))
        qraw, kvv = q_ref[...], kv_ref[...]
        qg, kg = qg_ref[...], kg_ref[...]

        @pl.when(pl.program_id(0) == 0)
        def _():
            dkv_ref[...] = jnp.zeros_like(dkv_ref)
            dqg_ref[...] = jnp.zeros_like(dqg_ref)
            dkg_ref[...] = jnp.zeros_like(dkg_ref)

        for pr in range(2):
            lo = pr * LANES
            uq, rq, uk, rk, v, qn, kn = _mem_heads(qraw, kvv, qg, kg, pr, m0)
            do = dm_ref[:, lo : lo + LANES]
            dqn = jnp.zeros((tm, LANES), F32)
            dkn = jnp.zeros((nm, LANES), F32)
            dv = jnp.zeros((nm, LANES), F32)
            for hh in range(2):
                mh = m0 if hh == 0 else jnp.logical_not(m0)
                qh = bf(jnp.where(mh, qn, 0.0))
                doh = bf(jnp.where(mh, do, 0.0))
                sc = dot_nt(qh, kn) * MEM_SCALE
                p = jnp.exp(sc - jnp.max(sc, axis=-1, keepdims=True))
                p = p / jnp.sum(p, axis=-1, keepdims=True)
                dp = dot_nt(doh, v)
                ds = bf(p * (dp - jnp.sum(p * dp, axis=-1, keepdims=True)))
                dqn = dqn + jnp.where(mh, dot_nn(ds, kn), 0.0) * MEM_SCALE
                dkn = dkn + dot_tn(ds, qh) * MEM_SCALE
                dv = dv + dot_tn(bf(p), doh)
            dqr, dqg = _rms64_bwd(dqn, uq, rq, qg, m0)
            dkr, dkg = _rms64_bwd(dkn, uk, rk, kg, m0)
            dq_ref[:, lo : lo + LANES] = dqr
            dkv_ref[:, lo : lo + LANES] += dkr
            dkv_ref[:, 2 * LANES + lo : 3 * LANES + lo] += dv
            dqg_ref[...] += dqg
            dkg_ref[...] += dkg

    gspec = pl.BlockSpec((None, 1, LANES), lambda i: (l, 0, 0))
    return pl.pallas_call(
        body,
        name="memattn_bwd",
        grid=(s // tm,),
        in_specs=[
            pl.BlockSpec((tm, 2 * LANES), lambda i: (i, qblk)),
            _const((nm, 4 * LANES)),
            gspec,
            gspec,
            pl.BlockSpec((tm, 2 * LANES), lambda i: (i, 3)),
        ],
        out_specs=[pl.BlockSpec((tm, 2 * LANES), lambda i: (i, 0)), _const((nm, 4 * LANES)), _const((1, LANES)), _const((1, LANES))],
        out_shape=[
            jax.ShapeDtypeStruct((s, 2 * LANES), F32),
            jax.ShapeDtypeStruct((nm, 4 * LANES), F32),
            jax.ShapeDtypeStruct((1, LANES), F32),
            jax.ShapeDtypeStruct((1, LANES), F32),
        ],
        compiler_params=_cparams(("arbitrary",)),
    )(proj, kvm, qg3, kg3, dmixed)


KV_MAIN = 768


def _log_sigmoid(z):
    return jnp.minimum(z, 0.0) - jnp.log(1.0 + jnp.exp(-jnp.abs(z)))


def kvprep_fwd(kvf, kg, fb):
    s = kvf.shape[0]
    tm = _tok(s)

    def body(kvf_ref, kg_ref, fb_ref, k_ref, v_ref, clf_ref, carry):
        m0 = _lane_mask0((1, LANES))

        @pl.when(pl.program_id(0) == 0)
        def _():
            carry[...] = jnp.zeros_like(carry)

        for j in range(KV_MAIN // LANES):
            u, _ = _rms64(kvf_ref[:, j * LANES : (j + 1) * LANES], m0)
            k_ref[:, j * LANES : (j + 1) * LANES] = bf(u * kg_ref[...])
        v_ref[...] = bf(kvf_ref[:, KV_MAIN : 2 * KV_MAIN])
        lf = _log_sigmoid(kvf_ref[:, 2 * KV_MAIN :] + fb_ref[...])
        clf_ref[...] = dot_nn(_tri(tm), lf, HI) + carry[...]
        carry[...] += jnp.sum(lf, axis=0, keepdims=True)

    n = kvf.shape[1]
    return pl.pallas_call(
        body,
        name="kvprep_fwd",
        grid=(s // tm,),
        in_specs=[pl.BlockSpec((tm, n), lambda i: (i, 0)), _const((1, LANES)), _const((1, LANES))],
        out_specs=[pl.BlockSpec((tm, KV_MAIN), lambda i: (i, 0))] * 2 + [pl.BlockSpec((tm, LANES), lambda i: (i, 0))],
        out_shape=[jax.ShapeDtypeStruct((s, KV_MAIN), BF16)] * 2 + [jax.ShapeDtypeStruct((s, LANES), F32)],
        scratch_shapes=[pltpu.VMEM((1, LANES), F32)],
        compiler_params=_cparams(("arbitrary",)),
    )(kvf, kg, fb)


def kvprep_bwd(kvf, kg, fb, dk, dv, dclf):
    s, n = kvf.shape
    tm = _tok(s)
    nb = s // tm

    def body(kvf_ref, kg_ref, fb_ref, dk_ref, dv_ref, dclf_ref, o_ref, dkg_ref, dfb_ref, carry):
        m0 = _lane_mask0((1, LANES))

        @pl.when(pl.program_id(0) == 0)
        def _():
            carry[...] = jnp.zeros_like(carry)
            dkg_ref[...] = jnp.zeros_like(dkg_ref)
            dfb_ref[...] = jnp.zeros_like(dfb_ref)

        kg_ = kg_ref[...]
        for j in range(KV_MAIN // LANES):
            cols = slice(j * LANES, (j + 1) * LANES)
            u, r = _rms64(kvf_ref[:, cols], m0)
            dkr, dkg = _rms64_bwd(dk_ref[:, cols], u, r, kg_, m0)
            o_ref[:, cols] = dkr
            dkg_ref[...] += dkg
        o_ref[:, KV_MAIN : 2 * KV_MAIN] = dv_ref[...]
        z = kvf_ref[:, 2 * KV_MAIN :] + fb_ref[...]
        dc = dclf_ref[...]
        dlf = dot_nn(_tri(tm, upper=True), dc, HI) + carry[...]
        carry[...] += jnp.sum(dc, axis=0, keepdims=True)
        dz = dlf * _sigmoid(-z)
        o_ref[:, 2 * KV_MAIN :] = dz
        dfb_ref[...] += jnp.sum(dz, axis=0, keepdims=True)

    rev = lambda i: (nb - 1 - i, 0)
    return pl.pallas_call(
        body,
        name="kvprep_bwd",
        grid=(nb,),
        in_specs=[pl.BlockSpec((tm, n), rev), _const((1, LANES)), _const((1, LANES)), pl.BlockSpec((tm, KV_MAIN), rev),
                  pl.BlockSpec((tm, KV_MAIN), rev), pl.BlockSpec((tm, LANES), rev)],
        out_specs=[pl.BlockSpec((tm, n), rev), _const((1, LANES)), _const((1, LANES))],
        out_shape=[jax.ShapeDtypeStruct((s, n), F32), jax.ShapeDtypeStruct((1, LANES), F32), jax.ShapeDtypeStruct((1, LANES), F32)],
        scratch_shapes=[pltpu.VMEM((1, LANES), F32)],
        compiler_params=_cparams(("arbitrary",)),
    )(kvf, kg, fb, dk, dv, dclf)


FOX_SCALE = HEAD64**-0.5


def _lane_col(block, lane_idx, h):
    return jnp.sum(jnp.where(lane_idx == h, block, 0.0), axis=-1, keepdims=True)


def _causal(tq, ext, i, transposed=False):
    if transposed:
        key = lax.broadcasted_iota(jnp.int32, (ext, tq), 0)
        qry = lax.broadcasted_iota(jnp.int32, (ext, tq), 1) + i * tq
    else:
        qry = lax.broadcasted_iota(jnp.int32, (tq, ext), 0) + i * tq
        key = lax.broadcasted_iota(jnp.int32, (tq, ext), 1)
    return key <= qry


def fox_fwd(proj, k_sh, v_sh, clf, clf_t, qg3, j_layer, comm=None):
    s = proj.shape[0]
    npair = 6
    tq = TQ if s % TQ == 0 else s
    nq = s // tq

    def body(q_ref, gate_ref, k_ref, v_ref, clf_ref, clft_ref, qg_ref, main_ref, o_ref, lse_ref):
        j = pl.program_id(0)
        lane = lax.broadcasted_iota(jnp.int32, (1, LANES), 1)
        m0 = lane < HEAD64
        u, _ = _rms64(q_ref[...], m0)
        qn = u * qg_ref[...] * FOX_SCALE
        clfv = clf_ref[...]
        for hh in range(2):
            h = 2 * j + hh
            mh = m0 if hh == 0 else jnp.logical_not(m0)
            qh = bf(jnp.where(mh, qn, 0.0))
            dcol = _lane_col(clfv, lane, h)
            drow = clft_ref[pl.ds(h, 1), :]
            for i in range(nq):
                rows = slice(i * tq, (i + 1) * tq)
                ext = (i + 1) * tq
                sc = dot_nt(qh[rows], k_ref[0:ext, :]) + dcol[rows] - drow[:, :ext]
                sc = jnp.where(_causal(tq, ext, i), sc, -jnp.inf)
                m = jnp.max(sc, axis=-1, keepdims=True)
                p = jnp.exp(sc - m)
                lsum = jnp.sum(p, axis=-1, keepdims=True)
                pv = dot_nn(bf(p), v_ref[0:ext, :]) / lsum
                lse = m + jnp.log(lsum)
                if hh == 0:
                    o_ref[rows, :] = pv
                    lse_ref[rows, :] = jnp.where(lane == 0, lse, 0.0)
                else:
                    o_ref[rows, :] = jnp.where(mh, pv, o_ref[rows, :])
                    lse_ref[rows, :] = jnp.where(lane == 1, lse, lse_ref[rows, :])
        main_ref[...] = bf(o_ref[...] * _sigmoid(gate_ref[...]))

    blk = lambda off: pl.BlockSpec((s, LANES), lambda j: (0, off + j))
    return _carry(
        body,
        comm,
        name="fox_fwd",
        grid=(npair,),
        in_specs=[blk(0), blk(npair), blk(0), blk(0), _const((s, LANES)), _const((16, s)),
                  pl.BlockSpec((None, 1, LANES), lambda j: (j_layer, 0, 0))],
        out_specs=[blk(0)] * 3,
        out_shape=[jax.ShapeDtypeStruct((s, npair * LANES), BF16)] + [jax.ShapeDtypeStruct((s, npair * LANES), F32)] * 2,
        scratch_shapes=[],
        args=(proj, proj, k_sh, v_sh, clf, clf_t, qg3),
    )


def fox_bwd(proj, k_sh, v_sh, clf, clf_t, qg3, j_layer, o, lse, lse_t, dmixed, dk_in, dv_in, dclf_in):
    s = proj.shape[0]
    npair = 6
    tq = TQ if s % TQ == 0 else s
    nq = s // tq

    def body(q_ref, gate_ref, k_ref, v_ref, clf_ref, clft_ref, qg_ref, o_ref, lse_ref, lset_ref, dm_ref, dkin_ref, dvin_ref, dclfin_ref,
             dq_ref, dgate_ref, dk_ref, dv_ref, dclf_ref, dqg_ref, dqn_s, dcl_s):
        j = pl.program_id(0)
        lane = lax.broadcasted_iota(jnp.int32, (1, LANES), 1)
        m0 = lane < HEAD64
        qg = qg_ref[...]
        u, r = _rms64(q_ref[...], m0)
        qn = u * qg * FOX_SCALE
        ov = o_ref[...]
        gate = gate_ref[...]
        sg = _sigmoid(gate)
        dmain = dm_ref[...]
        do = dmain * sg
        dgate_ref[...] = dmain * ov * sg * (1.0 - sg)
        dk_ref[...] = dkin_ref[...]
        dv_ref[...] = dvin_ref[...]
        clfv = clf_ref[...]
        lsev = lse_ref[...]
        ones8 = jnp.ones((8, LANES), F32)

        @pl.when(j == 0)
        def _():
            dclf_ref[...] = dclfin_ref[...]
            dqg_ref[...] = jnp.zeros_like(dqg_ref)

        for hh in range(2):
            h = 2 * j + hh
            mh = m0 if hh == 0 else jnp.logical_not(m0)
            qh = bf(jnp.where(mh, qn, 0.0))
            doh = jnp.where(mh, do, 0.0)
            dohb = bf(doh)
            doo = doh * ov
            dcol = _lane_col(clfv, lane, h)
            drow = clft_ref[pl.ds(h, 1), :]
            lcol = _lane_col(lsev, lane, hh)
            lrow = lset_ref[pl.ds(h, 1), :]
            delta = jnp.sum(doo, axis=-1, keepdims=True)
            dcl_s[...] = jnp.zeros_like(dcl_s)
            for i in range(nq):
                rows = slice(i * tq, (i + 1) * tq)
                ext = (i + 1) * tq
                kk, vv = k_ref[0:ext, :], v_ref[0:ext, :]
                sc = dot_nt(qh[rows], kk) + dcol[rows] - drow[:, :ext]
                p = jnp.where(_causal(tq, ext, i), jnp.exp(sc - lcol[rows]), 0.0)
                ds = p * (dot_nt(dohb[rows], vv) - delta[rows])
                dqh = dot_nn(bf(ds), kk) * FOX_SCALE
                if hh == 0:
                    dqn_s[rows, :] = dqh
                else:
                    dqn_s[rows, :] = jnp.where(mh, dqh, dqn_s[rows, :])
                dcl_s[rows, :] += jnp.sum(ds, axis=-1, keepdims=True)
                sct = dot_nt(kk, qh[rows]) + drow[:, rows] - dcol[:ext]
                pt = jnp.where(_causal(tq, ext, i, transposed=True), jnp.exp(sct - lrow[:, rows]), 0.0)
                delta_row = dot_nt(ones8, doo[rows], HI)[0:1]
                dst = pt * (dot_nt(vv, dohb[rows]) - delta_row)
                dv_ref[0:ext, :] += dot_nn(bf(pt), dohb[rows])
                dk_ref[0:ext, :] += dot_nn(bf(dst), qh[rows])
                dcl_s[0:ext, :] -= jnp.sum(dst, axis=-1, keepdims=True)
            dclf_ref[...] += jnp.where(lane == h, dcl_s[...], 0.0)
        dqr, dqg = _rms64_bwd(dqn_s[...], u, r, qg, m0)
        dq_ref[...] = dqr
        dqg_ref[...] += dqg

    blk = lambda off: pl.BlockSpec((s, LANES), lambda j: (0, off + j))
    full = _const((s, LANES))
    return pl.pallas_call(
        body,
        name="fox_bwd",
        grid=(npair,),
        in_specs=[blk(0), blk(npair), blk(0), blk(0), full, _const((16, s)), pl.BlockSpec((None, 1, LANES), lambda j: (j_layer, 0, 0)),
                  blk(0), blk(0), _const((16, s)), blk(0), blk(0), blk(0), full],
        out_specs=[blk(0)] * 4 + [full, _const((1, LANES))],
        out_shape=[jax.ShapeDtypeStruct((s, npair * LANES), F32)] * 4
        + [jax.ShapeDtypeStruct((s, LANES), F32), jax.ShapeDtypeStruct((1, LANES), F32)],
        scratch_shapes=[pltpu.VMEM((s, LANES), F32), pltpu.VMEM((s, LANES), F32)],
        compiler_params=_cparams(("arbitrary",)),
    )(proj, proj, k_sh, v_sh, clf, clf_t, qg3, o, lse, lse_t, dmixed, dk_in, dv_in, dclf_in)


def loss_head(y, target):
    s, d = y.shape
    tm = _tok(s)

    def body(y_ref, t_ref, loss_ref, dy_ref):
        err = y_ref[...] - t_ref[...]
        dy_ref[...] = err * (1.0 / d)

        @pl.when(pl.program_id(0) == 0)
        def _():
            loss_ref[...] = jnp.zeros_like(loss_ref)

        part = jnp.sum(jnp.mean(err * err, axis=-1, keepdims=True), axis=0, keepdims=True)
        loss_ref[...] += 0.5 * part

    row = pl.BlockSpec((tm, d), lambda i: (i, 0))
    return pl.pallas_call(
        body,
        name="loss_head",
        grid=(s // tm,),
        in_specs=[row, row],
        out_specs=[_const((1, 1)), row],
        out_shape=[jax.ShapeDtypeStruct((1, 1), F32), jax.ShapeDtypeStruct((s, d), F32)],
        compiler_params=_cparams(("arbitrary",)),
    )(y, target)


def _row_tile(r, c, n_arrays):
    budget = VMEM_LIMIT_BYTES // 2
    padded_c = -(-c // LANES) * LANES
    best = None
    for t in range(8, r + 1, 8):
        if r % t == 0 and 2 * n_arrays * t * padded_c * 4 <= budget:
            best = t
    return r if best is None else best


def _as2d(a):
    return a.reshape(-1, a.shape[-1]) if a.ndim >= 2 else a.reshape(1, -1)


def adamw(w, gs, m, v):
    shape = w.shape
    w2, m2, v2 = (_as2d(t) for t in (w, m, v))
    rows, c = w2.shape
    gs = [g.reshape(-1, c) for g in gs]
    assert sum(g.shape[0] for g in gs) == rows
    tr = _row_tile(min(g.shape[0] for g in gs), c, 8)
    assert all(g.shape[0] % tr == 0 for g in gs)
    c1 = 1.0 - ADAM_B1**ADAM_STEP
    c2 = 1.0 - ADAM_B2**ADAM_STEP
    outs = None
    first = 0
    for g in gs:
        n_prev = 0 if outs is None else 4
        r = g.shape[0]

        def body(w_ref, g_ref, m_ref, v_ref, *rest, n_prev=n_prev):
            go_ref, d_ref, nm_ref, nv_ref = rest[n_prev:]
            gv = g_ref[...]
            nm = ADAM_B1 * m_ref[...] + (1.0 - ADAM_B1) * gv
            nv = ADAM_B2 * v_ref[...] + (1.0 - ADAM_B2) * (gv * gv)
            go_ref[...] = gv
            nm_ref[...] = nm
            nv_ref[...] = nv
            d_ref[...] = -ADAM_LR * ((nm / c1) / (jnp.sqrt(nv / c2) + ADAM_EPS) + ADAM_WD * w_ref[...])

        spec = pl.BlockSpec((tr, c), lambda i, b0=first // tr: (b0 + i, 0))
        outs = pl.pallas_call(
            body,
            name="adamw",
            grid=(r // tr,),
            in_specs=[spec, pl.BlockSpec((tr, c), lambda i: (i, 0)), spec, spec] + [ANY] * n_prev,
            out_specs=[spec] * 4,
            out_shape=[jax.ShapeDtypeStruct((rows, c), F32)] * 4,
            input_output_aliases={4 + i: i for i in range(n_prev)},
            compiler_params=_cparams(("arbitrary",)),
        )(w2, g, m2, v2, *([] if outs is None else outs))
        first += r
    return tuple(t.reshape(shape) for t in outs)


def pair_sum(g, recv, c_arr):
    _, k, r, c = g.shape
    tr = _row_tile(r, c, 3)

    def body(c_ref, g_ref, r_ref, o_ref):
        o_ref[...] = bf(g_ref[...] + r_ref[...])

    return pl.pallas_call(
        body,
        name="pair_sum",
        grid_spec=pltpu.PrefetchScalarGridSpec(
            num_scalar_prefetch=1,
            grid=(k, r // tr),
            in_specs=[pl.BlockSpec((None, None, tr, c), lambda kk, i, cr: (cr[0], kk, i, 0)), pl.BlockSpec((None, tr, c), lambda kk, i, cr: (kk, i, 0))],
            out_specs=pl.BlockSpec((None, tr, c), lambda kk, i, cr: (kk, i, 0)),
        ),
        out_shape=jax.ShapeDtypeStruct((k, r, c), BF16),
        compiler_params=_cparams(("arbitrary", "arbitrary")),
    )(c_arr, g, recv)


def chip_sum(p, q, sel):
    _, r, c = p.shape
    tr = _row_tile(r, c, 4)

    def body(sel_ref, p_ref, q_ref, o_ref):
        acc = p_ref[...].astype(F32)
        for i in range(q.shape[0]):
            acc = acc + q_ref[i].astype(F32)
        o_ref[...] = acc

    return pl.pallas_call(
        body,
        name="chip_sum",
        grid_spec=pltpu.PrefetchScalarGridSpec(
            num_scalar_prefetch=1,
            grid=(r // tr,),
            in_specs=[pl.BlockSpec((None, tr, c), lambda i, sr: (sr[0], i, 0)), pl.BlockSpec((q.shape[0], tr, c), lambda i, sr: (0, i, 0))],
            out_specs=pl.BlockSpec((None, tr, c), lambda i, sr: (sr[1], i, 0)),
        ),
        out_shape=jax.ShapeDtypeStruct((2, r, c), F32),
        compiler_params=_cparams(("arbitrary",)),
    )(sel, p, q)


def cast_into_slot(w4, g, sel, dtype):
    _, _, r, c = w4.shape
    tr = _row_tile(r, c, 2)

    def body(sel_ref, w_ref, o_ref):
        o_ref[...] = w_ref[...].astype(dtype)

    return pl.pallas_call(
        body,
        name="cast_into_slot",
        grid_spec=pltpu.PrefetchScalarGridSpec(
            num_scalar_prefetch=1,
            grid=(2, r // tr),
            in_specs=[pl.BlockSpec((None, None, tr, c), lambda hf, i, sr: (g, hf, i, 0))],
            out_specs=pl.BlockSpec((None, None, tr, c), lambda hf, i, sr: (sr[0], hf, i, 0)),
        ),
        out_shape=jax.ShapeDtypeStruct((N_CHIPS, 2, r, c), dtype),
        compiler_params=_cparams(("arbitrary", "arbitrary")),
    )(sel, w4)


def _place():
    x, y, c = lax.axis_index("x"), lax.axis_index("y"), lax.axis_index("c")
    chips = [(1 - x, y), (x, 1 - y), (1 - x, 1 - y)]
    return x, y, c, 2 * x + y, chips, [2 * cx + cy for cx, cy in chips]


def _rcopy(src, dst, send, recv, dev):
    return pltpu.make_async_remote_copy(src_ref=src, dst_ref=dst, send_sem=send, recv_sem=recv, device_id=dev, device_id_type=MESH)


class Gather:
    def __init__(self, bufs):
        n = len(bufs)
        self.n = n
        self.args = list(bufs)
        self.out_shape = [jax.ShapeDtypeStruct(t.shape, t.dtype) for t in bufs]
        self.aliases = {a: a for a in range(n)}
        self.scratch = [pltpu.SemaphoreType.DMA((n, 6)), pltpu.SemaphoreType.DMA((n, 6))]

    def _sends(self, outs, send, recv):
        x, y, c, me, chips, _ = _place()
        cps = []
        for a in range(self.n):
            mine = outs[a].at[me, c]
            cps += [_rcopy(mine, mine, send.at[a, j], recv.at[a, j], (*chips[j], c)) for j in range(3)]
        return cps

    def start(self, ins, outs, scr):
        for cp in self._sends(outs, *scr):
            cp.start()

    def finish(self, ins, outs, scr):
        send, recv = scr
        x, y, c, me, chips, cidx = _place()
        sib = (x, y, 1 - c)
        passed = []
        for a in range(self.n):
            for j in range(3):
                landed = outs[a].at[cidx[j], c]
                _rcopy(landed, landed, send.at[a, j], recv.at[a, j], (*chips[j], c)).wait_recv()
                fwd = _rcopy(landed, landed, send.at[a, 3 + j], recv.at[a, 3 + j], sib)
                fwd.start()
                passed.append(fwd)
        for a in range(self.n):
            for j in range(3):
                theirs = outs[a].at[cidx[j], 1 - c]
                _rcopy(theirs, theirs, send.at[a, 3 + j], recv.at[a, 3 + j], sib).wait_recv()
        for cp in self._sends(outs, send, recv) + passed:
            cp.wait_send()


class PairExchange:
    def __init__(self, gs):
        n = len(gs)
        self.n = n
        self.args = list(gs)
        self.out_shape = [jax.ShapeDtypeStruct(t.shape[1:], t.dtype) for t in gs]
        self.aliases = {}
        self.scratch = [pltpu.SemaphoreType.DMA((n,)), pltpu.SemaphoreType.DMA((n,))]

    def _copies(self, ins, outs, send, recv):
        x, y, c = lax.axis_index("x"), lax.axis_index("y"), lax.axis_index("c")
        return [_rcopy(ins[a].at[1 - c], outs[a], send.at[a], recv.at[a], (x, y, 1 - c)) for a in range(self.n)]

    def start(self, ins, outs, scr):
        for cp in self._copies(ins, outs, *scr):
            cp.start()

    def finish(self, ins, outs, scr):
        for cp in self._copies(ins, outs, *scr):
            cp.wait()


class ChipExchange:
    def __init__(self, ps):
        n = len(ps)
        self.n = n
        self.args = list(ps)
        self.out_shape = [jax.ShapeDtypeStruct((3,) + t.shape[1:], t.dtype) for t in ps]
        self.aliases = {}
        self.scratch = [pltpu.SemaphoreType.DMA((n, 3)), pltpu.SemaphoreType.DMA((n, 3))]

    def _sends(self, ins, outs, send, recv):
        x, y, c, me, chips, cidx = _place()
        return [
            _rcopy(ins[a].at[cidx[j]], outs[a].at[j], send.at[a, j], recv.at[a, j], (*chips[j], c))
            for a in range(self.n)
            for j in range(3)
        ]

    def start(self, ins, outs, scr):
        for cp in self._sends(ins, outs, *scr):
            cp.start()

    def finish(self, ins, outs, scr):
        send, recv = scr
        x, y, c, me, chips, _ = _place()
        for a in range(self.n):
            for j in range(3):
                landed = outs[a].at[j]
                _rcopy(landed, landed, send.at[a, j], recv.at[a, j], (*chips[j], c)).wait_recv()
        for cp in self._sends(ins, outs, send, recv):
            cp.wait_send()


class PairShare:
    def __init__(self, bufs):
        n = len(bufs)
        self.n = n
        self.args = list(bufs)
        self.out_shape = [jax.ShapeDtypeStruct(t.shape, t.dtype) for t in bufs]
        self.aliases = {a: a for a in range(n)}
        self.scratch = [pltpu.SemaphoreType.DMA((n,)), pltpu.SemaphoreType.DMA((n,))]

    def _sends(self, outs, send, recv):
        x, y, c = lax.axis_index("x"), lax.axis_index("y"), lax.axis_index("c")
        return [_rcopy(outs[a].at[c], outs[a].at[c], send.at[a], recv.at[a], (x, y, 1 - c)) for a in range(self.n)]

    def start(self, ins, outs, scr):
        for cp in self._sends(outs, *scr):
            cp.start()

    def finish(self, ins, outs, scr):
        send, recv = scr
        x, y, c = lax.axis_index("x"), lax.axis_index("y"), lax.axis_index("c")
        for a in range(self.n):
            theirs = outs[a].at[1 - c]
            _rcopy(theirs, theirs, send.at[a], recv.at[a], (x, y, 1 - c)).wait_recv()
        for cp in self._sends(outs, send, recv):
            cp.wait_send()


class Multi:
    def __init__(self, comms):
        self.comms = comms
        self.args, self.out_shape, self.scratch, self.aliases = [], [], [], {}
        self.spans = []
        for cm in comms:
            a0, o0, s0 = len(self.args), len(self.out_shape), len(self.scratch)
            self.aliases.update({a0 + i: o0 + o for i, o in cm.aliases.items()})
            self.args += cm.args
            self.out_shape += cm.out_shape
            self.scratch += cm.scratch
            self.spans.append((slice(a0, len(self.args)), slice(o0, len(self.out_shape)), slice(s0, len(self.scratch))))

    def start(self, ins, outs, scr):
        for cm, (sa, so, ss) in zip(self.comms, self.spans):
            cm.start(ins[sa], outs[so], scr[ss])

    def finish(self, ins, outs, scr):
        for cm, (sa, so, ss) in zip(self.comms, self.spans):
            cm.finish(ins[sa], outs[so], scr[ss])

    def split(self, res):
        return [list(res[so]) for _, so, _ in self.spans]


def run_comm(comm, name):
    na, no = len(comm.args), len(comm.out_shape)

    def body(*refs):
        ins, outs, scr = refs[:na], refs[na : na + no], refs[na + no :]
        comm.start(ins, outs, scr)
        comm.finish(ins, outs, scr)

    return pl.pallas_call(
        body,
        name=name,
        in_specs=[ANY] * na,
        out_specs=[ANY] * no,
        out_shape=comm.out_shape,
        input_output_aliases=comm.aliases,
        scratch_shapes=comm.scratch,
    )(*comm.args)


def _carry(body, comm, *, name, grid, in_specs, out_specs, out_shape, scratch_shapes, args):
    params = _cparams(("arbitrary",))
    if comm is None:
        res = pl.pallas_call(body, name=name, grid=grid, in_specs=in_specs, out_specs=out_specs, out_shape=out_shape,
                             scratch_shapes=scratch_shapes, compiler_params=params)(*args)
        return res, None
    ni, no, ns = len(in_specs), len(out_specs), len(scratch_shapes)
    ci, co = len(comm.args), len(comm.out_shape)

    def wrapped(*refs):
        ins, c_ins = refs[:ni], refs[ni : ni + ci]
        p = ni + ci
        outs, c_outs = refs[p : p + no], refs[p + no : p + no + co]
        p += no + co
        scr, c_scr = refs[p : p + ns], refs[p + ns :]

        @pl.when(pl.program_id(0) == 0)
        def _():
            comm.start(c_ins, c_outs, c_scr)

        body(*ins, *outs, *scr)

        @pl.when(pl.program_id(0) == grid[0] - 1)
        def _():
            comm.finish(c_ins, c_outs, c_scr)

    res = pl.pallas_call(
        wrapped,
        name=name + "_carry",
        grid=grid,
        in_specs=list(in_specs) + [ANY] * ci,
        out_specs=list(out_specs) + [ANY] * co,
        out_shape=list(out_shape) + list(comm.out_shape),
        input_output_aliases={ni + i: no + o for i, o in comm.aliases.items()},
        scratch_shapes=list(scratch_shapes) + list(comm.scratch),
        compiler_params=params,
    )(*args, *comm.args)
    return res[:no], res[no:]


def small_allreduce(buf):
    r = buf.shape[0]

    def body(b_ref, o_ref, slots, send, recv):
        x, y, c = lax.axis_index("x"), lax.axis_index("y"), lax.axis_index("c")
        me = 4 * x + 2 * y + c
        slots[me] = b_ref[...]
        cps = []
        peers = []
        for mask in range(1, N_DEV):
            fx, fy, fc = (mask >> 2) & 1, (mask >> 1) & 1, mask & 1
            px, py, pc = (1 - x if fx else x), (1 - y if fy else y), (1 - c if fc else c)
            peers.append(4 * px + 2 * py + pc)
            cps.append(_rcopy(b_ref, slots.at[me], send.at[mask - 1], recv.at[mask - 1], (px, py, pc)))
        for cp in cps:
            cp.start()
        for k, pid in enumerate(peers):
            landed = slots.at[pid]
            _rcopy(landed, landed, send.at[k], recv.at[k], (x, y, c)).wait_recv()
        for cp in cps:
            cp.wait_send()
        acc = slots[0]
        for i in range(1, N_DEV):
            acc = acc + slots[i]
        o_ref[...] = acc

    vm = pl.BlockSpec(memory_space=pltpu.VMEM)
    return pl.pallas_call(
        body,
        name="small_allreduce",
        in_specs=[vm],
        out_specs=vm,
        out_shape=jax.ShapeDtypeStruct(buf.shape, F32),
        scratch_shapes=[pltpu.VMEM((N_DEV, r, LANES), F32), pltpu.SemaphoreType.DMA((N_DEV - 1,)), pltpu.SemaphoreType.DMA((N_DEV - 1,))],
    )(buf)


WEIGHT_NAMES = ["ffn1_norm", "ffn1_w_gate", "ffn1_w_up", "ffn1_w_down", "mix_norm", "mem_norm", "w_mem_kv", "mem_q_gain",
                "mem_k_gain", "w_in_a", "hgrn_lb_logits", "hgrn_o_gain", "w_in_b", "fox_q_gain", "kv_norm", "w_kv", "fox_f_bias",
                "fox_k_gain", "w_out", "ffn2_norm", "ffn2_w_gate", "ffn2_w_up", "ffn2_w_down"]
SHARDED = ["ffn1_w_gate", "ffn1_w_up", "ffn1_w_down", "w_mem_kv", "w_in_a", "w_in_b", "w_kv", "w_out", "ffn2_w_gate", "ffn2_w_up", "ffn2_w_down"]
SMALL = [n for n in WEIGHT_NAMES if n not in SHARDED]
FFN1 = ["ffn1_w_gate", "ffn1_w_up", "ffn1_w_down"]
FFN2 = ["ffn2_w_gate", "ffn2_w_up", "ffn2_w_down"]
PER_LAYER = FFN1 + FFN2 + ["w_mem_kv", "w_out"]
TRANSPOSED = ["ffn1_w_gate", "ffn1_w_up", "ffn2_w_gate", "ffn2_w_up", "w_in_a", "w_in_b"]
N_LAYERS, N_A = 4, 2
KV_PAD = 13 * LANES


def _halves(t):
    return t.reshape((2, t.shape[0] // 2) + t.shape[1:])


def _cols_from_chips(g):
    return jnp.moveaxis(g, 0, 2).reshape(g.shape[1], g.shape[2], N_CHIPS * g.shape[3])


def _rows_from_chips(g):
    return jnp.moveaxis(g, 0, 1).reshape(g.shape[1], N_CHIPS * g.shape[2], g.shape[3])


def _pair_tile(g):
    return jnp.tile(g, (1, 2)).reshape(g.shape[0], 1, LANES)


def _pair_fold(g):
    return g[:, :HEAD64] + g[:, HEAD64:]


def kernel(x, mem, ffn1_norm, ffn1_w_gate, ffn1_w_up, ffn1_w_down, mix_norm, mem_norm, w_mem_kv, mem_q_gain, mem_k_gain, w_in_a, hgrn_lb_logits, hgrn_o_gain, w_in_b, fox_q_gain, kv_norm, w_kv, fox_f_bias, fox_k_gain, w_out, ffn2_norm, ffn2_w_gate, ffn2_w_up, ffn2_w_down, loss_target, m_ffn1_norm, m_ffn1_w_gate, m_ffn1_w_up, m_ffn1_w_down, m_mix_norm, m_mem_norm, m_w_mem_kv, m_mem_q_gain, m_mem_k_gain, m_w_in_a, m_hgrn_lb_logits, m_hgrn_o_gain, m_w_in_b, m_fox_q_gain, m_kv_norm, m_w_kv, m_fox_f_bias, m_fox_k_gain, m_w_out, m_ffn2_norm, m_ffn2_w_gate, m_ffn2_w_up, m_ffn2_w_down, v_ffn1_norm, v_ffn1_w_gate, v_ffn1_w_up, v_ffn1_w_down, v_mix_norm, v_mem_norm, v_w_mem_kv, v_mem_q_gain, v_mem_k_gain, v_w_in_a, v_hgrn_lb_logits, v_hgrn_o_gain, v_w_in_b, v_fox_q_gain, v_kv_norm, v_w_kv, v_fox_f_bias, v_fox_k_gain, v_w_out, v_ffn2_norm, v_ffn2_w_gate, v_ffn2_w_up, v_ffn2_w_down):
    given = dict(locals())
    def oriented(n, t):
        return jnp.swapaxes(t, 1, 2) if n in TRANSPOSED else t

    w = {n: oriented(n, given[n]) for n in WEIGHT_NAMES}
    xs, mems, tgt = x[0], mem[0], loss_target[0]
    s, d = xs.shape
    my_chip = 2 * lax.axis_index("x") + lax.axis_index("y")
    sel = jnp.stack([my_chip, lax.axis_index("c")]).astype(jnp.int32)
    c_arr = sel[1:]

    def w_in_name(l):
        return "w_in_a" if l < N_A else "w_in_b"

    def cast(n, l):
        t = w[n]
        rows, cols = t.shape[-2:]
        own = 0 if t.ndim == 2 else (l - N_A if n == "w_in_b" else l)
        return cast_into_slot(t.reshape(-1, 2, rows // 2, cols), own, sel, BF16)

    def view(buf, n):
        rows, cols = w[n].shape[-2:]
        return buf.reshape(N_CHIPS, rows, cols) if w[n].ndim == 2 else buf.reshape(N_CHIPS, 1, rows, cols)

    def mixer(l):
        return [(w_in_name(l), l), ("w_mem_kv", l), ("w_out", l)]

    first = [(n, 0) for n in PER_LAYER] + [("w_in_a", 0), ("w_kv", 0)]
    carried = {
        (0, "ffn1"): mixer(1), (0, "mix"): [(n, 1) for n in FFN1 + FFN2[:2]], (0, "ffn2"): [(FFN2[2], 1)],
        (1, "ffn1"): mixer(2), (1, "mix"): [(n, 2) for n in FFN1 + FFN2[:2]], (1, "ffn2"): [(FFN2[2], 2)],
        (2, "ffn1"): [(FFN1[0], 3)], (2, "mix"): [(FFN1[1], 3), (FFN1[2], 3)], (2, "ffn2"): [(FFN2[0], 3)] + mixer(3),
        (3, "ffn1"): [(FFN2[1], 3)], (3, "mix"): [(FFN2[2], 3)],
    }
    bufs = {it: cast(*it) for it in first + [it for items in carried.values() for it in items]}
    lb_buf = cast_into_slot(hgrn_lb_logits.reshape(1, 2, 1, -1), 0, sel, F32)
    got0 = run_comm(Gather([bufs[it] for it in first] + [lb_buf]), "gather_layer0")
    got = {it: view(b, it[0]) for it, b in zip(first, got0[:-1])}
    w_kv_full = _cols_from_chips(got[("w_kv", 0)][:, None])
    w_kv_full = jnp.pad(w_kv_full, ((0, 0), (0, 0), (0, KV_PAD - w_kv_full.shape[-1])))
    logits3 = jnp.moveaxis(got0[-1].reshape(N_CHIPS, 2, -1), 0, 1).reshape(2, 1, -1)
    lb3 = lb_fwd(logits3)
    w_in, w_mkv, w_o = {}, {}, {}

    def gather_behind(key):
        items = carried.get(key)
        return None if items is None else Gather([bufs[it] for it in items])

    def landed(key, res):
        if res is not None:
            got.update({it: view(b, it[0]) for it, b in zip(carried[key], res)})

    norm3 = {n: w[n].reshape(N_LAYERS, 1, d) for n in ("ffn1_norm", "mix_norm", "mem_norm", "ffn2_norm")}
    kvn3 = kv_norm.reshape(1, 1, d)
    mqg3, mkg3 = _pair_tile(mem_q_gain), _pair_tile(mem_k_gain)
    og3 = hgrn_o_gain.reshape(N_A, 1, LANES)
    fqg3 = _pair_tile(fox_q_gain)
    fkg = jnp.tile(fox_k_gain, 2).reshape(1, LANES)
    fb = jnp.pad(fox_f_bias, (0, LANES - fox_f_bias.shape[0])).reshape(1, LANES)

    sv = [dict() for _ in range(N_LAYERS)]
    h = xs
    kv = None
    for l in range(N_LAYERS):
        t = sv[l]
        t["x0"] = h
        (h, t["a1"], t["b1"]), res = ffn_fwd(h, norm3["ffn1_norm"], l, *[got[(n, l)] for n in FFN1], 0, comm=gather_behind((l, "ffn1")))
        landed((l, "ffn1"), res)
        t["x1"] = h
        w_in[l] = _rows_from_chips(got[(w_in_name(l), l)])
        t["proj"] = proj_fwd(h, norm3["mix_norm"], l, w_in[l], 0, wt=True)
        if l < N_A:
            (main, t["o"]), res = hgrn_fwd(t["proj"], lb3, og3, l, comm=gather_behind((l, "mix")))
            t["qblk"] = 12
        else:
            (main, t["o"], t["lse"]), res = fox_fwd(t["proj"], kv["k"], kv["v"], kv["clf"], kv["clf_t"], fqg3, l - N_A, comm=gather_behind((l, "mix")))
            t["qblk"] = 6
        landed((l, "mix"), res)
        w_mkv[l], w_o[l] = _rows_from_chips(got[("w_mem_kv", l)]), _rows_from_chips(got[("w_out", l)])
        t["kvm"] = proj_fwd(mems, norm3["mem_norm"], l, w_mkv[l], 0)
        memo = memattn_fwd(t["proj"], t["qblk"], t["kvm"], mqg3, mkg3, l)
        t["mixed"] = jnp.concatenate([main, memo], axis=-1)
        h = mm_res(h, t["mixed"], w_o[l], 0)
        t["x2"] = h
        (h, t["a2"], t["b2"]), res = ffn_fwd(h, norm3["ffn2_norm"], l, *[got[(n, l)] for n in FFN2], 0, comm=gather_behind((l, "ffn2")))
        landed((l, "ffn2"), res)
        if l == N_A - 1:
            kv = {"x": h, "kvf": proj_fwd(h, kvn3, 0, w_kv_full, 0)}
            kv["k"], kv["v"], kv["clf"] = kvprep_fwd(kv["kvf"], fkg, fb)
            kv["clf_t"] = kv["clf"][:, :16].T

    loss_local, dx = loss_head(h, tgt)

    nc = N_CHIPS
    fc = ffn1_w_down.shape[1]
    FOX_GROUP = "fox"
    gbuf1 = {n: lax.empty((2, nc, 1, fc, d), F32) for n in FFN1 + FFN2}
    gsplit = [dict(), dict()]

    def row_layout(stack):
        lw, rr, cc = stack.shape
        return stack.reshape(lw, nc, rr // nc, cc)

    def halves_layout(g):
        rr, cc = g.shape
        return jnp.transpose(g.reshape(nc, 2, rr // (2 * nc), cc), (1, 0, 2, 3))

    def group_layout(key):
        if key == FOX_GROUP:
            lay = {n: b.reshape(2, nc, fc, d) for n, b in gbuf1.items()}
            lay["w_mem_kv"] = row_layout(jnp.stack(dw_mkv[N_A:]))
            lay["w_out"] = row_layout(jnp.stack(dw_o[N_A:]))
            lay["w_in_b"] = row_layout(jnp.stack(dw_in["b"]))
            names = PER_LAYER + ["w_in_b"]
        else:
            lay = {n: b.reshape(2, nc, fc // 2, d) for n, b in gsplit[key].items()}
            lay["w_mem_kv"], lay["w_out"] = halves_layout(dw_mkv[key]), halves_layout(dw_o[key])
            lay["w_in_a"] = halves_layout(dw_in["a"][key])
            names = PER_LAYER + ["w_in_a"]
            if key == N_A - 1:
                kv_cols = w_kv.shape[-1] * nc
                lay["w_kv"] = jnp.transpose(dw_kv[:, :kv_cols].reshape(2, d // 2, nc, kv_cols // nc), (0, 2, 1, 3))
                names = names + ["w_kv"]
        return names, [lay[n] for n in names]

    n_ffn = len(FFN1) + len(FFN2)
    riding = {1: FOX_GROUP, 0: 1}
    reduced = {}
    unshared = None
    dw_in = {"a": [None] * N_A, "b": [None] * (N_LAYERS - N_A)}
    dw_o, dw_mkv = [None] * N_LAYERS, [None] * N_LAYERS
    sg = {n: [None] * N_LAYERS for n in ("ffn1_norm", "mix_norm", "mem_norm", "ffn2_norm", "mem_q_gain", "mem_k_gain")}
    sg["hgrn_o_gain"], sg["fox_q_gain"], dlb = [None] * N_A, [None] * (N_LAYERS - N_A), [None] * N_A
    dk_sh = jnp.zeros((s, KV_MAIN), F32)
    dv_sh = jnp.zeros((s, KV_MAIN), F32)
    dclf = jnp.zeros((s, LANES), F32)
    zero_mem = jnp.zeros(mems.shape, F32)
    dw_kv = None
    for l in reversed(range(N_LAYERS)):
        t = sv[l]
        gi, li = l // 2, l % 2
        if l == N_A - 1:
            dkvf, dfkg, dfb = kvprep_bwd(kv["kvf"], fkg, fb, dk_sh, dv_sh, dclf)
            dx, sg["kv_norm"], xn_kv, dpb = proj_bwd(kv["x"], kvn3, 0, [dkvf], w_kv_full, 0, dx)
            dw_kv = wgrad(xn_kv, dpb)
        ride = riding.get(l)
        comms = []
        if unshared is not None:
            comms.append(PairShare(unshared[2]))
        if ride is not None:
            names_r, gl_r = group_layout(ride)
            comms.append(PairExchange(gl_r))
        comm = Multi(comms) if comms else None
        (dx, da, db, hm, xn, dyb, sg["ffn2_norm"][l]), res = ffn_bwd(t["x2"], norm3["ffn2_norm"], l, dx, t["a2"], t["b2"], *[got[(n, l)] for n in FFN2], 0, comm=comm)
        if comm is not None:
            res = comm.split(res)
            if unshared is not None:
                reduced[unshared[0]] = dict(zip(unshared[1], res.pop(0)))
                unshared = None
            if ride is not None:
                partial_r = [pair_sum(g, r, c_arr) for g, r in zip(gl_r, res.pop(0))]

        def ffn_wgrads(which, da, db, hm, xn, dyb):
            for n, (a_, b_) in zip(which, ((da, xn), (db, xn), (hm, dyb))):
                if gi == 1:
                    gbuf1[n] = wgrad(a_, b_, buf=gbuf1[n], l=li)
                else:
                    gsplit[l][n] = wgrad(a_, b_, split=True)

        ffn_wgrads(FFN2, da, db, hm, xn, dyb)
        dmixed, dxb = mm_nt(dx, w_o[l], 0)
        dw_o[l] = wgrad(t["mixed"], dxb)
        dqm, dkvm, dmq, dmk = memattn_bwd(t["proj"], t["qblk"], t["kvm"], mqg3, mkg3, l, dmixed)
        sg["mem_q_gain"][l], sg["mem_k_gain"][l] = _pair_fold(dmq), _pair_fold(dmk)
        _, sg["mem_norm"][l], memn, dkvmb = proj_bwd(mems, norm3["mem_norm"], l, [dkvm], w_mkv[l], 0, zero_mem)
        dw_mkv[l] = wgrad(memn, dkvmb)
        if l < N_A:
            comm = ChipExchange(partial_r[:n_ffn]) if ride is not None else None
            (dzq, dzf, dvi, dzg, dlb[l], sg["hgrn_o_gain"][l]), res = hgrn_bwd(t["proj"], lb3, og3, l, t["o"], dmixed, comm=comm)
            if ride is not None:
                landed_r = list(res)
            parts, key, wl, tmw = [dzq, dzf, dvi, dzg, dqm], "a", l, 13 * LANES
        else:
            lse_t = t["lse"].reshape(s, 6, LANES)[:, :, :2].reshape(s, 12).T
            lse_t = jnp.pad(lse_t, ((0, 4), (0, 0)))
            dq, dgate, dk_sh, dv_sh, dclf, dfq = fox_bwd(t["proj"], kv["k"], kv["v"], kv["clf"], kv["clf_t"], fqg3, l - N_A, t["o"], t["lse"], lse_t, dmixed, dk_sh, dv_sh, dclf)
            sg["fox_q_gain"][l - N_A] = _pair_fold(dfq)
            parts, key, wl, tmw = [dq, dgate, dqm], "b", l - N_A, 7 * LANES
        dx, sg["mix_norm"][l], hn, dpb = proj_bwd(t["x1"], norm3["mix_norm"], l, parts, w_in[l], 0, dx, wt=True)
        dw_in[key][wl] = wgrad(dpb, hn, tm=tmw)
        comm = ChipExchange(partial_r[n_ffn:]) if ride is not None else None
        (dx, da, db, hm, xn, dyb, sg["ffn1_norm"][l]), res = ffn_bwd(t["x0"], norm3["ffn1_norm"], l, dx, t["a1"], t["b1"], *[got[(n, l)] for n in FFN1], 0, comm=comm)
        if ride is not None:
            unshared = (ride, names_r, [chip_sum(p, q, sel) for p, q in zip(partial_r, landed_r + list(res))])
        ffn_wgrads(FFN1, da, db, hm, xn, dyb)

    names0, gl0 = group_layout(0)
    recv0 = run_comm(PairExchange(gl0), "pair_exchange")
    partial0 = [pair_sum(g, r, c_arr) for g, r in zip(gl0, recv0)]
    landed0 = run_comm(ChipExchange(partial0), "chip_exchange")
    mine0 = [chip_sum(p, q, sel) for p, q in zip(partial0, landed0)]
    both = run_comm(PairShare(unshared[2] + mine0), "pair_share")
    reduced[unshared[0]] = dict(zip(unshared[1], both[: len(unshared[1])]))
    reduced[0] = dict(zip(names0, both[len(unshared[1]) :]))
    gparts = {n: [reduced[0][n], reduced[1][n], reduced[FOX_GROUP][n]] for n in PER_LAYER}
    gparts.update({"w_in_a": [reduced[0]["w_in_a"], reduced[1]["w_in_a"]], "w_kv": [reduced[1]["w_kv"]],
                   "w_in_b": [reduced[FOX_GROUP]["w_in_b"]]})

    dlogits = lb_bwd(logits3, dlb[1]).reshape(2, -1)
    small = {
        "ffn1_norm": jnp.concatenate(sg["ffn1_norm"]), "mix_norm": jnp.concatenate(sg["mix_norm"]),
        "mem_norm": jnp.concatenate(sg["mem_norm"]), "ffn2_norm": jnp.concatenate(sg["ffn2_norm"]),
        "mem_q_gain": jnp.concatenate(sg["mem_q_gain"]), "mem_k_gain": jnp.concatenate(sg["mem_k_gain"]),
        "hgrn_o_gain": jnp.concatenate(sg["hgrn_o_gain"]), "fox_q_gain": jnp.concatenate(sg["fox_q_gain"]),
        "kv_norm": sg["kv_norm"], "fox_f_bias": dfb[:, : fox_f_bias.shape[0]], "fox_k_gain": _pair_fold(dfkg),
        "hgrn_lb_logits": dlogits,
    }
    flat = [small[n].reshape(-1) for n in SMALL] + [loss_local.reshape(-1)]
    sizes = [f.shape[0] for f in flat]
    total = sum(sizes)
    padded = -(-total // (8 * LANES)) * (8 * LANES)
    packed = jnp.pad(jnp.concatenate(flat), (0, padded - total)).reshape(-1, LANES)
    summed = small_allreduce(packed).reshape(-1)
    off = 0
    for n, sz in zip(SMALL, sizes[:-1]):
        gparts[n] = [summed[off : off + sz].reshape(dlogits.shape if n == "hgrn_lb_logits" else w[n].shape)]
        off += sz
    loss = summed[off]
    lbw = hgrn_lb_logits.shape[1]
    gparts["hgrn_lb_logits"] = [lax.dynamic_slice_in_dim(gparts["hgrn_lb_logits"][0], my_chip * lbw, lbw, axis=1)]

    grads, delta, new_m, new_v = {}, {}, {}, {}
    for n in WEIGHT_NAMES:
        res = adamw(w[n], gparts[n], oriented(n, given["m_" + n]), oriented(n, given["v_" + n]))
        grads[n], delta[n], new_m[n], new_v[n] = (oriented(n, t) for t in res)
    return (loss, dx[None], *[grads[n] for n in WEIGHT_NAMES], *[delta[n] for n in WEIGHT_NAMES],
            *[new_m[n] for n in WEIGHT_NAMES], *[new_v[n] for n in WEIGHT_NAMES])
```

```python
import functools

import jax
import jax.numpy as jnp
from jax import lax
from jax.experimental import pallas as pl
from jax.experimental.pallas import tpu as pltpu

F32, BF16 = jnp.float32, jnp.bfloat16
HI = lax.Precision.HIGHEST
EPS = 1e-6
MESH = pl.DeviceIdType.MESH
ANY = pl.BlockSpec(memory_space=pl.ANY)

VMEM_LIMIT_BYTES = 56 << 20
N_CHIPS = 4
N_DEV = 8
LANES = 128
HEAD64 = 64
CHUNK = 64
SUB = 32
HGRN_HEADS_PER_STEP = 2
TQ = 256
TOK = 256

ADAM_LR, ADAM_B1, ADAM_B2, ADAM_EPS, ADAM_WD, ADAM_STEP = 0.001, 0.9, 0.999, 1e-08, 0.01, 10


def _cparams(sem=None, **kw):
    return pltpu.CompilerParams(dimension_semantics=sem, vmem_limit_bytes=VMEM_LIMIT_BYTES, **kw)


def _mm(a, b, dims, prec=None):
    return lax.dot_general(a, b, (dims, ((), ())), preferred_element_type=F32, precision=prec)


def dot_nn(a, b, prec=None):
    return _mm(a, b, ((1,), (0,)), prec)


def dot_nt(a, b, prec=None):
    return _mm(a, b, ((1,), (1,)), prec)


def dot_tn(a, b, prec=None):
    return _mm(a, b, ((0,), (0,)), prec)


def bf(v):
    return v.astype(BF16)


def _sigmoid(z):
    return jax.nn.sigmoid(z)


def _dsilu(z, s):
    return s * (1.0 + z * (1.0 - s))


def _rms(x):
    r = lax.rsqrt(jnp.mean(x * x, axis=-1, keepdims=True) + EPS)
    return x * r, r


def _rms_bwd(dxn, u, r, g):
    du = dxn * g
    dx = r * (du - u * jnp.mean(du * u, axis=-1, keepdims=True))
    return dx, jnp.sum(dxn * u, axis=0, keepdims=True)


def _lane_mask0(shape):
    return lax.broadcasted_iota(jnp.int32, shape, len(shape) - 1) < HEAD64


def _rms64(x, m0):
    sq = x * x
    s0 = jnp.sum(jnp.where(m0, sq, 0.0), axis=-1, keepdims=True)
    s1 = jnp.sum(jnp.where(m0, 0.0, sq), axis=-1, keepdims=True)
    r = lax.rsqrt(jnp.where(m0, s0, s1) * (1.0 / HEAD64) + EPS)
    return x * r, r


def _rms64_bwd(dxn, u, r, g, m0):
    du = dxn * g
    t = du * u
    t0 = jnp.sum(jnp.where(m0, t, 0.0), axis=-1, keepdims=True)
    t1 = jnp.sum(jnp.where(m0, 0.0, t), axis=-1, keepdims=True)
    dx = r * (du - u * (jnp.where(m0, t0, t1) * (1.0 / HEAD64)))
    return dx, jnp.sum(dxn * u, axis=0, keepdims=True)


def _tok(s):
    return TOK if s % TOK == 0 else s


def _const(shape):
    return pl.BlockSpec(shape, lambda *_: (0,) * len(shape))


def ffn_fwd(x, gain3, l, wg, wu, wd, wl, comm=None):
    s, d = x.shape
    nc, _, fc, _ = wg.shape
    tm = _tok(s)

    def body(x_ref, g_ref, wg_ref, wu_ref, wd_ref, xo_ref, a_ref, b_ref):
        xv = x_ref[...]
        u, _ = _rms(xv)
        xn = bf(u * g_ref[...])
        y = jnp.zeros((tm, d), F32)
        for c in range(nc):
            a = dot_nt(xn, wg_ref[c])
            b = dot_nt(xn, wu_ref[c])
            a_ref[c] = bf(a)
            b_ref[c] = bf(b)
            y = y + dot_nn(bf(a * _sigmoid(a) * b), wd_ref[c])
        xo_ref[...] = xv + 0.5 * y

    wspec = pl.BlockSpec((nc, None, fc, d), lambda i: (0, wl, 0, 0), pipeline_mode=pl.Buffered(1))
    wdspec = pl.BlockSpec((nc, None, fc, d), lambda i: (0, wl, 0, 0), pipeline_mode=pl.Buffered(1))
    row = pl.BlockSpec((tm, d), lambda i: (i, 0))
    act = pl.BlockSpec((nc, tm, fc), lambda i: (0, i, 0))
    return _carry(
        body,
        comm,
        name="ffn_fwd",
        grid=(s // tm,),
        in_specs=[row, pl.BlockSpec((None, 1, d), lambda i: (l, 0, 0)), wspec, wspec, wdspec],
        out_specs=[row, act, act],
        out_shape=[
            jax.ShapeDtypeStruct((s, d), F32),
            jax.ShapeDtypeStruct((nc, s, fc), BF16),
            jax.ShapeDtypeStruct((nc, s, fc), BF16),
        ],
        scratch_shapes=[],
        args=(x, gain3, wg, wu, wd),
    )


def ffn_bwd(x, gain3, l, dout, a, b, wg, wu, wd, wl, comm=None):
    s, d = x.shape
    nc, _, fc, _ = wg.shape
    tm = _tok(s)

    def body(x_ref, g_ref, do_ref, a_ref, b_ref, wg_ref, wu_ref, wd_ref, dx_ref, da_ref, db_ref, hm_ref, xn_ref, dy_ref, dg_ref):
        xv = x_ref[...]
        g = g_ref[...]
        u, r = _rms(xv)
        xn_ref[...] = bf(u * g)
        dout = do_ref[...]
        dy = bf(0.5 * dout)
        dy_ref[...] = dy
        dxn = jnp.zeros((tm, d), F32)
        for c in range(nc):
            av = a_ref[c].astype(F32)
            bv = b_ref[c].astype(F32)
            sg = _sigmoid(av)
            sl = av * sg
            dh = dot_nt(dy, wd_ref[c])
            da = bf(dh * bv * _dsilu(av, sg))
            db = bf(dh * sl)
            da_ref[c] = da
            db_ref[c] = db
            hm_ref[c] = bf(sl * bv)
            dxn = dxn + dot_nn(da, wg_ref[c]) + dot_nn(db, wu_ref[c])
        dx, dg = _rms_bwd(dxn, u, r, g)
        dx_ref[...] = dout + dx

        @pl.when(pl.program_id(0) == 0)
        def _():
            dg_ref[...] = jnp.zeros_like(dg_ref)

        dg_ref[...] += dg

    wspec = pl.BlockSpec((nc, None, fc, d), lambda i: (0, wl, 0, 0), pipeline_mode=pl.Buffered(1))
    wdspec = pl.BlockSpec((nc, None, fc, d), lambda i: (0, wl, 0, 0), pipeline_mode=pl.Buffered(1))
    row = pl.BlockSpec((tm, d), lambda i: (i, 0))
    act = pl.BlockSpec((nc, tm, fc), lambda i: (0, i, 0))
    act_shape = jax.ShapeDtypeStruct((nc, s, fc), BF16)
    return _carry(
        body,
        comm,
        name="ffn_bwd",
        grid=(s // tm,),
        in_specs=[row, pl.BlockSpec((None, 1, d), lambda i: (l, 0, 0)), row, act, act, wspec, wspec, wdspec],
        out_specs=[row, act, act, act, row, row, _const((1, d))],
        out_shape=[
            jax.ShapeDtypeStruct((s, d), F32),
            act_shape,
            act_shape,
            act_shape,
            jax.ShapeDtypeStruct((s, d), BF16),
            jax.ShapeDtypeStruct((s, d), BF16),
            jax.ShapeDtypeStruct((1, d), F32),
        ],
        scratch_shapes=[],
        args=(x, gain3, dout, a, b, wg, wu, wd),
    )


def wgrad(a, b, tn=None, buf=None, l=None, tm=None, split=False):
    ca = a.shape[0] if a.ndim == 3 else 1
    cb = b.shape[0] if b.ndim == 3 else 1
    nc = max(ca, cb)
    s, m = a.shape[-2:]
    n = b.shape[-1]
    tn = n if tn is None else tn
    assert n % tn == 0

    def body(*refs):
        a_ref, b_ref, o_ref = refs[0], refs[1], refs[-1]
        res = dot_tn(a_ref[...], b_ref[...])
        if split:
            o_ref[0] = res[: m // 2]
            o_ref[1] = res[m // 2 :]
        else:
            o_ref[...] = res

    params = _cparams(("arbitrary", "arbitrary"))
    if buf is None and not split:
        assert nc == 1 and a.ndim == 2 and b.ndim == 2
        tm = m if tm is None else tm
        assert m % tm == 0
        return pl.pallas_call(
            body,
            name="wgrad",
            grid=(m // tm, n // tn),
            in_specs=[pl.BlockSpec((s, tm), lambda i, j: (0, i)), pl.BlockSpec((s, tn), lambda i, j: (0, j))],
            out_specs=pl.BlockSpec((tm, tn), lambda i, j: (i, j)),
            out_shape=jax.ShapeDtypeStruct((m, n), F32),
            compiler_params=params,
        )(a, b)
    a_spec = pl.BlockSpec((None, s, m), lambda c, j: (c, 0, 0)) if a.ndim == 3 else pl.BlockSpec((s, m), lambda c, j: (0, 0))
    b_spec = pl.BlockSpec((None, s, tn), lambda c, j: (c, 0, j)) if b.ndim == 3 else pl.BlockSpec((s, tn), lambda c, j: (0, j))
    if split:
        return pl.pallas_call(
            body,
            name="wgrad_split",
            grid=(nc, n // tn),
            in_specs=[a_spec, b_spec],
            out_specs=pl.BlockSpec((2, None, None, m // 2, tn), lambda c, j: (0, c, 0, 0, j)),
            out_shape=jax.ShapeDtypeStruct((2, nc, 1, m // 2, n), F32),
            compiler_params=params,
        )(a, b)
    lh = buf.shape[2]
    hi, lo = l // lh, l % lh
    o_spec = pl.BlockSpec((None, None, None, m, tn), lambda c, j: (hi, c, lo, 0, j))
    return pl.pallas_call(
        body,
        name="wgrad_buf",
        grid=(nc, n // tn),
        in_specs=[a_spec, b_spec, ANY],
        out_specs=o_spec,
        out_shape=jax.ShapeDtypeStruct(buf.shape, F32),
        input_output_aliases={2: 0},
        compiler_params=params,
    )(a, b, buf)


def proj_fwd(x, gain3, l, w, wl, wt=False):
    s, d = x.shape
    n = w.shape[1] if wt else w.shape[2]
    tm = _tok(s)

    def body(x_ref, g_ref, w_ref, o_ref):
        u, _ = _rms(x_ref[...])
        xn = bf(u * g_ref[...])
        o_ref[...] = dot_nt(xn, w_ref[...]) if wt else dot_nn(xn, w_ref[...])

    return pl.pallas_call(
        body,
        name="proj_fwd",
        grid=(s // tm,),
        in_specs=[
            pl.BlockSpec((tm, d), lambda i: (i, 0)),
            pl.BlockSpec((None, 1, d), lambda i: (l, 0, 0)),
            pl.BlockSpec((None,) + w.shape[1:], lambda i: (wl, 0, 0)),
        ],
        out_specs=pl.BlockSpec((tm, n), lambda i: (i, 0)),
        out_shape=jax.ShapeDtypeStruct((s, n), F32),
        compiler_params=_cparams(("arbitrary",)),
    )(x, gain3, w)


def proj_bwd(x, gain3, l, parts, w, wl, dx_in, wt=False):
    s, d = x.shape
    n = w.shape[1] if wt else w.shape[2]
    widths = [p.shape[1] for p in parts]
    assert sum(widths) == n
    tm = _tok(s)
    npart = len(parts)

    def body(*refs):
        x_ref, g_ref, w_ref, dxin_ref = refs[:4]
        p_refs = refs[4 : 4 + npart]
        dx_ref, dg_ref, xn_ref, dpb_ref = refs[4 + npart :]
        g = g_ref[...]
        u, r = _rms(x_ref[...])
        xn_ref[...] = bf(u * g)
        dxn = jnp.zeros((tm, d), F32)
        off = 0
        for p_ref, wd_ in zip(p_refs, widths):
            dp = bf(p_ref[...])
            dpb_ref[:, off : off + wd_] = dp
            dxn = dxn + (dot_nn(dp, w_ref[off : off + wd_, :]) if wt else dot_nt(dp, w_ref[:, off : off + wd_]))
            off += wd_
        dx, dg = _rms_bwd(dxn, u, r, g)
        dx_ref[...] = dxin_ref[...] + dx

        @pl.when(pl.program_id(0) == 0)
        def _():
            dg_ref[...] = jnp.zeros_like(dg_ref)

        dg_ref[...] += dg

    row = pl.BlockSpec((tm, d), lambda i: (i, 0))
    return pl.pallas_call(
        body,
        name="proj_bwd",
        grid=(s // tm,),
        in_specs=[row, pl.BlockSpec((None, 1, d), lambda i: (l, 0, 0)), pl.BlockSpec((None,) + w.shape[1:], lambda i: (wl, 0, 0)), row]
        + [pl.BlockSpec((tm, wd_), lambda i: (i, 0)) for wd_ in widths],
        out_specs=[row, _const((1, d)), row, pl.BlockSpec((tm, n), lambda i: (i, 0))],
        out_shape=[
            jax.ShapeDtypeStruct((s, d), F32),
            jax.ShapeDtypeStruct((1, d), F32),
            jax.ShapeDtypeStruct((s, d), BF16),
            jax.ShapeDtypeStruct((s, n), BF16),
        ],
        compiler_params=_cparams(("arbitrary",)),
    )(x, gain3, w, dx_in, *parts)


def mm_res(x, a, w, l):
    s, d = x.shape
    k = a.shape[1]
    tm = _tok(s)

    def body(x_ref, a_ref, w_ref, o_ref):
        o_ref[...] = x_ref[...] + dot_nn(a_ref[...], w_ref[...])

    return pl.pallas_call(
        body,
        name="mm_res",
        grid=(s // tm,),
        in_specs=[
            pl.BlockSpec((tm, d), lambda i: (i, 0)),
            pl.BlockSpec((tm, k), lambda i: (i, 0)),
            pl.BlockSpec((None, k, d), lambda i: (l, 0, 0)),
        ],
        out_specs=pl.BlockSpec((tm, d), lambda i: (i, 0)),
        out_shape=jax.ShapeDtypeStruct((s, d), F32),
        compiler_params=_cparams(("arbitrary",)),
    )(x, a, w)


def mm_nt(dx, w, l):
    s, d = dx.shape
    k = w.shape[1]
    tm = _tok(s)

    def body(dx_ref, w_ref, o_ref, dxb_ref):
        dxb = bf(dx_ref[...])
        dxb_ref[...] = dxb
        o_ref[...] = dot_nt(dxb, w_ref[...])

    return pl.pallas_call(
        body,
        name="mm_nt",
        grid=(s // tm,),
        in_specs=[pl.BlockSpec((tm, d), lambda i: (i, 0)), pl.BlockSpec((None, k, d), lambda i: (l, 0, 0))],
        out_specs=[pl.BlockSpec((tm, k), lambda i: (i, 0)), pl.BlockSpec((tm, d), lambda i: (i, 0))],
        out_shape=[jax.ShapeDtypeStruct((s, k), F32), jax.ShapeDtypeStruct((s, d), BF16)],
        compiler_params=_cparams(("arbitrary",)),
    )(dx, w)


def lb_fwd(logits3):
    def body(l_ref, o_ref):
        l0, l1 = l_ref[0], l_ref[1]
        m = jnp.maximum(l0, l1)
        e0, e1 = jnp.exp(l0 - m), jnp.exp(l1 - m)
        p0, p1 = e0 / (e0 + e1), e1 / (e0 + e1)
        o_ref[0] = p0 - p0
        o_ref[1] = (p0 + p1) - p0

    return pl.pallas_call(body, name="lb_fwd", out_shape=jax.ShapeDtypeStruct(logits3.shape, F32))(logits3)


def lb_bwd(logits3, dlb1):
    def body(l_ref, d_ref, o_ref):
        l0, l1 = l_ref[0], l_ref[1]
        m = jnp.maximum(l0, l1)
        e0, e1 = jnp.exp(l0 - m), jnp.exp(l1 - m)
        p0, p1 = e0 / (e0 + e1), e1 / (e0 + e1)
        t = d_ref[...] * p0 * p1
        o_ref[0] = -t
        o_ref[1] = t

    return pl.pallas_call(body, name="lb_bwd", out_shape=jax.ShapeDtypeStruct(logits3.shape, F32))(logits3, dlb1)


def _hgrn_gates(zq, zf, lb):
    sf = _sigmoid(zf)
    f = lb + (1.0 - lb) * sf
    sq = _sigmoid(zq)
    return sf, f, jnp.log(f), 1.0 - f, sq, zq * sq


def _tri(n, upper=False):
    r = lax.broadcasted_iota(jnp.int32, (n, n), 0)
    c = lax.broadcasted_iota(jnp.int32, (n, n), 1)
    return jnp.where((c >= r) if upper else (r >= c), 1.0, 0.0).astype(F32)


def hgrn_fwd(proj, lb3, og3, l, comm=None):
    s = proj.shape[0]
    nh = 6
    n_chunk = s // CHUNK
    nsub = CHUNK // SUB

    hb = HGRN_HEADS_PER_STEP
    wide = hb * LANES

    def body(zq_ref, zf_ref, vi_ref, zg_ref, lb_ref, og_ref, main_ref, o_ref, q_a, k_a, v_a, c_a):
        og = og_ref[...]
        tril = _tri(CHUNK)
        rowi = lax.broadcasted_iota(jnp.int32, (SUB, LANES), 0)

        def one_head(hd, rows, st):
            cols = slice(hd * LANES, (hd + 1) * LANES)
            q_s, k_s, v_s, c_s = q_a.at[hd], k_a.at[hd], v_a.at[hd], c_a.at[hd]
            zg = zg_ref[rows, cols]
            _, _, lf, k, _, q = _hgrn_gates(zq_ref[rows, cols], zf_ref[rows, cols], lb_ref[:, cols])
            v = vi_ref[rows, cols]
            c = dot_nn(tril, lf, HI)
            q_s[...] = q
            k_s[...] = k
            v_s[...] = v
            c_s[...] = c
            o_inter = dot_nt(q * jnp.exp(c), st, HI)
            parts = []
            for i in range(nsub):
                lo = i * SUB
                blk = pl.ds(lo, SUB)
                qb, cb = q_s[blk, :], c_s[blk, :]
                ob = o_inter[lo : lo + SUB]
                if i > 0:
                    rr = c_s[pl.ds(lo - 1, 1), :]
                    qt = qb * jnp.exp(cb - rr)
                    kt = k_s[pl.ds(0, lo), :] * jnp.exp(rr - c_s[pl.ds(0, lo), :])
                    ob = ob + dot_nn(dot_nt(qt, kt, HI), v_s[pl.ds(0, lo), :], HI)
                for t in range(SUB):
                    e = jnp.where(rowi >= t, jnp.exp(cb - c_s[pl.ds(lo + t, 1), :]), 0.0)
                    a = jnp.sum(qb * k_s[pl.ds(lo + t, 1), :] * e, axis=-1, keepdims=True)
                    ob = ob + a * v_s[pl.ds(lo + t, 1), :]
                parts.append(ob)
            o = jnp.concatenate(parts, axis=0)
            ce = c_s[pl.ds(CHUNK - 1, 1), :]
            st = st * jnp.exp(ce) + dot_tn(v, k * jnp.exp(ce - c), HI)
            on, _ = _rms(o)
            o_ref[rows, cols] = o
            main_ref[rows, cols] = bf(on * og * (zg * _sigmoid(zg)))
            return st

        def chunk(ci, sts):
            rows = pl.ds(pl.multiple_of(ci * CHUNK, CHUNK), CHUNK)
            return tuple(one_head(hd, rows, sts[hd]) for hd in range(hb))

        lax.fori_loop(0, n_chunk, chunk, tuple(jnp.zeros((LANES, LANES), F32) for _ in range(hb)))

    def col(k):
        return pl.BlockSpec((s, wide), lambda h: (0, k * (nh // hb) + h))

    vec = pl.BlockSpec((None, 1, wide), lambda h: (l, 0, h))
    return _carry(
        body,
        comm,
        name="hgrn_fwd",
        grid=(nh // hb,),
        in_specs=[col(0), col(1), col(2), col(3), vec, pl.BlockSpec((None, 1, LANES), lambda h: (l, 0, 0))],
        out_specs=[pl.BlockSpec((s, wide), lambda h: (0, h))] * 2,
        out_shape=[jax.ShapeDtypeStruct((s, nh * LANES), BF16), jax.ShapeDtypeStruct((s, nh * LANES), F32)],
        scratch_shapes=[pltpu.VMEM((hb, CHUNK, LANES), F32)] * 4,
        args=(proj, proj, proj, proj, lb3, og3),
    )


def hgrn_bwd(proj, lb3, og3, l, o, dmixed, comm=None):
    s = proj.shape[0]
    nh = 6
    n_chunk = s // CHUNK
    nsub = CHUNK // SUB

    hb = HGRN_HEADS_PER_STEP
    wide = hb * LANES

    def body(zq_ref, zf_ref, vi_ref, zg_ref, lb_ref, og_ref, o_ref, dm_ref,
             dzq_ref, dzf_ref, dvi_ref, dzg_ref, dlb_ref, dog_ref,
             st_a, q_a, k_a, v_a, c_a, do_a, dq_a, dk_a, dv_a, acc_a):
        og = og_ref[...]
        tril = _tri(CHUNK)
        triu = _tri(CHUNK, upper=True)
        rowi = lax.broadcasted_iota(jnp.int32, (SUB, LANES), 0)

        def fwd_head(hd, ci, rows, st):
            cols = slice(hd * LANES, (hd + 1) * LANES)
            _, _, lf, k, _, _ = _hgrn_gates(zq_ref[rows, cols], zf_ref[rows, cols], lb_ref[:, cols])
            c = dot_nn(tril, lf, HI)
            ce = jnp.sum(lf, axis=0, keepdims=True)
            st_a[hd, ci] = st
            return st * jnp.exp(ce) + dot_tn(vi_ref[rows, cols], k * jnp.exp(ce - c), HI)

        def fwd_chunk(ci, sts):
            rows = pl.ds(pl.multiple_of(ci * CHUNK, CHUNK), CHUNK)
            return tuple(fwd_head(hd, ci, rows, sts[hd]) for hd in range(hb))

        lax.fori_loop(0, n_chunk, fwd_chunk, tuple(jnp.zeros((LANES, LANES), F32) for _ in range(hb)))
        acc_a[...] = jnp.zeros_like(acc_a)

        def bwd_head(hd, ci, rows, carry):
            dst, cg = carry
            cols = slice(hd * LANES, (hd + 1) * LANES)
            q_s, k_s, v_s, c_s, do_s = q_a.at[hd], k_a.at[hd], v_a.at[hd], c_a.at[hd], do_a.at[hd]
            dq_s, dk_s, dv_s, acc_s = dq_a.at[hd], dk_a.at[hd], dv_a.at[hd], acc_a.at[hd]
            lb = lb_ref[:, cols]
            zq, zf, zg = zq_ref[rows, cols], zf_ref[rows, cols], zg_ref[rows, cols]
            sf, f, lf, k, sq, q = _hgrn_gates(zq, zf, lb)
            v = vi_ref[rows, cols]
            c = dot_nn(tril, lf, HI)
            st = st_a[hd, ci]
            on, r = _rms(o_ref[rows, cols])
            sg = _sigmoid(zg)
            dmain = dm_ref[rows, cols]
            dy = dmain * (zg * sg)
            dzg_ref[rows, cols] = dmain * (on * og) * _dsilu(zg, sg)
            do, dog = _rms_bwd(dy, on, r, og)
            acc_s[pl.ds(0, 1), :] += dog
            q_s[...] = q
            k_s[...] = k
            v_s[...] = v
            c_s[...] = c
            do_s[...] = do
            ce = c_s[pl.ds(CHUNK - 1, 1), :]
            eq = jnp.exp(c)
            ek = jnp.exp(ce - c)
            qt_all = q * eq
            dq_s[...] = dot_nn(do, st, HI) * eq
            dv_s[...] = dot_nt(k * ek, dst, HI)
            dk_s[...] = dot_nn(v, dst, HI) * ek
            dst = dst * jnp.exp(ce) + dot_tn(do, qt_all, HI)
            for i in range(nsub):
                lo = i * SUB
                blk = pl.ds(lo, SUB)
                qb, cb, dob = q_s[blk, :], c_s[blk, :], do_s[blk, :]
                if i > 0:
                    prev = pl.ds(0, lo)
                    rr = c_s[pl.ds(lo - 1, 1), :]
                    eqi = jnp.exp(cb - rr)
                    eki = jnp.exp(rr - c_s[prev, :])
                    qt = qb * eqi
                    kt = k_s[prev, :] * eki
                    amat = dot_nt(qt, kt, HI)
                    damat = dot_nt(dob, v_s[prev, :], HI)
                    dv_s[prev, :] += dot_tn(amat, dob, HI)
                    dq_s[blk, :] += dot_nn(damat, kt, HI) * eqi
                    dk_s[prev, :] += dot_tn(damat, qt, HI) * eki
                dqb = jnp.zeros((SUB, LANES), F32)
                for t in range(SUB):
                    row = pl.ds(lo + t, 1)
                    e = jnp.where(rowi >= t, jnp.exp(cb - c_s[row, :]), 0.0)
                    kr = k_s[row, :]
                    a = jnp.sum(qb * kr * e, axis=-1, keepdims=True)
                    da = jnp.sum(dob * v_s[row, :], axis=-1, keepdims=True)
                    dv_s[row, :] += jnp.sum(a * dob, axis=0, keepdims=True)
                    dqb = dqb + da * kr * e
                    dk_s[row, :] += jnp.sum(da * qb * e, axis=0, keepdims=True)
                dq_s[blk, :] += dqb
            dq, dk = dq_s[...], dk_s[...]
            dg = q * dq - k * dk
            dlf = dot_nn(triu, dg, HI) + cg
            cg = cg + jnp.sum(dg, axis=0, keepdims=True)
            df = dlf / f - dk
            dzf_ref[rows, cols] = df * (1.0 - lb) * sf * (1.0 - sf)
            acc_s[pl.ds(1, 1), :] += jnp.sum(df * (1.0 - sf), axis=0, keepdims=True)
            dzq_ref[rows, cols] = dq * _dsilu(zq, sq)
            dvi_ref[rows, cols] = dv_s[...]
            return dst, cg

        def bwd_chunk(jj, carries):
            ci = n_chunk - 1 - jj
            rows = pl.ds(pl.multiple_of(ci * CHUNK, CHUNK), CHUNK)
            return tuple(bwd_head(hd, ci, rows, carries[hd]) for hd in range(hb))

        zero = (jnp.zeros((LANES, LANES), F32), jnp.zeros((1, LANES), F32))
        lax.fori_loop(0, n_chunk, bwd_chunk, tuple(zero for _ in range(hb)))

        @pl.when(pl.program_id(0) == 0)
        def _():
            dog_ref[...] = jnp.zeros_like(dog_ref)

        for hd in range(hb):
            dlb_ref[:, hd * LANES : (hd + 1) * LANES] = acc_a[hd, pl.ds(1, 1), :]
            dog_ref[...] += acc_a[hd, pl.ds(0, 1), :]

    def col(k):
        return pl.BlockSpec((s, wide), lambda h: (0, k * (nh // hb) + h), pipeline_mode=pl.Buffered(1))

    head_in = pl.BlockSpec((s, wide), lambda h: (0, h), pipeline_mode=pl.Buffered(1))
    head = pl.BlockSpec((s, wide), lambda h: (0, h))
    vec = pl.BlockSpec((None, 1, wide), lambda h: (l, 0, h))
    ck = pltpu.VMEM((hb, CHUNK, LANES), F32)
    return _carry(
        body,
        comm,
        name="hgrn_bwd",
        grid=(nh // hb,),
        in_specs=[col(0), col(1), col(2), col(3), vec, pl.BlockSpec((None, 1, LANES), lambda h: (l, 0, 0)), head_in, head_in],
        out_specs=[head] * 4 + [pl.BlockSpec((1, wide), lambda h: (0, h)), _const((1, LANES))],
        out_shape=[jax.ShapeDtypeStruct((s, nh * LANES), F32)] * 4
        + [jax.ShapeDtypeStruct((1, nh * LANES), F32), jax.ShapeDtypeStruct((1, LANES), F32)],
        scratch_shapes=[pltpu.VMEM((hb, n_chunk, LANES, LANES), F32)] + [ck] * 8 + [pltpu.VMEM((hb, 8, LANES), F32)],
        args=(proj, proj, proj, proj, lb3, og3, o, dmixed),
    )


MEM_SCALE = HEAD64**-0.5


def _mem_heads(qraw, kvm, qg, kg, pr, m0):
    lo = pr * LANES
    uq, rq = _rms64(qraw[:, lo : lo + LANES], m0)
    uk, rk = _rms64(kvm[:, lo : lo + LANES], m0)
    v = bf(kvm[:, 2 * LANES + lo : 3 * LANES + lo])
    return uq, rq, uk, rk, v, uq * qg, bf(uk * kg)


def memattn_fwd(proj, qblk, kvm, qg3, kg3, l):
    s = proj.shape[0]
    nm = kvm.shape[0]
    tm = _tok(s)

    def body(q_ref, kv_ref, qg_ref, kg_ref, o_ref):
        m0 = _lane_mask0((1, LANES))
        qraw, kvv = q_ref[...], kv_ref[...]
        for pr in range(2):
            _, _, _, _, v, qn, kn = _mem_heads(qraw, kvv, qg_ref[...], kg_ref[...], pr, m0)
            out = jnp.zeros((tm, LANES), F32)
            for hh in range(2):
                mh = m0 if hh == 0 else jnp.logical_not(m0)
                sc = dot_nt(bf(jnp.where(mh, qn, 0.0)), kn) * MEM_SCALE
                p = jnp.exp(sc - jnp.max(sc, axis=-1, keepdims=True))
                p = p / jnp.sum(p, axis=-1, keepdims=True)
                out = jnp.where(mh, dot_nn(bf(p), v), out)
            o_ref[:, pr * LANES : (pr + 1) * LANES] = bf(out)

    gspec = pl.BlockSpec((None, 1, LANES), lambda i: (l, 0, 0))
    return pl.pallas_call(
        body,
        name="memattn_fwd",
        grid=(s // tm,),
        in_specs=[pl.BlockSpec((tm, 2 * LANES), lambda i: (i, qblk)), _const((nm, 4 * LANES)), gspec, gspec],
        out_specs=pl.BlockSpec((tm, 2 * LANES), lambda i: (i, 0)),
        out_shape=jax.ShapeDtypeStruct((s, 2 * LANES), BF16),
        compiler_params=_cparams(("arbitrary",)),
    )(proj, kvm, qg3, kg3)


def memattn_bwd(proj, qblk, kvm, qg3, kg3, l, dmixed):
    s = proj.shape[0]
    nm = kvm.shape[0]
    tm = _tok(s)

    def body(q_ref, kv_ref, qg_ref, kg_ref, dm_ref, dq_ref, dkv_ref, dqg_ref, dkg_ref):
        m0 = _lane_mask0((1, LANES))
        qraw, kvv = q_ref[...], kv_ref[...]
        qg, kg = qg_ref[...], kg_ref[...]

        @pl.when(pl.program_id(0) == 0)
        def _():
            dkv_ref[...] = jnp.zeros_like(dkv_ref)
            dqg_ref[...] = jnp.zeros_like(dqg_ref)
            dkg_ref[...] = jnp.zeros_like(dkg_ref)

        for pr in range(2):
            lo = pr * LANES
            uq, rq, uk, rk, v, qn, kn = _mem_heads(qraw, kvv, qg, kg, pr, m0)
            do = dm_ref[:, lo : lo + LANES]
            dqn = jnp.zeros((tm, LANES), F32)
            dkn = jnp.zeros((nm, LANES), F32)
            dv = jnp.zeros((nm, LANES), F32)
            for hh in range(2):
                mh = m0 if hh == 0 else jnp.logical_not(m0)
                qh = bf(jnp.where(mh, qn, 0.0))
                doh = bf(jnp.where(mh, do, 0.0))
                sc = dot_nt(qh, kn) * MEM_SCALE
                p = jnp.exp(sc - jnp.max(sc, axis=-1, keepdims=True))
                p = p / jnp.sum(p, axis=-1, keepdims=True)
                dp = dot_nt(doh, v)
                ds = bf(p * (dp - jnp.sum(p * dp, axis=-1, keepdims=True)))
                dqn = dqn + jnp.where(mh, dot_nn(ds, kn), 0.0) * MEM_SCALE
                dkn = dkn + dot_tn(ds, qh) * MEM_SCALE
                dv = dv + dot_tn(bf(p), doh)
            dqr, dqg = _rms64_bwd(dqn, uq, rq, qg, m0)
            dkr, dkg = _rms64_bwd(dkn, uk, rk, kg, m0)
            dq_ref[:, lo : lo + LANES] = dqr
            dkv_ref[:, lo : lo + LANES] += dkr
            dkv_ref[:, 2 * LANES + lo : 3 * LANES + lo] += dv
            dqg_ref[...] += dqg
            dkg_ref[...] += dkg

    gspec = pl.BlockSpec((None, 1, LANES), lambda i: (l, 0, 0))
    return pl.pallas_call(
        body,
        name="memattn_bwd",
        grid=(s // tm,),
        in_specs=[
            pl.BlockSpec((tm, 2 * LANES), lambda i: (i, qblk)),
            _const((nm, 4 * LANES)),
            gspec,
            gspec,
            pl.BlockSpec((tm, 2 * LANES), lambda i: (i, 3)),
        ],
        out_specs=[pl.BlockSpec((tm, 2 * LANES), lambda i: (i, 0)), _const((nm, 4 * LANES)), _const((1, LANES)), _const((1, LANES))],
        out_shape=[
            jax.ShapeDtypeStruct((s, 2 * LANES), F32),
            jax.ShapeDtypeStruct((nm, 4 * LANES), F32),
            jax.ShapeDtypeStruct((1, LANES), F32),
            jax.ShapeDtypeStruct((1, LANES), F32),
        ],
        compiler_params=_cparams(("arbitrary",)),
    )(proj, kvm, qg3, kg3, dmixed)


KV_MAIN = 768


def _log_sigmoid(z):
    return jnp.minimum(z, 0.0) - jnp.log(1.0 + jnp.exp(-jnp.abs(z)))


def kvprep_fwd(kvf, kg, fb):
    s = kvf.shape[0]
    tm = _tok(s)

    def body(kvf_ref, kg_ref, fb_ref, k_ref, v_ref, clf_ref, carry):
        m0 = _lane_mask0((1, LANES))

        @pl.when(pl.program_id(0) == 0)
        def _():
            carry[...] = jnp.zeros_like(carry)

        for j in range(KV_MAIN // LANES):
            u, _ = _rms64(kvf_ref[:, j * LANES : (j + 1) * LANES], m0)
            k_ref[:, j * LANES : (j + 1) * LANES] = bf(u * kg_ref[...])
        v_ref[...] = bf(kvf_ref[:, KV_MAIN : 2 * KV_MAIN])
        lf = _log_sigmoid(kvf_ref[:, 2 * KV_MAIN :] + fb_ref[...])
        clf_ref[...] = dot_nn(_tri(tm), lf, HI) + carry[...]
        carry[...] += jnp.sum(lf, axis=0, keepdims=True)

    n = kvf.shape[1]
    return pl.pallas_call(
        body,
        name="kvprep_fwd",
        grid=(s // tm,),
        in_specs=[pl.BlockSpec((tm, n), lambda i: (i, 0)), _const((1, LANES)), _const((1, LANES))],
        out_specs=[pl.BlockSpec((tm, KV_MAIN), lambda i: (i, 0))] * 2 + [pl.BlockSpec((tm, LANES), lambda i: (i, 0))],
        out_shape=[jax.ShapeDtypeStruct((s, KV_MAIN), BF16)] * 2 + [jax.ShapeDtypeStruct((s, LANES), F32)],
        scratch_shapes=[pltpu.VMEM((1, LANES), F32)],
        compiler_params=_cparams(("arbitrary",)),
    )(kvf, kg, fb)


def kvprep_bwd(kvf, kg, fb, dk, dv, dclf):
    s, n = kvf.shape
    tm = _tok(s)
    nb = s // tm

    def body(kvf_ref, kg_ref, fb_ref, dk_ref, dv_ref, dclf_ref, o_ref, dkg_ref, dfb_ref, carry):
        m0 = _lane_mask0((1, LANES))

        @pl.when(pl.program_id(0) == 0)
        def _():
            carry[...] = jnp.zeros_like(carry)
            dkg_ref[...] = jnp.zeros_like(dkg_ref)
            dfb_ref[...] = jnp.zeros_like(dfb_ref)

        kg_ = kg_ref[...]
        for j in range(KV_MAIN // LANES):
            cols = slice(j * LANES, (j + 1) * LANES)
            u, r = _rms64(kvf_ref[:, cols], m0)
            dkr, dkg = _rms64_bwd(dk_ref[:, cols], u, r, kg_, m0)
            o_ref[:, cols] = dkr
            dkg_ref[...] += dkg
        o_ref[:, KV_MAIN : 2 * KV_MAIN] = dv_ref[...]
        z = kvf_ref[:, 2 * KV_MAIN :] + fb_ref[...]
        dc = dclf_ref[...]
        dlf = dot_nn(_tri(tm, upper=True), dc, HI) + carry[...]
        carry[...] += jnp.sum(dc, axis=0, keepdims=True)
        dz = dlf * _sigmoid(-z)
        o_ref[:, 2 * KV_MAIN :] = dz
        dfb_ref[...] += jnp.sum(dz, axis=0, keepdims=True)

    rev = lambda i: (nb - 1 - i, 0)
    return pl.pallas_call(
        body,
        name="kvprep_bwd",
        grid=(nb,),
        in_specs=[pl.BlockSpec((tm, n), rev), _const((1, LANES)), _const((1, LANES)), pl.BlockSpec((tm, KV_MAIN), rev),
                  pl.BlockSpec((tm, KV_MAIN), rev), pl.BlockSpec((tm, LANES), rev)],
        out_specs=[pl.BlockSpec((tm, n), rev), _const((1, LANES)), _const((1, LANES))],
        out_shape=[jax.ShapeDtypeStruct((s, n), F32), jax.ShapeDtypeStruct((1, LANES), F32), jax.ShapeDtypeStruct((1, LANES), F32)],
        scratch_shapes=[pltpu.VMEM((1, LANES), F32)],
        compiler_params=_cparams(("arbitrary",)),
    )(kvf, kg, fb, dk, dv, dclf)


FOX_SCALE = HEAD64**-0.5


def _lane_col(block, lane_idx, h):
    return jnp.sum(jnp.where(lane_idx == h, block, 0.0), axis=-1, keepdims=True)


def _causal(tq, ext, i, transposed=False):
    if transposed:
        key = lax.broadcasted_iota(jnp.int32, (ext, tq), 0)
        qry = lax.broadcasted_iota(jnp.int32, (ext, tq), 1) + i * tq
    else:
        qry = lax.broadcasted_iota(jnp.int32, (tq, ext), 0) + i * tq
        key = lax.broadcasted_iota(jnp.int32, (tq, ext), 1)
    return key <= qry


def fox_fwd(proj, k_sh, v_sh, clf, clf_t, qg3, j_layer, comm=None):
    s = proj.shape[0]
    npair = 6
    tq = TQ if s % TQ == 0 else s
    nq = s // tq

    def body(q_ref, gate_ref, k_ref, v_ref, clf_ref, clft_ref, qg_ref, main_ref, o_ref, lse_ref):
        j = pl.program_id(0)
        lane = lax.broadcasted_iota(jnp.int32, (1, LANES), 1)
        m0 = lane < HEAD64
        u, _ = _rms64(q_ref[...], m0)
        qn = u * qg_ref[...] * FOX_SCALE
        clfv = clf_ref[...]
        for hh in range(2):
            h = 2 * j + hh
            mh = m0 if hh == 0 else jnp.logical_not(m0)
            qh = bf(jnp.where(mh, qn, 0.0))
            dcol = _lane_col(clfv, lane, h)
            drow = clft_ref[pl.ds(h, 1), :]
            for i in range(nq):
                rows = slice(i * tq, (i + 1) * tq)
                ext = (i + 1) * tq
                sc = dot_nt(qh[rows], k_ref[0:ext, :]) + dcol[rows] - drow[:, :ext]
                sc = jnp.where(_causal(tq, ext, i), sc, -jnp.inf)
                m = jnp.max(sc, axis=-1, keepdims=True)
                p = jnp.exp(sc - m)
                lsum = jnp.sum(p, axis=-1, keepdims=True)
                pv = dot_nn(bf(p), v_ref[0:ext, :]) / lsum
                lse = m + jnp.log(lsum)
                if hh == 0:
                    o_ref[rows, :] = pv
                    lse_ref[rows, :] = jnp.where(lane == 0, lse, 0.0)
                else:
                    o_ref[rows, :] = jnp.where(mh, pv, o_ref[rows, :])
                    lse_ref[rows, :] = jnp.where(lane == 1, lse, lse_ref[rows, :])
        main_ref[...] = bf(o_ref[...] * _sigmoid(gate_ref[...]))

    blk = lambda off: pl.BlockSpec((s, LANES), lambda j: (0, off + j))
    return _carry(
        body,
        comm,
        name="fox_fwd",
        grid=(npair,),
        in_specs=[blk(0), blk(npair), blk(0), blk(0), _const((s, LANES)), _const((16, s)),
                  pl.BlockSpec((None, 1, LANES), lambda j: (j_layer, 0, 0))],
        out_specs=[blk(0)] * 3,
        out_shape=[jax.ShapeDtypeStruct((s, npair * LANES), BF16)] + [jax.ShapeDtypeStruct((s, npair * LANES), F32)] * 2,
        scratch_shapes=[],
        args=(proj, proj, k_sh, v_sh, clf, clf_t, qg3),
    )


def fox_bwd(proj, k_sh, v_sh, clf, clf_t, qg3, j_layer, o, lse, lse_t, dmixed, dk_in, dv_in, dclf_in, comm=None):
    s = proj.shape[0]
    npair = 6
    tq = TQ if s % TQ == 0 else s
    nq = s // tq

    def body(q_ref, gate_ref, k_ref, v_ref, clf_ref, clft_ref, qg_ref, o_ref, lse_ref, lset_ref, dm_ref, dkin_ref, dvin_ref, dclfin_ref,
             dq_ref, dgate_ref, dk_ref, dv_ref, dclf_ref, dqg_ref, dqn_s, dcl_s):
        j = pl.program_id(0)
        lane = lax.broadcasted_iota(jnp.int32, (1, LANES), 1)
        m0 = lane < HEAD64
        qg = qg_ref[...]
        u, r = _rms64(q_ref[...], m0)
        qn = u * qg * FOX_SCALE
        ov = o_ref[...]
        gate = gate_ref[...]
        sg = _sigmoid(gate)
        dmain = dm_ref[...]
        do = dmain * sg
        dgate_ref[...] = dmain * ov * sg * (1.0 - sg)
        dk_ref[...] = dkin_ref[...]
        dv_ref[...] = dvin_ref[...]
        clfv = clf_ref[...]
        lsev = lse_ref[...]
        ones8 = jnp.ones((8, LANES), F32)

        @pl.when(j == 0)
        def _():
            dclf_ref[...] = dclfin_ref[...]
            dqg_ref[...] = jnp.zeros_like(dqg_ref)

        for hh in range(2):
            h = 2 * j + hh
            mh = m0 if hh == 0 else jnp.logical_not(m0)
            qh = bf(jnp.where(mh, qn, 0.0))
            doh = jnp.where(mh, do, 0.0)
            dohb = bf(doh)
            doo = doh * ov
            dcol = _lane_col(clfv, lane, h)
            drow = clft_ref[pl.ds(h, 1), :]
            lcol = _lane_col(lsev, lane, hh)
            lrow = lset_ref[pl.ds(h, 1), :]
            delta = jnp.sum(doo, axis=-1, keepdims=True)
            dcl_s[...] = jnp.zeros_like(dcl_s)
            for i in range(nq):
                rows = slice(i * tq, (i + 1) * tq)
                ext = (i + 1) * tq
                kk, vv = k_ref[0:ext, :], v_ref[0:ext, :]
                sc = dot_nt(qh[rows], kk) + dcol[rows] - drow[:, :ext]
                p = jnp.where(_causal(tq, ext, i), jnp.exp(sc - lcol[rows]), 0.0)
                ds = p * (dot_nt(dohb[rows], vv) - delta[rows])
                dqh = dot_nn(bf(ds), kk) * FOX_SCALE
                if hh == 0:
                    dqn_s[rows, :] = dqh
                else:
                    dqn_s[rows, :] = jnp.where(mh, dqh, dqn_s[rows, :])
                dcl_s[rows, :] += jnp.sum(ds, axis=-1, keepdims=True)
                sct = dot_nt(kk, qh[rows]) + drow[:, rows] - dcol[:ext]
                pt = jnp.where(_causal(tq, ext, i, transposed=True), jnp.exp(sct - lrow[:, rows]), 0.0)
                delta_row = dot_nt(ones8, doo[rows], HI)[0:1]
                dst = pt * (dot_nt(vv, dohb[rows]) - delta_row)
                dv_ref[0:ext, :] += dot_nn(bf(pt), dohb[rows])
                dk_ref[0:ext, :] += dot_nn(bf(dst), qh[rows])
                dcl_s[0:ext, :] -= jnp.sum(dst, axis=-1, keepdims=True)
            dclf_ref[...] += jnp.where(lane == h, dcl_s[...], 0.0)
        dqr, dqg = _rms64_bwd(dqn_s[...], u, r, qg, m0)
        dq_ref[...] = dqr
        dqg_ref[...] += dqg

    blk = lambda off: pl.BlockSpec((s, LANES), lambda j: (0, off + j))
    full = _const((s, LANES))
    return _carry(
        body,
        comm,
        name="fox_bwd",
        grid=(npair,),
        in_specs=[blk(0), blk(npair), blk(0), blk(0), full, _const((16, s)), pl.BlockSpec((None, 1, LANES), lambda j: (j_layer, 0, 0)),
                  blk(0), blk(0), _const((16, s)), blk(0), blk(0), blk(0), full],
        out_specs=[blk(0)] * 4 + [full, _const((1, LANES))],
        out_shape=[jax.ShapeDtypeStruct((s, npair * LANES), F32)] * 4
        + [jax.ShapeDtypeStruct((s, LANES), F32), jax.ShapeDtypeStruct((1, LANES), F32)],
        scratch_shapes=[pltpu.VMEM((s, LANES), F32), pltpu.VMEM((s, LANES), F32)],
        args=(proj, proj, k_sh, v_sh, clf, clf_t, qg3, o, lse, lse_t, dmixed, dk_in, dv_in, dclf_in),
    )


def loss_head(y, target):
    s, d = y.shape
    tm = _tok(s)

    def body(y_ref, t_ref, loss_ref, dy_ref):
        err = y_ref[...] - t_ref[...]
        dy_ref[...] = err * (1.0 / d)

        @pl.when(pl.program_id(0) == 0)
        def _():
            loss_ref[...] = jnp.zeros_like(loss_ref)

        part = jnp.sum(jnp.mean(err * err, axis=-1, keepdims=True), axis=0, keepdims=True)
        loss_ref[...] += 0.5 * part

    row = pl.BlockSpec((tm, d), lambda i: (i, 0))
    return pl.pallas_call(
        body,
        name="loss_head",
        grid=(s // tm,),
        in_specs=[row, row],
        out_specs=[_const((1, 1)), row],
        out_shape=[jax.ShapeDtypeStruct((1, 1), F32), jax.ShapeDtypeStruct((s, d), F32)],
        compiler_params=_cparams(("arbitrary",)),
    )(y, target)


def _row_tile(r, c, n_arrays):
    budget = VMEM_LIMIT_BYTES // 2
    padded_c = -(-c // LANES) * LANES
    best = None
    for t in range(8, r + 1, 8):
        if r % t == 0 and 2 * n_arrays * t * padded_c * 4 <= budget:
            best = t
    return r if best is None else best


def _as2d(a):
    return a.reshape(-1, a.shape[-1]) if a.ndim >= 2 else a.reshape(1, -1)


def adamw(w, gs, m, v):
    shape = w.shape
    w2, m2, v2 = (_as2d(t) for t in (w, m, v))
    rows, c = w2.shape
    gs = [g.reshape(-1, c) for g in gs]
    assert sum(g.shape[0] for g in gs) == rows
    tr = _row_tile(min(g.shape[0] for g in gs), c, 8)
    assert all(g.shape[0] % tr == 0 for g in gs)
    c1 = 1.0 - ADAM_B1**ADAM_STEP
    c2 = 1.0 - ADAM_B2**ADAM_STEP
    outs = None
    first = 0
    for g in gs:
        n_prev = 0 if outs is None else 4
        r = g.shape[0]

        def body(w_ref, g_ref, m_ref, v_ref, *rest, n_prev=n_prev):
            go_ref, d_ref, nm_ref, nv_ref = rest[n_prev:]
            gv = g_ref[...]
            nm = ADAM_B1 * m_ref[...] + (1.0 - ADAM_B1) * gv
            nv = ADAM_B2 * v_ref[...] + (1.0 - ADAM_B2) * (gv * gv)
            go_ref[...] = gv
            nm_ref[...] = nm
            nv_ref[...] = nv
            d_ref[...] = -ADAM_LR * ((nm / c1) / (jnp.sqrt(nv / c2) + ADAM_EPS) + ADAM_WD * w_ref[...])

        spec = pl.BlockSpec((tr, c), lambda i, b0=first // tr: (b0 + i, 0))
        outs = pl.pallas_call(
            body,
            name="adamw",
            grid=(r // tr,),
            in_specs=[spec, pl.BlockSpec((tr, c), lambda i: (i, 0)), spec, spec] + [ANY] * n_prev,
            out_specs=[spec] * 4,
            out_shape=[jax.ShapeDtypeStruct((rows, c), F32)] * 4,
            input_output_aliases={4 + i: i for i in range(n_prev)},
            compiler_params=_cparams(("arbitrary",)),
        )(w2, g, m2, v2, *([] if outs is None else outs))
        first += r
    return tuple(t.reshape(shape) for t in outs)


def pair_sum(g, recv, c_arr):
    _, k, r, c = g.shape
    tr = _row_tile(r, c, 3)

    def body(c_ref, g_ref, r_ref, o_ref):
        o_ref[...] = bf(g_ref[...] + r_ref[...])

    return pl.pallas_call(
        body,
        name="pair_sum",
        grid_spec=pltpu.PrefetchScalarGridSpec(
            num_scalar_prefetch=1,
            grid=(k, r // tr),
            in_specs=[pl.BlockSpec((None, None, tr, c), lambda kk, i, cr: (cr[0], kk, i, 0)), pl.BlockSpec((None, tr, c), lambda kk, i, cr: (kk, i, 0))],
            out_specs=pl.BlockSpec((None, tr, c), lambda kk, i, cr: (kk, i, 0)),
        ),
        out_shape=jax.ShapeDtypeStruct((k, r, c), BF16),
        compiler_params=_cparams(("arbitrary", "arbitrary")),
    )(c_arr, g, recv)


def chip_sum(p, q, sel):
    _, r, c = p.shape
    tr = _row_tile(r, c, 4)

    def body(sel_ref, p_ref, q_ref, o_ref):
        acc = p_ref[...].astype(F32)
        for i in range(q.shape[0]):
            acc = acc + q_ref[i].astype(F32)
        o_ref[...] = acc

    return pl.pallas_call(
        body,
        name="chip_sum",
        grid_spec=pltpu.PrefetchScalarGridSpec(
            num_scalar_prefetch=1,
            grid=(r // tr,),
            in_specs=[pl.BlockSpec((None, tr, c), lambda i, sr: (sr[0], i, 0)), pl.BlockSpec((q.shape[0], tr, c), lambda i, sr: (0, i, 0))],
            out_specs=pl.BlockSpec((None, tr, c), lambda i, sr: (sr[1], i, 0)),
        ),
        out_shape=jax.ShapeDtypeStruct((2, r, c), F32),
        compiler_params=_cparams(("arbitrary",)),
    )(sel, p, q)


def cast_into_slot(w4, g, sel, dtype):
    _, _, r, c = w4.shape
    tr = _row_tile(r, c, 2)

    def body(sel_ref, w_ref, o_ref):
        o_ref[...] = w_ref[...].astype(dtype)

    return pl.pallas_call(
        body,
        name="cast_into_slot",
        grid_spec=pltpu.PrefetchScalarGridSpec(
            num_scalar_prefetch=1,
            grid=(2, r // tr),
            in_specs=[pl.BlockSpec((None, None, tr, c), lambda hf, i, sr: (g, hf, i, 0))],
            out_specs=pl.BlockSpec((None, None, tr, c), lambda hf, i, sr: (sr[0], hf, i, 0)),
        ),
        out_shape=jax.ShapeDtypeStruct((N_CHIPS, 2, r, c), dtype),
        compiler_params=_cparams(("arbitrary", "arbitrary")),
    )(sel, w4)


def _place():
    x, y, c = lax.axis_index("x"), lax.axis_index("y"), lax.axis_index("c")
    chips = [(1 - x, y), (x, 1 - y), (1 - x, 1 - y)]
    return x, y, c, 2 * x + y, chips, [2 * cx + cy for cx, cy in chips]


def _rcopy(src, dst, send, recv, dev):
    return pltpu.make_async_remote_copy(src_ref=src, dst_ref=dst, send_sem=send, recv_sem=recv, device_id=dev, device_id_type=MESH)


class Gather:
    def __init__(self, bufs):
        n = len(bufs)
        self.n = n
        self.args = list(bufs)
        self.out_shape = [jax.ShapeDtypeStruct(t.shape, t.dtype) for t in bufs]
        self.aliases = {a: a for a in range(n)}
        self.scratch = [pltpu.SemaphoreType.DMA((n, 6)), pltpu.SemaphoreType.DMA((n, 6))]

    def _sends(self, outs, send, recv):
        x, y, c, me, chips, _ = _place()
        cps = []
        for a in range(self.n):
            mine = outs[a].at[me, c]
            cps += [_rcopy(mine, mine, send.at[a, j], recv.at[a, j], (*chips[j], c)) for j in range(3)]
        return cps

    def start(self, ins, outs, scr):
        for cp in self._sends(outs, *scr):
            cp.start()

    def finish(self, ins, outs, scr):
        send, recv = scr
        x, y, c, me, chips, cidx = _place()
        sib = (x, y, 1 - c)
        passed = []
        for a in range(self.n):
            for j in range(3):
                landed = outs[a].at[cidx[j], c]
                _rcopy(landed, landed, send.at[a, j], recv.at[a, j], (*chips[j], c)).wait_recv()
                fwd = _rcopy(landed, landed, send.at[a, 3 + j], recv.at[a, 3 + j], sib)
                fwd.start()
                passed.append(fwd)
        for a in range(self.n):
            for j in range(3):
                theirs = outs[a].at[cidx[j], 1 - c]
                _rcopy(theirs, theirs, send.at[a, 3 + j], recv.at[a, 3 + j], sib).wait_recv()
        for cp in self._sends(outs, send, recv) + passed:
            cp.wait_send()


class PairExchange:
    def __init__(self, gs):
        n = len(gs)
        self.n = n
        self.args = list(gs)
        self.out_shape = [jax.ShapeDtypeStruct(t.shape[1:], t.dtype) for t in gs]
        self.aliases = {}
        self.scratch = [pltpu.SemaphoreType.DMA((n,)), pltpu.SemaphoreType.DMA((n,))]

    def _copies(self, ins, outs, send, recv):
        x, y, c = lax.axis_index("x"), lax.axis_index("y"), lax.axis_index("c")
        return [_rcopy(ins[a].at[1 - c], outs[a], send.at[a], recv.at[a], (x, y, 1 - c)) for a in range(self.n)]

    def start(self, ins, outs, scr):
        for cp in self._copies(ins, outs, *scr):
            cp.start()

    def finish(self, ins, outs, scr):
        for cp in self._copies(ins, outs, *scr):
            cp.wait()


class ChipExchange:
    def __init__(self, ps):
        n = len(ps)
        self.n = n
        self.args = list(ps)
        self.out_shape = [jax.ShapeDtypeStruct((3,) + t.shape[1:], t.dtype) for t in ps]
        self.aliases = {}
        self.scratch = [pltpu.SemaphoreType.DMA((n, 3)), pltpu.SemaphoreType.DMA((n, 3))]

    def _sends(self, ins, outs, send, recv):
        x, y, c, me, chips, cidx = _place()
        return [
            _rcopy(ins[a].at[cidx[j]], outs[a].at[j], send.at[a, j], recv.at[a, j], (*chips[j], c))
            for a in range(self.n)
            for j in range(3)
        ]

    def start(self, ins, outs, scr):
        for cp in self._sends(ins, outs, *scr):
            cp.start()

    def finish(self, ins, outs, scr):
        send, recv = scr
        x, y, c, me, chips, _ = _place()
        for a in range(self.n):
            for j in range(3):
                landed = outs[a].at[j]
                _rcopy(landed, landed, send.at[a, j], recv.at[a, j], (*chips[j], c)).wait_recv()
        for cp in self._sends(ins, outs, send, recv):
            cp.wait_send()


class PairShare:
    def __init__(self, bufs):
        n = len(bufs)
        self.n = n
        self.args = list(bufs)
        self.out_shape = [jax.ShapeDtypeStruct(t.shape, t.dtype) for t in bufs]
        self.aliases = {a: a for a in range(n)}
        self.scratch = [pltpu.SemaphoreType.DMA((n,)), pltpu.SemaphoreType.DMA((n,))]

    def _sends(self, outs, send, recv):
        x, y, c = lax.axis_index("x"), lax.axis_index("y"), lax.axis_index("c")
        return [_rcopy(outs[a].at[c], outs[a].at[c], send.at[a], recv.at[a], (x, y, 1 - c)) for a in range(self.n)]

    def start(self, ins, outs, scr):
        for cp in self._sends(outs, *scr):
            cp.start()

    def finish(self, ins, outs, scr):
        send, recv = scr
        x, y, c = lax.axis_index("x"), lax.axis_index("y"), lax.axis_index("c")
        for a in range(self.n):
            theirs = outs[a].at[1 - c]
            _rcopy(theirs, theirs, send.at[a], recv.at[a], (x, y, 1 - c)).wait_recv()
        for cp in self._sends(outs, send, recv):
            cp.wait_send()


class Multi:
    def __init__(self, comms):
        self.comms = comms
        self.args, self.out_shape, self.scratch, self.aliases = [], [], [], {}
        self.spans = []
        for cm in comms:
            a0, o0, s0 = len(self.args), len(self.out_shape), len(self.scratch)
            self.aliases.update({a0 + i: o0 + o for i, o in cm.aliases.items()})
            self.args += cm.args
            self.out_shape += cm.out_shape
            self.scratch += cm.scratch
            self.spans.append((slice(a0, len(self.args)), slice(o0, len(self.out_shape)), slice(s0, len(self.scratch))))

    def start(self, ins, outs, scr):
        for cm, (sa, so, ss) in zip(self.comms, self.spans):
            cm.start(ins[sa], outs[so], scr[ss])

    def finish(self, ins, outs, scr):
        for cm, (sa, so, ss) in zip(self.comms, self.spans):
            cm.finish(ins[sa], outs[so], scr[ss])

    def split(self, res):
        return [list(res[so]) for _, so, _ in self.spans]


def run_comm(comm, name):
    na, no = len(comm.args), len(comm.out_shape)

    def body(*refs):
        ins, outs, scr = refs[:na], refs[na : na + no], refs[na + no :]
        comm.start(ins, outs, scr)
        comm.finish(ins, outs, scr)

    return pl.pallas_call(
        body,
        name=name,
        in_specs=[ANY] * na,
        out_specs=[ANY] * no,
        out_shape=comm.out_shape,
        input_output_aliases=comm.aliases,
        scratch_shapes=comm.scratch,
    )(*comm.args)


def _carry(body, comm, *, name, grid, in_specs, out_specs, out_shape, scratch_shapes, args):
    params = _cparams(("arbitrary",))
    if comm is None:
        res = pl.pallas_call(body, name=name, grid=grid, in_specs=in_specs, out_specs=out_specs, out_shape=out_shape,
                             scratch_shapes=scratch_shapes, compiler_params=params)(*args)
        return res, None
    ni, no, ns = len(in_specs), len(out_specs), len(scratch_shapes)
    ci, co = len(comm.args), len(comm.out_shape)

    def wrapped(*refs):
        ins, c_ins = refs[:ni], refs[ni : ni + ci]
        p = ni + ci
        outs, c_outs = refs[p : p + no], refs[p + no : p + no + co]
        p += no + co
        scr, c_scr = refs[p : p + ns], refs[p + ns :]

        @pl.when(pl.program_id(0) == 0)
        def _():
            comm.start(c_ins, c_outs, c_scr)

        body(*ins, *outs, *scr)

        @pl.when(pl.program_id(0) == grid[0] - 1)
        def _():
            comm.finish(c_ins, c_outs, c_scr)

    res = pl.pallas_call(
        wrapped,
        name=name + "_carry",
        grid=grid,
        in_specs=list(in_specs) + [ANY] * ci,
        out_specs=list(out_specs) + [ANY] * co,
        out_shape=list(out_shape) + list(comm.out_shape),
        input_output_aliases={ni + i: no + o for i, o in comm.aliases.items()},
        scratch_shapes=list(scratch_shapes) + list(comm.scratch),
        compiler_params=params,
    )(*args, *comm.args)
    return res[:no], res[no:]


def small_allreduce(buf):
    r = buf.shape[0]

    def body(b_ref, o_ref, slots, send, recv):
        x, y, c = lax.axis_index("x"), lax.axis_index("y"), lax.axis_index("c")
        me = 4 * x + 2 * y + c
        slots[me] = b_ref[...]
        cps = []
        peers = []
        for mask in range(1, N_DEV):
            fx, fy, fc = (mask >> 2) & 1, (mask >> 1) & 1, mask & 1
            px, py, pc = (1 - x if fx else x), (1 - y if fy else y), (1 - c if fc else c)
            peers.append(4 * px + 2 * py + pc)
            cps.append(_rcopy(b_ref, slots.at[me], send.at[mask - 1], recv.at[mask - 1], (px, py, pc)))
        for cp in cps:
            cp.start()
        for k, pid in enumerate(peers):
            landed = slots.at[pid]
            _rcopy(landed, landed, send.at[k], recv.at[k], (x, y, c)).wait_recv()
        for cp in cps:
            cp.wait_send()
        acc = slots[0]
        for i in range(1, N_DEV):
            acc = acc + slots[i]
        o_ref[...] = acc

    vm = pl.BlockSpec(memory_space=pltpu.VMEM)
    return pl.pallas_call(
        body,
        name="small_allreduce",
        in_specs=[vm],
        out_specs=vm,
        out_shape=jax.ShapeDtypeStruct(buf.shape, F32),
        scratch_shapes=[pltpu.VMEM((N_DEV, r, LANES), F32), pltpu.SemaphoreType.DMA((N_DEV - 1,)), pltpu.SemaphoreType.DMA((N_DEV - 1,))],
    )(buf)


WEIGHT_NAMES = ["ffn1_norm", "ffn1_w_gate", "ffn1_w_up", "ffn1_w_down", "mix_norm", "mem_norm", "w_mem_kv", "mem_q_gain",
                "mem_k_gain", "w_in_a", "hgrn_lb_logits", "hgrn_o_gain", "w_in_b", "fox_q_gain", "kv_norm", "w_kv", "fox_f_bias",
                "fox_k_gain", "w_out", "ffn2_norm", "ffn2_w_gate", "ffn2_w_up", "ffn2_w_down"]
SHARDED = ["ffn1_w_gate", "ffn1_w_up", "ffn1_w_down", "w_mem_kv", "w_in_a", "w_in_b", "w_kv", "w_out", "ffn2_w_gate", "ffn2_w_up", "ffn2_w_down"]
SMALL = [n for n in WEIGHT_NAMES if n not in SHARDED]
FFN1 = ["ffn1_w_gate", "ffn1_w_up", "ffn1_w_down"]
FFN2 = ["ffn2_w_gate", "ffn2_w_up", "ffn2_w_down"]
PER_LAYER = FFN1 + FFN2 + ["w_mem_kv", "w_out"]
TRANSPOSED = ["ffn1_w_gate", "ffn1_w_up", "ffn2_w_gate", "ffn2_w_up", "w_in_a", "w_in_b"]
N_LAYERS, N_A = 4, 2
KV_PAD = 13 * LANES


def _halves(t):
    return t.reshape((2, t.shape[0] // 2) + t.shape[1:])


def _cols_from_chips(g):
    return jnp.moveaxis(g, 0, 2).reshape(g.shape[1], g.shape[2], N_CHIPS * g.shape[3])


def _rows_from_chips(g):
    return jnp.moveaxis(g, 0, 1).reshape(g.shape[1], N_CHIPS * g.shape[2], g.shape[3])


def _pair_tile(g):
    return jnp.tile(g, (1, 2)).reshape(g.shape[0], 1, LANES)


def _pair_fold(g):
    return g[:, :HEAD64] + g[:, HEAD64:]


def kernel(x, mem, ffn1_norm, ffn1_w_gate, ffn1_w_up, ffn1_w_down, mix_norm, mem_norm, w_mem_kv, mem_q_gain, mem_k_gain, w_in_a, hgrn_lb_logits, hgrn_o_gain, w_in_b, fox_q_gain, kv_norm, w_kv, fox_f_bias, fox_k_gain, w_out, ffn2_norm, ffn2_w_gate, ffn2_w_up, ffn2_w_down, loss_target, m_ffn1_norm, m_ffn1_w_gate, m_ffn1_w_up, m_ffn1_w_down, m_mix_norm, m_mem_norm, m_w_mem_kv, m_mem_q_gain, m_mem_k_gain, m_w_in_a, m_hgrn_lb_logits, m_hgrn_o_gain, m_w_in_b, m_fox_q_gain, m_kv_norm, m_w_kv, m_fox_f_bias, m_fox_k_gain, m_w_out, m_ffn2_norm, m_ffn2_w_gate, m_ffn2_w_up, m_ffn2_w_down, v_ffn1_norm, v_ffn1_w_gate, v_ffn1_w_up, v_ffn1_w_down, v_mix_norm, v_mem_norm, v_w_mem_kv, v_mem_q_gain, v_mem_k_gain, v_w_in_a, v_hgrn_lb_logits, v_hgrn_o_gain, v_w_in_b, v_fox_q_gain, v_kv_norm, v_w_kv, v_fox_f_bias, v_fox_k_gain, v_w_out, v_ffn2_norm, v_ffn2_w_gate, v_ffn2_w_up, v_ffn2_w_down):
    given = dict(locals())
    def oriented(n, t):
        return jnp.swapaxes(t, 1, 2) if n in TRANSPOSED else t

    w = {n: oriented(n, given[n]) for n in WEIGHT_NAMES}
    xs, mems, tgt = x[0], mem[0], loss_target[0]
    s, d = xs.shape
    my_chip = 2 * lax.axis_index("x") + lax.axis_index("y")
    sel = jnp.stack([my_chip, lax.axis_index("c")]).astype(jnp.int32)
    c_arr = sel[1:]

    def w_in_name(l):
        return "w_in_a" if l < N_A else "w_in_b"

    def cast(n, l):
        t = w[n]
        rows, cols = t.shape[-2:]
        own = 0 if t.ndim == 2 else (l - N_A if n == "w_in_b" else l)
        return cast_into_slot(t.reshape(-1, 2, rows // 2, cols), own, sel, BF16)

    def view(buf, n):
        rows, cols = w[n].shape[-2:]
        return buf.reshape(N_CHIPS, rows, cols) if w[n].ndim == 2 else buf.reshape(N_CHIPS, 1, rows, cols)

    def mixer(l):
        return [(w_in_name(l), l), ("w_mem_kv", l), ("w_out", l)]

    first = [(n, 0) for n in PER_LAYER] + [("w_in_a", 0), ("w_kv", 0)]
    carried = {
        (0, "ffn1"): mixer(1), (0, "mix"): [(n, 1) for n in FFN1 + FFN2[:2]], (0, "ffn2"): [(FFN2[2], 1)],
        (1, "ffn1"): mixer(2), (1, "mix"): [(n, 2) for n in FFN1 + FFN2[:2]], (1, "ffn2"): [(FFN2[2], 2)],
        (2, "ffn1"): [(FFN1[0], 3)], (2, "mix"): [(FFN1[1], 3), (FFN1[2], 3)], (2, "ffn2"): [(FFN2[0], 3)] + mixer(3),
        (3, "ffn1"): [(FFN2[1], 3)], (3, "mix"): [(FFN2[2], 3)],
    }
    bufs = {it: cast(*it) for it in first + [it for items in carried.values() for it in items]}
    lb_buf = cast_into_slot(hgrn_lb_logits.reshape(1, 2, 1, -1), 0, sel, F32)
    got0 = run_comm(Gather([bufs[it] for it in first] + [lb_buf]), "gather_layer0")
    got = {it: view(b, it[0]) for it, b in zip(first, got0[:-1])}
    w_kv_full = _cols_from_chips(got[("w_kv", 0)][:, None])
    w_kv_full = jnp.pad(w_kv_full, ((0, 0), (0, 0), (0, KV_PAD - w_kv_full.shape[-1])))
    logits3 = jnp.moveaxis(got0[-1].reshape(N_CHIPS, 2, -1), 0, 1).reshape(2, 1, -1)
    lb3 = lb_fwd(logits3)
    w_in, w_mkv, w_o = {}, {}, {}

    def gather_behind(key):
        items = carried.get(key)
        return None if items is None else Gather([bufs[it] for it in items])

    def landed(key, res):
        if res is not None:
            got.update({it: view(b, it[0]) for it, b in zip(carried[key], res)})

    norm3 = {n: w[n].reshape(N_LAYERS, 1, d) for n in ("ffn1_norm", "mix_norm", "mem_norm", "ffn2_norm")}
    kvn3 = kv_norm.reshape(1, 1, d)
    mqg3, mkg3 = _pair_tile(mem_q_gain), _pair_tile(mem_k_gain)
    og3 = hgrn_o_gain.reshape(N_A, 1, LANES)
    fqg3 = _pair_tile(fox_q_gain)
    fkg = jnp.tile(fox_k_gain, 2).reshape(1, LANES)
    fb = jnp.pad(fox_f_bias, (0, LANES - fox_f_bias.shape[0])).reshape(1, LANES)

    sv = [dict() for _ in range(N_LAYERS)]
    h = xs
    kv = None
    for l in range(N_LAYERS):
        t = sv[l]
        t["x0"] = h
        (h, t["a1"], t["b1"]), res = ffn_fwd(h, norm3["ffn1_norm"], l, *[got[(n, l)] for n in FFN1], 0, comm=gather_behind((l, "ffn1")))
        landed((l, "ffn1"), res)
        t["x1"] = h
        w_in[l] = _rows_from_chips(got[(w_in_name(l), l)])
        t["proj"] = proj_fwd(h, norm3["mix_norm"], l, w_in[l], 0, wt=True)
        if l < N_A:
            (main, t["o"]), res = hgrn_fwd(t["proj"], lb3, og3, l, comm=gather_behind((l, "mix")))
            t["qblk"] = 12
        else:
            (main, t["o"], t["lse"]), res = fox_fwd(t["proj"], kv["k"], kv["v"], kv["clf"], kv["clf_t"], fqg3, l - N_A, comm=gather_behind((l, "mix")))
            t["qblk"] = 6
        landed((l, "mix"), res)
        w_mkv[l], w_o[l] = _rows_from_chips(got[("w_mem_kv", l)]), _rows_from_chips(got[("w_out", l)])
        t["kvm"] = proj_fwd(mems, norm3["mem_norm"], l, w_mkv[l], 0)
        memo = memattn_fwd(t["proj"], t["qblk"], t["kvm"], mqg3, mkg3, l)
        t["mixed"] = jnp.concatenate([main, memo], axis=-1)
        h = mm_res(h, t["mixed"], w_o[l], 0)
        t["x2"] = h
        (h, t["a2"], t["b2"]), res = ffn_fwd(h, norm3["ffn2_norm"], l, *[got[(n, l)] for n in FFN2], 0, comm=gather_behind((l, "ffn2")))
        landed((l, "ffn2"), res)
        if l == N_A - 1:
            kv = {"x": h, "kvf": proj_fwd(h, kvn3, 0, w_kv_full, 0)}
            kv["k"], kv["v"], kv["clf"] = kvprep_fwd(kv["kvf"], fkg, fb)
            kv["clf_t"] = kv["clf"][:, :16].T

    loss_local, dx = loss_head(h, tgt)

    nc = N_CHIPS
    fc = ffn1_w_down.shape[1]
    gsplit = [dict() for _ in range(N_LAYERS)]

    def halves_layout(g):
        rr, cc = g.shape
        return jnp.transpose(g.reshape(nc, 2, rr // (2 * nc), cc), (1, 0, 2, 3))

    def group_layout(l):
        lay = {n: b.reshape(2, nc, fc // 2, d) for n, b in gsplit[l].items()}
        lay["w_mem_kv"], lay["w_out"] = halves_layout(dw_mkv[l]), halves_layout(dw_o[l])
        lay[w_in_name(l)] = halves_layout(dw_in[l])
        names = PER_LAYER + [w_in_name(l)]
        if l == N_A - 1:
            kv_cols = w_kv.shape[-1] * nc
            lay["w_kv"] = jnp.transpose(dw_kv[:, :kv_cols].reshape(2, d // 2, nc, kv_cols // nc), (0, 2, 1, 3))
            names = names + ["w_kv"]
        return names, [lay[n] for n in names]

    n_ffn = len(FFN1) + len(FFN2)
    riding = {l: l + 1 for l in range(N_LAYERS - 1)}
    reduced = {}
    unshared = None
    dw_in, dw_o, dw_mkv = [None] * N_LAYERS, [None] * N_LAYERS, [None] * N_LAYERS
    sg = {n: [None] * N_LAYERS for n in ("ffn1_norm", "mix_norm", "mem_norm", "ffn2_norm", "mem_q_gain", "mem_k_gain")}
    sg["hgrn_o_gain"], sg["fox_q_gain"], dlb = [None] * N_A, [None] * (N_LAYERS - N_A), [None] * N_A
    dk_sh = jnp.zeros((s, KV_MAIN), F32)
    dv_sh = jnp.zeros((s, KV_MAIN), F32)
    dclf = jnp.zeros((s, LANES), F32)
    zero_mem = jnp.zeros(mems.shape, F32)
    dw_kv = None
    for l in reversed(range(N_LAYERS)):
        t = sv[l]
        if l == N_A - 1:
            dkvf, dfkg, dfb = kvprep_bwd(kv["kvf"], fkg, fb, dk_sh, dv_sh, dclf)
            dx, sg["kv_norm"], xn_kv, dpb = proj_bwd(kv["x"], kvn3, 0, [dkvf], w_kv_full, 0, dx)
            dw_kv = wgrad(xn_kv, dpb)
        ride = riding.get(l)
        comms = []
        if unshared is not None:
            comms.append(PairShare(unshared[2]))
        if ride is not None:
            names_r, gl_r = group_layout(ride)
            comms.append(PairExchange(gl_r))
        comm = Multi(comms) if comms else None
        (dx, da, db, hm, xn, dyb, sg["ffn2_norm"][l]), res = ffn_bwd(t["x2"], norm3["ffn2_norm"], l, dx, t["a2"], t["b2"], *[got[(n, l)] for n in FFN2], 0, comm=comm)
        if comm is not None:
            res = comm.split(res)
            if unshared is not None:
                reduced[unshared[0]] = dict(zip(unshared[1], res.pop(0)))
                unshared = None
            if ride is not None:
                partial_r = [pair_sum(g, r, c_arr) for g, r in zip(gl_r, res.pop(0))]

        def ffn_wgrads(which, da, db, hm, xn, dyb):
            for n, (a_, b_) in zip(which, ((da, xn), (db, xn), (hm, dyb))):
                gsplit[l][n] = wgrad(a_, b_, split=True)

        ffn_wgrads(FFN2, da, db, hm, xn, dyb)
        dmixed, dxb = mm_nt(dx, w_o[l], 0)
        dw_o[l] = wgrad(t["mixed"], dxb)
        dqm, dkvm, dmq, dmk = memattn_bwd(t["proj"], t["qblk"], t["kvm"], mqg3, mkg3, l, dmixed)
        sg["mem_q_gain"][l], sg["mem_k_gain"][l] = _pair_fold(dmq), _pair_fold(dmk)
        _, sg["mem_norm"][l], memn, dkvmb = proj_bwd(mems, norm3["mem_norm"], l, [dkvm], w_mkv[l], 0, zero_mem)
        dw_mkv[l] = wgrad(memn, dkvmb)
        comm = ChipExchange(partial_r[:n_ffn]) if ride is not None else None
        if l < N_A:
            (dzq, dzf, dvi, dzg, dlb[l], sg["hgrn_o_gain"][l]), res = hgrn_bwd(t["proj"], lb3, og3, l, t["o"], dmixed, comm=comm)
            parts, tmw = [dzq, dzf, dvi, dzg, dqm], 13 * LANES
        else:
            lse_t = t["lse"].reshape(s, 6, LANES)[:, :, :2].reshape(s, 12).T
            lse_t = jnp.pad(lse_t, ((0, 4), (0, 0)))
            (dq, dgate, dk_sh, dv_sh, dclf, dfq), res = fox_bwd(t["proj"], kv["k"], kv["v"], kv["clf"], kv["clf_t"], fqg3, l - N_A, t["o"], t["lse"], lse_t, dmixed, dk_sh, dv_sh, dclf, comm=comm)
            sg["fox_q_gain"][l - N_A] = _pair_fold(dfq)
            parts, tmw = [dq, dgate, dqm], 7 * LANES
        if ride is not None:
            landed_r = list(res)
        dx, sg["mix_norm"][l], hn, dpb = proj_bwd(t["x1"], norm3["mix_norm"], l, parts, w_in[l], 0, dx, wt=True)
        dw_in[l] = wgrad(dpb, hn, tm=tmw)
        comm = ChipExchange(partial_r[n_ffn:]) if ride is not None else None
        (dx, da, db, hm, xn, dyb, sg["ffn1_norm"][l]), res = ffn_bwd(t["x0"], norm3["ffn1_norm"], l, dx, t["a1"], t["b1"], *[got[(n, l)] for n in FFN1], 0, comm=comm)
        if ride is not None:
            unshared = (ride, names_r, [chip_sum(p, q, sel) for p, q in zip(partial_r, landed_r + list(res))])
        ffn_wgrads(FFN1, da, db, hm, xn, dyb)

    names0, gl0 = group_layout(0)
    recv0 = run_comm(PairExchange(gl0), "pair_exchange")
    partial0 = [pair_sum(g, r, c_arr) for g, r in zip(gl0, recv0)]
    landed0 = run_comm(ChipExchange(partial0), "chip_exchange")
    mine0 = [chip_sum(p, q, sel) for p, q in zip(partial0, landed0)]
    both = run_comm(PairShare(unshared[2] + mine0), "pair_share")
    reduced[unshared[0]] = dict(zip(unshared[1], both[: len(unshared[1])]))
    reduced[0] = dict(zip(names0, both[len(unshared[1]) :]))
    gparts = {n: [reduced[l][n] for l in range(N_LAYERS)] for n in PER_LAYER}
    gparts["w_in_a"] = [reduced[l]["w_in_a"] for l in range(N_A)]
    gparts["w_in_b"] = [reduced[l]["w_in_b"] for l in range(N_A, N_LAYERS)]
    gparts["w_kv"] = [reduced[N_A - 1]["w_kv"]]

    dlogits = lb_bwd(logits3, dlb[1]).reshape(2, -1)
    small = {
        "ffn1_norm": jnp.concatenate(sg["ffn1_norm"]), "mix_norm": jnp.concatenate(sg["mix_norm"]),
        "mem_norm": jnp.concatenate(sg["mem_norm"]), "ffn2_norm": jnp.concatenate(sg["ffn2_norm"]),
        "mem_q_gain": jnp.concatenate(sg["mem_q_gain"]), "mem_k_gain": jnp.concatenate(sg["mem_k_gain"]),
        "hgrn_o_gain": jnp.concatenate(sg["hgrn_o_gain"]), "fox_q_gain": jnp.concatenate(sg["fox_q_gain"]),
        "kv_norm": sg["kv_norm"], "fox_f_bias": dfb[:, : fox_f_bias.shape[0]], "fox_k_gain": _pair_fold(dfkg),
        "hgrn_lb_logits": dlogits,
    }
    flat = [small[n].reshape(-1) for n in SMALL] + [loss_local.reshape(-1)]
    sizes = [f.shape[0] for f in flat]
    total = sum(sizes)
    padded = -(-total // (8 * LANES)) * (8 * LANES)
    packed = jnp.pad(jnp.concatenate(flat), (0, padded - total)).reshape(-1, LANES)
    summed = small_allreduce(packed).reshape(-1)
    off = 0
    for n, sz in zip(SMALL, sizes[:-1]):
        gparts[n] = [summed[off : off + sz].reshape(dlogits.shape if n == "hgrn_lb_logits" else w[n].shape)]
        off += sz
    loss = summed[off]
    lbw = hgrn_lb_logits.shape[1]
    gparts["hgrn_lb_logits"] = [lax.dynamic_slice_in_dim(gparts["hgrn_lb_logits"][0], my_chip * lbw, lbw, axis=1)]

    grads, delta, new_m, new_v = {}, {}, {}, {}
    for n in WEIGHT_NAMES:
        res = adamw(w[n], gparts[n], oriented(n, given["m_" + n]), oriented(n, given["v_" + n]))
        grads[n], delta[n], new_m[n], new_v[n] = (oriented(n, t) for t in res)
    return (loss, dx[None], *[grads[n] for n in WEIGHT_NAMES], *[delta[n] for n in WEIGHT_NAMES],
            *[new_m[n] for n in WEIGHT_NAMES], *[new_v[n] for n in WEIGHT_NAMES])
```

```python
import functools

import jax
import jax.numpy as jnp
from jax import lax
from jax.experimental import pallas as pl
from jax.experimental.pallas import tpu as pltpu

F32, BF16 = jnp.float32, jnp.bfloat16
HI = lax.Precision.HIGHEST
EPS = 1e-6
MESH = pl.DeviceIdType.MESH
ANY = pl.BlockSpec(memory_space=pl.ANY)

VMEM_LIMIT_BYTES = 56 << 20
N_CHIPS = 4
N_DEV = 8
LANES = 128
HEAD64 = 64
CHUNK = 64
SUB = 32
HGRN_HEADS_PER_STEP = 2
TQ = 256
TOK = 256

ADAM_LR, ADAM_B1, ADAM_B2, ADAM_EPS, ADAM_WD, ADAM_STEP = 0.001, 0.9, 0.999, 1e-08, 0.01, 10


def _cparams(sem=None, **kw):
    return pltpu.CompilerParams(dimension_semantics=sem, vmem_limit_bytes=VMEM_LIMIT_BYTES, **kw)


def _mm(a, b, dims, prec=None):
    return lax.dot_general(a, b, (dims, ((), ())), preferred_element_type=F32, precision=prec)


def dot_nn(a, b, prec=None):
    return _mm(a, b, ((1,), (0,)), prec)


def dot_nt(a, b, prec=None):
    return _mm(a, b, ((1,), (1,)), prec)


def dot_tn(a, b, prec=None):
    return _mm(a, b, ((0,), (0,)), prec)


def bf(v):
    return v.astype(BF16)


def _sigmoid(z):
    return jax.nn.sigmoid(z)


def _dsilu(z, s):
    return s * (1.0 + z * (1.0 - s))


def _rms(x):
    r = lax.rsqrt(jnp.mean(x * x, axis=-1, keepdims=True) + EPS)
    return x * r, r


def _rms_bwd(dxn, u, r, g):
    du = dxn * g
    dx = r * (du - u * jnp.mean(du * u, axis=-1, keepdims=True))
    return dx, jnp.sum(dxn * u, axis=0, keepdims=True)


def _lane_mask0(shape):
    return lax.broadcasted_iota(jnp.int32, shape, len(shape) - 1) < HEAD64


def _rms64(x, m0):
    sq = x * x
    s0 = jnp.sum(jnp.where(m0, sq, 0.0), axis=-1, keepdims=True)
    s1 = jnp.sum(jnp.where(m0, 0.0, sq), axis=-1, keepdims=True)
    r = lax.rsqrt(jnp.where(m0, s0, s1) * (1.0 / HEAD64) + EPS)
    return x * r, r


def _rms64_bwd(dxn, u, r, g, m0):
    du = dxn * g
    t = du * u
    t0 = jnp.sum(jnp.where(m0, t, 0.0), axis=-1, keepdims=True)
    t1 = jnp.sum(jnp.where(m0, 0.0, t), axis=-1, keepdims=True)
    dx = r * (du - u * (jnp.where(m0, t0, t1) * (1.0 / HEAD64)))
    return dx, jnp.sum(dxn * u, axis=0, keepdims=True)


def _tok(s):
    return TOK if s % TOK == 0 else s


def _const(shape):
    return pl.BlockSpec(shape, lambda *_: (0,) * len(shape))


def ffn_fwd(x, gain3, l, wg, wu, wd, wl, comm=None):
    s, d = x.shape
    nc, _, fc, _ = wg.shape
    tm = _tok(s)

    def body(x_ref, g_ref, wg_ref, wu_ref, wd_ref, xo_ref, a_ref, b_ref):
        xv = x_ref[...]
        u, _ = _rms(xv)
        xn = bf(u * g_ref[...])
        y = jnp.zeros((tm, d), F32)
        for c in range(nc):
            a = dot_nt(xn, wg_ref[c])
            b = dot_nt(xn, wu_ref[c])
            a_ref[c] = bf(a)
            b_ref[c] = bf(b)
            y = y + dot_nn(bf(a * _sigmoid(a) * b), wd_ref[c])
        xo_ref[...] = xv + 0.5 * y

    wspec = pl.BlockSpec((nc, None, fc, d), lambda i: (0, wl, 0, 0), pipeline_mode=pl.Buffered(1))
    wdspec = pl.BlockSpec((nc, None, fc, d), lambda i: (0, wl, 0, 0), pipeline_mode=pl.Buffered(1))
    row = pl.BlockSpec((tm, d), lambda i: (i, 0))
    act = pl.BlockSpec((nc, tm, fc), lambda i: (0, i, 0))
    return _carry(
        body,
        comm,
        name="ffn_fwd",
        grid=(s // tm,),
        in_specs=[row, pl.BlockSpec((None, 1, d), lambda i: (l, 0, 0)), wspec, wspec, wdspec],
        out_specs=[row, act, act],
        out_shape=[
            jax.ShapeDtypeStruct((s, d), F32),
            jax.ShapeDtypeStruct((nc, s, fc), BF16),
            jax.ShapeDtypeStruct((nc, s, fc), BF16),
        ],
        scratch_shapes=[],
        args=(x, gain3, wg, wu, wd),
    )


def ffn_bwd(x, gain3, l, dout, a, b, wg, wu, wd, wl, comm=None):
    s, d = x.shape
    nc, _, fc, _ = wg.shape
    tm = _tok(s)

    def body(x_ref, g_ref, do_ref, a_ref, b_ref, wg_ref, wu_ref, wd_ref, dx_ref, da_ref, db_ref, hm_ref, xn_ref, dy_ref, dg_ref):
        xv = x_ref[...]
        g = g_ref[...]
        u, r = _rms(xv)
        xn_ref[...] = bf(u * g)
        dout = do_ref[...]
        dy = bf(0.5 * dout)
        dy_ref[...] = dy
        dxn = jnp.zeros((tm, d), F32)
        for c in range(nc):
            av = a_ref[c].astype(F32)
            bv = b_ref[c].astype(F32)
            sg = _sigmoid(av)
            sl = av * sg
            dh = dot_nt(dy, wd_ref[c])
            da = bf(dh * bv * _dsilu(av, sg))
            db = bf(dh * sl)
            da_ref[c] = da
            db_ref[c] = db
            hm_ref[c] = bf(sl * bv)
            dxn = dxn + dot_nn(da, wg_ref[c]) + dot_nn(db, wu_ref[c])
        dx, dg = _rms_bwd(dxn, u, r, g)
        dx_ref[...] = dout + dx

        @pl.when(pl.program_id(0) == 0)
        def _():
            dg_ref[...] = jnp.zeros_like(dg_ref)

        dg_ref[...] += dg

    wspec = pl.BlockSpec((nc, None, fc, d), lambda i: (0, wl, 0, 0), pipeline_mode=pl.Buffered(1))
    wdspec = pl.BlockSpec((nc, None, fc, d), lambda i: (0, wl, 0, 0), pipeline_mode=pl.Buffered(1))
    row = pl.BlockSpec((tm, d), lambda i: (i, 0))
    act = pl.BlockSpec((nc, tm, fc), lambda i: (0, i, 0))
    act_shape = jax.ShapeDtypeStruct((nc, s, fc), BF16)
    return _carry(
        body,
        comm,
        name="ffn_bwd",
        grid=(s // tm,),
        in_specs=[row, pl.BlockSpec((None, 1, d), lambda i: (l, 0, 0)), row, act, act, wspec, wspec, wdspec],
        out_specs=[row, act, act, act, row, row, _const((1, d))],
        out_shape=[
            jax.ShapeDtypeStruct((s, d), F32),
            act_shape,
            act_shape,
            act_shape,
            jax.ShapeDtypeStruct((s, d), BF16),
            jax.ShapeDtypeStruct((s, d), BF16),
            jax.ShapeDtypeStruct((1, d), F32),
        ],
        scratch_shapes=[],
        args=(x, gain3, dout, a, b, wg, wu, wd),
    )


def wgrad(a, b, tn=None, buf=None, l=None, tm=None, split=False):
    ca = a.shape[0] if a.ndim == 3 else 1
    cb = b.shape[0] if b.ndim == 3 else 1
    nc = max(ca, cb)
    s, m = a.shape[-2:]
    n = b.shape[-1]
    tn = n if tn is None else tn
    assert n % tn == 0

    def body(*refs):
        a_ref, b_ref, o_ref = refs[0], refs[1], refs[-1]
        res = dot_tn(a_ref[...], b_ref[...])
        if split:
            for o in (refs[2], refs[3]):
                o[0] = res[: m // 2].astype(o.dtype)
                o[1] = res[m // 2 :].astype(o.dtype)
        else:
            o_ref[...] = res

    params = _cparams(("arbitrary", "arbitrary"))
    if buf is None and not split:
        assert nc == 1 and a.ndim == 2 and b.ndim == 2
        tm = m if tm is None else tm
        assert m % tm == 0
        return pl.pallas_call(
            body,
            name="wgrad",
            grid=(m // tm, n // tn),
            in_specs=[pl.BlockSpec((s, tm), lambda i, j: (0, i)), pl.BlockSpec((s, tn), lambda i, j: (0, j))],
            out_specs=pl.BlockSpec((tm, tn), lambda i, j: (i, j)),
            out_shape=jax.ShapeDtypeStruct((m, n), F32),
            compiler_params=params,
        )(a, b)
    a_spec = pl.BlockSpec((None, s, m), lambda c, j: (c, 0, 0)) if a.ndim == 3 else pl.BlockSpec((s, m), lambda c, j: (0, 0))
    b_spec = pl.BlockSpec((None, s, tn), lambda c, j: (c, 0, j)) if b.ndim == 3 else pl.BlockSpec((s, tn), lambda c, j: (0, j))
    if split:
        halves = pl.BlockSpec((2, None, None, m // 2, tn), lambda c, j: (0, c, 0, 0, j))
        return pl.pallas_call(
            body,
            name="wgrad_split",
            grid=(nc, n // tn),
            in_specs=[a_spec, b_spec],
            out_specs=[halves, halves],
            out_shape=[jax.ShapeDtypeStruct((2, nc, 1, m // 2, n), F32), jax.ShapeDtypeStruct((2, nc, 1, m // 2, n), BF16)],
            compiler_params=params,
        )(a, b)
    lh = buf.shape[2]
    hi, lo = l // lh, l % lh
    o_spec = pl.BlockSpec((None, None, None, m, tn), lambda c, j: (hi, c, lo, 0, j))
    return pl.pallas_call(
        body,
        name="wgrad_buf",
        grid=(nc, n // tn),
        in_specs=[a_spec, b_spec, ANY],
        out_specs=o_spec,
        out_shape=jax.ShapeDtypeStruct(buf.shape, F32),
        input_output_aliases={2: 0},
        compiler_params=params,
    )(a, b, buf)


def proj_fwd(x, gain3, l, w, wl, wt=False):
    s, d = x.shape
    n = w.shape[1] if wt else w.shape[2]
    tm = _tok(s)

    def body(x_ref, g_ref, w_ref, o_ref):
        u, _ = _rms(x_ref[...])
        xn = bf(u * g_ref[...])
        o_ref[...] = dot_nt(xn, w_ref[...]) if wt else dot_nn(xn, w_ref[...])

    return pl.pallas_call(
        body,
        name="proj_fwd",
        grid=(s // tm,),
        in_specs=[
            pl.BlockSpec((tm, d), lambda i: (i, 0)),
            pl.BlockSpec((None, 1, d), lambda i: (l, 0, 0)),
            pl.BlockSpec((None,) + w.shape[1:], lambda i: (wl, 0, 0)),
        ],
        out_specs=pl.BlockSpec((tm, n), lambda i: (i, 0)),
        out_shape=jax.ShapeDtypeStruct((s, n), F32),
        compiler_params=_cparams(("arbitrary",)),
    )(x, gain3, w)


def proj_bwd(x, gain3, l, parts, w, wl, dx_in, wt=False):
    s, d = x.shape
    n = w.shape[1] if wt else w.shape[2]
    widths = [p.shape[1] for p in parts]
    assert sum(widths) == n
    tm = _tok(s)
    npart = len(parts)

    def body(*refs):
        x_ref, g_ref, w_ref, dxin_ref = refs[:4]
        p_refs = refs[4 : 4 + npart]
        dx_ref, dg_ref, xn_ref, dpb_ref = refs[4 + npart :]
        g = g_ref[...]
        u, r = _rms(x_ref[...])
        xn_ref[...] = bf(u * g)
        dxn = jnp.zeros((tm, d), F32)
        off = 0
        for p_ref, wd_ in zip(p_refs, widths):
            dp = bf(p_ref[...])
            dpb_ref[:, off : off + wd_] = dp
            dxn = dxn + (dot_nn(dp, w_ref[off : off + wd_, :]) if wt else dot_nt(dp, w_ref[:, off : off + wd_]))
            off += wd_
        dx, dg = _rms_bwd(dxn, u, r, g)
        dx_ref[...] = dxin_ref[...] + dx

        @pl.when(pl.program_id(0) == 0)
        def _():
            dg_ref[...] = jnp.zeros_like(dg_ref)

        dg_ref[...] += dg

    row = pl.BlockSpec((tm, d), lambda i: (i, 0))
    return pl.pallas_call(
        body,
        name="proj_bwd",
        grid=(s // tm,),
        in_specs=[row, pl.BlockSpec((None, 1, d), lambda i: (l, 0, 0)), pl.BlockSpec((None,) + w.shape[1:], lambda i: (wl, 0, 0)), row]
        + [pl.BlockSpec((tm, wd_), lambda i: (i, 0)) for wd_ in widths],
        out_specs=[row, _const((1, d)), row, pl.BlockSpec((tm, n), lambda i: (i, 0))],
        out_shape=[
            jax.ShapeDtypeStruct((s, d), F32),
            jax.ShapeDtypeStruct((1, d), F32),
            jax.ShapeDtypeStruct((s, d), BF16),
            jax.ShapeDtypeStruct((s, n), BF16),
        ],
        compiler_params=_cparams(("arbitrary",)),
    )(x, gain3, w, dx_in, *parts)


def mm_res(x, a, w, l):
    s, d = x.shape
    k = a.shape[1]
    tm = _tok(s)

    def body(x_ref, a_ref, w_ref, o_ref):
        o_ref[...] = x_ref[...] + dot_nn(a_ref[...], w_ref[...])

    return pl.pallas_call(
        body,
        name="mm_res",
        grid=(s // tm,),
        in_specs=[
            pl.BlockSpec((tm, d), lambda i: (i, 0)),
            pl.BlockSpec((tm, k), lambda i: (i, 0)),
            pl.BlockSpec((None, k, d), lambda i: (l, 0, 0)),
        ],
        out_specs=pl.BlockSpec((tm, d), lambda i: (i, 0)),
        out_shape=jax.ShapeDtypeStruct((s, d), F32),
        compiler_params=_cparams(("arbitrary",)),
    )(x, a, w)


def mm_nt(dx, w, l):
    s, d = dx.shape
    k = w.shape[1]
    tm = _tok(s)

    def body(dx_ref, w_ref, o_ref, dxb_ref):
        dxb = bf(dx_ref[...])
        dxb_ref[...] = dxb
        o_ref[...] = dot_nt(dxb, w_ref[...])

    return pl.pallas_call(
        body,
        name="mm_nt",
        grid=(s // tm,),
        in_specs=[pl.BlockSpec((tm, d), lambda i: (i, 0)), pl.BlockSpec((None, k, d), lambda i: (l, 0, 0))],
        out_specs=[pl.BlockSpec((tm, k), lambda i: (i, 0)), pl.BlockSpec((tm, d), lambda i: (i, 0))],
        out_shape=[jax.ShapeDtypeStruct((s, k), F32), jax.ShapeDtypeStruct((s, d), BF16)],
        compiler_params=_cparams(("arbitrary",)),
    )(dx, w)


def lb_fwd(logits3):
    def body(l_ref, o_ref):
        l0, l1 = l_ref[0], l_ref[1]
        m = jnp.maximum(l0, l1)
        e0, e1 = jnp.exp(l0 - m), jnp.exp(l1 - m)
        p0, p1 = e0 / (e0 + e1), e1 / (e0 + e1)
        o_ref[0] = p0 - p0
        o_ref[1] = (p0 + p1) - p0

    return pl.pallas_call(body, name="lb_fwd", out_shape=jax.ShapeDtypeStruct(logits3.shape, F32))(logits3)


def lb_bwd(logits3, dlb1):
    def body(l_ref, d_ref, o_ref):
        l0, l1 = l_ref[0], l_ref[1]
        m = jnp.maximum(l0, l1)
        e0, e1 = jnp.exp(l0 - m), jnp.exp(l1 - m)
        p0, p1 = e0 / (e0 + e1), e1 / (e0 + e1)
        t = d_ref[...] * p0 * p1
        o_ref[0] = -t
        o_ref[1] = t

    return pl.pallas_call(body, name="lb_bwd", out_shape=jax.ShapeDtypeStruct(logits3.shape, F32))(logits3, dlb1)


def _hgrn_gates(zq, zf, lb):
    sf = _sigmoid(zf)
    f = lb + (1.0 - lb) * sf
    sq = _sigmoid(zq)
    return sf, f, jnp.log(f), 1.0 - f, sq, zq * sq


def _tri(n, upper=False):
    r = lax.broadcasted_iota(jnp.int32, (n, n), 0)
    c = lax.broadcasted_iota(jnp.int32, (n, n), 1)
    return jnp.where((c >= r) if upper else (r >= c), 1.0, 0.0).astype(F32)


def hgrn_fwd(proj, lb3, og3, l, comm=None):
    s = proj.shape[0]
    nh = 6
    n_chunk = s // CHUNK
    nsub = CHUNK // SUB

    hb = HGRN_HEADS_PER_STEP
    wide = hb * LANES

    def body(zq_ref, zf_ref, vi_ref, zg_ref, lb_ref, og_ref, main_ref, o_ref, q_a, k_a, v_a, c_a):
        og = og_ref[...]
        tril = _tri(CHUNK)
        rowi = lax.broadcasted_iota(jnp.int32, (SUB, LANES), 0)

        def one_head(hd, rows, st):
            cols = slice(hd * LANES, (hd + 1) * LANES)
            q_s, k_s, v_s, c_s = q_a.at[hd], k_a.at[hd], v_a.at[hd], c_a.at[hd]
            zg = zg_ref[rows, cols]
            _, _, lf, k, _, q = _hgrn_gates(zq_ref[rows, cols], zf_ref[rows, cols], lb_ref[:, cols])
            v = vi_ref[rows, cols]
            c = dot_nn(tril, lf, HI)
            q_s[...] = q
            k_s[...] = k
            v_s[...] = v
            c_s[...] = c
            o_inter = dot_nt(q * jnp.exp(c), st, HI)
            parts = []
            for i in range(nsub):
                lo = i * SUB
                blk = pl.ds(lo, SUB)
                qb, cb = q_s[blk, :], c_s[blk, :]
                ob = o_inter[lo : lo + SUB]
                if i > 0:
                    rr = c_s[pl.ds(lo - 1, 1), :]
                    qt = qb * jnp.exp(cb - rr)
                    kt = k_s[pl.ds(0, lo), :] * jnp.exp(rr - c_s[pl.ds(0, lo), :])
                    ob = ob + dot_nn(dot_nt(qt, kt, HI), v_s[pl.ds(0, lo), :], HI)
                for t in range(SUB):
                    e = jnp.where(rowi >= t, jnp.exp(cb - c_s[pl.ds(lo + t, 1), :]), 0.0)
                    a = jnp.sum(qb * k_s[pl.ds(lo + t, 1), :] * e, axis=-1, keepdims=True)
                    ob = ob + a * v_s[pl.ds(lo + t, 1), :]
                parts.append(ob)
            o = jnp.concatenate(parts, axis=0)
            ce = c_s[pl.ds(CHUNK - 1, 1), :]
            st = st * jnp.exp(ce) + dot_tn(v, k * jnp.exp(ce - c), HI)
            on, _ = _rms(o)
            o_ref[rows, cols] = o
            main_ref[rows, cols] = bf(on * og * (zg * _sigmoid(zg)))
            return st

        def chunk(ci, sts):
            rows = pl.ds(pl.multiple_of(ci * CHUNK, CHUNK), CHUNK)
            return tuple(one_head(hd, rows, sts[hd]) for hd in range(hb))

        lax.fori_loop(0, n_chunk, chunk, tuple(jnp.zeros((LANES, LANES), F32) for _ in range(hb)))

    def col(k):
        return pl.BlockSpec((s, wide), lambda h: (0, k * (nh // hb) + h))

    vec = pl.BlockSpec((None, 1, wide), lambda h: (l, 0, h))
    return _carry(
        body,
        comm,
        name="hgrn_fwd",
        grid=(nh // hb,),
        in_specs=[col(0), col(1), col(2), col(3), vec, pl.BlockSpec((None, 1, LANES), lambda h: (l, 0, 0))],
        out_specs=[pl.BlockSpec((s, wide), lambda h: (0, h))] * 2,
        out_shape=[jax.ShapeDtypeStruct((s, nh * LANES), BF16), jax.ShapeDtypeStruct((s, nh * LANES), F32)],
        scratch_shapes=[pltpu.VMEM((hb, CHUNK, LANES), F32)] * 4,
        args=(proj, proj, proj, proj, lb3, og3),
    )


def hgrn_bwd(proj, lb3, og3, l, o, dmixed, comm=None):
    s = proj.shape[0]
    nh = 6
    n_chunk = s // CHUNK
    nsub = CHUNK // SUB

    hb = HGRN_HEADS_PER_STEP
    wide = hb * LANES

    def body(zq_ref, zf_ref, vi_ref, zg_ref, lb_ref, og_ref, o_ref, dm_ref,
             dzq_ref, dzf_ref, dvi_ref, dzg_ref, dlb_ref, dog_ref,
             st_a, q_a, k_a, v_a, c_a, do_a, dq_a, dk_a, dv_a, acc_a):
        og = og_ref[...]
        tril = _tri(CHUNK)
        triu = _tri(CHUNK, upper=True)
        rowi = lax.broadcasted_iota(jnp.int32, (SUB, LANES), 0)

        def fwd_head(hd, ci, rows, st):
            cols = slice(hd * LANES, (hd + 1) * LANES)
            _, _, lf, k, _, _ = _hgrn_gates(zq_ref[rows, cols], zf_ref[rows, cols], lb_ref[:, cols])
            c = dot_nn(tril, lf, HI)
            ce = jnp.sum(lf, axis=0, keepdims=True)
            st_a[hd, ci] = st
            return st * jnp.exp(ce) + dot_tn(vi_ref[rows, cols], k * jnp.exp(ce - c), HI)

        def fwd_chunk(ci, sts):
            rows = pl.ds(pl.multiple_of(ci * CHUNK, CHUNK), CHUNK)
            return tuple(fwd_head(hd, ci, rows, sts[hd]) for hd in range(hb))

        lax.fori_loop(0, n_chunk, fwd_chunk, tuple(jnp.zeros((LANES, LANES), F32) for _ in range(hb)))
        acc_a[...] = jnp.zeros_like(acc_a)

        def bwd_head(hd, ci, rows, carry):
            dst, cg = carry
            cols = slice(hd * LANES, (hd + 1) * LANES)
            q_s, k_s, v_s, c_s, do_s = q_a.at[hd], k_a.at[hd], v_a.at[hd], c_a.at[hd], do_a.at[hd]
            dq_s, dk_s, dv_s, acc_s = dq_a.at[hd], dk_a.at[hd], dv_a.at[hd], acc_a.at[hd]
            lb = lb_ref[:, cols]
            zq, zf, zg = zq_ref[rows, cols], zf_ref[rows, cols], zg_ref[rows, cols]
            sf, f, lf, k, sq, q = _hgrn_gates(zq, zf, lb)
            v = vi_ref[rows, cols]
            c = dot_nn(tril, lf, HI)
            st = st_a[hd, ci]
            on, r = _rms(o_ref[rows, cols])
            sg = _sigmoid(zg)
            dmain = dm_ref[rows, cols]
            dy = dmain * (zg * sg)
            dzg_ref[rows, cols] = dmain * (on * og) * _dsilu(zg, sg)
            do, dog = _rms_bwd(dy, on, r, og)
            acc_s[pl.ds(0, 1), :] += dog
            q_s[...] = q
            k_s[...] = k
            v_s[...] = v
            c_s[...] = c
            do_s[...] = do
            ce = c_s[pl.ds(CHUNK - 1, 1), :]
            eq = jnp.exp(c)
            ek = jnp.exp(ce - c)
            qt_all = q * eq
            dq_s[...] = dot_nn(do, st, HI) * eq
            dv_s[...] = dot_nt(k * ek, dst, HI)
            dk_s[...] = dot_nn(v, dst, HI) * ek
            dst = dst * jnp.exp(ce) + dot_tn(do, qt_all, HI)
            for i in range(nsub):
                lo = i * SUB
                blk = pl.ds(lo, SUB)
                qb, cb, dob = q_s[blk, :], c_s[blk, :], do_s[blk, :]
                if i > 0:
                    prev = pl.ds(0, lo)
                    rr = c_s[pl.ds(lo - 1, 1), :]
                    eqi = jnp.exp(cb - rr)
                    eki = jnp.exp(rr - c_s[prev, :])
                    qt = qb * eqi
                    kt = k_s[prev, :] * eki
                    amat = dot_nt(qt, kt, HI)
                    damat = dot_nt(dob, v_s[prev, :], HI)
                    dv_s[prev, :] += dot_tn(amat, dob, HI)
                    dq_s[blk, :] += dot_nn(damat, kt, HI) * eqi
                    dk_s[prev, :] += dot_tn(damat, qt, HI) * eki
                dqb = jnp.zeros((SUB, LANES), F32)
                for t in range(SUB):
                    row = pl.ds(lo + t, 1)
                    e = jnp.where(rowi >= t, jnp.exp(cb - c_s[row, :]), 0.0)
                    kr = k_s[row, :]
                    a = jnp.sum(qb * kr * e, axis=-1, keepdims=True)
                    da = jnp.sum(dob * v_s[row, :], axis=-1, keepdims=True)
                    dv_s[row, :] += jnp.sum(a * dob, axis=0, keepdims=True)
                    dqb = dqb + da * kr * e
                    dk_s[row, :] += jnp.sum(da * qb * e, axis=0, keepdims=True)
                dq_s[blk, :] += dqb
            dq, dk = dq_s[...], dk_s[...]
            dg = q * dq - k * dk
            dlf = dot_nn(triu, dg, HI) + cg
            cg = cg + jnp.sum(dg, axis=0, keepdims=True)
            df = dlf / f - dk
            dzf_ref[rows, cols] = df * (1.0 - lb) * sf * (1.0 - sf)
            acc_s[pl.ds(1, 1), :] += jnp.sum(df * (1.0 - sf), axis=0, keepdims=True)
            dzq_ref[rows, cols] = dq * _dsilu(zq, sq)
            dvi_ref[rows, cols] = dv_s[...]
            return dst, cg

        def bwd_chunk(jj, carries):
            ci = n_chunk - 1 - jj
            rows = pl.ds(pl.multiple_of(ci * CHUNK, CHUNK), CHUNK)
            return tuple(bwd_head(hd, ci, rows, carries[hd]) for hd in range(hb))

        zero = (jnp.zeros((LANES, LANES), F32), jnp.zeros((1, LANES), F32))
        lax.fori_loop(0, n_chunk, bwd_chunk, tuple(zero for _ in range(hb)))

        @pl.when(pl.program_id(0) == 0)
        def _():
            dog_ref[...] = jnp.zeros_like(dog_ref)

        for hd in range(hb):
            dlb_ref[:, hd * LANES : (hd + 1) * LANES] = acc_a[hd, pl.ds(1, 1), :]
            dog_ref[...] += acc_a[hd, pl.ds(0, 1), :]

    def col(k):
        return pl.BlockSpec((s, wide), lambda h: (0, k * (nh // hb) + h), pipeline_mode=pl.Buffered(1))

    head_in = pl.BlockSpec((s, wide), lambda h: (0, h), pipeline_mode=pl.Buffered(1))
    head = pl.BlockSpec((s, wide), lambda h: (0, h))
    vec = pl.BlockSpec((None, 1, wide), lambda h: (l, 0, h))
    ck = pltpu.VMEM((hb, CHUNK, LANES), F32)
    return _carry(
        body,
        comm,
        name="hgrn_bwd",
        grid=(nh // hb,),
        in_specs=[col(0), col(1), col(2), col(3), vec, pl.BlockSpec((None, 1, LANES), lambda h: (l, 0, 0)), head_in, head_in],
        out_specs=[head] * 4 + [pl.BlockSpec((1, wide), lambda h: (0, h)), _const((1, LANES))],
        out_shape=[jax.ShapeDtypeStruct((s, nh * LANES), F32)] * 4
        + [jax.ShapeDtypeStruct((1, nh * LANES), F32), jax.ShapeDtypeStruct((1, LANES), F32)],
        scratch_shapes=[pltpu.VMEM((hb, n_chunk, LANES, LANES), F32)] + [ck] * 8 + [pltpu.VMEM((hb, 8, LANES), F32)],
        args=(proj, proj, proj, proj, lb3, og3, o, dmixed),
    )


MEM_SCALE = HEAD64**-0.5


def _mem_heads(qraw, kvm, qg, kg, pr, m0):
    lo = pr * LANES
    uq, rq = _rms64(qraw[:, lo : lo + LANES], m0)
    uk, rk = _rms64(kvm[:, lo : lo + LANES], m0)
    v = bf(kvm[:, 2 * LANES + lo : 3 * LANES + lo])
    return uq, rq, uk, rk, v, uq * qg, bf(uk * kg)


def memattn_fwd(proj, qblk, kvm, qg3, kg3, l):
    s = proj.shape[0]
    nm = kvm.shape[0]
    tm = _tok(s)

    def body(q_ref, kv_ref, qg_ref, kg_ref, o_ref):
        m0 = _lane_mask0((1, LANES))
        qraw, kvv = q_ref[...], kv_ref[...]
        for pr in range(2):
            _, _, _, _, v, qn, kn = _mem_heads(qraw, kvv, qg_ref[...], kg_ref[...], pr, m0)
            out = jnp.zeros((tm, LANES), F32)
            for hh in range(2):
                mh = m0 if hh == 0 else jnp.logical_not(m0)
                sc = dot_nt(bf(jnp.where(mh, qn, 0.0)), kn) * MEM_SCALE
                p = jnp.exp(sc - jnp.max(sc, axis=-1, keepdims=True))
                p = p / jnp.sum(p, axis=-1, keepdims=True)
                out = jnp.where(mh, dot_nn(bf(p), v), out)
            o_ref[:, pr * LANES : (pr + 1) * LANES] = bf(out)

    gspec = pl.BlockSpec((None, 1, LANES), lambda i: (l, 0, 0))
    return pl.pallas_call(
        body,
        name="memattn_fwd",
        grid=(s // tm,),
        in_specs=[pl.BlockSpec((tm, 2 * LANES), lambda i: (i, qblk)), _const((nm, 4 * LANES)), gspec, gspec],
        out_specs=pl.BlockSpec((tm, 2 * LANES), lambda i: (i, 0)),
        out_shape=jax.ShapeDtypeStruct((s, 2 * LANES), BF16),
        compiler_params=_cparams(("arbitrary",)),
    )(proj, kvm, qg3, kg3)


def memattn_bwd(proj, qblk, kvm, qg3, kg3, l, dmixed):
    s = proj.shape[0]
    nm = kvm.shape[0]
    tm = _tok(s)

    def body(q_ref, kv_ref, qg_ref, kg_ref, dm_ref, dq_ref, dkv_ref, dqg_ref, dkg_ref):
        m0 = _lane_mask0((1, LANES))
        qraw, kvv = q_ref[...], kv_ref[...]
        qg, kg = qg_ref[...], kg_ref[...]

        @pl.when(pl.program_id(0) == 0)
        def _():
            dkv_ref[...] = jnp.zeros_like(dkv_ref)
            dqg_ref[...] = jnp.zeros_like(dqg_ref)
            dkg_ref[...] = jnp.zeros_like(dkg_ref)

        for pr in range(2):
            lo = pr * LANES
            uq, rq, uk, rk, v, qn, kn = _mem_heads(qraw, kvv, qg, kg, pr, m0)
            do = dm_ref[:, lo : lo + LANES]
            dqn = jnp.zeros((tm, LANES), F32)
            dkn = jnp.zeros((nm, LANES), F32)
            dv = jnp.zeros((nm, LANES), F32)
            for hh in range(2):
                mh = m0 if hh == 0 else jnp.logical_not(m0)
                qh = bf(jnp.where(mh, qn, 0.0))
                doh = bf(jnp.where(mh, do, 0.0))
                sc = dot_nt(qh, kn) * MEM_SCALE
                p = jnp.exp(sc - jnp.max(sc, axis=-1, keepdims=True))
                p = p / jnp.sum(p, axis=-1, keepdims=True)
                dp = dot_nt(doh, v)
                ds = bf(p * (dp - jnp.sum(p * dp, axis=-1, keepdims=True)))
                dqn = dqn + jnp.where(mh, dot_nn(ds, kn), 0.0) * MEM_SCALE
                dkn = dkn + dot_tn(ds, qh) * MEM_SCALE
                dv = dv + dot_tn(bf(p), doh)
            dqr, dqg = _rms64_bwd(dqn, uq, rq, qg, m0)
            dkr, dkg = _rms64_bwd(dkn, uk, rk, kg, m0)
            dq_ref[:, lo : lo + LANES] = dqr
            dkv_ref[:, lo : lo + LANES] += dkr
            dkv_ref[:, 2 * LANES + lo : 3 * LANES + lo] += dv
            dqg_ref[...] += dqg
            dkg_ref[...] += dkg

    gspec = pl.BlockSpec((None, 1, LANES), lambda i: (l, 0, 0))
    return pl.pallas_call(
        body,
        name="memattn_bwd",
        grid=(s // tm,),
        in_specs=[
            pl.BlockSpec((tm, 2 * LANES), lambda i: (i, qblk)),
            _const((nm, 4 * LANES)),
            gspec,
            gspec,
            pl.BlockSpec((tm, 2 * LANES), lambda i: (i, 3)),
        ],
        out_specs=[pl.BlockSpec((tm, 2 * LANES), lambda i: (i, 0)), _const((nm, 4 * LANES)), _const((1, LANES)), _const((1, LANES))],
        out_shape=[
            jax.ShapeDtypeStruct((s, 2 * LANES), F32),
            jax.ShapeDtypeStruct((nm, 4 * LANES), F32),
            jax.ShapeDtypeStruct((1, LANES), F32),
            jax.ShapeDtypeStruct((1, LANES), F32),
        ],
        compiler_params=_cparams(("arbitrary",)),
    )(proj, kvm, qg3, kg3, dmixed)


KV_MAIN = 768


def _log_sigmoid(z):
    return jnp.minimum(z, 0.0) - jnp.log(1.0 + jnp.exp(-jnp.abs(z)))


def kvprep_fwd(kvf, kg, fb):
    s = kvf.shape[0]
    tm = _tok(s)

    def body(kvf_ref, kg_ref, fb_ref, k_ref, v_ref, clf_ref, carry):
        m0 = _lane_mask0((1, LANES))

        @pl.when(pl.program_id(0) == 0)
        def _():
            carry[...] = jnp.zeros_like(carry)

        for j in range(KV_MAIN // LANES):
            u, _ = _rms64(kvf_ref[:, j * LANES : (j + 1) * LANES], m0)
            k_ref[:, j * LANES : (j + 1) * LANES] = bf(u * kg_ref[...])
        v_ref[...] = bf(kvf_ref[:, KV_MAIN : 2 * KV_MAIN])
        lf = _log_sigmoid(kvf_ref[:, 2 * KV_MAIN :] + fb_ref[...])
        clf_ref[...] = dot_nn(_tri(tm), lf, HI) + carry[...]
        carry[...] += jnp.sum(lf, axis=0, keepdims=True)

    n = kvf.shape[1]
    return pl.pallas_call(
        body,
        name="kvprep_fwd",
        grid=(s // tm,),
        in_specs=[pl.BlockSpec((tm, n), lambda i: (i, 0)), _const((1, LANES)), _const((1, LANES))],
        out_specs=[pl.BlockSpec((tm, KV_MAIN), lambda i: (i, 0))] * 2 + [pl.BlockSpec((tm, LANES), lambda i: (i, 0))],
        out_shape=[jax.ShapeDtypeStruct((s, KV_MAIN), BF16)] * 2 + [jax.ShapeDtypeStruct((s, LANES), F32)],
        scratch_shapes=[pltpu.VMEM((1, LANES), F32)],
        compiler_params=_cparams(("arbitrary",)),
    )(kvf, kg, fb)


def kvprep_bwd(kvf, kg, fb, dk, dv, dclf):
    s, n = kvf.shape
    tm = _tok(s)
    nb = s // tm

    def body(kvf_ref, kg_ref, fb_ref, dk_ref, dv_ref, dclf_ref, o_ref, dkg_ref, dfb_ref, carry):
        m0 = _lane_mask0((1, LANES))

        @pl.when(pl.program_id(0) == 0)
        def _():
            carry[...] = jnp.zeros_like(carry)
            dkg_ref[...] = jnp.zeros_like(dkg_ref)
            dfb_ref[...] = jnp.zeros_like(dfb_ref)

        kg_ = kg_ref[...]
        for j in range(KV_MAIN // LANES):
            cols = slice(j * LANES, (j + 1) * LANES)
            u, r = _rms64(kvf_ref[:, cols], m0)
            dkr, dkg = _rms64_bwd(dk_ref[:, cols], u, r, kg_, m0)
            o_ref[:, cols] = dkr
            dkg_ref[...] += dkg
        o_ref[:, KV_MAIN : 2 * KV_MAIN] = dv_ref[...]
        z = kvf_ref[:, 2 * KV_MAIN :] + fb_ref[...]
        dc = dclf_ref[...]
        dlf = dot_nn(_tri(tm, upper=True), dc, HI) + carry[...]
        carry[...] += jnp.sum(dc, axis=0, keepdims=True)
        dz = dlf * _sigmoid(-z)
        o_ref[:, 2 * KV_MAIN :] = dz
        dfb_ref[...] += jnp.sum(dz, axis=0, keepdims=True)

    rev = lambda i: (nb - 1 - i, 0)
    return pl.pallas_call(
        body,
        name="kvprep_bwd",
        grid=(nb,),
        in_specs=[pl.BlockSpec((tm, n), rev), _const((1, LANES)), _const((1, LANES)), pl.BlockSpec((tm, KV_MAIN), rev),
                  pl.BlockSpec((tm, KV_MAIN), rev), pl.BlockSpec((tm, LANES), rev)],
        out_specs=[pl.BlockSpec((tm, n), rev), _const((1, LANES)), _const((1, LANES))],
        out_shape=[jax.ShapeDtypeStruct((s, n), F32), jax.ShapeDtypeStruct((1, LANES), F32), jax.ShapeDtypeStruct((1, LANES), F32)],
        scratch_shapes=[pltpu.VMEM((1, LANES), F32)],
        compiler_params=_cparams(("arbitrary",)),
    )(kvf, kg, fb, dk, dv, dclf)


FOX_SCALE = HEAD64**-0.5


def _lane_col(block, lane_idx, h):
    return jnp.sum(jnp.where(lane_idx == h, block, 0.0), axis=-1, keepdims=True)


def _causal(tq, ext, i, transposed=False):
    if transposed:
        key = lax.broadcasted_iota(jnp.int32, (ext, tq), 0)
        qry = lax.broadcasted_iota(jnp.int32, (ext, tq), 1) + i * tq
    else:
        qry = lax.broadcasted_iota(jnp.int32, (tq, ext), 0) + i * tq
        key = lax.broadcasted_iota(jnp.int32, (tq, ext), 1)
    return key <= qry


def fox_fwd(proj, k_sh, v_sh, clf, clf_t, qg3, j_layer, comm=None):
    s = proj.shape[0]
    npair = 6
    tq = TQ if s % TQ == 0 else s
    nq = s // tq

    def body(q_ref, gate_ref, k_ref, v_ref, clf_ref, clft_ref, qg_ref, main_ref, o_ref, lse_ref):
        j = pl.program_id(0)
        lane = lax.broadcasted_iota(jnp.int32, (1, LANES), 1)
        m0 = lane < HEAD64
        u, _ = _rms64(q_ref[...], m0)
        qn = u * qg_ref[...] * FOX_SCALE
        clfv = clf_ref[...]
        for hh in range(2):
            h = 2 * j + hh
            mh = m0 if hh == 0 else jnp.logical_not(m0)
            qh = bf(jnp.where(mh, qn, 0.0))
            dcol = _lane_col(clfv, lane, h)
            drow = clft_ref[pl.ds(h, 1), :]
            for i in range(nq):
                rows = slice(i * tq, (i + 1) * tq)
                ext = (i + 1) * tq
                sc = dot_nt(qh[rows], k_ref[0:ext, :]) + dcol[rows] - drow[:, :ext]
                sc = jnp.where(_causal(tq, ext, i), sc, -jnp.inf)
                m = jnp.max(sc, axis=-1, keepdims=True)
                p = jnp.exp(sc - m)
                lsum = jnp.sum(p, axis=-1, keepdims=True)
                pv = dot_nn(bf(p), v_ref[0:ext, :]) / lsum
                lse = m + jnp.log(lsum)
                if hh == 0:
                    o_ref[rows, :] = pv
                    lse_ref[rows, :] = jnp.where(lane == 0, lse, 0.0)
                else:
                    o_ref[rows, :] = jnp.where(mh, pv, o_ref[rows, :])
                    lse_ref[rows, :] = jnp.where(lane == 1, lse, lse_ref[rows, :])
        main_ref[...] = bf(o_ref[...] * _sigmoid(gate_ref[...]))

    blk = lambda off: pl.BlockSpec((s, LANES), lambda j: (0, off + j))
    return _carry(
        body,
        comm,
        name="fox_fwd",
        grid=(npair,),
        in_specs=[blk(0), blk(npair), blk(0), blk(0), _const((s, LANES)), _const((16, s)),
                  pl.BlockSpec((None, 1, LANES), lambda j: (j_layer, 0, 0))],
        out_specs=[blk(0)] * 3,
        out_shape=[jax.ShapeDtypeStruct((s, npair * LANES), BF16)] + [jax.ShapeDtypeStruct((s, npair * LANES), F32)] * 2,
        scratch_shapes=[],
        args=(proj, proj, k_sh, v_sh, clf, clf_t, qg3),
    )


def fox_bwd(proj, k_sh, v_sh, clf, clf_t, qg3, j_layer, o, lse, lse_t, dmixed, dk_in, dv_in, dclf_in, comm=None):
    s = proj.shape[0]
    npair = 6
    tq = TQ if s % TQ == 0 else s
    nq = s // tq

    def body(q_ref, gate_ref, k_ref, v_ref, clf_ref, clft_ref, qg_ref, o_ref, lse_ref, lset_ref, dm_ref, dkin_ref, dvin_ref, dclfin_ref,
             dq_ref, dgate_ref, dk_ref, dv_ref, dclf_ref, dqg_ref, dqn_s, dcl_s):
        j = pl.program_id(0)
        lane = lax.broadcasted_iota(jnp.int32, (1, LANES), 1)
        m0 = lane < HEAD64
        qg = qg_ref[...]
        u, r = _rms64(q_ref[...], m0)
        qn = u * qg * FOX_SCALE
        ov = o_ref[...]
        gate = gate_ref[...]
        sg = _sigmoid(gate)
        dmain = dm_ref[...]
        do = dmain * sg
        dgate_ref[...] = dmain * ov * sg * (1.0 - sg)
        dk_ref[...] = dkin_ref[...]
        dv_ref[...] = dvin_ref[...]
        clfv = clf_ref[...]
        lsev = lse_ref[...]
        ones8 = jnp.ones((8, LANES), F32)

        @pl.when(j == 0)
        def _():
            dclf_ref[...] = dclfin_ref[...]
            dqg_ref[...] = jnp.zeros_like(dqg_ref)

        for hh in range(2):
            h = 2 * j + hh
            mh = m0 if hh == 0 else jnp.logical_not(m0)
            qh = bf(jnp.where(mh, qn, 0.0))
            doh = jnp.where(mh, do, 0.0)
            dohb = bf(doh)
            doo = doh * ov
            dcol = _lane_col(clfv, lane, h)
            drow = clft_ref[pl.ds(h, 1), :]
            lcol = _lane_col(lsev, lane, hh)
            lrow = lset_ref[pl.ds(h, 1), :]
            delta = jnp.sum(doo, axis=-1, keepdims=True)
            dcl_s[...] = jnp.zeros_like(dcl_s)
            for i in range(nq):
                rows = slice(i * tq, (i + 1) * tq)
                ext = (i + 1) * tq
                kk, vv = k_ref[0:ext, :], v_ref[0:ext, :]
                sc = dot_nt(qh[rows], kk) + dcol[rows] - drow[:, :ext]
                p = jnp.where(_causal(tq, ext, i), jnp.exp(sc - lcol[rows]), 0.0)
                ds = p * (dot_nt(dohb[rows], vv) - delta[rows])
                dqh = dot_nn(bf(ds), kk) * FOX_SCALE
                if hh == 0:
                    dqn_s[rows, :] = dqh
                else:
                    dqn_s[rows, :] = jnp.where(mh, dqh, dqn_s[rows, :])
                dcl_s[rows, :] += jnp.sum(ds, axis=-1, keepdims=True)
                sct = dot_nt(kk, qh[rows]) + drow[:, rows] - dcol[:ext]
                pt = jnp.where(_causal(tq, ext, i, transposed=True), jnp.exp(sct - lrow[:, rows]), 0.0)
                delta_row = dot_nt(ones8, doo[rows], HI)[0:1]
                dst = pt * (dot_nt(vv, dohb[rows]) - delta_row)
                dv_ref[0:ext, :] += dot_nn(bf(pt), dohb[rows])
                dk_ref[0:ext, :] += dot_nn(bf(dst), qh[rows])
                dcl_s[0:ext, :] -= jnp.sum(dst, axis=-1, keepdims=True)
            dclf_ref[...] += jnp.where(lane == h, dcl_s[...], 0.0)
        dqr, dqg = _rms64_bwd(dqn_s[...], u, r, qg, m0)
        dq_ref[...] = dqr
        dqg_ref[...] += dqg

    blk = lambda off: pl.BlockSpec((s, LANES), lambda j: (0, off + j))
    full = _const((s, LANES))
    return _carry(
        body,
        comm,
        name="fox_bwd",
        grid=(npair,),
        in_specs=[blk(0), blk(npair), blk(0), blk(0), full, _const((16, s)), pl.BlockSpec((None, 1, LANES), lambda j: (j_layer, 0, 0)),
                  blk(0), blk(0), _const((16, s)), blk(0), blk(0), blk(0), full],
        out_specs=[blk(0)] * 4 + [full, _const((1, LANES))],
        out_shape=[jax.ShapeDtypeStruct((s, npair * LANES), F32)] * 4
        + [jax.ShapeDtypeStruct((s, LANES), F32), jax.ShapeDtypeStruct((1, LANES), F32)],
        scratch_shapes=[pltpu.VMEM((s, LANES), F32), pltpu.VMEM((s, LANES), F32)],
        args=(proj, proj, k_sh, v_sh, clf, clf_t, qg3, o, lse, lse_t, dmixed, dk_in, dv_in, dclf_in),
    )


def loss_head(y, target):
    s, d = y.shape
    tm = _tok(s)

    def body(y_ref, t_ref, loss_ref, dy_ref):
        err = y_ref[...] - t_ref[...]
        dy_ref[...] = err * (1.0 / d)

        @pl.when(pl.program_id(0) == 0)
        def _():
            loss_ref[...] = jnp.zeros_like(loss_ref)

        part = jnp.sum(jnp.mean(err * err, axis=-1, keepdims=True), axis=0, keepdims=True)
        loss_ref[...] += 0.5 * part

    row = pl.BlockSpec((tm, d), lambda i: (i, 0))
    return pl.pallas_call(
        body,
        name="loss_head",
        grid=(s // tm,),
        in_specs=[row, row],
        out_specs=[_const((1, 1)), row],
        out_shape=[jax.ShapeDtypeStruct((1, 1), F32), jax.ShapeDtypeStruct((s, d), F32)],
        compiler_params=_cparams(("arbitrary",)),
    )(y, target)


def _row_tile(r, c, n_arrays):
    budget = VMEM_LIMIT_BYTES // 2
    padded_c = -(-c // LANES) * LANES
    for step in (16, 8):
        fits = [t for t in range(step, r + 1, step) if r % t == 0 and 2 * n_arrays * t * padded_c * 4 <= budget]
        if fits:
            return fits[-1]
    return r


def _as2d(a):
    return a.reshape(-1, a.shape[-1]) if a.ndim >= 2 else a.reshape(1, -1)


def adamw(w, gs, m, v):
    shape = w.shape
    w2, m2, v2 = (_as2d(t) for t in (w, m, v))
    rows, c = w2.shape
    gs = [g.reshape(-1, c) for g in gs]
    assert sum(g.shape[0] for g in gs) == rows
    tr = _row_tile(min(g.shape[0] for g in gs), c, 8)
    assert all(g.shape[0] % tr == 0 for g in gs)
    c1 = 1.0 - ADAM_B1**ADAM_STEP
    c2 = 1.0 - ADAM_B2**ADAM_STEP
    outs = None
    first = 0
    for g in gs:
        n_prev = 0 if outs is None else 4
        r = g.shape[0]

        def body(w_ref, g_ref, m_ref, v_ref, *rest, n_prev=n_prev):
            go_ref, d_ref, nm_ref, nv_ref = rest[n_prev:]
            gv = g_ref[...]
            nm = ADAM_B1 * m_ref[...] + (1.0 - ADAM_B1) * gv
            nv = ADAM_B2 * v_ref[...] + (1.0 - ADAM_B2) * (gv * gv)
            go_ref[...] = gv
            nm_ref[...] = nm
            nv_ref[...] = nv
            d_ref[...] = -ADAM_LR * ((nm / c1) / (jnp.sqrt(nv / c2) + ADAM_EPS) + ADAM_WD * w_ref[...])

        spec = pl.BlockSpec((tr, c), lambda i, b0=first // tr: (b0 + i, 0))
        outs = pl.pallas_call(
            body,
            name="adamw",
            grid=(r // tr,),
            in_specs=[spec, pl.BlockSpec((tr, c), lambda i: (i, 0)), spec, spec] + [ANY] * n_prev,
            out_specs=[spec] * 4,
            out_shape=[jax.ShapeDtypeStruct((rows, c), F32)] * 4,
            input_output_aliases={4 + i: i for i in range(n_prev)},
            compiler_params=_cparams(("arbitrary",)),
        )(w2, g, m2, v2, *([] if outs is None else outs))
        first += r
    return tuple(t.reshape(shape) for t in outs)


def _by_shape(arrays, fn):
    groups = {}
    for i, t in enumerate(arrays):
        groups.setdefault((t.shape, str(t.dtype)), []).append(i)
    out = [None] * len(arrays)
    for idx in groups.values():
        for i, res in zip(idx, fn(idx)):
            out[i] = res
    return out


def pair_sum(gs, recvs, c_arr):
    n = len(gs)
    _, k, r, c = gs[0].shape
    tr = _row_tile(r, c, 3 * n)

    def body(c_ref, *refs):
        for i in range(n):
            refs[2 * n + i][...] = bf(refs[i][...] + refs[n + i][...].astype(F32))

    slab = pl.BlockSpec((None, tr, c), lambda kk, i, cr: (kk, i, 0))
    return pl.pallas_call(
        body,
        name="pair_sum",
        grid_spec=pltpu.PrefetchScalarGridSpec(
            num_scalar_prefetch=1,
            grid=(k, r // tr),
            in_specs=[pl.BlockSpec((None, None, tr, c), lambda kk, i, cr: (cr[0], kk, i, 0))] * n + [slab] * n,
            out_specs=[slab] * n,
        ),
        out_shape=[jax.ShapeDtypeStruct((k, r, c), BF16)] * n,
        compiler_params=_cparams(("arbitrary", "arbitrary")),
    )(c_arr, *gs, *recvs)


def chip_sum(ps, qs, sel):
    n = len(ps)
    _, r, c = ps[0].shape
    nq = qs[0].shape[0]
    tr = _row_tile(r, c, 4 * n)

    def body(sel_ref, *refs):
        for a in range(n):
            acc = refs[a][...].astype(F32)
            for i in range(nq):
                acc = acc + refs[n + a][i].astype(F32)
            refs[2 * n + a][...] = acc

    return pl.pallas_call(
        body,
        name="chip_sum",
        grid_spec=pltpu.PrefetchScalarGridSpec(
            num_scalar_prefetch=1,
            grid=(r // tr,),
            in_specs=[pl.BlockSpec((None, tr, c), lambda i, sr: (sr[0], i, 0))] * n + [pl.BlockSpec((nq, tr, c), lambda i, sr: (0, i, 0))] * n,
            out_specs=[pl.BlockSpec((None, tr, c), lambda i, sr: (sr[1], i, 0))] * n,
        ),
        out_shape=[jax.ShapeDtypeStruct((2, r, c), F32)] * n,
        compiler_params=_cparams(("arbitrary",)),
    )(sel, *ps, *qs)


def cast_into_slot(w4s, owns, sel, dtype):
    n = len(w4s)
    _, _, r, c = w4s[0].shape
    tr = _row_tile(r, c, 2 * n)

    def body(sel_ref, *refs):
        for i in range(n):
            refs[n + i][...] = refs[i][...].astype(dtype)

    return pl.pallas_call(
        body,
        name="cast_into_slot",
        grid_spec=pltpu.PrefetchScalarGridSpec(
            num_scalar_prefetch=1,
            grid=(2, r // tr),
            in_specs=[pl.BlockSpec((None, None, tr, c), lambda hf, i, sr, g=g: (g, hf, i, 0)) for g in owns],
            out_specs=[pl.BlockSpec((None, None, tr, c), lambda hf, i, sr: (sr[0], hf, i, 0))] * n,
        ),
        out_shape=[jax.ShapeDtypeStruct((N_CHIPS, 2, r, c), dtype)] * n,
        compiler_params=_cparams(("arbitrary", "arbitrary")),
    )(sel, *w4s)


def _place():
    x, y, c = lax.axis_index("x"), lax.axis_index("y"), lax.axis_index("c")
    chips = [(1 - x, y), (x, 1 - y), (1 - x, 1 - y)]
    return x, y, c, 2 * x + y, chips, [2 * cx + cy for cx, cy in chips]


def _rcopy(src, dst, send, recv, dev):
    return pltpu.make_async_remote_copy(src_ref=src, dst_ref=dst, send_sem=send, recv_sem=recv, device_id=dev, device_id_type=MESH)


class Gather:
    def __init__(self, bufs):
        n = len(bufs)
        self.n = n
        self.args = list(bufs)
        self.out_shape = [jax.ShapeDtypeStruct(t.shape, t.dtype) for t in bufs]
        self.aliases = {a: a for a in range(n)}
        self.scratch = [pltpu.SemaphoreType.DMA((n, 6)), pltpu.SemaphoreType.DMA((n, 6))]

    def _sends(self, outs, send, recv):
        x, y, c, me, chips, _ = _place()
        cps = []
        for a in range(self.n):
            mine = outs[a].at[me, c]
            cps += [_rcopy(mine, mine, send.at[a, j], recv.at[a, j], (*chips[j], c)) for j in range(3)]
        return cps

    def start(self, ins, outs, scr):
        for cp in self._sends(outs, *scr):
            cp.start()

    def finish(self, ins, outs, scr):
        send, recv = scr
        x, y, c, me, chips, cidx = _place()
        sib = (x, y, 1 - c)
        passed = []
        for a in range(self.n):
            for j in range(3):
                landed = outs[a].at[cidx[j], c]
                _rcopy(landed, landed, send.at[a, j], recv.at[a, j], (*chips[j], c)).wait_recv()
                fwd = _rcopy(landed, landed, send.at[a, 3 + j], recv.at[a, 3 + j], sib)
                fwd.start()
                passed.append(fwd)
        for a in range(self.n):
            for j in range(3):
                theirs = outs[a].at[cidx[j], 1 - c]
                _rcopy(theirs, theirs, send.at[a, 3 + j], recv.at[a, 3 + j], sib).wait_recv()
        for cp in self._sends(outs, send, recv) + passed:
            cp.wait_send()


class PairExchange:
    def __init__(self, gs):
        n = len(gs)
        self.n = n
        self.args = list(gs)
        self.out_shape = [jax.ShapeDtypeStruct(t.shape[1:], t.dtype) for t in gs]
        self.aliases = {}
        self.scratch = [pltpu.SemaphoreType.DMA((n,)), pltpu.SemaphoreType.DMA((n,))]

    def _copies(self, ins, outs, send, recv):
        x, y, c = lax.axis_index("x"), lax.axis_index("y"), lax.axis_index("c")
        return [_rcopy(ins[a].at[1 - c], outs[a], send.at[a], recv.at[a], (x, y, 1 - c)) for a in range(self.n)]

    def start(self, ins, outs, scr):
        for cp in self._copies(ins, outs, *scr):
            cp.start()

    def finish(self, ins, outs, scr):
        for cp in self._copies(ins, outs, *scr):
            cp.wait()


class ChipExchange:
    def __init__(self, ps):
        n = len(ps)
        self.n = n
        self.args = list(ps)
        self.out_shape = [jax.ShapeDtypeStruct((3,) + t.shape[1:], t.dtype) for t in ps]
        self.aliases = {}
        self.scratch = [pltpu.SemaphoreType.DMA((n, 3)), pltpu.SemaphoreType.DMA((n, 3))]

    def _sends(self, ins, outs, send, recv):
        x, y, c, me, chips, cidx = _place()
        return [
            _rcopy(ins[a].at[cidx[j]], outs[a].at[j], send.at[a, j], recv.at[a, j], (*chips[j], c))
            for a in range(self.n)
            for j in range(3)
        ]

    def start(self, ins, outs, scr):
        for cp in self._sends(ins, outs, *scr):
            cp.start()

    def finish(self, ins, outs, scr):
        send, recv = scr
        x, y, c, me, chips, _ = _place()
        for a in range(self.n):
            for j in range(3):
                landed = outs[a].at[j]
                _rcopy(landed, landed, send.at[a, j], recv.at[a, j], (*chips[j], c)).wait_recv()
        for cp in self._sends(ins, outs, send, recv):
            cp.wait_send()


class PairShare:
    def __init__(self, bufs):
        n = len(bufs)
        self.n = n
        self.args = list(bufs)
        self.out_shape = [jax.ShapeDtypeStruct(t.shape, t.dtype) for t in bufs]
        self.aliases = {a: a for a in range(n)}
        self.scratch = [pltpu.SemaphoreType.DMA((n,)), pltpu.SemaphoreType.DMA((n,))]

    def _sends(self, outs, send, recv):
        x, y, c = lax.axis_index("x"), lax.axis_index("y"), lax.axis_index("c")
        return [_rcopy(outs[a].at[c], outs[a].at[c], send.at[a], recv.at[a], (x, y, 1 - c)) for a in range(self.n)]

    def start(self, ins, outs, scr):
        for cp in self._sends(outs, *scr):
            cp.start()

    def finish(self, ins, outs, scr):
        send, recv = scr
        x, y, c = lax.axis_index("x"), lax.axis_index("y"), lax.axis_index("c")
        for a in range(self.n):
            theirs = outs[a].at[1 - c]
            _rcopy(theirs, theirs, send.at[a], recv.at[a], (x, y, 1 - c)).wait_recv()
        for cp in self._sends(outs, send, recv):
            cp.wait_send()


class Multi:
    def __init__(self, comms):
        self.comms = comms
        self.args, self.out_shape, self.scratch, self.aliases = [], [], [], {}
        self.spans = []
        for cm in comms:
            a0, o0, s0 = len(self.args), len(self.out_shape), len(self.scratch)
            self.aliases.update({a0 + i: o0 + o for i, o in cm.aliases.items()})
            self.args += cm.args
            self.out_shape += cm.out_shape
            self.scratch += cm.scratch
            self.spans.append((slice(a0, len(self.args)), slice(o0, len(self.out_shape)), slice(s0, len(self.scratch))))

    def start(self, ins, outs, scr):
        for cm, (sa, so, ss) in zip(self.comms, self.spans):
            cm.start(ins[sa], outs[so], scr[ss])

    def finish(self, ins, outs, scr):
        for cm, (sa, so, ss) in zip(self.comms, self.spans):
            cm.finish(ins[sa], outs[so], scr[ss])

    def split(self, res):
        return [list(res[so]) for _, so, _ in self.spans]


def run_comm(comm, name):
    na, no = len(comm.args), len(comm.out_shape)

    def body(*refs):
        ins, outs, scr = refs[:na], refs[na : na + no], refs[na + no :]
        comm.start(ins, outs, scr)
        comm.finish(ins, outs, scr)

    return pl.pallas_call(
        body,
        name=name,
        in_specs=[ANY] * na,
        out_specs=[ANY] * no,
        out_shape=comm.out_shape,
        input_output_aliases=comm.aliases,
        scratch_shapes=comm.scratch,
    )(*comm.args)


def _carry(body, comm, *, name, grid, in_specs, out_specs, out_shape, scratch_shapes, args):
    params = _cparams(("arbitrary",))
    if comm is None:
        res = pl.pallas_call(body, name=name, grid=grid, in_specs=in_specs, out_specs=out_specs, out_shape=out_shape,
                             scratch_shapes=scratch_shapes, compiler_params=params)(*args)
        return res, None
    ni, no, ns = len(in_specs), len(out_specs), len(scratch_shapes)
    ci, co = len(comm.args), len(comm.out_shape)

    def wrapped(*refs):
        ins, c_ins = refs[:ni], refs[ni : ni + ci]
        p = ni + ci
        outs, c_outs = refs[p : p + no], refs[p + no : p + no + co]
        p += no + co
        scr, c_scr = refs[p : p + ns], refs[p + ns :]

        @pl.when(pl.program_id(0) == 0)
        def _():
            comm.start(c_ins, c_outs, c_scr)

        body(*ins, *outs, *scr)

        @pl.when(pl.program_id(0) == grid[0] - 1)
        def _():
            comm.finish(c_ins, c_outs, c_scr)

    res = pl.pallas_call(
        wrapped,
        name=name + "_carry",
        grid=grid,
        in_specs=list(in_specs) + [ANY] * ci,
        out_specs=list(out_specs) + [ANY] * co,
        out_shape=list(out_shape) + list(comm.out_shape),
        input_output_aliases={ni + i: no + o for i, o in comm.aliases.items()},
        scratch_shapes=list(scratch_shapes) + list(comm.scratch),
        compiler_params=params,
    )(*args, *comm.args)
    return res[:no], res[no:]


def small_allreduce(buf):
    r = buf.shape[0]

    def body(b_ref, o_ref, slots, send, recv):
        x, y, c = lax.axis_index("x"), lax.axis_index("y"), lax.axis_index("c")
        me = 4 * x + 2 * y + c
        slots[me] = b_ref[...]
        cps = []
        peers = []
        for mask in range(1, N_DEV):
            fx, fy, fc = (mask >> 2) & 1, (mask >> 1) & 1, mask & 1
            px, py, pc = (1 - x if fx else x), (1 - y if fy else y), (1 - c if fc else c)
            peers.append(4 * px + 2 * py + pc)
            cps.append(_rcopy(b_ref, slots.at[me], send.at[mask - 1], recv.at[mask - 1], (px, py, pc)))
        for cp in cps:
            cp.start()
        for k, pid in enumerate(peers):
            landed = slots.at[pid]
            _rcopy(landed, landed, send.at[k], recv.at[k], (x, y, c)).wait_recv()
        for cp in cps:
            cp.wait_send()
        acc = slots[0]
        for i in range(1, N_DEV):
            acc = acc + slots[i]
        o_ref[...] = acc

    vm = pl.BlockSpec(memory_space=pltpu.VMEM)
    return pl.pallas_call(
        body,
        name="small_allreduce",
        in_specs=[vm],
        out_specs=vm,
        out_shape=jax.ShapeDtypeStruct(buf.shape, F32),
        scratch_shapes=[pltpu.VMEM((N_DEV, r, LANES), F32), pltpu.SemaphoreType.DMA((N_DEV - 1,)), pltpu.SemaphoreType.DMA((N_DEV - 1,))],
    )(buf)


WEIGHT_NAMES = ["ffn1_norm", "ffn1_w_gate", "ffn1_w_up", "ffn1_w_down", "mix_norm", "mem_norm", "w_mem_kv", "mem_q_gain",
                "mem_k_gain", "w_in_a", "hgrn_lb_logits", "hgrn_o_gain", "w_in_b", "fox_q_gain", "kv_norm", "w_kv", "fox_f_bias",
                "fox_k_gain", "w_out", "ffn2_norm", "ffn2_w_gate", "ffn2_w_up", "ffn2_w_down"]
SHARDED = ["ffn1_w_gate", "ffn1_w_up", "ffn1_w_down", "w_mem_kv", "w_in_a", "w_in_b", "w_kv", "w_out", "ffn2_w_gate", "ffn2_w_up", "ffn2_w_down"]
SMALL = [n for n in WEIGHT_NAMES if n not in SHARDED]
FFN1 = ["ffn1_w_gate", "ffn1_w_up", "ffn1_w_down"]
FFN2 = ["ffn2_w_gate", "ffn2_w_up", "ffn2_w_down"]
PER_LAYER = FFN1 + FFN2 + ["w_mem_kv", "w_out"]
TRANSPOSED = ["ffn1_w_gate", "ffn1_w_up", "ffn2_w_gate", "ffn2_w_up", "w_in_a", "w_in_b"]
N_LAYERS, N_A = 4, 2
KV_PAD = 13 * LANES


def _halves(t):
    return t.reshape((2, t.shape[0] // 2) + t.shape[1:])


def _cols_from_chips(g):
    return jnp.moveaxis(g, 0, 2).reshape(g.shape[1], g.shape[2], N_CHIPS * g.shape[3])


def _rows_from_chips(g):
    return jnp.moveaxis(g, 0, 1).reshape(g.shape[1], N_CHIPS * g.shape[2], g.shape[3])


def _pair_tile(g):
    return jnp.tile(g, (1, 2)).reshape(g.shape[0], 1, LANES)


def _pair_fold(g):
    return g[:, :HEAD64] + g[:, HEAD64:]


def kernel(x, mem, ffn1_norm, ffn1_w_gate, ffn1_w_up, ffn1_w_down, mix_norm, mem_norm, w_mem_kv, mem_q_gain, mem_k_gain, w_in_a, hgrn_lb_logits, hgrn_o_gain, w_in_b, fox_q_gain, kv_norm, w_kv, fox_f_bias, fox_k_gain, w_out, ffn2_norm, ffn2_w_gate, ffn2_w_up, ffn2_w_down, loss_target, m_ffn1_norm, m_ffn1_w_gate, m_ffn1_w_up, m_ffn1_w_down, m_mix_norm, m_mem_norm, m_w_mem_kv, m_mem_q_gain, m_mem_k_gain, m_w_in_a, m_hgrn_lb_logits, m_hgrn_o_gain, m_w_in_b, m_fox_q_gain, m_kv_norm, m_w_kv, m_fox_f_bias, m_fox_k_gain, m_w_out, m_ffn2_norm, m_ffn2_w_gate, m_ffn2_w_up, m_ffn2_w_down, v_ffn1_norm, v_ffn1_w_gate, v_ffn1_w_up, v_ffn1_w_down, v_mix_norm, v_mem_norm, v_w_mem_kv, v_mem_q_gain, v_mem_k_gain, v_w_in_a, v_hgrn_lb_logits, v_hgrn_o_gain, v_w_in_b, v_fox_q_gain, v_kv_norm, v_w_kv, v_fox_f_bias, v_fox_k_gain, v_w_out, v_ffn2_norm, v_ffn2_w_gate, v_ffn2_w_up, v_ffn2_w_down):
    given = dict(locals())
    def oriented(n, t):
        return jnp.swapaxes(t, 1, 2) if n in TRANSPOSED else t

    w = {n: oriented(n, given[n]) for n in WEIGHT_NAMES}
    xs, mems, tgt = x[0], mem[0], loss_target[0]
    s, d = xs.shape
    my_chip = 2 * lax.axis_index("x") + lax.axis_index("y")
    sel = jnp.stack([my_chip, lax.axis_index("c")]).astype(jnp.int32)
    c_arr = sel[1:]

    def w_in_name(l):
        return "w_in_a" if l < N_A else "w_in_b"

    def halves_of(n):
        rows, cols = w[n].shape[-2:]
        return w[n].reshape(-1, 2, rows // 2, cols)

    def own_of(n, l):
        return 0 if w[n].ndim == 2 else (l - N_A if n == "w_in_b" else l)

    def view(buf, n):
        rows, cols = w[n].shape[-2:]
        return buf.reshape(N_CHIPS, rows, cols) if w[n].ndim == 2 else buf.reshape(N_CHIPS, 1, rows, cols)

    def mixer(l):
        return [(w_in_name(l), l), ("w_mem_kv", l), ("w_out", l)]

    first = [(n, 0) for n in PER_LAYER] + [("w_in_a", 0), ("w_kv", 0)]
    carried = {
        (0, "ffn1"): mixer(1), (0, "mix"): [(n, 1) for n in FFN1 + FFN2[:2]], (0, "ffn2"): [(FFN2[2], 1)],
        (1, "ffn1"): mixer(2), (1, "mix"): [(n, 2) for n in FFN1 + FFN2[:2]], (1, "ffn2"): [(FFN2[2], 2)],
        (2, "ffn1"): [(FFN1[0], 3)], (2, "mix"): [(FFN1[1], 3), (FFN1[2], 3)], (2, "ffn2"): [(FFN2[0], 3)] + mixer(3),
        (3, "ffn1"): [(FFN2[1], 3)], (3, "mix"): [(FFN2[2], 3)],
    }
    bufs = {}
    every = first + [it for items in carried.values() for it in items]
    for layer in range(N_LAYERS):
        its = [it for it in every if it[1] == layer]
        srcs = [halves_of(n) for n, _ in its]
        cast = lambda idx: cast_into_slot([srcs[i] for i in idx], [own_of(*its[i]) for i in idx], sel, BF16)
        bufs.update(zip(its, _by_shape(srcs, cast)))
    lb_buf = cast_into_slot([hgrn_lb_logits.reshape(1, 2, 1, -1)], [0], sel, F32)[0]
    got0 = run_comm(Gather([bufs[it] for it in first] + [lb_buf]), "gather_layer0")
    got = {it: view(b, it[0]) for it, b in zip(first, got0[:-1])}
    w_kv_full = _cols_from_chips(got[("w_kv", 0)][:, None])
    w_kv_full = jnp.pad(w_kv_full, ((0, 0), (0, 0), (0, KV_PAD - w_kv_full.shape[-1])))
    logits3 = jnp.moveaxis(got0[-1].reshape(N_CHIPS, 2, -1), 0, 1).reshape(2, 1, -1)
    lb3 = lb_fwd(logits3)
    w_in, w_mkv, w_o = {}, {}, {}

    def gather_behind(key):
        items = carried.get(key)
        return None if items is None else Gather([bufs[it] for it in items])

    def landed(key, res):
        if res is not None:
            got.update({it: view(b, it[0]) for it, b in zip(carried[key], res)})

    norm3 = {n: w[n].reshape(N_LAYERS, 1, d) for n in ("ffn1_norm", "mix_norm", "mem_norm", "ffn2_norm")}
    kvn3 = kv_norm.reshape(1, 1, d)
    mqg3, mkg3 = _pair_tile(mem_q_gain), _pair_tile(mem_k_gain)
    og3 = hgrn_o_gain.reshape(N_A, 1, LANES)
    fqg3 = _pair_tile(fox_q_gain)
    fkg = jnp.tile(fox_k_gain, 2).reshape(1, LANES)
    fb = jnp.pad(fox_f_bias, (0, LANES - fox_f_bias.shape[0])).reshape(1, LANES)

    sv = [dict() for _ in range(N_LAYERS)]
    h = xs
    kv = None
    for l in range(N_LAYERS):
        t = sv[l]
        t["x0"] = h
        (h, t["a1"], t["b1"]), res = ffn_fwd(h, norm3["ffn1_norm"], l, *[got[(n, l)] for n in FFN1], 0, comm=gather_behind((l, "ffn1")))
        landed((l, "ffn1"), res)
        t["x1"] = h
        w_in[l] = _rows_from_chips(got[(w_in_name(l), l)])
        t["proj"] = proj_fwd(h, norm3["mix_norm"], l, w_in[l], 0, wt=True)
        if l < N_A:
            (main, t["o"]), res = hgrn_fwd(t["proj"], lb3, og3, l, comm=gather_behind((l, "mix")))
            t["qblk"] = 12
        else:
            (main, t["o"], t["lse"]), res = fox_fwd(t["proj"], kv["k"], kv["v"], kv["clf"], kv["clf_t"], fqg3, l - N_A, comm=gather_behind((l, "mix")))
            t["qblk"] = 6
        landed((l, "mix"), res)
        w_mkv[l], w_o[l] = _rows_from_chips(got[("w_mem_kv", l)]), _rows_from_chips(got[("w_out", l)])
        t["kvm"] = proj_fwd(mems, norm3["mem_norm"], l, w_mkv[l], 0)
        memo = memattn_fwd(t["proj"], t["qblk"], t["kvm"], mqg3, mkg3, l)
        t["mixed"] = jnp.concatenate([main, memo], axis=-1)
        h = mm_res(h, t["mixed"], w_o[l], 0)
        t["x2"] = h
        (h, t["a2"], t["b2"]), res = ffn_fwd(h, norm3["ffn2_norm"], l, *[got[(n, l)] for n in FFN2], 0, comm=gather_behind((l, "ffn2")))
        landed((l, "ffn2"), res)
        if l == N_A - 1:
            kv = {"x": h, "kvf": proj_fwd(h, kvn3, 0, w_kv_full, 0)}
            kv["k"], kv["v"], kv["clf"] = kvprep_fwd(kv["kvf"], fkg, fb)
            kv["clf_t"] = kv["clf"][:, :16].T

    loss_local, dx = loss_head(h, tgt)

    nc = N_CHIPS
    fc = ffn1_w_down.shape[1]
    gsplit = [dict() for _ in range(N_LAYERS)]

    def halves_layout(g):
        rr, cc = g.shape
        return jnp.transpose(g.reshape(nc, 2, rr // (2 * nc), cc), (1, 0, 2, 3))

    def group_layout(l):
        lay = {n: b[0].reshape(2, nc, fc // 2, d) for n, b in gsplit[l].items()}
        lay16 = {n: b[1].reshape(2, nc, fc // 2, d) for n, b in gsplit[l].items()}
        lay["w_mem_kv"], lay["w_out"] = halves_layout(dw_mkv[l]), halves_layout(dw_o[l])
        lay[w_in_name(l)] = halves_layout(dw_in[l])
        names = PER_LAYER + [w_in_name(l)]
        if l == N_A - 1:
            kv_cols = w_kv.shape[-1] * nc
            lay["w_kv"] = jnp.transpose(dw_kv[:, :kv_cols].reshape(2, d // 2, nc, kv_cols // nc), (0, 2, 1, 3))
            names = names + ["w_kv"]
        return names, [lay[n] for n in names], [lay16[n] if n in lay16 else bf(lay[n]) for n in names]

    def pair_sums(gl, recv):
        return _by_shape(gl, lambda idx: pair_sum([gl[i] for i in idx], [recv[i] for i in idx], c_arr))

    def chip_sums(ps, qs):
        return _by_shape(ps, lambda idx: chip_sum([ps[i] for i in idx], [qs[i] for i in idx], sel))

    n_ffn = len(FFN1) + len(FFN2)
    riding = {l: l + 1 for l in range(N_LAYERS - 1)}
    reduced = {}
    unshared = None
    dw_in, dw_o, dw_mkv = [None] * N_LAYERS, [None] * N_LAYERS, [None] * N_LAYERS
    sg = {n: [None] * N_LAYERS for n in ("ffn1_norm", "mix_norm", "mem_norm", "ffn2_norm", "mem_q_gain", "mem_k_gain")}
    sg["hgrn_o_gain"], sg["fox_q_gain"], dlb = [None] * N_A, [None] * (N_LAYERS - N_A), [None] * N_A
    dk_sh = jnp.zeros((s, KV_MAIN), F32)
    dv_sh = jnp.zeros((s, KV_MAIN), F32)
    dclf = jnp.zeros((s, LANES), F32)
    zero_mem = jnp.zeros(mems.shape, F32)
    dw_kv = None
    for l in reversed(range(N_LAYERS)):
        t = sv[l]
        if l == N_A - 1:
            dkvf, dfkg, dfb = kvprep_bwd(kv["kvf"], fkg, fb, dk_sh, dv_sh, dclf)
            dx, sg["kv_norm"], xn_kv, dpb = proj_bwd(kv["x"], kvn3, 0, [dkvf], w_kv_full, 0, dx)
            dw_kv = wgrad(xn_kv, dpb)
        ride = riding.get(l)
        comms = []
        if unshared is not None:
            comms.append(PairShare(unshared[2]))
        if ride is not None:
            names_r, gl_r, gl16_r = group_layout(ride)
            comms.append(PairExchange(gl16_r))
        comm = Multi(comms) if comms else None
        (dx, da, db, hm, xn, dyb, sg["ffn2_norm"][l]), res = ffn_bwd(t["x2"], norm3["ffn2_norm"], l, dx, t["a2"], t["b2"], *[got[(n, l)] for n in FFN2], 0, comm=comm)
        if comm is not None:
            res = comm.split(res)
            if unshared is not None:
                reduced[unshared[0]] = dict(zip(unshared[1], res.pop(0)))
                unshared = None
            if ride is not None:
                partial_r = pair_sums(gl_r, res.pop(0))

        def ffn_wgrads(which, da, db, hm, xn, dyb):
            for n, (a_, b_) in zip(which, ((da, xn), (db, xn), (hm, dyb))):
                gsplit[l][n] = wgrad(a_, b_, split=True)

        ffn_wgrads(FFN2, da, db, hm, xn, dyb)
        dmixed, dxb = mm_nt(dx, w_o[l], 0)
        dw_o[l] = wgrad(t["mixed"], dxb)
        dqm, dkvm, dmq, dmk = memattn_bwd(t["proj"], t["qblk"], t["kvm"], mqg3, mkg3, l, dmixed)
        sg["mem_q_gain"][l], sg["mem_k_gain"][l] = _pair_fold(dmq), _pair_fold(dmk)
        _, sg["mem_norm"][l], memn, dkvmb = proj_bwd(mems, norm3["mem_norm"], l, [dkvm], w_mkv[l], 0, zero_mem)
        dw_mkv[l] = wgrad(memn, dkvmb)
        comm = ChipExchange(partial_r[:n_ffn]) if ride is not None else None
        if l < N_A:
            (dzq, dzf, dvi, dzg, dlb[l], sg["hgrn_o_gain"][l]), res = hgrn_bwd(t["proj"], lb3, og3, l, t["o"], dmixed, comm=comm)
            parts, tmw = [dzq, dzf, dvi, dzg, dqm], 13 * LANES
        else:
            lse_t = t["lse"].reshape(s, 6, LANES)[:, :, :2].reshape(s, 12).T
            lse_t = jnp.pad(lse_t, ((0, 4), (0, 0)))
            (dq, dgate, dk_sh, dv_sh, dclf, dfq), res = fox_bwd(t["proj"], kv["k"], kv["v"], kv["clf"], kv["clf_t"], fqg3, l - N_A, t["o"], t["lse"], lse_t, dmixed, dk_sh, dv_sh, dclf, comm=comm)
            sg["fox_q_gain"][l - N_A] = _pair_fold(dfq)
            parts, tmw = [dq, dgate, dqm], 7 * LANES
        if ride is not None:
            landed_r = list(res)
        dx, sg["mix_norm"][l], hn, dpb = proj_bwd(t["x1"], norm3["mix_norm"], l, parts, w_in[l], 0, dx, wt=True)
        dw_in[l] = wgrad(dpb, hn, tm=tmw)
        comm = ChipExchange(partial_r[n_ffn:]) if ride is not None else None
        (dx, da, db, hm, xn, dyb, sg["ffn1_norm"][l]), res = ffn_bwd(t["x0"], norm3["ffn1_norm"], l, dx, t["a1"], t["b1"], *[got[(n, l)] for n in FFN1], 0, comm=comm)
        if ride is not None:
            unshared = (ride, names_r, chip_sums(partial_r, landed_r + list(res)))
        ffn_wgrads(FFN1, da, db, hm, xn, dyb)

    names0, gl0, gl16_0 = group_layout(0)
    recv0 = run_comm(PairExchange(gl16_0), "pair_exchange")
    partial0 = pair_sums(gl0, recv0)
    landed0 = run_comm(ChipExchange(partial0), "chip_exchange")
    mine0 = chip_sums(partial0, list(landed0))
    both = run_comm(PairShare(unshared[2] + mine0), "pair_share")
    reduced[unshared[0]] = dict(zip(unshared[1], both[: len(unshared[1])]))
    reduced[0] = dict(zip(names0, both[len(unshared[1]) :]))
    gparts = {n: [reduced[l][n] for l in range(N_LAYERS)] for n in PER_LAYER}
    gparts["w_in_a"] = [reduced[l]["w_in_a"] for l in range(N_A)]
    gparts["w_in_b"] = [reduced[l]["w_in_b"] for l in range(N_A, N_LAYERS)]
    gparts["w_kv"] = [reduced[N_A - 1]["w_kv"]]

    dlogits = lb_bwd(logits3, dlb[1]).reshape(2, -1)
    small = {
        "ffn1_norm": jnp.concatenate(sg["ffn1_norm"]), "mix_norm": jnp.concatenate(sg["mix_norm"]),
        "mem_norm": jnp.concatenate(sg["mem_norm"]), "ffn2_norm": jnp.concatenate(sg["ffn2_norm"]),
        "mem_q_gain": jnp.concatenate(sg["mem_q_gain"]), "mem_k_gain": jnp.concatenate(sg["mem_k_gain"]),
        "hgrn_o_gain": jnp.concatenate(sg["hgrn_o_gain"]), "fox_q_gain": jnp.concatenate(sg["fox_q_gain"]),
        "kv_norm": sg["kv_norm"], "fox_f_bias": dfb[:, : fox_f_bias.shape[0]], "fox_k_gain": _pair_fold(dfkg),
        "hgrn_lb_logits": dlogits,
    }
    flat = [small[n].reshape(-1) for n in SMALL] + [loss_local.reshape(-1)]
    sizes = [f.shape[0] for f in flat]
    total = sum(sizes)
    padded = -(-total // (8 * LANES)) * (8 * LANES)
    packed = jnp.pad(jnp.concatenate(flat), (0, padded - total)).reshape(-1, LANES)
    summed = small_allreduce(packed).reshape(-1)
    off = 0
    for n, sz in zip(SMALL, sizes[:-1]):
        gparts[n] = [summed[off : off + sz].reshape(dlogits.shape if n == "hgrn_lb_logits" else w[n].shape)]
        off += sz
    loss = summed[off]
    lbw = hgrn_lb_logits.shape[1]
    gparts["hgrn_lb_logits"] = [lax.dynamic_slice_in_dim(gparts["hgrn_lb_logits"][0], my_chip * lbw, lbw, axis=1)]

    grads, delta, new_m, new_v = {}, {}, {}, {}
    for n in WEIGHT_NAMES:
        res = adamw(w[n], gparts[n], oriented(n, given["m_" + n]), oriented(n, given["v_" + n]))
        grads[n], delta[n], new_m[n], new_v[n] = (oriented(n, t) for t in res)
    return (loss, dx[None], *[grads[n] for n in WEIGHT_NAMES], *[delta[n] for n in WEIGHT_NAMES],
            *[new_m[n] for n in WEIGHT_NAMES], *[new_v[n] for n in WEIGHT_NAMES])
```

```python
import functools

import jax
import jax.numpy as jnp
from jax import lax
from jax.experimental import pallas as pl
from jax.experimental.pallas import tpu as pltpu

F32, BF16 = jnp.float32, jnp.bfloat16
HI = lax.Precision.HIGHEST
EPS = 1e-6
MESH = pl.DeviceIdType.MESH
ANY = pl.BlockSpec(memory_space=pl.ANY)

VMEM_LIMIT_BYTES = 56 << 20
N_CHIPS = 4
N_DEV = 8
LANES = 128
HEAD64 = 64
CHUNK = 64
SUB = 32
HGRN_HEADS_PER_STEP = 2
TQ = 256
TOK = 256

ADAM_LR, ADAM_B1, ADAM_B2, ADAM_EPS, ADAM_WD, ADAM_STEP = 0.001, 0.9, 0.999, 1e-08, 0.01, 10


def _cparams(sem=None, **kw):
    return pltpu.CompilerParams(dimension_semantics=sem, vmem_limit_bytes=VMEM_LIMIT_BYTES, **kw)


def _mm(a, b, dims, prec=None):
    return lax.dot_general(a, b, (dims, ((), ())), preferred_element_type=F32, precision=prec)


def dot_nn(a, b, prec=None):
    return _mm(a, b, ((1,), (0,)), prec)


def dot_nt(a, b, prec=None):
    return _mm(a, b, ((1,), (1,)), prec)


def dot_tn(a, b, prec=None):
    return _mm(a, b, ((0,), (0,)), prec)


def bf(v):
    return v.astype(BF16)


def _sigmoid(z):
    return jax.nn.sigmoid(z)


def _dsilu(z, s):
    return s * (1.0 + z * (1.0 - s))


def _rms(x):
    r = lax.rsqrt(jnp.mean(x * x, axis=-1, keepdims=True) + EPS)
    return x * r, r


def _rms_bwd(dxn, u, r, g):
    du = dxn * g
    dx = r * (du - u * jnp.mean(du * u, axis=-1, keepdims=True))
    return dx, jnp.sum(dxn * u, axis=0, keepdims=True)


def _lane_mask0(shape):
    return lax.broadcasted_iota(jnp.int32, shape, len(shape) - 1) < HEAD64


def _rms64(x, m0):
    sq = x * x
    s0 = jnp.sum(jnp.where(m0, sq, 0.0), axis=-1, keepdims=True)
    s1 = jnp.sum(jnp.where(m0, 0.0, sq), axis=-1, keepdims=True)
    r = lax.rsqrt(jnp.where(m0, s0, s1) * (1.0 / HEAD64) + EPS)
    return x * r, r


def _rms64_bwd(dxn, u, r, g, m0):
    du = dxn * g
    t = du * u
    t0 = jnp.sum(jnp.where(m0, t, 0.0), axis=-1, keepdims=True)
    t1 = jnp.sum(jnp.where(m0, 0.0, t), axis=-1, keepdims=True)
    dx = r * (du - u * (jnp.where(m0, t0, t1) * (1.0 / HEAD64)))
    return dx, jnp.sum(dxn * u, axis=0, keepdims=True)


def _tok(s):
    return TOK if s % TOK == 0 else s


def _const(shape):
    return pl.BlockSpec(shape, lambda *_: (0,) * len(shape))


def ffn_fwd(x, gain3, l, wg, wu, wd, wl, comm=None):
    s, d = x.shape
    nc, _, fc, _ = wg.shape
    tm = _tok(s)

    def body(x_ref, g_ref, wg_ref, wu_ref, wd_ref, xo_ref, a_ref, b_ref):
        xv = x_ref[...]
        u, _ = _rms(xv)
        xn = bf(u * g_ref[...])
        y = jnp.zeros((tm, d), F32)
        for c in range(nc):
            a = dot_nt(xn, wg_ref[c])
            b = dot_nt(xn, wu_ref[c])
            a_ref[c] = bf(a)
            b_ref[c] = bf(b)
            y = y + dot_nn(bf(a * _sigmoid(a) * b), wd_ref[c])
        xo_ref[...] = xv + 0.5 * y

    wspec = pl.BlockSpec((nc, None, fc, d), lambda i: (0, wl, 0, 0), pipeline_mode=pl.Buffered(1))
    wdspec = pl.BlockSpec((nc, None, fc, d), lambda i: (0, wl, 0, 0), pipeline_mode=pl.Buffered(1))
    row = pl.BlockSpec((tm, d), lambda i: (i, 0))
    act = pl.BlockSpec((nc, tm, fc), lambda i: (0, i, 0))
    return _carry(
        body,
        comm,
        name="ffn_fwd",
        grid=(s // tm,),
        in_specs=[row, pl.BlockSpec((None, 1, d), lambda i: (l, 0, 0)), wspec, wspec, wdspec],
        out_specs=[row, act, act],
        out_shape=[
            jax.ShapeDtypeStruct((s, d), F32),
            jax.ShapeDtypeStruct((nc, s, fc), BF16),
            jax.ShapeDtypeStruct((nc, s, fc), BF16),
        ],
        scratch_shapes=[],
        args=(x, gain3, wg, wu, wd),
    )


def ffn_bwd(x, gain3, l, dout, a, b, wg, wu, wd, wl, comm=None):
    s, d = x.shape
    nc, _, fc, _ = wg.shape
    tm = _tok(s)

    def body(x_ref, g_ref, do_ref, a_ref, b_ref, wg_ref, wu_ref, wd_ref, dx_ref, da_ref, db_ref, hm_ref, xn_ref, dy_ref, dg_ref):
        xv = x_ref[...]
        g = g_ref[...]
        u, r = _rms(xv)
        xn_ref[...] = bf(u * g)
        dout = do_ref[...]
        dy = bf(0.5 * dout)
        dy_ref[...] = dy
        dxn = jnp.zeros((tm, d), F32)
        for c in range(nc):
            av = a_ref[c].astype(F32)
            bv = b_ref[c].astype(F32)
            sg = _sigmoid(av)
            sl = av * sg
            dh = dot_nt(dy, wd_ref[c])
            da = bf(dh * bv * _dsilu(av, sg))
            db = bf(dh * sl)
            da_ref[c] = da
            db_ref[c] = db
            hm_ref[c] = bf(sl * bv)
            dxn = dxn + dot_nn(da, wg_ref[c]) + dot_nn(db, wu_ref[c])
        dx, dg = _rms_bwd(dxn, u, r, g)
        dx_ref[...] = dout + dx

        @pl.when(pl.program_id(0) == 0)
        def _():
            dg_ref[...] = jnp.zeros_like(dg_ref)

        dg_ref[...] += dg

    wspec = pl.BlockSpec((nc, None, fc, d), lambda i: (0, wl, 0, 0), pipeline_mode=pl.Buffered(1))
    wdspec = pl.BlockSpec((nc, None, fc, d), lambda i: (0, wl, 0, 0), pipeline_mode=pl.Buffered(1))
    row = pl.BlockSpec((tm, d), lambda i: (i, 0))
    act = pl.BlockSpec((nc, tm, fc), lambda i: (0, i, 0))
    act_shape = jax.ShapeDtypeStruct((nc, s, fc), BF16)
    return _carry(
        body,
        comm,
        name="ffn_bwd",
        grid=(s // tm,),
        in_specs=[row, pl.BlockSpec((None, 1, d), lambda i: (l, 0, 0)), row, act, act, wspec, wspec, wdspec],
        out_specs=[row, act, act, act, row, row, _const((1, d))],
        out_shape=[
            jax.ShapeDtypeStruct((s, d), F32),
            act_shape,
            act_shape,
            act_shape,
            jax.ShapeDtypeStruct((s, d), BF16),
            jax.ShapeDtypeStruct((s, d), BF16),
            jax.ShapeDtypeStruct((1, d), F32),
        ],
        scratch_shapes=[],
        args=(x, gain3, dout, a, b, wg, wu, wd),
    )


def wgrad(a, b, tn=None, tm=None, split=False, chip_rows=None):
    ca = a.shape[0] if a.ndim == 3 else 1
    cb = b.shape[0] if b.ndim == 3 else 1
    nc = max(ca, cb)
    s, m = a.shape[-2:]
    n = b.shape[-1]
    tn = n if tn is None else tn
    assert n % tn == 0
    tm = m if tm is None else tm
    per_tile = None if chip_rows is None else tm // chip_rows

    def body(*refs):
        a_ref, b_ref = refs[0], refs[1]
        res = dot_tn(a_ref[...], b_ref[...])
        if split:
            for o in refs[2:]:
                o[0] = res[: m // 2].astype(o.dtype)
                o[1] = res[m // 2 :].astype(o.dtype)
        elif chip_rows is not None:
            hr = chip_rows // 2
            for o in refs[2:]:
                for k in range(per_tile):
                    for hf in range(2):
                        r0 = k * chip_rows + hf * hr
                        o[hf, k] = res[r0 : r0 + hr].astype(o.dtype)
        else:
            refs[2][...] = res

    params = _cparams(("arbitrary", "arbitrary"))
    if not split:
        assert nc == 1 and a.ndim == 2 and b.ndim == 2 and m % tm == 0
        in_specs = [pl.BlockSpec((s, tm), lambda i, j: (0, i)), pl.BlockSpec((s, tn), lambda i, j: (0, j))]
        if chip_rows is None:
            return pl.pallas_call(
                body,
                name="wgrad",
                grid=(m // tm, n // tn),
                in_specs=in_specs,
                out_specs=pl.BlockSpec((tm, tn), lambda i, j: (i, j)),
                out_shape=jax.ShapeDtypeStruct((m, n), F32),
                compiler_params=params,
            )(a, b)
        assert tm % chip_rows == 0
        laid = pl.BlockSpec((2, per_tile, chip_rows // 2, tn), lambda i, j: (0, i, 0, j))
        shape = (2, m // chip_rows, chip_rows // 2, n)
        return pl.pallas_call(
            body,
            name="wgrad_chips",
            grid=(m // tm, n // tn),
            in_specs=in_specs,
            out_specs=[laid, laid],
            out_shape=[jax.ShapeDtypeStruct(shape, F32), jax.ShapeDtypeStruct(shape, BF16)],
            compiler_params=params,
        )(a, b)
    a_spec = pl.BlockSpec((None, s, m), lambda c, j: (c, 0, 0)) if a.ndim == 3 else pl.BlockSpec((s, m), lambda c, j: (0, 0))
    b_spec = pl.BlockSpec((None, s, tn), lambda c, j: (c, 0, j)) if b.ndim == 3 else pl.BlockSpec((s, tn), lambda c, j: (0, j))
    halves = pl.BlockSpec((2, None, None, m // 2, tn), lambda c, j: (0, c, 0, 0, j))
    return pl.pallas_call(
        body,
        name="wgrad_split",
        grid=(nc, n // tn),
        in_specs=[a_spec, b_spec],
        out_specs=[halves, halves],
        out_shape=[jax.ShapeDtypeStruct((2, nc, 1, m // 2, n), F32), jax.ShapeDtypeStruct((2, nc, 1, m // 2, n), BF16)],
        compiler_params=params,
    )(a, b)


def proj_fwd(x, gain3, l, w, wl, wt=False):
    s, d = x.shape
    n = w.shape[1] if wt else w.shape[2]
    tm = _tok(s)

    def body(x_ref, g_ref, w_ref, o_ref):
        u, _ = _rms(x_ref[...])
        xn = bf(u * g_ref[...])
        o_ref[...] = dot_nt(xn, w_ref[...]) if wt else dot_nn(xn, w_ref[...])

    return pl.pallas_call(
        body,
        name="proj_fwd",
        grid=(s // tm,),
        in_specs=[
            pl.BlockSpec((tm, d), lambda i: (i, 0)),
            pl.BlockSpec((None, 1, d), lambda i: (l, 0, 0)),
            pl.BlockSpec((None,) + w.shape[1:], lambda i: (wl, 0, 0)),
        ],
        out_specs=pl.BlockSpec((tm, n), lambda i: (i, 0)),
        out_shape=jax.ShapeDtypeStruct((s, n), F32),
        compiler_params=_cparams(("arbitrary",)),
    )(x, gain3, w)


def proj_bwd(x, gain3, l, parts, w, wl, dx_in, wt=False):
    s, d = x.shape
    n = w.shape[1] if wt else w.shape[2]
    widths = [p.shape[1] for p in parts]
    assert sum(widths) == n
    tm = _tok(s)
    npart = len(parts)

    def body(*refs):
        x_ref, g_ref, w_ref, dxin_ref = refs[:4]
        p_refs = refs[4 : 4 + npart]
        dx_ref, dg_ref, xn_ref, dpb_ref = refs[4 + npart :]
        g = g_ref[...]
        u, r = _rms(x_ref[...])
        xn_ref[...] = bf(u * g)
        dxn = jnp.zeros((tm, d), F32)
        off = 0
        for p_ref, wd_ in zip(p_refs, widths):
            dp = bf(p_ref[...])
            dpb_ref[:, off : off + wd_] = dp
            dxn = dxn + (dot_nn(dp, w_ref[off : off + wd_, :]) if wt else dot_nt(dp, w_ref[:, off : off + wd_]))
            off += wd_
        dx, dg = _rms_bwd(dxn, u, r, g)
        dx_ref[...] = dxin_ref[...] + dx

        @pl.when(pl.program_id(0) == 0)
        def _():
            dg_ref[...] = jnp.zeros_like(dg_ref)

        dg_ref[...] += dg

    row = pl.BlockSpec((tm, d), lambda i: (i, 0))
    return pl.pallas_call(
        body,
        name="proj_bwd",
        grid=(s // tm,),
        in_specs=[row, pl.BlockSpec((None, 1, d), lambda i: (l, 0, 0)), pl.BlockSpec((None,) + w.shape[1:], lambda i: (wl, 0, 0)), row]
        + [pl.BlockSpec((tm, wd_), lambda i: (i, 0)) for wd_ in widths],
        out_specs=[row, _const((1, d)), row, pl.BlockSpec((tm, n), lambda i: (i, 0))],
        out_shape=[
            jax.ShapeDtypeStruct((s, d), F32),
            jax.ShapeDtypeStruct((1, d), F32),
            jax.ShapeDtypeStruct((s, d), BF16),
            jax.ShapeDtypeStruct((s, n), BF16),
        ],
        compiler_params=_cparams(("arbitrary",)),
    )(x, gain3, w, dx_in, *parts)


def mm_res(x, a, w, l):
    s, d = x.shape
    k = a.shape[1]
    tm = _tok(s)

    def body(x_ref, a_ref, w_ref, o_ref):
        o_ref[...] = x_ref[...] + dot_nn(a_ref[...], w_ref[...])

    return pl.pallas_call(
        body,
        name="mm_res",
        grid=(s // tm,),
        in_specs=[
            pl.BlockSpec((tm, d), lambda i: (i, 0)),
            pl.BlockSpec((tm, k), lambda i: (i, 0)),
            pl.BlockSpec((None, k, d), lambda i: (l, 0, 0)),
        ],
        out_specs=pl.BlockSpec((tm, d), lambda i: (i, 0)),
        out_shape=jax.ShapeDtypeStruct((s, d), F32),
        compiler_params=_cparams(("arbitrary",)),
    )(x, a, w)


def mm_nt(dx, w, l):
    s, d = dx.shape
    k = w.shape[1]
    tm = _tok(s)

    def body(dx_ref, w_ref, o_ref, dxb_ref):
        dxb = bf(dx_ref[...])
        dxb_ref[...] = dxb
        o_ref[...] = dot_nt(dxb, w_ref[...])

    return pl.pallas_call(
        body,
        name="mm_nt",
        grid=(s // tm,),
        in_specs=[pl.BlockSpec((tm, d), lambda i: (i, 0)), pl.BlockSpec((None, k, d), lambda i: (l, 0, 0))],
        out_specs=[pl.BlockSpec((tm, k), lambda i: (i, 0)), pl.BlockSpec((tm, d), lambda i: (i, 0))],
        out_shape=[jax.ShapeDtypeStruct((s, k), F32), jax.ShapeDtypeStruct((s, d), BF16)],
        compiler_params=_cparams(("arbitrary",)),
    )(dx, w)


def lb_fwd(logits3):
    def body(l_ref, o_ref):
        l0, l1 = l_ref[0], l_ref[1]
        m = jnp.maximum(l0, l1)
        e0, e1 = jnp.exp(l0 - m), jnp.exp(l1 - m)
        p0, p1 = e0 / (e0 + e1), e1 / (e0 + e1)
        o_ref[0] = p0 - p0
        o_ref[1] = (p0 + p1) - p0

    return pl.pallas_call(body, name="lb_fwd", out_shape=jax.ShapeDtypeStruct(logits3.shape, F32))(logits3)


def lb_bwd(logits3, dlb1):
    def body(l_ref, d_ref, o_ref):
        l0, l1 = l_ref[0], l_ref[1]
        m = jnp.maximum(l0, l1)
        e0, e1 = jnp.exp(l0 - m), jnp.exp(l1 - m)
        p0, p1 = e0 / (e0 + e1), e1 / (e0 + e1)
        t = d_ref[...] * p0 * p1
        o_ref[0] = -t
        o_ref[1] = t

    return pl.pallas_call(body, name="lb_bwd", out_shape=jax.ShapeDtypeStruct(logits3.shape, F32))(logits3, dlb1)


def _hgrn_gates(zq, zf, lb):
    sf = _sigmoid(zf)
    f = lb + (1.0 - lb) * sf
    sq = _sigmoid(zq)
    return sf, f, jnp.log(f), 1.0 - f, sq, zq * sq


def _tri(n, upper=False):
    r = lax.broadcasted_iota(jnp.int32, (n, n), 0)
    c = lax.broadcasted_iota(jnp.int32, (n, n), 1)
    return jnp.where((c >= r) if upper else (r >= c), 1.0, 0.0).astype(F32)


def hgrn_fwd(proj, lb3, og3, l, comm=None):
    s = proj.shape[0]
    nh = 6
    n_chunk = s // CHUNK
    nsub = CHUNK // SUB

    hb = HGRN_HEADS_PER_STEP
    wide = hb * LANES

    def body(zq_ref, zf_ref, vi_ref, zg_ref, lb_ref, og_ref, main_ref, o_ref, q_a, k_a, v_a, c_a):
        og = og_ref[...]
        tril = _tri(CHUNK)
        rowi = lax.broadcasted_iota(jnp.int32, (SUB, LANES), 0)

        def one_head(hd, rows, st):
            cols = slice(hd * LANES, (hd + 1) * LANES)
            q_s, k_s, v_s, c_s = q_a.at[hd], k_a.at[hd], v_a.at[hd], c_a.at[hd]
            zg = zg_ref[rows, cols]
            _, _, lf, k, _, q = _hgrn_gates(zq_ref[rows, cols], zf_ref[rows, cols], lb_ref[:, cols])
            v = vi_ref[rows, cols]
            c = dot_nn(tril, lf, HI)
            q_s[...] = q
            k_s[...] = k
            v_s[...] = v
            c_s[...] = c
            o_inter = dot_nt(q * jnp.exp(c), st, HI)
            parts = []
            for i in range(nsub):
                lo = i * SUB
                blk = pl.ds(lo, SUB)
                qb, cb = q_s[blk, :], c_s[blk, :]
                ob = o_inter[lo : lo + SUB]
                if i > 0:
                    rr = c_s[pl.ds(lo - 1, 1), :]
                    qt = qb * jnp.exp(cb - rr)
                    kt = k_s[pl.ds(0, lo), :] * jnp.exp(rr - c_s[pl.ds(0, lo), :])
                    ob = ob + dot_nn(dot_nt(qt, kt, HI), v_s[pl.ds(0, lo), :], HI)
                for t in range(SUB):
                    e = jnp.where(rowi >= t, jnp.exp(cb - c_s[pl.ds(lo + t, 1), :]), 0.0)
                    a = jnp.sum(qb * k_s[pl.ds(lo + t, 1), :] * e, axis=-1, keepdims=True)
                    ob = ob + a * v_s[pl.ds(lo + t, 1), :]
                parts.append(ob)
            o = jnp.concatenate(parts, axis=0)
            ce = c_s[pl.ds(CHUNK - 1, 1), :]
            st = st * jnp.exp(ce) + dot_tn(v, k * jnp.exp(ce - c), HI)
            on, _ = _rms(o)
            o_ref[rows, cols] = o
            main_ref[rows, cols] = bf(on * og * (zg * _sigmoid(zg)))
            return st

        def chunk(ci, sts):
            rows = pl.ds(pl.multiple_of(ci * CHUNK, CHUNK), CHUNK)
            return tuple(one_head(hd, rows, sts[hd]) for hd in range(hb))

        lax.fori_loop(0, n_chunk, chunk, tuple(jnp.zeros((LANES, LANES), F32) for _ in range(hb)))

    def col(k):
        return pl.BlockSpec((s, wide), lambda h: (0, k * (nh // hb) + h))

    vec = pl.BlockSpec((None, 1, wide), lambda h: (l, 0, h))
    return _carry(
        body,
        comm,
        name="hgrn_fwd",
        grid=(nh // hb,),
        in_specs=[col(0), col(1), col(2), col(3), vec, pl.BlockSpec((None, 1, LANES), lambda h: (l, 0, 0))],
        out_specs=[pl.BlockSpec((s, wide), lambda h: (0, h))] * 2,
        out_shape=[jax.ShapeDtypeStruct((s, nh * LANES), BF16), jax.ShapeDtypeStruct((s, nh * LANES), F32)],
        scratch_shapes=[pltpu.VMEM((hb, CHUNK, LANES), F32)] * 4,
        args=(proj, proj, proj, proj, lb3, og3),
    )


def hgrn_bwd(proj, lb3, og3, l, o, dmixed, comm=None):
    s = proj.shape[0]
    nh = 6
    n_chunk = s // CHUNK
    nsub = CHUNK // SUB

    hb = HGRN_HEADS_PER_STEP
    wide = hb * LANES

    def body(zq_ref, zf_ref, vi_ref, zg_ref, lb_ref, og_ref, o_ref, dm_ref,
             dzq_ref, dzf_ref, dvi_ref, dzg_ref, dlb_ref, dog_ref,
             st_a, q_a, k_a, v_a, c_a, do_a, dq_a, dk_a, dv_a, acc_a):
        og = og_ref[...]
        tril = _tri(CHUNK)
        triu = _tri(CHUNK, upper=True)
        rowi = lax.broadcasted_iota(jnp.int32, (SUB, LANES), 0)

        def fwd_head(hd, ci, rows, st):
            cols = slice(hd * LANES, (hd + 1) * LANES)
            _, _, lf, k, _, _ = _hgrn_gates(zq_ref[rows, cols], zf_ref[rows, cols], lb_ref[:, cols])
            c = dot_nn(tril, lf, HI)
            ce = jnp.sum(lf, axis=0, keepdims=True)
            st_a[hd, ci] = st
            return st * jnp.exp(ce) + dot_tn(vi_ref[rows, cols], k * jnp.exp(ce - c), HI)

        def fwd_chunk(ci, sts):
            rows = pl.ds(pl.multiple_of(ci * CHUNK, CHUNK), CHUNK)
            return tuple(fwd_head(hd, ci, rows, sts[hd]) for hd in range(hb))

        lax.fori_loop(0, n_chunk, fwd_chunk, tuple(jnp.zeros((LANES, LANES), F32) for _ in range(hb)))
        acc_a[...] = jnp.zeros_like(acc_a)

        def bwd_head(hd, ci, rows, carry):
            dst, cg = carry
            cols = slice(hd * LANES, (hd + 1) * LANES)
            q_s, k_s, v_s, c_s, do_s = q_a.at[hd], k_a.at[hd], v_a.at[hd], c_a.at[hd], do_a.at[hd]
            dq_s, dk_s, dv_s, acc_s = dq_a.at[hd], dk_a.at[hd], dv_a.at[hd], acc_a.at[hd]
            lb = lb_ref[:, cols]
            zq, zf, zg = zq_ref[rows, cols], zf_ref[rows, cols], zg_ref[rows, cols]
            sf, f, lf, k, sq, q = _hgrn_gates(zq, zf, lb)
            v = vi_ref[rows, cols]
            c = dot_nn(tril, lf, HI)
            st = st_a[hd, ci]
            on, r = _rms(o_ref[rows, cols])
            sg = _sigmoid(zg)
            dmain = dm_ref[rows, cols]
            dy = dmain * (zg * sg)
            dzg_ref[rows, cols] = dmain * (on * og) * _dsilu(zg, sg)
            do, dog = _rms_bwd(dy, on, r, og)
            acc_s[pl.ds(0, 1), :] += dog
            q_s[...] = q
            k_s[...] = k
            v_s[...] = v
            c_s[...] = c
            do_s[...] = do
            ce = c_s[pl.ds(CHUNK - 1, 1), :]
            eq = jnp.exp(c)
            ek = jnp.exp(ce - c)
            qt_all = q * eq
            dq_s[...] = dot_nn(do, st, HI) * eq
            dv_s[...] = dot_nt(k * ek, dst, HI)
            dk_s[...] = dot_nn(v, dst, HI) * ek
            dst = dst * jnp.exp(ce) + dot_tn(do, qt_all, HI)
            for i in range(nsub):
                lo = i * SUB
                blk = pl.ds(lo, SUB)
                qb, cb, dob = q_s[blk, :], c_s[blk, :], do_s[blk, :]
                if i > 0:
                    prev = pl.ds(0, lo)
                    rr = c_s[pl.ds(lo - 1, 1), :]
                    eqi = jnp.exp(cb - rr)
                    eki = jnp.exp(rr - c_s[prev, :])
                    qt = qb * eqi
                    kt = k_s[prev, :] * eki
                    amat = dot_nt(qt, kt, HI)
                    damat = dot_nt(dob, v_s[prev, :], HI)
                    dv_s[prev, :] += dot_tn(amat, dob, HI)
                    dq_s[blk, :] += dot_nn(damat, kt, HI) * eqi
                    dk_s[prev, :] += dot_tn(damat, qt, HI) * eki
                dqb = jnp.zeros((SUB, LANES), F32)
                for t in range(SUB):
                    row = pl.ds(lo + t, 1)
                    e = jnp.where(rowi >= t, jnp.exp(cb - c_s[row, :]), 0.0)
                    kr = k_s[row, :]
                    a = jnp.sum(qb * kr * e, axis=-1, keepdims=True)
                    da = jnp.sum(dob * v_s[row, :], axis=-1, keepdims=True)
                    dv_s[row, :] += jnp.sum(a * dob, axis=0, keepdims=True)
                    dqb = dqb + da * kr * e
                    dk_s[row, :] += jnp.sum(da * qb * e, axis=0, keepdims=True)
                dq_s[blk, :] += dqb
            dq, dk = dq_s[...], dk_s[...]
            dg = q * dq - k * dk
            dlf = dot_nn(triu, dg, HI) + cg
            cg = cg + jnp.sum(dg, axis=0, keepdims=True)
            df = dlf / f - dk
            dzf_ref[rows, cols] = df * (1.0 - lb) * sf * (1.0 - sf)
            acc_s[pl.ds(1, 1), :] += jnp.sum(df * (1.0 - sf), axis=0, keepdims=True)
            dzq_ref[rows, cols] = dq * _dsilu(zq, sq)
            dvi_ref[rows, cols] = dv_s[...]
            return dst, cg

        def bwd_chunk(jj, carries):
            ci = n_chunk - 1 - jj
            rows = pl.ds(pl.multiple_of(ci * CHUNK, CHUNK), CHUNK)
            return tuple(bwd_head(hd, ci, rows, carries[hd]) for hd in range(hb))

        zero = (jnp.zeros((LANES, LANES), F32), jnp.zeros((1, LANES), F32))
        lax.fori_loop(0, n_chunk, bwd_chunk, tuple(zero for _ in range(hb)))

        @pl.when(pl.program_id(0) == 0)
        def _():
            dog_ref[...] = jnp.zeros_like(dog_ref)

        for hd in range(hb):
            dlb_ref[:, hd * LANES : (hd + 1) * LANES] = acc_a[hd, pl.ds(1, 1), :]
            dog_ref[...] += acc_a[hd, pl.ds(0, 1), :]

    def col(k):
        return pl.BlockSpec((s, wide), lambda h: (0, k * (nh // hb) + h), pipeline_mode=pl.Buffered(1))

    head_in = pl.BlockSpec((s, wide), lambda h: (0, h), pipeline_mode=pl.Buffered(1))
    head = pl.BlockSpec((s, wide), lambda h: (0, h))
    vec = pl.BlockSpec((None, 1, wide), lambda h: (l, 0, h))
    ck = pltpu.VMEM((hb, CHUNK, LANES), F32)
    return _carry(
        body,
        comm,
        name="hgrn_bwd",
        grid=(nh // hb,),
        in_specs=[col(0), col(1), col(2), col(3), vec, pl.BlockSpec((None, 1, LANES), lambda h: (l, 0, 0)), head_in, head_in],
        out_specs=[head] * 4 + [pl.BlockSpec((1, wide), lambda h: (0, h)), _const((1, LANES))],
        out_shape=[jax.ShapeDtypeStruct((s, nh * LANES), F32)] * 4
        + [jax.ShapeDtypeStruct((1, nh * LANES), F32), jax.ShapeDtypeStruct((1, LANES), F32)],
        scratch_shapes=[pltpu.VMEM((hb, n_chunk, LANES, LANES), F32)] + [ck] * 8 + [pltpu.VMEM((hb, 8, LANES), F32)],
        args=(proj, proj, proj, proj, lb3, og3, o, dmixed),
    )


MEM_SCALE = HEAD64**-0.5


def _mem_heads(qraw, kvm, qg, kg, pr, m0):
    lo = pr * LANES
    uq, rq = _rms64(qraw[:, lo : lo + LANES], m0)
    uk, rk = _rms64(kvm[:, lo : lo + LANES], m0)
    v = bf(kvm[:, 2 * LANES + lo : 3 * LANES + lo])
    return uq, rq, uk, rk, v, uq * qg, bf(uk * kg)


def memattn_fwd(proj, qblk, kvm, qg3, kg3, l):
    s = proj.shape[0]
    nm = kvm.shape[0]
    tm = _tok(s)

    def body(q_ref, kv_ref, qg_ref, kg_ref, o_ref):
        m0 = _lane_mask0((1, LANES))
        qraw, kvv = q_ref[...], kv_ref[...]
        for pr in range(2):
            _, _, _, _, v, qn, kn = _mem_heads(qraw, kvv, qg_ref[...], kg_ref[...], pr, m0)
            out = jnp.zeros((tm, LANES), F32)
            for hh in range(2):
                mh = m0 if hh == 0 else jnp.logical_not(m0)
                sc = dot_nt(bf(jnp.where(mh, qn, 0.0)), kn) * MEM_SCALE
                p = jnp.exp(sc - jnp.max(sc, axis=-1, keepdims=True))
                p = p / jnp.sum(p, axis=-1, keepdims=True)
                out = jnp.where(mh, dot_nn(bf(p), v), out)
            o_ref[:, pr * LANES : (pr + 1) * LANES] = bf(out)

    gspec = pl.BlockSpec((None, 1, LANES), lambda i: (l, 0, 0))
    return pl.pallas_call(
        body,
        name="memattn_fwd",
        grid=(s // tm,),
        in_specs=[pl.BlockSpec((tm, 2 * LANES), lambda i: (i, qblk)), _const((nm, 4 * LANES)), gspec, gspec],
        out_specs=pl.BlockSpec((tm, 2 * LANES), lambda i: (i, 0)),
        out_shape=jax.ShapeDtypeStruct((s, 2 * LANES), BF16),
        compiler_params=_cparams(("arbitrary",)),
    )(proj, kvm, qg3, kg3)


def memattn_bwd(proj, qblk, kvm, qg3, kg3, l, dmixed):
    s = proj.shape[0]
    nm = kvm.shape[0]
    tm = _tok(s)

    def body(q_ref, kv_ref, qg_ref, kg_ref, dm_ref, dq_ref, dkv_ref, dqg_ref, dkg_ref):
        m0 = _lane_mask0((1, LANES))
        qraw, kvv = q_ref[...], kv_ref[...]
        qg, kg = qg_ref[...], kg_ref[...]

        @pl.when(pl.program_id(0) == 0)
        def _():
            dkv_ref[...] = jnp.zeros_like(dkv_ref)
            dqg_ref[...] = jnp.zeros_like(dqg_ref)
            dkg_ref[...] = jnp.zeros_like(dkg_ref)

        for pr in range(2):
            lo = pr * LANES
            uq, rq, uk, rk, v, qn, kn = _mem_heads(qraw, kvv, qg, kg, pr, m0)
            do = dm_ref[:, lo : lo + LANES]
            dqn = jnp.zeros((tm, LANES), F32)
            dkn = jnp.zeros((nm, LANES), F32)
            dv = jnp.zeros((nm, LANES), F32)
            for hh in range(2):
                mh = m0 if hh == 0 else jnp.logical_not(m0)
                qh = bf(jnp.where(mh, qn, 0.0))
                doh = bf(jnp.where(mh, do, 0.0))
                sc = dot_nt(qh, kn) * MEM_SCALE
                p = jnp.exp(sc - jnp.max(sc, axis=-1, keepdims=True))
                p = p / jnp.sum(p, axis=-1, keepdims=True)
                dp = dot_nt(doh, v)
                ds = bf(p * (dp - jnp.sum(p * dp, axis=-1, keepdims=True)))
                dqn = dqn + jnp.where(mh, dot_nn(ds, kn), 0.0) * MEM_SCALE
                dkn = dkn + dot_tn(ds, qh) * MEM_SCALE
                dv = dv + dot_tn(bf(p), doh)
            dqr, dqg = _rms64_bwd(dqn, uq, rq, qg, m0)
            dkr, dkg = _rms64_bwd(dkn, uk, rk, kg, m0)
            dq_ref[:, lo : lo + LANES] = dqr
            dkv_ref[:, lo : lo + LANES] += dkr
            dkv_ref[:, 2 * LANES + lo : 3 * LANES + lo] += dv
            dqg_ref[...] += dqg
            dkg_ref[...] += dkg

    gspec = pl.BlockSpec((None, 1, LANES), lambda i: (l, 0, 0))
    return pl.pallas_call(
        body,
        name="memattn_bwd",
        grid=(s // tm,),
        in_specs=[
            pl.BlockSpec((tm, 2 * LANES), lambda i: (i, qblk)),
            _const((nm, 4 * LANES)),
            gspec,
            gspec,
            pl.BlockSpec((tm, 2 * LANES), lambda i: (i, 3)),
        ],
        out_specs=[pl.BlockSpec((tm, 2 * LANES), lambda i: (i, 0)), _const((nm, 4 * LANES)), _const((1, LANES)), _const((1, LANES))],
        out_shape=[
            jax.ShapeDtypeStruct((s, 2 * LANES), F32),
            jax.ShapeDtypeStruct((nm, 4 * LANES), F32),
            jax.ShapeDtypeStruct((1, LANES), F32),
            jax.ShapeDtypeStruct((1, LANES), F32),
        ],
        compiler_params=_cparams(("arbitrary",)),
    )(proj, kvm, qg3, kg3, dmixed)


KV_MAIN = 768


def _log_sigmoid(z):
    return jnp.minimum(z, 0.0) - jnp.log(1.0 + jnp.exp(-jnp.abs(z)))


def kvprep_fwd(kvf, kg, fb):
    s = kvf.shape[0]
    tm = _tok(s)

    def body(kvf_ref, kg_ref, fb_ref, k_ref, v_ref, clf_ref, carry):
        m0 = _lane_mask0((1, LANES))

        @pl.when(pl.program_id(0) == 0)
        def _():
            carry[...] = jnp.zeros_like(carry)

        for j in range(KV_MAIN // LANES):
            u, _ = _rms64(kvf_ref[:, j * LANES : (j + 1) * LANES], m0)
            k_ref[:, j * LANES : (j + 1) * LANES] = bf(u * kg_ref[...])
        v_ref[...] = bf(kvf_ref[:, KV_MAIN : 2 * KV_MAIN])
        lf = _log_sigmoid(kvf_ref[:, 2 * KV_MAIN :] + fb_ref[...])
        clf_ref[...] = dot_nn(_tri(tm), lf, HI) + carry[...]
        carry[...] += jnp.sum(lf, axis=0, keepdims=True)

    n = kvf.shape[1]
    return pl.pallas_call(
        body,
        name="kvprep_fwd",
        grid=(s // tm,),
        in_specs=[pl.BlockSpec((tm, n), lambda i: (i, 0)), _const((1, LANES)), _const((1, LANES))],
        out_specs=[pl.BlockSpec((tm, KV_MAIN), lambda i: (i, 0))] * 2 + [pl.BlockSpec((tm, LANES), lambda i: (i, 0))],
        out_shape=[jax.ShapeDtypeStruct((s, KV_MAIN), BF16)] * 2 + [jax.ShapeDtypeStruct((s, LANES), F32)],
        scratch_shapes=[pltpu.VMEM((1, LANES), F32)],
        compiler_params=_cparams(("arbitrary",)),
    )(kvf, kg, fb)


def kvprep_bwd(kvf, kg, fb, dk, dv, dclf):
    s, n = kvf.shape
    tm = _tok(s)
    nb = s // tm

    def body(kvf_ref, kg_ref, fb_ref, dk_ref, dv_ref, dclf_ref, o_ref, dkg_ref, dfb_ref, carry):
        m0 = _lane_mask0((1, LANES))

        @pl.when(pl.program_id(0) == 0)
        def _():
            carry[...] = jnp.zeros_like(carry)
            dkg_ref[...] = jnp.zeros_like(dkg_ref)
            dfb_ref[...] = jnp.zeros_like(dfb_ref)

        kg_ = kg_ref[...]
        for j in range(KV_MAIN // LANES):
            cols = slice(j * LANES, (j + 1) * LANES)
            u, r = _rms64(kvf_ref[:, cols], m0)
            dkr, dkg = _rms64_bwd(dk_ref[:, cols], u, r, kg_, m0)
            o_ref[:, cols] = dkr
            dkg_ref[...] += dkg
        o_ref[:, KV_MAIN : 2 * KV_MAIN] = dv_ref[...]
        z = kvf_ref[:, 2 * KV_MAIN :] + fb_ref[...]
        dc = dclf_ref[...]
        dlf = dot_nn(_tri(tm, upper=True), dc, HI) + carry[...]
        carry[...] += jnp.sum(dc, axis=0, keepdims=True)
        dz = dlf * _sigmoid(-z)
        o_ref[:, 2 * KV_MAIN :] = dz
        dfb_ref[...] += jnp.sum(dz, axis=0, keepdims=True)

    rev = lambda i: (nb - 1 - i, 0)
    return pl.pallas_call(
        body,
        name="kvprep_bwd",
        grid=(nb,),
        in_specs=[pl.BlockSpec((tm, n), rev), _const((1, LANES)), _const((1, LANES)), pl.BlockSpec((tm, KV_MAIN), rev),
                  pl.BlockSpec((tm, KV_MAIN), rev), pl.BlockSpec((tm, LANES), rev)],
        out_specs=[pl.BlockSpec((tm, n), rev), _const((1, LANES)), _const((1, LANES))],
        out_shape=[jax.ShapeDtypeStruct((s, n), F32), jax.ShapeDtypeStruct((1, LANES), F32), jax.ShapeDtypeStruct((1, LANES), F32)],
        scratch_shapes=[pltpu.VMEM((1, LANES), F32)],
        compiler_params=_cparams(("arbitrary",)),
    )(kvf, kg, fb, dk, dv, dclf)


FOX_SCALE = HEAD64**-0.5


def _lane_col(block, lane_idx, h):
    return jnp.sum(jnp.where(lane_idx == h, block, 0.0), axis=-1, keepdims=True)


def _causal(tq, ext, i, transposed=False):
    if transposed:
        key = lax.broadcasted_iota(jnp.int32, (ext, tq), 0)
        qry = lax.broadcasted_iota(jnp.int32, (ext, tq), 1) + i * tq
    else:
        qry = lax.broadcasted_iota(jnp.int32, (tq, ext), 0) + i * tq
        key = lax.broadcasted_iota(jnp.int32, (tq, ext), 1)
    return key <= qry


def fox_fwd(proj, k_sh, v_sh, clf, clf_t, qg3, j_layer, comm=None):
    s = proj.shape[0]
    npair = 6
    tq = TQ if s % TQ == 0 else s
    nq = s // tq

    def body(q_ref, gate_ref, k_ref, v_ref, clf_ref, clft_ref, qg_ref, main_ref, o_ref, lse_ref):
        j = pl.program_id(0)
        lane = lax.broadcasted_iota(jnp.int32, (1, LANES), 1)
        m0 = lane < HEAD64
        u, _ = _rms64(q_ref[...], m0)
        qn = u * qg_ref[...] * FOX_SCALE
        clfv = clf_ref[...]
        for hh in range(2):
            h = 2 * j + hh
            mh = m0 if hh == 0 else jnp.logical_not(m0)
            qh = bf(jnp.where(mh, qn, 0.0))
            dcol = _lane_col(clfv, lane, h)
            drow = clft_ref[pl.ds(h, 1), :]
            for i in range(nq):
                rows = slice(i * tq, (i + 1) * tq)
                ext = (i + 1) * tq
                sc = dot_nt(qh[rows], k_ref[0:ext, :]) + dcol[rows] - drow[:, :ext]
                sc = jnp.where(_causal(tq, ext, i), sc, -jnp.inf)
                m = jnp.max(sc, axis=-1, keepdims=True)
                p = jnp.exp(sc - m)
                lsum = jnp.sum(p, axis=-1, keepdims=True)
                pv = dot_nn(bf(p), v_ref[0:ext, :]) / lsum
                lse = m + jnp.log(lsum)
                if hh == 0:
                    o_ref[rows, :] = pv
                    lse_ref[rows, :] = jnp.where(lane == 0, lse, 0.0)
                else:
                    o_ref[rows, :] = jnp.where(mh, pv, o_ref[rows, :])
                    lse_ref[rows, :] = jnp.where(lane == 1, lse, lse_ref[rows, :])
        main_ref[...] = bf(o_ref[...] * _sigmoid(gate_ref[...]))

    blk = lambda off: pl.BlockSpec((s, LANES), lambda j: (0, off + j))
    return _carry(
        body,
        comm,
        name="fox_fwd",
        grid=(npair,),
        in_specs=[blk(0), blk(npair), blk(0), blk(0), _const((s, LANES)), _const((16, s)),
                  pl.BlockSpec((None, 1, LANES), lambda j: (j_layer, 0, 0))],
        out_specs=[blk(0)] * 3,
        out_shape=[jax.ShapeDtypeStruct((s, npair * LANES), BF16)] + [jax.ShapeDtypeStruct((s, npair * LANES), F32)] * 2,
        scratch_shapes=[],
        args=(proj, proj, k_sh, v_sh, clf, clf_t, qg3),
    )


def fox_bwd(proj, k_sh, v_sh, clf, clf_t, qg3, j_layer, o, lse, lse_t, dmixed, dk_in, dv_in, dclf_in, comm=None):
    s = proj.shape[0]
    npair = 6
    tq = TQ if s % TQ == 0 else s
    nq = s // tq

    def body(q_ref, gate_ref, k_ref, v_ref, clf_ref, clft_ref, qg_ref, o_ref, lse_ref, lset_ref, dm_ref, dkin_ref, dvin_ref, dclfin_ref,
             dq_ref, dgate_ref, dk_ref, dv_ref, dclf_ref, dqg_ref, dqn_s, dcl_s):
        j = pl.program_id(0)
        lane = lax.broadcasted_iota(jnp.int32, (1, LANES), 1)
        m0 = lane < HEAD64
        qg = qg_ref[...]
        u, r = _rms64(q_ref[...], m0)
        qn = u * qg * FOX_SCALE
        ov = o_ref[...]
        gate = gate_ref[...]
        sg = _sigmoid(gate)
        dmain = dm_ref[...]
        do = dmain * sg
        dgate_ref[...] = dmain * ov * sg * (1.0 - sg)
        dk_ref[...] = dkin_ref[...]
        dv_ref[...] = dvin_ref[...]
        clfv = clf_ref[...]
        lsev = lse_ref[...]
        ones8 = jnp.ones((8, LANES), F32)

        @pl.when(j == 0)
        def _():
            dclf_ref[...] = dclfin_ref[...]
            dqg_ref[...] = jnp.zeros_like(dqg_ref)

        for hh in range(2):
            h = 2 * j + hh
            mh = m0 if hh == 0 else jnp.logical_not(m0)
            qh = bf(jnp.where(mh, qn, 0.0))
            doh = jnp.where(mh, do, 0.0)
            dohb = bf(doh)
            doo = doh * ov
            dcol = _lane_col(clfv, lane, h)
            drow = clft_ref[pl.ds(h, 1), :]
            lcol = _lane_col(lsev, lane, hh)
            lrow = lset_ref[pl.ds(h, 1), :]
            delta = jnp.sum(doo, axis=-1, keepdims=True)
            dcl_s[...] = jnp.zeros_like(dcl_s)
            for i in range(nq):
                rows = slice(i * tq, (i + 1) * tq)
                ext = (i + 1) * tq
                kk, vv = k_ref[0:ext, :], v_ref[0:ext, :]
                sc = dot_nt(qh[rows], kk) + dcol[rows] - drow[:, :ext]
                p = jnp.where(_causal(tq, ext, i), jnp.exp(sc - lcol[rows]), 0.0)
                ds = p * (dot_nt(dohb[rows], vv) - delta[rows])
                dqh = dot_nn(bf(ds), kk) * FOX_SCALE
                if hh == 0:
                    dqn_s[rows, :] = dqh
                else:
                    dqn_s[rows, :] = jnp.where(mh, dqh, dqn_s[rows, :])
                dcl_s[rows, :] += jnp.sum(ds, axis=-1, keepdims=True)
                sct = dot_nt(kk, qh[rows]) + drow[:, rows] - dcol[:ext]
                pt = jnp.where(_causal(tq, ext, i, transposed=True), jnp.exp(sct - lrow[:, rows]), 0.0)
                delta_row = dot_nt(ones8, doo[rows], HI)[0:1]
                dst = pt * (dot_nt(vv, dohb[rows]) - delta_row)
                dv_ref[0:ext, :] += dot_nn(bf(pt), dohb[rows])
                dk_ref[0:ext, :] += dot_nn(bf(dst), qh[rows])
                dcl_s[0:ext, :] -= jnp.sum(dst, axis=-1, keepdims=True)
            dclf_ref[...] += jnp.where(lane == h, dcl_s[...], 0.0)
        dqr, dqg = _rms64_bwd(dqn_s[...], u, r, qg, m0)
        dq_ref[...] = dqr
        dqg_ref[...] += dqg

    blk = lambda off: pl.BlockSpec((s, LANES), lambda j: (0, off + j))
    full = _const((s, LANES))
    return _carry(
        body,
        comm,
        name="fox_bwd",
        grid=(npair,),
        in_specs=[blk(0), blk(npair), blk(0), blk(0), full, _const((16, s)), pl.BlockSpec((None, 1, LANES), lambda j: (j_layer, 0, 0)),
                  blk(0), blk(0), _const((16, s)), blk(0), blk(0), blk(0), full],
        out_specs=[blk(0)] * 4 + [full, _const((1, LANES))],
        out_shape=[jax.ShapeDtypeStruct((s, npair * LANES), F32)] * 4
        + [jax.ShapeDtypeStruct((s, LANES), F32), jax.ShapeDtypeStruct((1, LANES), F32)],
        scratch_shapes=[pltpu.VMEM((s, LANES), F32), pltpu.VMEM((s, LANES), F32)],
        args=(proj, proj, k_sh, v_sh, clf, clf_t, qg3, o, lse, lse_t, dmixed, dk_in, dv_in, dclf_in),
    )


def loss_head(y, target):
    s, d = y.shape
    tm = _tok(s)

    def body(y_ref, t_ref, loss_ref, dy_ref):
        err = y_ref[...] - t_ref[...]
        dy_ref[...] = err * (1.0 / d)

        @pl.when(pl.program_id(0) == 0)
        def _():
            loss_ref[...] = jnp.zeros_like(loss_ref)

        part = jnp.sum(jnp.mean(err * err, axis=-1, keepdims=True), axis=0, keepdims=True)
        loss_ref[...] += 0.5 * part

    row = pl.BlockSpec((tm, d), lambda i: (i, 0))
    return pl.pallas_call(
        body,
        name="loss_head",
        grid=(s // tm,),
        in_specs=[row, row],
        out_specs=[_const((1, 1)), row],
        out_shape=[jax.ShapeDtypeStruct((1, 1), F32), jax.ShapeDtypeStruct((s, d), F32)],
        compiler_params=_cparams(("arbitrary",)),
    )(y, target)


def _row_tile(r, c, n_arrays):
    budget = VMEM_LIMIT_BYTES // 2
    padded_c = -(-c // LANES) * LANES
    for step in (16, 8):
        fits = [t for t in range(step, r + 1, step) if r % t == 0 and 2 * n_arrays * t * padded_c * 4 <= budget]
        if fits:
            return fits[-1]
    return r


def _as2d(a):
    return a.reshape(-1, a.shape[-1]) if a.ndim >= 2 else a.reshape(1, -1)


def adamw(ws, gss, ms, vs):
    nw = len(ws)
    shape = ws[0].shape
    w2, m2, v2 = ([_as2d(t) for t in lst] for lst in (ws, ms, vs))
    rows, c = w2[0].shape
    gss = [[g.reshape(-1, c) for g in gs] for gs in gss]
    sizes = [g.shape[0] for g in gss[0]]
    assert sum(sizes) == rows
    tr = _row_tile(min(sizes), c, 8 * nw)
    assert all(r % tr == 0 for r in sizes)
    c1 = 1.0 - ADAM_B1**ADAM_STEP
    c2 = 1.0 - ADAM_B2**ADAM_STEP
    outs = []
    first = 0
    for k, r in enumerate(sizes):
        n_prev = len(outs)

        def body(*refs, n_prev=n_prev):
            out_refs = refs[4 * nw + n_prev :]
            for i in range(nw):
                w_ref, g_ref, m_ref, v_ref = refs[4 * i : 4 * i + 4]
                go_ref, d_ref, nm_ref, nv_ref = out_refs[4 * i : 4 * i + 4]
                gv = g_ref[...]
                nm = ADAM_B1 * m_ref[...] + (1.0 - ADAM_B1) * gv
                nv = ADAM_B2 * v_ref[...] + (1.0 - ADAM_B2) * (gv * gv)
                go_ref[...] = gv
                nm_ref[...] = nm
                nv_ref[...] = nv
                d_ref[...] = -ADAM_LR * ((nm / c1) / (jnp.sqrt(nv / c2) + ADAM_EPS) + ADAM_WD * w_ref[...])

        spec = pl.BlockSpec((tr, c), lambda i, b0=first // tr: (b0 + i, 0))
        args = [t for i in range(nw) for t in (w2[i], gss[i][k], m2[i], v2[i])]
        outs = pl.pallas_call(
            body,
            name="adamw",
            grid=(r // tr,),
            in_specs=[spec, pl.BlockSpec((tr, c), lambda i: (i, 0)), spec, spec] * nw + [ANY] * n_prev,
            out_specs=[spec] * (4 * nw),
            out_shape=[jax.ShapeDtypeStruct((rows, c), F32)] * (4 * nw),
            input_output_aliases={4 * nw + j: j for j in range(n_prev)},
            compiler_params=_cparams(("arbitrary",)),
        )(*args, *outs)
        first += r
    return [tuple(t.reshape(shape) for t in outs[4 * i : 4 * i + 4]) for i in range(nw)]


def _by_shape(arrays, fn):
    groups = {}
    for i, t in enumerate(arrays):
        groups.setdefault((t.shape, str(t.dtype)), []).append(i)
    out = [None] * len(arrays)
    for idx in groups.values():
        for i, res in zip(idx, fn(idx)):
            out[i] = res
    return out


def pair_sum(gs, recvs, c_arr):
    n = len(gs)
    _, k, r, c = gs[0].shape
    tr = _row_tile(r, c, 3 * n)

    def body(c_ref, *refs):
        for i in range(n):
            refs[2 * n + i][...] = bf(refs[i][...] + refs[n + i][...].astype(F32))

    slab = pl.BlockSpec((None, tr, c), lambda kk, i, cr: (kk, i, 0))
    return pl.pallas_call(
        body,
        name="pair_sum",
        grid_spec=pltpu.PrefetchScalarGridSpec(
            num_scalar_prefetch=1,
            grid=(k, r // tr),
            in_specs=[pl.BlockSpec((None, None, tr, c), lambda kk, i, cr: (cr[0], kk, i, 0))] * n + [slab] * n,
            out_specs=[slab] * n,
        ),
        out_shape=[jax.ShapeDtypeStruct((k, r, c), BF16)] * n,
        compiler_params=_cparams(("arbitrary", "arbitrary")),
    )(c_arr, *gs, *recvs)


def chip_sum(ps, qs, sel):
    n = len(ps)
    _, r, c = ps[0].shape
    nq = qs[0].shape[0]
    tr = _row_tile(r, c, 4 * n)

    def body(sel_ref, *refs):
        for a in range(n):
            acc = refs[a][...].astype(F32)
            for i in range(nq):
                acc = acc + refs[n + a][i].astype(F32)
            refs[2 * n + a][...] = acc

    return pl.pallas_call(
        body,
        name="chip_sum",
        grid_spec=pltpu.PrefetchScalarGridSpec(
            num_scalar_prefetch=1,
            grid=(r // tr,),
            in_specs=[pl.BlockSpec((None, tr, c), lambda i, sr: (sr[0], i, 0))] * n + [pl.BlockSpec((nq, tr, c), lambda i, sr: (0, i, 0))] * n,
            out_specs=[pl.BlockSpec((None, tr, c), lambda i, sr: (sr[1], i, 0))] * n,
        ),
        out_shape=[jax.ShapeDtypeStruct((2, r, c), F32)] * n,
        compiler_params=_cparams(("arbitrary",)),
    )(sel, *ps, *qs)


def cast_into_slot(w4s, owns, sel, dtype):
    n = len(w4s)
    _, _, r, c = w4s[0].shape
    tr = _row_tile(r, c, 2 * n)

    def body(sel_ref, *refs):
        for i in range(n):
            refs[n + i][...] = refs[i][...].astype(dtype)

    return pl.pallas_call(
        body,
        name="cast_into_slot",
        grid_spec=pltpu.PrefetchScalarGridSpec(
            num_scalar_prefetch=1,
            grid=(2, r // tr),
            in_specs=[pl.BlockSpec((None, None, tr, c), lambda hf, i, sr, g=g: (g, hf, i, 0)) for g in owns],
            out_specs=[pl.BlockSpec((None, None, tr, c), lambda hf, i, sr: (sr[0], hf, i, 0))] * n,
        ),
        out_shape=[jax.ShapeDtypeStruct((N_CHIPS, 2, r, c), dtype)] * n,
        compiler_params=_cparams(("arbitrary", "arbitrary")),
    )(sel, *w4s)


def _place():
    x, y, c = lax.axis_index("x"), lax.axis_index("y"), lax.axis_index("c")
    chips = [(1 - x, y), (x, 1 - y), (1 - x, 1 - y)]
    return x, y, c, 2 * x + y, chips, [2 * cx + cy for cx, cy in chips]


def _rcopy(src, dst, send, recv, dev):
    return pltpu.make_async_remote_copy(src_ref=src, dst_ref=dst, send_sem=send, recv_sem=recv, device_id=dev, device_id_type=MESH)


class Gather:
    def __init__(self, bufs):
        n = len(bufs)
        self.n = n
        self.args = list(bufs)
        self.out_shape = [jax.ShapeDtypeStruct(t.shape, t.dtype) for t in bufs]
        self.aliases = {a: a for a in range(n)}
        self.scratch = [pltpu.SemaphoreType.DMA((n, 6)), pltpu.SemaphoreType.DMA((n, 6))]

    def _sends(self, outs, send, recv):
        x, y, c, me, chips, _ = _place()
        cps = []
        for a in range(self.n):
            mine = outs[a].at[me, c]
            cps += [_rcopy(mine, mine, send.at[a, j], recv.at[a, j], (*chips[j], c)) for j in range(3)]
        return cps

    def start(self, ins, outs, scr):
        for cp in self._sends(outs, *scr):
            cp.start()

    def finish(self, ins, outs, scr):
        send, recv = scr
        x, y, c, me, chips, cidx = _place()
        sib = (x, y, 1 - c)
        passed = []
        for a in range(self.n):
            for j in range(3):
                landed = outs[a].at[cidx[j], c]
                _rcopy(landed, landed, send.at[a, j], recv.at[a, j], (*chips[j], c)).wait_recv()
                fwd = _rcopy(landed, landed, send.at[a, 3 + j], recv.at[a, 3 + j], sib)
                fwd.start()
                passed.append(fwd)
        for a in range(self.n):
            for j in range(3):
                theirs = outs[a].at[cidx[j], 1 - c]
                _rcopy(theirs, theirs, send.at[a, 3 + j], recv.at[a, 3 + j], sib).wait_recv()
        for cp in self._sends(outs, send, recv) + passed:
            cp.wait_send()


class PairExchange:
    def __init__(self, gs):
        n = len(gs)
        self.n = n
        self.args = list(gs)
        self.out_shape = [jax.ShapeDtypeStruct(t.shape[1:], t.dtype) for t in gs]
        self.aliases = {}
        self.scratch = [pltpu.SemaphoreType.DMA((n,)), pltpu.SemaphoreType.DMA((n,))]

    def _copies(self, ins, outs, send, recv):
        x, y, c = lax.axis_index("x"), lax.axis_index("y"), lax.axis_index("c")
        return [_rcopy(ins[a].at[1 - c], outs[a], send.at[a], recv.at[a], (x, y, 1 - c)) for a in range(self.n)]

    def start(self, ins, outs, scr):
        for cp in self._copies(ins, outs, *scr):
            cp.start()

    def finish(self, ins, outs, scr):
        for cp in self._copies(ins, outs, *scr):
            cp.wait()


class ChipExchange:
    def __init__(self, ps):
        n = len(ps)
        self.n = n
        self.args = list(ps)
        self.out_shape = [jax.ShapeDtypeStruct((3,) + t.shape[1:], t.dtype) for t in ps]
        self.aliases = {}
        self.scratch = [pltpu.SemaphoreType.DMA((n, 3)), pltpu.SemaphoreType.DMA((n, 3))]

    def _sends(self, ins, outs, send, recv):
        x, y, c, me, chips, cidx = _place()
        return [
            _rcopy(ins[a].at[cidx[j]], outs[a].at[j], send.at[a, j], recv.at[a, j], (*chips[j], c))
            for a in range(self.n)
            for j in range(3)
        ]

    def start(self, ins, outs, scr):
        for cp in self._sends(ins, outs, *scr):
            cp.start()

    def finish(self, ins, outs, scr):
        send, recv = scr
        x, y, c, me, chips, _ = _place()
        for a in range(self.n):
            for j in range(3):
                landed = outs[a].at[j]
                _rcopy(landed, landed, send.at[a, j], recv.at[a, j], (*chips[j], c)).wait_recv()
        for cp in self._sends(ins, outs, send, recv):
            cp.wait_send()


class PairShare:
    def __init__(self, bufs):
        n = len(bufs)
        self.n = n
        self.args = list(bufs)
        self.out_shape = [jax.ShapeDtypeStruct(t.shape, t.dtype) for t in bufs]
        self.aliases = {a: a for a in range(n)}
        self.scratch = [pltpu.SemaphoreType.DMA((n,)), pltpu.SemaphoreType.DMA((n,))]

    def _sends(self, outs, send, recv):
        x, y, c = lax.axis_index("x"), lax.axis_index("y"), lax.axis_index("c")
        return [_rcopy(outs[a].at[c], outs[a].at[c], send.at[a], recv.at[a], (x, y, 1 - c)) for a in range(self.n)]

    def start(self, ins, outs, scr):
        for cp in self._sends(outs, *scr):
            cp.start()

    def finish(self, ins, outs, scr):
        send, recv = scr
        x, y, c = lax.axis_index("x"), lax.axis_index("y"), lax.axis_index("c")
        for a in range(self.n):
            theirs = outs[a].at[1 - c]
            _rcopy(theirs, theirs, send.at[a], recv.at[a], (x, y, 1 - c)).wait_recv()
        for cp in self._sends(outs, send, recv):
            cp.wait_send()


class Multi:
    def __init__(self, comms):
        self.comms = comms
        self.args, self.out_shape, self.scratch, self.aliases = [], [], [], {}
        self.spans = []
        for cm in comms:
            a0, o0, s0 = len(self.args), len(self.out_shape), len(self.scratch)
            self.aliases.update({a0 + i: o0 + o for i, o in cm.aliases.items()})
            self.args += cm.args
            self.out_shape += cm.out_shape
            self.scratch += cm.scratch
            self.spans.append((slice(a0, len(self.args)), slice(o0, len(self.out_shape)), slice(s0, len(self.scratch))))

    def start(self, ins, outs, scr):
        for cm, (sa, so, ss) in zip(self.comms, self.spans):
            cm.start(ins[sa], outs[so], scr[ss])

    def finish(self, ins, outs, scr):
        for cm, (sa, so, ss) in zip(self.comms, self.spans):
            cm.finish(ins[sa], outs[so], scr[ss])

    def split(self, res):
        return [list(res[so]) for _, so, _ in self.spans]


def run_comm(comm, name):
    na, no = len(comm.args), len(comm.out_shape)

    def body(*refs):
        ins, outs, scr = refs[:na], refs[na : na + no], refs[na + no :]
        comm.start(ins, outs, scr)
        comm.finish(ins, outs, scr)

    return pl.pallas_call(
        body,
        name=name,
        in_specs=[ANY] * na,
        out_specs=[ANY] * no,
        out_shape=comm.out_shape,
        input_output_aliases=comm.aliases,
        scratch_shapes=comm.scratch,
    )(*comm.args)


def _carry(body, comm, *, name, grid, in_specs, out_specs, out_shape, scratch_shapes, args):
    params = _cparams(("arbitrary",))
    if comm is None:
        res = pl.pallas_call(body, name=name, grid=grid, in_specs=in_specs, out_specs=out_specs, out_shape=out_shape,
                             scratch_shapes=scratch_shapes, compiler_params=params)(*args)
        return res, None
    ni, no, ns = len(in_specs), len(out_specs), len(scratch_shapes)
    ci, co = len(comm.args), len(comm.out_shape)

    def wrapped(*refs):
        ins, c_ins = refs[:ni], refs[ni : ni + ci]
        p = ni + ci
        outs, c_outs = refs[p : p + no], refs[p + no : p + no + co]
        p += no + co
        scr, c_scr = refs[p : p + ns], refs[p + ns :]

        @pl.when(pl.program_id(0) == 0)
        def _():
            comm.start(c_ins, c_outs, c_scr)

        body(*ins, *outs, *scr)

        @pl.when(pl.program_id(0) == grid[0] - 1)
        def _():
            comm.finish(c_ins, c_outs, c_scr)

    res = pl.pallas_call(
        wrapped,
        name=name + "_carry",
        grid=grid,
        in_specs=list(in_specs) + [ANY] * ci,
        out_specs=list(out_specs) + [ANY] * co,
        out_shape=list(out_shape) + list(comm.out_shape),
        input_output_aliases={ni + i: no + o for i, o in comm.aliases.items()},
        scratch_shapes=list(scratch_shapes) + list(comm.scratch),
        compiler_params=params,
    )(*args, *comm.args)
    return res[:no], res[no:]


def small_allreduce(buf):
    r = buf.shape[0]

    def body(b_ref, o_ref, slots, send, recv):
        x, y, c = lax.axis_index("x"), lax.axis_index("y"), lax.axis_index("c")
        me = 4 * x + 2 * y + c
        slots[me] = b_ref[...]
        cps = []
        peers = []
        for mask in range(1, N_DEV):
            fx, fy, fc = (mask >> 2) & 1, (mask >> 1) & 1, mask & 1
            px, py, pc = (1 - x if fx else x), (1 - y if fy else y), (1 - c if fc else c)
            peers.append(4 * px + 2 * py + pc)
            cps.append(_rcopy(b_ref, slots.at[me], send.at[mask - 1], recv.at[mask - 1], (px, py, pc)))
        for cp in cps:
            cp.start()
        for k, pid in enumerate(peers):
            landed = slots.at[pid]
            _rcopy(landed, landed, send.at[k], recv.at[k], (x, y, c)).wait_recv()
        for cp in cps:
            cp.wait_send()
        acc = slots[0]
        for i in range(1, N_DEV):
            acc = acc + slots[i]
        o_ref[...] = acc

    vm = pl.BlockSpec(memory_space=pltpu.VMEM)
    return pl.pallas_call(
        body,
        name="small_allreduce",
        in_specs=[vm],
        out_specs=vm,
        out_shape=jax.ShapeDtypeStruct(buf.shape, F32),
        scratch_shapes=[pltpu.VMEM((N_DEV, r, LANES), F32), pltpu.SemaphoreType.DMA((N_DEV - 1,)), pltpu.SemaphoreType.DMA((N_DEV - 1,))],
    )(buf)


WEIGHT_NAMES = ["ffn1_norm", "ffn1_w_gate", "ffn1_w_up", "ffn1_w_down", "mix_norm", "mem_norm", "w_mem_kv", "mem_q_gain",
                "mem_k_gain", "w_in_a", "hgrn_lb_logits", "hgrn_o_gain", "w_in_b", "fox_q_gain", "kv_norm", "w_kv", "fox_f_bias",
                "fox_k_gain", "w_out", "ffn2_norm", "ffn2_w_gate", "ffn2_w_up", "ffn2_w_down"]
SHARDED = ["ffn1_w_gate", "ffn1_w_up", "ffn1_w_down", "w_mem_kv", "w_in_a", "w_in_b", "w_kv", "w_out", "ffn2_w_gate", "ffn2_w_up", "ffn2_w_down"]
SMALL = [n for n in WEIGHT_NAMES if n not in SHARDED]
FFN1 = ["ffn1_w_gate", "ffn1_w_up", "ffn1_w_down"]
FFN2 = ["ffn2_w_gate", "ffn2_w_up", "ffn2_w_down"]
PER_LAYER = FFN1 + FFN2 + ["w_mem_kv", "w_out"]
TRANSPOSED = ["ffn1_w_gate", "ffn1_w_up", "ffn2_w_gate", "ffn2_w_up", "w_in_a", "w_in_b"]
N_LAYERS, N_A = 4, 2
KV_PAD = 13 * LANES


def _halves(t):
    return t.reshape((2, t.shape[0] // 2) + t.shape[1:])


def _cols_from_chips(g):
    return jnp.moveaxis(g, 0, 2).reshape(g.shape[1], g.shape[2], N_CHIPS * g.shape[3])


def _rows_from_chips(g):
    return jnp.moveaxis(g, 0, 1).reshape(g.shape[1], N_CHIPS * g.shape[2], g.shape[3])


def _pair_tile(g):
    return jnp.tile(g, (1, 2)).reshape(g.shape[0], 1, LANES)


def _pair_fold(g):
    return g[:, :HEAD64] + g[:, HEAD64:]


def kernel(x, mem, ffn1_norm, ffn1_w_gate, ffn1_w_up, ffn1_w_down, mix_norm, mem_norm, w_mem_kv, mem_q_gain, mem_k_gain, w_in_a, hgrn_lb_logits, hgrn_o_gain, w_in_b, fox_q_gain, kv_norm, w_kv, fox_f_bias, fox_k_gain, w_out, ffn2_norm, ffn2_w_gate, ffn2_w_up, ffn2_w_down, loss_target, m_ffn1_norm, m_ffn1_w_gate, m_ffn1_w_up, m_ffn1_w_down, m_mix_norm, m_mem_norm, m_w_mem_kv, m_mem_q_gain, m_mem_k_gain, m_w_in_a, m_hgrn_lb_logits, m_hgrn_o_gain, m_w_in_b, m_fox_q_gain, m_kv_norm, m_w_kv, m_fox_f_bias, m_fox_k_gain, m_w_out, m_ffn2_norm, m_ffn2_w_gate, m_ffn2_w_up, m_ffn2_w_down, v_ffn1_norm, v_ffn1_w_gate, v_ffn1_w_up, v_ffn1_w_down, v_mix_norm, v_mem_norm, v_w_mem_kv, v_mem_q_gain, v_mem_k_gain, v_w_in_a, v_hgrn_lb_logits, v_hgrn_o_gain, v_w_in_b, v_fox_q_gain, v_kv_norm, v_w_kv, v_fox_f_bias, v_fox_k_gain, v_w_out, v_ffn2_norm, v_ffn2_w_gate, v_ffn2_w_up, v_ffn2_w_down):
    given = dict(locals())
    def oriented(n, t):
        return jnp.swapaxes(t, 1, 2) if n in TRANSPOSED else t

    w = {n: oriented(n, given[n]) for n in WEIGHT_NAMES}
    xs, mems, tgt = x[0], mem[0], loss_target[0]
    s, d = xs.shape
    my_chip = 2 * lax.axis_index("x") + lax.axis_index("y")
    sel = jnp.stack([my_chip, lax.axis_index("c")]).astype(jnp.int32)
    c_arr = sel[1:]

    def w_in_name(l):
        return "w_in_a" if l < N_A else "w_in_b"

    def halves_of(n):
        rows, cols = w[n].shape[-2:]
        return w[n].reshape(-1, 2, rows // 2, cols)

    def own_of(n, l):
        return 0 if w[n].ndim == 2 else (l - N_A if n == "w_in_b" else l)

    def view(buf, n):
        rows, cols = w[n].shape[-2:]
        return buf.reshape(N_CHIPS, rows, cols) if w[n].ndim == 2 else buf.reshape(N_CHIPS, 1, rows, cols)

    def mixer(l):
        return [(w_in_name(l), l), ("w_mem_kv", l), ("w_out", l)]

    first = [(n, 0) for n in PER_LAYER] + [("w_in_a", 0), ("w_kv", 0)]
    carried = {
        (0, "ffn1"): mixer(1), (0, "mix"): [(n, 1) for n in FFN1 + FFN2[:2]], (0, "ffn2"): [(FFN2[2], 1)],
        (1, "ffn1"): mixer(2), (1, "mix"): [(n, 2) for n in FFN1 + FFN2[:2]], (1, "ffn2"): [(FFN2[2], 2)],
        (2, "ffn1"): [(FFN1[0], 3)], (2, "mix"): [(FFN1[1], 3), (FFN1[2], 3)], (2, "ffn2"): [(FFN2[0], 3)] + mixer(3),
        (3, "ffn1"): [(FFN2[1], 3)], (3, "mix"): [(FFN2[2], 3)],
    }
    bufs = {}
    every = first + [it for items in carried.values() for it in items]
    for layer in range(N_LAYERS):
        its = [it for it in every if it[1] == layer]
        srcs = [halves_of(n) for n, _ in its]
        cast = lambda idx: cast_into_slot([srcs[i] for i in idx], [own_of(*its[i]) for i in idx], sel, BF16)
        bufs.update(zip(its, _by_shape(srcs, cast)))
    lb_buf = cast_into_slot([hgrn_lb_logits.reshape(1, 2, 1, -1)], [0], sel, F32)[0]
    got0 = run_comm(Gather([bufs[it] for it in first] + [lb_buf]), "gather_layer0")
    got = {it: view(b, it[0]) for it, b in zip(first, got0[:-1])}
    w_kv_full = _cols_from_chips(got[("w_kv", 0)][:, None])
    w_kv_full = jnp.pad(w_kv_full, ((0, 0), (0, 0), (0, KV_PAD - w_kv_full.shape[-1])))
    logits3 = jnp.moveaxis(got0[-1].reshape(N_CHIPS, 2, -1), 0, 1).reshape(2, 1, -1)
    lb3 = lb_fwd(logits3)
    w_in, w_mkv, w_o = {}, {}, {}

    def gather_behind(key):
        items = carried.get(key)
        return None if items is None else Gather([bufs[it] for it in items])

    def landed(key, res):
        if res is not None:
            got.update({it: view(b, it[0]) for it, b in zip(carried[key], res)})

    norm3 = {n: w[n].reshape(N_LAYERS, 1, d) for n in ("ffn1_norm", "mix_norm", "mem_norm", "ffn2_norm")}
    kvn3 = kv_norm.reshape(1, 1, d)
    mqg3, mkg3 = _pair_tile(mem_q_gain), _pair_tile(mem_k_gain)
    og3 = hgrn_o_gain.reshape(N_A, 1, LANES)
    fqg3 = _pair_tile(fox_q_gain)
    fkg = jnp.tile(fox_k_gain, 2).reshape(1, LANES)
    fb = jnp.pad(fox_f_bias, (0, LANES - fox_f_bias.shape[0])).reshape(1, LANES)

    sv = [dict() for _ in range(N_LAYERS)]
    h = xs
    kv = None
    for l in range(N_LAYERS):
        t = sv[l]
        t["x0"] = h
        (h, t["a1"], t["b1"]), res = ffn_fwd(h, norm3["ffn1_norm"], l, *[got[(n, l)] for n in FFN1], 0, comm=gather_behind((l, "ffn1")))
        landed((l, "ffn1"), res)
        t["x1"] = h
        w_in[l] = _rows_from_chips(got[(w_in_name(l), l)])
        t["proj"] = proj_fwd(h, norm3["mix_norm"], l, w_in[l], 0, wt=True)
        if l < N_A:
            (main, t["o"]), res = hgrn_fwd(t["proj"], lb3, og3, l, comm=gather_behind((l, "mix")))
            t["qblk"] = 12
        else:
            (main, t["o"], t["lse"]), res = fox_fwd(t["proj"], kv["k"], kv["v"], kv["clf"], kv["clf_t"], fqg3, l - N_A, comm=gather_behind((l, "mix")))
            t["qblk"] = 6
        landed((l, "mix"), res)
        w_mkv[l], w_o[l] = _rows_from_chips(got[("w_mem_kv", l)]), _rows_from_chips(got[("w_out", l)])
        t["kvm"] = proj_fwd(mems, norm3["mem_norm"], l, w_mkv[l], 0)
        memo = memattn_fwd(t["proj"], t["qblk"], t["kvm"], mqg3, mkg3, l)
        t["mixed"] = jnp.concatenate([main, memo], axis=-1)
        h = mm_res(h, t["mixed"], w_o[l], 0)
        t["x2"] = h
        (h, t["a2"], t["b2"]), res = ffn_fwd(h, norm3["ffn2_norm"], l, *[got[(n, l)] for n in FFN2], 0, comm=gather_behind((l, "ffn2")))
        landed((l, "ffn2"), res)
        if l == N_A - 1:
            kv = {"x": h, "kvf": proj_fwd(h, kvn3, 0, w_kv_full, 0)}
            kv["k"], kv["v"], kv["clf"] = kvprep_fwd(kv["kvf"], fkg, fb)
            kv["clf_t"] = kv["clf"][:, :16].T

    loss_local, dx = loss_head(h, tgt)

    nc = N_CHIPS
    fc = ffn1_w_down.shape[1]
    gsplit = [dict() for _ in range(N_LAYERS)]

    def group_layout(l):
        lay = {n: b[0].reshape(2, nc, fc // 2, d) for n, b in gsplit[l].items()}
        lay16 = {n: b[1].reshape(2, nc, fc // 2, d) for n, b in gsplit[l].items()}
        for n, (g32, g16) in (("w_mem_kv", dw_mkv[l]), ("w_out", dw_o[l]), (w_in_name(l), dw_in[l])):
            lay[n], lay16[n] = g32, g16
        names = PER_LAYER + [w_in_name(l)]
        if l == N_A - 1:
            kv_cols = w_kv.shape[-1] * nc
            lay["w_kv"] = jnp.transpose(dw_kv[:, :kv_cols].reshape(2, d // 2, nc, kv_cols // nc), (0, 2, 1, 3))
            names = names + ["w_kv"]
        return names, [lay[n] for n in names], [lay16[n] if n in lay16 else bf(lay[n]) for n in names]

    def pair_sums(gl, recv):
        return _by_shape(gl, lambda idx: pair_sum([gl[i] for i in idx], [recv[i] for i in idx], c_arr))

    def chip_sums(ps, qs):
        return _by_shape(ps, lambda idx: chip_sum([ps[i] for i in idx], [qs[i] for i in idx], sel))

    n_ffn = len(FFN1) + len(FFN2)
    riding = {l: l + 1 for l in range(N_LAYERS - 1)}
    reduced = {}
    unshared = None
    dw_in, dw_o, dw_mkv = [None] * N_LAYERS, [None] * N_LAYERS, [None] * N_LAYERS
    sg = {n: [None] * N_LAYERS for n in ("ffn1_norm", "mix_norm", "mem_norm", "ffn2_norm", "mem_q_gain", "mem_k_gain")}
    sg["hgrn_o_gain"], sg["fox_q_gain"], dlb = [None] * N_A, [None] * (N_LAYERS - N_A), [None] * N_A
    dk_sh = jnp.zeros((s, KV_MAIN), F32)
    dv_sh = jnp.zeros((s, KV_MAIN), F32)
    dclf = jnp.zeros((s, LANES), F32)
    zero_mem = jnp.zeros(mems.shape, F32)
    dw_kv = None
    for l in reversed(range(N_LAYERS)):
        t = sv[l]
        if l == N_A - 1:
            dkvf, dfkg, dfb = kvprep_bwd(kv["kvf"], fkg, fb, dk_sh, dv_sh, dclf)
            dx, sg["kv_norm"], xn_kv, dpb = proj_bwd(kv["x"], kvn3, 0, [dkvf], w_kv_full, 0, dx)
            dw_kv = wgrad(xn_kv, dpb)
        ride = riding.get(l)
        comms = []
        if unshared is not None:
            comms.append(PairShare(unshared[2]))
        if ride is not None:
            names_r, gl_r, gl16_r = group_layout(ride)
            comms.append(PairExchange(gl16_r))
        comm = Multi(comms) if comms else None
        (dx, da, db, hm, xn, dyb, sg["ffn2_norm"][l]), res = ffn_bwd(t["x2"], norm3["ffn2_norm"], l, dx, t["a2"], t["b2"], *[got[(n, l)] for n in FFN2], 0, comm=comm)
        if comm is not None:
            res = comm.split(res)
            if unshared is not None:
                reduced[unshared[0]] = dict(zip(unshared[1], res.pop(0)))
                unshared = None
            if ride is not None:
                partial_r = pair_sums(gl_r, res.pop(0))

        def ffn_wgrads(which, da, db, hm, xn, dyb):
            for n, (a_, b_) in zip(which, ((da, xn), (db, xn), (hm, dyb))):
                gsplit[l][n] = wgrad(a_, b_, split=True)

        ffn_wgrads(FFN2, da, db, hm, xn, dyb)
        dmixed, dxb = mm_nt(dx, w_o[l], 0)
        dw_o[l] = wgrad(t["mixed"], dxb, chip_rows=w_out.shape[1])
        dqm, dkvm, dmq, dmk = memattn_bwd(t["proj"], t["qblk"], t["kvm"], mqg3, mkg3, l, dmixed)
        sg["mem_q_gain"][l], sg["mem_k_gain"][l] = _pair_fold(dmq), _pair_fold(dmk)
        _, sg["mem_norm"][l], memn, dkvmb = proj_bwd(mems, norm3["mem_norm"], l, [dkvm], w_mkv[l], 0, zero_mem)
        dw_mkv[l] = wgrad(memn, dkvmb, chip_rows=w_mem_kv.shape[1])
        comm = ChipExchange(partial_r[:n_ffn]) if ride is not None else None
        if l < N_A:
            (dzq, dzf, dvi, dzg, dlb[l], sg["hgrn_o_gain"][l]), res = hgrn_bwd(t["proj"], lb3, og3, l, t["o"], dmixed, comm=comm)
            parts, tmw = [dzq, dzf, dvi, dzg, dqm], 13 * LANES
        else:
            lse_t = t["lse"].reshape(s, 6, LANES)[:, :, :2].reshape(s, 12).T
            lse_t = jnp.pad(lse_t, ((0, 4), (0, 0)))
            (dq, dgate, dk_sh, dv_sh, dclf, dfq), res = fox_bwd(t["proj"], kv["k"], kv["v"], kv["clf"], kv["clf_t"], fqg3, l - N_A, t["o"], t["lse"], lse_t, dmixed, dk_sh, dv_sh, dclf, comm=comm)
            sg["fox_q_gain"][l - N_A] = _pair_fold(dfq)
            parts, tmw = [dq, dgate, dqm], 7 * LANES
        if ride is not None:
            landed_r = list(res)
        dx, sg["mix_norm"][l], hn, dpb = proj_bwd(t["x1"], norm3["mix_norm"], l, parts, w_in[l], 0, dx, wt=True)
        dw_in[l] = wgrad(dpb, hn, tn=d // 2, tm=tmw, chip_rows=tmw // 2)
        comm = ChipExchange(partial_r[n_ffn:]) if ride is not None else None
        (dx, da, db, hm, xn, dyb, sg["ffn1_norm"][l]), res = ffn_bwd(t["x0"], norm3["ffn1_norm"], l, dx, t["a1"], t["b1"], *[got[(n, l)] for n in FFN1], 0, comm=comm)
        if ride is not None:
            unshared = (ride, names_r, chip_sums(partial_r, landed_r + list(res)))
        ffn_wgrads(FFN1, da, db, hm, xn, dyb)

    names0, gl0, gl16_0 = group_layout(0)
    recv0 = run_comm(PairExchange(gl16_0), "pair_exchange")
    partial0 = pair_sums(gl0, recv0)
    landed0 = run_comm(ChipExchange(partial0), "chip_exchange")
    mine0 = chip_sums(partial0, list(landed0))
    both = run_comm(PairShare(unshared[2] + mine0), "pair_share")
    reduced[unshared[0]] = dict(zip(unshared[1], both[: len(unshared[1])]))
    reduced[0] = dict(zip(names0, both[len(unshared[1]) :]))
    gparts = {n: [reduced[l][n] for l in range(N_LAYERS)] for n in PER_LAYER}
    gparts["w_in_a"] = [reduced[l]["w_in_a"] for l in range(N_A)]
    gparts["w_in_b"] = [reduced[l]["w_in_b"] for l in range(N_A, N_LAYERS)]
    gparts["w_kv"] = [reduced[N_A - 1]["w_kv"]]

    dlogits = lb_bwd(logits3, dlb[1]).reshape(2, -1)
    small = {
        "ffn1_norm": jnp.concatenate(sg["ffn1_norm"]), "mix_norm": jnp.concatenate(sg["mix_norm"]),
        "mem_norm": jnp.concatenate(sg["mem_norm"]), "ffn2_norm": jnp.concatenate(sg["ffn2_norm"]),
        "mem_q_gain": jnp.concatenate(sg["mem_q_gain"]), "mem_k_gain": jnp.concatenate(sg["mem_k_gain"]),
        "hgrn_o_gain": jnp.concatenate(sg["hgrn_o_gain"]), "fox_q_gain": jnp.concatenate(sg["fox_q_gain"]),
        "kv_norm": sg["kv_norm"], "fox_f_bias": dfb[:, : fox_f_bias.shape[0]], "fox_k_gain": _pair_fold(dfkg),
        "hgrn_lb_logits": dlogits,
    }
    flat = [small[n].reshape(-1) for n in SMALL] + [loss_local.reshape(-1)]
    sizes = [f.shape[0] for f in flat]
    total = sum(sizes)
    padded = -(-total // (8 * LANES)) * (8 * LANES)
    packed = jnp.pad(jnp.concatenate(flat), (0, padded - total)).reshape(-1, LANES)
    summed = small_allreduce(packed).reshape(-1)
    off = 0
    for n, sz in zip(SMALL, sizes[:-1]):
        gparts[n] = [summed[off : off + sz].reshape(dlogits.shape if n == "hgrn_lb_logits" else w[n].shape)]
        off += sz
    loss = summed[off]
    lbw = hgrn_lb_logits.shape[1]
    gparts["hgrn_lb_logits"] = [lax.dynamic_slice_in_dim(gparts["hgrn_lb_logits"][0], my_chip * lbw, lbw, axis=1)]

    def update(idx):
        ns = [WEIGHT_NAMES[i] for i in idx]
        return adamw([w[n] for n in ns], [gparts[n] for n in ns], [oriented(n, given["m_" + n]) for n in ns],
                     [oriented(n, given["v_" + n]) for n in ns])

    grads, delta, new_m, new_v = {}, {}, {}, {}
    for n, res in zip(WEIGHT_NAMES, _by_shape([w[n] for n in WEIGHT_NAMES], update)):
        grads[n], delta[n], new_m[n], new_v[n] = (oriented(n, t) for t in res)
    return (loss, dx[None], *[grads[n] for n in WEIGHT_NAMES], *[delta[n] for n in WEIGHT_NAMES],
            *[new_m[n] for n in WEIGHT_NAMES], *[new_v[n] for n in WEIGHT_NAMES])
```

```python
import jax
import jax.numpy as jnp
from jax import lax
from jax.experimental import pallas as pl
from jax.experimental.pallas import tpu as pltpu

F32, BF16 = jnp.float32, jnp.bfloat16
HI = lax.Precision.HIGHEST
EPS = 1e-6
MESH = pl.DeviceIdType.MESH
ANY = pl.BlockSpec(memory_space=pl.ANY)

VMEM_LIMIT_BYTES = 56 << 20
N_CHIPS = 4
N_DEV = 8
LANES = 128
HEAD64 = 64
CHUNK = 64
SUB = 32
HGRN_HEADS_PER_STEP = 2
TQ = 256
TOK = 256

ADAM_LR, ADAM_B1, ADAM_B2, ADAM_EPS, ADAM_WD, ADAM_STEP = 0.001, 0.9, 0.999, 1e-08, 0.01, 10


def _cparams(sem=None, **kw):
    return pltpu.CompilerParams(dimension_semantics=sem, vmem_limit_bytes=VMEM_LIMIT_BYTES, **kw)


def _mm(a, b, dims, prec=None):
    return lax.dot_general(a, b, (dims, ((), ())), preferred_element_type=F32, precision=prec)


def dot_nn(a, b, prec=None):
    return _mm(a, b, ((1,), (0,)), prec)


def dot_nt(a, b, prec=None):
    return _mm(a, b, ((1,), (1,)), prec)


def dot_tn(a, b, prec=None):
    return _mm(a, b, ((0,), (0,)), prec)


def bf(v):
    return v.astype(BF16)


def _sigmoid(z):
    return jax.nn.sigmoid(z)


def _dsilu(z, s):
    return s * (1.0 + z * (1.0 - s))


def _rms(x):
    r = lax.rsqrt(jnp.mean(x * x, axis=-1, keepdims=True) + EPS)
    return x * r, r


def _rms_bwd(dxn, u, r, g):
    du = dxn * g
    dx = r * (du - u * jnp.mean(du * u, axis=-1, keepdims=True))
    return dx, jnp.sum(dxn * u, axis=0, keepdims=True)


def _lane_mask0(shape):
    return lax.broadcasted_iota(jnp.int32, shape, len(shape) - 1) < HEAD64


def _rms64(x, m0):
    sq = x * x
    s0 = jnp.sum(jnp.where(m0, sq, 0.0), axis=-1, keepdims=True)
    s1 = jnp.sum(jnp.where(m0, 0.0, sq), axis=-1, keepdims=True)
    r = lax.rsqrt(jnp.where(m0, s0, s1) * (1.0 / HEAD64) + EPS)
    return x * r, r


def _rms64_bwd(dxn, u, r, g, m0):
    du = dxn * g
    t = du * u
    t0 = jnp.sum(jnp.where(m0, t, 0.0), axis=-1, keepdims=True)
    t1 = jnp.sum(jnp.where(m0, 0.0, t), axis=-1, keepdims=True)
    dx = r * (du - u * (jnp.where(m0, t0, t1) * (1.0 / HEAD64)))
    return dx, jnp.sum(dxn * u, axis=0, keepdims=True)


def _tok(s):
    return TOK if s % TOK == 0 else s


def _const(shape):
    return pl.BlockSpec(shape, lambda *_: (0,) * len(shape))


def ffn_fwd(x, gain3, l, wg, wu, wd, wl, comm=None):
    s, d = x.shape
    nc, _, fc, _ = wg.shape
    tm = _tok(s)

    def body(x_ref, g_ref, wg_ref, wu_ref, wd_ref, xo_ref, a_ref, b_ref):
        xv = x_ref[...]
        u, _ = _rms(xv)
        xn = bf(u * g_ref[...])
        y = jnp.zeros((tm, d), F32)
        for c in range(nc):
            a = dot_nt(xn, wg_ref[c])
            b = dot_nt(xn, wu_ref[c])
            a_ref[c] = bf(a)
            b_ref[c] = bf(b)
            y = y + dot_nn(bf(a * _sigmoid(a) * b), wd_ref[c])
        xo_ref[...] = xv + 0.5 * y

    wspec = pl.BlockSpec((nc, None, fc, d), lambda i: (0, wl, 0, 0), pipeline_mode=pl.Buffered(1))
    wdspec = pl.BlockSpec((nc, None, fc, d), lambda i: (0, wl, 0, 0), pipeline_mode=pl.Buffered(1))
    row = pl.BlockSpec((tm, d), lambda i: (i, 0))
    act = pl.BlockSpec((nc, tm, fc), lambda i: (0, i, 0))
    return _carry(
        body,
        comm,
        name="ffn_fwd",
        grid=(s // tm,),
        in_specs=[row, pl.BlockSpec((None, 1, d), lambda i: (l, 0, 0)), wspec, wspec, wdspec],
        out_specs=[row, act, act],
        out_shape=[
            jax.ShapeDtypeStruct((s, d), F32),
            jax.ShapeDtypeStruct((nc, s, fc), BF16),
            jax.ShapeDtypeStruct((nc, s, fc), BF16),
        ],
        scratch_shapes=[],
        args=(x, gain3, wg, wu, wd),
    )


def ffn_bwd(x, gain3, l, dout, a, b, wg, wu, wd, wl, comm=None):
    s, d = x.shape
    nc, _, fc, _ = wg.shape
    tm = _tok(s)

    def body(x_ref, g_ref, do_ref, a_ref, b_ref, wg_ref, wu_ref, wd_ref, dx_ref, da_ref, db_ref, hm_ref, xn_ref, dy_ref, dg_ref):
        xv = x_ref[...]
        g = g_ref[...]
        u, r = _rms(xv)
        xn_ref[...] = bf(u * g)
        dout = do_ref[...]
        dy = bf(0.5 * dout)
        dy_ref[...] = dy
        dxn = jnp.zeros((tm, d), F32)
        for c in range(nc):
            av = a_ref[c].astype(F32)
            bv = b_ref[c].astype(F32)
            sg = _sigmoid(av)
            sl = av * sg
            dh = dot_nt(dy, wd_ref[c])
            da = bf(dh * bv * _dsilu(av, sg))
            db = bf(dh * sl)
            da_ref[c] = da
            db_ref[c] = db
            hm_ref[c] = bf(sl * bv)
            dxn = dxn + dot_nn(da, wg_ref[c]) + dot_nn(db, wu_ref[c])
        dx, dg = _rms_bwd(dxn, u, r, g)
        dx_ref[...] = dout + dx

        @pl.when(pl.program_id(0) == 0)
        def _():
            dg_ref[...] = jnp.zeros_like(dg_ref)

        dg_ref[...] += dg

    wspec = pl.BlockSpec((nc, None, fc, d), lambda i: (0, wl, 0, 0), pipeline_mode=pl.Buffered(1))
    wdspec = pl.BlockSpec((nc, None, fc, d), lambda i: (0, wl, 0, 0), pipeline_mode=pl.Buffered(1))
    row = pl.BlockSpec((tm, d), lambda i: (i, 0))
    act = pl.BlockSpec((nc, tm, fc), lambda i: (0, i, 0))
    act_shape = jax.ShapeDtypeStruct((nc, s, fc), BF16)
    return _carry(
        body,
        comm,
        name="ffn_bwd",
        grid=(s // tm,),
        in_specs=[row, pl.BlockSpec((None, 1, d), lambda i: (l, 0, 0)), row, act, act, wspec, wspec, wdspec],
        out_specs=[row, act, act, act, row, row, _const((1, d))],
        out_shape=[
            jax.ShapeDtypeStruct((s, d), F32),
            act_shape,
            act_shape,
            act_shape,
            jax.ShapeDtypeStruct((s, d), BF16),
            jax.ShapeDtypeStruct((s, d), BF16),
            jax.ShapeDtypeStruct((1, d), F32),
        ],
        scratch_shapes=[],
        args=(x, gain3, dout, a, b, wg, wu, wd),
    )


def wgrad(a, b, tn=None, tm=None, split=False, chip_rows=None):
    ca = a.shape[0] if a.ndim == 3 else 1
    cb = b.shape[0] if b.ndim == 3 else 1
    nc = max(ca, cb)
    s, m = a.shape[-2:]
    n = b.shape[-1]
    tn = n if tn is None else tn
    assert n % tn == 0
    tm = m if tm is None else tm
    per_tile = None if chip_rows is None else tm // chip_rows

    def body(*refs):
        a_ref, b_ref = refs[0], refs[1]
        res = dot_tn(a_ref[...], b_ref[...])
        if split:
            for o in refs[2:]:
                o[0] = res[: m // 2].astype(o.dtype)
                o[1] = res[m // 2 :].astype(o.dtype)
        elif chip_rows is not None:
            hr = chip_rows // 2
            for o in refs[2:]:
                for k in range(per_tile):
                    for hf in range(2):
                        r0 = k * chip_rows + hf * hr
                        o[hf, k] = res[r0 : r0 + hr].astype(o.dtype)
        else:
            refs[2][...] = res

    params = _cparams(("arbitrary", "arbitrary"))
    if not split:
        assert nc == 1 and a.ndim == 2 and b.ndim == 2 and m % tm == 0
        in_specs = [pl.BlockSpec((s, tm), lambda i, j: (0, i)), pl.BlockSpec((s, tn), lambda i, j: (0, j))]
        if chip_rows is None:
            return pl.pallas_call(
                body,
                name="wgrad",
                grid=(m // tm, n // tn),
                in_specs=in_specs,
                out_specs=pl.BlockSpec((tm, tn), lambda i, j: (i, j)),
                out_shape=jax.ShapeDtypeStruct((m, n), F32),
                compiler_params=params,
            )(a, b)
        assert tm % chip_rows == 0
        laid = pl.BlockSpec((2, per_tile, chip_rows // 2, tn), lambda i, j: (0, i, 0, j))
        shape = (2, m // chip_rows, chip_rows // 2, n)
        return pl.pallas_call(
            body,
            name="wgrad_chips",
            grid=(m // tm, n // tn),
            in_specs=in_specs,
            out_specs=[laid, laid],
            out_shape=[jax.ShapeDtypeStruct(shape, F32), jax.ShapeDtypeStruct(shape, BF16)],
            compiler_params=params,
        )(a, b)
    a_spec = pl.BlockSpec((None, s, m), lambda c, j: (c, 0, 0)) if a.ndim == 3 else pl.BlockSpec((s, m), lambda c, j: (0, 0))
    b_spec = pl.BlockSpec((None, s, tn), lambda c, j: (c, 0, j)) if b.ndim == 3 else pl.BlockSpec((s, tn), lambda c, j: (0, j))
    halves = pl.BlockSpec((2, None, None, m // 2, tn), lambda c, j: (0, c, 0, 0, j))
    return pl.pallas_call(
        body,
        name="wgrad_split",
        grid=(nc, n // tn),
        in_specs=[a_spec, b_spec],
        out_specs=[halves, halves],
        out_shape=[jax.ShapeDtypeStruct((2, nc, 1, m // 2, n), F32), jax.ShapeDtypeStruct((2, nc, 1, m // 2, n), BF16)],
        compiler_params=params,
    )(a, b)


def proj_fwd(x, gain3, l, w, wl, wt=False):
    s, d = x.shape
    n = w.shape[1] if wt else w.shape[2]
    tm = _tok(s)

    def body(x_ref, g_ref, w_ref, o_ref):
        u, _ = _rms(x_ref[...])
        xn = bf(u * g_ref[...])
        o_ref[...] = dot_nt(xn, w_ref[...]) if wt else dot_nn(xn, w_ref[...])

    return pl.pallas_call(
        body,
        name="proj_fwd",
        grid=(s // tm,),
        in_specs=[
            pl.BlockSpec((tm, d), lambda i: (i, 0)),
            pl.BlockSpec((None, 1, d), lambda i: (l, 0, 0)),
            pl.BlockSpec((None,) + w.shape[1:], lambda i: (wl, 0, 0)),
        ],
        out_specs=pl.BlockSpec((tm, n), lambda i: (i, 0)),
        out_shape=jax.ShapeDtypeStruct((s, n), F32),
        compiler_params=_cparams(("arbitrary",)),
    )(x, gain3, w)


def proj_bwd(x, gain3, l, parts, w, wl, dx_in, wt=False):
    s, d = x.shape
    n = w.shape[1] if wt else w.shape[2]
    widths = [p.shape[1] for p in parts]
    assert sum(widths) == n
    tm = _tok(s)
    npart = len(parts)

    def body(*refs):
        x_ref, g_ref, w_ref, dxin_ref = refs[:4]
        p_refs = refs[4 : 4 + npart]
        dx_ref, dg_ref, xn_ref, dpb_ref = refs[4 + npart :]
        g = g_ref[...]
        u, r = _rms(x_ref[...])
        xn_ref[...] = bf(u * g)
        dxn = jnp.zeros((tm, d), F32)
        off = 0
        for p_ref, wd_ in zip(p_refs, widths):
            dp = bf(p_ref[...])
            dpb_ref[:, off : off + wd_] = dp
            dxn = dxn + (dot_nn(dp, w_ref[off : off + wd_, :]) if wt else dot_nt(dp, w_ref[:, off : off + wd_]))
            off += wd_
        dx, dg = _rms_bwd(dxn, u, r, g)
        dx_ref[...] = dxin_ref[...] + dx

        @pl.when(pl.program_id(0) == 0)
        def _():
            dg_ref[...] = jnp.zeros_like(dg_ref)

        dg_ref[...] += dg

    row = pl.BlockSpec((tm, d), lambda i: (i, 0))
    return pl.pallas_call(
        body,
        name="proj_bwd",
        grid=(s // tm,),
        in_specs=[row, pl.BlockSpec((None, 1, d), lambda i: (l, 0, 0)), pl.BlockSpec((None,) + w.shape[1:], lambda i: (wl, 0, 0)), row]
        + [pl.BlockSpec((tm, wd_), lambda i: (i, 0)) for wd_ in widths],
        out_specs=[row, _const((1, d)), row, pl.BlockSpec((tm, n), lambda i: (i, 0))],
        out_shape=[
            jax.ShapeDtypeStruct((s, d), F32),
            jax.ShapeDtypeStruct((1, d), F32),
            jax.ShapeDtypeStruct((s, d), BF16),
            jax.ShapeDtypeStruct((s, n), BF16),
        ],
        compiler_params=_cparams(("arbitrary",)),
    )(x, gain3, w, dx_in, *parts)


def mm_res(x, a, w, l):
    s, d = x.shape
    k = a.shape[1]
    tm = _tok(s)

    def body(x_ref, a_ref, w_ref, o_ref):
        o_ref[...] = x_ref[...] + dot_nn(a_ref[...], w_ref[...])

    return pl.pallas_call(
        body,
        name="mm_res",
        grid=(s // tm,),
        in_specs=[
            pl.BlockSpec((tm, d), lambda i: (i, 0)),
            pl.BlockSpec((tm, k), lambda i: (i, 0)),
            pl.BlockSpec((None, k, d), lambda i: (l, 0, 0)),
        ],
        out_specs=pl.BlockSpec((tm, d), lambda i: (i, 0)),
        out_shape=jax.ShapeDtypeStruct((s, d), F32),
        compiler_params=_cparams(("arbitrary",)),
    )(x, a, w)


def mm_nt(dx, w, l):
    s, d = dx.shape
    k = w.shape[1]
    tm = _tok(s)

    def body(dx_ref, w_ref, o_ref, dxb_ref):
        dxb = bf(dx_ref[...])
        dxb_ref[...] = dxb
        o_ref[...] = dot_nt(dxb, w_ref[...])

    return pl.pallas_call(
        body,
        name="mm_nt",
        grid=(s // tm,),
        in_specs=[pl.BlockSpec((tm, d), lambda i: (i, 0)), pl.BlockSpec((None, k, d), lambda i: (l, 0, 0))],
        out_specs=[pl.BlockSpec((tm, k), lambda i: (i, 0)), pl.BlockSpec((tm, d), lambda i: (i, 0))],
        out_shape=[jax.ShapeDtypeStruct((s, k), F32), jax.ShapeDtypeStruct((s, d), BF16)],
        compiler_params=_cparams(("arbitrary",)),
    )(dx, w)


def lb_fwd(logits3):
    def body(l_ref, o_ref):
        l0, l1 = l_ref[0], l_ref[1]
        m = jnp.maximum(l0, l1)
        e0, e1 = jnp.exp(l0 - m), jnp.exp(l1 - m)
        p0, p1 = e0 / (e0 + e1), e1 / (e0 + e1)
        o_ref[0] = p0 - p0
        o_ref[1] = (p0 + p1) - p0

    return pl.pallas_call(body, name="lb_fwd", out_shape=jax.ShapeDtypeStruct(logits3.shape, F32))(logits3)


def lb_bwd(logits3, dlb1):
    def body(l_ref, d_ref, o_ref):
        l0, l1 = l_ref[0], l_ref[1]
        m = jnp.maximum(l0, l1)
        e0, e1 = jnp.exp(l0 - m), jnp.exp(l1 - m)
        p0, p1 = e0 / (e0 + e1), e1 / (e0 + e1)
        t = d_ref[...] * p0 * p1
        o_ref[0] = -t
        o_ref[1] = t

    return pl.pallas_call(body, name="lb_bwd", out_shape=jax.ShapeDtypeStruct(logits3.shape, F32))(logits3, dlb1)


def _hgrn_gates(zq, zf, lb):
    sf = _sigmoid(zf)
    f = lb + (1.0 - lb) * sf
    sq = _sigmoid(zq)
    return sf, f, jnp.log(f), 1.0 - f, sq, zq * sq


def _tri(n, upper=False):
    r = lax.broadcasted_iota(jnp.int32, (n, n), 0)
    c = lax.broadcasted_iota(jnp.int32, (n, n), 1)
    return jnp.where((c >= r) if upper else (r >= c), 1.0, 0.0).astype(F32)


def hgrn_fwd(proj, lb3, og3, l, comm=None):
    s = proj.shape[0]
    nh = 6
    n_chunk = s // CHUNK
    nsub = CHUNK // SUB

    hb = HGRN_HEADS_PER_STEP
    wide = hb * LANES

    def body(zq_ref, zf_ref, vi_ref, zg_ref, lb_ref, og_ref, main_ref, o_ref, q_a, k_a, v_a, c_a):
        og = og_ref[...]
        tril = _tri(CHUNK)
        rowi = lax.broadcasted_iota(jnp.int32, (SUB, LANES), 0)

        def one_head(hd, rows, st):
            cols = slice(hd * LANES, (hd + 1) * LANES)
            q_s, k_s, v_s, c_s = q_a.at[hd], k_a.at[hd], v_a.at[hd], c_a.at[hd]
            zg = zg_ref[rows, cols]
            _, _, lf, k, _, q = _hgrn_gates(zq_ref[rows, cols], zf_ref[rows, cols], lb_ref[:, cols])
            v = vi_ref[rows, cols]
            c = dot_nn(tril, lf, HI)
            q_s[...] = q
            k_s[...] = k
            v_s[...] = v
            c_s[...] = c
            o_inter = dot_nt(q * jnp.exp(c), st, HI)
            parts = []
            for i in range(nsub):
                lo = i * SUB
                blk = pl.ds(lo, SUB)
                qb, cb = q_s[blk, :], c_s[blk, :]
                ob = o_inter[lo : lo + SUB]
                if i > 0:
                    rr = c_s[pl.ds(lo - 1, 1), :]
                    qt = qb * jnp.exp(cb - rr)
                    kt = k_s[pl.ds(0, lo), :] * jnp.exp(rr - c_s[pl.ds(0, lo), :])
                    ob = ob + dot_nn(dot_nt(qt, kt, HI), v_s[pl.ds(0, lo), :], HI)
                for t in range(SUB):
                    e = jnp.where(rowi >= t, jnp.exp(cb - c_s[pl.ds(lo + t, 1), :]), 0.0)
                    a = jnp.sum(qb * k_s[pl.ds(lo + t, 1), :] * e, axis=-1, keepdims=True)
                    ob = ob + a * v_s[pl.ds(lo + t, 1), :]
                parts.append(ob)
            o = jnp.concatenate(parts, axis=0)
            ce = c_s[pl.ds(CHUNK - 1, 1), :]
            st = st * jnp.exp(ce) + dot_tn(v, k * jnp.exp(ce - c), HI)
            on, _ = _rms(o)
            o_ref[rows, cols] = o
            main_ref[rows, cols] = bf(on * og * (zg * _sigmoid(zg)))
            return st

        def chunk(ci, sts):
            rows = pl.ds(pl.multiple_of(ci * CHUNK, CHUNK), CHUNK)
            return tuple(one_head(hd, rows, sts[hd]) for hd in range(hb))

        lax.fori_loop(0, n_chunk, chunk, tuple(jnp.zeros((LANES, LANES), F32) for _ in range(hb)))

    def col(k):
        return pl.BlockSpec((s, wide), lambda h: (0, k * (nh // hb) + h))

    vec = pl.BlockSpec((None, 1, wide), lambda h: (l, 0, h))
    return _carry(
        body,
        comm,
        name="hgrn_fwd",
        grid=(nh // hb,),
        in_specs=[col(0), col(1), col(2), col(3), vec, pl.BlockSpec((None, 1, LANES), lambda h: (l, 0, 0))],
        out_specs=[pl.BlockSpec((s, wide), lambda h: (0, h))] * 2,
        out_shape=[jax.ShapeDtypeStruct((s, nh * LANES), BF16), jax.ShapeDtypeStruct((s, nh * LANES), F32)],
        scratch_shapes=[pltpu.VMEM((hb, CHUNK, LANES), F32)] * 4,
        args=(proj, proj, proj, proj, lb3, og3),
    )


def hgrn_bwd(proj, lb3, og3, l, o, dmixed, comm=None):
    s = proj.shape[0]
    nh = 6
    n_chunk = s // CHUNK
    nsub = CHUNK // SUB

    hb = HGRN_HEADS_PER_STEP
    wide = hb * LANES

    def body(zq_ref, zf_ref, vi_ref, zg_ref, lb_ref, og_ref, o_ref, dm_ref,
             dzq_ref, dzf_ref, dvi_ref, dzg_ref, dlb_ref, dog_ref,
             st_a, q_a, k_a, v_a, c_a, do_a, dq_a, dk_a, dv_a, acc_a):
        og = og_ref[...]
        tril = _tri(CHUNK)
        triu = _tri(CHUNK, upper=True)
        rowi = lax.broadcasted_iota(jnp.int32, (SUB, LANES), 0)

        def fwd_head(hd, ci, rows, st):
            cols = slice(hd * LANES, (hd + 1) * LANES)
            _, _, lf, k, _, _ = _hgrn_gates(zq_ref[rows, cols], zf_ref[rows, cols], lb_ref[:, cols])
            c = dot_nn(tril, lf, HI)
            ce = jnp.sum(lf, axis=0, keepdims=True)
            st_a[hd, ci] = st
            return st * jnp.exp(ce) + dot_tn(vi_ref[rows, cols], k * jnp.exp(ce - c), HI)

        def fwd_chunk(ci, sts):
            rows = pl.ds(pl.multiple_of(ci * CHUNK, CHUNK), CHUNK)
            return tuple(fwd_head(hd, ci, rows, sts[hd]) for hd in range(hb))

        lax.fori_loop(0, n_chunk, fwd_chunk, tuple(jnp.zeros((LANES, LANES), F32) for _ in range(hb)))
        acc_a[...] = jnp.zeros_like(acc_a)

        def bwd_head(hd, ci, rows, carry):
            dst, cg = carry
            cols = slice(hd * LANES, (hd + 1) * LANES)
            q_s, k_s, v_s, c_s, do_s = q_a.at[hd], k_a.at[hd], v_a.at[hd], c_a.at[hd], do_a.at[hd]
            dq_s, dk_s, dv_s, acc_s = dq_a.at[hd], dk_a.at[hd], dv_a.at[hd], acc_a.at[hd]
            lb = lb_ref[:, cols]
            zq, zf, zg = zq_ref[rows, cols], zf_ref[rows, cols], zg_ref[rows, cols]
            sf, f, lf, k, sq, q = _hgrn_gates(zq, zf, lb)
            v = vi_ref[rows, cols]
            c = dot_nn(tril, lf, HI)
            st = st_a[hd, ci]
            on, r = _rms(o_ref[rows, cols])
            sg = _sigmoid(zg)
            dmain = dm_ref[rows, cols]
            dy = dmain * (zg * sg)
            dzg_ref[rows, cols] = dmain * (on * og) * _dsilu(zg, sg)
            do, dog = _rms_bwd(dy, on, r, og)
            acc_s[pl.ds(0, 1), :] += dog
            q_s[...] = q
            k_s[...] = k
            v_s[...] = v
            c_s[...] = c
            do_s[...] = do
            ce = c_s[pl.ds(CHUNK - 1, 1), :]
            eq = jnp.exp(c)
            ek = jnp.exp(ce - c)
            qt_all = q * eq
            dq_s[...] = dot_nn(do, st, HI) * eq
            dv_s[...] = dot_nt(k * ek, dst, HI)
            dk_s[...] = dot_nn(v, dst, HI) * ek
            dst = dst * jnp.exp(ce) + dot_tn(do, qt_all, HI)
            for i in range(nsub):
                lo = i * SUB
                blk = pl.ds(lo, SUB)
                qb, cb, dob = q_s[blk, :], c_s[blk, :], do_s[blk, :]
                if i > 0:
                    prev = pl.ds(0, lo)
                    rr = c_s[pl.ds(lo - 1, 1), :]
                    eqi = jnp.exp(cb - rr)
                    eki = jnp.exp(rr - c_s[prev, :])
                    qt = qb * eqi
                    kt = k_s[prev, :] * eki
                    amat = dot_nt(qt, kt, HI)
                    damat = dot_nt(dob, v_s[prev, :], HI)
                    dv_s[prev, :] += dot_tn(amat, dob, HI)
                    dq_s[blk, :] += dot_nn(damat, kt, HI) * eqi
                    dk_s[prev, :] += dot_tn(damat, qt, HI) * eki
                dqb = jnp.zeros((SUB, LANES), F32)
                for t in range(SUB):
                    row = pl.ds(lo + t, 1)
                    e = jnp.where(rowi >= t, jnp.exp(cb - c_s[row, :]), 0.0)
                    kr = k_s[row, :]
                    a = jnp.sum(qb * kr * e, axis=-1, keepdims=True)
                    da = jnp.sum(dob * v_s[row, :], axis=-1, keepdims=True)
                    dv_s[row, :] += jnp.sum(a * dob, axis=0, keepdims=True)
                    dqb = dqb + da * kr * e
                    dk_s[row, :] += jnp.sum(da * qb * e, axis=0, keepdims=True)
                dq_s[blk, :] += dqb
            dq, dk = dq_s[...], dk_s[...]
            dg = q * dq - k * dk
            dlf = dot_nn(triu, dg, HI) + cg
            cg = cg + jnp.sum(dg, axis=0, keepdims=True)
            df = dlf / f - dk
            dzf_ref[rows, cols] = df * (1.0 - lb) * sf * (1.0 - sf)
            acc_s[pl.ds(1, 1), :] += jnp.sum(df * (1.0 - sf), axis=0, keepdims=True)
            dzq_ref[rows, cols] = dq * _dsilu(zq, sq)
            dvi_ref[rows, cols] = dv_s[...]
            return dst, cg

        def bwd_chunk(jj, carries):
            ci = n_chunk - 1 - jj
            rows = pl.ds(pl.multiple_of(ci * CHUNK, CHUNK), CHUNK)
            return tuple(bwd_head(hd, ci, rows, carries[hd]) for hd in range(hb))

        zero = (jnp.zeros((LANES, LANES), F32), jnp.zeros((1, LANES), F32))
        lax.fori_loop(0, n_chunk, bwd_chunk, tuple(zero for _ in range(hb)))

        @pl.when(pl.program_id(0) == 0)
        def _():
            dog_ref[...] = jnp.zeros_like(dog_ref)

        for hd in range(hb):
            dlb_ref[:, hd * LANES : (hd + 1) * LANES] = acc_a[hd, pl.ds(1, 1), :]
            dog_ref[...] += acc_a[hd, pl.ds(0, 1), :]

    def col(k):
        return pl.BlockSpec((s, wide), lambda h: (0, k * (nh // hb) + h), pipeline_mode=pl.Buffered(1))

    head_in = pl.BlockSpec((s, wide), lambda h: (0, h), pipeline_mode=pl.Buffered(1))
    head = pl.BlockSpec((s, wide), lambda h: (0, h))
    vec = pl.BlockSpec((None, 1, wide), lambda h: (l, 0, h))
    ck = pltpu.VMEM((hb, CHUNK, LANES), F32)
    return _carry(
        body,
        comm,
        name="hgrn_bwd",
        grid=(nh // hb,),
        in_specs=[col(0), col(1), col(2), col(3), vec, pl.BlockSpec((None, 1, LANES), lambda h: (l, 0, 0)), head_in, head_in],
        out_specs=[head] * 4 + [pl.BlockSpec((1, wide), lambda h: (0, h)), _const((1, LANES))],
        out_shape=[jax.ShapeDtypeStruct((s, nh * LANES), F32)] * 4
        + [jax.ShapeDtypeStruct((1, nh * LANES), F32), jax.ShapeDtypeStruct((1, LANES), F32)],
        scratch_shapes=[pltpu.VMEM((hb, n_chunk, LANES, LANES), F32)] + [ck] * 8 + [pltpu.VMEM((hb, 8, LANES), F32)],
        args=(proj, proj, proj, proj, lb3, og3, o, dmixed),
    )


MEM_SCALE = HEAD64**-0.5


def _mem_heads(qraw, kvm, qg, kg, pr, m0):
    lo = pr * LANES
    uq, rq = _rms64(qraw[:, lo : lo + LANES], m0)
    uk, rk = _rms64(kvm[:, lo : lo + LANES], m0)
    v = bf(kvm[:, 2 * LANES + lo : 3 * LANES + lo])
    return uq, rq, uk, rk, v, uq * qg, bf(uk * kg)


def memattn_fwd(proj, qblk, kvm, qg3, kg3, l):
    s = proj.shape[0]
    nm = kvm.shape[0]
    tm = _tok(s)

    def body(q_ref, kv_ref, qg_ref, kg_ref, o_ref):
        m0 = _lane_mask0((1, LANES))
        qraw, kvv = q_ref[...], kv_ref[...]
        for pr in range(2):
            _, _, _, _, v, qn, kn = _mem_heads(qraw, kvv, qg_ref[...], kg_ref[...], pr, m0)
            out = jnp.zeros((tm, LANES), F32)
            for hh in range(2):
                mh = m0 if hh == 0 else jnp.logical_not(m0)
                sc = dot_nt(bf(jnp.where(mh, qn, 0.0)), kn) * MEM_SCALE
                p = jnp.exp(sc - jnp.max(sc, axis=-1, keepdims=True))
                p = p / jnp.sum(p, axis=-1, keepdims=True)
                out = jnp.where(mh, dot_nn(bf(p), v), out)
            o_ref[:, pr * LANES : (pr + 1) * LANES] = bf(out)

    gspec = pl.BlockSpec((None, 1, LANES), lambda i: (l, 0, 0))
    return pl.pallas_call(
        body,
        name="memattn_fwd",
        grid=(s // tm,),
        in_specs=[pl.BlockSpec((tm, 2 * LANES), lambda i: (i, qblk)), _const((nm, 4 * LANES)), gspec, gspec],
        out_specs=pl.BlockSpec((tm, 2 * LANES), lambda i: (i, 0)),
        out_shape=jax.ShapeDtypeStruct((s, 2 * LANES), BF16),
        compiler_params=_cparams(("arbitrary",)),
    )(proj, kvm, qg3, kg3)


def memattn_bwd(proj, qblk, kvm, qg3, kg3, l, dmixed):
    s = proj.shape[0]
    nm = kvm.shape[0]
    tm = _tok(s)

    def body(q_ref, kv_ref, qg_ref, kg_ref, dm_ref, dq_ref, dkv_ref, dqg_ref, dkg_ref):
        m0 = _lane_mask0((1, LANES))
        qraw, kvv = q_ref[...], kv_ref[...]
        qg, kg = qg_ref[...], kg_ref[...]

        @pl.when(pl.program_id(0) == 0)
        def _():
            dkv_ref[...] = jnp.zeros_like(dkv_ref)
            dqg_ref[...] = jnp.zeros_like(dqg_ref)
            dkg_ref[...] = jnp.zeros_like(dkg_ref)

        for pr in range(2):
            lo = pr * LANES
            uq, rq, uk, rk, v, qn, kn = _mem_heads(qraw, kvv, qg, kg, pr, m0)
            do = dm_ref[:, lo : lo + LANES]
            dqn = jnp.zeros((tm, LANES), F32)
            dkn = jnp.zeros((nm, LANES), F32)
            dv = jnp.zeros((nm, LANES), F32)
            for hh in range(2):
                mh = m0 if hh == 0 else jnp.logical_not(m0)
                qh = bf(jnp.where(mh, qn, 0.0))
                doh = bf(jnp.where(mh, do, 0.0))
                sc = dot_nt(qh, kn) * MEM_SCALE
                p = jnp.exp(sc - jnp.max(sc, axis=-1, keepdims=True))
                p = p / jnp.sum(p, axis=-1, keepdims=True)
                dp = dot_nt(doh, v)
                ds = bf(p * (dp - jnp.sum(p * dp, axis=-1, keepdims=True)))
                dqn = dqn + jnp.where(mh, dot_nn(ds, kn), 0.0) * MEM_SCALE
                dkn = dkn + dot_tn(ds, qh) * MEM_SCALE
                dv = dv + dot_tn(bf(p), doh)
            dqr, dqg = _rms64_bwd(dqn, uq, rq, qg, m0)
            dkr, dkg = _rms64_bwd(dkn, uk, rk, kg, m0)
            dq_ref[:, lo : lo + LANES] = dqr
            dkv_ref[:, lo : lo + LANES] += dkr
            dkv_ref[:, 2 * LANES + lo : 3 * LANES + lo] += dv
            dqg_ref[...] += dqg
            dkg_ref[...] += dkg

    gspec = pl.BlockSpec((None, 1, LANES), lambda i: (l, 0, 0))
    return pl.pallas_call(
        body,
        name="memattn_bwd",
        grid=(s // tm,),
        in_specs=[
            pl.BlockSpec((tm, 2 * LANES), lambda i: (i, qblk)),
            _const((nm, 4 * LANES)),
            gspec,
            gspec,
            pl.BlockSpec((tm, 2 * LANES), lambda i: (i, 3)),
        ],
        out_specs=[pl.BlockSpec((tm, 2 * LANES), lambda i: (i, 0)), _const((nm, 4 * LANES)), _const((1, LANES)), _const((1, LANES))],
        out_shape=[
            jax.ShapeDtypeStruct((s, 2 * LANES), F32),
            jax.ShapeDtypeStruct((nm, 4 * LANES), F32),
            jax.ShapeDtypeStruct((1, LANES), F32),
            jax.ShapeDtypeStruct((1, LANES), F32),
        ],
        compiler_params=_cparams(("arbitrary",)),
    )(proj, kvm, qg3, kg3, dmixed)


KV_MAIN = 768


def _log_sigmoid(z):
    return jnp.minimum(z, 0.0) - jnp.log(1.0 + jnp.exp(-jnp.abs(z)))


def kvprep_fwd(kvf, kg, fb):
    s = kvf.shape[0]
    tm = _tok(s)

    def body(kvf_ref, kg_ref, fb_ref, k_ref, v_ref, clf_ref, carry):
        m0 = _lane_mask0((1, LANES))

        @pl.when(pl.program_id(0) == 0)
        def _():
            carry[...] = jnp.zeros_like(carry)

        for j in range(KV_MAIN // LANES):
            u, _ = _rms64(kvf_ref[:, j * LANES : (j + 1) * LANES], m0)
            k_ref[:, j * LANES : (j + 1) * LANES] = bf(u * kg_ref[...])
        v_ref[...] = bf(kvf_ref[:, KV_MAIN : 2 * KV_MAIN])
        lf = _log_sigmoid(kvf_ref[:, 2 * KV_MAIN :] + fb_ref[...])
        clf_ref[...] = dot_nn(_tri(tm), lf, HI) + carry[...]
        carry[...] += jnp.sum(lf, axis=0, keepdims=True)

    n = kvf.shape[1]
    return pl.pallas_call(
        body,
        name="kvprep_fwd",
        grid=(s // tm,),
        in_specs=[pl.BlockSpec((tm, n), lambda i: (i, 0)), _const((1, LANES)), _const((1, LANES))],
        out_specs=[pl.BlockSpec((tm, KV_MAIN), lambda i: (i, 0))] * 2 + [pl.BlockSpec((tm, LANES), lambda i: (i, 0))],
        out_shape=[jax.ShapeDtypeStruct((s, KV_MAIN), BF16)] * 2 + [jax.ShapeDtypeStruct((s, LANES), F32)],
        scratch_shapes=[pltpu.VMEM((1, LANES), F32)],
        compiler_params=_cparams(("arbitrary",)),
    )(kvf, kg, fb)


def kvprep_bwd(kvf, kg, fb, dk, dv, dclf):
    s, n = kvf.shape
    tm = _tok(s)
    nb = s // tm

    def body(kvf_ref, kg_ref, fb_ref, dk_ref, dv_ref, dclf_ref, o_ref, dkg_ref, dfb_ref, carry):
        m0 = _lane_mask0((1, LANES))

        @pl.when(pl.program_id(0) == 0)
        def _():
            carry[...] = jnp.zeros_like(carry)
            dkg_ref[...] = jnp.zeros_like(dkg_ref)
            dfb_ref[...] = jnp.zeros_like(dfb_ref)

        kg_ = kg_ref[...]
        for j in range(KV_MAIN // LANES):
            cols = slice(j * LANES, (j + 1) * LANES)
            u, r = _rms64(kvf_ref[:, cols], m0)
            dkr, dkg = _rms64_bwd(dk_ref[:, cols], u, r, kg_, m0)
            o_ref[:, cols] = dkr
            dkg_ref[...] += dkg
        o_ref[:, KV_MAIN : 2 * KV_MAIN] = dv_ref[...]
        z = kvf_ref[:, 2 * KV_MAIN :] + fb_ref[...]
        dc = dclf_ref[...]
        dlf = dot_nn(_tri(tm, upper=True), dc, HI) + carry[...]
        carry[...] += jnp.sum(dc, axis=0, keepdims=True)
        dz = dlf * _sigmoid(-z)
        o_ref[:, 2 * KV_MAIN :] = dz
        dfb_ref[...] += jnp.sum(dz, axis=0, keepdims=True)

    rev = lambda i: (nb - 1 - i, 0)
    return pl.pallas_call(
        body,
        name="kvprep_bwd",
        grid=(nb,),
        in_specs=[pl.BlockSpec((tm, n), rev), _const((1, LANES)), _const((1, LANES)), pl.BlockSpec((tm, KV_MAIN), rev),
                  pl.BlockSpec((tm, KV_MAIN), rev), pl.BlockSpec((tm, LANES), rev)],
        out_specs=[pl.BlockSpec((tm, n), rev), _const((1, LANES)), _const((1, LANES))],
        out_shape=[jax.ShapeDtypeStruct((s, n), F32), jax.ShapeDtypeStruct((1, LANES), F32), jax.ShapeDtypeStruct((1, LANES), F32)],
        scratch_shapes=[pltpu.VMEM((1, LANES), F32)],
        compiler_params=_cparams(("arbitrary",)),
    )(kvf, kg, fb, dk, dv, dclf)


FOX_SCALE = HEAD64**-0.5


def _lane_col(block, lane_idx, h):
    return jnp.sum(jnp.where(lane_idx == h, block, 0.0), axis=-1, keepdims=True)


def _causal(tq, ext, i, transposed=False):
    if transposed:
        key = lax.broadcasted_iota(jnp.int32, (ext, tq), 0)
        qry = lax.broadcasted_iota(jnp.int32, (ext, tq), 1) + i * tq
    else:
        qry = lax.broadcasted_iota(jnp.int32, (tq, ext), 0) + i * tq
        key = lax.broadcasted_iota(jnp.int32, (tq, ext), 1)
    return key <= qry


def fox_fwd(proj, k_sh, v_sh, clf, clf_t, qg3, j_layer, comm=None):
    s = proj.shape[0]
    npair = 6
    tq = TQ if s % TQ == 0 else s
    nq = s // tq

    def body(q_ref, gate_ref, k_ref, v_ref, clf_ref, clft_ref, qg_ref, main_ref, o_ref, lse_ref):
        j = pl.program_id(0)
        lane = lax.broadcasted_iota(jnp.int32, (1, LANES), 1)
        m0 = lane < HEAD64
        u, _ = _rms64(q_ref[...], m0)
        qn = u * qg_ref[...] * FOX_SCALE
        clfv = clf_ref[...]
        for hh in range(2):
            h = 2 * j + hh
            mh = m0 if hh == 0 else jnp.logical_not(m0)
            qh = bf(jnp.where(mh, qn, 0.0))
            dcol = _lane_col(clfv, lane, h)
            drow = clft_ref[pl.ds(h, 1), :]
            for i in range(nq):
                rows = slice(i * tq, (i + 1) * tq)
                ext = (i + 1) * tq
                sc = dot_nt(qh[rows], k_ref[0:ext, :]) + dcol[rows] - drow[:, :ext]
                sc = jnp.where(_causal(tq, ext, i), sc, -jnp.inf)
                m = jnp.max(sc, axis=-1, keepdims=True)
                p = jnp.exp(sc - m)
                lsum = jnp.sum(p, axis=-1, keepdims=True)
                pv = dot_nn(bf(p), v_ref[0:ext, :]) / lsum
                lse = m + jnp.log(lsum)
                if hh == 0:
                    o_ref[rows, :] = pv
                    lse_ref[rows, :] = jnp.where(lane == 0, lse, 0.0)
                else:
                    o_ref[rows, :] = jnp.where(mh, pv, o_ref[rows, :])
                    lse_ref[rows, :] = jnp.where(lane == 1, lse, lse_ref[rows, :])
        main_ref[...] = bf(o_ref[...] * _sigmoid(gate_ref[...]))

    blk = lambda off: pl.BlockSpec((s, LANES), lambda j: (0, off + j))
    return _carry(
        body,
        comm,
        name="fox_fwd",
        grid=(npair,),
        in_specs=[blk(0), blk(npair), blk(0), blk(0), _const((s, LANES)), _const((16, s)),
                  pl.BlockSpec((None, 1, LANES), lambda j: (j_layer, 0, 0))],
        out_specs=[blk(0)] * 3,
        out_shape=[jax.ShapeDtypeStruct((s, npair * LANES), BF16)] + [jax.ShapeDtypeStruct((s, npair * LANES), F32)] * 2,
        scratch_shapes=[],
        args=(proj, proj, k_sh, v_sh, clf, clf_t, qg3),
    )


def fox_bwd(proj, k_sh, v_sh, clf, clf_t, qg3, j_layer, o, lse, lse_t, dmixed, dk_in, dv_in, dclf_in, comm=None):
    s = proj.shape[0]
    npair = 6
    tq = TQ if s % TQ == 0 else s
    nq = s // tq

    def body(q_ref, gate_ref, k_ref, v_ref, clf_ref, clft_ref, qg_ref, o_ref, lse_ref, lset_ref, dm_ref, dkin_ref, dvin_ref, dclfin_ref,
             dq_ref, dgate_ref, dk_ref, dv_ref, dclf_ref, dqg_ref, dqn_s, dcl_s):
        j = pl.program_id(0)
        lane = lax.broadcasted_iota(jnp.int32, (1, LANES), 1)
        m0 = lane < HEAD64
        qg = qg_ref[...]
        u, r = _rms64(q_ref[...], m0)
        qn = u * qg * FOX_SCALE
        ov = o_ref[...]
        gate = gate_ref[...]
        sg = _sigmoid(gate)
        dmain = dm_ref[...]
        do = dmain * sg
        dgate_ref[...] = dmain * ov * sg * (1.0 - sg)
        dk_ref[...] = dkin_ref[...]
        dv_ref[...] = dvin_ref[...]
        clfv = clf_ref[...]
        lsev = lse_ref[...]
        ones8 = jnp.ones((8, LANES), F32)

        @pl.when(j == 0)
        def _():
            dclf_ref[...] = dclfin_ref[...]
            dqg_ref[...] = jnp.zeros_like(dqg_ref)

        for hh in range(2):
            h = 2 * j + hh
            mh = m0 if hh == 0 else jnp.logical_not(m0)
            qh = bf(jnp.where(mh, qn, 0.0))
            doh = jnp.where(mh, do, 0.0)
            dohb = bf(doh)
            doo = doh * ov
            dcol = _lane_col(clfv, lane, h)
            drow = clft_ref[pl.ds(h, 1), :]
            lcol = _lane_col(lsev, lane, hh)
            lrow = lset_ref[pl.ds(h, 1), :]
            delta = jnp.sum(doo, axis=-1, keepdims=True)
            dcl_s[...] = jnp.zeros_like(dcl_s)
            for i in range(nq):
                rows = slice(i * tq, (i + 1) * tq)
                ext = (i + 1) * tq
                kk, vv = k_ref[0:ext, :], v_ref[0:ext, :]
                sc = dot_nt(qh[rows], kk) + dcol[rows] - drow[:, :ext]
                p = jnp.where(_causal(tq, ext, i), jnp.exp(sc - lcol[rows]), 0.0)
                ds = p * (dot_nt(dohb[rows], vv) - delta[rows])
                dqh = dot_nn(bf(ds), kk) * FOX_SCALE
                if hh == 0:
                    dqn_s[rows, :] = dqh
                else:
                    dqn_s[rows, :] = jnp.where(mh, dqh, dqn_s[rows, :])
                dcl_s[rows, :] += jnp.sum(ds, axis=-1, keepdims=True)
                sct = dot_nt(kk, qh[rows]) + drow[:, rows] - dcol[:ext]
                pt = jnp.where(_causal(tq, ext, i, transposed=True), jnp.exp(sct - lrow[:, rows]), 0.0)
                delta_row = dot_nt(ones8, doo[rows], HI)[0:1]
                dst = pt * (dot_nt(vv, dohb[rows]) - delta_row)
                dv_ref[0:ext, :] += dot_nn(bf(pt), dohb[rows])
                dk_ref[0:ext, :] += dot_nn(bf(dst), qh[rows])
                dcl_s[0:ext, :] -= jnp.sum(dst, axis=-1, keepdims=True)
            dclf_ref[...] += jnp.where(lane == h, dcl_s[...], 0.0)
        dqr, dqg = _rms64_bwd(dqn_s[...], u, r, qg, m0)
        dq_ref[...] = dqr
        dqg_ref[...] += dqg

    blk = lambda off: pl.BlockSpec((s, LANES), lambda j: (0, off + j))
    full = _const((s, LANES))
    return _carry(
        body,
        comm,
        name="fox_bwd",
        grid=(npair,),
        in_specs=[blk(0), blk(npair), blk(0), blk(0), full, _const((16, s)), pl.BlockSpec((None, 1, LANES), lambda j: (j_layer, 0, 0)),
                  blk(0), blk(0), _const((16, s)), blk(0), blk(0), blk(0), full],
        out_specs=[blk(0)] * 4 + [full, _const((1, LANES))],
        out_shape=[jax.ShapeDtypeStruct((s, npair * LANES), F32)] * 4
        + [jax.ShapeDtypeStruct((s, LANES), F32), jax.ShapeDtypeStruct((1, LANES), F32)],
        scratch_shapes=[pltpu.VMEM((s, LANES), F32), pltpu.VMEM((s, LANES), F32)],
        args=(proj, proj, k_sh, v_sh, clf, clf_t, qg3, o, lse, lse_t, dmixed, dk_in, dv_in, dclf_in),
    )


def loss_head(y, target):
    s, d = y.shape
    tm = _tok(s)

    def body(y_ref, t_ref, loss_ref, dy_ref):
        err = y_ref[...] - t_ref[...]
        dy_ref[...] = err * (1.0 / d)

        @pl.when(pl.program_id(0) == 0)
        def _():
            loss_ref[...] = jnp.zeros_like(loss_ref)

        part = jnp.sum(jnp.mean(err * err, axis=-1, keepdims=True), axis=0, keepdims=True)
        loss_ref[...] += 0.5 * part

    row = pl.BlockSpec((tm, d), lambda i: (i, 0))
    return pl.pallas_call(
        body,
        name="loss_head",
        grid=(s // tm,),
        in_specs=[row, row],
        out_specs=[_const((1, 1)), row],
        out_shape=[jax.ShapeDtypeStruct((1, 1), F32), jax.ShapeDtypeStruct((s, d), F32)],
        compiler_params=_cparams(("arbitrary",)),
    )(y, target)


def _row_tile(r, c, n_arrays):
    budget = VMEM_LIMIT_BYTES // 2
    padded_c = -(-c // LANES) * LANES
    for step in (16, 8):
        fits = [t for t in range(step, r + 1, step) if r % t == 0 and 2 * n_arrays * t * padded_c * 4 <= budget]
        if fits:
            return fits[-1]
    return r


def _as2d(a):
    return a.reshape(-1, a.shape[-1]) if a.ndim >= 2 else a.reshape(1, -1)


def adamw(ws, gss, ms, vs, sizes=None, which=None, prev=None, comms=None):
    nw = len(ws)
    shape = ws[0].shape
    w2, m2, v2 = ([_as2d(t) for t in lst] for lst in (ws, ms, vs))
    rows, c = w2[0].shape
    gss = [[g.reshape(-1, c) for g in gs] for gs in gss]
    sizes = [g.shape[0] for g in gss[0]] if sizes is None else sizes
    which = list(range(len(sizes))) if which is None else which
    comms = [None] * len(which) if comms is None else comms
    assert sum(sizes) == rows
    tr = _row_tile(min(sizes), c, 8 * nw)
    assert all(r % tr == 0 for r in sizes)
    c1 = 1.0 - ADAM_B1**ADAM_STEP
    c2 = 1.0 - ADAM_B2**ADAM_STEP
    outs = [] if prev is None else [t.reshape(rows, c) for res in prev for t in res]
    carried = []
    for j, k in enumerate(which):
        n_prev = len(outs)
        r = sizes[k]
        first = sum(sizes[:k])

        def body(*refs, n_prev=n_prev):
            out_refs = refs[4 * nw + n_prev :]
            for i in range(nw):
                w_ref, g_ref, m_ref, v_ref = refs[4 * i : 4 * i + 4]
                go_ref, d_ref, nm_ref, nv_ref = out_refs[4 * i : 4 * i + 4]
                gv = g_ref[...]
                nm = ADAM_B1 * m_ref[...] + (1.0 - ADAM_B1) * gv
                nv = ADAM_B2 * v_ref[...] + (1.0 - ADAM_B2) * (gv * gv)
                go_ref[...] = gv
                nm_ref[...] = nm
                nv_ref[...] = nv
                d_ref[...] = -ADAM_LR * ((nm / c1) / (jnp.sqrt(nv / c2) + ADAM_EPS) + ADAM_WD * w_ref[...])

        spec = pl.BlockSpec((tr, c), lambda i, b0=first // tr: (b0 + i, 0))
        args = [t for i in range(nw) for t in (w2[i], gss[i][j], m2[i], v2[i])]
        outs, res = _carry(
            body,
            comms[j],
            name="adamw",
            grid=(r // tr,),
            in_specs=[spec, pl.BlockSpec((tr, c), lambda i: (i, 0)), spec, spec] * nw + [ANY] * n_prev,
            out_specs=[spec] * (4 * nw),
            out_shape=[jax.ShapeDtypeStruct((rows, c), F32)] * (4 * nw),
            scratch_shapes=[],
            args=(*args, *outs),
            aliases={4 * nw + q: q for q in range(n_prev)},
        )
        outs = list(outs)
        carried.append(res)
    return [tuple(t.reshape(shape) for t in outs[4 * i : 4 * i + 4]) for i in range(nw)], carried


def _by_shape(arrays, fn):
    groups = {}
    for i, t in enumerate(arrays):
        groups.setdefault((t.shape, str(t.dtype)), []).append(i)
    out = [None] * len(arrays)
    for idx in groups.values():
        for i, res in zip(idx, fn(idx)):
            out[i] = res
    return out


def pair_sum(gs, recvs, c_arr):
    n = len(gs)
    _, k, r, c = gs[0].shape
    tr = _row_tile(r, c, 3 * n)

    def body(c_ref, *refs):
        for i in range(n):
            refs[2 * n + i][...] = bf(refs[i][...] + refs[n + i][...].astype(F32))

    slab = pl.BlockSpec((None, tr, c), lambda kk, i, cr: (kk, i, 0))
    return pl.pallas_call(
        body,
        name="pair_sum",
        grid_spec=pltpu.PrefetchScalarGridSpec(
            num_scalar_prefetch=1,
            grid=(k, r // tr),
            in_specs=[pl.BlockSpec((None, None, tr, c), lambda kk, i, cr: (cr[0], kk, i, 0))] * n + [slab] * n,
            out_specs=[slab] * n,
        ),
        out_shape=[jax.ShapeDtypeStruct((k, r, c), BF16)] * n,
        compiler_params=_cparams(("arbitrary", "arbitrary")),
    )(c_arr, *gs, *recvs)


def chip_sum(ps, qs, sel):
    n = len(ps)
    _, r, c = ps[0].shape
    nq = qs[0].shape[0]
    tr = _row_tile(r, c, 4 * n)

    def body(sel_ref, *refs):
        for a in range(n):
            acc = refs[a][...].astype(F32)
            for i in range(nq):
                acc = acc + refs[n + a][i].astype(F32)
            refs[2 * n + a][...] = acc

    return pl.pallas_call(
        body,
        name="chip_sum",
        grid_spec=pltpu.PrefetchScalarGridSpec(
            num_scalar_prefetch=1,
            grid=(r // tr,),
            in_specs=[pl.BlockSpec((None, tr, c), lambda i, sr: (sr[0], i, 0))] * n + [pl.BlockSpec((nq, tr, c), lambda i, sr: (0, i, 0))] * n,
            out_specs=[pl.BlockSpec((None, tr, c), lambda i, sr: (sr[1], i, 0))] * n,
        ),
        out_shape=[jax.ShapeDtypeStruct((2, r, c), F32)] * n,
        compiler_params=_cparams(("arbitrary",)),
    )(sel, *ps, *qs)


def cast_into_slot(w4s, owns, sel, dtype):
    n = len(w4s)
    _, _, r, c = w4s[0].shape
    tr = _row_tile(r, c, 2 * n)

    def body(sel_ref, *refs):
        for i in range(n):
            refs[n + i][...] = refs[i][...].astype(dtype)

    return pl.pallas_call(
        body,
        name="cast_into_slot",
        grid_spec=pltpu.PrefetchScalarGridSpec(
            num_scalar_prefetch=1,
            grid=(2, r // tr),
            in_specs=[pl.BlockSpec((None, None, tr, c), lambda hf, i, sr, g=g: (g, hf, i, 0)) for g in owns],
            out_specs=[pl.BlockSpec((None, None, tr, c), lambda hf, i, sr: (sr[0], hf, i, 0))] * n,
        ),
        out_shape=[jax.ShapeDtypeStruct((N_CHIPS, 2, r, c), dtype)] * n,
        compiler_params=_cparams(("arbitrary", "arbitrary")),
    )(sel, *w4s)


def _place():
    x, y, c = lax.axis_index("x"), lax.axis_index("y"), lax.axis_index("c")
    chips = [(1 - x, y), (x, 1 - y), (1 - x, 1 - y)]
    return x, y, c, 2 * x + y, chips, [2 * cx + cy for cx, cy in chips]


def _rcopy(src, dst, send, recv, dev):
    return pltpu.make_async_remote_copy(src_ref=src, dst_ref=dst, send_sem=send, recv_sem=recv, device_id=dev, device_id_type=MESH)


class Gather:
    def __init__(self, bufs):
        n = len(bufs)
        self.n = n
        self.args = list(bufs)
        self.out_shape = [jax.ShapeDtypeStruct(t.shape, t.dtype) for t in bufs]
        self.aliases = {a: a for a in range(n)}
        self.scratch = [pltpu.SemaphoreType.DMA((n, 6)), pltpu.SemaphoreType.DMA((n, 6))]

    def _sends(self, outs, send, recv):
        x, y, c, me, chips, _ = _place()
        cps = []
        for a in range(self.n):
            mine = outs[a].at[me, c]
            cps += [_rcopy(mine, mine, send.at[a, j], recv.at[a, j], (*chips[j], c)) for j in range(3)]
        return cps

    def start(self, ins, outs, scr):
        for cp in self._sends(outs, *scr):
            cp.start()

    def finish(self, ins, outs, scr):
        send, recv = scr
        x, y, c, me, chips, cidx = _place()
        sib = (x, y, 1 - c)
        passed = []
        for a in range(self.n):
            for j in range(3):
                landed = outs[a].at[cidx[j], c]
                _rcopy(landed, landed, send.at[a, j], recv.at[a, j], (*chips[j], c)).wait_recv()
                fwd = _rcopy(landed, landed, send.at[a, 3 + j], recv.at[a, 3 + j], sib)
                fwd.start()
                passed.append(fwd)
        for a in range(self.n):
            for j in range(3):
                theirs = outs[a].at[cidx[j], 1 - c]
                _rcopy(theirs, theirs, send.at[a, 3 + j], recv.at[a, 3 + j], sib).wait_recv()
        for cp in self._sends(outs, send, recv) + passed:
            cp.wait_send()


class PairExchange:
    def __init__(self, gs):
        n = len(gs)
        self.n = n
        self.args = list(gs)
        self.out_shape = [jax.ShapeDtypeStruct(t.shape[1:], t.dtype) for t in gs]
        self.aliases = {}
        self.scratch = [pltpu.SemaphoreType.DMA((n,)), pltpu.SemaphoreType.DMA((n,))]

    def _copies(self, ins, outs, send, recv):
        x, y, c = lax.axis_index("x"), lax.axis_index("y"), lax.axis_index("c")
        return [_rcopy(ins[a].at[1 - c], outs[a], send.at[a], recv.at[a], (x, y, 1 - c)) for a in range(self.n)]

    def start(self, ins, outs, scr):
        for cp in self._copies(ins, outs, *scr):
            cp.start()

    def finish(self, ins, outs, scr):
        for cp in self._copies(ins, outs, *scr):
            cp.wait()


class ChipExchange:
    def __init__(self, ps):
        n = len(ps)
        self.n = n
        self.args = list(ps)
        self.out_shape = [jax.ShapeDtypeStruct((3,) + t.shape[1:], t.dtype) for t in ps]
        self.aliases = {}
        self.scratch = [pltpu.SemaphoreType.DMA((n, 3)), pltpu.SemaphoreType.DMA((n, 3))]

    def _sends(self, ins, outs, send, recv):
        x, y, c, me, chips, cidx = _place()
        return [
            _rcopy(ins[a].at[cidx[j]], outs[a].at[j], send.at[a, j], recv.at[a, j], (*chips[j], c))
            for a in range(self.n)
            for j in range(3)
        ]

    def start(self, ins, outs, scr):
        for cp in self._sends(ins, outs, *scr):
            cp.start()

    def finish(self, ins, outs, scr):
        send, recv = scr
        x, y, c, me, chips, _ = _place()
        for a in range(self.n):
            for j in range(3):
                landed = outs[a].at[j]
                _rcopy(landed, landed, send.at[a, j], recv.at[a, j], (*chips[j], c)).wait_recv()
        for cp in self._sends(ins, outs, send, recv):
            cp.wait_send()


class PairShare:
    def __init__(self, bufs):
        n = len(bufs)
        self.n = n
        self.args = list(bufs)
        self.out_shape = [jax.ShapeDtypeStruct(t.shape, t.dtype) for t in bufs]
        self.aliases = {a: a for a in range(n)}
        self.scratch = [pltpu.SemaphoreType.DMA((n,)), pltpu.SemaphoreType.DMA((n,))]

    def _sends(self, outs, send, recv):
        x, y, c = lax.axis_index("x"), lax.axis_index("y"), lax.axis_index("c")
        return [_rcopy(outs[a].at[c], outs[a].at[c], send.at[a], recv.at[a], (x, y, 1 - c)) for a in range(self.n)]

    def start(self, ins, outs, scr):
        for cp in self._sends(outs, *scr):
            cp.start()

    def finish(self, ins, outs, scr):
        send, recv = scr
        x, y, c = lax.axis_index("x"), lax.axis_index("y"), lax.axis_index("c")
        for a in range(self.n):
            theirs = outs[a].at[1 - c]
            _rcopy(theirs, theirs, send.at[a], recv.at[a], (x, y, 1 - c)).wait_recv()
        for cp in self._sends(outs, send, recv):
            cp.wait_send()


class Multi:
    def __init__(self, comms):
        self.comms = comms
        self.args, self.out_shape, self.scratch, self.aliases = [], [], [], {}
        self.spans = []
        for cm in comms:
            a0, o0, s0 = len(self.args), len(self.out_shape), len(self.scratch)
            self.aliases.update({a0 + i: o0 + o for i, o in cm.aliases.items()})
            self.args += cm.args
            self.out_shape += cm.out_shape
            self.scratch += cm.scratch
            self.spans.append((slice(a0, len(self.args)), slice(o0, len(self.out_shape)), slice(s0, len(self.scratch))))

    def start(self, ins, outs, scr):
        for cm, (sa, so, ss) in zip(self.comms, self.spans):
            cm.start(ins[sa], outs[so], scr[ss])

    def finish(self, ins, outs, scr):
        for cm, (sa, so, ss) in zip(self.comms, self.spans):
            cm.finish(ins[sa], outs[so], scr[ss])

    def split(self, res):
        return [list(res[so]) for _, so, _ in self.spans]


def run_comm(comm, name):
    na, no = len(comm.args), len(comm.out_shape)

    def body(*refs):
        ins, outs, scr = refs[:na], refs[na : na + no], refs[na + no :]
        comm.start(ins, outs, scr)
        comm.finish(ins, outs, scr)

    return pl.pallas_call(
        body,
        name=name,
        in_specs=[ANY] * na,
        out_specs=[ANY] * no,
        out_shape=comm.out_shape,
        input_output_aliases=comm.aliases,
        scratch_shapes=comm.scratch,
    )(*comm.args)


def _carry(body, comm, *, name, grid, in_specs, out_specs, out_shape, scratch_shapes, args, aliases=None):
    params = _cparams(("arbitrary",))
    aliases = {} if aliases is None else aliases
    if comm is None:
        res = pl.pallas_call(body, name=name, grid=grid, in_specs=in_specs, out_specs=out_specs, out_shape=out_shape,
                             input_output_aliases=aliases, scratch_shapes=scratch_shapes, compiler_params=params)(*args)
        return res, None
    ni, no, ns = len(in_specs), len(out_specs), len(scratch_shapes)
    ci, co = len(comm.args), len(comm.out_shape)

    def wrapped(*refs):
        ins, c_ins = refs[:ni], refs[ni : ni + ci]
        p = ni + ci
        outs, c_outs = refs[p : p + no], refs[p + no : p + no + co]
        p += no + co
        scr, c_scr = refs[p : p + ns], refs[p + ns :]

        @pl.when(pl.program_id(0) == 0)
        def _():
            comm.start(c_ins, c_outs, c_scr)

        body(*ins, *outs, *scr)

        @pl.when(pl.program_id(0) == grid[0] - 1)
        def _():
            comm.finish(c_ins, c_outs, c_scr)

    res = pl.pallas_call(
        wrapped,
        name=name + "_carry",
        grid=grid,
        in_specs=list(in_specs) + [ANY] * ci,
        out_specs=list(out_specs) + [ANY] * co,
        out_shape=list(out_shape) + list(comm.out_shape),
        input_output_aliases={**aliases, **{ni + i: no + o for i, o in comm.aliases.items()}},
        scratch_shapes=list(scratch_shapes) + list(comm.scratch),
        compiler_params=params,
    )(*args, *comm.args)
    return res[:no], res[no:]


def small_allreduce(buf):
    r = buf.shape[0]

    def body(b_ref, o_ref, slots, send, recv):
        x, y, c = lax.axis_index("x"), lax.axis_index("y"), lax.axis_index("c")
        me = 4 * x + 2 * y + c
        slots[me] = b_ref[...]
        cps = []
        peers = []
        for mask in range(1, N_DEV):
            fx, fy, fc = (mask >> 2) & 1, (mask >> 1) & 1, mask & 1
            px, py, pc = (1 - x if fx else x), (1 - y if fy else y), (1 - c if fc else c)
            peers.append(4 * px + 2 * py + pc)
            cps.append(_rcopy(b_ref, slots.at[me], send.at[mask - 1], recv.at[mask - 1], (px, py, pc)))
        for cp in cps:
            cp.start()
        for k, pid in enumerate(peers):
            landed = slots.at[pid]
            _rcopy(landed, landed, send.at[k], recv.at[k], (x, y, c)).wait_recv()
        for cp in cps:
            cp.wait_send()
        acc = slots[0]
        for i in range(1, N_DEV):
            acc = acc + slots[i]
        o_ref[...] = acc

    vm = pl.BlockSpec(memory_space=pltpu.VMEM)
    return pl.pallas_call(
        body,
        name="small_allreduce",
        in_specs=[vm],
        out_specs=vm,
        out_shape=jax.ShapeDtypeStruct(buf.shape, F32),
        scratch_shapes=[pltpu.VMEM((N_DEV, r, LANES), F32), pltpu.SemaphoreType.DMA((N_DEV - 1,)), pltpu.SemaphoreType.DMA((N_DEV - 1,))],
    )(buf)


WEIGHT_NAMES = ["ffn1_norm", "ffn1_w_gate", "ffn1_w_up", "ffn1_w_down", "mix_norm", "mem_norm", "w_mem_kv", "mem_q_gain",
                "mem_k_gain", "w_in_a", "hgrn_lb_logits", "hgrn_o_gain", "w_in_b", "fox_q_gain", "kv_norm", "w_kv", "fox_f_bias",
                "fox_k_gain", "w_out", "ffn2_norm", "ffn2_w_gate", "ffn2_w_up", "ffn2_w_down"]
SHARDED = ["ffn1_w_gate", "ffn1_w_up", "ffn1_w_down", "w_mem_kv", "w_in_a", "w_in_b", "w_kv", "w_out", "ffn2_w_gate", "ffn2_w_up", "ffn2_w_down"]
SMALL = [n for n in WEIGHT_NAMES if n not in SHARDED]
FFN1 = ["ffn1_w_gate", "ffn1_w_up", "ffn1_w_down"]
FFN2 = ["ffn2_w_gate", "ffn2_w_up", "ffn2_w_down"]
PER_LAYER = FFN1 + FFN2 + ["w_mem_kv", "w_out"]
TRANSPOSED = ["ffn1_w_gate", "ffn1_w_up", "ffn2_w_gate", "ffn2_w_up", "w_in_a", "w_in_b"]
N_LAYERS, N_A = 4, 2
KV_PAD = 13 * LANES


def _cols_from_chips(g):
    return jnp.moveaxis(g, 0, 2).reshape(g.shape[1], g.shape[2], N_CHIPS * g.shape[3])


def _rows_from_chips(g):
    return jnp.moveaxis(g, 0, 1).reshape(g.shape[1], N_CHIPS * g.shape[2], g.shape[3])


def _pair_tile(g):
    return jnp.tile(g, (1, 2)).reshape(g.shape[0], 1, LANES)


def _pair_fold(g):
    return g[:, :HEAD64] + g[:, HEAD64:]


def kernel(x, mem, ffn1_norm, ffn1_w_gate, ffn1_w_up, ffn1_w_down, mix_norm, mem_norm, w_mem_kv, mem_q_gain, mem_k_gain, w_in_a, hgrn_lb_logits, hgrn_o_gain, w_in_b, fox_q_gain, kv_norm, w_kv, fox_f_bias, fox_k_gain, w_out, ffn2_norm, ffn2_w_gate, ffn2_w_up, ffn2_w_down, loss_target, m_ffn1_norm, m_ffn1_w_gate, m_ffn1_w_up, m_ffn1_w_down, m_mix_norm, m_mem_norm, m_w_mem_kv, m_mem_q_gain, m_mem_k_gain, m_w_in_a, m_hgrn_lb_logits, m_hgrn_o_gain, m_w_in_b, m_fox_q_gain, m_kv_norm, m_w_kv, m_fox_f_bias, m_fox_k_gain, m_w_out, m_ffn2_norm, m_ffn2_w_gate, m_ffn2_w_up, m_ffn2_w_down, v_ffn1_norm, v_ffn1_w_gate, v_ffn1_w_up, v_ffn1_w_down, v_mix_norm, v_mem_norm, v_w_mem_kv, v_mem_q_gain, v_mem_k_gain, v_w_in_a, v_hgrn_lb_logits, v_hgrn_o_gain, v_w_in_b, v_fox_q_gain, v_kv_norm, v_w_kv, v_fox_f_bias, v_fox_k_gain, v_w_out, v_ffn2_norm, v_ffn2_w_gate, v_ffn2_w_up, v_ffn2_w_down):
    given = dict(locals())
    def oriented(n, t):
        return jnp.swapaxes(t, 1, 2) if n in TRANSPOSED else t

    w = {n: oriented(n, given[n]) for n in WEIGHT_NAMES}
    xs, mems, tgt = x[0], mem[0], loss_target[0]
    s, d = xs.shape
    my_chip = 2 * lax.axis_index("x") + lax.axis_index("y")
    sel = jnp.stack([my_chip, lax.axis_index("c")]).astype(jnp.int32)
    c_arr = sel[1:]

    def w_in_name(l):
        return "w_in_a" if l < N_A else "w_in_b"

    def halves_of(n):
        rows, cols = w[n].shape[-2:]
        return w[n].reshape(-1, 2, rows // 2, cols)

    def own_of(n, l):
        return 0 if w[n].ndim == 2 else (l - N_A if n == "w_in_b" else l)

    def view(buf, n):
        rows, cols = w[n].shape[-2:]
        return buf.reshape(N_CHIPS, rows, cols) if w[n].ndim == 2 else buf.reshape(N_CHIPS, 1, rows, cols)

    def mixer(l):
        return [(w_in_name(l), l), ("w_mem_kv", l), ("w_out", l)]

    first = [(n, 0) for n in PER_LAYER] + [("w_in_a", 0), ("w_kv", 0)]
    carried = {
        (0, "ffn1"): mixer(1), (0, "mix"): [(n, 1) for n in FFN1 + FFN2[:2]], (0, "ffn2"): [(FFN2[2], 1)],
        (1, "ffn1"): mixer(2), (1, "mix"): [(n, 2) for n in FFN1 + FFN2[:2]], (1, "ffn2"): [(FFN2[2], 2)],
        (2, "ffn1"): [(FFN1[0], 3)], (2, "mix"): [(FFN1[1], 3), (FFN1[2], 3)], (2, "ffn2"): [(FFN2[0], 3)] + mixer(3),
        (3, "ffn1"): [(FFN2[1], 3)], (3, "mix"): [(FFN2[2], 3)],
    }
    bufs = {}
    every = first + [it for items in carried.values() for it in items]
    for layer in range(N_LAYERS):
        its = [it for it in every if it[1] == layer]
        srcs = [halves_of(n) for n, _ in its]
        cast = lambda idx: cast_into_slot([srcs[i] for i in idx], [own_of(*its[i]) for i in idx], sel, BF16)
        bufs.update(zip(its, _by_shape(srcs, cast)))
    lb_buf = cast_into_slot([hgrn_lb_logits.reshape(1, 2, 1, -1)], [0], sel, F32)[0]
    got0 = run_comm(Gather([bufs[it] for it in first] + [lb_buf]), "gather_layer0")
    got = {it: view(b, it[0]) for it, b in zip(first, got0[:-1])}
    w_kv_full = _cols_from_chips(got[("w_kv", 0)][:, None])
    w_kv_full = jnp.pad(w_kv_full, ((0, 0), (0, 0), (0, KV_PAD - w_kv_full.shape[-1])))
    logits3 = jnp.moveaxis(got0[-1].reshape(N_CHIPS, 2, -1), 0, 1).reshape(2, 1, -1)
    lb3 = lb_fwd(logits3)
    w_in, w_mkv, w_o = {}, {}, {}

    def gather_behind(key):
        items = carried.get(key)
        return None if items is None else Gather([bufs[it] for it in items])

    def landed(key, res):
        if res is not None:
            got.update({it: view(b, it[0]) for it, b in zip(carried[key], res)})

    norm3 = {n: w[n].reshape(N_LAYERS, 1, d) for n in ("ffn1_norm", "mix_norm", "mem_norm", "ffn2_norm")}
    kvn3 = kv_norm.reshape(1, 1, d)
    mqg3, mkg3 = _pair_tile(mem_q_gain), _pair_tile(mem_k_gain)
    og3 = hgrn_o_gain.reshape(N_A, 1, LANES)
    fqg3 = _pair_tile(fox_q_gain)
    fkg = jnp.tile(fox_k_gain, 2).reshape(1, LANES)
    fb = jnp.pad(fox_f_bias, (0, LANES - fox_f_bias.shape[0])).reshape(1, LANES)

    sv = [dict() for _ in range(N_LAYERS)]
    h = xs
    kv = None
    for l in range(N_LAYERS):
        t = sv[l]
        t["x0"] = h
        (h, t["a1"], t["b1"]), res = ffn_fwd(h, norm3["ffn1_norm"], l, *[got[(n, l)] for n in FFN1], 0, comm=gather_behind((l, "ffn1")))
        landed((l, "ffn1"), res)
        t["x1"] = h
        w_in[l] = _rows_from_chips(got[(w_in_name(l), l)])
        t["proj"] = proj_fwd(h, norm3["mix_norm"], l, w_in[l], 0, wt=True)
        if l < N_A:
            (main, t["o"]), res = hgrn_fwd(t["proj"], lb3, og3, l, comm=gather_behind((l, "mix")))
            t["qblk"] = 12
        else:
            (main, t["o"], t["lse"]), res = fox_fwd(t["proj"], kv["k"], kv["v"], kv["clf"], kv["clf_t"], fqg3, l - N_A, comm=gather_behind((l, "mix")))
            t["qblk"] = 6
        landed((l, "mix"), res)
        w_mkv[l], w_o[l] = _rows_from_chips(got[("w_mem_kv", l)]), _rows_from_chips(got[("w_out", l)])
        t["kvm"] = proj_fwd(mems, norm3["mem_norm"], l, w_mkv[l], 0)
        memo = memattn_fwd(t["proj"], t["qblk"], t["kvm"], mqg3, mkg3, l)
        t["mixed"] = jnp.concatenate([main, memo], axis=-1)
        h = mm_res(h, t["mixed"], w_o[l], 0)
        t["x2"] = h
        (h, t["a2"], t["b2"]), res = ffn_fwd(h, norm3["ffn2_norm"], l, *[got[(n, l)] for n in FFN2], 0, comm=gather_behind((l, "ffn2")))
        landed((l, "ffn2"), res)
        if l == N_A - 1:
            kv = {"x": h, "kvf": proj_fwd(h, kvn3, 0, w_kv_full, 0)}
            kv["k"], kv["v"], kv["clf"] = kvprep_fwd(kv["kvf"], fkg, fb)
            kv["clf_t"] = kv["clf"][:, :16].T

    loss_local, dx = loss_head(h, tgt)

    nc = N_CHIPS
    fc = ffn1_w_down.shape[1]
    gsplit = [dict() for _ in range(N_LAYERS)]

    def group_layout(l):
        lay = {n: b[0].reshape(2, nc, fc // 2, d) for n, b in gsplit[l].items()}
        lay16 = {n: b[1].reshape(2, nc, fc // 2, d) for n, b in gsplit[l].items()}
        for n, (g32, g16) in (("w_mem_kv", dw_mkv[l]), ("w_out", dw_o[l]), (w_in_name(l), dw_in[l])):
            lay[n], lay16[n] = g32, g16
        names = PER_LAYER + [w_in_name(l)]
        if l == N_A - 1:
            kv_cols = w_kv.shape[-1] * nc
            lay["w_kv"] = jnp.transpose(dw_kv[:, :kv_cols].reshape(2, d // 2, nc, kv_cols // nc), (0, 2, 1, 3))
            names = names + ["w_kv"]
        return names, [lay[n] for n in names], [lay16[n] if n in lay16 else bf(lay[n]) for n in names]

    def pair_sums(gl, recv):
        return _by_shape(gl, lambda idx: pair_sum([gl[i] for i in idx], [recv[i] for i in idx], c_arr))

    def chip_sums(ps, qs):
        return _by_shape(ps, lambda idx: chip_sum([ps[i] for i in idx], [qs[i] for i in idx], sel))

    n_ffn = len(FFN1) + len(FFN2)
    riding = {l: l + 1 for l in range(N_LAYERS - 1)}
    reduced = {}
    unshared = None
    dw_in, dw_o, dw_mkv = [None] * N_LAYERS, [None] * N_LAYERS, [None] * N_LAYERS
    sg = {n: [None] * N_LAYERS for n in ("ffn1_norm", "mix_norm", "mem_norm", "ffn2_norm", "mem_q_gain", "mem_k_gain")}
    sg["hgrn_o_gain"], sg["fox_q_gain"], dlb = [None] * N_A, [None] * (N_LAYERS - N_A), [None] * N_A
    dk_sh = jnp.zeros((s, KV_MAIN), F32)
    dv_sh = jnp.zeros((s, KV_MAIN), F32)
    dclf = jnp.zeros((s, LANES), F32)
    zero_mem = jnp.zeros(mems.shape, F32)
    dw_kv = None
    for l in reversed(range(N_LAYERS)):
        t = sv[l]
        if l == N_A - 1:
            dkvf, dfkg, dfb = kvprep_bwd(kv["kvf"], fkg, fb, dk_sh, dv_sh, dclf)
            dx, sg["kv_norm"], xn_kv, dpb = proj_bwd(kv["x"], kvn3, 0, [dkvf], w_kv_full, 0, dx)
            dw_kv = wgrad(xn_kv, dpb)
        ride = riding.get(l)
        comms = []
        if unshared is not None:
            comms.append(PairShare(unshared[2]))
        if ride is not None:
            names_r, gl_r, gl16_r = group_layout(ride)
            comms.append(PairExchange(gl16_r))
        comm = Multi(comms) if comms else None
        (dx, da, db, hm, xn, dyb, sg["ffn2_norm"][l]), res = ffn_bwd(t["x2"], norm3["ffn2_norm"], l, dx, t["a2"], t["b2"], *[got[(n, l)] for n in FFN2], 0, comm=comm)
        if comm is not None:
            res = comm.split(res)
            if unshared is not None:
                reduced[unshared[0]] = dict(zip(unshared[1], res.pop(0)))
                unshared = None
            if ride is not None:
                partial_r = pair_sums(gl_r, res.pop(0))

        def ffn_wgrads(which, da, db, hm, xn, dyb):
            for n, (a_, b_) in zip(which, ((da, xn), (db, xn), (hm, dyb))):
                gsplit[l][n] = wgrad(a_, b_, split=True)

        ffn_wgrads(FFN2, da, db, hm, xn, dyb)
        dmixed, dxb = mm_nt(dx, w_o[l], 0)
        dw_o[l] = wgrad(t["mixed"], dxb, chip_rows=w_out.shape[1])
        dqm, dkvm, dmq, dmk = memattn_bwd(t["proj"], t["qblk"], t["kvm"], mqg3, mkg3, l, dmixed)
        sg["mem_q_gain"][l], sg["mem_k_gain"][l] = _pair_fold(dmq), _pair_fold(dmk)
        _, sg["mem_norm"][l], memn, dkvmb = proj_bwd(mems, norm3["mem_norm"], l, [dkvm], w_mkv[l], 0, zero_mem)
        dw_mkv[l] = wgrad(memn, dkvmb, chip_rows=w_mem_kv.shape[1])
        comm = ChipExchange(partial_r[:n_ffn]) if ride is not None else None
        if l < N_A:
            (dzq, dzf, dvi, dzg, dlb[l], sg["hgrn_o_gain"][l]), res = hgrn_bwd(t["proj"], lb3, og3, l, t["o"], dmixed, comm=comm)
            parts, tmw = [dzq, dzf, dvi, dzg, dqm], 13 * LANES
        else:
            lse_t = t["lse"].reshape(s, 6, LANES)[:, :, :2].reshape(s, 12).T
            lse_t = jnp.pad(lse_t, ((0, 4), (0, 0)))
            (dq, dgate, dk_sh, dv_sh, dclf, dfq), res = fox_bwd(t["proj"], kv["k"], kv["v"], kv["clf"], kv["clf_t"], fqg3, l - N_A, t["o"], t["lse"], lse_t, dmixed, dk_sh, dv_sh, dclf, comm=comm)
            sg["fox_q_gain"][l - N_A] = _pair_fold(dfq)
            parts, tmw = [dq, dgate, dqm], 7 * LANES
        if ride is not None:
            landed_r = list(res)
        dx, sg["mix_norm"][l], hn, dpb = proj_bwd(t["x1"], norm3["mix_norm"], l, parts, w_in[l], 0, dx, wt=True)
        dw_in[l] = wgrad(dpb, hn, tn=d // 2, tm=tmw, chip_rows=tmw // 2)
        comm = ChipExchange(partial_r[n_ffn:]) if ride is not None else None
        (dx, da, db, hm, xn, dyb, sg["ffn1_norm"][l]), res = ffn_bwd(t["x0"], norm3["ffn1_norm"], l, dx, t["a1"], t["b1"], *[got[(n, l)] for n in FFN1], 0, comm=comm)
        if ride is not None:
            unshared = (ride, names_r, chip_sums(partial_r, landed_r + list(res)))
        ffn_wgrads(FFN1, da, db, hm, xn, dyb)

    names0, gl0, gl16_0 = group_layout(0)
    tail = Multi([PairShare(unshared[2]), PairExchange(gl16_0)])
    shared1, recv0 = tail.split(run_comm(tail, "pair_exchange"))
    reduced[unshared[0]] = dict(zip(unshared[1], shared1))
    partial0 = pair_sums(gl0, recv0)
    ffn = FFN1 + FFN2
    upper = list(range(1, N_LAYERS))
    pieces = [partial0[:2], partial0[2:4], partial0[4:]]
    ffn_state = [[w[n] for n in ffn], None, [oriented(n, given["m_" + n]) for n in ffn], [oriented(n, given["v_" + n]) for n in ffn]]
    ffn_sizes = [ffn_state[0][0].shape[1]] * N_LAYERS
    ffn_upper, landed_parts = adamw(ffn_state[0], [[reduced[l][n] for l in upper] for n in ffn], ffn_state[2], ffn_state[3],
                                    sizes=ffn_sizes, which=upper, comms=[ChipExchange(p) for p in pieces])
    landed0 = [q for part in landed_parts for q in part]
    mine0 = chip_sums(partial0, landed0)
    reduced[0] = dict(zip(names0, run_comm(PairShare(mine0), "pair_share")))
    ffn_done, _ = adamw(ffn_state[0], [[reduced[0][n]] for n in ffn], ffn_state[2], ffn_state[3], sizes=ffn_sizes, which=[0], prev=ffn_upper)
    gparts = {n: [reduced[l][n] for l in range(N_LAYERS)] for n in PER_LAYER}
    gparts["w_in_a"] = [reduced[l]["w_in_a"] for l in range(N_A)]
    gparts["w_in_b"] = [reduced[l]["w_in_b"] for l in range(N_A, N_LAYERS)]
    gparts["w_kv"] = [reduced[N_A - 1]["w_kv"]]

    dlogits = lb_bwd(logits3, dlb[1]).reshape(2, -1)
    small = {
        "ffn1_norm": jnp.concatenate(sg["ffn1_norm"]), "mix_norm": jnp.concatenate(sg["mix_norm"]),
        "mem_norm": jnp.concatenate(sg["mem_norm"]), "ffn2_norm": jnp.concatenate(sg["ffn2_norm"]),
        "mem_q_gain": jnp.concatenate(sg["mem_q_gain"]), "mem_k_gain": jnp.concatenate(sg["mem_k_gain"]),
        "hgrn_o_gain": jnp.concatenate(sg["hgrn_o_gain"]), "fox_q_gain": jnp.concatenate(sg["fox_q_gain"]),
        "kv_norm": sg["kv_norm"], "fox_f_bias": dfb[:, : fox_f_bias.shape[0]], "fox_k_gain": _pair_fold(dfkg),
        "hgrn_lb_logits": dlogits,
    }
    flat = [small[n].reshape(-1) for n in SMALL] + [loss_local.reshape(-1)]
    sizes = [f.shape[0] for f in flat]
    total = sum(sizes)
    padded = -(-total // (8 * LANES)) * (8 * LANES)
    packed = jnp.pad(jnp.concatenate(flat), (0, padded - total)).reshape(-1, LANES)
    summed = small_allreduce(packed).reshape(-1)
    off = 0
    for n, sz in zip(SMALL, sizes[:-1]):
        gparts[n] = [summed[off : off + sz].reshape(dlogits.shape if n == "hgrn_lb_logits" else w[n].shape)]
        off += sz
    loss = summed[off]
    lbw = hgrn_lb_logits.shape[1]
    gparts["hgrn_lb_logits"] = [lax.dynamic_slice_in_dim(gparts["hgrn_lb_logits"][0], my_chip * lbw, lbw, axis=1)]

    rest = [n for n in WEIGHT_NAMES if n not in ffn]

    def update(idx):
        ns = [rest[i] for i in idx]
        return adamw([w[n] for n in ns], [gparts[n] for n in ns], [oriented(n, given["m_" + n]) for n in ns],
                     [oriented(n, given["v_" + n]) for n in ns])[0]

    grads, delta, new_m, new_v = {}, {}, {}, {}
    updated = list(zip(ffn, ffn_done)) + list(zip(rest, _by_shape([w[n] for n in rest], update)))
    for n, res in updated:
        grads[n], delta[n], new_m[n], new_v[n] = (oriented(n, t) for t in res)
    return (loss, dx[None], *[grads[n] for n in WEIGHT_NAMES], *[delta[n] for n in WEIGHT_NAMES],
            *[new_m[n] for n in WEIGHT_NAMES], *[new_v[n] for n in WEIGHT_NAMES])
```

```python
import jax
import jax.numpy as jnp
from jax import lax
from jax.experimental import pallas as pl
from jax.experimental.pallas import tpu as pltpu

F32, BF16 = jnp.float32, jnp.bfloat16
HI = lax.Precision.HIGHEST
EPS = 1e-6
MESH = pl.DeviceIdType.MESH
ANY = pl.BlockSpec(memory_space=pl.ANY)

VMEM_LIMIT_BYTES = 56 << 20
N_CHIPS = 4
N_DEV = 8
LANES = 128
HEAD64 = 64
CHUNK = 64
SUB = 32
HGRN_HEADS_PER_STEP = 2
TQ = 256
TOK = 256

ADAM_LR, ADAM_B1, ADAM_B2, ADAM_EPS, ADAM_WD, ADAM_STEP = 0.001, 0.9, 0.999, 1e-08, 0.01, 10


def _cparams(sem=None, **kw):
    return pltpu.CompilerParams(dimension_semantics=sem, vmem_limit_bytes=VMEM_LIMIT_BYTES, **kw)


def _mm(a, b, dims, prec=None):
    return lax.dot_general(a, b, (dims, ((), ())), preferred_element_type=F32, precision=prec)


def dot_nn(a, b, prec=None):
    return _mm(a, b, ((1,), (0,)), prec)


def dot_nt(a, b, prec=None):
    return _mm(a, b, ((1,), (1,)), prec)


def dot_tn(a, b, prec=None):
    return _mm(a, b, ((0,), (0,)), prec)


def bf(v):
    return v.astype(BF16)


def _sigmoid(z):
    return jax.nn.sigmoid(z)


def _dsilu(z, s):
    return s * (1.0 + z * (1.0 - s))


def _rms(x):
    r = lax.rsqrt(jnp.mean(x * x, axis=-1, keepdims=True) + EPS)
    return x * r, r


def _rms_bwd(dxn, u, r, g):
    du = dxn * g
    dx = r * (du - u * jnp.mean(du * u, axis=-1, keepdims=True))
    return dx, jnp.sum(dxn * u, axis=0, keepdims=True)


def _lane_mask0(shape):
    return lax.broadcasted_iota(jnp.int32, shape, len(shape) - 1) < HEAD64


def _rms64(x, m0):
    sq = x * x
    s0 = jnp.sum(jnp.where(m0, sq, 0.0), axis=-1, keepdims=True)
    s1 = jnp.sum(jnp.where(m0, 0.0, sq), axis=-1, keepdims=True)
    r = lax.rsqrt(jnp.where(m0, s0, s1) * (1.0 / HEAD64) + EPS)
    return x * r, r


def _rms64_bwd(dxn, u, r, g, m0):
    du = dxn * g
    t = du * u
    t0 = jnp.sum(jnp.where(m0, t, 0.0), axis=-1, keepdims=True)
    t1 = jnp.sum(jnp.where(m0, 0.0, t), axis=-1, keepdims=True)
    dx = r * (du - u * (jnp.where(m0, t0, t1) * (1.0 / HEAD64)))
    return dx, jnp.sum(dxn * u, axis=0, keepdims=True)


def _tok(s):
    return TOK if s % TOK == 0 else s


def _const(shape):
    return pl.BlockSpec(shape, lambda *_: (0,) * len(shape))


def ffn_fwd(x, gain3, l, wg, wu, wd, wl, comm=None):
    s, d = x.shape
    nc, _, fc, _ = wg.shape
    tm = _tok(s)

    def body(x_ref, g_ref, wg_ref, wu_ref, wd_ref, xo_ref, a_ref, b_ref):
        xv = x_ref[...]
        u, _ = _rms(xv)
        xn = bf(u * g_ref[...])
        y = jnp.zeros((tm, d), F32)
        for c in range(nc):
            a = dot_nt(xn, wg_ref[c])
            b = dot_nt(xn, wu_ref[c])
            a_ref[c] = bf(a)
            b_ref[c] = bf(b)
            y = y + dot_nn(bf(a * _sigmoid(a) * b), wd_ref[c])
        xo_ref[...] = xv + 0.5 * y

    wspec = pl.BlockSpec((nc, None, fc, d), lambda i: (0, wl, 0, 0), pipeline_mode=pl.Buffered(1))
    wdspec = pl.BlockSpec((nc, None, fc, d), lambda i: (0, wl, 0, 0), pipeline_mode=pl.Buffered(1))
    row = pl.BlockSpec((tm, d), lambda i: (i, 0))
    act = pl.BlockSpec((nc, tm, fc), lambda i: (0, i, 0))
    return _carry(
        body,
        comm,
        name="ffn_fwd",
        grid=(s // tm,),
        in_specs=[row, pl.BlockSpec((None, 1, d), lambda i: (l, 0, 0)), wspec, wspec, wdspec],
        out_specs=[row, act, act],
        out_shape=[
            jax.ShapeDtypeStruct((s, d), F32),
            jax.ShapeDtypeStruct((nc, s, fc), BF16),
            jax.ShapeDtypeStruct((nc, s, fc), BF16),
        ],
        scratch_shapes=[],
        args=(x, gain3, wg, wu, wd),
    )


def ffn_bwd(x, gain3, l, dout, a, b, wg, wu, wd, wl, comm=None):
    s, d = x.shape
    nc, _, fc, _ = wg.shape
    tm = _tok(s)

    def body(x_ref, g_ref, do_ref, a_ref, b_ref, wg_ref, wu_ref, wd_ref, dx_ref, da_ref, db_ref, hm_ref, xn_ref, dy_ref, dg_ref):
        xv = x_ref[...]
        g = g_ref[...]
        u, r = _rms(xv)
        xn_ref[...] = bf(u * g)
        dout = do_ref[...]
        dy = bf(0.5 * dout)
        dy_ref[...] = dy
        dxn = jnp.zeros((tm, d), F32)
        for c in range(nc):
            av = a_ref[c].astype(F32)
            bv = b_ref[c].astype(F32)
            sg = _sigmoid(av)
            sl = av * sg
            dh = dot_nt(dy, wd_ref[c])
            da = bf(dh * bv * _dsilu(av, sg))
            db = bf(dh * sl)
            da_ref[c] = da
            db_ref[c] = db
            hm_ref[c] = bf(sl * bv)
            dxn = dxn + dot_nn(da, wg_ref[c]) + dot_nn(db, wu_ref[c])
        dx, dg = _rms_bwd(dxn, u, r, g)
        dx_ref[...] = dout + dx

        @pl.when(pl.program_id(0) == 0)
        def _():
            dg_ref[...] = jnp.zeros_like(dg_ref)

        dg_ref[...] += dg

    wspec = pl.BlockSpec((nc, None, fc, d), lambda i: (0, wl, 0, 0), pipeline_mode=pl.Buffered(1))
    wdspec = pl.BlockSpec((nc, None, fc, d), lambda i: (0, wl, 0, 0), pipeline_mode=pl.Buffered(1))
    row = pl.BlockSpec((tm, d), lambda i: (i, 0))
    act = pl.BlockSpec((nc, tm, fc), lambda i: (0, i, 0))
    act_shape = jax.ShapeDtypeStruct((nc, s, fc), BF16)
    return _carry(
        body,
        comm,
        name="ffn_bwd",
        grid=(s // tm,),
        in_specs=[row, pl.BlockSpec((None, 1, d), lambda i: (l, 0, 0)), row, act, act, wspec, wspec, wdspec],
        out_specs=[row, act, act, act, row, row, _const((1, d))],
        out_shape=[
            jax.ShapeDtypeStruct((s, d), F32),
            act_shape,
            act_shape,
            act_shape,
            jax.ShapeDtypeStruct((s, d), BF16),
            jax.ShapeDtypeStruct((s, d), BF16),
            jax.ShapeDtypeStruct((1, d), F32),
        ],
        scratch_shapes=[],
        args=(x, gain3, dout, a, b, wg, wu, wd),
    )


def wgrad(a, b, tn=None, tm=None, split=False, chip_rows=None):
    ca = a.shape[0] if a.ndim == 3 else 1
    cb = b.shape[0] if b.ndim == 3 else 1
    nc = max(ca, cb)
    s, m = a.shape[-2:]
    n = b.shape[-1]
    tn = n if tn is None else tn
    assert n % tn == 0
    tm = m if tm is None else tm
    per_tile = None if chip_rows is None else tm // chip_rows

    def body(*refs):
        a_ref, b_ref = refs[0], refs[1]
        res = dot_tn(a_ref[...], b_ref[...])
        if split:
            for o in refs[2:]:
                o[0] = res[: m // 2].astype(o.dtype)
                o[1] = res[m // 2 :].astype(o.dtype)
        elif chip_rows is not None:
            hr = chip_rows // 2
            for o in refs[2:]:
                for k in range(per_tile):
                    for hf in range(2):
                        r0 = k * chip_rows + hf * hr
                        o[hf, k] = res[r0 : r0 + hr].astype(o.dtype)
        else:
            refs[2][...] = res

    params = _cparams(("arbitrary", "arbitrary"))
    if not split:
        assert nc == 1 and a.ndim == 2 and b.ndim == 2 and m % tm == 0
        in_specs = [pl.BlockSpec((s, tm), lambda i, j: (0, i)), pl.BlockSpec((s, tn), lambda i, j: (0, j))]
        if chip_rows is None:
            return pl.pallas_call(
                body,
                name="wgrad",
                grid=(m // tm, n // tn),
                in_specs=in_specs,
                out_specs=pl.BlockSpec((tm, tn), lambda i, j: (i, j)),
                out_shape=jax.ShapeDtypeStruct((m, n), F32),
                compiler_params=params,
            )(a, b)
        assert tm % chip_rows == 0
        laid = pl.BlockSpec((2, per_tile, chip_rows // 2, tn), lambda i, j: (0, i, 0, j))
        shape = (2, m // chip_rows, chip_rows // 2, n)
        return pl.pallas_call(
            body,
            name="wgrad_chips",
            grid=(m // tm, n // tn),
            in_specs=in_specs,
            out_specs=[laid, laid],
            out_shape=[jax.ShapeDtypeStruct(shape, F32), jax.ShapeDtypeStruct(shape, BF16)],
            compiler_params=params,
        )(a, b)
    a_spec = pl.BlockSpec((None, s, m), lambda c, j: (c, 0, 0)) if a.ndim == 3 else pl.BlockSpec((s, m), lambda c, j: (0, 0))
    b_spec = pl.BlockSpec((None, s, tn), lambda c, j: (c, 0, j)) if b.ndim == 3 else pl.BlockSpec((s, tn), lambda c, j: (0, j))
    halves = pl.BlockSpec((2, None, None, m // 2, tn), lambda c, j: (0, c, 0, 0, j))
    return pl.pallas_call(
        body,
        name="wgrad_split",
        grid=(nc, n // tn),
        in_specs=[a_spec, b_spec],
        out_specs=[halves, halves],
        out_shape=[jax.ShapeDtypeStruct((2, nc, 1, m // 2, n), F32), jax.ShapeDtypeStruct((2, nc, 1, m // 2, n), BF16)],
        compiler_params=params,
    )(a, b)


def proj_fwd(x, gain3, l, w, wl, wt=False):
    s, d = x.shape
    n = w.shape[1] if wt else w.shape[2]
    tm = _tok(s)

    def body(x_ref, g_ref, w_ref, o_ref):
        u, _ = _rms(x_ref[...])
        xn = bf(u * g_ref[...])
        o_ref[...] = dot_nt(xn, w_ref[...]) if wt else dot_nn(xn, w_ref[...])

    return pl.pallas_call(
        body,
        name="proj_fwd",
        grid=(s // tm,),
        in_specs=[
            pl.BlockSpec((tm, d), lambda i: (i, 0)),
            pl.BlockSpec((None, 1, d), lambda i: (l, 0, 0)),
            pl.BlockSpec((None,) + w.shape[1:], lambda i: (wl, 0, 0)),
        ],
        out_specs=pl.BlockSpec((tm, n), lambda i: (i, 0)),
        out_shape=jax.ShapeDtypeStruct((s, n), F32),
        compiler_params=_cparams(("arbitrary",)),
    )(x, gain3, w)


def proj_bwd(x, gain3, l, parts, w, wl, dx_in, wt=False):
    s, d = x.shape
    n = w.shape[1] if wt else w.shape[2]
    widths = [p.shape[1] for p in parts]
    assert sum(widths) == n
    tm = _tok(s)
    npart = len(parts)

    def body(*refs):
        x_ref, g_ref, w_ref, dxin_ref = refs[:4]
        p_refs = refs[4 : 4 + npart]
        dx_ref, dg_ref, xn_ref, dpb_ref = refs[4 + npart :]
        g = g_ref[...]
        u, r = _rms(x_ref[...])
        xn_ref[...] = bf(u * g)
        dxn = jnp.zeros((tm, d), F32)
        off = 0
        for p_ref, wd_ in zip(p_refs, widths):
            dp = bf(p_ref[...])
            dpb_ref[:, off : off + wd_] = dp
            dxn = dxn + (dot_nn(dp, w_ref[off : off + wd_, :]) if wt else dot_nt(dp, w_ref[:, off : off + wd_]))
            off += wd_
        dx, dg = _rms_bwd(dxn, u, r, g)
        dx_ref[...] = dxin_ref[...] + dx

        @pl.when(pl.program_id(0) == 0)
        def _():
            dg_ref[...] = jnp.zeros_like(dg_ref)

        dg_ref[...] += dg

    row = pl.BlockSpec((tm, d), lambda i: (i, 0))
    return pl.pallas_call(
        body,
        name="proj_bwd",
        grid=(s // tm,),
        in_specs=[row, pl.BlockSpec((None, 1, d), lambda i: (l, 0, 0)), pl.BlockSpec((None,) + w.shape[1:], lambda i: (wl, 0, 0)), row]
        + [pl.BlockSpec((tm, wd_), lambda i: (i, 0)) for wd_ in widths],
        out_specs=[row, _const((1, d)), row, pl.BlockSpec((tm, n), lambda i: (i, 0))],
        out_shape=[
            jax.ShapeDtypeStruct((s, d), F32),
            jax.ShapeDtypeStruct((1, d), F32),
            jax.ShapeDtypeStruct((s, d), BF16),
            jax.ShapeDtypeStruct((s, n), BF16),
        ],
        compiler_params=_cparams(("arbitrary",)),
    )(x, gain3, w, dx_in, *parts)


def mm_res(x, a, w, l):
    s, d = x.shape
    k = a.shape[1]
    tm = _tok(s)

    def body(x_ref, a_ref, w_ref, o_ref):
        o_ref[...] = x_ref[...] + dot_nn(a_ref[...], w_ref[...])

    return pl.pallas_call(
        body,
        name="mm_res",
        grid=(s // tm,),
        in_specs=[
            pl.BlockSpec((tm, d), lambda i: (i, 0)),
            pl.BlockSpec((tm, k), lambda i: (i, 0)),
            pl.BlockSpec((None, k, d), lambda i: (l, 0, 0)),
        ],
        out_specs=pl.BlockSpec((tm, d), lambda i: (i, 0)),
        out_shape=jax.ShapeDtypeStruct((s, d), F32),
        compiler_params=_cparams(("arbitrary",)),
    )(x, a, w)


def mm_nt(dx, w, l):
    s, d = dx.shape
    k = w.shape[1]
    tm = _tok(s)

    def body(dx_ref, w_ref, o_ref, dxb_ref):
        dxb = bf(dx_ref[...])
        dxb_ref[...] = dxb
        o_ref[...] = dot_nt(dxb, w_ref[...])

    return pl.pallas_call(
        body,
        name="mm_nt",
        grid=(s // tm,),
        in_specs=[pl.BlockSpec((tm, d), lambda i: (i, 0)), pl.BlockSpec((None, k, d), lambda i: (l, 0, 0))],
        out_specs=[pl.BlockSpec((tm, k), lambda i: (i, 0)), pl.BlockSpec((tm, d), lambda i: (i, 0))],
        out_shape=[jax.ShapeDtypeStruct((s, k), F32), jax.ShapeDtypeStruct((s, d), BF16)],
        compiler_params=_cparams(("arbitrary",)),
    )(dx, w)


def lb_fwd(logits3):
    def body(l_ref, o_ref):
        l0, l1 = l_ref[0], l_ref[1]
        m = jnp.maximum(l0, l1)
        e0, e1 = jnp.exp(l0 - m), jnp.exp(l1 - m)
        p0, p1 = e0 / (e0 + e1), e1 / (e0 + e1)
        o_ref[0] = p0 - p0
        o_ref[1] = (p0 + p1) - p0

    return pl.pallas_call(body, name="lb_fwd", out_shape=jax.ShapeDtypeStruct(logits3.shape, F32))(logits3)


def lb_bwd(logits3, dlb1):
    def body(l_ref, d_ref, o_ref):
        l0, l1 = l_ref[0], l_ref[1]
        m = jnp.maximum(l0, l1)
        e0, e1 = jnp.exp(l0 - m), jnp.exp(l1 - m)
        p0, p1 = e0 / (e0 + e1), e1 / (e0 + e1)
        t = d_ref[...] * p0 * p1
        o_ref[0] = -t
        o_ref[1] = t

    return pl.pallas_call(body, name="lb_bwd", out_shape=jax.ShapeDtypeStruct(logits3.shape, F32))(logits3, dlb1)


def _hgrn_gates(zq, zf, lb):
    sf = _sigmoid(zf)
    f = lb + (1.0 - lb) * sf
    sq = _sigmoid(zq)
    return sf, f, jnp.log(f), 1.0 - f, sq, zq * sq


def _tri(n, upper=False):
    r = lax.broadcasted_iota(jnp.int32, (n, n), 0)
    c = lax.broadcasted_iota(jnp.int32, (n, n), 1)
    return jnp.where((c >= r) if upper else (r >= c), 1.0, 0.0).astype(F32)


def hgrn_fwd(proj, lb3, og3, l, comm=None):
    s = proj.shape[0]
    nh = 6
    n_chunk = s // CHUNK
    nsub = CHUNK // SUB

    hb = HGRN_HEADS_PER_STEP
    wide = hb * LANES

    def body(zq_ref, zf_ref, vi_ref, zg_ref, lb_ref, og_ref, main_ref, o_ref, q_a, k_a, v_a, c_a):
        og = og_ref[...]
        tril = _tri(CHUNK)
        rowi = lax.broadcasted_iota(jnp.int32, (SUB, LANES), 0)

        def one_head(hd, rows, st):
            cols = slice(hd * LANES, (hd + 1) * LANES)
            q_s, k_s, v_s, c_s = q_a.at[hd], k_a.at[hd], v_a.at[hd], c_a.at[hd]
            zg = zg_ref[rows, cols]
            _, _, lf, k, _, q = _hgrn_gates(zq_ref[rows, cols], zf_ref[rows, cols], lb_ref[:, cols])
            v = vi_ref[rows, cols]
            c = dot_nn(tril, lf, HI)
            q_s[...] = q
            k_s[...] = k
            v_s[...] = v
            c_s[...] = c
            o_inter = dot_nt(q * jnp.exp(c), st, HI)
            parts = []
            for i in range(nsub):
                lo = i * SUB
                blk = pl.ds(lo, SUB)
                qb, cb = q_s[blk, :], c_s[blk, :]
                ob = o_inter[lo : lo + SUB]
                if i > 0:
                    rr = c_s[pl.ds(lo - 1, 1), :]
                    qt = qb * jnp.exp(cb - rr)
                    kt = k_s[pl.ds(0, lo), :] * jnp.exp(rr - c_s[pl.ds(0, lo), :])
                    ob = ob + dot_nn(dot_nt(qt, kt, HI), v_s[pl.ds(0, lo), :], HI)
                for t in range(SUB):
                    e = jnp.where(rowi >= t, jnp.exp(cb - c_s[pl.ds(lo + t, 1), :]), 0.0)
                    a = jnp.sum(qb * k_s[pl.ds(lo + t, 1), :] * e, axis=-1, keepdims=True)
                    ob = ob + a * v_s[pl.ds(lo + t, 1), :]
                parts.append(ob)
            o = jnp.concatenate(parts, axis=0)
            ce = c_s[pl.ds(CHUNK - 1, 1), :]
            st = st * jnp.exp(ce) + dot_tn(v, k * jnp.exp(ce - c), HI)
            on, _ = _rms(o)
            o_ref[rows, cols] = o
            main_ref[rows, cols] = bf(on * og * (zg * _sigmoid(zg)))
            return st

        def chunk(ci, sts):
            rows = pl.ds(pl.multiple_of(ci * CHUNK, CHUNK), CHUNK)
            return tuple(one_head(hd, rows, sts[hd]) for hd in range(hb))

        lax.fori_loop(0, n_chunk, chunk, tuple(jnp.zeros((LANES, LANES), F32) for _ in range(hb)))

    def col(k):
        return pl.BlockSpec((s, wide), lambda h: (0, k * (nh // hb) + h))

    vec = pl.BlockSpec((None, 1, wide), lambda h: (l, 0, h))
    return _carry(
        body,
        comm,
        name="hgrn_fwd",
        grid=(nh // hb,),
        in_specs=[col(0), col(1), col(2), col(3), vec, pl.BlockSpec((None, 1, LANES), lambda h: (l, 0, 0))],
        out_specs=[pl.BlockSpec((s, wide), lambda h: (0, h))] * 2,
        out_shape=[jax.ShapeDtypeStruct((s, nh * LANES), BF16), jax.ShapeDtypeStruct((s, nh * LANES), F32)],
        scratch_shapes=[pltpu.VMEM((hb, CHUNK, LANES), F32)] * 4,
        args=(proj, proj, proj, proj, lb3, og3),
    )


def hgrn_bwd(proj, lb3, og3, l, o, dmixed, comm=None):
    s = proj.shape[0]
    nh = 6
    n_chunk = s // CHUNK
    nsub = CHUNK // SUB

    hb = HGRN_HEADS_PER_STEP
    wide = hb * LANES

    def body(zq_ref, zf_ref, vi_ref, zg_ref, lb_ref, og_ref, o_ref, dm_ref,
             dzq_ref, dzf_ref, dvi_ref, dzg_ref, dlb_ref, dog_ref,
             st_a, q_a, k_a, v_a, c_a, do_a, dq_a, dk_a, dv_a, acc_a):
        og = og_ref[...]
        tril = _tri(CHUNK)
        triu = _tri(CHUNK, upper=True)
        rowi = lax.broadcasted_iota(jnp.int32, (SUB, LANES), 0)

        def fwd_head(hd, ci, rows, st):
            cols = slice(hd * LANES, (hd + 1) * LANES)
            _, _, lf, k, _, _ = _hgrn_gates(zq_ref[rows, cols], zf_ref[rows, cols], lb_ref[:, cols])
            c = dot_nn(tril, lf, HI)
            ce = jnp.sum(lf, axis=0, keepdims=True)
            st_a[hd, ci] = st
            return st * jnp.exp(ce) + dot_tn(vi_ref[rows, cols], k * jnp.exp(ce - c), HI)

        def fwd_chunk(ci, sts):
            rows = pl.ds(pl.multiple_of(ci * CHUNK, CHUNK), CHUNK)
            return tuple(fwd_head(hd, ci, rows, sts[hd]) for hd in range(hb))

        lax.fori_loop(0, n_chunk, fwd_chunk, tuple(jnp.zeros((LANES, LANES), F32) for _ in range(hb)))
        acc_a[...] = jnp.zeros_like(acc_a)

        def bwd_head(hd, ci, rows, carry):
            dst, cg = carry
            cols = slice(hd * LANES, (hd + 1) * LANES)
            q_s, k_s, v_s, c_s, do_s = q_a.at[hd], k_a.at[hd], v_a.at[hd], c_a.at[hd], do_a.at[hd]
            dq_s, dk_s, dv_s, acc_s = dq_a.at[hd], dk_a.at[hd], dv_a.at[hd], acc_a.at[hd]
            lb = lb_ref[:, cols]
            zq, zf, zg = zq_ref[rows, cols], zf_ref[rows, cols], zg_ref[rows, cols]
            sf, f, lf, k, sq, q = _hgrn_gates(zq, zf, lb)
            v = vi_ref[rows, cols]
            c = dot_nn(tril, lf, HI)
            st = st_a[hd, ci]
            on, r = _rms(o_ref[rows, cols])
            sg = _sigmoid(zg)
            dmain = dm_ref[rows, cols]
            dy = dmain * (zg * sg)
            dzg_ref[rows, cols] = dmain * (on * og) * _dsilu(zg, sg)
            do, dog = _rms_bwd(dy, on, r, og)
            acc_s[pl.ds(0, 1), :] += dog
            q_s[...] = q
            k_s[...] = k
            v_s[...] = v
            c_s[...] = c
            do_s[...] = do
            ce = c_s[pl.ds(CHUNK - 1, 1), :]
            eq = jnp.exp(c)
            ek = jnp.exp(ce - c)
            qt_all = q * eq
            dq_s[...] = dot_nn(do, st, HI) * eq
            dv_s[...] = dot_nt(k * ek, dst, HI)
            dk_s[...] = dot_nn(v, dst, HI) * ek
            dst = dst * jnp.exp(ce) + dot_tn(do, qt_all, HI)
            for i in range(nsub):
                lo = i * SUB
                blk = pl.ds(lo, SUB)
                qb, cb, dob = q_s[blk, :], c_s[blk, :], do_s[blk, :]
                if i > 0:
                    prev = pl.ds(0, lo)
                    rr = c_s[pl.ds(lo - 1, 1), :]
                    eqi = jnp.exp(cb - rr)
                    eki = jnp.exp(rr - c_s[prev, :])
                    qt = qb * eqi
                    kt = k_s[prev, :] * eki
                    amat = dot_nt(qt, kt, HI)
                    damat = dot_nt(dob, v_s[prev, :], HI)
                    dv_s[prev, :] += dot_tn(amat, dob, HI)
                    dq_s[blk, :] += dot_nn(damat, kt, HI) * eqi
                    dk_s[prev, :] += dot_tn(damat, qt, HI) * eki
                dqb = jnp.zeros((SUB, LANES), F32)
                for t in range(SUB):
                    row = pl.ds(lo + t, 1)
                    e = jnp.where(rowi >= t, jnp.exp(cb - c_s[row, :]), 0.0)
                    kr = k_s[row, :]
                    a = jnp.sum(qb * kr * e, axis=-1, keepdims=True)
                    da = jnp.sum(dob * v_s[row, :], axis=-1, keepdims=True)
                    dv_s[row, :] += jnp.sum(a * dob, axis=0, keepdims=True)
                    dqb = dqb + da * kr * e
                    dk_s[row, :] += jnp.sum(da * qb * e, axis=0, keepdims=True)
                dq_s[blk, :] += dqb
            dq, dk = dq_s[...], dk_s[...]
            dg = q * dq - k * dk
            dlf = dot_nn(triu, dg, HI) + cg
            cg = cg + jnp.sum(dg, axis=0, keepdims=True)
            df = dlf / f - dk
            dzf_ref[rows, cols] = df * (1.0 - lb) * sf * (1.0 - sf)
            acc_s[pl.ds(1, 1), :] += jnp.sum(df * (1.0 - sf), axis=0, keepdims=True)
            dzq_ref[rows, cols] = dq * _dsilu(zq, sq)
            dvi_ref[rows, cols] = dv_s[...]
            return dst, cg

        def bwd_chunk(jj, carries):
            ci = n_chunk - 1 - jj
            rows = pl.ds(pl.multiple_of(ci * CHUNK, CHUNK), CHUNK)
            return tuple(bwd_head(hd, ci, rows, carries[hd]) for hd in range(hb))

        zero = (jnp.zeros((LANES, LANES), F32), jnp.zeros((1, LANES), F32))
        lax.fori_loop(0, n_chunk, bwd_chunk, tuple(zero for _ in range(hb)))

        @pl.when(pl.program_id(0) == 0)
        def _():
            dog_ref[...] = jnp.zeros_like(dog_ref)

        for hd in range(hb):
            dlb_ref[:, hd * LANES : (hd + 1) * LANES] = acc_a[hd, pl.ds(1, 1), :]
            dog_ref[...] += acc_a[hd, pl.ds(0, 1), :]

    def col(k):
        return pl.BlockSpec((s, wide), lambda h: (0, k * (nh // hb) + h), pipeline_mode=pl.Buffered(1))

    head_in = pl.BlockSpec((s, wide), lambda h: (0, h), pipeline_mode=pl.Buffered(1))
    head = pl.BlockSpec((s, wide), lambda h: (0, h))
    vec = pl.BlockSpec((None, 1, wide), lambda h: (l, 0, h))
    ck = pltpu.VMEM((hb, CHUNK, LANES), F32)
    return _carry(
        body,
        comm,
        name="hgrn_bwd",
        grid=(nh // hb,),
        in_specs=[col(0), col(1), col(2), col(3), vec, pl.BlockSpec((None, 1, LANES), lambda h: (l, 0, 0)), head_in, head_in],
        out_specs=[head] * 4 + [pl.BlockSpec((1, wide), lambda h: (0, h)), _const((1, LANES))],
        out_shape=[jax.ShapeDtypeStruct((s, nh * LANES), F32)] * 4
        + [jax.ShapeDtypeStruct((1, nh * LANES), F32), jax.ShapeDtypeStruct((1, LANES), F32)],
        scratch_shapes=[pltpu.VMEM((hb, n_chunk, LANES, LANES), F32)] + [ck] * 8 + [pltpu.VMEM((hb, 8, LANES), F32)],
        args=(proj, proj, proj, proj, lb3, og3, o, dmixed),
    )


MEM_SCALE = HEAD64**-0.5


def _mem_heads(qraw, kvm, qg, kg, pr, m0):
    lo = pr * LANES
    uq, rq = _rms64(qraw[:, lo : lo + LANES], m0)
    uk, rk = _rms64(kvm[:, lo : lo + LANES], m0)
    v = bf(kvm[:, 2 * LANES + lo : 3 * LANES + lo])
    return uq, rq, uk, rk, v, uq * qg, bf(uk * kg)


def memattn_fwd(proj, qblk, kvm, qg3, kg3, l):
    s = proj.shape[0]
    nm = kvm.shape[0]
    tm = _tok(s)

    def body(q_ref, kv_ref, qg_ref, kg_ref, o_ref):
        m0 = _lane_mask0((1, LANES))
        qraw, kvv = q_ref[...], kv_ref[...]
        for pr in range(2):
            _, _, _, _, v, qn, kn = _mem_heads(qraw, kvv, qg_ref[...], kg_ref[...], pr, m0)
            out = jnp.zeros((tm, LANES), F32)
            for hh in range(2):
                mh = m0 if hh == 0 else jnp.logical_not(m0)
                sc = dot_nt(bf(jnp.where(mh, qn, 0.0)), kn) * MEM_SCALE
                p = jnp.exp(sc - jnp.max(sc, axis=-1, keepdims=True))
                p = p / jnp.sum(p, axis=-1, keepdims=True)
                out = jnp.where(mh, dot_nn(bf(p), v), out)
            o_ref[:, pr * LANES : (pr + 1) * LANES] = bf(out)

    gspec = pl.BlockSpec((None, 1, LANES), lambda i: (l, 0, 0))
    return pl.pallas_call(
        body,
        name="memattn_fwd",
        grid=(s // tm,),
        in_specs=[pl.BlockSpec((tm, 2 * LANES), lambda i: (i, qblk)), _const((nm, 4 * LANES)), gspec, gspec],
        out_specs=pl.BlockSpec((tm, 2 * LANES), lambda i: (i, 0)),
        out_shape=jax.ShapeDtypeStruct((s, 2 * LANES), BF16),
        compiler_params=_cparams(("arbitrary",)),
    )(proj, kvm, qg3, kg3)


def memattn_bwd(proj, qblk, kvm, qg3, kg3, l, dmixed):
    s = proj.shape[0]
    nm = kvm.shape[0]
    tm = _tok(s)

    def body(q_ref, kv_ref, qg_ref, kg_ref, dm_ref, dq_ref, dkv_ref, dqg_ref, dkg_ref):
        m0 = _lane_mask0((1, LANES))
        qraw, kvv = q_ref[...], kv_ref[...]
        qg, kg = qg_ref[...], kg_ref[...]

        @pl.when(pl.program_id(0) == 0)
        def _():
            dkv_ref[...] = jnp.zeros_like(dkv_ref)
            dqg_ref[...] = jnp.zeros_like(dqg_ref)
            dkg_ref[...] = jnp.zeros_like(dkg_ref)

        for pr in range(2):
            lo = pr * LANES
            uq, rq, uk, rk, v, qn, kn = _mem_heads(qraw, kvv, qg, kg, pr, m0)
            do = dm_ref[:, lo : lo + LANES]
            dqn = jnp.zeros((tm, LANES), F32)
            dkn = jnp.zeros((nm, LANES), F32)
            dv = jnp.zeros((nm, LANES), F32)
            for hh in range(2):
                mh = m0 if hh == 0 else jnp.logical_not(m0)
                qh = bf(jnp.where(mh, qn, 0.0))
                doh = bf(jnp.where(mh, do, 0.0))
                sc = dot_nt(qh, kn) * MEM_SCALE
                p = jnp.exp(sc - jnp.max(sc, axis=-1, keepdims=True))
                p = p / jnp.sum(p, axis=-1, keepdims=True)
                dp = dot_nt(doh, v)
                ds = bf(p * (dp - jnp.sum(p * dp, axis=-1, keepdims=True)))
                dqn = dqn + jnp.where(mh, dot_nn(ds, kn), 0.0) * MEM_SCALE
                dkn = dkn + dot_tn(ds, qh) * MEM_SCALE
                dv = dv + dot_tn(bf(p), doh)
            dqr, dqg = _rms64_bwd(dqn, uq, rq, qg, m0)
            dkr, dkg = _rms64_bwd(dkn, uk, rk, kg, m0)
            dq_ref[:, lo : lo + LANES] = dqr
            dkv_ref[:, lo : lo + LANES] += dkr
            dkv_ref[:, 2 * LANES + lo : 3 * LANES + lo] += dv
            dqg_ref[...] += dqg
            dkg_ref[...] += dkg

    gspec = pl.BlockSpec((None, 1, LANES), lambda i: (l, 0, 0))
    return pl.pallas_call(
        body,
        name="memattn_bwd",
        grid=(s // tm,),
        in_specs=[
            pl.BlockSpec((tm, 2 * LANES), lambda i: (i, qblk)),
            _const((nm, 4 * LANES)),
            gspec,
            gspec,
            pl.BlockSpec((tm, 2 * LANES), lambda i: (i, 3)),
        ],
        out_specs=[pl.BlockSpec((tm, 2 * LANES), lambda i: (i, 0)), _const((nm, 4 * LANES)), _const((1, LANES)), _const((1, LANES))],
        out_shape=[
            jax.ShapeDtypeStruct((s, 2 * LANES), F32),
            jax.ShapeDtypeStruct((nm, 4 * LANES), F32),
            jax.ShapeDtypeStruct((1, LANES), F32),
            jax.ShapeDtypeStruct((1, LANES), F32),
        ],
        compiler_params=_cparams(("arbitrary",)),
    )(proj, kvm, qg3, kg3, dmixed)


KV_MAIN = 768


def _log_sigmoid(z):
    return jnp.minimum(z, 0.0) - jnp.log(1.0 + jnp.exp(-jnp.abs(z)))


def kvprep_fwd(kvf, kg, fb):
    s = kvf.shape[0]
    tm = _tok(s)

    def body(kvf_ref, kg_ref, fb_ref, k_ref, v_ref, clf_ref, carry):
        m0 = _lane_mask0((1, LANES))

        @pl.when(pl.program_id(0) == 0)
        def _():
            carry[...] = jnp.zeros_like(carry)

        for j in range(KV_MAIN // LANES):
            u, _ = _rms64(kvf_ref[:, j * LANES : (j + 1) * LANES], m0)
            k_ref[:, j * LANES : (j + 1) * LANES] = bf(u * kg_ref[...])
        v_ref[...] = bf(kvf_ref[:, KV_MAIN : 2 * KV_MAIN])
        lf = _log_sigmoid(kvf_ref[:, 2 * KV_MAIN :] + fb_ref[...])
        clf_ref[...] = dot_nn(_tri(tm), lf, HI) + carry[...]
        carry[...] += jnp.sum(lf, axis=0, keepdims=True)

    n = kvf.shape[1]
    return pl.pallas_call(
        body,
        name="kvprep_fwd",
        grid=(s // tm,),
        in_specs=[pl.BlockSpec((tm, n), lambda i: (i, 0)), _const((1, LANES)), _const((1, LANES))],
        out_specs=[pl.BlockSpec((tm, KV_MAIN), lambda i: (i, 0))] * 2 + [pl.BlockSpec((tm, LANES), lambda i: (i, 0))],
        out_shape=[jax.ShapeDtypeStruct((s, KV_MAIN), BF16)] * 2 + [jax.ShapeDtypeStruct((s, LANES), F32)],
        scratch_shapes=[pltpu.VMEM((1, LANES), F32)],
        compiler_params=_cparams(("arbitrary",)),
    )(kvf, kg, fb)


def kvprep_bwd(kvf, kg, fb, dk, dv, dclf):
    s, n = kvf.shape
    tm = _tok(s)
    nb = s // tm

    def body(kvf_ref, kg_ref, fb_ref, dk_ref, dv_ref, dclf_ref, o_ref, dkg_ref, dfb_ref, carry):
        m0 = _lane_mask0((1, LANES))

        @pl.when(pl.program_id(0) == 0)
        def _():
            carry[...] = jnp.zeros_like(carry)
            dkg_ref[...] = jnp.zeros_like(dkg_ref)
            dfb_ref[...] = jnp.zeros_like(dfb_ref)

        kg_ = kg_ref[...]
        for j in range(KV_MAIN // LANES):
            cols = slice(j * LANES, (j + 1) * LANES)
            u, r = _rms64(kvf_ref[:, cols], m0)
            dkr, dkg = _rms64_bwd(dk_ref[:, cols], u, r, kg_, m0)
            o_ref[:, cols] = dkr
            dkg_ref[...] += dkg
        o_ref[:, KV_MAIN : 2 * KV_MAIN] = dv_ref[...]
        z = kvf_ref[:, 2 * KV_MAIN :] + fb_ref[...]
        dc = dclf_ref[...]
        dlf = dot_nn(_tri(tm, upper=True), dc, HI) + carry[...]
        carry[...] += jnp.sum(dc, axis=0, keepdims=True)
        dz = dlf * _sigmoid(-z)
        o_ref[:, 2 * KV_MAIN :] = dz
        dfb_ref[...] += jnp.sum(dz, axis=0, keepdims=True)

    rev = lambda i: (nb - 1 - i, 0)
    return pl.pallas_call(
        body,
        name="kvprep_bwd",
        grid=(nb,),
        in_specs=[pl.BlockSpec((tm, n), rev), _const((1, LANES)), _const((1, LANES)), pl.BlockSpec((tm, KV_MAIN), rev),
                  pl.BlockSpec((tm, KV_MAIN), rev), pl.BlockSpec((tm, LANES), rev)],
        out_specs=[pl.BlockSpec((tm, n), rev), _const((1, LANES)), _const((1, LANES))],
        out_shape=[jax.ShapeDtypeStruct((s, n), F32), jax.ShapeDtypeStruct((1, LANES), F32), jax.ShapeDtypeStruct((1, LANES), F32)],
        scratch_shapes=[pltpu.VMEM((1, LANES), F32)],
        compiler_params=_cparams(("arbitrary",)),
    )(kvf, kg, fb, dk, dv, dclf)


FOX_SCALE = HEAD64**-0.5


def _lane_col(block, lane_idx, h):
    return jnp.sum(jnp.where(lane_idx == h, block, 0.0), axis=-1, keepdims=True)


def _causal(tq, ext, i, transposed=False):
    if transposed:
        key = lax.broadcasted_iota(jnp.int32, (ext, tq), 0)
        qry = lax.broadcasted_iota(jnp.int32, (ext, tq), 1) + i * tq
    else:
        qry = lax.broadcasted_iota(jnp.int32, (tq, ext), 0) + i * tq
        key = lax.broadcasted_iota(jnp.int32, (tq, ext), 1)
    return key <= qry


def fox_fwd(proj, k_sh, v_sh, clf, clf_t, qg3, j_layer, comm=None):
    s = proj.shape[0]
    npair = 6
    tq = TQ if s % TQ == 0 else s
    nq = s // tq

    def body(q_ref, gate_ref, k_ref, v_ref, clf_ref, clft_ref, qg_ref, main_ref, o_ref, lse_ref):
        j = pl.program_id(0)
        lane = lax.broadcasted_iota(jnp.int32, (1, LANES), 1)
        m0 = lane < HEAD64
        u, _ = _rms64(q_ref[...], m0)
        qn = u * qg_ref[...] * FOX_SCALE
        clfv = clf_ref[...]
        for hh in range(2):
            h = 2 * j + hh
            mh = m0 if hh == 0 else jnp.logical_not(m0)
            qh = bf(jnp.where(mh, qn, 0.0))
            dcol = _lane_col(clfv, lane, h)
            drow = clft_ref[pl.ds(h, 1), :]
            for i in range(nq):
                rows = slice(i * tq, (i + 1) * tq)
                ext = (i + 1) * tq
                sc = dot_nt(qh[rows], k_ref[0:ext, :]) + dcol[rows] - drow[:, :ext]
                sc = jnp.where(_causal(tq, ext, i), sc, -jnp.inf)
                m = jnp.max(sc, axis=-1, keepdims=True)
                p = jnp.exp(sc - m)
                lsum = jnp.sum(p, axis=-1, keepdims=True)
                pv = dot_nn(bf(p), v_ref[0:ext, :]) / lsum
                lse = m + jnp.log(lsum)
                if hh == 0:
                    o_ref[rows, :] = pv
                    lse_ref[rows, :] = jnp.where(lane == 0, lse, 0.0)
                else:
                    o_ref[rows, :] = jnp.where(mh, pv, o_ref[rows, :])
                    lse_ref[rows, :] = jnp.where(lane == 1, lse, lse_ref[rows, :])
        main_ref[...] = bf(o_ref[...] * _sigmoid(gate_ref[...]))

    blk = lambda off: pl.BlockSpec((s, LANES), lambda j: (0, off + j))
    return _carry(
        body,
        comm,
        name="fox_fwd",
        grid=(npair,),
        in_specs=[blk(0), blk(npair), blk(0), blk(0), _const((s, LANES)), _const((16, s)),
                  pl.BlockSpec((None, 1, LANES), lambda j: (j_layer, 0, 0))],
        out_specs=[blk(0)] * 3,
        out_shape=[jax.ShapeDtypeStruct((s, npair * LANES), BF16)] + [jax.ShapeDtypeStruct((s, npair * LANES), F32)] * 2,
        scratch_shapes=[],
        args=(proj, proj, k_sh, v_sh, clf, clf_t, qg3),
    )


def fox_bwd(proj, k_sh, v_sh, clf, clf_t, qg3, j_layer, o, lse, lse_t, dmixed, dk_in, dv_in, dclf_in, comm=None):
    s = proj.shape[0]
    npair = 6
    tq = TQ if s % TQ == 0 else s
    nq = s // tq

    def body(q_ref, gate_ref, k_ref, v_ref, clf_ref, clft_ref, qg_ref, o_ref, lse_ref, lset_ref, dm_ref, dkin_ref, dvin_ref, dclfin_ref,
             dq_ref, dgate_ref, dk_ref, dv_ref, dclf_ref, dqg_ref, dqn_s, dcl_s):
        j = pl.program_id(0)
        lane = lax.broadcasted_iota(jnp.int32, (1, LANES), 1)
        m0 = lane < HEAD64
        qg = qg_ref[...]
        u, r = _rms64(q_ref[...], m0)
        qn = u * qg * FOX_SCALE
        ov = o_ref[...]
        gate = gate_ref[...]
        sg = _sigmoid(gate)
        dmain = dm_ref[...]
        do = dmain * sg
        dgate_ref[...] = dmain * ov * sg * (1.0 - sg)
        dk_ref[...] = dkin_ref[...]
        dv_ref[...] = dvin_ref[...]
        clfv = clf_ref[...]
        lsev = lse_ref[...]
        ones8 = jnp.ones((8, LANES), F32)

        @pl.when(j == 0)
        def _():
            dclf_ref[...] = dclfin_ref[...]
            dqg_ref[...] = jnp.zeros_like(dqg_ref)

        for hh in range(2):
            h = 2 * j + hh
            mh = m0 if hh == 0 else jnp.logical_not(m0)
            qh = bf(jnp.where(mh, qn, 0.0))
            doh = jnp.where(mh, do, 0.0)
            dohb = bf(doh)
            doo = doh * ov
            dcol = _lane_col(clfv, lane, h)
            drow = clft_ref[pl.ds(h, 1), :]
            lcol = _lane_col(lsev, lane, hh)
            lrow = lset_ref[pl.ds(h, 1), :]
            delta = jnp.sum(doo, axis=-1, keepdims=True)
            dcl_s[...] = jnp.zeros_like(dcl_s)
            for i in range(nq):
                rows = slice(i * tq, (i + 1) * tq)
                ext = (i + 1) * tq
                kk, vv = k_ref[0:ext, :], v_ref[0:ext, :]
                sc = dot_nt(qh[rows], kk) + dcol[rows] - drow[:, :ext]
                p = jnp.where(_causal(tq, ext, i), jnp.exp(sc - lcol[rows]), 0.0)
                ds = p * (dot_nt(dohb[rows], vv) - delta[rows])
                dqh = dot_nn(bf(ds), kk) * FOX_SCALE
                if hh == 0:
                    dqn_s[rows, :] = dqh
                else:
                    dqn_s[rows, :] = jnp.where(mh, dqh, dqn_s[rows, :])
                dcl_s[rows, :] += jnp.sum(ds, axis=-1, keepdims=True)
                sct = dot_nt(kk, qh[rows]) + drow[:, rows] - dcol[:ext]
                pt = jnp.where(_causal(tq, ext, i, transposed=True), jnp.exp(sct - lrow[:, rows]), 0.0)
                delta_row = dot_nt(ones8, doo[rows], HI)[0:1]
                dst = pt * (dot_nt(vv, dohb[rows]) - delta_row)
                dv_ref[0:ext, :] += dot_nn(bf(pt), dohb[rows])
                dk_ref[0:ext, :] += dot_nn(bf(dst), qh[rows])
                dcl_s[0:ext, :] -= jnp.sum(dst, axis=-1, keepdims=True)
            dclf_ref[...] += jnp.where(lane == h, dcl_s[...], 0.0)
        dqr, dqg = _rms64_bwd(dqn_s[...], u, r, qg, m0)
        dq_ref[...] = dqr
        dqg_ref[...] += dqg

    blk = lambda off: pl.BlockSpec((s, LANES), lambda j: (0, off + j))
    full = _const((s, LANES))
    return _carry(
        body,
        comm,
        name="fox_bwd",
        grid=(npair,),
        in_specs=[blk(0), blk(npair), blk(0), blk(0), full, _const((16, s)), pl.BlockSpec((None, 1, LANES), lambda j: (j_layer, 0, 0)),
                  blk(0), blk(0), _const((16, s)), blk(0), blk(0), blk(0), full],
        out_specs=[blk(0)] * 4 + [full, _const((1, LANES))],
        out_shape=[jax.ShapeDtypeStruct((s, npair * LANES), F32)] * 4
        + [jax.ShapeDtypeStruct((s, LANES), F32), jax.ShapeDtypeStruct((1, LANES), F32)],
        scratch_shapes=[pltpu.VMEM((s, LANES), F32), pltpu.VMEM((s, LANES), F32)],
        args=(proj, proj, k_sh, v_sh, clf, clf_t, qg3, o, lse, lse_t, dmixed, dk_in, dv_in, dclf_in),
    )


def loss_head(y, target):
    s, d = y.shape
    tm = _tok(s)

    def body(y_ref, t_ref, loss_ref, dy_ref):
        err = y_ref[...] - t_ref[...]
        dy_ref[...] = err * (1.0 / d)

        @pl.when(pl.program_id(0) == 0)
        def _():
            loss_ref[...] = jnp.zeros_like(loss_ref)

        part = jnp.sum(jnp.mean(err * err, axis=-1, keepdims=True), axis=0, keepdims=True)
        loss_ref[...] += 0.5 * part

    row = pl.BlockSpec((tm, d), lambda i: (i, 0))
    return pl.pallas_call(
        body,
        name="loss_head",
        grid=(s // tm,),
        in_specs=[row, row],
        out_specs=[_const((1, 1)), row],
        out_shape=[jax.ShapeDtypeStruct((1, 1), F32), jax.ShapeDtypeStruct((s, d), F32)],
        compiler_params=_cparams(("arbitrary",)),
    )(y, target)


def _row_tile(r, c, n_arrays):
    budget = VMEM_LIMIT_BYTES // 2
    padded_c = -(-c // LANES) * LANES
    for step in (16, 8):
        fits = [t for t in range(step, r + 1, step) if r % t == 0 and 2 * n_arrays * t * padded_c * 4 <= budget]
        if fits:
            return fits[-1]
    return r


def _as2d(a):
    return a.reshape(-1, a.shape[-1]) if a.ndim >= 2 else a.reshape(1, -1)


def adamw(ws, gss, ms, vs):
    nw = len(ws)
    shape = ws[0].shape
    w2, m2, v2 = ([_as2d(t) for t in lst] for lst in (ws, ms, vs))
    rows, c = w2[0].shape
    gss = [[g.reshape(-1, c) for g in gs] for gs in gss]
    sizes = [g.shape[0] for g in gss[0]]
    assert sum(sizes) == rows
    tr = _row_tile(min(sizes), c, 8 * nw)
    assert all(r % tr == 0 for r in sizes)
    c1 = 1.0 - ADAM_B1**ADAM_STEP
    c2 = 1.0 - ADAM_B2**ADAM_STEP
    outs = []
    first = 0
    for k, r in enumerate(sizes):
        n_prev = len(outs)

        def body(*refs, n_prev=n_prev):
            out_refs = refs[4 * nw + n_prev :]
            for i in range(nw):
                w_ref, g_ref, m_ref, v_ref = refs[4 * i : 4 * i + 4]
                go_ref, d_ref, nm_ref, nv_ref = out_refs[4 * i : 4 * i + 4]
                gv = g_ref[...]
                nm = ADAM_B1 * m_ref[...] + (1.0 - ADAM_B1) * gv
                nv = ADAM_B2 * v_ref[...] + (1.0 - ADAM_B2) * (gv * gv)
                go_ref[...] = gv
                nm_ref[...] = nm
                nv_ref[...] = nv
                d_ref[...] = -ADAM_LR * ((nm / c1) / (jnp.sqrt(nv / c2) + ADAM_EPS) + ADAM_WD * w_ref[...])

        spec = pl.BlockSpec((tr, c), lambda i, b0=first // tr: (b0 + i, 0))
        args = [t for i in range(nw) for t in (w2[i], gss[i][k], m2[i], v2[i])]
        outs = pl.pallas_call(
            body,
            name="adamw",
            grid=(r // tr,),
            in_specs=[spec, pl.BlockSpec((tr, c), lambda i: (i, 0)), spec, spec] * nw + [ANY] * n_prev,
            out_specs=[spec] * (4 * nw),
            out_shape=[jax.ShapeDtypeStruct((rows, c), F32)] * (4 * nw),
            input_output_aliases={4 * nw + j: j for j in range(n_prev)},
            compiler_params=_cparams(("arbitrary",)),
        )(*args, *outs)
        first += r
    return [tuple(t.reshape(shape) for t in outs[4 * i : 4 * i + 4]) for i in range(nw)]


def _by_shape(arrays, fn):
    groups = {}
    for i, t in enumerate(arrays):
        groups.setdefault((t.shape, str(t.dtype)), []).append(i)
    out = [None] * len(arrays)
    for idx in groups.values():
        for i, res in zip(idx, fn(idx)):
            out[i] = res
    return out


def pair_sum(gs, recvs, c_arr):
    n = len(gs)
    _, k, r, c = gs[0].shape
    tr = _row_tile(r, c, 3 * n)

    def body(c_ref, *refs):
        for i in range(n):
            refs[2 * n + i][...] = bf(refs[i][...] + refs[n + i][...].astype(F32))

    slab = pl.BlockSpec((None, tr, c), lambda kk, i, cr: (kk, i, 0))
    return pl.pallas_call(
        body,
        name="pair_sum",
        grid_spec=pltpu.PrefetchScalarGridSpec(
            num_scalar_prefetch=1,
            grid=(k, r // tr),
            in_specs=[pl.BlockSpec((None, None, tr, c), lambda kk, i, cr: (cr[0], kk, i, 0))] * n + [slab] * n,
            out_specs=[slab] * n,
        ),
        out_shape=[jax.ShapeDtypeStruct((k, r, c), BF16)] * n,
        compiler_params=_cparams(("arbitrary", "arbitrary")),
    )(c_arr, *gs, *recvs)


def chip_sum(ps, qs, sel):
    n = len(ps)
    _, r, c = ps[0].shape
    nq = qs[0].shape[0]
    tr = _row_tile(r, c, 4 * n)

    def body(sel_ref, *refs):
        for a in range(n):
            acc = refs[a][...].astype(F32)
            for i in range(nq):
                acc = acc + refs[n + a][i].astype(F32)
            refs[2 * n + a][...] = acc

    return pl.pallas_call(
        body,
        name="chip_sum",
        grid_spec=pltpu.PrefetchScalarGridSpec(
            num_scalar_prefetch=1,
            grid=(r // tr,),
            in_specs=[pl.BlockSpec((None, tr, c), lambda i, sr: (sr[0], i, 0))] * n + [pl.BlockSpec((nq, tr, c), lambda i, sr: (0, i, 0))] * n,
            out_specs=[pl.BlockSpec((None, tr, c), lambda i, sr: (sr[1], i, 0))] * n,
        ),
        out_shape=[jax.ShapeDtypeStruct((2, r, c), F32)] * n,
        compiler_params=_cparams(("arbitrary",)),
    )(sel, *ps, *qs)


def cast_into_slot(w4s, owns, sel, dtype):
    n = len(w4s)
    _, _, r, c = w4s[0].shape
    tr = _row_tile(r, c, 2 * n)

    def body(sel_ref, *refs):
        for i in range(n):
            refs[n + i][...] = refs[i][...].astype(dtype)

    return pl.pallas_call(
        body,
        name="cast_into_slot",
        grid_spec=pltpu.PrefetchScalarGridSpec(
            num_scalar_prefetch=1,
            grid=(2, r // tr),
            in_specs=[pl.BlockSpec((None, None, tr, c), lambda hf, i, sr, g=g: (g, hf, i, 0)) for g in owns],
            out_specs=[pl.BlockSpec((None, None, tr, c), lambda hf, i, sr: (sr[0], hf, i, 0))] * n,
        ),
        out_shape=[jax.ShapeDtypeStruct((N_CHIPS, 2, r, c), dtype)] * n,
        compiler_params=_cparams(("arbitrary", "arbitrary")),
    )(sel, *w4s)


def _place():
    x, y, c = lax.axis_index("x"), lax.axis_index("y"), lax.axis_index("c")
    chips = [(1 - x, y), (x, 1 - y), (1 - x, 1 - y)]
    return x, y, c, 2 * x + y, chips, [2 * cx + cy for cx, cy in chips]


def _rcopy(src, dst, send, recv, dev):
    return pltpu.make_async_remote_copy(src_ref=src, dst_ref=dst, send_sem=send, recv_sem=recv, device_id=dev, device_id_type=MESH)


class Gather:
    def __init__(self, bufs):
        n = len(bufs)
        self.n = n
        self.args = list(bufs)
        self.out_shape = [jax.ShapeDtypeStruct(t.shape, t.dtype) for t in bufs]
        self.aliases = {a: a for a in range(n)}
        self.scratch = [pltpu.SemaphoreType.DMA((n, 6)), pltpu.SemaphoreType.DMA((n, 6))]

    def _sends(self, outs, send, recv):
        x, y, c, me, chips, _ = _place()
        cps = []
        for a in range(self.n):
            mine = outs[a].at[me, c]
            cps += [_rcopy(mine, mine, send.at[a, j], recv.at[a, j], (*chips[j], c)) for j in range(3)]
        return cps

    def start(self, ins, outs, scr):
        for cp in self._sends(outs, *scr):
            cp.start()

    def finish(self, ins, outs, scr):
        send, recv = scr
        x, y, c, me, chips, cidx = _place()
        sib = (x, y, 1 - c)
        passed = []
        for a in range(self.n):
            for j in range(3):
                landed = outs[a].at[cidx[j], c]
                _rcopy(landed, landed, send.at[a, j], recv.at[a, j], (*chips[j], c)).wait_recv()
                fwd = _rcopy(landed, landed, send.at[a, 3 + j], recv.at[a, 3 + j], sib)
                fwd.start()
                passed.append(fwd)
        for a in range(self.n):
            for j in range(3):
                theirs = outs[a].at[cidx[j], 1 - c]
                _rcopy(theirs, theirs, send.at[a, 3 + j], recv.at[a, 3 + j], sib).wait_recv()
        for cp in self._sends(outs, send, recv) + passed:
            cp.wait_send()


class PairExchange:
    def __init__(self, gs):
        n = len(gs)
        self.n = n
        self.args = list(gs)
        self.out_shape = [jax.ShapeDtypeStruct(t.shape[1:], t.dtype) for t in gs]
        self.aliases = {}
        self.scratch = [pltpu.SemaphoreType.DMA((n,)), pltpu.SemaphoreType.DMA((n,))]

    def _copies(self, ins, outs, send, recv):
        x, y, c = lax.axis_index("x"), lax.axis_index("y"), lax.axis_index("c")
        return [_rcopy(ins[a].at[1 - c], outs[a], send.at[a], recv.at[a], (x, y, 1 - c)) for a in range(self.n)]

    def start(self, ins, outs, scr):
        for cp in self._copies(ins, outs, *scr):
            cp.start()

    def finish(self, ins, outs, scr):
        for cp in self._copies(ins, outs, *scr):
            cp.wait()


class ChipExchange:
    def __init__(self, ps):
        n = len(ps)
        self.n = n
        self.args = list(ps)
        self.out_shape = [jax.ShapeDtypeStruct((3,) + t.shape[1:], t.dtype) for t in ps]
        self.aliases = {}
        self.scratch = [pltpu.SemaphoreType.DMA((n, 3)), pltpu.SemaphoreType.DMA((n, 3))]

    def _sends(self, ins, outs, send, recv):
        x, y, c, me, chips, cidx = _place()
        return [
            _rcopy(ins[a].at[cidx[j]], outs[a].at[j], send.at[a, j], recv.at[a, j], (*chips[j], c))
            for a in range(self.n)
            for j in range(3)
        ]

    def start(self, ins, outs, scr):
        for cp in self._sends(ins, outs, *scr):
            cp.start()

    def finish(self, ins, outs, scr):
        send, recv = scr
        x, y, c, me, chips, _ = _place()
        for a in range(self.n):
            for j in range(3):
                landed = outs[a].at[j]
                _rcopy(landed, landed, send.at[a, j], recv.at[a, j], (*chips[j], c)).wait_recv()
        for cp in self._sends(ins, outs, send, recv):
            cp.wait_send()


class PairShare:
    def __init__(self, bufs):
        n = len(bufs)
        self.n = n
        self.args = list(bufs)
        self.out_shape = [jax.ShapeDtypeStruct(t.shape, t.dtype) for t in bufs]
        self.aliases = {a: a for a in range(n)}
        self.scratch = [pltpu.SemaphoreType.DMA((n,)), pltpu.SemaphoreType.DMA((n,))]

    def _sends(self, outs, send, recv):
        x, y, c = lax.axis_index("x"), lax.axis_index("y"), lax.axis_index("c")
        return [_rcopy(outs[a].at[c], outs[a].at[c], send.at[a], recv.at[a], (x, y, 1 - c)) for a in range(self.n)]

    def start(self, ins, outs, scr):
        for cp in self._sends(outs, *scr):
            cp.start()

    def finish(self, ins, outs, scr):
        send, recv = scr
        x, y, c = lax.axis_index("x"), lax.axis_index("y"), lax.axis_index("c")
        for a in range(self.n):
            theirs = outs[a].at[1 - c]
            _rcopy(theirs, theirs, send.at[a], recv.at[a], (x, y, 1 - c)).wait_recv()
        for cp in self._sends(outs, send, recv):
            cp.wait_send()


class Multi:
    def __init__(self, comms):
        self.comms = comms
        self.args, self.out_shape, self.scratch, self.aliases = [], [], [], {}
        self.spans = []
        for cm in comms:
            a0, o0, s0 = len(self.args), len(self.out_shape), len(self.scratch)
            self.aliases.update({a0 + i: o0 + o for i, o in cm.aliases.items()})
            self.args += cm.args
            self.out_shape += cm.out_shape
            self.scratch += cm.scratch
            self.spans.append((slice(a0, len(self.args)), slice(o0, len(self.out_shape)), slice(s0, len(self.scratch))))

    def start(self, ins, outs, scr):
        for cm, (sa, so, ss) in zip(self.comms, self.spans):
            cm.start(ins[sa], outs[so], scr[ss])

    def finish(self, ins, outs, scr):
        for cm, (sa, so, ss) in zip(self.comms, self.spans):
            cm.finish(ins[sa], outs[so], scr[ss])

    def split(self, res):
        return [list(res[so]) for _, so, _ in self.spans]


def run_comm(comm, name):
    na, no = len(comm.args), len(comm.out_shape)

    def body(*refs):
        ins, outs, scr = refs[:na], refs[na : na + no], refs[na + no :]
        comm.start(ins, outs, scr)
        comm.finish(ins, outs, scr)

    return pl.pallas_call(
        body,
        name=name,
        in_specs=[ANY] * na,
        out_specs=[ANY] * no,
        out_shape=comm.out_shape,
        input_output_aliases=comm.aliases,
        scratch_shapes=comm.scratch,
    )(*comm.args)


def _carry(body, comm, *, name, grid, in_specs, out_specs, out_shape, scratch_shapes, args):
    params = _cparams(("arbitrary",))
    if comm is None:
        res = pl.pallas_call(body, name=name, grid=grid, in_specs=in_specs, out_specs=out_specs, out_shape=out_shape,
                             scratch_shapes=scratch_shapes, compiler_params=params)(*args)
        return res, None
    ni, no, ns = len(in_specs), len(out_specs), len(scratch_shapes)
    ci, co = len(comm.args), len(comm.out_shape)

    def wrapped(*refs):
        ins, c_ins = refs[:ni], refs[ni : ni + ci]
        p = ni + ci
        outs, c_outs = refs[p : p + no], refs[p + no : p + no + co]
        p += no + co
        scr, c_scr = refs[p : p + ns], refs[p + ns :]

        @pl.when(pl.program_id(0) == 0)
        def _():
            comm.start(c_ins, c_outs, c_scr)

        body(*ins, *outs, *scr)

        @pl.when(pl.program_id(0) == grid[0] - 1)
        def _():
            comm.finish(c_ins, c_outs, c_scr)

    res = pl.pallas_call(
        wrapped,
        name=name + "_carry",
        grid=grid,
        in_specs=list(in_specs) + [ANY] * ci,
        out_specs=list(out_specs) + [ANY] * co,
        out_shape=list(out_shape) + list(comm.out_shape),
        input_output_aliases={ni + i: no + o for i, o in comm.aliases.items()},
        scratch_shapes=list(scratch_shapes) + list(comm.scratch),
        compiler_params=params,
    )(*args, *comm.args)
    return res[:no], res[no:]


def small_allreduce(buf):
    r = buf.shape[0]

    def body(b_ref, o_ref, slots, send, recv):
        x, y, c = lax.axis_index("x"), lax.axis_index("y"), lax.axis_index("c")
        me = 4 * x + 2 * y + c
        slots[me] = b_ref[...]
        cps = []
        peers = []
        for mask in range(1, N_DEV):
            fx, fy, fc = (mask >> 2) & 1, (mask >> 1) & 1, mask & 1
            px, py, pc = (1 - x if fx else x), (1 - y if fy else y), (1 - c if fc else c)
            peers.append(4 * px + 2 * py + pc)
            cps.append(_rcopy(b_ref, slots.at[me], send.at[mask - 1], recv.at[mask - 1], (px, py, pc)))
        for cp in cps:
            cp.start()
        for k, pid in enumerate(peers):
            landed = slots.at[pid]
            _rcopy(landed, landed, send.at[k], recv.at[k], (x, y, c)).wait_recv()
        for cp in cps:
            cp.wait_send()
        acc = slots[0]
        for i in range(1, N_DEV):
            acc = acc + slots[i]
        o_ref[...] = acc

    vm = pl.BlockSpec(memory_space=pltpu.VMEM)
    return pl.pallas_call(
        body,
        name="small_allreduce",
        in_specs=[vm],
        out_specs=vm,
        out_shape=jax.ShapeDtypeStruct(buf.shape, F32),
        scratch_shapes=[pltpu.VMEM((N_DEV, r, LANES), F32), pltpu.SemaphoreType.DMA((N_DEV - 1,)), pltpu.SemaphoreType.DMA((N_DEV - 1,))],
    )(buf)


WEIGHT_NAMES = ["ffn1_norm", "ffn1_w_gate", "ffn1_w_up", "ffn1_w_down", "mix_norm", "mem_norm", "w_mem_kv", "mem_q_gain",
                "mem_k_gain", "w_in_a", "hgrn_lb_logits", "hgrn_o_gain", "w_in_b", "fox_q_gain", "kv_norm", "w_kv", "fox_f_bias",
                "fox_k_gain", "w_out", "ffn2_norm", "ffn2_w_gate", "ffn2_w_up", "ffn2_w_down"]
SHARDED = ["ffn1_w_gate", "ffn1_w_up", "ffn1_w_down", "w_mem_kv", "w_in_a", "w_in_b", "w_kv", "w_out", "ffn2_w_gate", "ffn2_w_up", "ffn2_w_down"]
SMALL = [n for n in WEIGHT_NAMES if n not in SHARDED]
FFN1 = ["ffn1_w_gate", "ffn1_w_up", "ffn1_w_down"]
FFN2 = ["ffn2_w_gate", "ffn2_w_up", "ffn2_w_down"]
PER_LAYER = FFN1 + FFN2 + ["w_mem_kv", "w_out"]
TRANSPOSED = ["ffn1_w_gate", "ffn1_w_up", "ffn2_w_gate", "ffn2_w_up", "w_in_a", "w_in_b"]
N_LAYERS, N_A = 4, 2
KV_PAD = 13 * LANES


def _cols_from_chips(g):
    return jnp.moveaxis(g, 0, 2).reshape(g.shape[1], g.shape[2], N_CHIPS * g.shape[3])


def _rows_from_chips(g):
    return jnp.moveaxis(g, 0, 1).reshape(g.shape[1], N_CHIPS * g.shape[2], g.shape[3])


def _pair_tile(g):
    return jnp.tile(g, (1, 2)).reshape(g.shape[0], 1, LANES)


def _pair_fold(g):
    return g[:, :HEAD64] + g[:, HEAD64:]


def kernel(x, mem, ffn1_norm, ffn1_w_gate, ffn1_w_up, ffn1_w_down, mix_norm, mem_norm, w_mem_kv, mem_q_gain, mem_k_gain, w_in_a, hgrn_lb_logits, hgrn_o_gain, w_in_b, fox_q_gain, kv_norm, w_kv, fox_f_bias, fox_k_gain, w_out, ffn2_norm, ffn2_w_gate, ffn2_w_up, ffn2_w_down, loss_target, m_ffn1_norm, m_ffn1_w_gate, m_ffn1_w_up, m_ffn1_w_down, m_mix_norm, m_mem_norm, m_w_mem_kv, m_mem_q_gain, m_mem_k_gain, m_w_in_a, m_hgrn_lb_logits, m_hgrn_o_gain, m_w_in_b, m_fox_q_gain, m_kv_norm, m_w_kv, m_fox_f_bias, m_fox_k_gain, m_w_out, m_ffn2_norm, m_ffn2_w_gate, m_ffn2_w_up, m_ffn2_w_down, v_ffn1_norm, v_ffn1_w_gate, v_ffn1_w_up, v_ffn1_w_down, v_mix_norm, v_mem_norm, v_w_mem_kv, v_mem_q_gain, v_mem_k_gain, v_w_in_a, v_hgrn_lb_logits, v_hgrn_o_gain, v_w_in_b, v_fox_q_gain, v_kv_norm, v_w_kv, v_fox_f_bias, v_fox_k_gain, v_w_out, v_ffn2_norm, v_ffn2_w_gate, v_ffn2_w_up, v_ffn2_w_down):
    given = dict(locals())
    def oriented(n, t):
        return jnp.swapaxes(t, 1, 2) if n in TRANSPOSED else t

    w = {n: oriented(n, given[n]) for n in WEIGHT_NAMES}
    xs, mems, tgt = x[0], mem[0], loss_target[0]
    s, d = xs.shape
    my_chip = 2 * lax.axis_index("x") + lax.axis_index("y")
    sel = jnp.stack([my_chip, lax.axis_index("c")]).astype(jnp.int32)
    c_arr = sel[1:]

    def w_in_name(l):
        return "w_in_a" if l < N_A else "w_in_b"

    def halves_of(n):
        rows, cols = w[n].shape[-2:]
        return w[n].reshape(-1, 2, rows // 2, cols)

    def own_of(n, l):
        return 0 if w[n].ndim == 2 else (l - N_A if n == "w_in_b" else l)

    def view(buf, n):
        rows, cols = w[n].shape[-2:]
        return buf.reshape(N_CHIPS, rows, cols) if w[n].ndim == 2 else buf.reshape(N_CHIPS, 1, rows, cols)

    def mixer(l):
        return [(w_in_name(l), l), ("w_mem_kv", l), ("w_out", l)]

    first = [(n, 0) for n in PER_LAYER] + [("w_in_a", 0), ("w_kv", 0)]
    carried = {
        (0, "ffn1"): mixer(1), (0, "mix"): [(n, 1) for n in FFN1 + FFN2[:2]], (0, "ffn2"): [(FFN2[2], 1)],
        (1, "ffn1"): mixer(2), (1, "mix"): [(n, 2) for n in FFN1 + FFN2[:2]], (1, "ffn2"): [(FFN2[2], 2)],
        (2, "ffn1"): [(FFN1[0], 3)] + mixer(3)[1:], (2, "mix"): [(FFN1[1], 3), (FFN1[2], 3)], (2, "ffn2"): [(FFN2[0], 3)] + mixer(3)[:1],
        (3, "ffn1"): [(FFN2[1], 3)], (3, "mix"): [(FFN2[2], 3)],
    }
    bufs = {}
    every = first + [it for items in carried.values() for it in items]
    for layer in range(N_LAYERS):
        its = [it for it in every if it[1] == layer]
        srcs = [halves_of(n) for n, _ in its]
        cast = lambda idx: cast_into_slot([srcs[i] for i in idx], [own_of(*its[i]) for i in idx], sel, BF16)
        bufs.update(zip(its, _by_shape(srcs, cast)))
    lb_buf = cast_into_slot([hgrn_lb_logits.reshape(1, 2, 1, -1)], [0], sel, F32)[0]
    got0 = run_comm(Gather([bufs[it] for it in first] + [lb_buf]), "gather_layer0")
    got = {it: view(b, it[0]) for it, b in zip(first, got0[:-1])}
    w_kv_full = _cols_from_chips(got[("w_kv", 0)][:, None])
    w_kv_full = jnp.pad(w_kv_full, ((0, 0), (0, 0), (0, KV_PAD - w_kv_full.shape[-1])))
    logits3 = jnp.moveaxis(got0[-1].reshape(N_CHIPS, 2, -1), 0, 1).reshape(2, 1, -1)
    lb3 = lb_fwd(logits3)
    w_in, w_mkv, w_o = {}, {}, {}

    def gather_behind(key):
        items = carried.get(key)
        return None if items is None else Gather([bufs[it] for it in items])

    def landed(key, res):
        if res is not None:
            got.update({it: view(b, it[0]) for it, b in zip(carried[key], res)})

    norm3 = {n: w[n].reshape(N_LAYERS, 1, d) for n in ("ffn1_norm", "mix_norm", "mem_norm", "ffn2_norm")}
    kvn3 = kv_norm.reshape(1, 1, d)
    mqg3, mkg3 = _pair_tile(mem_q_gain), _pair_tile(mem_k_gain)
    og3 = hgrn_o_gain.reshape(N_A, 1, LANES)
    fqg3 = _pair_tile(fox_q_gain)
    fkg = jnp.tile(fox_k_gain, 2).reshape(1, LANES)
    fb = jnp.pad(fox_f_bias, (0, LANES - fox_f_bias.shape[0])).reshape(1, LANES)

    sv = [dict() for _ in range(N_LAYERS)]
    h = xs
    kv = None
    for l in range(N_LAYERS):
        t = sv[l]
        t["x0"] = h
        (h, t["a1"], t["b1"]), res = ffn_fwd(h, norm3["ffn1_norm"], l, *[got[(n, l)] for n in FFN1], 0, comm=gather_behind((l, "ffn1")))
        landed((l, "ffn1"), res)
        t["x1"] = h
        w_in[l] = _rows_from_chips(got[(w_in_name(l), l)])
        t["proj"] = proj_fwd(h, norm3["mix_norm"], l, w_in[l], 0, wt=True)
        if l < N_A:
            (main, t["o"]), res = hgrn_fwd(t["proj"], lb3, og3, l, comm=gather_behind((l, "mix")))
            t["qblk"] = 12
        else:
            (main, t["o"], t["lse"]), res = fox_fwd(t["proj"], kv["k"], kv["v"], kv["clf"], kv["clf_t"], fqg3, l - N_A, comm=gather_behind((l, "mix")))
            t["qblk"] = 6
        landed((l, "mix"), res)
        w_mkv[l], w_o[l] = _rows_from_chips(got[("w_mem_kv", l)]), _rows_from_chips(got[("w_out", l)])
        t["kvm"] = proj_fwd(mems, norm3["mem_norm"], l, w_mkv[l], 0)
        memo = memattn_fwd(t["proj"], t["qblk"], t["kvm"], mqg3, mkg3, l)
        t["mixed"] = jnp.concatenate([main, memo], axis=-1)
        h = mm_res(h, t["mixed"], w_o[l], 0)
        t["x2"] = h
        (h, t["a2"], t["b2"]), res = ffn_fwd(h, norm3["ffn2_norm"], l, *[got[(n, l)] for n in FFN2], 0, comm=gather_behind((l, "ffn2")))
        landed((l, "ffn2"), res)
        if l == N_A - 1:
            kv = {"x": h, "kvf": proj_fwd(h, kvn3, 0, w_kv_full, 0)}
            kv["k"], kv["v"], kv["clf"] = kvprep_fwd(kv["kvf"], fkg, fb)
            kv["clf_t"] = kv["clf"][:, :16].T

    loss_local, dx = loss_head(h, tgt)

    nc = N_CHIPS
    fc = ffn1_w_down.shape[1]
    gsplit = [dict() for _ in range(N_LAYERS)]

    def group_layout(l):
        lay = {n: b[0].reshape(2, nc, fc // 2, d) for n, b in gsplit[l].items()}
        lay16 = {n: b[1].reshape(2, nc, fc // 2, d) for n, b in gsplit[l].items()}
        for n, (g32, g16) in (("w_mem_kv", dw_mkv[l]), ("w_out", dw_o[l]), (w_in_name(l), dw_in[l])):
            lay[n], lay16[n] = g32, g16
        names = PER_LAYER + [w_in_name(l)]
        if l == N_A - 1:
            kv_cols = w_kv.shape[-1] * nc
            lay["w_kv"] = jnp.transpose(dw_kv[:, :kv_cols].reshape(2, d // 2, nc, kv_cols // nc), (0, 2, 1, 3))
            names = names + ["w_kv"]
        return names, [lay[n] for n in names], [lay16[n] if n in lay16 else bf(lay[n]) for n in names]

    def pair_sums(gl, recv):
        return _by_shape(gl, lambda idx: pair_sum([gl[i] for i in idx], [recv[i] for i in idx], c_arr))

    def chip_sums(ps, qs):
        return _by_shape(ps, lambda idx: chip_sum([ps[i] for i in idx], [qs[i] for i in idx], sel))

    n_ffn = len(FFN1) + len(FFN2)
    riding = {l: l + 1 for l in range(N_LAYERS - 1)}
    reduced = {}
    unshared = None
    dw_in, dw_o, dw_mkv = [None] * N_LAYERS, [None] * N_LAYERS, [None] * N_LAYERS
    sg = {n: [None] * N_LAYERS for n in ("ffn1_norm", "mix_norm", "mem_norm", "ffn2_norm", "mem_q_gain", "mem_k_gain")}
    sg["hgrn_o_gain"], sg["fox_q_gain"], dlb = [None] * N_A, [None] * (N_LAYERS - N_A), [None] * N_A
    dk_sh = jnp.zeros((s, KV_MAIN), F32)
    dv_sh = jnp.zeros((s, KV_MAIN), F32)
    dclf = jnp.zeros((s, LANES), F32)
    zero_mem = jnp.zeros(mems.shape, F32)
    dw_kv = None
    for l in reversed(range(N_LAYERS)):
        t = sv[l]
        if l == N_A - 1:
            dkvf, dfkg, dfb = kvprep_bwd(kv["kvf"], fkg, fb, dk_sh, dv_sh, dclf)
            dx, sg["kv_norm"], xn_kv, dpb = proj_bwd(kv["x"], kvn3, 0, [dkvf], w_kv_full, 0, dx)
            dw_kv = wgrad(xn_kv, dpb)
        ride = riding.get(l)
        comms = []
        if unshared is not None:
            comms.append(PairShare(unshared[2]))
        if ride is not None:
            names_r, gl_r, gl16_r = group_layout(ride)
            comms.append(PairExchange(gl16_r))
        comm = Multi(comms) if comms else None
        (dx, da, db, hm, xn, dyb, sg["ffn2_norm"][l]), res = ffn_bwd(t["x2"], norm3["ffn2_norm"], l, dx, t["a2"], t["b2"], *[got[(n, l)] for n in FFN2], 0, comm=comm)
        if comm is not None:
            res = comm.split(res)
            if unshared is not None:
                reduced[unshared[0]] = dict(zip(unshared[1], res.pop(0)))
                unshared = None
            if ride is not None:
                partial_r = pair_sums(gl_r, res.pop(0))

        def ffn_wgrads(which, da, db, hm, xn, dyb):
            for n, (a_, b_) in zip(which, ((da, xn), (db, xn), (hm, dyb))):
                gsplit[l][n] = wgrad(a_, b_, split=True)

        ffn_wgrads(FFN2, da, db, hm, xn, dyb)
        dmixed, dxb = mm_nt(dx, w_o[l], 0)
        dw_o[l] = wgrad(t["mixed"], dxb, chip_rows=w_out.shape[1])
        dqm, dkvm, dmq, dmk = memattn_bwd(t["proj"], t["qblk"], t["kvm"], mqg3, mkg3, l, dmixed)
        sg["mem_q_gain"][l], sg["mem_k_gain"][l] = _pair_fold(dmq), _pair_fold(dmk)
        _, sg["mem_norm"][l], memn, dkvmb = proj_bwd(mems, norm3["mem_norm"], l, [dkvm], w_mkv[l], 0, zero_mem)
        dw_mkv[l] = wgrad(memn, dkvmb, chip_rows=w_mem_kv.shape[1])
        comm = ChipExchange(partial_r[:n_ffn]) if ride is not None else None
        if l < N_A:
            (dzq, dzf, dvi, dzg, dlb[l], sg["hgrn_o_gain"][l]), res = hgrn_bwd(t["proj"], lb3, og3, l, t["o"], dmixed, comm=comm)
            parts, tmw = [dzq, dzf, dvi, dzg, dqm], 13 * LANES
        else:
            lse_t = t["lse"].reshape(s, 6, LANES)[:, :, :2].reshape(s, 12).T
            lse_t = jnp.pad(lse_t, ((0, 4), (0, 0)))
            (dq, dgate, dk_sh, dv_sh, dclf, dfq), res = fox_bwd(t["proj"], kv["k"], kv["v"], kv["clf"], kv["clf_t"], fqg3, l - N_A, t["o"], t["lse"], lse_t, dmixed, dk_sh, dv_sh, dclf, comm=comm)
            sg["fox_q_gain"][l - N_A] = _pair_fold(dfq)
            parts, tmw = [dq, dgate, dqm], 7 * LANES
        if ride is not None:
            landed_r = list(res)
        dx, sg["mix_norm"][l], hn, dpb = proj_bwd(t["x1"], norm3["mix_norm"], l, parts, w_in[l], 0, dx, wt=True)
        dw_in[l] = wgrad(dpb, hn, tn=d // 2, tm=tmw, chip_rows=tmw // 2)
        comm = ChipExchange(partial_r[n_ffn:]) if ride is not None else None
        (dx, da, db, hm, xn, dyb, sg["ffn1_norm"][l]), res = ffn_bwd(t["x0"], norm3["ffn1_norm"], l, dx, t["a1"], t["b1"], *[got[(n, l)] for n in FFN1], 0, comm=comm)
        if ride is not None:
            unshared = (ride, names_r, chip_sums(partial_r, landed_r + list(res)))
        ffn_wgrads(FFN1, da, db, hm, xn, dyb)

    names0, gl0, gl16_0 = group_layout(0)
    recv0 = run_comm(PairExchange(gl16_0), "pair_exchange")
    partial0 = pair_sums(gl0, recv0)
    landed0 = run_comm(ChipExchange(partial0), "chip_exchange")
    mine0 = chip_sums(partial0, list(landed0))
    both = run_comm(PairShare(unshared[2] + mine0), "pair_share")
    reduced[unshared[0]] = dict(zip(unshared[1], both[: len(unshared[1])]))
    reduced[0] = dict(zip(names0, both[len(unshared[1]) :]))
    gparts = {n: [reduced[l][n] for l in range(N_LAYERS)] for n in PER_LAYER}
    gparts["w_in_a"] = [reduced[l]["w_in_a"] for l in range(N_A)]
    gparts["w_in_b"] = [reduced[l]["w_in_b"] for l in range(N_A, N_LAYERS)]
    gparts["w_kv"] = [reduced[N_A - 1]["w_kv"]]

    dlogits = lb_bwd(logits3, dlb[1]).reshape(2, -1)
    small = {
        "ffn1_norm": jnp.concatenate(sg["ffn1_norm"]), "mix_norm": jnp.concatenate(sg["mix_norm"]),
        "mem_norm": jnp.concatenate(sg["mem_norm"]), "ffn2_norm": jnp.concatenate(sg["ffn2_norm"]),
        "mem_q_gain": jnp.concatenate(sg["mem_q_gain"]), "mem_k_gain": jnp.concatenate(sg["mem_k_gain"]),
        "hgrn_o_gain": jnp.concatenate(sg["hgrn_o_gain"]), "fox_q_gain": jnp.concatenate(sg["fox_q_gain"]),
        "kv_norm": sg["kv_norm"], "fox_f_bias": dfb[:, : fox_f_bias.shape[0]], "fox_k_gain": _pair_fold(dfkg),
        "hgrn_lb_logits": dlogits,
    }
    flat = [small[n].reshape(-1) for n in SMALL] + [loss_local.reshape(-1)]
    sizes = [f.shape[0] for f in flat]
    total = sum(sizes)
    padded = -(-total // (8 * LANES)) * (8 * LANES)
    packed = jnp.pad(jnp.concatenate(flat), (0, padded - total)).reshape(-1, LANES)
    summed = small_allreduce(packed).reshape(-1)
    off = 0
    for n, sz in zip(SMALL, sizes[:-1]):
        gparts[n] = [summed[off : off + sz].reshape(dlogits.shape if n == "hgrn_lb_logits" else w[n].shape)]
        off += sz
    loss = summed[off]
    lbw = hgrn_lb_logits.shape[1]
    gparts["hgrn_lb_logits"] = [lax.dynamic_slice_in_dim(gparts["hgrn_lb_logits"][0], my_chip * lbw, lbw, axis=1)]

    def update(idx):
        ns = [WEIGHT_NAMES[i] for i in idx]
        return adamw([w[n] for n in ns], [gparts[n] for n in ns], [oriented(n, given["m_" + n]) for n in ns],
                     [oriented(n, given["v_" + n]) for n in ns])

    grads, delta, new_m, new_v = {}, {}, {}, {}
    for n, res in zip(WEIGHT_NAMES, _by_shape([w[n] for n in WEIGHT_NAMES], update)):
        grads[n], delta[n], new_m[n], new_v[n] = (oriented(n, t) for t in res)
    return (loss, dx[None], *[grads[n] for n in WEIGHT_NAMES], *[delta[n] for n in WEIGHT_NAMES],
            *[new_m[n] for n in WEIGHT_NAMES], *[new_v[n] for n in WEIGHT_NAMES])
```

```python
import jax
import jax.numpy as jnp
from jax import lax
from jax.experimental import pallas as pl
from jax.experimental.pallas import tpu as pltpu

F32, BF16 = jnp.float32, jnp.bfloat16
HI = lax.Precision.HIGHEST
EPS = 1e-6
MESH = pl.DeviceIdType.MESH
ANY = pl.BlockSpec(memory_space=pl.ANY)

VMEM_LIMIT_BYTES = 56 << 20
N_CHIPS = 4
N_DEV = 8
LANES = 128
HEAD64 = 64
CHUNK = 128
SUB = 32
HGRN_HEADS_PER_STEP = 2
TQ = 256
TOK = 256

ADAM_LR, ADAM_B1, ADAM_B2, ADAM_EPS, ADAM_WD, ADAM_STEP = 0.001, 0.9, 0.999, 1e-08, 0.01, 10


def _cparams(sem=None, **kw):
    return pltpu.CompilerParams(dimension_semantics=sem, vmem_limit_bytes=VMEM_LIMIT_BYTES, **kw)


def _mm(a, b, dims, prec=None):
    return lax.dot_general(a, b, (dims, ((), ())), preferred_element_type=F32, precision=prec)


def dot_nn(a, b, prec=None):
    return _mm(a, b, ((1,), (0,)), prec)


def dot_nt(a, b, prec=None):
    return _mm(a, b, ((1,), (1,)), prec)


def dot_tn(a, b, prec=None):
    return _mm(a, b, ((0,), (0,)), prec)


def bf(v):
    return v.astype(BF16)


def _sigmoid(z):
    return jax.nn.sigmoid(z)


def _dsilu(z, s):
    return s * (1.0 + z * (1.0 - s))


def _rms(x):
    r = lax.rsqrt(jnp.mean(x * x, axis=-1, keepdims=True) + EPS)
    return x * r, r


def _rms_bwd(dxn, u, r, g):
    du = dxn * g
    dx = r * (du - u * jnp.mean(du * u, axis=-1, keepdims=True))
    return dx, jnp.sum(dxn * u, axis=0, keepdims=True)


def _lane_mask0(shape):
    return lax.broadcasted_iota(jnp.int32, shape, len(shape) - 1) < HEAD64


def _rms64(x, m0):
    sq = x * x
    s0 = jnp.sum(jnp.where(m0, sq, 0.0), axis=-1, keepdims=True)
    s1 = jnp.sum(jnp.where(m0, 0.0, sq), axis=-1, keepdims=True)
    r = lax.rsqrt(jnp.where(m0, s0, s1) * (1.0 / HEAD64) + EPS)
    return x * r, r


def _rms64_bwd(dxn, u, r, g, m0):
    du = dxn * g
    t = du * u
    t0 = jnp.sum(jnp.where(m0, t, 0.0), axis=-1, keepdims=True)
    t1 = jnp.sum(jnp.where(m0, 0.0, t), axis=-1, keepdims=True)
    dx = r * (du - u * (jnp.where(m0, t0, t1) * (1.0 / HEAD64)))
    return dx, jnp.sum(dxn * u, axis=0, keepdims=True)


def _tok(s):
    return TOK if s % TOK == 0 else s


def _const(shape):
    return pl.BlockSpec(shape, lambda *_: (0,) * len(shape))


def ffn_fwd(x, gain3, l, wg, wu, wd, wl, comm=None):
    s, d = x.shape
    nc, _, fc, _ = wg.shape
    tm = _tok(s)

    def body(x_ref, g_ref, wg_ref, wu_ref, wd_ref, xo_ref, a_ref, b_ref):
        xv = x_ref[...]
        u, _ = _rms(xv)
        xn = bf(u * g_ref[...])
        y = jnp.zeros((tm, d), F32)
        for c in range(nc):
            a = dot_nt(xn, wg_ref[c])
            b = dot_nt(xn, wu_ref[c])
            a_ref[c] = bf(a)
            b_ref[c] = bf(b)
            y = y + dot_nn(bf(a * _sigmoid(a) * b), wd_ref[c])
        xo_ref[...] = xv + 0.5 * y

    wspec = pl.BlockSpec((nc, None, fc, d), lambda i: (0, wl, 0, 0), pipeline_mode=pl.Buffered(1))
    wdspec = pl.BlockSpec((nc, None, fc, d), lambda i: (0, wl, 0, 0), pipeline_mode=pl.Buffered(1))
    row = pl.BlockSpec((tm, d), lambda i: (i, 0))
    act = pl.BlockSpec((nc, tm, fc), lambda i: (0, i, 0))
    return _carry(
        body,
        comm,
        name="ffn_fwd",
        grid=(s // tm,),
        in_specs=[row, pl.BlockSpec((None, 1, d), lambda i: (l, 0, 0)), wspec, wspec, wdspec],
        out_specs=[row, act, act],
        out_shape=[
            jax.ShapeDtypeStruct((s, d), F32),
            jax.ShapeDtypeStruct((nc, s, fc), BF16),
            jax.ShapeDtypeStruct((nc, s, fc), BF16),
        ],
        scratch_shapes=[],
        args=(x, gain3, wg, wu, wd),
    )


def ffn_bwd(x, gain3, l, dout, a, b, wg, wu, wd, wl, comm=None):
    s, d = x.shape
    nc, _, fc, _ = wg.shape
    tm = _tok(s)

    def body(x_ref, g_ref, do_ref, a_ref, b_ref, wg_ref, wu_ref, wd_ref, dx_ref, da_ref, db_ref, hm_ref, xn_ref, dy_ref, dg_ref):
        xv = x_ref[...]
        g = g_ref[...]
        u, r = _rms(xv)
        xn_ref[...] = bf(u * g)
        dout = do_ref[...]
        dy = bf(0.5 * dout)
        dy_ref[...] = dy
        dxn = jnp.zeros((tm, d), F32)
        for c in range(nc):
            av = a_ref[c].astype(F32)
            bv = b_ref[c].astype(F32)
            sg = _sigmoid(av)
            sl = av * sg
            dh = dot_nt(dy, wd_ref[c])
            da = bf(dh * bv * _dsilu(av, sg))
            db = bf(dh * sl)
            da_ref[c] = da
            db_ref[c] = db
            hm_ref[c] = bf(sl * bv)
            dxn = dxn + dot_nn(da, wg_ref[c]) + dot_nn(db, wu_ref[c])
        dx, dg = _rms_bwd(dxn, u, r, g)
        dx_ref[...] = dout + dx

        @pl.when(pl.program_id(0) == 0)
        def _():
            dg_ref[...] = jnp.zeros_like(dg_ref)

        dg_ref[...] += dg

    wspec = pl.BlockSpec((nc, None, fc, d), lambda i: (0, wl, 0, 0), pipeline_mode=pl.Buffered(1))
    wdspec = pl.BlockSpec((nc, None, fc, d), lambda i: (0, wl, 0, 0), pipeline_mode=pl.Buffered(1))
    row = pl.BlockSpec((tm, d), lambda i: (i, 0))
    act = pl.BlockSpec((nc, tm, fc), lambda i: (0, i, 0))
    act_shape = jax.ShapeDtypeStruct((nc, s, fc), BF16)
    return _carry(
        body,
        comm,
        name="ffn_bwd",
        grid=(s // tm,),
        in_specs=[row, pl.BlockSpec((None, 1, d), lambda i: (l, 0, 0)), row, act, act, wspec, wspec, wdspec],
        out_specs=[row, act, act, act, row, row, _const((1, d))],
        out_shape=[
            jax.ShapeDtypeStruct((s, d), F32),
            act_shape,
            act_shape,
            act_shape,
            jax.ShapeDtypeStruct((s, d), BF16),
            jax.ShapeDtypeStruct((s, d), BF16),
            jax.ShapeDtypeStruct((1, d), F32),
        ],
        scratch_shapes=[],
        args=(x, gain3, dout, a, b, wg, wu, wd),
    )


def wgrad(a, b, tn=None, tm=None, split=False, chip_rows=None):
    ca = a.shape[0] if a.ndim == 3 else 1
    cb = b.shape[0] if b.ndim == 3 else 1
    nc = max(ca, cb)
    s, m = a.shape[-2:]
    n = b.shape[-1]
    tn = n if tn is None else tn
    assert n % tn == 0
    tm = m if tm is None else tm
    per_tile = None if chip_rows is None else tm // chip_rows

    def body(*refs):
        a_ref, b_ref = refs[0], refs[1]
        res = dot_tn(a_ref[...], b_ref[...])
        if split:
            for o in refs[2:]:
                o[0] = res[: m // 2].astype(o.dtype)
                o[1] = res[m // 2 :].astype(o.dtype)
        elif chip_rows is not None:
            hr = chip_rows // 2
            for o in refs[2:]:
                for k in range(per_tile):
                    for hf in range(2):
                        r0 = k * chip_rows + hf * hr
                        o[hf, k] = res[r0 : r0 + hr].astype(o.dtype)
        else:
            refs[2][...] = res

    params = _cparams(("arbitrary", "arbitrary"))
    if not split:
        assert nc == 1 and a.ndim == 2 and b.ndim == 2 and m % tm == 0
        in_specs = [pl.BlockSpec((s, tm), lambda i, j: (0, i)), pl.BlockSpec((s, tn), lambda i, j: (0, j))]
        if chip_rows is None:
            return pl.pallas_call(
                body,
                name="wgrad",
                grid=(m // tm, n // tn),
                in_specs=in_specs,
                out_specs=pl.BlockSpec((tm, tn), lambda i, j: (i, j)),
                out_shape=jax.ShapeDtypeStruct((m, n), F32),
                compiler_params=params,
            )(a, b)
        assert tm % chip_rows == 0
        laid = pl.BlockSpec((2, per_tile, chip_rows // 2, tn), lambda i, j: (0, i, 0, j))
        shape = (2, m // chip_rows, chip_rows // 2, n)
        return pl.pallas_call(
            body,
            name="wgrad_chips",
            grid=(m // tm, n // tn),
            in_specs=in_specs,
            out_specs=[laid, laid],
            out_shape=[jax.ShapeDtypeStruct(shape, F32), jax.ShapeDtypeStruct(shape, BF16)],
            compiler_params=params,
        )(a, b)
    a_spec = pl.BlockSpec((None, s, m), lambda c, j: (c, 0, 0)) if a.ndim == 3 else pl.BlockSpec((s, m), lambda c, j: (0, 0))
    b_spec = pl.BlockSpec((None, s, tn), lambda c, j: (c, 0, j)) if b.ndim == 3 else pl.BlockSpec((s, tn), lambda c, j: (0, j))
    halves = pl.BlockSpec((2, None, None, m // 2, tn), lambda c, j: (0, c, 0, 0, j))
    return pl.pallas_call(
        body,
        name="wgrad_split",
        grid=(nc, n // tn),
        in_specs=[a_spec, b_spec],
        out_specs=[halves, halves],
        out_shape=[jax.ShapeDtypeStruct((2, nc, 1, m // 2, n), F32), jax.ShapeDtypeStruct((2, nc, 1, m // 2, n), BF16)],
        compiler_params=params,
    )(a, b)


def proj_fwd(x, gain3, l, w, wl, wt=False):
    s, d = x.shape
    n = w.shape[1] if wt else w.shape[2]
    tm = _tok(s)

    def body(x_ref, g_ref, w_ref, o_ref):
        u, _ = _rms(x_ref[...])
        xn = bf(u * g_ref[...])
        o_ref[...] = dot_nt(xn, w_ref[...]) if wt else dot_nn(xn, w_ref[...])

    return pl.pallas_call(
        body,
        name="proj_fwd",
        grid=(s // tm,),
        in_specs=[
            pl.BlockSpec((tm, d), lambda i: (i, 0)),
            pl.BlockSpec((None, 1, d), lambda i: (l, 0, 0)),
            pl.BlockSpec((None,) + w.shape[1:], lambda i: (wl, 0, 0)),
        ],
        out_specs=pl.BlockSpec((tm, n), lambda i: (i, 0)),
        out_shape=jax.ShapeDtypeStruct((s, n), F32),
        compiler_params=_cparams(("arbitrary",)),
    )(x, gain3, w)


def proj_bwd(x, gain3, l, parts, w, wl, dx_in, wt=False):
    s, d = x.shape
    n = w.shape[1] if wt else w.shape[2]
    widths = [p.shape[1] for p in parts]
    assert sum(widths) == n
    tm = _tok(s)
    npart = len(parts)

    def body(*refs):
        x_ref, g_ref, w_ref, dxin_ref = refs[:4]
        p_refs = refs[4 : 4 + npart]
        dx_ref, dg_ref, xn_ref, dpb_ref = refs[4 + npart :]
        g = g_ref[...]
        u, r = _rms(x_ref[...])
        xn_ref[...] = bf(u * g)
        dxn = jnp.zeros((tm, d), F32)
        off = 0
        for p_ref, wd_ in zip(p_refs, widths):
            dp = bf(p_ref[...])
            dpb_ref[:, off : off + wd_] = dp
            dxn = dxn + (dot_nn(dp, w_ref[off : off + wd_, :]) if wt else dot_nt(dp, w_ref[:, off : off + wd_]))
            off += wd_
        dx, dg = _rms_bwd(dxn, u, r, g)
        dx_ref[...] = dxin_ref[...] + dx

        @pl.when(pl.program_id(0) == 0)
        def _():
            dg_ref[...] = jnp.zeros_like(dg_ref)

        dg_ref[...] += dg

    row = pl.BlockSpec((tm, d), lambda i: (i, 0))
    return pl.pallas_call(
        body,
        name="proj_bwd",
        grid=(s // tm,),
        in_specs=[row, pl.BlockSpec((None, 1, d), lambda i: (l, 0, 0)), pl.BlockSpec((None,) + w.shape[1:], lambda i: (wl, 0, 0)), row]
        + [pl.BlockSpec((tm, wd_), lambda i: (i, 0)) for wd_ in widths],
        out_specs=[row, _const((1, d)), row, pl.BlockSpec((tm, n), lambda i: (i, 0))],
        out_shape=[
            jax.ShapeDtypeStruct((s, d), F32),
            jax.ShapeDtypeStruct((1, d), F32),
            jax.ShapeDtypeStruct((s, d), BF16),
            jax.ShapeDtypeStruct((s, n), BF16),
        ],
        compiler_params=_cparams(("arbitrary",)),
    )(x, gain3, w, dx_in, *parts)


def mm_res(x, a, w, l):
    s, d = x.shape
    k = a.shape[1]
    tm = _tok(s)

    def body(x_ref, a_ref, w_ref, o_ref):
        o_ref[...] = x_ref[...] + dot_nn(a_ref[...], w_ref[...])

    return pl.pallas_call(
        body,
        name="mm_res",
        grid=(s // tm,),
        in_specs=[
            pl.BlockSpec((tm, d), lambda i: (i, 0)),
            pl.BlockSpec((tm, k), lambda i: (i, 0)),
            pl.BlockSpec((None, k, d), lambda i: (l, 0, 0)),
        ],
        out_specs=pl.BlockSpec((tm, d), lambda i: (i, 0)),
        out_shape=jax.ShapeDtypeStruct((s, d), F32),
        compiler_params=_cparams(("arbitrary",)),
    )(x, a, w)


def mm_nt(dx, w, l):
    s, d = dx.shape
    k = w.shape[1]
    tm = _tok(s)

    def body(dx_ref, w_ref, o_ref, dxb_ref):
        dxb = bf(dx_ref[...])
        dxb_ref[...] = dxb
        o_ref[...] = dot_nt(dxb, w_ref[...])

    return pl.pallas_call(
        body,
        name="mm_nt",
        grid=(s // tm,),
        in_specs=[pl.BlockSpec((tm, d), lambda i: (i, 0)), pl.BlockSpec((None, k, d), lambda i: (l, 0, 0))],
        out_specs=[pl.BlockSpec((tm, k), lambda i: (i, 0)), pl.BlockSpec((tm, d), lambda i: (i, 0))],
        out_shape=[jax.ShapeDtypeStruct((s, k), F32), jax.ShapeDtypeStruct((s, d), BF16)],
        compiler_params=_cparams(("arbitrary",)),
    )(dx, w)


def lb_fwd(logits3):
    def body(l_ref, o_ref):
        l0, l1 = l_ref[0], l_ref[1]
        m = jnp.maximum(l0, l1)
        e0, e1 = jnp.exp(l0 - m), jnp.exp(l1 - m)
        p0, p1 = e0 / (e0 + e1), e1 / (e0 + e1)
        o_ref[0] = p0 - p0
        o_ref[1] = (p0 + p1) - p0

    return pl.pallas_call(body, name="lb_fwd", out_shape=jax.ShapeDtypeStruct(logits3.shape, F32))(logits3)


def lb_bwd(logits3, dlb1):
    def body(l_ref, d_ref, o_ref):
        l0, l1 = l_ref[0], l_ref[1]
        m = jnp.maximum(l0, l1)
        e0, e1 = jnp.exp(l0 - m), jnp.exp(l1 - m)
        p0, p1 = e0 / (e0 + e1), e1 / (e0 + e1)
        t = d_ref[...] * p0 * p1
        o_ref[0] = -t
        o_ref[1] = t

    return pl.pallas_call(body, name="lb_bwd", out_shape=jax.ShapeDtypeStruct(logits3.shape, F32))(logits3, dlb1)


def _hgrn_gates(zq, zf, lb):
    sf = _sigmoid(zf)
    f = lb + (1.0 - lb) * sf
    sq = _sigmoid(zq)
    return sf, f, jnp.log(f), 1.0 - f, sq, zq * sq


def _tri(n, upper=False):
    r = lax.broadcasted_iota(jnp.int32, (n, n), 0)
    c = lax.broadcasted_iota(jnp.int32, (n, n), 1)
    return jnp.where((c >= r) if upper else (r >= c), 1.0, 0.0).astype(F32)


def hgrn_fwd(proj, lb3, og3, l, comm=None):
    s = proj.shape[0]
    nh = 6
    n_chunk = s // CHUNK
    nsub = CHUNK // SUB

    hb = HGRN_HEADS_PER_STEP
    wide = hb * LANES

    def body(zq_ref, zf_ref, vi_ref, zg_ref, lb_ref, og_ref, main_ref, o_ref, q_a, k_a, v_a, c_a):
        og = og_ref[...]
        tril = _tri(CHUNK)
        rowi = lax.broadcasted_iota(jnp.int32, (SUB, LANES), 0)

        def one_head(hd, rows, st):
            cols = slice(hd * LANES, (hd + 1) * LANES)
            q_s, k_s, v_s, c_s = q_a.at[hd], k_a.at[hd], v_a.at[hd], c_a.at[hd]
            zg = zg_ref[rows, cols]
            _, _, lf, k, _, q = _hgrn_gates(zq_ref[rows, cols], zf_ref[rows, cols], lb_ref[:, cols])
            v = vi_ref[rows, cols]
            c = dot_nn(tril, lf, HI)
            q_s[...] = q
            k_s[...] = k
            v_s[...] = v
            c_s[...] = c
            o_inter = dot_nt(q * jnp.exp(c), st, HI)
            parts = []
            for i in range(nsub):
                lo = i * SUB
                blk = pl.ds(lo, SUB)
                qb, cb = q_s[blk, :], c_s[blk, :]
                ob = o_inter[lo : lo + SUB]
                if i > 0:
                    rr = c_s[pl.ds(lo - 1, 1), :]
                    qt = qb * jnp.exp(cb - rr)
                    kt = k_s[pl.ds(0, lo), :] * jnp.exp(rr - c_s[pl.ds(0, lo), :])
                    ob = ob + dot_nn(dot_nt(qt, kt, HI), v_s[pl.ds(0, lo), :], HI)
                for t in range(SUB):
                    e = jnp.where(rowi >= t, jnp.exp(cb - c_s[pl.ds(lo + t, 1), :]), 0.0)
                    a = jnp.sum(qb * k_s[pl.ds(lo + t, 1), :] * e, axis=-1, keepdims=True)
                    ob = ob + a * v_s[pl.ds(lo + t, 1), :]
                parts.append(ob)
            o = jnp.concatenate(parts, axis=0)
            ce = c_s[pl.ds(CHUNK - 1, 1), :]
            st = st * jnp.exp(ce) + dot_tn(v, k * jnp.exp(ce - c), HI)
            on, _ = _rms(o)
            o_ref[rows, cols] = o
            main_ref[rows, cols] = bf(on * og * (zg * _sigmoid(zg)))
            return st

        def chunk(ci, sts):
            rows = pl.ds(pl.multiple_of(ci * CHUNK, CHUNK), CHUNK)
            return tuple(one_head(hd, rows, sts[hd]) for hd in range(hb))

        lax.fori_loop(0, n_chunk, chunk, tuple(jnp.zeros((LANES, LANES), F32) for _ in range(hb)))

    def col(k):
        return pl.BlockSpec((s, wide), lambda h: (0, k * (nh // hb) + h))

    vec = pl.BlockSpec((None, 1, wide), lambda h: (l, 0, h))
    return _carry(
        body,
        comm,
        name="hgrn_fwd",
        grid=(nh // hb,),
        in_specs=[col(0), col(1), col(2), col(3), vec, pl.BlockSpec((None, 1, LANES), lambda h: (l, 0, 0))],
        out_specs=[pl.BlockSpec((s, wide), lambda h: (0, h))] * 2,
        out_shape=[jax.ShapeDtypeStruct((s, nh * LANES), BF16), jax.ShapeDtypeStruct((s, nh * LANES), F32)],
        scratch_shapes=[pltpu.VMEM((hb, CHUNK, LANES), F32)] * 4,
        args=(proj, proj, proj, proj, lb3, og3),
    )


def hgrn_bwd(proj, lb3, og3, l, o, dmixed, comm=None):
    s = proj.shape[0]
    nh = 6
    n_chunk = s // CHUNK
    nsub = CHUNK // SUB

    hb = HGRN_HEADS_PER_STEP
    wide = hb * LANES

    def body(zq_ref, zf_ref, vi_ref, zg_ref, lb_ref, og_ref, o_ref, dm_ref,
             dzq_ref, dzf_ref, dvi_ref, dzg_ref, dlb_ref, dog_ref,
             st_a, q_a, k_a, v_a, c_a, do_a, dq_a, dk_a, dv_a, acc_a):
        og = og_ref[...]
        tril = _tri(CHUNK)
        triu = _tri(CHUNK, upper=True)
        rowi = lax.broadcasted_iota(jnp.int32, (SUB, LANES), 0)

        def fwd_head(hd, ci, rows, st):
            cols = slice(hd * LANES, (hd + 1) * LANES)
            _, _, lf, k, _, _ = _hgrn_gates(zq_ref[rows, cols], zf_ref[rows, cols], lb_ref[:, cols])
            c = dot_nn(tril, lf, HI)
            ce = jnp.sum(lf, axis=0, keepdims=True)
            st_a[hd, ci] = st
            return st * jnp.exp(ce) + dot_tn(vi_ref[rows, cols], k * jnp.exp(ce - c), HI)

        def fwd_chunk(ci, sts):
            rows = pl.ds(pl.multiple_of(ci * CHUNK, CHUNK), CHUNK)
            return tuple(fwd_head(hd, ci, rows, sts[hd]) for hd in range(hb))

        lax.fori_loop(0, n_chunk, fwd_chunk, tuple(jnp.zeros((LANES, LANES), F32) for _ in range(hb)))
        acc_a[...] = jnp.zeros_like(acc_a)

        def bwd_head(hd, ci, rows, carry):
            dst, cg = carry
            cols = slice(hd * LANES, (hd + 1) * LANES)
            q_s, k_s, v_s, c_s, do_s = q_a.at[hd], k_a.at[hd], v_a.at[hd], c_a.at[hd], do_a.at[hd]
            dq_s, dk_s, dv_s, acc_s = dq_a.at[hd], dk_a.at[hd], dv_a.at[hd], acc_a.at[hd]
            lb = lb_ref[:, cols]
            zq, zf, zg = zq_ref[rows, cols], zf_ref[rows, cols], zg_ref[rows, cols]
            sf, f, lf, k, sq, q = _hgrn_gates(zq, zf, lb)
            v = vi_ref[rows, cols]
            c = dot_nn(tril, lf, HI)
            st = st_a[hd, ci]
            on, r = _rms(o_ref[rows, cols])
            sg = _sigmoid(zg)
            dmain = dm_ref[rows, cols]
            dy = dmain * (zg * sg)
            dzg_ref[rows, cols] = dmain * (on * og) * _dsilu(zg, sg)
            do, dog = _rms_bwd(dy, on, r, og)
            acc_s[pl.ds(0, 1), :] += dog
            q_s[...] = q
            k_s[...] = k
            v_s[...] = v
            c_s[...] = c
            do_s[...] = do
            ce = c_s[pl.ds(CHUNK - 1, 1), :]
            eq = jnp.exp(c)
            ek = jnp.exp(ce - c)
            qt_all = q * eq
            dq_s[...] = dot_nn(do, st, HI) * eq
            dv_s[...] = dot_nt(k * ek, dst, HI)
            dk_s[...] = dot_nn(v, dst, HI) * ek
            dst = dst * jnp.exp(ce) + dot_tn(do, qt_all, HI)
            for i in range(nsub):
                lo = i * SUB
                blk = pl.ds(lo, SUB)
                qb, cb, dob = q_s[blk, :], c_s[blk, :], do_s[blk, :]
                if i > 0:
                    prev = pl.ds(0, lo)
                    rr = c_s[pl.ds(lo - 1, 1), :]
                    eqi = jnp.exp(cb - rr)
                    eki = jnp.exp(rr - c_s[prev, :])
                    qt = qb * eqi
                    kt = k_s[prev, :] * eki
                    amat = dot_nt(qt, kt, HI)
                    damat = dot_nt(dob, v_s[prev, :], HI)
                    dv_s[prev, :] += dot_tn(amat, dob, HI)
                    dq_s[blk, :] += dot_nn(damat, kt, HI) * eqi
                    dk_s[prev, :] += dot_tn(damat, qt, HI) * eki
                dqb = jnp.zeros((SUB, LANES), F32)
                for t in range(SUB):
                    row = pl.ds(lo + t, 1)
                    e = jnp.where(rowi >= t, jnp.exp(cb - c_s[row, :]), 0.0)
                    kr = k_s[row, :]
                    a = jnp.sum(qb * kr * e, axis=-1, keepdims=True)
                    da = jnp.sum(dob * v_s[row, :], axis=-1, keepdims=True)
                    dv_s[row, :] += jnp.sum(a * dob, axis=0, keepdims=True)
                    dqb = dqb + da * kr * e
                    dk_s[row, :] += jnp.sum(da * qb * e, axis=0, keepdims=True)
                dq_s[blk, :] += dqb
            dq, dk = dq_s[...], dk_s[...]
            dg = q * dq - k * dk
            dlf = dot_nn(triu, dg, HI) + cg
            cg = cg + jnp.sum(dg, axis=0, keepdims=True)
            df = dlf / f - dk
            dzf_ref[rows, cols] = df * (1.0 - lb) * sf * (1.0 - sf)
            acc_s[pl.ds(1, 1), :] += jnp.sum(df * (1.0 - sf), axis=0, keepdims=True)
            dzq_ref[rows, cols] = dq * _dsilu(zq, sq)
            dvi_ref[rows, cols] = dv_s[...]
            return dst, cg

        def bwd_chunk(jj, carries):
            ci = n_chunk - 1 - jj
            rows = pl.ds(pl.multiple_of(ci * CHUNK, CHUNK), CHUNK)
            return tuple(bwd_head(hd, ci, rows, carries[hd]) for hd in range(hb))

        zero = (jnp.zeros((LANES, LANES), F32), jnp.zeros((1, LANES), F32))
        lax.fori_loop(0, n_chunk, bwd_chunk, tuple(zero for _ in range(hb)))

        @pl.when(pl.program_id(0) == 0)
        def _():
            dog_ref[...] = jnp.zeros_like(dog_ref)

        for hd in range(hb):
            dlb_ref[:, hd * LANES : (hd + 1) * LANES] = acc_a[hd, pl.ds(1, 1), :]
            dog_ref[...] += acc_a[hd, pl.ds(0, 1), :]

    def col(k):
        return pl.BlockSpec((s, wide), lambda h: (0, k * (nh // hb) + h), pipeline_mode=pl.Buffered(1))

    head_in = pl.BlockSpec((s, wide), lambda h: (0, h), pipeline_mode=pl.Buffered(1))
    head = pl.BlockSpec((s, wide), lambda h: (0, h))
    vec = pl.BlockSpec((None, 1, wide), lambda h: (l, 0, h))
    ck = pltpu.VMEM((hb, CHUNK, LANES), F32)
    return _carry(
        body,
        comm,
        name="hgrn_bwd",
        grid=(nh // hb,),
        in_specs=[col(0), col(1), col(2), col(3), vec, pl.BlockSpec((None, 1, LANES), lambda h: (l, 0, 0)), head_in, head_in],
        out_specs=[head] * 4 + [pl.BlockSpec((1, wide), lambda h: (0, h)), _const((1, LANES))],
        out_shape=[jax.ShapeDtypeStruct((s, nh * LANES), F32)] * 4
        + [jax.ShapeDtypeStruct((1, nh * LANES), F32), jax.ShapeDtypeStruct((1, LANES), F32)],
        scratch_shapes=[pltpu.VMEM((hb, n_chunk, LANES, LANES), F32)] + [ck] * 8 + [pltpu.VMEM((hb, 8, LANES), F32)],
        args=(proj, proj, proj, proj, lb3, og3, o, dmixed),
    )


MEM_SCALE = HEAD64**-0.5


def _mem_heads(qraw, kvm, qg, kg, pr, m0):
    lo = pr * LANES
    uq, rq = _rms64(qraw[:, lo : lo + LANES], m0)
    uk, rk = _rms64(kvm[:, lo : lo + LANES], m0)
    v = bf(kvm[:, 2 * LANES + lo : 3 * LANES + lo])
    return uq, rq, uk, rk, v, uq * qg, bf(uk * kg)


def memattn_fwd(proj, qblk, kvm, qg3, kg3, l):
    s = proj.shape[0]
    nm = kvm.shape[0]
    tm = _tok(s)

    def body(q_ref, kv_ref, qg_ref, kg_ref, o_ref):
        m0 = _lane_mask0((1, LANES))
        qraw, kvv = q_ref[...], kv_ref[...]
        for pr in range(2):
            _, _, _, _, v, qn, kn = _mem_heads(qraw, kvv, qg_ref[...], kg_ref[...], pr, m0)
            out = jnp.zeros((tm, LANES), F32)
            for hh in range(2):
                mh = m0 if hh == 0 else jnp.logical_not(m0)
                sc = dot_nt(bf(jnp.where(mh, qn, 0.0)), kn) * MEM_SCALE
                p = jnp.exp(sc - jnp.max(sc, axis=-1, keepdims=True))
                p = p / jnp.sum(p, axis=-1, keepdims=True)
                out = jnp.where(mh, dot_nn(bf(p), v), out)
            o_ref[:, pr * LANES : (pr + 1) * LANES] = bf(out)

    gspec = pl.BlockSpec((None, 1, LANES), lambda i: (l, 0, 0))
    return pl.pallas_call(
        body,
        name="memattn_fwd",
        grid=(s // tm,),
        in_specs=[pl.BlockSpec((tm, 2 * LANES), lambda i: (i, qblk)), _const((nm, 4 * LANES)), gspec, gspec],
        out_specs=pl.BlockSpec((tm, 2 * LANES), lambda i: (i, 0)),
        out_shape=jax.ShapeDtypeStruct((s, 2 * LANES), BF16),
        compiler_params=_cparams(("arbitrary",)),
    )(proj, kvm, qg3, kg3)


def memattn_bwd(proj, qblk, kvm, qg3, kg3, l, dmixed):
    s = proj.shape[0]
    nm = kvm.shape[0]
    tm = _tok(s)

    def body(q_ref, kv_ref, qg_ref, kg_ref, dm_ref, dq_ref, dkv_ref, dqg_ref, dkg_ref):
        m0 = _lane_mask0((1, LANES))
        qraw, kvv = q_ref[...], kv_ref[...]
        qg, kg = qg_ref[...], kg_ref[...]

        @pl.when(pl.program_id(0) == 0)
        def _():
            dkv_ref[...] = jnp.zeros_like(dkv_ref)
            dqg_ref[...] = jnp.zeros_like(dqg_ref)
            dkg_ref[...] = jnp.zeros_like(dkg_ref)

        for pr in range(2):
            lo = pr * LANES
            uq, rq, uk, rk, v, qn, kn = _mem_heads(qraw, kvv, qg, kg, pr, m0)
            do = dm_ref[:, lo : lo + LANES]
            dqn = jnp.zeros((tm, LANES), F32)
            dkn = jnp.zeros((nm, LANES), F32)
            dv = jnp.zeros((nm, LANES), F32)
            for hh in range(2):
                mh = m0 if hh == 0 else jnp.logical_not(m0)
                qh = bf(jnp.where(mh, qn, 0.0))
                doh = bf(jnp.where(mh, do, 0.0))
                sc = dot_nt(qh, kn) * MEM_SCALE
                p = jnp.exp(sc - jnp.max(sc, axis=-1, keepdims=True))
                p = p / jnp.sum(p, axis=-1, keepdims=True)
                dp = dot_nt(doh, v)
                ds = bf(p * (dp - jnp.sum(p * dp, axis=-1, keepdims=True)))
                dqn = dqn + jnp.where(mh, dot_nn(ds, kn), 0.0) * MEM_SCALE
                dkn = dkn + dot_tn(ds, qh) * MEM_SCALE
                dv = dv + dot_tn(bf(p), doh)
            dqr, dqg = _rms64_bwd(dqn, uq, rq, qg, m0)
            dkr, dkg = _rms64_bwd(dkn, uk, rk, kg, m0)
            dq_ref[:, lo : lo + LANES] = dqr
            dkv_ref[:, lo : lo + LANES] += dkr
            dkv_ref[:, 2 * LANES + lo : 3 * LANES + lo] += dv
            dqg_ref[...] += dqg
            dkg_ref[...] += dkg

    gspec = pl.BlockSpec((None, 1, LANES), lambda i: (l, 0, 0))
    return pl.pallas_call(
        body,
        name="memattn_bwd",
        grid=(s // tm,),
        in_specs=[
            pl.BlockSpec((tm, 2 * LANES), lambda i: (i, qblk)),
            _const((nm, 4 * LANES)),
            gspec,
            gspec,
            pl.BlockSpec((tm, 2 * LANES), lambda i: (i, 3)),
        ],
        out_specs=[pl.BlockSpec((tm, 2 * LANES), lambda i: (i, 0)), _const((nm, 4 * LANES)), _const((1, LANES)), _const((1, LANES))],
        out_shape=[
            jax.ShapeDtypeStruct((s, 2 * LANES), F32),
            jax.ShapeDtypeStruct((nm, 4 * LANES), F32),
            jax.ShapeDtypeStruct((1, LANES), F32),
            jax.ShapeDtypeStruct((1, LANES), F32),
        ],
        compiler_params=_cparams(("arbitrary",)),
    )(proj, kvm, qg3, kg3, dmixed)


KV_MAIN = 768


def _log_sigmoid(z):
    return jnp.minimum(z, 0.0) - jnp.log(1.0 + jnp.exp(-jnp.abs(z)))


def kvprep_fwd(kvf, kg, fb):
    s = kvf.shape[0]
    tm = _tok(s)

    def body(kvf_ref, kg_ref, fb_ref, k_ref, v_ref, clf_ref, carry):
        m0 = _lane_mask0((1, LANES))

        @pl.when(pl.program_id(0) == 0)
        def _():
            carry[...] = jnp.zeros_like(carry)

        for j in range(KV_MAIN // LANES):
            u, _ = _rms64(kvf_ref[:, j * LANES : (j + 1) * LANES], m0)
            k_ref[:, j * LANES : (j + 1) * LANES] = bf(u * kg_ref[...])
        v_ref[...] = bf(kvf_ref[:, KV_MAIN : 2 * KV_MAIN])
        lf = _log_sigmoid(kvf_ref[:, 2 * KV_MAIN :] + fb_ref[...])
        clf_ref[...] = dot_nn(_tri(tm), lf, HI) + carry[...]
        carry[...] += jnp.sum(lf, axis=0, keepdims=True)

    n = kvf.shape[1]
    return pl.pallas_call(
        body,
        name="kvprep_fwd",
        grid=(s // tm,),
        in_specs=[pl.BlockSpec((tm, n), lambda i: (i, 0)), _const((1, LANES)), _const((1, LANES))],
        out_specs=[pl.BlockSpec((tm, KV_MAIN), lambda i: (i, 0))] * 2 + [pl.BlockSpec((tm, LANES), lambda i: (i, 0))],
        out_shape=[jax.ShapeDtypeStruct((s, KV_MAIN), BF16)] * 2 + [jax.ShapeDtypeStruct((s, LANES), F32)],
        scratch_shapes=[pltpu.VMEM((1, LANES), F32)],
        compiler_params=_cparams(("arbitrary",)),
    )(kvf, kg, fb)


def kvprep_bwd(kvf, kg, fb, dk, dv, dclf):
    s, n = kvf.shape
    tm = _tok(s)
    nb = s // tm

    def body(kvf_ref, kg_ref, fb_ref, dk_ref, dv_ref, dclf_ref, o_ref, dkg_ref, dfb_ref, carry):
        m0 = _lane_mask0((1, LANES))

        @pl.when(pl.program_id(0) == 0)
        def _():
            carry[...] = jnp.zeros_like(carry)
            dkg_ref[...] = jnp.zeros_like(dkg_ref)
            dfb_ref[...] = jnp.zeros_like(dfb_ref)

        kg_ = kg_ref[...]
        for j in range(KV_MAIN // LANES):
            cols = slice(j * LANES, (j + 1) * LANES)
            u, r = _rms64(kvf_ref[:, cols], m0)
            dkr, dkg = _rms64_bwd(dk_ref[:, cols], u, r, kg_, m0)
            o_ref[:, cols] = dkr
            dkg_ref[...] += dkg
        o_ref[:, KV_MAIN : 2 * KV_MAIN] = dv_ref[...]
        z = kvf_ref[:, 2 * KV_MAIN :] + fb_ref[...]
        dc = dclf_ref[...]
        dlf = dot_nn(_tri(tm, upper=True), dc, HI) + carry[...]
        carry[...] += jnp.sum(dc, axis=0, keepdims=True)
        dz = dlf * _sigmoid(-z)
        o_ref[:, 2 * KV_MAIN :] = dz
        dfb_ref[...] += jnp.sum(dz, axis=0, keepdims=True)

    rev = lambda i: (nb - 1 - i, 0)
    return pl.pallas_call(
        body,
        name="kvprep_bwd",
        grid=(nb,),
        in_specs=[pl.BlockSpec((tm, n), rev), _const((1, LANES)), _const((1, LANES)), pl.BlockSpec((tm, KV_MAIN), rev),
                  pl.BlockSpec((tm, KV_MAIN), rev), pl.BlockSpec((tm, LANES), rev)],
        out_specs=[pl.BlockSpec((tm, n), rev), _const((1, LANES)), _const((1, LANES))],
        out_shape=[jax.ShapeDtypeStruct((s, n), F32), jax.ShapeDtypeStruct((1, LANES), F32), jax.ShapeDtypeStruct((1, LANES), F32)],
        scratch_shapes=[pltpu.VMEM((1, LANES), F32)],
        compiler_params=_cparams(("arbitrary",)),
    )(kvf, kg, fb, dk, dv, dclf)


FOX_SCALE = HEAD64**-0.5


def _lane_col(block, lane_idx, h):
    return jnp.sum(jnp.where(lane_idx == h, block, 0.0), axis=-1, keepdims=True)


def _causal(tq, ext, i, transposed=False):
    if transposed:
        key = lax.broadcasted_iota(jnp.int32, (ext, tq), 0)
        qry = lax.broadcasted_iota(jnp.int32, (ext, tq), 1) + i * tq
    else:
        qry = lax.broadcasted_iota(jnp.int32, (tq, ext), 0) + i * tq
        key = lax.broadcasted_iota(jnp.int32, (tq, ext), 1)
    return key <= qry


def fox_fwd(proj, k_sh, v_sh, clf, clf_t, qg3, j_layer, comm=None):
    s = proj.shape[0]
    npair = 6
    tq = TQ if s % TQ == 0 else s
    nq = s // tq

    def body(q_ref, gate_ref, k_ref, v_ref, clf_ref, clft_ref, qg_ref, main_ref, o_ref, lse_ref):
        j = pl.program_id(0)
        lane = lax.broadcasted_iota(jnp.int32, (1, LANES), 1)
        m0 = lane < HEAD64
        u, _ = _rms64(q_ref[...], m0)
        qn = u * qg_ref[...] * FOX_SCALE
        clfv = clf_ref[...]
        for hh in range(2):
            h = 2 * j + hh
            mh = m0 if hh == 0 else jnp.logical_not(m0)
            qh = bf(jnp.where(mh, qn, 0.0))
            dcol = _lane_col(clfv, lane, h)
            drow = clft_ref[pl.ds(h, 1), :]
            for i in range(nq):
                rows = slice(i * tq, (i + 1) * tq)
                ext = (i + 1) * tq
                sc = dot_nt(qh[rows], k_ref[0:ext, :]) + dcol[rows] - drow[:, :ext]
                sc = jnp.where(_causal(tq, ext, i), sc, -jnp.inf)
                m = jnp.max(sc, axis=-1, keepdims=True)
                p = jnp.exp(sc - m)
                lsum = jnp.sum(p, axis=-1, keepdims=True)
                pv = dot_nn(bf(p), v_ref[0:ext, :]) / lsum
                lse = m + jnp.log(lsum)
                if hh == 0:
                    o_ref[rows, :] = pv
                    lse_ref[rows, :] = jnp.where(lane == 0, lse, 0.0)
                else:
                    o_ref[rows, :] = jnp.where(mh, pv, o_ref[rows, :])
                    lse_ref[rows, :] = jnp.where(lane == 1, lse, lse_ref[rows, :])
        main_ref[...] = bf(o_ref[...] * _sigmoid(gate_ref[...]))

    blk = lambda off: pl.BlockSpec((s, LANES), lambda j: (0, off + j))
    return _carry(
        body,
        comm,
        name="fox_fwd",
        grid=(npair,),
        in_specs=[blk(0), blk(npair), blk(0), blk(0), _const((s, LANES)), _const((16, s)),
                  pl.BlockSpec((None, 1, LANES), lambda j: (j_layer, 0, 0))],
        out_specs=[blk(0)] * 3,
        out_shape=[jax.ShapeDtypeStruct((s, npair * LANES), BF16)] + [jax.ShapeDtypeStruct((s, npair * LANES), F32)] * 2,
        scratch_shapes=[],
        args=(proj, proj, k_sh, v_sh, clf, clf_t, qg3),
    )


def fox_bwd(proj, k_sh, v_sh, clf, clf_t, qg3, j_layer, o, lse, lse_t, dmixed, dk_in, dv_in, dclf_in, comm=None):
    s = proj.shape[0]
    npair = 6
    tq = TQ if s % TQ == 0 else s
    nq = s // tq

    def body(q_ref, gate_ref, k_ref, v_ref, clf_ref, clft_ref, qg_ref, o_ref, lse_ref, lset_ref, dm_ref, dkin_ref, dvin_ref, dclfin_ref,
             dq_ref, dgate_ref, dk_ref, dv_ref, dclf_ref, dqg_ref, dqn_s, dcl_s):
        j = pl.program_id(0)
        lane = lax.broadcasted_iota(jnp.int32, (1, LANES), 1)
        m0 = lane < HEAD64
        qg = qg_ref[...]
        u, r = _rms64(q_ref[...], m0)
        qn = u * qg * FOX_SCALE
        ov = o_ref[...]
        gate = gate_ref[...]
        sg = _sigmoid(gate)
        dmain = dm_ref[...]
        do = dmain * sg
        dgate_ref[...] = dmain * ov * sg * (1.0 - sg)
        dk_ref[...] = dkin_ref[...]
        dv_ref[...] = dvin_ref[...]
        clfv = clf_ref[...]
        lsev = lse_ref[...]
        ones8 = jnp.ones((8, LANES), F32)

        @pl.when(j == 0)
        def _():
            dclf_ref[...] = dclfin_ref[...]
            dqg_ref[...] = jnp.zeros_like(dqg_ref)

        for hh in range(2):
            h = 2 * j + hh
            mh = m0 if hh == 0 else jnp.logical_not(m0)
            qh = bf(jnp.where(mh, qn, 0.0))
            doh = jnp.where(mh, do, 0.0)
            dohb = bf(doh)
            doo = doh * ov
            dcol = _lane_col(clfv, lane, h)
            drow = clft_ref[pl.ds(h, 1), :]
            lcol = _lane_col(lsev, lane, hh)
            lrow = lset_ref[pl.ds(h, 1), :]
            delta = jnp.sum(doo, axis=-1, keepdims=True)
            dcl_s[...] = jnp.zeros_like(dcl_s)
            for i in range(nq):
                rows = slice(i * tq, (i + 1) * tq)
                ext = (i + 1) * tq
                kk, vv = k_ref[0:ext, :], v_ref[0:ext, :]
                sc = dot_nt(qh[rows], kk) + dcol[rows] - drow[:, :ext]
                p = jnp.where(_causal(tq, ext, i), jnp.exp(sc - lcol[rows]), 0.0)
                ds = p * (dot_nt(dohb[rows], vv) - delta[rows])
                dqh = dot_nn(bf(ds), kk) * FOX_SCALE
                if hh == 0:
                    dqn_s[rows, :] = dqh
                else:
                    dqn_s[rows, :] = jnp.where(mh, dqh, dqn_s[rows, :])
                dcl_s[rows, :] += jnp.sum(ds, axis=-1, keepdims=True)
                sct = dot_nt(kk, qh[rows]) + drow[:, rows] - dcol[:ext]
                pt = jnp.where(_causal(tq, ext, i, transposed=True), jnp.exp(sct - lrow[:, rows]), 0.0)
                delta_row = dot_nt(ones8, doo[rows], HI)[0:1]
                dst = pt * (dot_nt(vv, dohb[rows]) - delta_row)
                dv_ref[0:ext, :] += dot_nn(bf(pt), dohb[rows])
                dk_ref[0:ext, :] += dot_nn(bf(dst), qh[rows])
                dcl_s[0:ext, :] -= jnp.sum(dst, axis=-1, keepdims=True)
            dclf_ref[...] += jnp.where(lane == h, dcl_s[...], 0.0)
        dqr, dqg = _rms64_bwd(dqn_s[...], u, r, qg, m0)
        dq_ref[...] = dqr
        dqg_ref[...] += dqg

    blk = lambda off: pl.BlockSpec((s, LANES), lambda j: (0, off + j))
    full = _const((s, LANES))
    return _carry(
        body,
        comm,
        name="fox_bwd",
        grid=(npair,),
        in_specs=[blk(0), blk(npair), blk(0), blk(0), full, _const((16, s)), pl.BlockSpec((None, 1, LANES), lambda j: (j_layer, 0, 0)),
                  blk(0), blk(0), _const((16, s)), blk(0), blk(0), blk(0), full],
        out_specs=[blk(0)] * 4 + [full, _const((1, LANES))],
        out_shape=[jax.ShapeDtypeStruct((s, npair * LANES), F32)] * 4
        + [jax.ShapeDtypeStruct((s, LANES), F32), jax.ShapeDtypeStruct((1, LANES), F32)],
        scratch_shapes=[pltpu.VMEM((s, LANES), F32), pltpu.VMEM((s, LANES), F32)],
        args=(proj, proj, k_sh, v_sh, clf, clf_t, qg3, o, lse, lse_t, dmixed, dk_in, dv_in, dclf_in),
    )


def loss_head(y, target):
    s, d = y.shape
    tm = _tok(s)

    def body(y_ref, t_ref, loss_ref, dy_ref):
        err = y_ref[...] - t_ref[...]
        dy_ref[...] = err * (1.0 / d)

        @pl.when(pl.program_id(0) == 0)
        def _():
            loss_ref[...] = jnp.zeros_like(loss_ref)

        part = jnp.sum(jnp.mean(err * err, axis=-1, keepdims=True), axis=0, keepdims=True)
        loss_ref[...] += 0.5 * part

    row = pl.BlockSpec((tm, d), lambda i: (i, 0))
    return pl.pallas_call(
        body,
        name="loss_head",
        grid=(s // tm,),
        in_specs=[row, row],
        out_specs=[_const((1, 1)), row],
        out_shape=[jax.ShapeDtypeStruct((1, 1), F32), jax.ShapeDtypeStruct((s, d), F32)],
        compiler_params=_cparams(("arbitrary",)),
    )(y, target)


def _row_tile(r, c, n_arrays):
    budget = VMEM_LIMIT_BYTES // 2
    padded_c = -(-c // LANES) * LANES
    for step in (16, 8):
        fits = [t for t in range(step, r + 1, step) if r % t == 0 and 2 * n_arrays * t * padded_c * 4 <= budget]
        if fits:
            return fits[-1]
    return r


def _as2d(a):
    return a.reshape(-1, a.shape[-1]) if a.ndim >= 2 else a.reshape(1, -1)


def adamw(ws, gss, ms, vs):
    nw = len(ws)
    shape = ws[0].shape
    w2, m2, v2 = ([_as2d(t) for t in lst] for lst in (ws, ms, vs))
    rows, c = w2[0].shape
    gss = [[g.reshape(-1, c) for g in gs] for gs in gss]
    sizes = [g.shape[0] for g in gss[0]]
    assert sum(sizes) == rows
    tr = _row_tile(min(sizes), c, 8 * nw)
    assert all(r % tr == 0 for r in sizes)
    c1 = 1.0 - ADAM_B1**ADAM_STEP
    c2 = 1.0 - ADAM_B2**ADAM_STEP
    outs = []
    first = 0
    for k, r in enumerate(sizes):
        n_prev = len(outs)

        def body(*refs, n_prev=n_prev):
            out_refs = refs[4 * nw + n_prev :]
            for i in range(nw):
                w_ref, g_ref, m_ref, v_ref = refs[4 * i : 4 * i + 4]
                go_ref, d_ref, nm_ref, nv_ref = out_refs[4 * i : 4 * i + 4]
                gv = g_ref[...]
                nm = ADAM_B1 * m_ref[...] + (1.0 - ADAM_B1) * gv
                nv = ADAM_B2 * v_ref[...] + (1.0 - ADAM_B2) * (gv * gv)
                go_ref[...] = gv
                nm_ref[...] = nm
                nv_ref[...] = nv
                d_ref[...] = -ADAM_LR * ((nm / c1) / (jnp.sqrt(nv / c2) + ADAM_EPS) + ADAM_WD * w_ref[...])

        spec = pl.BlockSpec((tr, c), lambda i, b0=first // tr: (b0 + i, 0))
        args = [t for i in range(nw) for t in (w2[i], gss[i][k], m2[i], v2[i])]
        outs = pl.pallas_call(
            body,
            name="adamw",
            grid=(r // tr,),
            in_specs=[spec, pl.BlockSpec((tr, c), lambda i: (i, 0)), spec, spec] * nw + [ANY] * n_prev,
            out_specs=[spec] * (4 * nw),
            out_shape=[jax.ShapeDtypeStruct((rows, c), F32)] * (4 * nw),
            input_output_aliases={4 * nw + j: j for j in range(n_prev)},
            compiler_params=_cparams(("arbitrary",)),
        )(*args, *outs)
        first += r
    return [tuple(t.reshape(shape) for t in outs[4 * i : 4 * i + 4]) for i in range(nw)]


def _by_shape(arrays, fn):
    groups = {}
    for i, t in enumerate(arrays):
        groups.setdefault((t.shape, str(t.dtype)), []).append(i)
    out = [None] * len(arrays)
    for idx in groups.values():
        for i, res in zip(idx, fn(idx)):
            out[i] = res
    return out


def pair_sum(gs, recvs, c_arr):
    n = len(gs)
    _, k, r, c = gs[0].shape
    tr = _row_tile(r, c, 3 * n)

    def body(c_ref, *refs):
        for i in range(n):
            refs[2 * n + i][...] = bf(refs[i][...] + refs[n + i][...].astype(F32))

    slab = pl.BlockSpec((None, tr, c), lambda kk, i, cr: (kk, i, 0))
    return pl.pallas_call(
        body,
        name="pair_sum",
        grid_spec=pltpu.PrefetchScalarGridSpec(
            num_scalar_prefetch=1,
            grid=(k, r // tr),
            in_specs=[pl.BlockSpec((None, None, tr, c), lambda kk, i, cr: (cr[0], kk, i, 0))] * n + [slab] * n,
            out_specs=[slab] * n,
        ),
        out_shape=[jax.ShapeDtypeStruct((k, r, c), BF16)] * n,
        compiler_params=_cparams(("arbitrary", "arbitrary")),
    )(c_arr, *gs, *recvs)


def chip_sum(ps, qs, sel):
    n = len(ps)
    _, r, c = ps[0].shape
    nq = qs[0].shape[0]
    tr = _row_tile(r, c, 4 * n)

    def body(sel_ref, *refs):
        for a in range(n):
            acc = refs[a][...].astype(F32)
            for i in range(nq):
                acc = acc + refs[n + a][i].astype(F32)
            refs[2 * n + a][...] = acc

    return pl.pallas_call(
        body,
        name="chip_sum",
        grid_spec=pltpu.PrefetchScalarGridSpec(
            num_scalar_prefetch=1,
            grid=(r // tr,),
            in_specs=[pl.BlockSpec((None, tr, c), lambda i, sr: (sr[0], i, 0))] * n + [pl.BlockSpec((nq, tr, c), lambda i, sr: (0, i, 0))] * n,
            out_specs=[pl.BlockSpec((None, tr, c), lambda i, sr: (sr[1], i, 0))] * n,
        ),
        out_shape=[jax.ShapeDtypeStruct((2, r, c), F32)] * n,
        compiler_params=_cparams(("arbitrary",)),
    )(sel, *ps, *qs)


def cast_into_slot(w4s, owns, sel, dtype):
    n = len(w4s)
    _, _, r, c = w4s[0].shape
    tr = _row_tile(r, c, 2 * n)

    def body(sel_ref, *refs):
        for i in range(n):
            refs[n + i][...] = refs[i][...].astype(dtype)

    return pl.pallas_call(
        body,
        name="cast_into_slot",
        grid_spec=pltpu.PrefetchScalarGridSpec(
            num_scalar_prefetch=1,
            grid=(2, r // tr),
            in_specs=[pl.BlockSpec((None, None, tr, c), lambda hf, i, sr, g=g: (g, hf, i, 0)) for g in owns],
            out_specs=[pl.BlockSpec((None, None, tr, c), lambda hf, i, sr: (sr[0], hf, i, 0))] * n,
        ),
        out_shape=[jax.ShapeDtypeStruct((N_CHIPS, 2, r, c), dtype)] * n,
        compiler_params=_cparams(("arbitrary", "arbitrary")),
    )(sel, *w4s)


def _place():
    x, y, c = lax.axis_index("x"), lax.axis_index("y"), lax.axis_index("c")
    chips = [(1 - x, y), (x, 1 - y), (1 - x, 1 - y)]
    return x, y, c, 2 * x + y, chips, [2 * cx + cy for cx, cy in chips]


def _rcopy(src, dst, send, recv, dev):
    return pltpu.make_async_remote_copy(src_ref=src, dst_ref=dst, send_sem=send, recv_sem=recv, device_id=dev, device_id_type=MESH)


class Gather:
    def __init__(self, bufs):
        n = len(bufs)
        self.n = n
        self.args = list(bufs)
        self.out_shape = [jax.ShapeDtypeStruct(t.shape, t.dtype) for t in bufs]
        self.aliases = {a: a for a in range(n)}
        self.scratch = [pltpu.SemaphoreType.DMA((n, 6)), pltpu.SemaphoreType.DMA((n, 6))]

    def _sends(self, outs, send, recv):
        x, y, c, me, chips, _ = _place()
        cps = []
        for a in range(self.n):
            mine = outs[a].at[me, c]
            cps += [_rcopy(mine, mine, send.at[a, j], recv.at[a, j], (*chips[j], c)) for j in range(3)]
        return cps

    def start(self, ins, outs, scr):
        for cp in self._sends(outs, *scr):
            cp.start()

    def finish(self, ins, outs, scr):
        send, recv = scr
        x, y, c, me, chips, cidx = _place()
        sib = (x, y, 1 - c)
        passed = []
        for a in range(self.n):
            for j in range(3):
                landed = outs[a].at[cidx[j], c]
                _rcopy(landed, landed, send.at[a, j], recv.at[a, j], (*chips[j], c)).wait_recv()
                fwd = _rcopy(landed, landed, send.at[a, 3 + j], recv.at[a, 3 + j], sib)
                fwd.start()
                passed.append(fwd)
        for a in range(self.n):
            for j in range(3):
                theirs = outs[a].at[cidx[j], 1 - c]
                _rcopy(theirs, theirs, send.at[a, 3 + j], recv.at[a, 3 + j], sib).wait_recv()
        for cp in self._sends(outs, send, recv) + passed:
            cp.wait_send()


class PairExchange:
    def __init__(self, gs):
        n = len(gs)
        self.n = n
        self.args = list(gs)
        self.out_shape = [jax.ShapeDtypeStruct(t.shape[1:], t.dtype) for t in gs]
        self.aliases = {}
        self.scratch = [pltpu.SemaphoreType.DMA((n,)), pltpu.SemaphoreType.DMA((n,))]

    def _copies(self, ins, outs, send, recv):
        x, y, c = lax.axis_index("x"), lax.axis_index("y"), lax.axis_index("c")
        return [_rcopy(ins[a].at[1 - c], outs[a], send.at[a], recv.at[a], (x, y, 1 - c)) for a in range(self.n)]

    def start(self, ins, outs, scr):
        for cp in self._copies(ins, outs, *scr):
            cp.start()

    def finish(self, ins, outs, scr):
        for cp in self._copies(ins, outs, *scr):
            cp.wait()


class ChipExchange:
    def __init__(self, ps):
        n = len(ps)
        self.n = n
        self.args = list(ps)
        self.out_shape = [jax.ShapeDtypeStruct((3,) + t.shape[1:], t.dtype) for t in ps]
        self.aliases = {}
        self.scratch = [pltpu.SemaphoreType.DMA((n, 3)), pltpu.SemaphoreType.DMA((n, 3))]

    def _sends(self, ins, outs, send, recv):
        x, y, c, me, chips, cidx = _place()
        return [
            _rcopy(ins[a].at[cidx[j]], outs[a].at[j], send.at[a, j], recv.at[a, j], (*chips[j], c))
            for a in range(self.n)
            for j in range(3)
        ]

    def start(self, ins, outs, scr):
        for cp in self._sends(ins, outs, *scr):
            cp.start()

    def finish(self, ins, outs, scr):
        send, recv = scr
        x, y, c, me, chips, _ = _place()
        for a in range(self.n):
            for j in range(3):
                landed = outs[a].at[j]
                _rcopy(landed, landed, send.at[a, j], recv.at[a, j], (*chips[j], c)).wait_recv()
        for cp in self._sends(ins, outs, send, recv):
            cp.wait_send()


class PairShare:
    def __init__(self, bufs):
        n = len(bufs)
        self.n = n
        self.args = list(bufs)
        self.out_shape = [jax.ShapeDtypeStruct(t.shape, t.dtype) for t in bufs]
        self.aliases = {a: a for a in range(n)}
        self.scratch = [pltpu.SemaphoreType.DMA((n,)), pltpu.SemaphoreType.DMA((n,))]

    def _sends(self, outs, send, recv):
        x, y, c = lax.axis_index("x"), lax.axis_index("y"), lax.axis_index("c")
        return [_rcopy(outs[a].at[c], outs[a].at[c], send.at[a], recv.at[a], (x, y, 1 - c)) for a in range(self.n)]

    def start(self, ins, outs, scr):
        for cp in self._sends(outs, *scr):
            cp.start()

    def finish(self, ins, outs, scr):
        send, recv = scr
        x, y, c = lax.axis_index("x"), lax.axis_index("y"), lax.axis_index("c")
        for a in range(self.n):
            theirs = outs[a].at[1 - c]
            _rcopy(theirs, theirs, send.at[a], recv.at[a], (x, y, 1 - c)).wait_recv()
        for cp in self._sends(outs, send, recv):
            cp.wait_send()


class Multi:
    def __init__(self, comms):
        self.comms = comms
        self.args, self.out_shape, self.scratch, self.aliases = [], [], [], {}
        self.spans = []
        for cm in comms:
            a0, o0, s0 = len(self.args), len(self.out_shape), len(self.scratch)
            self.aliases.update({a0 + i: o0 + o for i, o in cm.aliases.items()})
            self.args += cm.args
            self.out_shape += cm.out_shape
            self.scratch += cm.scratch
            self.spans.append((slice(a0, len(self.args)), slice(o0, len(self.out_shape)), slice(s0, len(self.scratch))))

    def start(self, ins, outs, scr):
        for cm, (sa, so, ss) in zip(self.comms, self.spans):
            cm.start(ins[sa], outs[so], scr[ss])

    def finish(self, ins, outs, scr):
        for cm, (sa, so, ss) in zip(self.comms, self.spans):
            cm.finish(ins[sa], outs[so], scr[ss])

    def split(self, res):
        return [list(res[so]) for _, so, _ in self.spans]


def run_comm(comm, name):
    na, no = len(comm.args), len(comm.out_shape)

    def body(*refs):
        ins, outs, scr = refs[:na], refs[na : na + no], refs[na + no :]
        comm.start(ins, outs, scr)
        comm.finish(ins, outs, scr)

    return pl.pallas_call(
        body,
        name=name,
        in_specs=[ANY] * na,
        out_specs=[ANY] * no,
        out_shape=comm.out_shape,
        input_output_aliases=comm.aliases,
        scratch_shapes=comm.scratch,
    )(*comm.args)


def _carry(body, comm, *, name, grid, in_specs, out_specs, out_shape, scratch_shapes, args):
    params = _cparams(("arbitrary",))
    if comm is None:
        res = pl.pallas_call(body, name=name, grid=grid, in_specs=in_specs, out_specs=out_specs, out_shape=out_shape,
                             scratch_shapes=scratch_shapes, compiler_params=params)(*args)
        return res, None
    ni, no, ns = len(in_specs), len(out_specs), len(scratch_shapes)
    ci, co = len(comm.args), len(comm.out_shape)

    def wrapped(*refs):
        ins, c_ins = refs[:ni], refs[ni : ni + ci]
        p = ni + ci
        outs, c_outs = refs[p : p + no], refs[p + no : p + no + co]
        p += no + co
        scr, c_scr = refs[p : p + ns], refs[p + ns :]

        @pl.when(pl.program_id(0) == 0)
        def _():
            comm.start(c_ins, c_outs, c_scr)

        body(*ins, *outs, *scr)

        @pl.when(pl.program_id(0) == grid[0] - 1)
        def _():
            comm.finish(c_ins, c_outs, c_scr)

    res = pl.pallas_call(
        wrapped,
        name=name + "_carry",
        grid=grid,
        in_specs=list(in_specs) + [ANY] * ci,
        out_specs=list(out_specs) + [ANY] * co,
        out_shape=list(out_shape) + list(comm.out_shape),
        input_output_aliases={ni + i: no + o for i, o in comm.aliases.items()},
        scratch_shapes=list(scratch_shapes) + list(comm.scratch),
        compiler_params=params,
    )(*args, *comm.args)
    return res[:no], res[no:]


def small_allreduce(buf):
    r = buf.shape[0]

    def body(b_ref, o_ref, slots, send, recv):
        x, y, c = lax.axis_index("x"), lax.axis_index("y"), lax.axis_index("c")
        me = 4 * x + 2 * y + c
        slots[me] = b_ref[...]
        cps = []
        peers = []
        for mask in range(1, N_DEV):
            fx, fy, fc = (mask >> 2) & 1, (mask >> 1) & 1, mask & 1
            px, py, pc = (1 - x if fx else x), (1 - y if fy else y), (1 - c if fc else c)
            peers.append(4 * px + 2 * py + pc)
            cps.append(_rcopy(b_ref, slots.at[me], send.at[mask - 1], recv.at[mask - 1], (px, py, pc)))
        for cp in cps:
            cp.start()
        for k, pid in enumerate(peers):
            landed = slots.at[pid]
            _rcopy(landed, landed, send.at[k], recv.at[k], (x, y, c)).wait_recv()
        for cp in cps:
            cp.wait_send()
        acc = slots[0]
        for i in range(1, N_DEV):
            acc = acc + slots[i]
        o_ref[...] = acc

    vm = pl.BlockSpec(memory_space=pltpu.VMEM)
    return pl.pallas_call(
        body,
        name="small_allreduce",
        in_specs=[vm],
        out_specs=vm,
        out_shape=jax.ShapeDtypeStruct(buf.shape, F32),
        scratch_shapes=[pltpu.VMEM((N_DEV, r, LANES), F32), pltpu.SemaphoreType.DMA((N_DEV - 1,)), pltpu.SemaphoreType.DMA((N_DEV - 1,))],
    )(buf)


WEIGHT_NAMES = ["ffn1_norm", "ffn1_w_gate", "ffn1_w_up", "ffn1_w_down", "mix_norm", "mem_norm", "w_mem_kv", "mem_q_gain",
                "mem_k_gain", "w_in_a", "hgrn_lb_logits", "hgrn_o_gain", "w_in_b", "fox_q_gain", "kv_norm", "w_kv", "fox_f_bias",
                "fox_k_gain", "w_out", "ffn2_norm", "ffn2_w_gate", "ffn2_w_up", "ffn2_w_down"]
SHARDED = ["ffn1_w_gate", "ffn1_w_up", "ffn1_w_down", "w_mem_kv", "w_in_a", "w_in_b", "w_kv", "w_out", "ffn2_w_gate", "ffn2_w_up", "ffn2_w_down"]
SMALL = [n for n in WEIGHT_NAMES if n not in SHARDED]
FFN1 = ["ffn1_w_gate", "ffn1_w_up", "ffn1_w_down"]
FFN2 = ["ffn2_w_gate", "ffn2_w_up", "ffn2_w_down"]
PER_LAYER = FFN1 + FFN2 + ["w_mem_kv", "w_out"]
TRANSPOSED = ["ffn1_w_gate", "ffn1_w_up", "ffn2_w_gate", "ffn2_w_up", "w_in_a", "w_in_b"]
N_LAYERS, N_A = 4, 2
KV_PAD = 13 * LANES


def _cols_from_chips(g):
    return jnp.moveaxis(g, 0, 2).reshape(g.shape[1], g.shape[2], N_CHIPS * g.shape[3])


def _rows_from_chips(g):
    return jnp.moveaxis(g, 0, 1).reshape(g.shape[1], N_CHIPS * g.shape[2], g.shape[3])


def _pair_tile(g):
    return jnp.tile(g, (1, 2)).reshape(g.shape[0], 1, LANES)


def _pair_fold(g):
    return g[:, :HEAD64] + g[:, HEAD64:]


def kernel(x, mem, ffn1_norm, ffn1_w_gate, ffn1_w_up, ffn1_w_down, mix_norm, mem_norm, w_mem_kv, mem_q_gain, mem_k_gain, w_in_a, hgrn_lb_logits, hgrn_o_gain, w_in_b, fox_q_gain, kv_norm, w_kv, fox_f_bias, fox_k_gain, w_out, ffn2_norm, ffn2_w_gate, ffn2_w_up, ffn2_w_down, loss_target, m_ffn1_norm, m_ffn1_w_gate, m_ffn1_w_up, m_ffn1_w_down, m_mix_norm, m_mem_norm, m_w_mem_kv, m_mem_q_gain, m_mem_k_gain, m_w_in_a, m_hgrn_lb_logits, m_hgrn_o_gain, m_w_in_b, m_fox_q_gain, m_kv_norm, m_w_kv, m_fox_f_bias, m_fox_k_gain, m_w_out, m_ffn2_norm, m_ffn2_w_gate, m_ffn2_w_up, m_ffn2_w_down, v_ffn1_norm, v_ffn1_w_gate, v_ffn1_w_up, v_ffn1_w_down, v_mix_norm, v_mem_norm, v_w_mem_kv, v_mem_q_gain, v_mem_k_gain, v_w_in_a, v_hgrn_lb_logits, v_hgrn_o_gain, v_w_in_b, v_fox_q_gain, v_kv_norm, v_w_kv, v_fox_f_bias, v_fox_k_gain, v_w_out, v_ffn2_norm, v_ffn2_w_gate, v_ffn2_w_up, v_ffn2_w_down):
    given = dict(locals())
    def oriented(n, t):
        return jnp.swapaxes(t, 1, 2) if n in TRANSPOSED else t

    w = {n: oriented(n, given[n]) for n in WEIGHT_NAMES}
    xs, mems, tgt = x[0], mem[0], loss_target[0]
    s, d = xs.shape
    my_chip = 2 * lax.axis_index("x") + lax.axis_index("y")
    sel = jnp.stack([my_chip, lax.axis_index("c")]).astype(jnp.int32)
    c_arr = sel[1:]

    def w_in_name(l):
        return "w_in_a" if l < N_A else "w_in_b"

    def halves_of(n):
        rows, cols = w[n].shape[-2:]
        return w[n].reshape(-1, 2, rows // 2, cols)

    def own_of(n, l):
        return 0 if w[n].ndim == 2 else (l - N_A if n == "w_in_b" else l)

    def view(buf, n):
        rows, cols = w[n].shape[-2:]
        return buf.reshape(N_CHIPS, rows, cols) if w[n].ndim == 2 else buf.reshape(N_CHIPS, 1, rows, cols)

    def mixer(l):
        return [(w_in_name(l), l), ("w_mem_kv", l), ("w_out", l)]

    first = [(n, 0) for n in PER_LAYER] + [("w_in_a", 0), ("w_kv", 0)]
    carried = {
        (0, "ffn1"): mixer(1), (0, "mix"): [(n, 1) for n in FFN1 + FFN2[:2]], (0, "ffn2"): [(FFN2[2], 1)],
        (1, "ffn1"): mixer(2), (1, "mix"): [(n, 2) for n in FFN1 + FFN2[:2]], (1, "ffn2"): [(FFN2[2], 2)],
        (2, "ffn1"): [(FFN1[0], 3)] + mixer(3)[1:], (2, "mix"): [(FFN1[1], 3), (FFN1[2], 3)], (2, "ffn2"): [(FFN2[0], 3)] + mixer(3)[:1],
        (3, "ffn1"): [(FFN2[1], 3)], (3, "mix"): [(FFN2[2], 3)],
    }
    bufs = {}
    every = first + [it for items in carried.values() for it in items]
    for layer in range(N_LAYERS):
        its = [it for it in every if it[1] == layer]
        srcs = [halves_of(n) for n, _ in its]
        cast = lambda idx: cast_into_slot([srcs[i] for i in idx], [own_of(*its[i]) for i in idx], sel, BF16)
        bufs.update(zip(its, _by_shape(srcs, cast)))
    lb_buf = cast_into_slot([hgrn_lb_logits.reshape(1, 2, 1, -1)], [0], sel, F32)[0]
    got0 = run_comm(Gather([bufs[it] for it in first] + [lb_buf]), "gather_layer0")
    got = {it: view(b, it[0]) for it, b in zip(first, got0[:-1])}
    w_kv_full = _cols_from_chips(got[("w_kv", 0)][:, None])
    w_kv_full = jnp.pad(w_kv_full, ((0, 0), (0, 0), (0, KV_PAD - w_kv_full.shape[-1])))
    logits3 = jnp.moveaxis(got0[-1].reshape(N_CHIPS, 2, -1), 0, 1).reshape(2, 1, -1)
    lb3 = lb_fwd(logits3)
    w_in, w_mkv, w_o = {}, {}, {}

    def gather_behind(key):
        items = carried.get(key)
        return None if items is None else Gather([bufs[it] for it in items])

    def landed(key, res):
        if res is not None:
            got.update({it: view(b, it[0]) for it, b in zip(carried[key], res)})

    norm3 = {n: w[n].reshape(N_LAYERS, 1, d) for n in ("ffn1_norm", "mix_norm", "mem_norm", "ffn2_norm")}
    kvn3 = kv_norm.reshape(1, 1, d)
    mqg3, mkg3 = _pair_tile(mem_q_gain), _pair_tile(mem_k_gain)
    og3 = hgrn_o_gain.reshape(N_A, 1, LANES)
    fqg3 = _pair_tile(fox_q_gain)
    fkg = jnp.tile(fox_k_gain, 2).reshape(1, LANES)
    fb = jnp.pad(fox_f_bias, (0, LANES - fox_f_bias.shape[0])).reshape(1, LANES)

    sv = [dict() for _ in range(N_LAYERS)]
    h = xs
    kv = None
    for l in range(N_LAYERS):
        t = sv[l]
        t["x0"] = h
        (h, t["a1"], t["b1"]), res = ffn_fwd(h, norm3["ffn1_norm"], l, *[got[(n, l)] for n in FFN1], 0, comm=gather_behind((l, "ffn1")))
        landed((l, "ffn1"), res)
        t["x1"] = h
        w_in[l] = _rows_from_chips(got[(w_in_name(l), l)])
        t["proj"] = proj_fwd(h, norm3["mix_norm"], l, w_in[l], 0, wt=True)
        if l < N_A:
            (main, t["o"]), res = hgrn_fwd(t["proj"], lb3, og3, l, comm=gather_behind((l, "mix")))
            t["qblk"] = 12
        else:
            (main, t["o"], t["lse"]), res = fox_fwd(t["proj"], kv["k"], kv["v"], kv["clf"], kv["clf_t"], fqg3, l - N_A, comm=gather_behind((l, "mix")))
            t["qblk"] = 6
        landed((l, "mix"), res)
        w_mkv[l], w_o[l] = _rows_from_chips(got[("w_mem_kv", l)]), _rows_from_chips(got[("w_out", l)])
        t["kvm"] = proj_fwd(mems, norm3["mem_norm"], l, w_mkv[l], 0)
        memo = memattn_fwd(t["proj"], t["qblk"], t["kvm"], mqg3, mkg3, l)
        t["mixed"] = jnp.concatenate([main, memo], axis=-1)
        h = mm_res(h, t["mixed"], w_o[l], 0)
        t["x2"] = h
        (h, t["a2"], t["b2"]), res = ffn_fwd(h, norm3["ffn2_norm"], l, *[got[(n, l)] for n in FFN2], 0, comm=gather_behind((l, "ffn2")))
        landed((l, "ffn2"), res)
        if l == N_A - 1:
            kv = {"x": h, "kvf": proj_fwd(h, kvn3, 0, w_kv_full, 0)}
            kv["k"], kv["v"], kv["clf"] = kvprep_fwd(kv["kvf"], fkg, fb)
            kv["clf_t"] = kv["clf"][:, :16].T

    loss_local, dx = loss_head(h, tgt)

    nc = N_CHIPS
    fc = ffn1_w_down.shape[1]
    gsplit = [dict() for _ in range(N_LAYERS)]

    def group_layout(l):
        lay = {n: b[0].reshape(2, nc, fc // 2, d) for n, b in gsplit[l].items()}
        lay16 = {n: b[1].reshape(2, nc, fc // 2, d) for n, b in gsplit[l].items()}
        for n, (g32, g16) in (("w_mem_kv", dw_mkv[l]), ("w_out", dw_o[l]), (w_in_name(l), dw_in[l])):
            lay[n], lay16[n] = g32, g16
        names = PER_LAYER + [w_in_name(l)]
        if l == N_A - 1:
            kv_cols = w_kv.shape[-1] * nc
            lay["w_kv"] = jnp.transpose(dw_kv[:, :kv_cols].reshape(2, d // 2, nc, kv_cols // nc), (0, 2, 1, 3))
            names = names + ["w_kv"]
        return names, [lay[n] for n in names], [lay16[n] if n in lay16 else bf(lay[n]) for n in names]

    def pair_sums(gl, recv):
        return _by_shape(gl, lambda idx: pair_sum([gl[i] for i in idx], [recv[i] for i in idx], c_arr))

    def chip_sums(ps, qs):
        return _by_shape(ps, lambda idx: chip_sum([ps[i] for i in idx], [qs[i] for i in idx], sel))

    n_ffn = len(FFN1) + len(FFN2)
    riding = {l: l + 1 for l in range(N_LAYERS - 1)}
    reduced = {}
    unshared = None
    dw_in, dw_o, dw_mkv = [None] * N_LAYERS, [None] * N_LAYERS, [None] * N_LAYERS
    sg = {n: [None] * N_LAYERS for n in ("ffn1_norm", "mix_norm", "mem_norm", "ffn2_norm", "mem_q_gain", "mem_k_gain")}
    sg["hgrn_o_gain"], sg["fox_q_gain"], dlb = [None] * N_A, [None] * (N_LAYERS - N_A), [None] * N_A
    dk_sh = jnp.zeros((s, KV_MAIN), F32)
    dv_sh = jnp.zeros((s, KV_MAIN), F32)
    dclf = jnp.zeros((s, LANES), F32)
    zero_mem = jnp.zeros(mems.shape, F32)
    dw_kv = None
    for l in reversed(range(N_LAYERS)):
        t = sv[l]
        if l == N_A - 1:
            dkvf, dfkg, dfb = kvprep_bwd(kv["kvf"], fkg, fb, dk_sh, dv_sh, dclf)
            dx, sg["kv_norm"], xn_kv, dpb = proj_bwd(kv["x"], kvn3, 0, [dkvf], w_kv_full, 0, dx)
            dw_kv = wgrad(xn_kv, dpb)
        ride = riding.get(l)
        comms = []
        if unshared is not None:
            comms.append(PairShare(unshared[2]))
        if ride is not None:
            names_r, gl_r, gl16_r = group_layout(ride)
            comms.append(PairExchange(gl16_r))
        comm = Multi(comms) if comms else None
        (dx, da, db, hm, xn, dyb, sg["ffn2_norm"][l]), res = ffn_bwd(t["x2"], norm3["ffn2_norm"], l, dx, t["a2"], t["b2"], *[got[(n, l)] for n in FFN2], 0, comm=comm)
        if comm is not None:
            res = comm.split(res)
            if unshared is not None:
                reduced[unshared[0]] = dict(zip(unshared[1], res.pop(0)))
                unshared = None
            if ride is not None:
                partial_r = pair_sums(gl_r, res.pop(0))

        def ffn_wgrads(which, da, db, hm, xn, dyb):
            for n, (a_, b_) in zip(which, ((da, xn), (db, xn), (hm, dyb))):
                gsplit[l][n] = wgrad(a_, b_, split=True)

        ffn_wgrads(FFN2, da, db, hm, xn, dyb)
        dmixed, dxb = mm_nt(dx, w_o[l], 0)
        dw_o[l] = wgrad(t["mixed"], dxb, chip_rows=w_out.shape[1])
        dqm, dkvm, dmq, dmk = memattn_bwd(t["proj"], t["qblk"], t["kvm"], mqg3, mkg3, l, dmixed)
        sg["mem_q_gain"][l], sg["mem_k_gain"][l] = _pair_fold(dmq), _pair_fold(dmk)
        _, sg["mem_norm"][l], memn, dkvmb = proj_bwd(mems, norm3["mem_norm"], l, [dkvm], w_mkv[l], 0, zero_mem)
        dw_mkv[l] = wgrad(memn, dkvmb, chip_rows=w_mem_kv.shape[1])
        comm = ChipExchange(partial_r[:n_ffn]) if ride is not None else None
        if l < N_A:
            (dzq, dzf, dvi, dzg, dlb[l], sg["hgrn_o_gain"][l]), res = hgrn_bwd(t["proj"], lb3, og3, l, t["o"], dmixed, comm=comm)
            parts, tmw = [dzq, dzf, dvi, dzg, dqm], 13 * LANES
        else:
            lse_t = t["lse"].reshape(s, 6, LANES)[:, :, :2].reshape(s, 12).T
            lse_t = jnp.pad(lse_t, ((0, 4), (0, 0)))
            (dq, dgate, dk_sh, dv_sh, dclf, dfq), res = fox_bwd(t["proj"], kv["k"], kv["v"], kv["clf"], kv["clf_t"], fqg3, l - N_A, t["o"], t["lse"], lse_t, dmixed, dk_sh, dv_sh, dclf, comm=comm)
            sg["fox_q_gain"][l - N_A] = _pair_fold(dfq)
            parts, tmw = [dq, dgate, dqm], 7 * LANES
        if ride is not None:
            landed_r = list(res)
        dx, sg["mix_norm"][l], hn, dpb = proj_bwd(t["x1"], norm3["mix_norm"], l, parts, w_in[l], 0, dx, wt=True)
        dw_in[l] = wgrad(dpb, hn, tn=d // 2, tm=tmw, chip_rows=tmw // 2)
        comm = ChipExchange(partial_r[n_ffn:]) if ride is not None else None
        (dx, da, db, hm, xn, dyb, sg["ffn1_norm"][l]), res = ffn_bwd(t["x0"], norm3["ffn1_norm"], l, dx, t["a1"], t["b1"], *[got[(n, l)] for n in FFN1], 0, comm=comm)
        if ride is not None:
            unshared = (ride, names_r, chip_sums(partial_r, landed_r + list(res)))
        ffn_wgrads(FFN1, da, db, hm, xn, dyb)

    names0, gl0, gl16_0 = group_layout(0)
    recv0 = run_comm(PairExchange(gl16_0), "pair_exchange")
    partial0 = pair_sums(gl0, recv0)
    landed0 = run_comm(ChipExchange(partial0), "chip_exchange")
    mine0 = chip_sums(partial0, list(landed0))
    both = run_comm(PairShare(unshared[2] + mine0), "pair_share")
    reduced[unshared[0]] = dict(zip(unshared[1], both[: len(unshared[1])]))
    reduced[0] = dict(zip(names0, both[len(unshared[1]) :]))
    gparts = {n: [reduced[l][n] for l in range(N_LAYERS)] for n in PER_LAYER}
    gparts["w_in_a"] = [reduced[l]["w_in_a"] for l in range(N_A)]
    gparts["w_in_b"] = [reduced[l]["w_in_b"] for l in range(N_A, N_LAYERS)]
    gparts["w_kv"] = [reduced[N_A - 1]["w_kv"]]

    dlogits = lb_bwd(logits3, dlb[1]).reshape(2, -1)
    small = {
        "ffn1_norm": jnp.concatenate(sg["ffn1_norm"]), "mix_norm": jnp.concatenate(sg["mix_norm"]),
        "mem_norm": jnp.concatenate(sg["mem_norm"]), "ffn2_norm": jnp.concatenate(sg["ffn2_norm"]),
        "mem_q_gain": jnp.concatenate(sg["mem_q_gain"]), "mem_k_gain": jnp.concatenate(sg["mem_k_gain"]),
        "hgrn_o_gain": jnp.concatenate(sg["hgrn_o_gain"]), "fox_q_gain": jnp.concatenate(sg["fox_q_gain"]),
        "kv_norm": sg["kv_norm"], "fox_f_bias": dfb[:, : fox_f_bias.shape[0]], "fox_k_gain": _pair_fold(dfkg),
        "hgrn_lb_logits": dlogits,
    }
    flat = [small[n].reshape(-1) for n in SMALL] + [loss_local.reshape(-1)]
    sizes = [f.shape[0] for f in flat]
    total = sum(sizes)
    padded = -(-total // (8 * LANES)) * (8 * LANES)
    packed = jnp.pad(jnp.concatenate(flat), (0, padded - total)).reshape(-1, LANES)
    summed = small_allreduce(packed).reshape(-1)
    off = 0
    for n, sz in zip(SMALL, sizes[:-1]):
        gparts[n] = [summed[off : off + sz].reshape(dlogits.shape if n == "hgrn_lb_logits" else w[n].shape)]
        off += sz
    loss = summed[off]
    lbw = hgrn_lb_logits.shape[1]
    gparts["hgrn_lb_logits"] = [lax.dynamic_slice_in_dim(gparts["hgrn_lb_logits"][0], my_chip * lbw, lbw, axis=1)]

    def update(idx):
        ns = [WEIGHT_NAMES[i] for i in idx]
        return adamw([w[n] for n in ns], [gparts[n] for n in ns], [oriented(n, given["m_" + n]) for n in ns],
                     [oriented(n, given["v_" + n]) for n in ns])

    grads, delta, new_m, new_v = {}, {}, {}, {}
    for n, res in zip(WEIGHT_NAMES, _by_shape([w[n] for n in WEIGHT_NAMES], update)):
        grads[n], delta[n], new_m[n], new_v[n] = (oriented(n, t) for t in res)
    return (loss, dx[None], *[grads[n] for n in WEIGHT_NAMES], *[delta[n] for n in WEIGHT_NAMES],
            *[new_m[n] for n in WEIGHT_NAMES], *[new_v[n] for n in WEIGHT_NAMES])
```

```python
import jax
import jax.numpy as jnp
from jax import lax
from jax.experimental import pallas as pl
from jax.experimental.pallas import tpu as pltpu

F32, BF16 = jnp.float32, jnp.bfloat16
HI = lax.Precision.HIGHEST
EPS = 1e-6
MESH = pl.DeviceIdType.MESH
ANY = pl.BlockSpec(memory_space=pl.ANY)

VMEM_LIMIT_BYTES = 56 << 20
N_CHIPS = 4
N_DEV = 8
LANES = 128
HEAD64 = 64
CHUNK = 128
SUB = 32
HGRN_HEADS_PER_STEP = 2
TQ = 256
TOK = 256

ADAM_LR, ADAM_B1, ADAM_B2, ADAM_EPS, ADAM_WD, ADAM_STEP = 0.001, 0.9, 0.999, 1e-08, 0.01, 10


def _cparams(sem=None, **kw):
    return pltpu.CompilerParams(dimension_semantics=sem, vmem_limit_bytes=VMEM_LIMIT_BYTES, **kw)


def _mm(a, b, dims, prec=None):
    return lax.dot_general(a, b, (dims, ((), ())), preferred_element_type=F32, precision=prec)


def dot_nn(a, b, prec=None):
    return _mm(a, b, ((1,), (0,)), prec)


def dot_nt(a, b, prec=None):
    return _mm(a, b, ((1,), (1,)), prec)


def dot_tn(a, b, prec=None):
    return _mm(a, b, ((0,), (0,)), prec)


def bf(v):
    return v.astype(BF16)


def _sigmoid(z):
    return jax.nn.sigmoid(z)


def _dsilu(z, s):
    return s * (1.0 + z * (1.0 - s))


def _rms(x):
    r = lax.rsqrt(jnp.mean(x * x, axis=-1, keepdims=True) + EPS)
    return x * r, r


def _rms_bwd(dxn, u, r, g):
    du = dxn * g
    dx = r * (du - u * jnp.mean(du * u, axis=-1, keepdims=True))
    return dx, jnp.sum(dxn * u, axis=0, keepdims=True)


def _lane_mask0(shape):
    return lax.broadcasted_iota(jnp.int32, shape, len(shape) - 1) < HEAD64


def _rms64(x, m0):
    sq = x * x
    s0 = jnp.sum(jnp.where(m0, sq, 0.0), axis=-1, keepdims=True)
    s1 = jnp.sum(jnp.where(m0, 0.0, sq), axis=-1, keepdims=True)
    r = lax.rsqrt(jnp.where(m0, s0, s1) * (1.0 / HEAD64) + EPS)
    return x * r, r


def _rms64_bwd(dxn, u, r, g, m0):
    du = dxn * g
    t = du * u
    t0 = jnp.sum(jnp.where(m0, t, 0.0), axis=-1, keepdims=True)
    t1 = jnp.sum(jnp.where(m0, 0.0, t), axis=-1, keepdims=True)
    dx = r * (du - u * (jnp.where(m0, t0, t1) * (1.0 / HEAD64)))
    return dx, jnp.sum(dxn * u, axis=0, keepdims=True)


def _tok(s):
    return TOK if s % TOK == 0 else s


def _const(shape):
    return pl.BlockSpec(shape, lambda *_: (0,) * len(shape))


def ffn_fwd(x, gain3, l, wg, wu, wd, wl, comm=None):
    s, d = x.shape
    nc, _, fc, _ = wg.shape
    tm = _tok(s)

    def body(x_ref, g_ref, wg_ref, wu_ref, wd_ref, xo_ref, a_ref, b_ref):
        xv = x_ref[...]
        u, _ = _rms(xv)
        xn = bf(u * g_ref[...])
        y = jnp.zeros((tm, d), F32)
        for c in range(nc):
            a = dot_nt(xn, wg_ref[c])
            b = dot_nt(xn, wu_ref[c])
            a_ref[c] = bf(a)
            b_ref[c] = bf(b)
            y = y + dot_nn(bf(a * _sigmoid(a) * b), wd_ref[c])
        xo_ref[...] = xv + 0.5 * y

    wspec = pl.BlockSpec((nc, None, fc, d), lambda i: (0, wl, 0, 0), pipeline_mode=pl.Buffered(1))
    wdspec = pl.BlockSpec((nc, None, fc, d), lambda i: (0, wl, 0, 0), pipeline_mode=pl.Buffered(1))
    row = pl.BlockSpec((tm, d), lambda i: (i, 0))
    act = pl.BlockSpec((nc, tm, fc), lambda i: (0, i, 0))
    return _carry(
        body,
        comm,
        name="ffn_fwd",
        grid=(s // tm,),
        in_specs=[row, pl.BlockSpec((None, 1, d), lambda i: (l, 0, 0)), wspec, wspec, wdspec],
        out_specs=[row, act, act],
        out_shape=[
            jax.ShapeDtypeStruct((s, d), F32),
            jax.ShapeDtypeStruct((nc, s, fc), BF16),
            jax.ShapeDtypeStruct((nc, s, fc), BF16),
        ],
        scratch_shapes=[],
        args=(x, gain3, wg, wu, wd),
    )


def ffn_bwd(x, gain3, l, dout, a, b, wg, wu, wd, wl, comm=None):
    s, d = x.shape
    nc, _, fc, _ = wg.shape
    tm = _tok(s)

    def body(x_ref, g_ref, do_ref, a_ref, b_ref, wg_ref, wu_ref, wd_ref, dx_ref, da_ref, db_ref, hm_ref, xn_ref, dy_ref, dg_ref):
        xv = x_ref[...]
        g = g_ref[...]
        u, r = _rms(xv)
        xn_ref[...] = bf(u * g)
        dout = do_ref[...]
        dy = bf(0.5 * dout)
        dy_ref[...] = dy
        dxn = jnp.zeros((tm, d), F32)
        for c in range(nc):
            av = a_ref[c].astype(F32)
            bv = b_ref[c].astype(F32)
            sg = _sigmoid(av)
            sl = av * sg
            dh = dot_nt(dy, wd_ref[c])
            da = bf(dh * bv * _dsilu(av, sg))
            db = bf(dh * sl)
            da_ref[c] = da
            db_ref[c] = db
            hm_ref[c] = bf(sl * bv)
            dxn = dxn + dot_nn(da, wg_ref[c]) + dot_nn(db, wu_ref[c])
        dx, dg = _rms_bwd(dxn, u, r, g)
        dx_ref[...] = dout + dx

        @pl.when(pl.program_id(0) == 0)
        def _():
            dg_ref[...] = jnp.zeros_like(dg_ref)

        dg_ref[...] += dg

    wspec = pl.BlockSpec((nc, None, fc, d), lambda i: (0, wl, 0, 0), pipeline_mode=pl.Buffered(1))
    wdspec = pl.BlockSpec((nc, None, fc, d), lambda i: (0, wl, 0, 0), pipeline_mode=pl.Buffered(1))
    row = pl.BlockSpec((tm, d), lambda i: (i, 0))
    act = pl.BlockSpec((nc, tm, fc), lambda i: (0, i, 0))
    act_shape = jax.ShapeDtypeStruct((nc, s, fc), BF16)
    return _carry(
        body,
        comm,
        name="ffn_bwd",
        grid=(s // tm,),
        in_specs=[row, pl.BlockSpec((None, 1, d), lambda i: (l, 0, 0)), row, act, act, wspec, wspec, wdspec],
        out_specs=[row, act, act, act, row, row, _const((1, d))],
        out_shape=[
            jax.ShapeDtypeStruct((s, d), F32),
            act_shape,
            act_shape,
            act_shape,
            jax.ShapeDtypeStruct((s, d), BF16),
            jax.ShapeDtypeStruct((s, d), BF16),
            jax.ShapeDtypeStruct((1, d), F32),
        ],
        scratch_shapes=[],
        args=(x, gain3, dout, a, b, wg, wu, wd),
    )


def wgrad(a, b, tn=None, tm=None, split=False, chip_rows=None):
    ca = a.shape[0] if a.ndim == 3 else 1
    cb = b.shape[0] if b.ndim == 3 else 1
    nc = max(ca, cb)
    s, m = a.shape[-2:]
    n = b.shape[-1]
    tn = n if tn is None else tn
    assert n % tn == 0
    tm = m if tm is None else tm
    per_tile = None if chip_rows is None else tm // chip_rows

    def body(*refs):
        a_ref, b_ref = refs[0], refs[1]
        res = dot_tn(a_ref[...], b_ref[...])
        if split:
            for o in refs[2:]:
                o[0] = res[: m // 2].astype(o.dtype)
                o[1] = res[m // 2 :].astype(o.dtype)
        elif chip_rows is not None:
            hr = chip_rows // 2
            for o in refs[2:]:
                for k in range(per_tile):
                    for hf in range(2):
                        r0 = k * chip_rows + hf * hr
                        o[hf, k] = res[r0 : r0 + hr].astype(o.dtype)
        else:
            refs[2][...] = res

    params = _cparams(("arbitrary", "arbitrary"))
    if not split:
        assert nc == 1 and a.ndim == 2 and b.ndim == 2 and m % tm == 0
        in_specs = [pl.BlockSpec((s, tm), lambda i, j: (0, i)), pl.BlockSpec((s, tn), lambda i, j: (0, j))]
        if chip_rows is None:
            return pl.pallas_call(
                body,
                name="wgrad",
                grid=(m // tm, n // tn),
                in_specs=in_specs,
                out_specs=pl.BlockSpec((tm, tn), lambda i, j: (i, j)),
                out_shape=jax.ShapeDtypeStruct((m, n), F32),
                compiler_params=params,
            )(a, b)
        assert tm % chip_rows == 0
        laid = pl.BlockSpec((2, per_tile, chip_rows // 2, tn), lambda i, j: (0, i, 0, j))
        shape = (2, m // chip_rows, chip_rows // 2, n)
        return pl.pallas_call(
            body,
            name="wgrad_chips",
            grid=(m // tm, n // tn),
            in_specs=in_specs,
            out_specs=[laid, laid],
            out_shape=[jax.ShapeDtypeStruct(shape, F32), jax.ShapeDtypeStruct(shape, BF16)],
            compiler_params=params,
        )(a, b)
    a_spec = pl.BlockSpec((None, s, m), lambda c, j: (c, 0, 0)) if a.ndim == 3 else pl.BlockSpec((s, m), lambda c, j: (0, 0))
    b_spec = pl.BlockSpec((None, s, tn), lambda c, j: (c, 0, j)) if b.ndim == 3 else pl.BlockSpec((s, tn), lambda c, j: (0, j))
    halves = pl.BlockSpec((2, None, None, m // 2, tn), lambda c, j: (0, c, 0, 0, j))
    return pl.pallas_call(
        body,
        name="wgrad_split",
        grid=(nc, n // tn),
        in_specs=[a_spec, b_spec],
        out_specs=[halves, halves],
        out_shape=[jax.ShapeDtypeStruct((2, nc, 1, m // 2, n), F32), jax.ShapeDtypeStruct((2, nc, 1, m // 2, n), BF16)],
        compiler_params=params,
    )(a, b)


def proj_fwd(x, gain3, l, w, wl, wt=False):
    s, d = x.shape
    n = w.shape[1] if wt else w.shape[2]
    tm = _tok(s)

    def body(x_ref, g_ref, w_ref, o_ref):
        u, _ = _rms(x_ref[...])
        xn = bf(u * g_ref[...])
        o_ref[...] = dot_nt(xn, w_ref[...]) if wt else dot_nn(xn, w_ref[...])

    return pl.pallas_call(
        body,
        name="proj_fwd",
        grid=(s // tm,),
        in_specs=[
            pl.BlockSpec((tm, d), lambda i: (i, 0)),
            pl.BlockSpec((None, 1, d), lambda i: (l, 0, 0)),
            pl.BlockSpec((None,) + w.shape[1:], lambda i: (wl, 0, 0)),
        ],
        out_specs=pl.BlockSpec((tm, n), lambda i: (i, 0)),
        out_shape=jax.ShapeDtypeStruct((s, n), F32),
        compiler_params=_cparams(("arbitrary",)),
    )(x, gain3, w)


def proj_bwd(x, gain3, l, parts, w, wl, dx_in, wt=False):
    s, d = x.shape
    n = w.shape[1] if wt else w.shape[2]
    widths = [p.shape[1] for p in parts]
    assert sum(widths) == n
    tm = _tok(s)
    npart = len(parts)

    def body(*refs):
        x_ref, g_ref, w_ref, dxin_ref = refs[:4]
        p_refs = refs[4 : 4 + npart]
        dx_ref, dg_ref, xn_ref, dpb_ref = refs[4 + npart :]
        g = g_ref[...]
        u, r = _rms(x_ref[...])
        xn_ref[...] = bf(u * g)
        dxn = jnp.zeros((tm, d), F32)
        off = 0
        for p_ref, wd_ in zip(p_refs, widths):
            dp = bf(p_ref[...])
            dpb_ref[:, off : off + wd_] = dp
            dxn = dxn + (dot_nn(dp, w_ref[off : off + wd_, :]) if wt else dot_nt(dp, w_ref[:, off : off + wd_]))
            off += wd_
        dx, dg = _rms_bwd(dxn, u, r, g)
        dx_ref[...] = dxin_ref[...] + dx

        @pl.when(pl.program_id(0) == 0)
        def _():
            dg_ref[...] = jnp.zeros_like(dg_ref)

        dg_ref[...] += dg

    row = pl.BlockSpec((tm, d), lambda i: (i, 0))
    return pl.pallas_call(
        body,
        name="proj_bwd",
        grid=(s // tm,),
        in_specs=[row, pl.BlockSpec((None, 1, d), lambda i: (l, 0, 0)), pl.BlockSpec((None,) + w.shape[1:], lambda i: (wl, 0, 0)), row]
        + [pl.BlockSpec((tm, wd_), lambda i: (i, 0)) for wd_ in widths],
        out_specs=[row, _const((1, d)), row, pl.BlockSpec((tm, n), lambda i: (i, 0))],
        out_shape=[
            jax.ShapeDtypeStruct((s, d), F32),
            jax.ShapeDtypeStruct((1, d), F32),
            jax.ShapeDtypeStruct((s, d), BF16),
            jax.ShapeDtypeStruct((s, n), BF16),
        ],
        compiler_params=_cparams(("arbitrary",)),
    )(x, gain3, w, dx_in, *parts)


def mm_res(x, a, w, l):
    s, d = x.shape
    k = a.shape[1]
    tm = _tok(s)

    def body(x_ref, a_ref, w_ref, o_ref):
        o_ref[...] = x_ref[...] + dot_nn(a_ref[...], w_ref[...])

    return pl.pallas_call(
        body,
        name="mm_res",
        grid=(s // tm,),
        in_specs=[
            pl.BlockSpec((tm, d), lambda i: (i, 0)),
            pl.BlockSpec((tm, k), lambda i: (i, 0)),
            pl.BlockSpec((None, k, d), lambda i: (l, 0, 0)),
        ],
        out_specs=pl.BlockSpec((tm, d), lambda i: (i, 0)),
        out_shape=jax.ShapeDtypeStruct((s, d), F32),
        compiler_params=_cparams(("arbitrary",)),
    )(x, a, w)


def mm_nt(dx, w, l):
    s, d = dx.shape
    k = w.shape[1]
    tm = _tok(s)

    def body(dx_ref, w_ref, o_ref, dxb_ref):
        dxb = bf(dx_ref[...])
        dxb_ref[...] = dxb
        o_ref[...] = dot_nt(dxb, w_ref[...])

    return pl.pallas_call(
        body,
        name="mm_nt",
        grid=(s // tm,),
        in_specs=[pl.BlockSpec((tm, d), lambda i: (i, 0)), pl.BlockSpec((None, k, d), lambda i: (l, 0, 0))],
        out_specs=[pl.BlockSpec((tm, k), lambda i: (i, 0)), pl.BlockSpec((tm, d), lambda i: (i, 0))],
        out_shape=[jax.ShapeDtypeStruct((s, k), F32), jax.ShapeDtypeStruct((s, d), BF16)],
        compiler_params=_cparams(("arbitrary",)),
    )(dx, w)


def lb_fwd(logits3):
    def body(l_ref, o_ref):
        l0, l1 = l_ref[0], l_ref[1]
        m = jnp.maximum(l0, l1)
        e0, e1 = jnp.exp(l0 - m), jnp.exp(l1 - m)
        p0, p1 = e0 / (e0 + e1), e1 / (e0 + e1)
        o_ref[0] = p0 - p0
        o_ref[1] = (p0 + p1) - p0

    return pl.pallas_call(body, name="lb_fwd", out_shape=jax.ShapeDtypeStruct(logits3.shape, F32))(logits3)


def lb_bwd(logits3, dlb1):
    def body(l_ref, d_ref, o_ref):
        l0, l1 = l_ref[0], l_ref[1]
        m = jnp.maximum(l0, l1)
        e0, e1 = jnp.exp(l0 - m), jnp.exp(l1 - m)
        p0, p1 = e0 / (e0 + e1), e1 / (e0 + e1)
        t = d_ref[...] * p0 * p1
        o_ref[0] = -t
        o_ref[1] = t

    return pl.pallas_call(body, name="lb_bwd", out_shape=jax.ShapeDtypeStruct(logits3.shape, F32))(logits3, dlb1)


def _hgrn_gates(zq, zf, lb):
    sf = _sigmoid(zf)
    f = lb + (1.0 - lb) * sf
    sq = _sigmoid(zq)
    return sf, f, jnp.log(f), 1.0 - f, sq, zq * sq


def _tri(n, upper=False):
    r = lax.broadcasted_iota(jnp.int32, (n, n), 0)
    c = lax.broadcasted_iota(jnp.int32, (n, n), 1)
    return jnp.where((c >= r) if upper else (r >= c), 1.0, 0.0).astype(F32)


def hgrn_fwd(proj, lb3, og3, l, comm=None):
    s = proj.shape[0]
    nh = 6
    n_chunk = s // CHUNK
    nsub = CHUNK // SUB

    hb = HGRN_HEADS_PER_STEP
    wide = hb * LANES

    def body(zq_ref, zf_ref, vi_ref, zg_ref, lb_ref, og_ref, main_ref, o_ref, q_a, k_a, v_a, c_a):
        og = og_ref[...]
        tril = _tri(CHUNK)
        rowi = lax.broadcasted_iota(jnp.int32, (SUB, LANES), 0)

        def one_head(hd, rows, st):
            cols = slice(hd * LANES, (hd + 1) * LANES)
            q_s, k_s, v_s, c_s = q_a.at[hd], k_a.at[hd], v_a.at[hd], c_a.at[hd]
            zg = zg_ref[rows, cols]
            _, _, lf, k, _, q = _hgrn_gates(zq_ref[rows, cols], zf_ref[rows, cols], lb_ref[:, cols])
            v = vi_ref[rows, cols]
            c = dot_nn(tril, lf, HI)
            q_s[...] = q
            k_s[...] = k
            v_s[...] = v
            c_s[...] = c
            o_inter = dot_nt(q * jnp.exp(c), st, HI)
            parts = []
            for i in range(nsub):
                lo = i * SUB
                blk = pl.ds(lo, SUB)
                qb, cb = q_s[blk, :], c_s[blk, :]
                ob = o_inter[lo : lo + SUB]
                if i > 0:
                    rr = c_s[pl.ds(lo - 1, 1), :]
                    qt = qb * jnp.exp(cb - rr)
                    kt = k_s[pl.ds(0, lo), :] * jnp.exp(rr - c_s[pl.ds(0, lo), :])
                    ob = ob + dot_nn(dot_nt(qt, kt, HI), v_s[pl.ds(0, lo), :], HI)
                for t in range(SUB):
                    e = jnp.where(rowi >= t, jnp.exp(cb - c_s[pl.ds(lo + t, 1), :]), 0.0)
                    a = jnp.sum(qb * k_s[pl.ds(lo + t, 1), :] * e, axis=-1, keepdims=True)
                    ob = ob + a * v_s[pl.ds(lo + t, 1), :]
                parts.append(ob)
            o = jnp.concatenate(parts, axis=0)
            ce = c_s[pl.ds(CHUNK - 1, 1), :]
            st = st * jnp.exp(ce) + dot_tn(v, k * jnp.exp(ce - c), HI)
            on, _ = _rms(o)
            o_ref[rows, cols] = o
            main_ref[rows, cols] = bf(on * og * (zg * _sigmoid(zg)))
            return st

        def chunk(ci, sts):
            rows = pl.ds(pl.multiple_of(ci * CHUNK, CHUNK), CHUNK)
            return tuple(one_head(hd, rows, sts[hd]) for hd in range(hb))

        lax.fori_loop(0, n_chunk, chunk, tuple(jnp.zeros((LANES, LANES), F32) for _ in range(hb)))

    def col(k):
        return pl.BlockSpec((s, wide), lambda h: (0, k * (nh // hb) + h))

    vec = pl.BlockSpec((None, 1, wide), lambda h: (l, 0, h))
    return _carry(
        body,
        comm,
        name="hgrn_fwd",
        grid=(nh // hb,),
        in_specs=[col(0), col(1), col(2), col(3), vec, pl.BlockSpec((None, 1, LANES), lambda h: (l, 0, 0))],
        out_specs=[pl.BlockSpec((s, wide), lambda h: (0, h))] * 2,
        out_shape=[jax.ShapeDtypeStruct((s, nh * LANES), BF16), jax.ShapeDtypeStruct((s, nh * LANES), F32)],
        scratch_shapes=[pltpu.VMEM((hb, CHUNK, LANES), F32)] * 4,
        args=(proj, proj, proj, proj, lb3, og3),
    )


def hgrn_bwd(proj, lb3, og3, l, o, dmixed, comm=None):
    s = proj.shape[0]
    nh = 6
    n_chunk = s // CHUNK
    nsub = CHUNK // SUB

    hb = HGRN_HEADS_PER_STEP
    wide = hb * LANES

    def body(zq_ref, zf_ref, vi_ref, zg_ref, lb_ref, og_ref, o_ref, dm_ref,
             dzq_ref, dzf_ref, dvi_ref, dzg_ref, dlb_ref, dog_ref,
             st_a, q_a, k_a, v_a, c_a, do_a, dq_a, dk_a, dv_a, acc_a):
        og = og_ref[...]
        tril = _tri(CHUNK)
        triu = _tri(CHUNK, upper=True)
        rowi = lax.broadcasted_iota(jnp.int32, (SUB, LANES), 0)

        def fwd_head(hd, ci, rows, st):
            cols = slice(hd * LANES, (hd + 1) * LANES)
            _, _, lf, k, _, _ = _hgrn_gates(zq_ref[rows, cols], zf_ref[rows, cols], lb_ref[:, cols])
            c = dot_nn(tril, lf, HI)
            ce = jnp.sum(lf, axis=0, keepdims=True)
            st_a[hd, ci] = st
            return st * jnp.exp(ce) + dot_tn(vi_ref[rows, cols], k * jnp.exp(ce - c), HI)

        def fwd_chunk(ci, sts):
            rows = pl.ds(pl.multiple_of(ci * CHUNK, CHUNK), CHUNK)
            return tuple(fwd_head(hd, ci, rows, sts[hd]) for hd in range(hb))

        lax.fori_loop(0, n_chunk, fwd_chunk, tuple(jnp.zeros((LANES, LANES), F32) for _ in range(hb)))
        acc_a[...] = jnp.zeros_like(acc_a)

        def bwd_head(hd, ci, rows, carry):
            dst, cg = carry
            cols = slice(hd * LANES, (hd + 1) * LANES)
            q_s, k_s, v_s, c_s, do_s = q_a.at[hd], k_a.at[hd], v_a.at[hd], c_a.at[hd], do_a.at[hd]
            dq_s, dk_s, dv_s, acc_s = dq_a.at[hd], dk_a.at[hd], dv_a.at[hd], acc_a.at[hd]
            lb = lb_ref[:, cols]
            zq, zf, zg = zq_ref[rows, cols], zf_ref[rows, cols], zg_ref[rows, cols]
            sf, f, lf, k, sq, q = _hgrn_gates(zq, zf, lb)
            v = vi_ref[rows, cols]
            c = dot_nn(tril, lf, HI)
            st = st_a[hd, ci]
            on, r = _rms(o_ref[rows, cols])
            sg = _sigmoid(zg)
            dmain = dm_ref[rows, cols]
            dy = dmain * (zg * sg)
            dzg_ref[rows, cols] = dmain * (on * og) * _dsilu(zg, sg)
            do, dog = _rms_bwd(dy, on, r, og)
            acc_s[pl.ds(0, 1), :] += dog
            q_s[...] = q
            k_s[...] = k
            v_s[...] = v
            c_s[...] = c
            do_s[...] = do
            ce = c_s[pl.ds(CHUNK - 1, 1), :]
            eq = jnp.exp(c)
            ek = jnp.exp(ce - c)
            qt_all = q * eq
            dq_s[...] = dot_nn(do, st, HI) * eq
            dv_s[...] = dot_nt(k * ek, dst, HI)
            dk_s[...] = dot_nn(v, dst, HI) * ek
            dst = dst * jnp.exp(ce) + dot_tn(do, qt_all, HI)
            for i in range(nsub):
                lo = i * SUB
                blk = pl.ds(lo, SUB)
                qb, cb, dob = q_s[blk, :], c_s[blk, :], do_s[blk, :]
                if i > 0:
                    prev = pl.ds(0, lo)
                    rr = c_s[pl.ds(lo - 1, 1), :]
                    eqi = jnp.exp(cb - rr)
                    eki = jnp.exp(rr - c_s[prev, :])
                    qt = qb * eqi
                    kt = k_s[prev, :] * eki
                    amat = dot_nt(qt, kt, HI)
                    damat = dot_nt(dob, v_s[prev, :], HI)
                    dv_s[prev, :] += dot_tn(amat, dob, HI)
                    dq_s[blk, :] += dot_nn(damat, kt, HI) * eqi
                    dk_s[prev, :] += dot_tn(damat, qt, HI) * eki
                hs = SUB // 2
                tiles = [(r0, qb[r0 : r0 + hs], cb[r0 : r0 + hs], dob[r0 : r0 + hs]) for r0 in (0, hs)]
                dqh = [jnp.zeros((hs, LANES), F32), jnp.zeros((hs, LANES), F32)]
                for t in range(SUB):
                    row = pl.ds(lo + t, 1)
                    kr, vr, cr = k_s[row, :], v_s[row, :], c_s[row, :]
                    dv_t = dk_t = None
                    for hi, (r0, qh, ch, doh) in enumerate(tiles):
                        if t >= r0 + hs:
                            continue
                        e = jnp.exp(ch - cr)
                        if t > r0:
                            e = jnp.where(rowi[:hs] >= t - r0, e, 0.0)
                        a = jnp.sum(qh * kr * e, axis=-1, keepdims=True)
                        da = jnp.sum(doh * vr, axis=-1, keepdims=True)
                        dv_p = jnp.sum(a * doh, axis=0, keepdims=True)
                        dk_p = jnp.sum(da * qh * e, axis=0, keepdims=True)
                        dqh[hi] = dqh[hi] + da * kr * e
                        dv_t = dv_p if dv_t is None else dv_t + dv_p
                        dk_t = dk_p if dk_t is None else dk_t + dk_p
                    dv_s[row, :] += dv_t
                    dk_s[row, :] += dk_t
                dq_s[pl.ds(lo, hs), :] += dqh[0]
                dq_s[pl.ds(lo + hs, hs), :] += dqh[1]
            dq, dk = dq_s[...], dk_s[...]
            dg = q * dq - k * dk
            dlf = dot_nn(triu, dg, HI) + cg
            cg = cg + jnp.sum(dg, axis=0, keepdims=True)
            df = dlf / f - dk
            dzf_ref[rows, cols] = df * (1.0 - lb) * sf * (1.0 - sf)
            acc_s[pl.ds(1, 1), :] += jnp.sum(df * (1.0 - sf), axis=0, keepdims=True)
            dzq_ref[rows, cols] = dq * _dsilu(zq, sq)
            dvi_ref[rows, cols] = dv_s[...]
            return dst, cg

        def bwd_chunk(jj, carries):
            ci = n_chunk - 1 - jj
            rows = pl.ds(pl.multiple_of(ci * CHUNK, CHUNK), CHUNK)
            return tuple(bwd_head(hd, ci, rows, carries[hd]) for hd in range(hb))

        zero = (jnp.zeros((LANES, LANES), F32), jnp.zeros((1, LANES), F32))
        lax.fori_loop(0, n_chunk, bwd_chunk, tuple(zero for _ in range(hb)))

        @pl.when(pl.program_id(0) == 0)
        def _():
            dog_ref[...] = jnp.zeros_like(dog_ref)

        for hd in range(hb):
            dlb_ref[:, hd * LANES : (hd + 1) * LANES] = acc_a[hd, pl.ds(1, 1), :]
            dog_ref[...] += acc_a[hd, pl.ds(0, 1), :]

    def col(k):
        return pl.BlockSpec((s, wide), lambda h: (0, k * (nh // hb) + h), pipeline_mode=pl.Buffered(1))

    head_in = pl.BlockSpec((s, wide), lambda h: (0, h), pipeline_mode=pl.Buffered(1))
    head = pl.BlockSpec((s, wide), lambda h: (0, h))
    vec = pl.BlockSpec((None, 1, wide), lambda h: (l, 0, h))
    ck = pltpu.VMEM((hb, CHUNK, LANES), F32)
    return _carry(
        body,
        comm,
        name="hgrn_bwd",
        grid=(nh // hb,),
        in_specs=[col(0), col(1), col(2), col(3), vec, pl.BlockSpec((None, 1, LANES), lambda h: (l, 0, 0)), head_in, head_in],
        out_specs=[head] * 4 + [pl.BlockSpec((1, wide), lambda h: (0, h)), _const((1, LANES))],
        out_shape=[jax.ShapeDtypeStruct((s, nh * LANES), F32)] * 4
        + [jax.ShapeDtypeStruct((1, nh * LANES), F32), jax.ShapeDtypeStruct((1, LANES), F32)],
        scratch_shapes=[pltpu.VMEM((hb, n_chunk, LANES, LANES), F32)] + [ck] * 8 + [pltpu.VMEM((hb, 8, LANES), F32)],
        args=(proj, proj, proj, proj, lb3, og3, o, dmixed),
    )


MEM_SCALE = HEAD64**-0.5


def _mem_heads(qraw, kvm, qg, kg, pr, m0):
    lo = pr * LANES
    uq, rq = _rms64(qraw[:, lo : lo + LANES], m0)
    uk, rk = _rms64(kvm[:, lo : lo + LANES], m0)
    v = bf(kvm[:, 2 * LANES + lo : 3 * LANES + lo])
    return uq, rq, uk, rk, v, uq * qg, bf(uk * kg)


def memattn_fwd(proj, qblk, kvm, qg3, kg3, l):
    s = proj.shape[0]
    nm = kvm.shape[0]
    tm = _tok(s)

    def body(q_ref, kv_ref, qg_ref, kg_ref, o_ref):
        m0 = _lane_mask0((1, LANES))
        qraw, kvv = q_ref[...], kv_ref[...]
        for pr in range(2):
            _, _, _, _, v, qn, kn = _mem_heads(qraw, kvv, qg_ref[...], kg_ref[...], pr, m0)
            out = jnp.zeros((tm, LANES), F32)
            for hh in range(2):
                mh = m0 if hh == 0 else jnp.logical_not(m0)
                sc = dot_nt(bf(jnp.where(mh, qn, 0.0)), kn) * MEM_SCALE
                p = jnp.exp(sc - jnp.max(sc, axis=-1, keepdims=True))
                p = p / jnp.sum(p, axis=-1, keepdims=True)
                out = jnp.where(mh, dot_nn(bf(p), v), out)
            o_ref[:, pr * LANES : (pr + 1) * LANES] = bf(out)

    gspec = pl.BlockSpec((None, 1, LANES), lambda i: (l, 0, 0))
    return pl.pallas_call(
        body,
        name="memattn_fwd",
        grid=(s // tm,),
        in_specs=[pl.BlockSpec((tm, 2 * LANES), lambda i: (i, qblk)), _const((nm, 4 * LANES)), gspec, gspec],
        out_specs=pl.BlockSpec((tm, 2 * LANES), lambda i: (i, 0)),
        out_shape=jax.ShapeDtypeStruct((s, 2 * LANES), BF16),
        compiler_params=_cparams(("arbitrary",)),
    )(proj, kvm, qg3, kg3)


def memattn_bwd(proj, qblk, kvm, qg3, kg3, l, dmixed):
    s = proj.shape[0]
    nm = kvm.shape[0]
    tm = _tok(s)

    def body(q_ref, kv_ref, qg_ref, kg_ref, dm_ref, dq_ref, dkv_ref, dqg_ref, dkg_ref):
        m0 = _lane_mask0((1, LANES))
        qraw, kvv = q_ref[...], kv_ref[...]
        qg, kg = qg_ref[...], kg_ref[...]

        @pl.when(pl.program_id(0) == 0)
        def _():
            dkv_ref[...] = jnp.zeros_like(dkv_ref)
            dqg_ref[...] = jnp.zeros_like(dqg_ref)
            dkg_ref[...] = jnp.zeros_like(dkg_ref)

        for pr in range(2):
            lo = pr * LANES
            uq, rq, uk, rk, v, qn, kn = _mem_heads(qraw, kvv, qg, kg, pr, m0)
            do = dm_ref[:, lo : lo + LANES]
            dqn = jnp.zeros((tm, LANES), F32)
            dkn = jnp.zeros((nm, LANES), F32)
            dv = jnp.zeros((nm, LANES), F32)
            for hh in range(2):
                mh = m0 if hh == 0 else jnp.logical_not(m0)
                qh = bf(jnp.where(mh, qn, 0.0))
                doh = bf(jnp.where(mh, do, 0.0))
                sc = dot_nt(qh, kn) * MEM_SCALE
                p = jnp.exp(sc - jnp.max(sc, axis=-1, keepdims=True))
                p = p / jnp.sum(p, axis=-1, keepdims=True)
                dp = dot_nt(doh, v)
                ds = bf(p * (dp - jnp.sum(p * dp, axis=-1, keepdims=True)))
                dqn = dqn + jnp.where(mh, dot_nn(ds, kn), 0.0) * MEM_SCALE
                dkn = dkn + dot_tn(ds, qh) * MEM_SCALE
                dv = dv + dot_tn(bf(p), doh)
            dqr, dqg = _rms64_bwd(dqn, uq, rq, qg, m0)
            dkr, dkg = _rms64_bwd(dkn, uk, rk, kg, m0)
            dq_ref[:, lo : lo + LANES] = dqr
            dkv_ref[:, lo : lo + LANES] += dkr
            dkv_ref[:, 2 * LANES + lo : 3 * LANES + lo] += dv
            dqg_ref[...] += dqg
            dkg_ref[...] += dkg

    gspec = pl.BlockSpec((None, 1, LANES), lambda i: (l, 0, 0))
    return pl.pallas_call(
        body,
        name="memattn_bwd",
        grid=(s // tm,),
        in_specs=[
            pl.BlockSpec((tm, 2 * LANES), lambda i: (i, qblk)),
            _const((nm, 4 * LANES)),
            gspec,
            gspec,
            pl.BlockSpec((tm, 2 * LANES), lambda i: (i, 3)),
        ],
        out_specs=[pl.BlockSpec((tm, 2 * LANES), lambda i: (i, 0)), _const((nm, 4 * LANES)), _const((1, LANES)), _const((1, LANES))],
        out_shape=[
            jax.ShapeDtypeStruct((s, 2 * LANES), F32),
            jax.ShapeDtypeStruct((nm, 4 * LANES), F32),
            jax.ShapeDtypeStruct((1, LANES), F32),
            jax.ShapeDtypeStruct((1, LANES), F32),
        ],
        compiler_params=_cparams(("arbitrary",)),
    )(proj, kvm, qg3, kg3, dmixed)


KV_MAIN = 768


def _log_sigmoid(z):
    return jnp.minimum(z, 0.0) - jnp.log(1.0 + jnp.exp(-jnp.abs(z)))


def kvprep_fwd(kvf, kg, fb):
    s = kvf.shape[0]
    tm = _tok(s)

    def body(kvf_ref, kg_ref, fb_ref, k_ref, v_ref, clf_ref, carry):
        m0 = _lane_mask0((1, LANES))

        @pl.when(pl.program_id(0) == 0)
        def _():
            carry[...] = jnp.zeros_like(carry)

        for j in range(KV_MAIN // LANES):
            u, _ = _rms64(kvf_ref[:, j * LANES : (j + 1) * LANES], m0)
            k_ref[:, j * LANES : (j + 1) * LANES] = bf(u * kg_ref[...])
        v_ref[...] = bf(kvf_ref[:, KV_MAIN : 2 * KV_MAIN])
        lf = _log_sigmoid(kvf_ref[:, 2 * KV_MAIN :] + fb_ref[...])
        clf_ref[...] = dot_nn(_tri(tm), lf, HI) + carry[...]
        carry[...] += jnp.sum(lf, axis=0, keepdims=True)

    n = kvf.shape[1]
    return pl.pallas_call(
        body,
        name="kvprep_fwd",
        grid=(s // tm,),
        in_specs=[pl.BlockSpec((tm, n), lambda i: (i, 0)), _const((1, LANES)), _const((1, LANES))],
        out_specs=[pl.BlockSpec((tm, KV_MAIN), lambda i: (i, 0))] * 2 + [pl.BlockSpec((tm, LANES), lambda i: (i, 0))],
        out_shape=[jax.ShapeDtypeStruct((s, KV_MAIN), BF16)] * 2 + [jax.ShapeDtypeStruct((s, LANES), F32)],
        scratch_shapes=[pltpu.VMEM((1, LANES), F32)],
        compiler_params=_cparams(("arbitrary",)),
    )(kvf, kg, fb)


def kvprep_bwd(kvf, kg, fb, dk, dv, dclf):
    s, n = kvf.shape
    tm = _tok(s)
    nb = s // tm

    def body(kvf_ref, kg_ref, fb_ref, dk_ref, dv_ref, dclf_ref, o_ref, dkg_ref, dfb_ref, carry):
        m0 = _lane_mask0((1, LANES))

        @pl.when(pl.program_id(0) == 0)
        def _():
            carry[...] = jnp.zeros_like(carry)
            dkg_ref[...] = jnp.zeros_like(dkg_ref)
            dfb_ref[...] = jnp.zeros_like(dfb_ref)

        kg_ = kg_ref[...]
        for j in range(KV_MAIN // LANES):
            cols = slice(j * LANES, (j + 1) * LANES)
            u, r = _rms64(kvf_ref[:, cols], m0)
            dkr, dkg = _rms64_bwd(dk_ref[:, cols], u, r, kg_, m0)
            o_ref[:, cols] = dkr
            dkg_ref[...] += dkg
        o_ref[:, KV_MAIN : 2 * KV_MAIN] = dv_ref[...]
        z = kvf_ref[:, 2 * KV_MAIN :] + fb_ref[...]
        dc = dclf_ref[...]
        dlf = dot_nn(_tri(tm, upper=True), dc, HI) + carry[...]
        carry[...] += jnp.sum(dc, axis=0, keepdims=True)
        dz = dlf * _sigmoid(-z)
        o_ref[:, 2 * KV_MAIN :] = dz
        dfb_ref[...] += jnp.sum(dz, axis=0, keepdims=True)

    rev = lambda i: (nb - 1 - i, 0)
    return pl.pallas_call(
        body,
        name="kvprep_bwd",
        grid=(nb,),
        in_specs=[pl.BlockSpec((tm, n), rev), _const((1, LANES)), _const((1, LANES)), pl.BlockSpec((tm, KV_MAIN), rev),
                  pl.BlockSpec((tm, KV_MAIN), rev), pl.BlockSpec((tm, LANES), rev)],
        out_specs=[pl.BlockSpec((tm, n), rev), _const((1, LANES)), _const((1, LANES))],
        out_shape=[jax.ShapeDtypeStruct((s, n), F32), jax.ShapeDtypeStruct((1, LANES), F32), jax.ShapeDtypeStruct((1, LANES), F32)],
        scratch_shapes=[pltpu.VMEM((1, LANES), F32)],
        compiler_params=_cparams(("arbitrary",)),
    )(kvf, kg, fb, dk, dv, dclf)


FOX_SCALE = HEAD64**-0.5


def _lane_col(block, lane_idx, h):
    return jnp.sum(jnp.where(lane_idx == h, block, 0.0), axis=-1, keepdims=True)


def _causal(tq, ext, i, transposed=False):
    if transposed:
        key = lax.broadcasted_iota(jnp.int32, (ext, tq), 0)
        qry = lax.broadcasted_iota(jnp.int32, (ext, tq), 1) + i * tq
    else:
        qry = lax.broadcasted_iota(jnp.int32, (tq, ext), 0) + i * tq
        key = lax.broadcasted_iota(jnp.int32, (tq, ext), 1)
    return key <= qry


def fox_fwd(proj, k_sh, v_sh, clf, clf_t, qg3, j_layer, comm=None):
    s = proj.shape[0]
    npair = 6
    tq = TQ if s % TQ == 0 else s
    nq = s // tq

    def body(q_ref, gate_ref, k_ref, v_ref, clf_ref, clft_ref, qg_ref, main_ref, o_ref, lse_ref):
        j = pl.program_id(0)
        lane = lax.broadcasted_iota(jnp.int32, (1, LANES), 1)
        m0 = lane < HEAD64
        u, _ = _rms64(q_ref[...], m0)
        qn = u * qg_ref[...] * FOX_SCALE
        clfv = clf_ref[...]
        for hh in range(2):
            h = 2 * j + hh
            mh = m0 if hh == 0 else jnp.logical_not(m0)
            qh = bf(jnp.where(mh, qn, 0.0))
            dcol = _lane_col(clfv, lane, h)
            drow = clft_ref[pl.ds(h, 1), :]
            for i in range(nq):
                rows = slice(i * tq, (i + 1) * tq)
                ext = (i + 1) * tq
                sc = dot_nt(qh[rows], k_ref[0:ext, :]) + dcol[rows] - drow[:, :ext]
                sc = jnp.where(_causal(tq, ext, i), sc, -jnp.inf)
                m = jnp.max(sc, axis=-1, keepdims=True)
                p = jnp.exp(sc - m)
                lsum = jnp.sum(p, axis=-1, keepdims=True)
                pv = dot_nn(bf(p), v_ref[0:ext, :]) / lsum
                lse = m + jnp.log(lsum)
                if hh == 0:
                    o_ref[rows, :] = pv
                    lse_ref[rows, :] = jnp.where(lane == 0, lse, 0.0)
                else:
                    o_ref[rows, :] = jnp.where(mh, pv, o_ref[rows, :])
                    lse_ref[rows, :] = jnp.where(lane == 1, lse, lse_ref[rows, :])
        main_ref[...] = bf(o_ref[...] * _sigmoid(gate_ref[...]))

    blk = lambda off: pl.BlockSpec((s, LANES), lambda j: (0, off + j))
    return _carry(
        body,
        comm,
        name="fox_fwd",
        grid=(npair,),
        in_specs=[blk(0), blk(npair), blk(0), blk(0), _const((s, LANES)), _const((16, s)),
                  pl.BlockSpec((None, 1, LANES), lambda j: (j_layer, 0, 0))],
        out_specs=[blk(0)] * 3,
        out_shape=[jax.ShapeDtypeStruct((s, npair * LANES), BF16)] + [jax.ShapeDtypeStruct((s, npair * LANES), F32)] * 2,
        scratch_shapes=[],
        args=(proj, proj, k_sh, v_sh, clf, clf_t, qg3),
    )


def fox_bwd(proj, k_sh, v_sh, clf, clf_t, qg3, j_layer, o, lse, lse_t, dmixed, dk_in, dv_in, dclf_in, comm=None):
    s = proj.shape[0]
    npair = 6
    tq = TQ if s % TQ == 0 else s
    nq = s // tq

    def body(q_ref, gate_ref, k_ref, v_ref, clf_ref, clft_ref, qg_ref, o_ref, lse_ref, lset_ref, dm_ref, dkin_ref, dvin_ref, dclfin_ref,
             dq_ref, dgate_ref, dk_ref, dv_ref, dclf_ref, dqg_ref, dqn_s, dcl_s):
        j = pl.program_id(0)
        lane = lax.broadcasted_iota(jnp.int32, (1, LANES), 1)
        m0 = lane < HEAD64
        qg = qg_ref[...]
        u, r = _rms64(q_ref[...], m0)
        qn = u * qg * FOX_SCALE
        ov = o_ref[...]
        gate = gate_ref[...]
        sg = _sigmoid(gate)
        dmain = dm_ref[...]
        do = dmain * sg
        dgate_ref[...] = dmain * ov * sg * (1.0 - sg)
        dk_ref[...] = dkin_ref[...]
        dv_ref[...] = dvin_ref[...]
        clfv = clf_ref[...]
        lsev = lse_ref[...]
        ones8 = jnp.ones((8, LANES), F32)

        @pl.when(j == 0)
        def _():
            dclf_ref[...] = dclfin_ref[...]
            dqg_ref[...] = jnp.zeros_like(dqg_ref)

        for hh in range(2):
            h = 2 * j + hh
            mh = m0 if hh == 0 else jnp.logical_not(m0)
            qh = bf(jnp.where(mh, qn, 0.0))
            doh = jnp.where(mh, do, 0.0)
            dohb = bf(doh)
            doo = doh * ov
            dcol = _lane_col(clfv, lane, h)
            drow = clft_ref[pl.ds(h, 1), :]
            lcol = _lane_col(lsev, lane, hh)
            lrow = lset_ref[pl.ds(h, 1), :]
            delta = jnp.sum(doo, axis=-1, keepdims=True)
            dcl_s[...] = jnp.zeros_like(dcl_s)
            for i in range(nq):
                rows = slice(i * tq, (i + 1) * tq)
                ext = (i + 1) * tq
                kk, vv = k_ref[0:ext, :], v_ref[0:ext, :]
                sc = dot_nt(qh[rows], kk) + dcol[rows] - drow[:, :ext]
                p = jnp.where(_causal(tq, ext, i), jnp.exp(sc - lcol[rows]), 0.0)
                ds = p * (dot_nt(dohb[rows], vv) - delta[rows])
                dqh = dot_nn(bf(ds), kk) * FOX_SCALE
                if hh == 0:
                    dqn_s[rows, :] = dqh
                else:
                    dqn_s[rows, :] = jnp.where(mh, dqh, dqn_s[rows, :])
                dcl_s[rows, :] += jnp.sum(ds, axis=-1, keepdims=True)
                sct = dot_nt(kk, qh[rows]) + drow[:, rows] - dcol[:ext]
                pt = jnp.where(_causal(tq, ext, i, transposed=True), jnp.exp(sct - lrow[:, rows]), 0.0)
                delta_row = dot_nt(ones8, doo[rows], HI)[0:1]
                dst = pt * (dot_nt(vv, dohb[rows]) - delta_row)
                dv_ref[0:ext, :] += dot_nn(bf(pt), dohb[rows])
                dk_ref[0:ext, :] += dot_nn(bf(dst), qh[rows])
                dcl_s[0:ext, :] -= jnp.sum(dst, axis=-1, keepdims=True)
            dclf_ref[...] += jnp.where(lane == h, dcl_s[...], 0.0)
        dqr, dqg = _rms64_bwd(dqn_s[...], u, r, qg, m0)
        dq_ref[...] = dqr
        dqg_ref[...] += dqg

    blk = lambda off: pl.BlockSpec((s, LANES), lambda j: (0, off + j))
    full = _const((s, LANES))
    return _carry(
        body,
        comm,
        name="fox_bwd",
        grid=(npair,),
        in_specs=[blk(0), blk(npair), blk(0), blk(0), full, _const((16, s)), pl.BlockSpec((None, 1, LANES), lambda j: (j_layer, 0, 0)),
                  blk(0), blk(0), _const((16, s)), blk(0), blk(0), blk(0), full],
        out_specs=[blk(0)] * 4 + [full, _const((1, LANES))],
        out_shape=[jax.ShapeDtypeStruct((s, npair * LANES), F32)] * 4
        + [jax.ShapeDtypeStruct((s, LANES), F32), jax.ShapeDtypeStruct((1, LANES), F32)],
        scratch_shapes=[pltpu.VMEM((s, LANES), F32), pltpu.VMEM((s, LANES), F32)],
        args=(proj, proj, k_sh, v_sh, clf, clf_t, qg3, o, lse, lse_t, dmixed, dk_in, dv_in, dclf_in),
    )


def loss_head(y, target):
    s, d = y.shape
    tm = _tok(s)

    def body(y_ref, t_ref, loss_ref, dy_ref):
        err = y_ref[...] - t_ref[...]
        dy_ref[...] = err * (1.0 / d)

        @pl.when(pl.program_id(0) == 0)
        def _():
            loss_ref[...] = jnp.zeros_like(loss_ref)

        part = jnp.sum(jnp.mean(err * err, axis=-1, keepdims=True), axis=0, keepdims=True)
        loss_ref[...] += 0.5 * part

    row = pl.BlockSpec((tm, d), lambda i: (i, 0))
    return pl.pallas_call(
        body,
        name="loss_head",
        grid=(s // tm,),
        in_specs=[row, row],
        out_specs=[_const((1, 1)), row],
        out_shape=[jax.ShapeDtypeStruct((1, 1), F32), jax.ShapeDtypeStruct((s, d), F32)],
        compiler_params=_cparams(("arbitrary",)),
    )(y, target)


def _row_tile(r, c, n_arrays):
    budget = VMEM_LIMIT_BYTES // 2
    padded_c = -(-c // LANES) * LANES
    for step in (16, 8):
        fits = [t for t in range(step, r + 1, step) if r % t == 0 and 2 * n_arrays * t * padded_c * 4 <= budget]
        if fits:
            return fits[-1]
    return r


def _as2d(a):
    return a.reshape(-1, a.shape[-1]) if a.ndim >= 2 else a.reshape(1, -1)


def adamw(ws, gss, ms, vs):
    nw = len(ws)
    shape = ws[0].shape
    w2, m2, v2 = ([_as2d(t) for t in lst] for lst in (ws, ms, vs))
    rows, c = w2[0].shape
    gss = [[g.reshape(-1, c) for g in gs] for gs in gss]
    sizes = [g.shape[0] for g in gss[0]]
    assert sum(sizes) == rows
    tr = _row_tile(min(sizes), c, 8 * nw)
    assert all(r % tr == 0 for r in sizes)
    c1 = 1.0 - ADAM_B1**ADAM_STEP
    c2 = 1.0 - ADAM_B2**ADAM_STEP
    outs = []
    first = 0
    for k, r in enumerate(sizes):
        n_prev = len(outs)

        def body(*refs, n_prev=n_prev):
            out_refs = refs[4 * nw + n_prev :]
            for i in range(nw):
                w_ref, g_ref, m_ref, v_ref = refs[4 * i : 4 * i + 4]
                go_ref, d_ref, nm_ref, nv_ref = out_refs[4 * i : 4 * i + 4]
                gv = g_ref[...]
                nm = ADAM_B1 * m_ref[...] + (1.0 - ADAM_B1) * gv
                nv = ADAM_B2 * v_ref[...] + (1.0 - ADAM_B2) * (gv * gv)
                go_ref[...] = gv
                nm_ref[...] = nm
                nv_ref[...] = nv
                d_ref[...] = -ADAM_LR * ((nm / c1) / (jnp.sqrt(nv / c2) + ADAM_EPS) + ADAM_WD * w_ref[...])

        spec = pl.BlockSpec((tr, c), lambda i, b0=first // tr: (b0 + i, 0))
        args = [t for i in range(nw) for t in (w2[i], gss[i][k], m2[i], v2[i])]
        outs = pl.pallas_call(
            body,
            name="adamw",
            grid=(r // tr,),
            in_specs=[spec, pl.BlockSpec((tr, c), lambda i: (i, 0)), spec, spec] * nw + [ANY] * n_prev,
            out_specs=[spec] * (4 * nw),
            out_shape=[jax.ShapeDtypeStruct((rows, c), F32)] * (4 * nw),
            input_output_aliases={4 * nw + j: j for j in range(n_prev)},
            compiler_params=_cparams(("arbitrary",)),
        )(*args, *outs)
        first += r
    return [tuple(t.reshape(shape) for t in outs[4 * i : 4 * i + 4]) for i in range(nw)]


def _by_shape(arrays, fn):
    groups = {}
    for i, t in enumerate(arrays):
        groups.setdefault((t.shape, str(t.dtype)), []).append(i)
    out = [None] * len(arrays)
    for idx in groups.values():
        for i, res in zip(idx, fn(idx)):
            out[i] = res
    return out


def pair_sum(gs, recvs, c_arr):
    n = len(gs)
    _, k, r, c = gs[0].shape
    tr = _row_tile(r, c, 3 * n)

    def body(c_ref, *refs):
        for i in range(n):
            refs[2 * n + i][...] = bf(refs[i][...] + refs[n + i][...].astype(F32))

    slab = pl.BlockSpec((None, tr, c), lambda kk, i, cr: (kk, i, 0))
    return pl.pallas_call(
        body,
        name="pair_sum",
        grid_spec=pltpu.PrefetchScalarGridSpec(
            num_scalar_prefetch=1,
            grid=(k, r // tr),
            in_specs=[pl.BlockSpec((None, None, tr, c), lambda kk, i, cr: (cr[0], kk, i, 0))] * n + [slab] * n,
            out_specs=[slab] * n,
        ),
        out_shape=[jax.ShapeDtypeStruct((k, r, c), BF16)] * n,
        compiler_params=_cparams(("arbitrary", "arbitrary")),
    )(c_arr, *gs, *recvs)


def chip_sum(ps, qs, sel):
    n = len(ps)
    _, r, c = ps[0].shape
    nq = qs[0].shape[0]
    tr = _row_tile(r, c, 4 * n)

    def body(sel_ref, *refs):
        for a in range(n):
            acc = refs[a][...].astype(F32)
            for i in range(nq):
                acc = acc + refs[n + a][i].astype(F32)
            refs[2 * n + a][...] = acc

    return pl.pallas_call(
        body,
        name="chip_sum",
        grid_spec=pltpu.PrefetchScalarGridSpec(
            num_scalar_prefetch=1,
            grid=(r // tr,),
            in_specs=[pl.BlockSpec((None, tr, c), lambda i, sr: (sr[0], i, 0))] * n + [pl.BlockSpec((nq, tr, c), lambda i, sr: (0, i, 0))] * n,
            out_specs=[pl.BlockSpec((None, tr, c), lambda i, sr: (sr[1], i, 0))] * n,
        ),
        out_shape=[jax.ShapeDtypeStruct((2, r, c), F32)] * n,
        compiler_params=_cparams(("arbitrary",)),
    )(sel, *ps, *qs)


def cast_into_slot(w4s, owns, sel, dtype):
    n = len(w4s)
    _, _, r, c = w4s[0].shape
    tr = _row_tile(r, c, 2 * n)

    def body(sel_ref, *refs):
        for i in range(n):
            refs[n + i][...] = refs[i][...].astype(dtype)

    return pl.pallas_call(
        body,
        name="cast_into_slot",
        grid_spec=pltpu.PrefetchScalarGridSpec(
            num_scalar_prefetch=1,
            grid=(2, r // tr),
            in_specs=[pl.BlockSpec((None, None, tr, c), lambda hf, i, sr, g=g: (g, hf, i, 0)) for g in owns],
            out_specs=[pl.BlockSpec((None, None, tr, c), lambda hf, i, sr: (sr[0], hf, i, 0))] * n,
        ),
        out_shape=[jax.ShapeDtypeStruct((N_CHIPS, 2, r, c), dtype)] * n,
        compiler_params=_cparams(("arbitrary", "arbitrary")),
    )(sel, *w4s)


def _place():
    x, y, c = lax.axis_index("x"), lax.axis_index("y"), lax.axis_index("c")
    chips = [(1 - x, y), (x, 1 - y), (1 - x, 1 - y)]
    return x, y, c, 2 * x + y, chips, [2 * cx + cy for cx, cy in chips]


def _rcopy(src, dst, send, recv, dev):
    return pltpu.make_async_remote_copy(src_ref=src, dst_ref=dst, send_sem=send, recv_sem=recv, device_id=dev, device_id_type=MESH)


class Gather:
    def __init__(self, bufs):
        n = len(bufs)
        self.n = n
        self.args = list(bufs)
        self.out_shape = [jax.ShapeDtypeStruct(t.shape, t.dtype) for t in bufs]
        self.aliases = {a: a for a in range(n)}
        self.scratch = [pltpu.SemaphoreType.DMA((n, 6)), pltpu.SemaphoreType.DMA((n, 6))]

    def _sends(self, outs, send, recv):
        x, y, c, me, chips, _ = _place()
        cps = []
        for a in range(self.n):
            mine = outs[a].at[me, c]
            cps += [_rcopy(mine, mine, send.at[a, j], recv.at[a, j], (*chips[j], c)) for j in range(3)]
        return cps

    def start(self, ins, outs, scr):
        for cp in self._sends(outs, *scr):
            cp.start()

    def finish(self, ins, outs, scr):
        send, recv = scr
        x, y, c, me, chips, cidx = _place()
        sib = (x, y, 1 - c)
        passed = []
        for a in range(self.n):
            for j in range(3):
                landed = outs[a].at[cidx[j], c]
                _rcopy(landed, landed, send.at[a, j], recv.at[a, j], (*chips[j], c)).wait_recv()
                fwd = _rcopy(landed, landed, send.at[a, 3 + j], recv.at[a, 3 + j], sib)
                fwd.start()
                passed.append(fwd)
        for a in range(self.n):
            for j in range(3):
                theirs = outs[a].at[cidx[j], 1 - c]
                _rcopy(theirs, theirs, send.at[a, 3 + j], recv.at[a, 3 + j], sib).wait_recv()
        for cp in self._sends(outs, send, recv) + passed:
            cp.wait_send()


class PairExchange:
    def __init__(self, gs):
        n = len(gs)
        self.n = n
        self.args = list(gs)
        self.out_shape = [jax.ShapeDtypeStruct(t.shape[1:], t.dtype) for t in gs]
        self.aliases = {}
        self.scratch = [pltpu.SemaphoreType.DMA((n,)), pltpu.SemaphoreType.DMA((n,))]

    def _copies(self, ins, outs, send, recv):
        x, y, c = lax.axis_index("x"), lax.axis_index("y"), lax.axis_index("c")
        return [_rcopy(ins[a].at[1 - c], outs[a], send.at[a], recv.at[a], (x, y, 1 - c)) for a in range(self.n)]

    def start(self, ins, outs, scr):
        for cp in self._copies(ins, outs, *scr):
            cp.start()

    def finish(self, ins, outs, scr):
        for cp in self._copies(ins, outs, *scr):
            cp.wait()


class ChipExchange:
    def __init__(self, ps):
        n = len(ps)
        self.n = n
        self.args = list(ps)
        self.out_shape = [jax.ShapeDtypeStruct((3,) + t.shape[1:], t.dtype) for t in ps]
        self.aliases = {}
        self.scratch = [pltpu.SemaphoreType.DMA((n, 3)), pltpu.SemaphoreType.DMA((n, 3))]

    def _sends(self, ins, outs, send, recv):
        x, y, c, me, chips, cidx = _place()
        return [
            _rcopy(ins[a].at[cidx[j]], outs[a].at[j], send.at[a, j], recv.at[a, j], (*chips[j], c))
            for a in range(self.n)
            for j in range(3)
        ]

    def start(self, ins, outs, scr):
        for cp in self._sends(ins, outs, *scr):
            cp.start()

    def finish(self, ins, outs, scr):
        send, recv = scr
        x, y, c, me, chips, _ = _place()
        for a in range(self.n):
            for j in range(3):
                landed = outs[a].at[j]
                _rcopy(landed, landed, send.at[a, j], recv.at[a, j], (*chips[j], c)).wait_recv()
        for cp in self._sends(ins, outs, send, recv):
            cp.wait_send()


class PairShare:
    def __init__(self, bufs):
        n = len(bufs)
        self.n = n
        self.args = list(bufs)
        self.out_shape = [jax.ShapeDtypeStruct(t.shape, t.dtype) for t in bufs]
        self.aliases = {a: a for a in range(n)}
        self.scratch = [pltpu.SemaphoreType.DMA((n,)), pltpu.SemaphoreType.DMA((n,))]

    def _sends(self, outs, send, recv):
        x, y, c = lax.axis_index("x"), lax.axis_index("y"), lax.axis_index("c")
        return [_rcopy(outs[a].at[c], outs[a].at[c], send.at[a], recv.at[a], (x, y, 1 - c)) for a in range(self.n)]

    def start(self, ins, outs, scr):
        for cp in self._sends(outs, *scr):
            cp.start()

    def finish(self, ins, outs, scr):
        send, recv = scr
        x, y, c = lax.axis_index("x"), lax.axis_index("y"), lax.axis_index("c")
        for a in range(self.n):
            theirs = outs[a].at[1 - c]
            _rcopy(theirs, theirs, send.at[a], recv.at[a], (x, y, 1 - c)).wait_recv()
        for cp in self._sends(outs, send, recv):
            cp.wait_send()


class Multi:
    def __init__(self, comms):
        self.comms = comms
        self.args, self.out_shape, self.scratch, self.aliases = [], [], [], {}
        self.spans = []
        for cm in comms:
            a0, o0, s0 = len(self.args), len(self.out_shape), len(self.scratch)
            self.aliases.update({a0 + i: o0 + o for i, o in cm.aliases.items()})
            self.args += cm.args
            self.out_shape += cm.out_shape
            self.scratch += cm.scratch
            self.spans.append((slice(a0, len(self.args)), slice(o0, len(self.out_shape)), slice(s0, len(self.scratch))))

    def start(self, ins, outs, scr):
        for cm, (sa, so, ss) in zip(self.comms, self.spans):
            cm.start(ins[sa], outs[so], scr[ss])

    def finish(self, ins, outs, scr):
        for cm, (sa, so, ss) in zip(self.comms, self.spans):
            cm.finish(ins[sa], outs[so], scr[ss])

    def split(self, res):
        return [list(res[so]) for _, so, _ in self.spans]


def run_comm(comm, name):
    na, no = len(comm.args), len(comm.out_shape)

    def body(*refs):
        ins, outs, scr = refs[:na], refs[na : na + no], refs[na + no :]
        comm.start(ins, outs, scr)
        comm.finish(ins, outs, scr)

    return pl.pallas_call(
        body,
        name=name,
        in_specs=[ANY] * na,
        out_specs=[ANY] * no,
        out_shape=comm.out_shape,
        input_output_aliases=comm.aliases,
        scratch_shapes=comm.scratch,
    )(*comm.args)


def _carry(body, comm, *, name, grid, in_specs, out_specs, out_shape, scratch_shapes, args):
    params = _cparams(("arbitrary",))
    if comm is None:
        res = pl.pallas_call(body, name=name, grid=grid, in_specs=in_specs, out_specs=out_specs, out_shape=out_shape,
                             scratch_shapes=scratch_shapes, compiler_params=params)(*args)
        return res, None
    ni, no, ns = len(in_specs), len(out_specs), len(scratch_shapes)
    ci, co = len(comm.args), len(comm.out_shape)

    def wrapped(*refs):
        ins, c_ins = refs[:ni], refs[ni : ni + ci]
        p = ni + ci
        outs, c_outs = refs[p : p + no], refs[p + no : p + no + co]
        p += no + co
        scr, c_scr = refs[p : p + ns], refs[p + ns :]

        @pl.when(pl.program_id(0) == 0)
        def _():
            comm.start(c_ins, c_outs, c_scr)

        body(*ins, *outs, *scr)

        @pl.when(pl.program_id(0) == grid[0] - 1)
        def _():
            comm.finish(c_ins, c_outs, c_scr)

    res = pl.pallas_call(
        wrapped,
        name=name + "_carry",
        grid=grid,
        in_specs=list(in_specs) + [ANY] * ci,
        out_specs=list(out_specs) + [ANY] * co,
        out_shape=list(out_shape) + list(comm.out_shape),
        input_output_aliases={ni + i: no + o for i, o in comm.aliases.items()},
        scratch_shapes=list(scratch_shapes) + list(comm.scratch),
        compiler_params=params,
    )(*args, *comm.args)
    return res[:no], res[no:]


def small_allreduce(buf):
    r = buf.shape[0]

    def body(b_ref, o_ref, slots, send, recv):
        x, y, c = lax.axis_index("x"), lax.axis_index("y"), lax.axis_index("c")
        me = 4 * x + 2 * y + c
        slots[me] = b_ref[...]
        cps = []
        peers = []
        for mask in range(1, N_DEV):
            fx, fy, fc = (mask >> 2) & 1, (mask >> 1) & 1, mask & 1
            px, py, pc = (1 - x if fx else x), (1 - y if fy else y), (1 - c if fc else c)
            peers.append(4 * px + 2 * py + pc)
            cps.append(_rcopy(b_ref, slots.at[me], send.at[mask - 1], recv.at[mask - 1], (px, py, pc)))
        for cp in cps:
            cp.start()
        for k, pid in enumerate(peers):
            landed = slots.at[pid]
            _rcopy(landed, landed, send.at[k], recv.at[k], (x, y, c)).wait_recv()
        for cp in cps:
            cp.wait_send()
        acc = slots[0]
        for i in range(1, N_DEV):
            acc = acc + slots[i]
        o_ref[...] = acc

    vm = pl.BlockSpec(memory_space=pltpu.VMEM)
    return pl.pallas_call(
        body,
        name="small_allreduce",
        in_specs=[vm],
        out_specs=vm,
        out_shape=jax.ShapeDtypeStruct(buf.shape, F32),
        scratch_shapes=[pltpu.VMEM((N_DEV, r, LANES), F32), pltpu.SemaphoreType.DMA((N_DEV - 1,)), pltpu.SemaphoreType.DMA((N_DEV - 1,))],
    )(buf)


WEIGHT_NAMES = ["ffn1_norm", "ffn1_w_gate", "ffn1_w_up", "ffn1_w_down", "mix_norm", "mem_norm", "w_mem_kv", "mem_q_gain",
                "mem_k_gain", "w_in_a", "hgrn_lb_logits", "hgrn_o_gain", "w_in_b", "fox_q_gain", "kv_norm", "w_kv", "fox_f_bias",
                "fox_k_gain", "w_out", "ffn2_norm", "ffn2_w_gate", "ffn2_w_up", "ffn2_w_down"]
SHARDED = ["ffn1_w_gate", "ffn1_w_up", "ffn1_w_down", "w_mem_kv", "w_in_a", "w_in_b", "w_kv", "w_out", "ffn2_w_gate", "ffn2_w_up", "ffn2_w_down"]
SMALL = [n for n in WEIGHT_NAMES if n not in SHARDED]
FFN1 = ["ffn1_w_gate", "ffn1_w_up", "ffn1_w_down"]
FFN2 = ["ffn2_w_gate", "ffn2_w_up", "ffn2_w_down"]
PER_LAYER = FFN1 + FFN2 + ["w_mem_kv", "w_out"]
TRANSPOSED = ["ffn1_w_gate", "ffn1_w_up", "ffn2_w_gate", "ffn2_w_up", "w_in_a", "w_in_b"]
N_LAYERS, N_A = 4, 2
KV_PAD = 13 * LANES


def _cols_from_chips(g):
    return jnp.moveaxis(g, 0, 2).reshape(g.shape[1], g.shape[2], N_CHIPS * g.shape[3])


def _rows_from_chips(g):
    return jnp.moveaxis(g, 0, 1).reshape(g.shape[1], N_CHIPS * g.shape[2], g.shape[3])


def _pair_tile(g):
    return jnp.tile(g, (1, 2)).reshape(g.shape[0], 1, LANES)


def _pair_fold(g):
    return g[:, :HEAD64] + g[:, HEAD64:]


def kernel(x, mem, ffn1_norm, ffn1_w_gate, ffn1_w_up, ffn1_w_down, mix_norm, mem_norm, w_mem_kv, mem_q_gain, mem_k_gain, w_in_a, hgrn_lb_logits, hgrn_o_gain, w_in_b, fox_q_gain, kv_norm, w_kv, fox_f_bias, fox_k_gain, w_out, ffn2_norm, ffn2_w_gate, ffn2_w_up, ffn2_w_down, loss_target, m_ffn1_norm, m_ffn1_w_gate, m_ffn1_w_up, m_ffn1_w_down, m_mix_norm, m_mem_norm, m_w_mem_kv, m_mem_q_gain, m_mem_k_gain, m_w_in_a, m_hgrn_lb_logits, m_hgrn_o_gain, m_w_in_b, m_fox_q_gain, m_kv_norm, m_w_kv, m_fox_f_bias, m_fox_k_gain, m_w_out, m_ffn2_norm, m_ffn2_w_gate, m_ffn2_w_up, m_ffn2_w_down, v_ffn1_norm, v_ffn1_w_gate, v_ffn1_w_up, v_ffn1_w_down, v_mix_norm, v_mem_norm, v_w_mem_kv, v_mem_q_gain, v_mem_k_gain, v_w_in_a, v_hgrn_lb_logits, v_hgrn_o_gain, v_w_in_b, v_fox_q_gain, v_kv_norm, v_w_kv, v_fox_f_bias, v_fox_k_gain, v_w_out, v_ffn2_norm, v_ffn2_w_gate, v_ffn2_w_up, v_ffn2_w_down):
    given = dict(locals())
    def oriented(n, t):
        return jnp.swapaxes(t, 1, 2) if n in TRANSPOSED else t

    w = {n: oriented(n, given[n]) for n in WEIGHT_NAMES}
    xs, mems, tgt = x[0], mem[0], loss_target[0]
    s, d = xs.shape
    my_chip = 2 * lax.axis_index("x") + lax.axis_index("y")
    sel = jnp.stack([my_chip, lax.axis_index("c")]).astype(jnp.int32)
    c_arr = sel[1:]

    def w_in_name(l):
        return "w_in_a" if l < N_A else "w_in_b"

    def halves_of(n):
        rows, cols = w[n].shape[-2:]
        return w[n].reshape(-1, 2, rows // 2, cols)

    def own_of(n, l):
        return 0 if w[n].ndim == 2 else (l - N_A if n == "w_in_b" else l)

    def view(buf, n):
        rows, cols = w[n].shape[-2:]
        return buf.reshape(N_CHIPS, rows, cols) if w[n].ndim == 2 else buf.reshape(N_CHIPS, 1, rows, cols)

    def mixer(l):
        return [(w_in_name(l), l), ("w_mem_kv", l), ("w_out", l)]

    first = [(n, 0) for n in PER_LAYER] + [("w_in_a", 0), ("w_kv", 0)]
    carried = {
        (0, "ffn1"): mixer(1), (0, "mix"): [(n, 1) for n in FFN1 + FFN2[:2]], (0, "ffn2"): [(FFN2[2], 1)],
        (1, "ffn1"): mixer(2), (1, "mix"): [(n, 2) for n in FFN1 + FFN2[:2]], (1, "ffn2"): [(FFN2[2], 2)],
        (2, "ffn1"): [(FFN1[0], 3)] + mixer(3)[1:], (2, "mix"): [(FFN1[1], 3), (FFN1[2], 3)], (2, "ffn2"): [(FFN2[0], 3)] + mixer(3)[:1],
        (3, "ffn1"): [(FFN2[1], 3)], (3, "mix"): [(FFN2[2], 3)],
    }
    bufs = {}
    every = first + [it for items in carried.values() for it in items]
    for layer in range(N_LAYERS):
        its = [it for it in every if it[1] == layer]
        srcs = [halves_of(n) for n, _ in its]
        cast = lambda idx: cast_into_slot([srcs[i] for i in idx], [own_of(*its[i]) for i in idx], sel, BF16)
        bufs.update(zip(its, _by_shape(srcs, cast)))
    lb_buf = cast_into_slot([hgrn_lb_logits.reshape(1, 2, 1, -1)], [0], sel, F32)[0]
    got0 = run_comm(Gather([bufs[it] for it in first] + [lb_buf]), "gather_layer0")
    got = {it: view(b, it[0]) for it, b in zip(first, got0[:-1])}
    w_kv_full = _cols_from_chips(got[("w_kv", 0)][:, None])
    w_kv_full = jnp.pad(w_kv_full, ((0, 0), (0, 0), (0, KV_PAD - w_kv_full.shape[-1])))
    logits3 = jnp.moveaxis(got0[-1].reshape(N_CHIPS, 2, -1), 0, 1).reshape(2, 1, -1)
    lb3 = lb_fwd(logits3)
    w_in, w_mkv, w_o = {}, {}, {}

    def gather_behind(key):
        items = carried.get(key)
        return None if items is None else Gather([bufs[it] for it in items])

    def landed(key, res):
        if res is not None:
            got.update({it: view(b, it[0]) for it, b in zip(carried[key], res)})

    norm3 = {n: w[n].reshape(N_LAYERS, 1, d) for n in ("ffn1_norm", "mix_norm", "mem_norm", "ffn2_norm")}
    kvn3 = kv_norm.reshape(1, 1, d)
    mqg3, mkg3 = _pair_tile(mem_q_gain), _pair_tile(mem_k_gain)
    og3 = hgrn_o_gain.reshape(N_A, 1, LANES)
    fqg3 = _pair_tile(fox_q_gain)
    fkg = jnp.tile(fox_k_gain, 2).reshape(1, LANES)
    fb = jnp.pad(fox_f_bias, (0, LANES - fox_f_bias.shape[0])).reshape(1, LANES)

    sv = [dict() for _ in range(N_LAYERS)]
    h = xs
    kv = None
    for l in range(N_LAYERS):
        t = sv[l]
        t["x0"] = h
        (h, t["a1"], t["b1"]), res = ffn_fwd(h, norm3["ffn1_norm"], l, *[got[(n, l)] for n in FFN1], 0, comm=gather_behind((l, "ffn1")))
        landed((l, "ffn1"), res)
        t["x1"] = h
        w_in[l] = _rows_from_chips(got[(w_in_name(l), l)])
        t["proj"] = proj_fwd(h, norm3["mix_norm"], l, w_in[l], 0, wt=True)
        if l < N_A:
            (main, t["o"]), res = hgrn_fwd(t["proj"], lb3, og3, l, comm=gather_behind((l, "mix")))
            t["qblk"] = 12
        else:
            (main, t["o"], t["lse"]), res = fox_fwd(t["proj"], kv["k"], kv["v"], kv["clf"], kv["clf_t"], fqg3, l - N_A, comm=gather_behind((l, "mix")))
            t["qblk"] = 6
        landed((l, "mix"), res)
        w_mkv[l], w_o[l] = _rows_from_chips(got[("w_mem_kv", l)]), _rows_from_chips(got[("w_out", l)])
        t["kvm"] = proj_fwd(mems, norm3["mem_norm"], l, w_mkv[l], 0)
        memo = memattn_fwd(t["proj"], t["qblk"], t["kvm"], mqg3, mkg3, l)
        t["mixed"] = jnp.concatenate([main, memo], axis=-1)
        h = mm_res(h, t["mixed"], w_o[l], 0)
        t["x2"] = h
        (h, t["a2"], t["b2"]), res = ffn_fwd(h, norm3["ffn2_norm"], l, *[got[(n, l)] for n in FFN2], 0, comm=gather_behind((l, "ffn2")))
        landed((l, "ffn2"), res)
        if l == N_A - 1:
            kv = {"x": h, "kvf": proj_fwd(h, kvn3, 0, w_kv_full, 0)}
            kv["k"], kv["v"], kv["clf"] = kvprep_fwd(kv["kvf"], fkg, fb)
            kv["clf_t"] = kv["clf"][:, :16].T

    loss_local, dx = loss_head(h, tgt)

    nc = N_CHIPS
    fc = ffn1_w_down.shape[1]
    gsplit = [dict() for _ in range(N_LAYERS)]

    def group_layout(l):
        lay = {n: b[0].reshape(2, nc, fc // 2, d) for n, b in gsplit[l].items()}
        lay16 = {n: b[1].reshape(2, nc, fc // 2, d) for n, b in gsplit[l].items()}
        for n, (g32, g16) in (("w_mem_kv", dw_mkv[l]), ("w_out", dw_o[l]), (w_in_name(l), dw_in[l])):
            lay[n], lay16[n] = g32, g16
        names = PER_LAYER + [w_in_name(l)]
        if l == N_A - 1:
            kv_cols = w_kv.shape[-1] * nc
            lay["w_kv"] = jnp.transpose(dw_kv[:, :kv_cols].reshape(2, d // 2, nc, kv_cols // nc), (0, 2, 1, 3))
            names = names + ["w_kv"]
        return names, [lay[n] for n in names], [lay16[n] if n in lay16 else bf(lay[n]) for n in names]

    def pair_sums(gl, recv):
        return _by_shape(gl, lambda idx: pair_sum([gl[i] for i in idx], [recv[i] for i in idx], c_arr))

    def chip_sums(ps, qs):
        return _by_shape(ps, lambda idx: chip_sum([ps[i] for i in idx], [qs[i] for i in idx], sel))

    n_ffn = len(FFN1) + len(FFN2)
    riding = {l: l + 1 for l in range(N_LAYERS - 1)}
    reduced = {}
    unshared = None
    dw_in, dw_o, dw_mkv = [None] * N_LAYERS, [None] * N_LAYERS, [None] * N_LAYERS
    sg = {n: [None] * N_LAYERS for n in ("ffn1_norm", "mix_norm", "mem_norm", "ffn2_norm", "mem_q_gain", "mem_k_gain")}
    sg["hgrn_o_gain"], sg["fox_q_gain"], dlb = [None] * N_A, [None] * (N_LAYERS - N_A), [None] * N_A
    dk_sh = jnp.zeros((s, KV_MAIN), F32)
    dv_sh = jnp.zeros((s, KV_MAIN), F32)
    dclf = jnp.zeros((s, LANES), F32)
    zero_mem = jnp.zeros(mems.shape, F32)
    dw_kv = None
    for l in reversed(range(N_LAYERS)):
        t = sv[l]
        if l == N_A - 1:
            dkvf, dfkg, dfb = kvprep_bwd(kv["kvf"], fkg, fb, dk_sh, dv_sh, dclf)
            dx, sg["kv_norm"], xn_kv, dpb = proj_bwd(kv["x"], kvn3, 0, [dkvf], w_kv_full, 0, dx)
            dw_kv = wgrad(xn_kv, dpb)
        ride = riding.get(l)
        comms = []
        if unshared is not None:
            comms.append(PairShare(unshared[2]))
        if ride is not None:
            names_r, gl_r, gl16_r = group_layout(ride)
            comms.append(PairExchange(gl16_r))
        comm = Multi(comms) if comms else None
        (dx, da, db, hm, xn, dyb, sg["ffn2_norm"][l]), res = ffn_bwd(t["x2"], norm3["ffn2_norm"], l, dx, t["a2"], t["b2"], *[got[(n, l)] for n in FFN2], 0, comm=comm)
        if comm is not None:
            res = comm.split(res)
            if unshared is not None:
                reduced[unshared[0]] = dict(zip(unshared[1], res.pop(0)))
                unshared = None
            if ride is not None:
                partial_r = pair_sums(gl_r, res.pop(0))

        def ffn_wgrads(which, da, db, hm, xn, dyb):
            for n, (a_, b_) in zip(which, ((da, xn), (db, xn), (hm, dyb))):
                gsplit[l][n] = wgrad(a_, b_, split=True)

        ffn_wgrads(FFN2, da, db, hm, xn, dyb)
        dmixed, dxb = mm_nt(dx, w_o[l], 0)
        dw_o[l] = wgrad(t["mixed"], dxb, chip_rows=w_out.shape[1])
        dqm, dkvm, dmq, dmk = memattn_bwd(t["proj"], t["qblk"], t["kvm"], mqg3, mkg3, l, dmixed)
        sg["mem_q_gain"][l], sg["mem_k_gain"][l] = _pair_fold(dmq), _pair_fold(dmk)
        _, sg["mem_norm"][l], memn, dkvmb = proj_bwd(mems, norm3["mem_norm"], l, [dkvm], w_mkv[l], 0, zero_mem)
        dw_mkv[l] = wgrad(memn, dkvmb, chip_rows=w_mem_kv.shape[1])
        comm = ChipExchange(partial_r[:n_ffn]) if ride is not None else None
        if l < N_A:
            (dzq, dzf, dvi, dzg, dlb[l], sg["hgrn_o_gain"][l]), res = hgrn_bwd(t["proj"], lb3, og3, l, t["o"], dmixed, comm=comm)
            parts, tmw = [dzq, dzf, dvi, dzg, dqm], 13 * LANES
        else:
            lse_t = t["lse"].reshape(s, 6, LANES)[:, :, :2].reshape(s, 12).T
            lse_t = jnp.pad(lse_t, ((0, 4), (0, 0)))
            (dq, dgate, dk_sh, dv_sh, dclf, dfq), res = fox_bwd(t["proj"], kv["k"], kv["v"], kv["clf"], kv["clf_t"], fqg3, l - N_A, t["o"], t["lse"], lse_t, dmixed, dk_sh, dv_sh, dclf, comm=comm)
            sg["fox_q_gain"][l - N_A] = _pair_fold(dfq)
            parts, tmw = [dq, dgate, dqm], 7 * LANES
        if ride is not None:
            landed_r = list(res)
        dx, sg["mix_norm"][l], hn, dpb = proj_bwd(t["x1"], norm3["mix_norm"], l, parts, w_in[l], 0, dx, wt=True)
        dw_in[l] = wgrad(dpb, hn, tn=d // 2, tm=tmw, chip_rows=tmw // 2)
        comm = ChipExchange(partial_r[n_ffn:]) if ride is not None else None
        (dx, da, db, hm, xn, dyb, sg["ffn1_norm"][l]), res = ffn_bwd(t["x0"], norm3["ffn1_norm"], l, dx, t["a1"], t["b1"], *[got[(n, l)] for n in FFN1], 0, comm=comm)
        if ride is not None:
            unshared = (ride, names_r, chip_sums(partial_r, landed_r + list(res)))
        ffn_wgrads(FFN1, da, db, hm, xn, dyb)

    names0, gl0, gl16_0 = group_layout(0)
    recv0 = run_comm(PairExchange(gl16_0), "pair_exchange")
    partial0 = pair_sums(gl0, recv0)
    landed0 = run_comm(ChipExchange(partial0), "chip_exchange")
    mine0 = chip_sums(partial0, list(landed0))
    both = run_comm(PairShare(unshared[2] + mine0), "pair_share")
    reduced[unshared[0]] = dict(zip(unshared[1], both[: len(unshared[1])]))
    reduced[0] = dict(zip(names0, both[len(unshared[1]) :]))
    gparts = {n: [reduced[l][n] for l in range(N_LAYERS)] for n in PER_LAYER}
    gparts["w_in_a"] = [reduced[l]["w_in_a"] for l in range(N_A)]
    gparts["w_in_b"] = [reduced[l]["w_in_b"] for l in range(N_A, N_LAYERS)]
    gparts["w_kv"] = [reduced[N_A - 1]["w_kv"]]

    dlogits = lb_bwd(logits3, dlb[1]).reshape(2, -1)
    small = {
        "ffn1_norm": jnp.concatenate(sg["ffn1_norm"]), "mix_norm": jnp.concatenate(sg["mix_norm"]),
        "mem_norm": jnp.concatenate(sg["mem_norm"]), "ffn2_norm": jnp.concatenate(sg["ffn2_norm"]),
        "mem_q_gain": jnp.concatenate(sg["mem_q_gain"]), "mem_k_gain": jnp.concatenate(sg["mem_k_gain"]),
        "hgrn_o_gain": jnp.concatenate(sg["hgrn_o_gain"]), "fox_q_gain": jnp.concatenate(sg["fox_q_gain"]),
        "kv_norm": sg["kv_norm"], "fox_f_bias": dfb[:, : fox_f_bias.shape[0]], "fox_k_gain": _pair_fold(dfkg),
        "hgrn_lb_logits": dlogits,
    }
    flat = [small[n].reshape(-1) for n in SMALL] + [loss_local.reshape(-1)]
    sizes = [f.shape[0] for f in flat]
    total = sum(sizes)
    padded = -(-total // (8 * LANES)) * (8 * LANES)
    packed = jnp.pad(jnp.concatenate(flat), (0, padded - total)).reshape(-1, LANES)
    summed = small_allreduce(packed).reshape(-1)
    off = 0
    for n, sz in zip(SMALL, sizes[:-1]):
        gparts[n] = [summed[off : off + sz].reshape(dlogits.shape if n == "hgrn_lb_logits" else w[n].shape)]
        off += sz
    loss = summed[off]
    lbw = hgrn_lb_logits.shape[1]
    gparts["hgrn_lb_logits"] = [lax.dynamic_slice_in_dim(gparts["hgrn_lb_logits"][0], my_chip * lbw, lbw, axis=1)]

    def update(idx):
        ns = [WEIGHT_NAMES[i] for i in idx]
        return adamw([w[n] for n in ns], [gparts[n] for n in ns], [oriented(n, given["m_" + n]) for n in ns],
                     [oriented(n, given["v_" + n]) for n in ns])

    grads, delta, new_m, new_v = {}, {}, {}, {}
    for n, res in zip(WEIGHT_NAMES, _by_shape([w[n] for n in WEIGHT_NAMES], update)):
        grads[n], delta[n], new_m[n], new_v[n] = (oriented(n, t) for t in res)
    return (loss, dx[None], *[grads[n] for n in WEIGHT_NAMES], *[delta[n] for n in WEIGHT_NAMES],
            *[new_m[n] for n in WEIGHT_NAMES], *[new_v[n] for n in WEIGHT_NAMES])
```
